```python
import math
import jax, jax.numpy as jnp
from jax import lax
import numpy as np

D_MODEL = 2048
BATCH = 2
SEQ = 4096
DEPTH = 1

HEAD_DIM = 128
NSA_HEADS = 8
NSA_KV_HEADS = 2
NSA_GROUP = NSA_HEADS // NSA_KV_HEADS
CMP_BLOCK = 32
CMP_STRIDE = 16
CMP_HIDDEN = 256
SEL_BLOCK = 64
SEL_TOPN = 16
WINDOW = 512
FORCED_SCORE = 1e4
MOBA_HEADS = 8
MOBA_BLOCK = 256
MOBA_TOPK = 3
N_HEADS_TOTAL = NSA_HEADS + MOBA_HEADS
REL_BUCKETS = 32
REL_MAX_DIST = 128
N_GROUPS = 8
EXPERTS_PER_GROUP = 8
N_EXPERTS = N_GROUPS * EXPERTS_PER_GROUP
EXPERT_FF = D_MODEL // 4
EXPERT_TOPK = 2
MOE_ROW_BLOCK = 128
Q_CHUNK = 64
BAND_BLOCK = 128
RMS_EPS = 1e-6
NSA_Q_COLS = NSA_HEADS * HEAD_DIM
NSA_KV_COLS = NSA_KV_HEADS * HEAD_DIM
NSA_GATE_COLS = 3 * NSA_HEADS
MOBA_COLS = MOBA_HEADS * HEAD_DIM
IN_SPLITS = (NSA_Q_COLS,) + (NSA_KV_COLS,) * 6 + (NSA_GATE_COLS, MOBA_COLS, MOBA_COLS, MOBA_COLS, D_MODEL, D_MODEL)
IN_COLS = sum(IN_SPLITS)

kernel_name = 'hybrid_nsa_moba_hier_moe_block'


def rmsnorm(x, g):
    xf = x.astype(jnp.float32)
    y = xf * lax.rsqrt(jnp.mean(xf * xf, axis=-1, keepdims=True) + RMS_EPS)
    return (y * g.astype(jnp.float32)).astype(x.dtype)


def rel_bucket(dist):
    n = jnp.maximum(jnp.asarray(dist, jnp.int32), 0)
    max_exact = REL_BUCKETS // 2
    nf = jnp.maximum(n, 1).astype(jnp.float32)
    large = max_exact + (jnp.log(nf / max_exact) / math.log(REL_MAX_DIST / max_exact) * (REL_BUCKETS - max_exact)).astype(jnp.int32)
    return jnp.where(n < max_exact, n, jnp.minimum(large, REL_BUCKETS - 1))


def masked_softmax(logits, mask):
    logits = jnp.where(mask, logits, -jnp.inf)
    m = jnp.max(logits, axis=-1, keepdims=True)
    m = jnp.where(jnp.isfinite(m), m, 0.0)
    p = jnp.exp(logits - m)
    d = jnp.sum(p, axis=-1, keepdims=True)
    return p / jnp.where(d > 0, d, 1.0)


def compress_blocks(kv, pe, w1, w2):
    B, S, G, dh = kv.shape
    n_c = (S - CMP_BLOCK) // CMP_STRIDE + 1
    idx = np.arange(n_c)[:, None] * CMP_STRIDE + np.arange(CMP_BLOCK)[None, :]
    blocks = kv[:, idx] + pe[None, None, :, None, :]
    blocks = blocks.transpose(0, 3, 1, 2, 4).reshape(B, G, n_c, CMP_BLOCK * dh)
    return jax.nn.gelu(blocks @ w1) @ w2


def nsa_mixer(q, k_cmp, v_cmp, k_slc, v_slc, k_win, v_win, gate_logits,
              pe_k, w1_k, w2_k, pe_v, w1_v, w2_v, rel_tab):
    B, S, H, dh = q.shape
    G = k_cmp.shape[2]
    hg = H // G
    scale = dh ** -0.5
    t = np.arange(S)
    qg = q.reshape(B, S, G, hg, dh).transpose(0, 2, 3, 1, 4)

    kc = compress_blocks(k_cmp, pe_k, w1_k, w2_k)
    vc = compress_blocks(v_cmp, pe_v, w1_v, w2_v)
    n_c = kc.shape[2]
    c_start = np.arange(n_c) * CMP_STRIDE
    dist_c = t[:, None] - (c_start + CMP_BLOCK - 1)[None, :]
    bias_c = rel_tab[rel_bucket(dist_c)].transpose(2, 0, 1).reshape(G, hg, S, n_c)
    logit_c = jnp.einsum('bgjsd,bgcd->bgjsc', qg, kc).astype(jnp.float32) * scale + bias_c
    p_c = masked_softmax(logit_c, dist_c >= 0)
    o_c = jnp.einsum('bgjsc,bgcd->bgjsd', p_c.astype(vc.dtype), vc)

    n_sel = S // SEL_BLOCK
    sb_start = np.arange(n_sel) * SEL_BLOCK
    overlap = (c_start[None, :] < sb_start[:, None] + SEL_BLOCK) & (c_start[None, :] + CMP_BLOCK > sb_start[:, None])
    p_sel = jnp.einsum('bgjsc,nc->bgsn', p_c, jnp.asarray(overlap, jnp.float32))
    j = np.arange(n_sel)[None, :]
    cur = (t // SEL_BLOCK)[:, None]
    valid = sb_start[None, :] <= t[:, None]
    forced = (j == 0) | (j == cur) | (j == cur - 1)
    score = jnp.where(forced, FORCED_SCORE, jnp.where(valid, p_sel, -jnp.inf))
    n_top = min(SEL_TOPN, n_sel)
    _, sel_idx = lax.top_k(score, n_top)

    ks_b = k_slc.transpose(0, 2, 1, 3).reshape(B, G, n_sel, SEL_BLOCK, dh)
    vs_b = v_slc.transpose(0, 2, 1, 3).reshape(B, G, n_sel, SEL_BLOCK, dh)
    tab_g = rel_tab.reshape(REL_BUCKETS, G, hg).transpose(1, 0, 2)
    b_ix = jnp.arange(B)[:, None, None, None]
    g_ix = jnp.arange(G)[None, :, None, None]
    n_keys = n_top * SEL_BLOCK

    def sel_chunk(ci):
        t0 = ci * Q_CHUNK
        qc = lax.dynamic_slice_in_dim(qg, t0, Q_CHUNK, axis=3)
        ic = lax.dynamic_slice_in_dim(sel_idx, t0, Q_CHUNK, axis=2)
        kk = ks_b[b_ix, g_ix, ic].reshape(B, G, Q_CHUNK, n_keys, dh)
        vv = vs_b[b_ix, g_ix, ic].reshape(B, G, Q_CHUNK, n_keys, dh)
        pos = (ic[..., None] * SEL_BLOCK + jnp.arange(SEL_BLOCK)).reshape(B, G, Q_CHUNK, n_keys)
        dist = (t0 + jnp.arange(Q_CHUNK))[:, None] - pos
        bias = jnp.moveaxis(tab_g[g_ix, rel_bucket(dist)], -1, 2)
        logit = jnp.einsum('bgjqd,bgqld->bgjql', qc, kk).astype(jnp.float32) * scale + bias
        p = masked_softmax(logit, (dist >= 0)[:, :, None])
        return jnp.einsum('bgjql,bgqld->bgjqd', p.astype(vv.dtype), vv)

    o_s = lax.map(sel_chunk, jnp.arange(S // Q_CHUNK))
    o_s = jnp.moveaxis(o_s, 0, 3).reshape(B, G, hg, S, dh)

    n_band = S // BAND_BLOCK
    span = BAND_BLOCK + WINDOW
    band_idx = np.arange(n_band)[:, None] * BAND_BLOCK + np.arange(span)[None, :]
    pad = ((0, 0), (0, 0), (WINDOW, 0), (0, 0))
    kw_b = jnp.pad(k_win.transpose(0, 2, 1, 3), pad)[:, :, band_idx]
    vw_b = jnp.pad(v_win.transpose(0, 2, 1, 3), pad)[:, :, band_idx]
    qb = qg.reshape(B, G, hg, n_band, BAND_BLOCK, dh)
    r = np.arange(BAND_BLOCK)[:, None]
    c = np.arange(span)[None, :]
    dist_w = r + WINDOW - c
    key_pos = np.arange(n_band)[:, None, None] * BAND_BLOCK + c[None] - WINDOW
    mask_w = (dist_w >= 0) & (dist_w < WINDOW) & (key_pos >= 0)
    bias_w = rel_tab[rel_bucket(dist_w)].transpose(2, 0, 1).reshape(G, hg, 1, BAND_BLOCK, span)
    logit_w = jnp.einsum('bgjnqd,bgnkd->bgjnqk', qb, kw_b).astype(jnp.float32) * scale + bias_w
    p_w = masked_softmax(logit_w, mask_w)
    o_w = jnp.einsum('bgjnqk,bgnkd->bgjnqd', p_w.astype(vw_b.dtype), vw_b).reshape(B, G, hg, S, dh)

    gates = jax.nn.sigmoid(gate_logits.astype(jnp.float32)).reshape(B, S, H, 3)

    def to_bshd(o):
        return o.transpose(0, 3, 1, 2, 4).reshape(B, S, H, dh)

    out = gates[..., 0:1] * to_bshd(o_c) + gates[..., 1:2] * to_bshd(o_s) + gates[..., 2:3] * to_bshd(o_w)
    return out.reshape(B, S, H * dh).astype(q.dtype)


def moba_mixer(q, k, v, rel_tab):
    B, S, H, dh = q.shape
    scale = dh ** -0.5
    q, k, v = (a.transpose(0, 2, 1, 3) for a in (q, k, v))
    n_blk = -(-S // MOBA_BLOCK)
    pad = ((0, 0), (0, 0), (0, n_blk * MOBA_BLOCK - S), (0, 0))
    kb = jnp.pad(k, pad).reshape(B, H, n_blk, MOBA_BLOCK, dh)
    vb = jnp.pad(v, pad).reshape(B, H, n_blk, MOBA_BLOCK, dh)
    k_mean = jnp.mean(kb, axis=3)
    t = np.arange(S)
    own = t // MOBA_BLOCK
    past = np.arange(n_blk)[None, :] < own[:, None]
    gate = jnp.einsum('bhsd,bhnd->bhsn', q, k_mean).astype(jnp.float32)
    gate = jnp.where(past, gate, -jnp.inf)
    n_top = max(1, min(MOBA_TOPK, n_blk - 1))
    _, top_idx = lax.top_k(gate, n_top)
    top_ok = top_idx < own[:, None]
    tab_h = rel_tab.T
    b_ix = jnp.arange(B)[:, None, None, None]
    h_ix = jnp.arange(H)[None, :, None, None]
    n_keys = n_top * MOBA_BLOCK

    def chunk(ci):
        t0 = ci * Q_CHUNK
        blk = t0 // MOBA_BLOCK
        tq = t0 + jnp.arange(Q_CHUNK)
        qc = lax.dynamic_slice_in_dim(q, t0, Q_CHUNK, axis=2)
        ic = lax.dynamic_slice_in_dim(top_idx, t0, Q_CHUNK, axis=2)
        okc = lax.dynamic_slice_in_dim(top_ok, t0, Q_CHUNK, axis=2)
        kk = kb[b_ix, h_ix, ic].reshape(B, H, Q_CHUNK, n_keys, dh)
        vv = vb[b_ix, h_ix, ic].reshape(B, H, Q_CHUNK, n_keys, dh)
        pos = (ic[..., None] * MOBA_BLOCK + jnp.arange(MOBA_BLOCK)).reshape(B, H, Q_CHUNK, n_keys)
        mask_sel = jnp.repeat(okc, MOBA_BLOCK, axis=-1)
        bias_sel = tab_h[h_ix, rel_bucket(tq[:, None] - pos)]
        k_own = lax.dynamic_index_in_dim(kb, blk, axis=2, keepdims=False)
        v_own = lax.dynamic_index_in_dim(vb, blk, axis=2, keepdims=False)
        dist_own = tq[:, None] - (blk * MOBA_BLOCK + jnp.arange(MOBA_BLOCK))[None, :]
        bias_own = rel_tab[rel_bucket(dist_own)].transpose(2, 0, 1)
        l_sel = jnp.einsum('bhqd,bhqld->bhql', qc, kk).astype(jnp.float32) * scale + bias_sel
        l_own = jnp.einsum('bhqd,bhkd->bhqk', qc, k_own).astype(jnp.float32) * scale + bias_own
        logit = jnp.concatenate([l_sel, l_own], axis=-1)
        mask = jnp.concatenate([mask_sel, jnp.broadcast_to(dist_own >= 0, (B, H, Q_CHUNK, MOBA_BLOCK))], axis=-1)
        p = masked_softmax(logit, mask).astype(v.dtype)
        return (jnp.einsum('bhql,bhqld->bhqd', p[..., :n_keys], vv)
                + jnp.einsum('bhqk,bhkd->bhqd', p[..., n_keys:], v_own))

    o = lax.map(chunk, jnp.arange(S // Q_CHUNK))
    o = jnp.moveaxis(o, 0, 2).reshape(B, H, S, dh)
    return o.transpose(0, 2, 1, 3).reshape(B, S, H * dh)


def hier_moe(h, w_group, b_group, w_router, b_router, w_gate, w_up, w_down):
    B, S, D = h.shape
    T = B * S
    ht = h.reshape(T, D)
    g_prob = jax.nn.softmax((ht @ w_group).astype(jnp.float32) + b_group, axis=-1)
    g_val, g_idx = lax.top_k(g_prob, 1)
    e_logit = jnp.einsum('td,gde->tge', ht, w_router).astype(jnp.float32) + b_router
    e_logit = jnp.take_along_axis(e_logit, g_idx[:, :, None], axis=1)[:, 0]
    e_val, e_idx = lax.top_k(jax.nn.softmax(e_logit, axis=-1), EXPERT_TOPK)
    weights = g_val * e_val / jnp.sum(e_val, axis=-1, keepdims=True)
    expert = g_idx * EXPERTS_PER_GROUP + e_idx

    n_assign = T * EXPERT_TOPK
    flat_e = expert.reshape(-1)
    flat_tok = jnp.repeat(jnp.arange(T), EXPERT_TOPK)
    flat_w = weights.reshape(-1)
    order = jnp.argsort(flat_e)
    s_e, s_tok, s_w = flat_e[order], flat_tok[order], flat_w[order]
    counts = jnp.bincount(flat_e, length=N_EXPERTS)
    padded = (counts + MOE_ROW_BLOCK - 1) // MOE_ROW_BLOCK * MOE_ROW_BLOCK
    pad_end = jnp.cumsum(padded)
    pad_start = pad_end - padded
    start = jnp.cumsum(counts) - counts
    dest = pad_start[s_e] + jnp.arange(n_assign) - start[s_e]
    n_blocks = -(-(n_assign + N_EXPERTS * (MOE_ROW_BLOCK - 1)) // MOE_ROW_BLOCK)
    n_rows = n_blocks * MOE_ROW_BLOCK
    row_tok = jnp.zeros((n_rows,), jnp.int32).at[dest].set(s_tok)
    row_w = jnp.zeros((n_rows,), jnp.float32).at[dest].set(s_w)
    block_e = jnp.minimum(jnp.searchsorted(pad_end, jnp.arange(n_blocks) * MOE_ROW_BLOCK, side='right'), N_EXPERTS - 1)

    def run_block(args):
        tok, e = args
        xr = ht[tok]
        return (jax.nn.silu(xr @ w_gate[e]) * (xr @ w_up[e])) @ w_down[e]

    y = lax.map(run_block, (row_tok.reshape(n_blocks, MOE_ROW_BLOCK), block_e)).reshape(n_rows, D)
    y = (y.astype(jnp.float32) * row_w[:, None]).astype(h.dtype)
    out = jnp.zeros((T, D), h.dtype).at[row_tok].add(y)
    return out.reshape(B, S, D)


def setup_inputs(seed: int = 0) -> dict:
    key = jax.random.key(seed)
    ks = jax.random.split(key, 24)

    def nrm(k, shape, scale):
        return jax.random.normal(k, shape, jnp.float32) * scale

    dh = HEAD_DIM
    return {
        'x': nrm(ks[0], (BATCH, SEQ, D_MODEL), 1.0),
        'rel_bias': nrm(ks[1], (REL_BUCKETS, N_HEADS_TOTAL), 0.2),
        'norm_mix': 1.0 + nrm(ks[2], (DEPTH, D_MODEL), 0.01),
        'w_in': nrm(ks[3], (DEPTH, D_MODEL, IN_COLS), D_MODEL ** -0.5),
        'cmp_pe_k': nrm(ks[4], (DEPTH, CMP_BLOCK, dh), 0.1),
        'cmp_w1_k': nrm(ks[5], (DEPTH, CMP_BLOCK * dh, CMP_HIDDEN), (CMP_BLOCK * dh) ** -0.5),
        'cmp_w2_k': nrm(ks[6], (DEPTH, CMP_HIDDEN, dh), CMP_HIDDEN ** -0.5),
        'cmp_pe_v': nrm(ks[7], (DEPTH, CMP_BLOCK, dh), 0.1),
        'cmp_w1_v': nrm(ks[8], (DEPTH, CMP_BLOCK * dh, CMP_HIDDEN), (CMP_BLOCK * dh) ** -0.5),
        'cmp_w2_v': nrm(ks[9], (DEPTH, CMP_HIDDEN, dh), CMP_HIDDEN ** -0.5),
        'w_up_nsa': nrm(ks[10], (DEPTH, NSA_Q_COLS, D_MODEL), NSA_Q_COLS ** -0.5),
        'w_up_moba': nrm(ks[11], (DEPTH, MOBA_COLS, D_MODEL), MOBA_COLS ** -0.5),
        'w_out': nrm(ks[12], (DEPTH, D_MODEL, D_MODEL), D_MODEL ** -0.5),
        'norm_ffn': 1.0 + nrm(ks[13], (DEPTH, D_MODEL), 0.01),
        'w_group': nrm(ks[14], (DEPTH, D_MODEL, N_GROUPS), D_MODEL ** -0.5),
        'b_group': nrm(ks[15], (DEPTH, N_GROUPS), 0.01),
        'w_router': nrm(ks[16], (DEPTH, N_GROUPS, D_MODEL, EXPERTS_PER_GROUP), D_MODEL ** -0.5),
        'b_router': nrm(ks[17], (DEPTH, N_GROUPS, EXPERTS_PER_GROUP), 0.01),
        'w_exp_gate': nrm(ks[18], (DEPTH, N_EXPERTS, D_MODEL, EXPERT_FF), D_MODEL ** -0.5),
        'w_exp_up': nrm(ks[19], (DEPTH, N_EXPERTS, D_MODEL, EXPERT_FF), D_MODEL ** -0.5),
        'w_exp_down': nrm(ks[20], (DEPTH, N_EXPERTS, EXPERT_FF, D_MODEL), EXPERT_FF ** -0.5),
        'final_norm': 1.0 + nrm(ks[21], (D_MODEL,), 0.01),
    }


def reference(x, rel_bias, norm_mix, w_in, cmp_pe_k, cmp_w1_k, cmp_w2_k, cmp_pe_v, cmp_w1_v, cmp_w2_v,
              w_up_nsa, w_up_moba, w_out, norm_ffn, w_group, b_group, w_router, b_router,
              w_exp_gate, w_exp_up, w_exp_down, final_norm):
    B, S, _ = x.shape
    split_at = [int(v) for v in np.cumsum(IN_SPLITS)[:-1]]
    tab_a = rel_bias[:, :NSA_HEADS]
    tab_b = rel_bias[:, NSA_HEADS:]

    def heads(a, n):
        return a.reshape(B, S, n, HEAD_DIM)

    for l in range(DEPTH):
        h = rmsnorm(x, norm_mix[l])
        (q_a, kc, vc, ksl, vsl, kw, vw, gate_a,
         q_b, k_b, v_b, gm_a, gm_b) = jnp.split(h @ w_in[l], split_at, axis=-1)
        o_a = nsa_mixer(heads(q_a, NSA_HEADS), heads(kc, NSA_KV_HEADS), heads(vc, NSA_KV_HEADS),
                        heads(ksl, NSA_KV_HEADS), heads(vsl, NSA_KV_HEADS),
                        heads(kw, NSA_KV_HEADS), heads(vw, NSA_KV_HEADS), gate_a,
                        cmp_pe_k[l], cmp_w1_k[l], cmp_w2_k[l], cmp_pe_v[l], cmp_w1_v[l], cmp_w2_v[l], tab_a)
        o_b = moba_mixer(heads(q_b, MOBA_HEADS), heads(k_b, MOBA_HEADS), heads(v_b, MOBA_HEADS), tab_b)
        merged = jax.nn.sigmoid(gm_a) * (o_a @ w_up_nsa[l]) + jax.nn.sigmoid(gm_b) * (o_b @ w_up_moba[l])
        x = x + merged @ w_out[l]
        x = x + hier_moe(rmsnorm(x, norm_ffn[l]), w_group[l], b_group[l], w_router[l], b_router[l],
                         w_exp_gate[l], w_exp_up[l], w_exp_down[l])
    return rmsnorm(x, final_norm)
```

```python
import functools
import math

import numpy as np
import jax
import jax.numpy as jnp
from jax import lax
from jax.experimental import pallas as pl
from jax.experimental.pallas import tpu as pltpu

F32 = jnp.float32
BF16 = jnp.bfloat16

HEAD_DIM = 128
NSA_HEADS = 8
NSA_KV_HEADS = 2
NSA_GROUP = NSA_HEADS // NSA_KV_HEADS
CMP_BLOCK = 32
CMP_STRIDE = 16
SEL_BLOCK = 64
SEL_TOPN = 16
WINDOW = 512
FORCED_SCORE = 1e4
MOBA_HEADS = 8
MOBA_BLOCK = 256
MOBA_TOPK = 3
REL_BUCKETS = 32
REL_MAX_DIST = 128
N_GROUPS = 8
EXPERTS_PER_GROUP = 8
N_EXPERTS = N_GROUPS * EXPERTS_PER_GROUP
EXPERT_TOPK = 2
RMS_EPS = 1e-6

LANES = 128
ATT_TILE = 512
MOE_ROWS = 256
MASK_BIG = 1e30
M_INIT = -3e38
VMEM_LIMIT = 48 * 1024 * 1024


def _cparams(sem, vmem=VMEM_LIMIT):
    return pltpu.CompilerParams(dimension_semantics=sem, vmem_limit_bytes=vmem)


def _rmsnorm_body(x_ref, g_ref, o_ref):
    x = x_ref[...]
    ms = jnp.mean(x * x, axis=-1, keepdims=True)
    o_ref[...] = (x * lax.rsqrt(ms + RMS_EPS) * g_ref[...]).astype(o_ref.dtype)


def _rmsnorm(x, g, out_dtype, tm=512):
    t, d = x.shape
    return pl.pallas_call(
        _rmsnorm_body,
        grid=(t // tm,),
        in_specs=[pl.BlockSpec((tm, d), lambda i: (i, 0)),
                  pl.BlockSpec((1, d), lambda i: (0, 0))],
        out_specs=pl.BlockSpec((tm, d), lambda i: (i, 0)),
        out_shape=jax.ShapeDtypeStruct((t, d), out_dtype),
        compiler_params=_cparams(("parallel",)),
        name="rmsnorm",
    )(x, g.reshape(1, d))


def _mm_body(*refs, has_res):
    if has_res:
        a_ref, w_ref, r_ref, o_ref = refs
    else:
        a_ref, w_ref, o_ref = refs
    acc = jnp.dot(a_ref[...], w_ref[...].astype(BF16), preferred_element_type=F32)
    if has_res:
        acc = acc + r_ref[...]
    o_ref[...] = acc.astype(o_ref.dtype)


def _matmul(a, w, col0, ncols, out_dtype, res=None, tm=1024, tn=512, name="matmul"):
    t, k = a.shape
    tn = min(tn, ncols)
    assert col0 % tn == 0 and ncols % tn == 0 and t % tm == 0
    off = col0 // tn
    in_specs = [pl.BlockSpec((tm, k), lambda i, j: (i, 0)),
                pl.BlockSpec((k, tn), lambda i, j: (0, j + off))]
    args = [a, w]
    if res is not None:
        in_specs.append(pl.BlockSpec((tm, tn), lambda i, j: (i, j)))
        args.append(res)
    return pl.pallas_call(
        functools.partial(_mm_body, has_res=res is not None),
        grid=(t // tm, ncols // tn),
        in_specs=in_specs,
        out_specs=pl.BlockSpec((tm, tn), lambda i, j: (i, j)),
        out_shape=jax.ShapeDtypeStruct((t, ncols), out_dtype),
        compiler_params=_cparams(("parallel", "parallel")),
        name=name,
    )(*args)


def _compress_body(u_ref, pe_ref, w1_ref, w2_ref, o_ref, *, nc):
    u = u_ref[...]
    w1 = w1_ref[...].astype(BF16)
    half = u.shape[1]
    a = jnp.dot(u, w1[:half], preferred_element_type=F32)
    b = jnp.dot(u, w1[half:], preferred_element_type=F32)
    peb = jnp.dot(pe_ref[...].astype(BF16), w1, preferred_element_type=F32)[0:1]
    pre = a + pltpu.roll(b, nc - 1, 0) + peb
    hid = jax.nn.gelu(pre)
    o_ref[...] = jnp.dot(hid.astype(BF16), w2_ref[...].astype(BF16),
                         preferred_element_type=F32).astype(o_ref.dtype)


def _compress(u, pe, w1, w2):
    bg, nc, kk = u.shape
    hid = w1.shape[1]
    dh = w2.shape[1]
    pe8 = jnp.broadcast_to(pe.reshape(1, -1), (16, pe.size))
    return pl.pallas_call(
        functools.partial(_compress_body, nc=nc),
        grid=(bg,),
        in_specs=[pl.BlockSpec((None, nc, kk), lambda i: (i, 0, 0)),
                  pl.BlockSpec((16, 2 * kk), lambda i: (0, 0)),
                  pl.BlockSpec((2 * kk, hid), lambda i: (0, 0)),
                  pl.BlockSpec((hid, dh), lambda i: (0, 0))],
        out_specs=pl.BlockSpec((None, nc, dh), lambda i: (i, 0, 0)),
        out_shape=jax.ShapeDtypeStruct((bg, nc, dh), BF16),
        compiler_params=_cparams(("parallel",)),
        name="nsa_compress",
    )(u, pe8, w1, w2)


def _split3(x):
    p1 = x.astype(BF16)
    r = x - p1.astype(F32)
    p2 = r.astype(BF16)
    p3 = (r - p2.astype(F32)).astype(BF16)
    return p1, p2, p3


def _rank_count(score, n_rows):
    n_iota = lax.broadcasted_iota(jnp.int32, score.shape, 0)
    cnt = jnp.zeros(score.shape, F32)
    for m in range(n_rows):
        row = score[m:m + 1, :]
        tie = jnp.where(n_iota > m, 1.0, 0.0)
        cnt = cnt + jnp.where(row > score, 1.0, jnp.where(row == score, tie, 0.0))
    return cnt


def _nsa_cmp_body(q_ref, kc_ref, vc_ref, bias_ref, ov_ref, oc_ref, mem_ref, *, tq, nc, n_sel, scale):
    t0 = pl.program_id(2) * tq
    kc = kc_ref[...]
    vc = vc_ref[...]
    t_idx = t0 + lax.broadcasted_iota(jnp.int32, (tq, nc), 0)
    c_idx = lax.broadcasted_iota(jnp.int32, (tq, nc), 1)
    valid = t_idx - (c_idx * CMP_STRIDE + (CMP_BLOCK - 1)) >= 0
    psum = jnp.zeros((tq, nc), F32)
    for j in range(NSA_GROUP):
        hs = slice(j * HEAD_DIM, (j + 1) * HEAD_DIM)
        s = lax.dot_general(q_ref[:, hs], kc, (((1,), (1,)), ((), ())),
                            preferred_element_type=F32) * scale + bias_ref[j]
        s = jnp.where(valid, s, -MASK_BIG)
        m = jnp.max(s, axis=-1, keepdims=True)
        m = jnp.where(m > -0.5 * MASK_BIG, m, 0.0)
        p = jnp.where(valid, jnp.exp(s - m), 0.0)
        d = jnp.sum(p, axis=-1, keepdims=True)
        p = p / jnp.where(d > 0, d, 1.0)
        oc_ref[:, hs] = jnp.dot(p.astype(BF16), vc, preferred_element_type=F32).astype(oc_ref.dtype)
        psum = psum + p
    ov = ov_ref[...]
    nt = (((1,), (1,)), ((), ()))
    p1, p2, p3 = _split3(psum)
    psel = (lax.dot_general(ov, p1, nt, preferred_element_type=F32)
            + lax.dot_general(ov, p2, nt, preferred_element_type=F32)
            + lax.dot_general(ov, p3, nt, preferred_element_type=F32))
    n_idx = lax.broadcasted_iota(jnp.int32, (n_sel, tq), 0)
    tt = t0 + lax.broadcasted_iota(jnp.int32, (n_sel, tq), 1)
    cur = tt // SEL_BLOCK
    forced = jnp.where(n_idx == 0, 1.0, jnp.where(n_idx == cur, 1.0, jnp.where(n_idx == cur - 1, 1.0, 0.0)))
    score = jnp.where(forced > 0.5, FORCED_SCORE, jnp.where(n_idx * SEL_BLOCK <= tt, psel, -1.0))
    cnt = _rank_count(score, n_sel)
    member = jnp.where(cnt < float(min(SEL_TOPN, n_sel)), 1.0, 0.0)
    if n_sel < LANES:
        member = jnp.concatenate([member, jnp.zeros((LANES - n_sel, tq), F32)], axis=0)
    mem_ref[...] = member.T.astype(mem_ref.dtype)


def _nsa_cmp(proj, kc, vc, bias_c, overlap, b, s, tq=256):
    g = NSA_KV_HEADS
    nc = kc.shape[2]
    n_sel = s // SEL_BLOCK
    gw = NSA_GROUP * HEAD_DIM
    body = functools.partial(_nsa_cmp_body, tq=tq, nc=nc, n_sel=n_sel, scale=HEAD_DIM ** -0.5)
    return pl.pallas_call(
        body,
        grid=(b, g, s // tq),
        in_specs=[pl.BlockSpec((None, tq, gw), lambda bi, gi, i: (bi, i, gi)),
                  pl.BlockSpec((None, None, nc, HEAD_DIM), lambda bi, gi, i: (bi, gi, 0, 0)),
                  pl.BlockSpec((None, None, nc, HEAD_DIM), lambda bi, gi, i: (bi, gi, 0, 0)),
                  pl.BlockSpec((NSA_GROUP, tq, nc), lambda bi, gi, i: (gi, i, 0)),
                  pl.BlockSpec((n_sel, nc), lambda bi, gi, i: (0, 0))],
        out_specs=[pl.BlockSpec((None, tq, gw), lambda bi, gi, i: (bi, i, gi)),
                   pl.BlockSpec((None, None, tq, LANES), lambda bi, gi, i: (bi, gi, i, 0))],
        out_shape=[jax.ShapeDtypeStruct((b, s, NSA_HEADS * HEAD_DIM), BF16),
                   jax.ShapeDtypeStruct((b, g, s, LANES), BF16)],
        compiler_params=_cparams(("parallel", "parallel", "parallel")),
        name="nsa_cmp_select",
    )(proj, kc, vc, bias_c, overlap)


def _moba_gate_body(q_ref, k_ref, mem_ref, *, s, nblk):
    k = k_ref[...].astype(F32)
    kmean = jnp.mean(k.reshape(nblk, MOBA_BLOCK, HEAD_DIM), axis=1)
    k1 = kmean.astype(BF16)
    k2 = (kmean - k1.astype(F32)).astype(BF16)
    q = q_ref[...]
    nt = (((1,), (1,)), ((), ()))
    gate = (lax.dot_general(k1, q, nt, preferred_element_type=F32)
            + lax.dot_general(k2, q, nt, preferred_element_type=F32))
    n_idx = lax.broadcasted_iota(jnp.int32, (nblk, s), 0)
    own = lax.broadcasted_iota(jnp.int32, (nblk, s), 1) // MOBA_BLOCK
    past = n_idx < own
    score = jnp.where(past, gate, -MASK_BIG)
    cnt = _rank_count(score, nblk)
    n_top = max(1, min(MOBA_TOPK, nblk - 1))
    sel = jnp.where(past, jnp.where(cnt < float(n_top), 1.0, 0.0), 0.0)
    member = jnp.where(n_idx == own, 1.0, sel)
    member = jnp.concatenate([member, jnp.zeros((LANES - nblk, s), F32)], axis=0)
    mem_ref[...] = member.T.astype(mem_ref.dtype)


def _moba_gate(proj, b, s):
    h = MOBA_HEADS
    nblk = s // MOBA_BLOCK
    return pl.pallas_call(
        functools.partial(_moba_gate_body, s=s, nblk=nblk),
        grid=(b, h),
        in_specs=[pl.BlockSpec((None, s, HEAD_DIM), lambda bi, hi: (bi, 0, hi)),
                  pl.BlockSpec((None, s, HEAD_DIM), lambda bi, hi: (bi, 0, h + hi))],
        out_specs=pl.BlockSpec((None, None, s, LANES), lambda bi, hi: (bi, hi, 0, 0)),
        out_shape=jax.ShapeDtypeStruct((b, h, s, LANES), BF16),
        compiler_params=_cparams(("parallel", "parallel")),
        name="moba_gate",
    )(proj, proj)


def _flash_body(qi_ref, ki_ref, bo_ref, fl_ref, *refs, nh, ratio, nm, scale):
    if nm:
        q_ref, k_ref, v_ref, bias_ref, mem_ref, et_ref, o_ref, m_ref, l_ref, acc_ref = refs
    else:
        q_ref, k_ref, v_ref, bias_ref, o_ref, m_ref, l_ref, acc_ref = refs
    del qi_ref, ki_ref
    p = pl.program_id(2)
    flag = fl_ref[p]
    bo = bo_ref[p]
    t = q_ref.shape[0]

    @pl.when((flag & 1) != 0)
    def _():
        m_ref[...] = jnp.full(m_ref.shape, M_INIT, F32)
        l_ref[...] = jnp.zeros(l_ref.shape, F32)
        acc_ref[...] = jnp.zeros(acc_ref.shape, F32)

    nt = (((1,), (1,)), ((), ()))
    reps = t // LANES
    for h in range(nh):
        hs = slice(h * HEAD_DIM, (h + 1) * HEAD_DIM)
        kv = h // ratio
        ks = slice(kv * HEAD_DIM, (kv + 1) * HEAD_DIM)
        q = q_ref[:, hs]
        k = k_ref[:, ks]
        if nm:
            mneg = mem_ref[h // (nh // nm)] - 1.0
            q = jnp.concatenate([q, mneg.astype(BF16)], axis=1)
            k = jnp.concatenate([k, et_ref[...]], axis=1)
        sc = lax.dot_general(q, k, nt, preferred_element_type=F32) * scale + bias_ref[h, bo]
        m_prev = m_ref[h]
        m_new = jnp.maximum(m_prev, jnp.max(sc, axis=-1, keepdims=True))
        alpha = jnp.exp(m_prev - m_new)
        pm = jnp.exp(sc - jnp.tile(m_new, (1, reps)))
        l_ref[h] = alpha * l_ref[h] + jnp.sum(pm, axis=-1, keepdims=True)
        acc_ref[h] = alpha * acc_ref[h] + jnp.dot(pm.astype(BF16), v_ref[:, ks], preferred_element_type=F32)
        m_ref[h] = m_new

    @pl.when((flag & 2) != 0)
    def _():
        for h in range(nh):
            l = l_ref[h]
            o_ref[:, h * HEAD_DIM:(h + 1) * HEAD_DIM] = (acc_ref[h] / jnp.where(l > 0, l, 1.0)).astype(o_ref.dtype)


def _flash(q_arr, q_off, k_arr, k_off, v_arr, v_off, bias, n_heads, ratio, nh, pairs, member=None, et=None,
           name="flash"):
    b, s, _ = q_arr.shape
    t = ATT_TILE
    nkv = nh // ratio
    ng = n_heads // nh
    qi = jnp.asarray([p[0] for p in pairs], jnp.int32)
    ki = jnp.asarray([p[1] for p in pairs], jnp.int32)
    bo = jnp.asarray([p[2] for p in pairs], jnp.int32)
    fl = jnp.asarray([p[3] for p in pairs], jnp.int32)
    nb = bias.shape[1]
    nm = 0
    in_specs = [
        pl.BlockSpec((None, t, nh * HEAD_DIM), lambda bi, gi, p, qi, ki, bo, fl: (bi, qi[p], q_off + gi)),
        pl.BlockSpec((None, t, nkv * HEAD_DIM), lambda bi, gi, p, qi, ki, bo, fl: (bi, ki[p], k_off + gi)),
        pl.BlockSpec((None, t, nkv * HEAD_DIM), lambda bi, gi, p, qi, ki, bo, fl: (bi, ki[p], v_off + gi)),
        pl.BlockSpec((nh, nb, t, t), lambda bi, gi, p, qi, ki, bo, fl: (gi, 0, 0, 0)),
    ]
    args = [q_arr, k_arr, v_arr, bias]
    if member is not None:
        nm = member.shape[1] // ng
        in_specs += [
            pl.BlockSpec((None, nm, t, LANES), lambda bi, gi, p, qi, ki, bo, fl: (bi, gi, qi[p], 0)),
            pl.BlockSpec((t, LANES), lambda bi, gi, p, qi, ki, bo, fl: (ki[p], 0)),
        ]
        args += [member, et]
    body = functools.partial(_flash_body, nh=nh, ratio=ratio, nm=nm, scale=HEAD_DIM ** -0.5)
    return pl.pallas_call(
        body,
        grid_spec=pltpu.PrefetchScalarGridSpec(
            num_scalar_prefetch=4,
            grid=(b, ng, len(pairs)),
            in_specs=in_specs,
            out_specs=pl.BlockSpec((None, t, nh * HEAD_DIM), lambda bi, gi, p, qi, ki, bo, fl: (bi, qi[p], gi)),
            scratch_shapes=[pltpu.VMEM((nh, t, LANES), F32)] * 3,
        ),
        out_shape=jax.ShapeDtypeStruct((b, s, n_heads * HEAD_DIM), BF16),
        compiler_params=_cparams(("parallel", "parallel", "arbitrary")),
        name=name,
    )(qi, ki, bo, fl, *args)


def _rel_bucket(dist):
    n = jnp.maximum(jnp.asarray(dist, jnp.int32), 0)
    max_exact = REL_BUCKETS // 2
    nf = jnp.maximum(n, 1).astype(jnp.float32)
    large = max_exact + (jnp.log(nf / max_exact) / math.log(REL_MAX_DIST / max_exact)
                         * (REL_BUCKETS - max_exact)).astype(jnp.int32)
    return jnp.where(n < max_exact, n, jnp.minimum(large, REL_BUCKETS - 1))


def _n_near(t):
    return -(-(REL_MAX_DIST - 1 + t) // t)


def _bias_tiles(tab, t, n_off, window=None):
    r = np.arange(t)[:, None]
    c = np.arange(t)[None, :]
    dist = np.stack([off * t + r - c for off in range(n_off)])
    ok = dist >= 0
    if window is not None:
        ok &= dist < window
    bias = tab[_rel_bucket(dist)]
    bias = jnp.where(jnp.asarray(ok)[..., None], bias, -MASK_BIG)
    return jnp.transpose(bias, (3, 0, 1, 2)).astype(F32)


def _causal_pairs(nq, n_near):
    pairs = []
    for qi in range(nq):
        for ki in range(qi + 1):
            pairs.append((qi, ki, min(qi - ki, n_near), (1 if ki == 0 else 0) | (2 if ki == qi else 0)))
    return pairs


def _window_pairs(nq, n_back):
    pairs = []
    for qi in range(nq):
        lo = max(0, qi - n_back)
        for ki in range(lo, qi + 1):
            pairs.append((qi, ki, qi - ki, (1 if ki == lo else 0) | (2 if ki == qi else 0)))
    return pairs


def _block_onehot(s, blk):
    return jnp.asarray(np.where(np.arange(s)[:, None] // blk == np.arange(LANES)[None, :], MASK_BIG, 0.0), BF16)


def _merge_body(oc_ref, os_ref, ow_ref, gl_ref, ob_ref, gma_ref, gmb_ref, wa_ref, wb_ref, o_ref, oa_ref):
    @pl.when(pl.program_id(1) == 0)
    def _():
        gates = jax.nn.sigmoid(gl_ref[...])
        for h in range(NSA_HEADS):
            hs = slice(h * HEAD_DIM, (h + 1) * HEAD_DIM)
            mix = (gates[:, 3 * h:3 * h + 1] * oc_ref[:, hs].astype(F32)
                   + gates[:, 3 * h + 1:3 * h + 2] * os_ref[:, hs].astype(F32)
                   + gates[:, 3 * h + 2:3 * h + 3] * ow_ref[:, hs].astype(F32))
            oa_ref[:, hs] = mix.astype(BF16)

    ya = jnp.dot(oa_ref[...], wa_ref[...].astype(BF16), preferred_element_type=F32)
    yb = jnp.dot(ob_ref[...], wb_ref[...].astype(BF16), preferred_element_type=F32)
    o_ref[...] = (jax.nn.sigmoid(gma_ref[...]) * ya + jax.nn.sigmoid(gmb_ref[...]) * yb).astype(o_ref.dtype)


def _merge(o_c, o_s, o_w, gate_logits, o_b, gm, w_up_a, w_up_b, tm=512, tn=512):
    t, ka = o_c.shape
    kb = o_b.shape[1]
    d = w_up_a.shape[1]
    nj = d // tn
    row = lambda i, j: (i, 0)
    return pl.pallas_call(
        _merge_body,
        grid=(t // tm, nj),
        in_specs=[pl.BlockSpec((tm, ka), row), pl.BlockSpec((tm, ka), row), pl.BlockSpec((tm, ka), row),
                  pl.BlockSpec((tm, LANES), row), pl.BlockSpec((tm, kb), row),
                  pl.BlockSpec((tm, tn), lambda i, j: (i, j)),
                  pl.BlockSpec((tm, tn), lambda i, j: (i, j + nj)),
                  pl.BlockSpec((ka, tn), lambda i, j: (0, j)),
                  pl.BlockSpec((kb, tn), lambda i, j: (0, j))],
        out_specs=pl.BlockSpec((tm, tn), lambda i, j: (i, j)),
        out_shape=jax.ShapeDtypeStruct((t, d), BF16),
        scratch_shapes=[pltpu.VMEM((tm, ka), BF16)],
        compiler_params=_cparams(("parallel", "arbitrary")),
        name="merge_up",
    )(o_c, o_s, o_w, gate_logits, o_b, gm, gm, w_up_a, w_up_b)


def _route_body(x_ref, g_ref, w_ref, b_ref, h_ref, info_ref, cnt_ref, carry_ref, *, tm):
    @pl.when(pl.program_id(0) == 0)
    def _():
        carry_ref[...] = jnp.zeros(carry_ref.shape, F32)

    x = x_ref[...]
    ms = jnp.mean(x * x, axis=-1, keepdims=True)
    h = x * lax.rsqrt(ms + RMS_EPS) * g_ref[...]
    h_ref[...] = h
    w = w_ref[...]
    h1 = h.astype(BF16)
    h2 = (h - h1.astype(F32)).astype(BF16)
    w1 = w.astype(BF16)
    w2 = (w - w1.astype(F32)).astype(BF16)
    logits = (jnp.dot(h1, w1, preferred_element_type=F32) + jnp.dot(h1, w2, preferred_element_type=F32)
              + jnp.dot(h2, w1, preferred_element_type=F32)) + b_ref[...]
    lane = lax.broadcasted_iota(jnp.int32, (tm, LANES), 1)
    lanef = lane.astype(F32)

    is_g = lane < N_GROUPS
    gl = jnp.where(is_g, logits, -MASK_BIG)
    ge = jnp.where(is_g, jnp.exp(gl - jnp.max(gl, axis=-1, keepdims=True)), 0.0)
    gp = ge / jnp.sum(ge, axis=-1, keepdims=True)
    g_val = jnp.max(gp, axis=-1, keepdims=True)
    g_idx = jnp.min(jnp.where(gp == g_val, lanef, float(LANES)), axis=-1, keepdims=True)

    lane_grp = ((lane - N_GROUPS) // EXPERTS_PER_GROUP).astype(F32)
    in_e = jnp.where(lane >= N_GROUPS, jnp.where(lane < N_GROUPS + N_EXPERTS, 1.0, 0.0), 0.0)
    is_e = jnp.where(lane_grp == g_idx, in_e, 0.0) > 0.5
    el = jnp.where(is_e, logits, -MASK_BIG)
    ee = jnp.where(is_e, jnp.exp(el - jnp.max(el, axis=-1, keepdims=True)), 0.0)
    ep = jnp.where(is_e, ee / jnp.sum(ee, axis=-1, keepdims=True), -1.0)
    v1 = jnp.max(ep, axis=-1, keepdims=True)
    l1 = jnp.min(jnp.where(ep == v1, lanef, float(LANES)), axis=-1, keepdims=True)
    ep2 = jnp.where(lanef == l1, -1.0, ep)
    v2 = jnp.max(ep2, axis=-1, keepdims=True)
    l2 = jnp.min(jnp.where(ep2 == v2, lanef, float(LANES)), axis=-1, keepdims=True)
    vs = v1 + v2
    wt1 = g_val * v1 / vs
    wt2 = g_val * v2 / vs
    e1 = l1 - float(N_GROUPS)
    e2 = l2 - float(N_GROUPS)

    oh = jnp.where(lanef == e1, 1.0, jnp.where(lanef == e2, 1.0, 0.0))
    r_i = lax.broadcasted_iota(jnp.int32, (tm, tm), 0)
    c_i = lax.broadcasted_iota(jnp.int32, (tm, tm), 1)
    tri = jnp.where(r_i > c_i, 1.0, 0.0).astype(BF16)
    base = jnp.dot(tri, oh.astype(BF16), preferred_element_type=F32) + carry_ref[...]
    r1 = jnp.sum(jnp.where(lanef == e1, base, 0.0), axis=-1, keepdims=True)
    r2 = jnp.sum(jnp.where(lanef == e2, base, 0.0), axis=-1, keepdims=True)
    carry_ref[...] = carry_ref[...] + jnp.sum(oh, axis=0, keepdims=True)
    cnt_ref[...] = jnp.broadcast_to(carry_ref[...], cnt_ref.shape)
    info = jnp.where(lane == 0, e1, jnp.where(lane == 1, e2, jnp.where(lane == 2, wt1, jnp.where(
        lane == 3, wt2, jnp.where(lane == 4, r1, jnp.where(lane == 5, r2, 0.0))))))
    info_ref[...] = info


def _route(x1, g, w_gr, b_gr, tm=512):
    t, d = x1.shape
    return pl.pallas_call(
        functools.partial(_route_body, tm=tm),
        grid=(t // tm,),
        in_specs=[pl.BlockSpec((tm, d), lambda i: (i, 0)),
                  pl.BlockSpec((1, d), lambda i: (0, 0)),
                  pl.BlockSpec((d, LANES), lambda i: (0, 0)),
                  pl.BlockSpec((1, LANES), lambda i: (0, 0))],
        out_specs=[pl.BlockSpec((tm, d), lambda i: (i, 0)),
                   pl.BlockSpec((tm, LANES), lambda i: (i, 0)),
                   pl.BlockSpec((8, LANES), lambda i: (0, 0))],
        out_shape=[jax.ShapeDtypeStruct((t, d), F32),
                   jax.ShapeDtypeStruct((t, LANES), F32),
                   jax.ShapeDtypeStruct((8, LANES), F32)],
        scratch_shapes=[pltpu.VMEM((1, LANES), F32)],
        compiler_params=_cparams(("arbitrary",)),
        name="moe_route",
    )(x1, g.reshape(1, d), w_gr, b_gr)


def _row_copy(src_ref, src_row, dst_ref, dst_row, sem):
    return pltpu.make_async_copy(src_ref.at[pl.ds(src_row, 1)], dst_ref.at[pl.ds(dst_row, 1)], sem)


def _dispatch_body(dest_ref, h_ref, xs_in_ref, xs_ref, sem, *, tm):
    del xs_in_ref
    base = pl.program_id(0) * tm

    def issue(r, c):
        for k in range(EXPERT_TOPK):
            _row_copy(h_ref, r, xs_ref, dest_ref[EXPERT_TOPK * (base + r) + k], sem).start()
        return c

    lax.fori_loop(0, tm, issue, 0)

    def drain(r, c):
        for k in range(EXPERT_TOPK):
            _row_copy(h_ref, r, xs_ref, dest_ref[EXPERT_TOPK * (base + r) + k], sem).wait()
        return c

    lax.fori_loop(0, tm, drain, 0)


def _dispatch(dest, h, n_rows, tm=512):
    t, d = h.shape
    xs0 = jnp.zeros((n_rows, d), h.dtype)
    return pl.pallas_call(
        functools.partial(_dispatch_body, tm=tm),
        grid_spec=pltpu.PrefetchScalarGridSpec(
            num_scalar_prefetch=1,
            grid=(t // tm,),
            in_specs=[pl.BlockSpec((tm, d), lambda i, dest: (i, 0)),
                      pl.BlockSpec(memory_space=pl.ANY)],
            out_specs=pl.BlockSpec(memory_space=pl.ANY),
            scratch_shapes=[pltpu.SemaphoreType.DMA(())],
        ),
        out_shape=jax.ShapeDtypeStruct((n_rows, d), h.dtype),
        input_output_aliases={2: 0},
        compiler_params=_cparams(("arbitrary",)),
        name="moe_dispatch",
    )(dest, h, xs0)


def _expert_body(be_ref, nu_ref, x_ref, wg_ref, wu_ref, wd_ref, y_ref, wg_s, wu_s, wd_s):
    i = pl.program_id(0)
    e = be_ref[i]
    prev = be_ref[jnp.maximum(i - 1, 0)]

    @pl.when((i == 0) | (e != prev))
    def _():
        wg_s[...] = wg_ref[...].astype(BF16)
        wu_s[...] = wu_ref[...].astype(BF16)
        wd_s[...] = wd_ref[...].astype(BF16)

    @pl.when(i < nu_ref[0])
    def _():
        x = x_ref[...].astype(BF16)
        g = jnp.dot(x, wg_s[...], preferred_element_type=F32)
        u = jnp.dot(x, wu_s[...], preferred_element_type=F32)
        mid = (jax.nn.silu(g) * u).astype(BF16)
        y_ref[...] = jnp.dot(mid, wd_s[...], preferred_element_type=F32)

    @pl.when(i >= nu_ref[0])
    def _():
        y_ref[...] = jnp.zeros(y_ref.shape, y_ref.dtype)


def _experts(block_e, n_used, xs, w_gate, w_up, w_down):
    n_rows, d = xs.shape
    n_blocks = n_rows // MOE_ROWS
    ff = w_gate.shape[2]
    blk = lambda i, be, nu: (jnp.minimum(i, nu[0] - 1), 0)
    wsel = lambda i, be, nu: (be[i], 0, 0)
    return pl.pallas_call(
        _expert_body,
        grid_spec=pltpu.PrefetchScalarGridSpec(
            num_scalar_prefetch=2,
            grid=(n_blocks,),
            in_specs=[pl.BlockSpec((MOE_ROWS, d), blk),
                      pl.BlockSpec((None, d, ff), wsel),
                      pl.BlockSpec((None, d, ff), wsel),
                      pl.BlockSpec((None, ff, d), wsel)],
            out_specs=pl.BlockSpec((MOE_ROWS, d), lambda i, be, nu: (i, 0)),
            scratch_shapes=[pltpu.VMEM((d, ff), BF16), pltpu.VMEM((d, ff), BF16), pltpu.VMEM((ff, d), BF16)],
        ),
        out_shape=jax.ShapeDtypeStruct((n_rows, d), F32),
        compiler_params=_cparams(("arbitrary",)),
        name="moe_experts",
    )(block_e, n_used, xs, w_gate, w_up, w_down)


def _combine_body(dest_ref, x_ref, info_ref, g_ref, ys_ref, o_ref, buf0, buf1, sem, *, tm):
    base = pl.program_id(0) * tm
    bufs = (buf0, buf1)

    def issue(r, c):
        for k in range(EXPERT_TOPK):
            _row_copy(ys_ref, dest_ref[EXPERT_TOPK * (base + r) + k], bufs[k], r, sem).start()
        return c

    lax.fori_loop(0, tm, issue, 0)

    def drain(r, c):
        for k in range(EXPERT_TOPK):
            _row_copy(ys_ref, dest_ref[EXPERT_TOPK * (base + r) + k], bufs[k], r, sem).wait()
        return c

    lax.fori_loop(0, tm, drain, 0)
    info = info_ref[...]
    y = x_ref[...] + (info[:, 2:3] * buf0[...] + info[:, 3:4] * buf1[...])
    ms = jnp.mean(y * y, axis=-1, keepdims=True)
    o_ref[...] = y * lax.rsqrt(ms + RMS_EPS) * g_ref[...]


def _combine(dest, x1, info, g, ys, tm=256):
    t, d = x1.shape
    return pl.pallas_call(
        functools.partial(_combine_body, tm=tm),
        grid_spec=pltpu.PrefetchScalarGridSpec(
            num_scalar_prefetch=1,
            grid=(t // tm,),
            in_specs=[pl.BlockSpec((tm, d), lambda i, dest: (i, 0)),
                      pl.BlockSpec((tm, LANES), lambda i, dest: (i, 0)),
                      pl.BlockSpec((1, d), lambda i, dest: (0, 0)),
                      pl.BlockSpec(memory_space=pl.ANY)],
            out_specs=pl.BlockSpec((tm, d), lambda i, dest: (i, 0)),
            scratch_shapes=[pltpu.VMEM((tm, d), F32), pltpu.VMEM((tm, d), F32), pltpu.SemaphoreType.DMA(())],
        ),
        out_shape=jax.ShapeDtypeStruct((t, d), F32),
        compiler_params=_cparams(("arbitrary",)),
        name="moe_combine",
    )(dest, x1, info, g.reshape(1, d), ys)


def _nsa(proj, gate_cols, pe_k, w1_k, w2_k, pe_v, w1_v, w2_v, tab, b, s):
    del gate_cols
    g, dh = NSA_KV_HEADS, HEAD_DIM
    qw = NSA_HEADS * dh
    nc = s // CMP_STRIDE

    def blocks16(col0):
        a = proj[:, :, col0:col0 + g * dh].reshape(b, nc, CMP_STRIDE, g, dh)
        return a.transpose(0, 3, 1, 2, 4).reshape(b * g, nc, CMP_STRIDE * dh)

    kc = _compress(blocks16(qw), pe_k, w1_k, w2_k).reshape(b, g, nc, dh)
    vc = _compress(blocks16(qw + g * dh), pe_v, w1_v, w2_v).reshape(b, g, nc, dh)

    tpos = np.arange(s)[:, None]
    c_start = np.arange(nc)[None, :] * CMP_STRIDE
    bias_c = jnp.transpose(tab[_rel_bucket(tpos - (c_start + CMP_BLOCK - 1))], (2, 0, 1)).astype(F32)
    n_sel = s // SEL_BLOCK
    sb = np.arange(n_sel)[:, None] * SEL_BLOCK
    overlap = jnp.asarray((c_start < sb + SEL_BLOCK) & (c_start + CMP_BLOCK > sb), BF16)
    o_c, member = _nsa_cmp(proj, kc, vc, bias_c, overlap, b, s)

    t = ATT_TILE
    nq = s // t
    nn = _n_near(t)
    kblk = qw // dh
    bias_d = _bias_tiles(tab, t, nn + 1)
    o_s = _flash(proj, 0, proj, kblk + 2 * g, proj, kblk + 3 * g, bias_d, NSA_HEADS, NSA_GROUP, NSA_GROUP,
                 _causal_pairs(nq, nn), member=member, et=_block_onehot(s, SEL_BLOCK), name="nsa_selected")
    n_back = -(-WINDOW // t)
    bias_w = _bias_tiles(tab, t, n_back + 1, window=WINDOW)
    o_w = _flash(proj, 0, proj, kblk + 4 * g, proj, kblk + 5 * g, bias_w, NSA_HEADS, NSA_GROUP, NSA_GROUP,
                 _window_pairs(nq, n_back), name="nsa_window")
    return o_c, o_s, o_w


def _moba(proj, tab, b, s):
    member = _moba_gate(proj, b, s)
    t = ATT_TILE
    nn = _n_near(t)
    nh = 4
    ng = MOBA_HEADS // nh
    bias_d = _bias_tiles(tab, t, nn + 1)
    return _flash(proj, 0, proj, ng, proj, 2 * ng, bias_d, MOBA_HEADS, 1, nh, _causal_pairs(s // t, nn),
                  member=member, et=_block_onehot(s, MOBA_BLOCK), name="moba_attn")


def _moe(x1, g_ffn, w_group, b_group, w_router, b_router, w_gate, w_up, w_down, g_final):
    t, d = x1.shape
    ng, _, epg = w_router.shape
    w_gr = jnp.concatenate([w_group, jnp.transpose(w_router, (1, 0, 2)).reshape(d, ng * epg),
                            jnp.zeros((d, LANES - ng - ng * epg), F32)], axis=1)
    b_gr = jnp.concatenate([b_group, b_router.reshape(-1), jnp.zeros((LANES - ng - ng * epg,), F32)]).reshape(1, LANES)
    h, info, cnt = _route(x1, g_ffn, w_gr, b_gr)
    n_e = ng * epg
    n_assign = t * EXPERT_TOPK
    n_blocks = -(-(n_assign + n_e * (MOE_ROWS - 1)) // MOE_ROWS)
    counts = cnt[0, :n_e].astype(jnp.int32)
    padded = (counts + MOE_ROWS - 1) // MOE_ROWS * MOE_ROWS
    pad_end = jnp.cumsum(padded)
    pad_start = pad_end - padded
    expert = info[:, 0:EXPERT_TOPK].astype(jnp.int32)
    rank = info[:, 4:4 + EXPERT_TOPK].astype(jnp.int32)
    dest = (pad_start[expert] + rank).reshape(-1)
    block_e = jnp.minimum(jnp.searchsorted(pad_end, jnp.arange(n_blocks) * MOE_ROWS, side='right'),
                          n_e - 1).astype(jnp.int32)
    n_used = (pad_end[-1:] // MOE_ROWS).astype(jnp.int32)
    xs = _dispatch(dest, h, n_blocks * MOE_ROWS)
    ys = _experts(block_e, n_used, xs, w_gate, w_up, w_down)
    return _combine(dest, x1, info, g_final, ys)


def kernel(x, rel_bias, norm_mix, w_in, cmp_pe_k, cmp_w1_k, cmp_w2_k, cmp_pe_v, cmp_w1_v, cmp_w2_v, w_up_nsa,
           w_up_moba, w_out, norm_ffn, w_group, b_group, w_router, b_router, w_exp_gate, w_exp_up, w_exp_down,
           final_norm):
    b, s, d = x.shape
    t = b * s
    depth = w_in.shape[0]
    tab_a = rel_bias[:, :NSA_HEADS]
    tab_b = rel_bias[:, NSA_HEADS:]
    a_cols = NSA_HEADS * HEAD_DIM + 6 * NSA_KV_HEADS * HEAD_DIM
    gate_cols = 3 * NSA_HEADS
    b_cols = 3 * MOBA_HEADS * HEAD_DIM
    xt = x.reshape(t, d)
    out = None
    for l in range(depth):
        h = _rmsnorm(xt, norm_mix[l], BF16)
        wl = w_in[l]
        w_rest = wl[:, a_cols + gate_cols:]
        proj_a = _matmul(h, wl, 0, a_cols, BF16, name="in_proj_a").reshape(b, s, a_cols)
        gate_a = _matmul(h, wl, a_cols, LANES, F32, tn=LANES, name="in_proj_gate")
        proj_b = _matmul(h, w_rest, 0, b_cols, BF16, name="in_proj_b").reshape(b, s, b_cols)
        gm = _matmul(h, w_rest, b_cols, 2 * d, F32, name="in_proj_gm")
        o_c, o_s, o_w = _nsa(proj_a, gate_cols, cmp_pe_k[l], cmp_w1_k[l], cmp_w2_k[l],
                             cmp_pe_v[l], cmp_w1_v[l], cmp_w2_v[l], tab_a, b, s)
        o_b = _moba(proj_b, tab_b, b, s)
        merged = _merge(o_c.reshape(t, -1), o_s.reshape(t, -1), o_w.reshape(t, -1), gate_a,
                        o_b.reshape(t, -1), gm, w_up_nsa[l], w_up_moba[l])
        x1 = _matmul(merged, w_out[l], 0, d, F32, res=xt, name="out_proj")
        assert l == depth - 1, "only the last layer's MoE is fused with the final norm"
        out = _moe(x1, norm_ffn[l], w_group[l], b_group[l], w_router[l], b_router[l],
                   w_exp_gate[l], w_exp_up[l], w_exp_down[l], final_norm)
    return out.reshape(b, s, d)
```

```python
import functools
import math

import numpy as np
import jax
import jax.numpy as jnp
from jax import lax
from jax.experimental import pallas as pl
from jax.experimental.pallas import tpu as pltpu

F32 = jnp.float32
BF16 = jnp.bfloat16

HEAD_DIM = 128
NSA_HEADS = 8
NSA_KV_HEADS = 2
NSA_GROUP = NSA_HEADS // NSA_KV_HEADS
CMP_BLOCK = 32
CMP_STRIDE = 16
SEL_BLOCK = 64
SEL_TOPN = 16
WINDOW = 512
FORCED_SCORE = 1e4
MOBA_HEADS = 8
MOBA_BLOCK = 256
MOBA_TOPK = 3
REL_BUCKETS = 32
REL_MAX_DIST = 128
N_GROUPS = 8
EXPERTS_PER_GROUP = 8
N_EXPERTS = N_GROUPS * EXPERTS_PER_GROUP
EXPERT_TOPK = 2
RMS_EPS = 1e-6

LANES = 128
ATT_TILE = 512
MOE_ROWS = 256
MASK_BIG = 1e30
M_INIT = -3e38
VMEM_LIMIT = 48 * 1024 * 1024


def _cparams(sem, vmem=VMEM_LIMIT):
    return pltpu.CompilerParams(dimension_semantics=sem, vmem_limit_bytes=vmem)


def _rmsnorm_body(x_ref, g_ref, o_ref):
    x = x_ref[...]
    ms = jnp.mean(x * x, axis=-1, keepdims=True)
    o_ref[...] = (x * lax.rsqrt(ms + RMS_EPS) * g_ref[...]).astype(o_ref.dtype)


def _rmsnorm(x, g, out_dtype, tm=512):
    t, d = x.shape
    return pl.pallas_call(
        _rmsnorm_body,
        grid=(t // tm,),
        in_specs=[pl.BlockSpec((tm, d), lambda i: (i, 0)),
                  pl.BlockSpec((1, d), lambda i: (0, 0))],
        out_specs=pl.BlockSpec((tm, d), lambda i: (i, 0)),
        out_shape=jax.ShapeDtypeStruct((t, d), out_dtype),
        compiler_params=_cparams(("parallel",)),
        name="rmsnorm",
    )(x, g.reshape(1, d))


def _mm_body(*refs, has_res):
    if has_res:
        a_ref, w_ref, r_ref, o_ref = refs
    else:
        a_ref, w_ref, o_ref = refs
    acc = jnp.dot(a_ref[...], w_ref[...].astype(BF16), preferred_element_type=F32)
    if has_res:
        acc = acc + r_ref[...]
    o_ref[...] = acc.astype(o_ref.dtype)


def _matmul(a, w, col0, ncols, out_dtype, res=None, tm=1024, tn=512, name="matmul"):
    t, k = a.shape
    tn = min(tn, ncols)
    assert col0 % tn == 0 and ncols % tn == 0 and t % tm == 0
    off = col0 // tn
    in_specs = [pl.BlockSpec((tm, k), lambda i, j: (i, 0)),
                pl.BlockSpec((k, tn), lambda i, j: (0, j + off))]
    args = [a, w]
    if res is not None:
        in_specs.append(pl.BlockSpec((tm, tn), lambda i, j: (i, j)))
        args.append(res)
    return pl.pallas_call(
        functools.partial(_mm_body, has_res=res is not None),
        grid=(t // tm, ncols // tn),
        in_specs=in_specs,
        out_specs=pl.BlockSpec((tm, tn), lambda i, j: (i, j)),
        out_shape=jax.ShapeDtypeStruct((t, ncols), out_dtype),
        compiler_params=_cparams(("parallel", "parallel")),
        name=name,
    )(*args)


def _compress_body(u_ref, pe_ref, w1_ref, w2_ref, o_ref, *, nc):
    u = u_ref[...]
    w1 = w1_ref[...].astype(BF16)
    half = u.shape[1]
    a = jnp.dot(u, w1[:half], preferred_element_type=F32)
    b = jnp.dot(u, w1[half:], preferred_element_type=F32)
    peb = jnp.dot(pe_ref[...].astype(BF16), w1, preferred_element_type=F32)[0:1]
    pre = a + pltpu.roll(b, nc - 1, 0) + peb
    hid = jax.nn.gelu(pre)
    o_ref[...] = jnp.dot(hid.astype(BF16), w2_ref[...].astype(BF16),
                         preferred_element_type=F32).astype(o_ref.dtype)


def _compress(u, pe, w1, w2):
    bg, nc, kk = u.shape
    hid = w1.shape[1]
    dh = w2.shape[1]
    pe8 = jnp.broadcast_to(pe.reshape(1, -1), (16, pe.size))
    return pl.pallas_call(
        functools.partial(_compress_body, nc=nc),
        grid=(bg,),
        in_specs=[pl.BlockSpec((None, nc, kk), lambda i: (i, 0, 0)),
                  pl.BlockSpec((16, 2 * kk), lambda i: (0, 0)),
                  pl.BlockSpec((2 * kk, hid), lambda i: (0, 0)),
                  pl.BlockSpec((hid, dh), lambda i: (0, 0))],
        out_specs=pl.BlockSpec((None, nc, dh), lambda i: (i, 0, 0)),
        out_shape=jax.ShapeDtypeStruct((bg, nc, dh), BF16),
        compiler_params=_cparams(("parallel",)),
        name="nsa_compress",
    )(u, pe8, w1, w2)


def _split3(x):
    p1 = x.astype(BF16)
    r = x - p1.astype(F32)
    p2 = r.astype(BF16)
    p3 = (r - p2.astype(F32)).astype(BF16)
    return p1, p2, p3


def _rank_count(score, n_rows):
    n_iota = lax.broadcasted_iota(jnp.int32, score.shape, 0)
    cnt = jnp.zeros(score.shape, F32)
    for m in range(n_rows):
        row = score[m:m + 1, :]
        tie = jnp.where(n_iota > m, 1.0, 0.0)
        cnt = cnt + jnp.where(row > score, 1.0, jnp.where(row == score, tie, 0.0))
    return cnt


def _nsa_cmp_body(q_ref, kc_ref, vc_ref, bias_ref, ov_ref, oc_ref, mem_ref, *, tq, nc, n_sel, scale):
    t0 = pl.program_id(2) * tq
    kc = kc_ref[...]
    vc = vc_ref[...]
    t_idx = t0 + lax.broadcasted_iota(jnp.int32, (tq, nc), 0)
    c_idx = lax.broadcasted_iota(jnp.int32, (tq, nc), 1)
    dist = t_idx - (c_idx * CMP_STRIDE + (CMP_BLOCK - 1))
    valid = dist >= 0
    n_k = REL_MAX_DIST // CMP_STRIDE
    kk = dist // CMP_STRIDE
    k_is = [kk == k for k in range(n_k)]
    psum = jnp.zeros((tq, nc), F32)
    for j in range(NSA_GROUP):
        hs = slice(j * HEAD_DIM, (j + 1) * HEAD_DIM)
        gt = bias_ref[j]
        bias = jnp.broadcast_to(gt[:, n_k:n_k + 1], (tq, nc))
        for k in range(n_k):
            bias = jnp.where(k_is[k], gt[:, k:k + 1], bias)
        s = lax.dot_general(q_ref[:, hs], kc, (((1,), (1,)), ((), ())),
                            preferred_element_type=F32) * scale + bias
        s = jnp.where(valid, s, -MASK_BIG)
        m = jnp.max(s, axis=-1, keepdims=True)
        m = jnp.where(m > -0.5 * MASK_BIG, m, 0.0)
        p = jnp.where(valid, jnp.exp(s - m), 0.0)
        d = jnp.sum(p, axis=-1, keepdims=True)
        p = p / jnp.where(d > 0, d, 1.0)
        oc_ref[:, hs] = jnp.dot(p.astype(BF16), vc, preferred_element_type=F32).astype(oc_ref.dtype)
        psum = psum + p
    ov = ov_ref[...]
    nt = (((1,), (1,)), ((), ()))
    p1, p2, p3 = _split3(psum)
    psel = (lax.dot_general(ov, p1, nt, preferred_element_type=F32)
            + lax.dot_general(ov, p2, nt, preferred_element_type=F32)
            + lax.dot_general(ov, p3, nt, preferred_element_type=F32))
    n_idx = lax.broadcasted_iota(jnp.int32, (n_sel, tq), 0)
    tt = t0 + lax.broadcasted_iota(jnp.int32, (n_sel, tq), 1)
    cur = tt // SEL_BLOCK
    forced = jnp.where(n_idx == 0, 1.0, jnp.where(n_idx == cur, 1.0, jnp.where(n_idx == cur - 1, 1.0, 0.0)))
    score = jnp.where(forced > 0.5, FORCED_SCORE, jnp.where(n_idx * SEL_BLOCK <= tt, psel, -1.0))
    cnt = _rank_count(score, n_sel)
    member = jnp.where(cnt < float(min(SEL_TOPN, n_sel)), 1.0, 0.0)
    if n_sel < LANES:
        member = jnp.concatenate([member, jnp.zeros((LANES - n_sel, tq), F32)], axis=0)
    mem_ref[...] = member.T.astype(mem_ref.dtype)


def _nsa_cmp(proj, kc, vc, tab, overlap, b, s, tq=256):
    g = NSA_KV_HEADS
    nc = kc.shape[2]
    n_sel = s // SEL_BLOCK
    gw = NSA_GROUP * HEAD_DIM
    assert tq % CMP_STRIDE == 0 and REL_MAX_DIST % CMP_STRIDE == 0
    n_k = REL_MAX_DIST // CMP_STRIDE
    rho = (np.arange(tq)[:, None] - (CMP_BLOCK - 1)) % CMP_STRIDE
    dd = np.concatenate([rho + CMP_STRIDE * np.arange(n_k)[None, :], np.full((tq, 1), REL_MAX_DIST)], axis=1)
    gtab = jnp.transpose(tab[_rel_bucket(dd)], (2, 0, 1)).astype(F32)
    bias_c = jnp.pad(gtab, ((0, 0), (0, 0), (0, LANES - n_k - 1)))
    body = functools.partial(_nsa_cmp_body, tq=tq, nc=nc, n_sel=n_sel, scale=HEAD_DIM ** -0.5)
    return pl.pallas_call(
        body,
        grid=(b, g, s // tq),
        in_specs=[pl.BlockSpec((None, tq, gw), lambda bi, gi, i: (bi, i, gi)),
                  pl.BlockSpec((None, None, nc, HEAD_DIM), lambda bi, gi, i: (bi, gi, 0, 0)),
                  pl.BlockSpec((None, None, nc, HEAD_DIM), lambda bi, gi, i: (bi, gi, 0, 0)),
                  pl.BlockSpec((NSA_GROUP, tq, LANES), lambda bi, gi, i: (gi, 0, 0)),
                  pl.BlockSpec((n_sel, nc), lambda bi, gi, i: (0, 0))],
        out_specs=[pl.BlockSpec((None, tq, gw), lambda bi, gi, i: (bi, i, gi)),
                   pl.BlockSpec((None, None, tq, LANES), lambda bi, gi, i: (bi, gi, i, 0))],
        out_shape=[jax.ShapeDtypeStruct((b, s, NSA_HEADS * HEAD_DIM), BF16),
                   jax.ShapeDtypeStruct((b, g, s, LANES), BF16)],
        compiler_params=_cparams(("parallel", "parallel", "parallel")),
        name="nsa_cmp_select",
    )(proj, kc, vc, bias_c, overlap)


def _moba_gate_body(q_ref, k_ref, mem_ref, *, s, nblk):
    k = k_ref[...].astype(F32)
    kmean = jnp.mean(k.reshape(nblk, MOBA_BLOCK, HEAD_DIM), axis=1)
    k1 = kmean.astype(BF16)
    k2 = (kmean - k1.astype(F32)).astype(BF16)
    q = q_ref[...]
    nt = (((1,), (1,)), ((), ()))
    gate = (lax.dot_general(k1, q, nt, preferred_element_type=F32)
            + lax.dot_general(k2, q, nt, preferred_element_type=F32))
    n_idx = lax.broadcasted_iota(jnp.int32, (nblk, s), 0)
    own = lax.broadcasted_iota(jnp.int32, (nblk, s), 1) // MOBA_BLOCK
    past = n_idx < own
    score = jnp.where(past, gate, -MASK_BIG)
    cnt = _rank_count(score, nblk)
    n_top = max(1, min(MOBA_TOPK, nblk - 1))
    sel = jnp.where(past, jnp.where(cnt < float(n_top), 1.0, 0.0), 0.0)
    member = jnp.where(n_idx == own, 1.0, sel)
    member = jnp.concatenate([member, jnp.zeros((LANES - nblk, s), F32)], axis=0)
    mem_ref[...] = member.T.astype(mem_ref.dtype)


def _moba_gate(proj, b, s):
    h = MOBA_HEADS
    nblk = s // MOBA_BLOCK
    return pl.pallas_call(
        functools.partial(_moba_gate_body, s=s, nblk=nblk),
        grid=(b, h),
        in_specs=[pl.BlockSpec((None, s, HEAD_DIM), lambda bi, hi: (bi, 0, hi)),
                  pl.BlockSpec((None, s, HEAD_DIM), lambda bi, hi: (bi, 0, h + hi))],
        out_specs=pl.BlockSpec((None, None, s, LANES), lambda bi, hi: (bi, hi, 0, 0)),
        out_shape=jax.ShapeDtypeStruct((b, h, s, LANES), BF16),
        compiler_params=_cparams(("parallel", "parallel")),
        name="moba_gate",
    )(proj, proj)


def _flash_body(qi_ref, ki_ref, bo_ref, fl_ref, *refs, nh, ratio, nm, scale):
    if nm:
        q_ref, k_ref, v_ref, bvec_ref, mem_ref, et_ref, o_ref, m_ref, l_ref, acc_ref, bias_ref = refs
    else:
        q_ref, k_ref, v_ref, bvec_ref, o_ref, m_ref, l_ref, acc_ref, bias_ref = refs
    del qi_ref, ki_ref
    p = pl.program_id(2)
    flag = fl_ref[p]
    bo = bo_ref[p]
    t = q_ref.shape[0]

    @pl.when(p == 0)
    def _():
        rows = 64
        for h in range(nh):
            for o in range(bvec_ref.shape[1]):
                vec = bvec_ref[h, o][0:1, :]
                for rc in range(t // rows):
                    x = pltpu.roll(jnp.broadcast_to(vec, (rows, 2 * t)), rc * rows, 1, stride=1, stride_axis=0)
                    bias_ref[h, o, rc * rows:(rc + 1) * rows, :] = x[:, :t]

    @pl.when((flag & 1) != 0)
    def _():
        m_ref[...] = jnp.full(m_ref.shape, M_INIT, F32)
        l_ref[...] = jnp.zeros(l_ref.shape, F32)
        acc_ref[...] = jnp.zeros(acc_ref.shape, F32)

    nt = (((1,), (1,)), ((), ()))
    reps = t // LANES
    for h in range(nh):
        hs = slice(h * HEAD_DIM, (h + 1) * HEAD_DIM)
        kv = h // ratio
        ks = slice(kv * HEAD_DIM, (kv + 1) * HEAD_DIM)
        q = q_ref[:, hs]
        k = k_ref[:, ks]
        if nm:
            mneg = mem_ref[h // (nh // nm)] - 1.0
            q = jnp.concatenate([q, mneg.astype(BF16)], axis=1)
            k = jnp.concatenate([k, et_ref[...]], axis=1)
        sc = lax.dot_general(q, k, nt, preferred_element_type=F32) * scale + bias_ref[h, bo]
        m_prev = m_ref[h]
        m_new = jnp.maximum(m_prev, jnp.max(sc, axis=-1, keepdims=True))
        alpha = jnp.exp(m_prev - m_new)
        pm = jnp.exp(sc - jnp.tile(m_new, (1, reps)))
        l_ref[h] = alpha * l_ref[h] + jnp.sum(pm, axis=-1, keepdims=True)
        acc_ref[h] = alpha * acc_ref[h] + jnp.dot(pm.astype(BF16), v_ref[:, ks], preferred_element_type=F32)
        m_ref[h] = m_new

    @pl.when((flag & 2) != 0)
    def _():
        for h in range(nh):
            l = l_ref[h]
            o_ref[:, h * HEAD_DIM:(h + 1) * HEAD_DIM] = (acc_ref[h] / jnp.where(l > 0, l, 1.0)).astype(o_ref.dtype)


def _flash(q_arr, q_off, k_arr, k_off, v_arr, v_off, bias, n_heads, ratio, nh, pairs, member=None, et=None,
           name="flash"):
    b, s, _ = q_arr.shape
    t = ATT_TILE
    nkv = nh // ratio
    ng = n_heads // nh
    qi = jnp.asarray([p[0] for p in pairs], jnp.int32)
    ki = jnp.asarray([p[1] for p in pairs], jnp.int32)
    bo = jnp.asarray([p[2] for p in pairs], jnp.int32)
    fl = jnp.asarray([p[3] for p in pairs], jnp.int32)
    nb = bias.shape[1]
    nm = 0
    in_specs = [
        pl.BlockSpec((None, t, nh * HEAD_DIM), lambda bi, gi, p, qi, ki, bo, fl: (bi, qi[p], q_off + gi)),
        pl.BlockSpec((None, t, nkv * HEAD_DIM), lambda bi, gi, p, qi, ki, bo, fl: (bi, ki[p], k_off + gi)),
        pl.BlockSpec((None, t, nkv * HEAD_DIM), lambda bi, gi, p, qi, ki, bo, fl: (bi, ki[p], v_off + gi)),
        pl.BlockSpec((nh, nb, 8, 2 * t), lambda bi, gi, p, qi, ki, bo, fl: (gi, 0, 0, 0)),
    ]
    args = [q_arr, k_arr, v_arr, bias]
    if member is not None:
        nm = member.shape[1] // ng
        in_specs += [
            pl.BlockSpec((None, nm, t, LANES), lambda bi, gi, p, qi, ki, bo, fl: (bi, gi, qi[p], 0)),
            pl.BlockSpec((t, LANES), lambda bi, gi, p, qi, ki, bo, fl: (ki[p], 0)),
        ]
        args += [member, et]
    body = functools.partial(_flash_body, nh=nh, ratio=ratio, nm=nm, scale=HEAD_DIM ** -0.5)
    return pl.pallas_call(
        body,
        grid_spec=pltpu.PrefetchScalarGridSpec(
            num_scalar_prefetch=4,
            grid=(b, ng, len(pairs)),
            in_specs=in_specs,
            out_specs=pl.BlockSpec((None, t, nh * HEAD_DIM), lambda bi, gi, p, qi, ki, bo, fl: (bi, qi[p], gi)),
            scratch_shapes=[pltpu.VMEM((nh, t, LANES), F32)] * 3 + [pltpu.VMEM((nh, nb, t, t), F32)],
        ),
        out_shape=jax.ShapeDtypeStruct((b, s, n_heads * HEAD_DIM), BF16),
        compiler_params=_cparams(("parallel", "parallel", "arbitrary")),
        name=name,
    )(qi, ki, bo, fl, *args)


def _rel_bucket(dist):
    n = jnp.maximum(jnp.asarray(dist, jnp.int32), 0)
    max_exact = REL_BUCKETS // 2
    nf = jnp.maximum(n, 1).astype(jnp.float32)
    large = max_exact + (jnp.log(nf / max_exact) / math.log(REL_MAX_DIST / max_exact)
                         * (REL_BUCKETS - max_exact)).astype(jnp.int32)
    return jnp.where(n < max_exact, n, jnp.minimum(large, REL_BUCKETS - 1))


def _n_near(t):
    return -(-(REL_MAX_DIST - 1 + t) // t)


def _bias_vecs(tab, t, n_off, window=None):
    k = np.arange(2 * t)[None, :]
    dist = np.arange(n_off)[:, None] * t + np.where(k < t, -k, 2 * t - k)
    ok = dist >= 0
    if window is not None:
        ok &= dist < window
    bias = jnp.where(jnp.asarray(ok)[..., None], tab[_rel_bucket(dist)], -MASK_BIG)
    bias = jnp.transpose(bias, (2, 0, 1)).astype(F32)
    return jnp.broadcast_to(bias[:, :, None, :], (bias.shape[0], n_off, 8, 2 * t))


def _causal_pairs(nq, n_near):
    pairs = []
    for qi in range(nq):
        for ki in range(qi + 1):
            pairs.append((qi, ki, min(qi - ki, n_near), (1 if ki == 0 else 0) | (2 if ki == qi else 0)))
    return pairs


def _window_pairs(nq, n_back):
    pairs = []
    for qi in range(nq):
        lo = max(0, qi - n_back)
        for ki in range(lo, qi + 1):
            pairs.append((qi, ki, qi - ki, (1 if ki == lo else 0) | (2 if ki == qi else 0)))
    return pairs


def _block_onehot(s, blk):
    return jnp.asarray(np.where(np.arange(s)[:, None] // blk == np.arange(LANES)[None, :], MASK_BIG, 0.0), BF16)


def _merge_body(oc_ref, os_ref, ow_ref, gl_ref, ob_ref, gma_ref, gmb_ref, wa_ref, wb_ref, o_ref, oa_ref):
    @pl.when(pl.program_id(1) == 0)
    def _():
        gates = jax.nn.sigmoid(gl_ref[...])
        for h in range(NSA_HEADS):
            hs = slice(h * HEAD_DIM, (h + 1) * HEAD_DIM)
            mix = (gates[:, 3 * h:3 * h + 1] * oc_ref[:, hs].astype(F32)
                   + gates[:, 3 * h + 1:3 * h + 2] * os_ref[:, hs].astype(F32)
                   + gates[:, 3 * h + 2:3 * h + 3] * ow_ref[:, hs].astype(F32))
            oa_ref[:, hs] = mix.astype(BF16)

    ya = jnp.dot(oa_ref[...], wa_ref[...].astype(BF16), preferred_element_type=F32)
    yb = jnp.dot(ob_ref[...], wb_ref[...].astype(BF16), preferred_element_type=F32)
    o_ref[...] = (jax.nn.sigmoid(gma_ref[...]) * ya + jax.nn.sigmoid(gmb_ref[...]) * yb).astype(o_ref.dtype)


def _merge(o_c, o_s, o_w, gate_logits, o_b, gm, w_up_a, w_up_b, tm=512, tn=512):
    t, ka = o_c.shape
    kb = o_b.shape[1]
    d = w_up_a.shape[1]
    nj = d // tn
    row = lambda i, j: (i, 0)
    return pl.pallas_call(
        _merge_body,
        grid=(t // tm, nj),
        in_specs=[pl.BlockSpec((tm, ka), row), pl.BlockSpec((tm, ka), row), pl.BlockSpec((tm, ka), row),
                  pl.BlockSpec((tm, LANES), row), pl.BlockSpec((tm, kb), row),
                  pl.BlockSpec((tm, tn), lambda i, j: (i, j)),
                  pl.BlockSpec((tm, tn), lambda i, j: (i, j + nj)),
                  pl.BlockSpec((ka, tn), lambda i, j: (0, j)),
                  pl.BlockSpec((kb, tn), lambda i, j: (0, j))],
        out_specs=pl.BlockSpec((tm, tn), lambda i, j: (i, j)),
        out_shape=jax.ShapeDtypeStruct((t, d), BF16),
        scratch_shapes=[pltpu.VMEM((tm, ka), BF16)],
        compiler_params=_cparams(("parallel", "arbitrary")),
        name="merge_up",
    )(o_c, o_s, o_w, gate_logits, o_b, gm, gm, w_up_a, w_up_b)


def _route_body(x_ref, g_ref, w_ref, b_ref, h_ref, info_ref, cnt_ref, carry_ref, *, tm):
    @pl.when(pl.program_id(0) == 0)
    def _():
        carry_ref[...] = jnp.zeros(carry_ref.shape, F32)

    x = x_ref[...]
    ms = jnp.mean(x * x, axis=-1, keepdims=True)
    h = x * lax.rsqrt(ms + RMS_EPS) * g_ref[...]
    h_ref[...] = h
    w = w_ref[...]
    h1 = h.astype(BF16)
    h2 = (h - h1.astype(F32)).astype(BF16)
    w1 = w.astype(BF16)
    w2 = (w - w1.astype(F32)).astype(BF16)
    logits = (jnp.dot(h1, w1, preferred_element_type=F32) + jnp.dot(h1, w2, preferred_element_type=F32)
              + jnp.dot(h2, w1, preferred_element_type=F32)) + b_ref[...]
    lane = lax.broadcasted_iota(jnp.int32, (tm, LANES), 1)
    lanef = lane.astype(F32)

    is_g = lane < N_GROUPS
    gl = jnp.where(is_g, logits, -MASK_BIG)
    ge = jnp.where(is_g, jnp.exp(gl - jnp.max(gl, axis=-1, keepdims=True)), 0.0)
    gp = ge / jnp.sum(ge, axis=-1, keepdims=True)
    g_val = jnp.max(gp, axis=-1, keepdims=True)
    g_idx = jnp.min(jnp.where(gp == g_val, lanef, float(LANES)), axis=-1, keepdims=True)

    lane_grp = ((lane - N_GROUPS) // EXPERTS_PER_GROUP).astype(F32)
    in_e = jnp.where(lane >= N_GROUPS, jnp.where(lane < N_GROUPS + N_EXPERTS, 1.0, 0.0), 0.0)
    is_e = jnp.where(lane_grp == g_idx, in_e, 0.0) > 0.5
    el = jnp.where(is_e, logits, -MASK_BIG)
    ee = jnp.where(is_e, jnp.exp(el - jnp.max(el, axis=-1, keepdims=True)), 0.0)
    ep = jnp.where(is_e, ee / jnp.sum(ee, axis=-1, keepdims=True), -1.0)
    v1 = jnp.max(ep, axis=-1, keepdims=True)
    l1 = jnp.min(jnp.where(ep == v1, lanef, float(LANES)), axis=-1, keepdims=True)
    ep2 = jnp.where(lanef == l1, -1.0, ep)
    v2 = jnp.max(ep2, axis=-1, keepdims=True)
    l2 = jnp.min(jnp.where(ep2 == v2, lanef, float(LANES)), axis=-1, keepdims=True)
    vs = v1 + v2
    wt1 = g_val * v1 / vs
    wt2 = g_val * v2 / vs
    e1 = l1 - float(N_GROUPS)
    e2 = l2 - float(N_GROUPS)

    oh = jnp.where(lanef == e1, 1.0, jnp.where(lanef == e2, 1.0, 0.0))
    r_i = lax.broadcasted_iota(jnp.int32, (tm, tm), 0)
    c_i = lax.broadcasted_iota(jnp.int32, (tm, tm), 1)
    tri = jnp.where(r_i > c_i, 1.0, 0.0).astype(BF16)
    base = jnp.dot(tri, oh.astype(BF16), preferred_element_type=F32) + carry_ref[...]
    r1 = jnp.sum(jnp.where(lanef == e1, base, 0.0), axis=-1, keepdims=True)
    r2 = jnp.sum(jnp.where(lanef == e2, base, 0.0), axis=-1, keepdims=True)
    carry_ref[...] = carry_ref[...] + jnp.sum(oh, axis=0, keepdims=True)
    cnt_ref[...] = jnp.broadcast_to(carry_ref[...], cnt_ref.shape)
    info = jnp.where(lane == 0, e1, jnp.where(lane == 1, e2, jnp.where(lane == 2, wt1, jnp.where(
        lane == 3, wt2, jnp.where(lane == 4, r1, jnp.where(lane == 5, r2, 0.0))))))
    info_ref[...] = info


def _route(x1, g, w_gr, b_gr, tm=512):
    t, d = x1.shape
    return pl.pallas_call(
        functools.partial(_route_body, tm=tm),
        grid=(t // tm,),
        in_specs=[pl.BlockSpec((tm, d), lambda i: (i, 0)),
                  pl.BlockSpec((1, d), lambda i: (0, 0)),
                  pl.BlockSpec((d, LANES), lambda i: (0, 0)),
                  pl.BlockSpec((1, LANES), lambda i: (0, 0))],
        out_specs=[pl.BlockSpec((tm, d), lambda i: (i, 0)),
                   pl.BlockSpec((tm, LANES), lambda i: (i, 0)),
                   pl.BlockSpec((8, LANES), lambda i: (0, 0))],
        out_shape=[jax.ShapeDtypeStruct((t, d), F32),
                   jax.ShapeDtypeStruct((t, LANES), F32),
                   jax.ShapeDtypeStruct((8, LANES), F32)],
        scratch_shapes=[pltpu.VMEM((1, LANES), F32)],
        compiler_params=_cparams(("arbitrary",)),
        name="moe_route",
    )(x1, g.reshape(1, d), w_gr, b_gr)


def _row_copy(src_ref, src_row, dst_ref, dst_row, sem):
    return pltpu.make_async_copy(src_ref.at[pl.ds(src_row, 1)], dst_ref.at[pl.ds(dst_row, 1)], sem)


def _dispatch_body(dest_ref, h_ref, xs_in_ref, xs_ref, sem, *, tm):
    del xs_in_ref
    base = pl.program_id(0) * tm

    def issue(r, c):
        for k in range(EXPERT_TOPK):
            _row_copy(h_ref, r, xs_ref, dest_ref[EXPERT_TOPK * (base + r) + k], sem).start()
        return c

    lax.fori_loop(0, tm, issue, 0)

    def drain(r, c):
        for k in range(EXPERT_TOPK):
            _row_copy(h_ref, r, xs_ref, dest_ref[EXPERT_TOPK * (base + r) + k], sem).wait()
        return c

    lax.fori_loop(0, tm, drain, 0)


def _dispatch(dest, h, n_rows, tm=512):
    t, d = h.shape
    xs0 = jnp.zeros((n_rows, d), h.dtype)
    return pl.pallas_call(
        functools.partial(_dispatch_body, tm=tm),
        grid_spec=pltpu.PrefetchScalarGridSpec(
            num_scalar_prefetch=1,
            grid=(t // tm,),
            in_specs=[pl.BlockSpec((tm, d), lambda i, dest: (i, 0)),
                      pl.BlockSpec(memory_space=pl.ANY)],
            out_specs=pl.BlockSpec(memory_space=pl.ANY),
            scratch_shapes=[pltpu.SemaphoreType.DMA(())],
        ),
        out_shape=jax.ShapeDtypeStruct((n_rows, d), h.dtype),
        input_output_aliases={2: 0},
        compiler_params=_cparams(("arbitrary",)),
        name="moe_dispatch",
    )(dest, h, xs0)


def _expert_body(be_ref, nu_ref, x_ref, wg_ref, wu_ref, wd_ref, y_ref, wg_s, wu_s, wd_s):
    i = pl.program_id(0)
    e = be_ref[i]
    prev = be_ref[jnp.maximum(i - 1, 0)]

    @pl.when((i == 0) | (e != prev))
    def _():
        wg_s[...] = wg_ref[...].astype(BF16)
        wu_s[...] = wu_ref[...].astype(BF16)
        wd_s[...] = wd_ref[...].astype(BF16)

    @pl.when(i < nu_ref[0])
    def _():
        x = x_ref[...].astype(BF16)
        g = jnp.dot(x, wg_s[...], preferred_element_type=F32)
        u = jnp.dot(x, wu_s[...], preferred_element_type=F32)
        mid = (jax.nn.silu(g) * u).astype(BF16)
        y_ref[...] = jnp.dot(mid, wd_s[...], preferred_element_type=F32)

    @pl.when(i >= nu_ref[0])
    def _():
        y_ref[...] = jnp.zeros(y_ref.shape, y_ref.dtype)


def _experts(block_e, n_used, xs, w_gate, w_up, w_down):
    n_rows, d = xs.shape
    n_blocks = n_rows // MOE_ROWS
    ff = w_gate.shape[2]
    blk = lambda i, be, nu: (jnp.minimum(i, nu[0] - 1), 0)
    wsel = lambda i, be, nu: (be[i], 0, 0)
    return pl.pallas_call(
        _expert_body,
        grid_spec=pltpu.PrefetchScalarGridSpec(
            num_scalar_prefetch=2,
            grid=(n_blocks,),
            in_specs=[pl.BlockSpec((MOE_ROWS, d), blk),
                      pl.BlockSpec((None, d, ff), wsel),
                      pl.BlockSpec((None, d, ff), wsel),
                      pl.BlockSpec((None, ff, d), wsel)],
            out_specs=pl.BlockSpec((MOE_ROWS, d), lambda i, be, nu: (i, 0)),
            scratch_shapes=[pltpu.VMEM((d, ff), BF16), pltpu.VMEM((d, ff), BF16), pltpu.VMEM((ff, d), BF16)],
        ),
        out_shape=jax.ShapeDtypeStruct((n_rows, d), F32),
        compiler_params=_cparams(("arbitrary",)),
        name="moe_experts",
    )(block_e, n_used, xs, w_gate, w_up, w_down)


def _combine_body(dest_ref, x_ref, info_ref, g_ref, ys_ref, o_ref, buf0, buf1, sem, *, tm):
    base = pl.program_id(0) * tm
    bufs = (buf0, buf1)

    def issue(r, c):
        for k in range(EXPERT_TOPK):
            _row_copy(ys_ref, dest_ref[EXPERT_TOPK * (base + r) + k], bufs[k], r, sem).start()
        return c

    lax.fori_loop(0, tm, issue, 0)

    def drain(r, c):
        for k in range(EXPERT_TOPK):
            _row_copy(ys_ref, dest_ref[EXPERT_TOPK * (base + r) + k], bufs[k], r, sem).wait()
        return c

    lax.fori_loop(0, tm, drain, 0)
    info = info_ref[...]
    y = x_ref[...] + (info[:, 2:3] * buf0[...] + info[:, 3:4] * buf1[...])
    ms = jnp.mean(y * y, axis=-1, keepdims=True)
    o_ref[...] = y * lax.rsqrt(ms + RMS_EPS) * g_ref[...]


def _combine(dest, x1, info, g, ys, tm=256):
    t, d = x1.shape
    return pl.pallas_call(
        functools.partial(_combine_body, tm=tm),
        grid_spec=pltpu.PrefetchScalarGridSpec(
            num_scalar_prefetch=1,
            grid=(t // tm,),
            in_specs=[pl.BlockSpec((tm, d), lambda i, dest: (i, 0)),
                      pl.BlockSpec((tm, LANES), lambda i, dest: (i, 0)),
                      pl.BlockSpec((1, d), lambda i, dest: (0, 0)),
                      pl.BlockSpec(memory_space=pl.ANY)],
            out_specs=pl.BlockSpec((tm, d), lambda i, dest: (i, 0)),
            scratch_shapes=[pltpu.VMEM((tm, d), F32), pltpu.VMEM((tm, d), F32), pltpu.SemaphoreType.DMA(())],
        ),
        out_shape=jax.ShapeDtypeStruct((t, d), F32),
        compiler_params=_cparams(("arbitrary",)),
        name="moe_combine",
    )(dest, x1, info, g.reshape(1, d), ys)


def _nsa(proj, gate_cols, pe_k, w1_k, w2_k, pe_v, w1_v, w2_v, tab, b, s):
    del gate_cols
    g, dh = NSA_KV_HEADS, HEAD_DIM
    qw = NSA_HEADS * dh
    nc = s // CMP_STRIDE

    def blocks16(col0):
        a = proj[:, :, col0:col0 + g * dh].reshape(b, nc, CMP_STRIDE, g, dh)
        return a.transpose(0, 3, 1, 2, 4).reshape(b * g, nc, CMP_STRIDE * dh)

    kc = _compress(blocks16(qw), pe_k, w1_k, w2_k).reshape(b, g, nc, dh)
    vc = _compress(blocks16(qw + g * dh), pe_v, w1_v, w2_v).reshape(b, g, nc, dh)

    c_start = np.arange(nc)[None, :] * CMP_STRIDE
    n_sel = s // SEL_BLOCK
    sb = np.arange(n_sel)[:, None] * SEL_BLOCK
    overlap = jnp.asarray((c_start < sb + SEL_BLOCK) & (c_start + CMP_BLOCK > sb), BF16)
    o_c, member = _nsa_cmp(proj, kc, vc, tab, overlap, b, s)

    t = ATT_TILE
    nq = s // t
    nn = _n_near(t)
    kblk = qw // dh
    bias_d = _bias_vecs(tab, t, nn + 1)
    o_s = _flash(proj, 0, proj, kblk + 2 * g, proj, kblk + 3 * g, bias_d, NSA_HEADS, NSA_GROUP, NSA_GROUP,
                 _causal_pairs(nq, nn), member=member, et=_block_onehot(s, SEL_BLOCK), name="nsa_selected")
    n_back = -(-WINDOW // t)
    bias_w = _bias_vecs(tab, t, n_back + 1, window=WINDOW)
    o_w = _flash(proj, 0, proj, kblk + 4 * g, proj, kblk + 5 * g, bias_w, NSA_HEADS, NSA_GROUP, NSA_GROUP,
                 _window_pairs(nq, n_back), name="nsa_window")
    return o_c, o_s, o_w


def _moba(proj, tab, b, s):
    member = _moba_gate(proj, b, s)
    t = ATT_TILE
    nn = _n_near(t)
    nh = 4
    ng = MOBA_HEADS // nh
    bias_d = _bias_vecs(tab, t, nn + 1)
    return _flash(proj, 0, proj, ng, proj, 2 * ng, bias_d, MOBA_HEADS, 1, nh, _causal_pairs(s // t, nn),
                  member=member, et=_block_onehot(s, MOBA_BLOCK), name="moba_attn")


def _moe(x1, g_ffn, w_group, b_group, w_router, b_router, w_gate, w_up, w_down, g_final):
    t, d = x1.shape
    ng, _, epg = w_router.shape
    w_gr = jnp.concatenate([w_group, jnp.transpose(w_router, (1, 0, 2)).reshape(d, ng * epg),
                            jnp.zeros((d, LANES - ng - ng * epg), F32)], axis=1)
    b_gr = jnp.concatenate([b_group, b_router.reshape(-1), jnp.zeros((LANES - ng - ng * epg,), F32)]).reshape(1, LANES)
    h, info, cnt = _route(x1, g_ffn, w_gr, b_gr)
    n_e = ng * epg
    n_assign = t * EXPERT_TOPK
    n_blocks = -(-(n_assign + n_e * (MOE_ROWS - 1)) // MOE_ROWS)
    counts = cnt[0, :n_e].astype(jnp.int32)
    padded = (counts + MOE_ROWS - 1) // MOE_ROWS * MOE_ROWS
    pad_end = jnp.cumsum(padded)
    pad_start = pad_end - padded
    expert = info[:, 0:EXPERT_TOPK].astype(jnp.int32)
    rank = info[:, 4:4 + EXPERT_TOPK].astype(jnp.int32)
    dest = (pad_start[expert] + rank).reshape(-1)
    block_e = jnp.minimum(jnp.searchsorted(pad_end, jnp.arange(n_blocks) * MOE_ROWS, side='right'),
                          n_e - 1).astype(jnp.int32)
    n_used = (pad_end[-1:] // MOE_ROWS).astype(jnp.int32)
    xs = _dispatch(dest, h, n_blocks * MOE_ROWS)
    ys = _experts(block_e, n_used, xs, w_gate, w_up, w_down)
    return _combine(dest, x1, info, g_final, ys)


def kernel(x, rel_bias, norm_mix, w_in, cmp_pe_k, cmp_w1_k, cmp_w2_k, cmp_pe_v, cmp_w1_v, cmp_w2_v, w_up_nsa,
           w_up_moba, w_out, norm_ffn, w_group, b_group, w_router, b_router, w_exp_gate, w_exp_up, w_exp_down,
           final_norm):
    b, s, d = x.shape
    t = b * s
    depth = w_in.shape[0]
    tab_a = rel_bias[:, :NSA_HEADS]
    tab_b = rel_bias[:, NSA_HEADS:]
    a_cols = NSA_HEADS * HEAD_DIM + 6 * NSA_KV_HEADS * HEAD_DIM
    gate_cols = 3 * NSA_HEADS
    b_cols = 3 * MOBA_HEADS * HEAD_DIM
    xt = x.reshape(t, d)
    out = None
    for l in range(depth):
        h = _rmsnorm(xt, norm_mix[l], BF16)
        wl = w_in[l]
        w_rest = wl[:, a_cols + gate_cols:]
        proj_a = _matmul(h, wl, 0, a_cols, BF16, name="in_proj_a").reshape(b, s, a_cols)
        gate_a = _matmul(h, wl, a_cols, LANES, F32, tn=LANES, name="in_proj_gate")
        proj_b = _matmul(h, w_rest, 0, b_cols, BF16, name="in_proj_b").reshape(b, s, b_cols)
        gm = _matmul(h, w_rest, b_cols, 2 * d, F32, name="in_proj_gm")
        o_c, o_s, o_w = _nsa(proj_a, gate_cols, cmp_pe_k[l], cmp_w1_k[l], cmp_w2_k[l],
                             cmp_pe_v[l], cmp_w1_v[l], cmp_w2_v[l], tab_a, b, s)
        o_b = _moba(proj_b, tab_b, b, s)
        merged = _merge(o_c.reshape(t, -1), o_s.reshape(t, -1), o_w.reshape(t, -1), gate_a,
                        o_b.reshape(t, -1), gm, w_up_nsa[l], w_up_moba[l])
        x1 = _matmul(merged, w_out[l], 0, d, F32, res=xt, name="out_proj")
        assert l == depth - 1, "only the last layer's MoE is fused with the final norm"
        out = _moe(x1, norm_ffn[l], w_group[l], b_group[l], w_router[l], b_router[l],
                   w_exp_gate[l], w_exp_up[l], w_exp_down[l], final_norm)
    return out.reshape(b, s, d)
```

```python
import functools
import math

import numpy as np
import jax
import jax.numpy as jnp
from jax import lax
from jax.experimental import pallas as pl
from jax.experimental.pallas import tpu as pltpu

F32 = jnp.float32
BF16 = jnp.bfloat16

HEAD_DIM = 128
NSA_HEADS = 8
NSA_KV_HEADS = 2
NSA_GROUP = NSA_HEADS // NSA_KV_HEADS
CMP_BLOCK = 32
CMP_STRIDE = 16
SEL_BLOCK = 64
SEL_TOPN = 16
WINDOW = 512
FORCED_SCORE = 1e4
MOBA_HEADS = 8
MOBA_BLOCK = 256
MOBA_TOPK = 3
REL_BUCKETS = 32
REL_MAX_DIST = 128
N_GROUPS = 8
EXPERTS_PER_GROUP = 8
N_EXPERTS = N_GROUPS * EXPERTS_PER_GROUP
EXPERT_TOPK = 2
RMS_EPS = 1e-6

LANES = 128
ATT_TILE = 512
MOE_ROWS = 256
MASK_BIG = 1e30
M_INIT = -3e38
LOG2E = math.log2(math.e)
Q_SCALE = HEAD_DIM ** -0.5 * LOG2E
VMEM_LIMIT = 48 * 1024 * 1024


def _cparams(sem, vmem=VMEM_LIMIT, flags=None):
    return pltpu.CompilerParams(dimension_semantics=sem, vmem_limit_bytes=vmem, flags=flags)


def _rmsnorm_body(x_ref, g_ref, o_ref):
    x = x_ref[...]
    ms = jnp.mean(x * x, axis=-1, keepdims=True)
    o_ref[...] = (x * lax.rsqrt(ms + RMS_EPS) * g_ref[...]).astype(o_ref.dtype)


def _rmsnorm(x, g, out_dtype, tm=512):
    t, d = x.shape
    return pl.pallas_call(
        _rmsnorm_body,
        grid=(t // tm,),
        in_specs=[pl.BlockSpec((tm, d), lambda i: (i, 0)),
                  pl.BlockSpec((1, d), lambda i: (0, 0))],
        out_specs=pl.BlockSpec((tm, d), lambda i: (i, 0)),
        out_shape=jax.ShapeDtypeStruct((t, d), out_dtype),
        compiler_params=_cparams(("parallel",)),
        name="rmsnorm",
    )(x, g.reshape(1, d))


def _mm_body(*refs, has_res, n_scaled, col_scale, w_transposed):
    if has_res:
        a_ref, w_ref, r_ref, o_ref = refs
    else:
        a_ref, w_ref, o_ref = refs
    w = w_ref[...].astype(BF16)
    if w_transposed:
        acc = lax.dot_general(a_ref[...], w, (((1,), (1,)), ((), ())), preferred_element_type=F32)
    else:
        acc = jnp.dot(a_ref[...], w, preferred_element_type=F32)
    if n_scaled:
        acc = acc * jnp.where(pl.program_id(1) < n_scaled, col_scale, 1.0)
    if has_res:
        acc = acc + r_ref[...]
    o_ref[...] = acc.astype(o_ref.dtype)


def _matmul(a, w, col0, ncols, out_dtype, res=None, scaled_cols=0, col_scale=1.0, w_transposed=False,
            tm=1024, tn=512, name="matmul"):
    t, k = a.shape
    tn = min(tn, ncols)
    assert ncols % tn == 0 and t % tm == 0 and scaled_cols % tn == 0
    if w_transposed:
        assert col0 % 8 == 0
        w_spec = pl.BlockSpec((pl.Element(tn), pl.Element(k)),
                              lambda i, j: (pl.multiple_of(col0 + j * tn, 8), 0))
    else:
        assert col0 % tn == 0
        off = col0 // tn
        w_spec = pl.BlockSpec((k, tn), lambda i, j: (0, j + off))
    in_specs = [pl.BlockSpec((tm, k), lambda i, j: (i, 0)), w_spec]
    args = [a, w]
    if res is not None:
        in_specs.append(pl.BlockSpec((tm, tn), lambda i, j: (i, j)))
        args.append(res)
    return pl.pallas_call(
        functools.partial(_mm_body, has_res=res is not None, n_scaled=scaled_cols // tn, col_scale=col_scale,
                          w_transposed=w_transposed),
        grid=(t // tm, ncols // tn),
        in_specs=in_specs,
        out_specs=pl.BlockSpec((tm, tn), lambda i, j: (i, j)),
        out_shape=jax.ShapeDtypeStruct((t, ncols), out_dtype),
        compiler_params=_cparams(("parallel", "parallel")),
        name=name,
    )(*args)


def _compress_body(u_ref, pe_ref, w1_ref, w2_ref, o_ref, *, nc):
    u = u_ref[...]
    w1 = w1_ref[...].astype(BF16)
    half = u.shape[1]
    a = jnp.dot(u, w1[:half], preferred_element_type=F32)
    b = jnp.dot(u, w1[half:], preferred_element_type=F32)
    peb = jnp.dot(pe_ref[...].astype(BF16), w1, preferred_element_type=F32)[0:1]
    pre = a + pltpu.roll(b, nc - 1, 0) + peb
    hid = jax.nn.gelu(pre)
    o_ref[...] = jnp.dot(hid.astype(BF16), w2_ref[...].astype(BF16),
                         preferred_element_type=F32).astype(o_ref.dtype)


def _compress(u, pe, w1, w2):
    bg, nc, kk = u.shape
    hid = w1.shape[1]
    dh = w2.shape[1]
    pe8 = jnp.broadcast_to(pe.reshape(1, -1), (16, pe.size))
    return pl.pallas_call(
        functools.partial(_compress_body, nc=nc),
        grid=(bg,),
        in_specs=[pl.BlockSpec((None, nc, kk), lambda i: (i, 0, 0)),
                  pl.BlockSpec((16, 2 * kk), lambda i: (0, 0)),
                  pl.BlockSpec((2 * kk, hid), lambda i: (0, 0)),
                  pl.BlockSpec((hid, dh), lambda i: (0, 0))],
        out_specs=pl.BlockSpec((None, nc, dh), lambda i: (i, 0, 0)),
        out_shape=jax.ShapeDtypeStruct((bg, nc, dh), BF16),
        compiler_params=_cparams(("parallel",)),
        name="nsa_compress",
    )(u, pe8, w1, w2)


def _split3(x):
    p1 = x.astype(BF16)
    r = x - p1.astype(F32)
    p2 = r.astype(BF16)
    p3 = (r - p2.astype(F32)).astype(BF16)
    return p1, p2, p3


def _rank_count(score, n_rows):
    groups = []
    for g0 in range(0, n_rows, 8):
        sg = score[g0:min(g0 + 8, n_rows), :]
        n_iota = g0 + lax.broadcasted_iota(jnp.int32, sg.shape, 0)
        cnt = jnp.zeros(sg.shape, F32)
        for m in range(n_rows):
            row = score[m:m + 1, :]
            if m < g0:
                beats = row >= sg
            elif m >= g0 + 8:
                beats = row > sg
            else:
                tie = jnp.where(n_iota > m, 1.0, 0.0)
                beats = jnp.where(row > sg, 1.0, jnp.where(row == sg, tie, 0.0)) > 0.5
            cnt = cnt + jnp.where(beats, 1.0, 0.0)
        groups.append(cnt)
    return jnp.concatenate(groups, axis=0) if len(groups) > 1 else groups[0]


def _nsa_cmp_body(q_ref, kc_ref, vc_ref, bias_ref, ov_ref, oc_ref, mem_ref, *, tq, nc, n_sel):
    t0 = pl.program_id(2) * tq
    kc = kc_ref[...]
    vc = vc_ref[...]
    t_idx = t0 + lax.broadcasted_iota(jnp.int32, (tq, nc), 0)
    c_idx = lax.broadcasted_iota(jnp.int32, (tq, nc), 1)
    dist = t_idx - (c_idx * CMP_STRIDE + (CMP_BLOCK - 1))
    valid = dist >= 0
    n_k = REL_MAX_DIST // CMP_STRIDE
    kk = dist // CMP_STRIDE
    k_is = [kk == k for k in range(n_k)]
    psum = jnp.zeros((tq, nc), F32)
    for j in range(NSA_GROUP):
        hs = slice(j * HEAD_DIM, (j + 1) * HEAD_DIM)
        gt = bias_ref[j] * LOG2E
        bias = jnp.broadcast_to(gt[:, n_k:n_k + 1], (tq, nc))
        for k in range(n_k):
            bias = jnp.where(k_is[k], gt[:, k:k + 1], bias)
        s = lax.dot_general(q_ref[:, hs], kc, (((1,), (1,)), ((), ())), preferred_element_type=F32) + bias
        s = jnp.where(valid, s, -MASK_BIG)
        m = jnp.max(s, axis=-1, keepdims=True)
        m = jnp.where(m > -0.5 * MASK_BIG, m, 0.0)
        p = jnp.where(valid, jnp.exp2(s - m), 0.0)
        d = jnp.sum(p, axis=-1, keepdims=True)
        p = p / jnp.where(d > 0, d, 1.0)
        oc_ref[:, hs] = jnp.dot(p.astype(BF16), vc, preferred_element_type=F32).astype(oc_ref.dtype)
        psum = psum + p
    ov = ov_ref[...]
    nt = (((1,), (1,)), ((), ()))
    p1, p2, p3 = _split3(psum)
    psel = (lax.dot_general(ov, p1, nt, preferred_element_type=F32)
            + lax.dot_general(ov, p2, nt, preferred_element_type=F32)
            + lax.dot_general(ov, p3, nt, preferred_element_type=F32))
    n_idx = lax.broadcasted_iota(jnp.int32, (n_sel, tq), 0)
    tt = t0 + lax.broadcasted_iota(jnp.int32, (n_sel, tq), 1)
    cur = tt // SEL_BLOCK
    forced = jnp.where(n_idx == 0, 1.0, jnp.where(n_idx == cur, 1.0, jnp.where(n_idx == cur - 1, 1.0, 0.0)))
    score = jnp.where(forced > 0.5, FORCED_SCORE, jnp.where(n_idx * SEL_BLOCK <= tt, psel, -1.0))
    cnt = _rank_count(score, n_sel)
    member = jnp.where(cnt < float(min(SEL_TOPN, n_sel)), 1.0, 0.0)
    if n_sel < LANES:
        member = jnp.concatenate([member, jnp.zeros((LANES - n_sel, tq), F32)], axis=0)
    mem_ref[...] = member.T.astype(mem_ref.dtype)


def _nsa_cmp(proj, kc, vc, tab, overlap, b, s, tq=256):
    g = NSA_KV_HEADS
    nc = kc.shape[2]
    n_sel = s // SEL_BLOCK
    gw = NSA_GROUP * HEAD_DIM
    assert tq % CMP_STRIDE == 0 and REL_MAX_DIST % CMP_STRIDE == 0
    n_k = REL_MAX_DIST // CMP_STRIDE
    rho = (np.arange(tq)[:, None] - (CMP_BLOCK - 1)) % CMP_STRIDE
    dd = np.concatenate([rho + CMP_STRIDE * np.arange(n_k)[None, :], np.full((tq, 1), REL_MAX_DIST)], axis=1)
    gtab = jnp.transpose(tab[_rel_bucket(dd)], (2, 0, 1)).astype(F32)
    bias_c = jnp.pad(gtab, ((0, 0), (0, 0), (0, LANES - n_k - 1)))
    body = functools.partial(_nsa_cmp_body, tq=tq, nc=nc, n_sel=n_sel)
    return pl.pallas_call(
        body,
        grid=(b, g, s // tq),
        in_specs=[pl.BlockSpec((None, tq, gw), lambda bi, gi, i: (bi, i, gi)),
                  pl.BlockSpec((None, None, nc, HEAD_DIM), lambda bi, gi, i: (bi, gi, 0, 0)),
                  pl.BlockSpec((None, None, nc, HEAD_DIM), lambda bi, gi, i: (bi, gi, 0, 0)),
                  pl.BlockSpec((NSA_GROUP, tq, LANES), lambda bi, gi, i: (gi, 0, 0)),
                  pl.BlockSpec((n_sel, nc), lambda bi, gi, i: (0, 0))],
        out_specs=[pl.BlockSpec((None, tq, gw), lambda bi, gi, i: (bi, i, gi)),
                   pl.BlockSpec((None, None, tq, LANES), lambda bi, gi, i: (bi, gi, i, 0))],
        out_shape=[jax.ShapeDtypeStruct((b, s, NSA_HEADS * HEAD_DIM), BF16),
                   jax.ShapeDtypeStruct((b, g, s, LANES), BF16)],
        compiler_params=_cparams(("parallel", "parallel", "parallel")),
        name="nsa_cmp_select",
    )(proj, kc, vc, bias_c, overlap)


def _moba_gate_body(q_ref, k_ref, mem_ref, *, s, nblk):
    k = k_ref[...].astype(F32)
    kmean = jnp.mean(k.reshape(nblk, MOBA_BLOCK, HEAD_DIM), axis=1)
    k1 = kmean.astype(BF16)
    k2 = (kmean - k1.astype(F32)).astype(BF16)
    q = q_ref[...]
    nt = (((1,), (1,)), ((), ()))
    gate = (lax.dot_general(k1, q, nt, preferred_element_type=F32)
            + lax.dot_general(k2, q, nt, preferred_element_type=F32))
    n_idx = lax.broadcasted_iota(jnp.int32, (nblk, s), 0)
    own = lax.broadcasted_iota(jnp.int32, (nblk, s), 1) // MOBA_BLOCK
    past = n_idx < own
    score = jnp.where(past, gate, -MASK_BIG)
    cnt = _rank_count(score, nblk)
    n_top = max(1, min(MOBA_TOPK, nblk - 1))
    sel = jnp.where(past, jnp.where(cnt < float(n_top), 1.0, 0.0), 0.0)
    member = jnp.where(n_idx == own, 1.0, sel)
    member = jnp.concatenate([member, jnp.zeros((LANES - nblk, s), F32)], axis=0)
    mem_ref[...] = member.T.astype(mem_ref.dtype)


def _moba_gate(proj, b, s):
    h = MOBA_HEADS
    nblk = s // MOBA_BLOCK
    return pl.pallas_call(
        functools.partial(_moba_gate_body, s=s, nblk=nblk),
        grid=(b, h),
        in_specs=[pl.BlockSpec((None, s, HEAD_DIM), lambda bi, hi: (bi, 0, hi)),
                  pl.BlockSpec((None, s, HEAD_DIM), lambda bi, hi: (bi, 0, h + hi))],
        out_specs=pl.BlockSpec((None, None, s, LANES), lambda bi, hi: (bi, hi, 0, 0)),
        out_shape=jax.ShapeDtypeStruct((b, h, s, LANES), BF16),
        compiler_params=_cparams(("parallel", "parallel")),
        name="moba_gate",
    )(proj, proj)


def _flash_body(qi_ref, ki_ref, bo_ref, fl_ref, *refs, nh, ratio, nm, n_near, has_far):
    if nm:
        (q_ref, k_ref, v_ref, bvec_ref, mem_ref, et_ref, o_ref,
         m_ref, l_ref, acc_ref, sh_ref, al_ref, bias_ref, s_ref, p_ref) = refs
    else:
        q_ref, k_ref, v_ref, bvec_ref, o_ref, m_ref, l_ref, acc_ref, sh_ref, al_ref, bias_ref, s_ref, p_ref = refs
    del qi_ref, ki_ref
    p = pl.program_id(2)
    flag = fl_ref[p]
    bo = bo_ref[p]
    t = q_ref.shape[0]
    rows = 64

    @pl.when(p == 0)
    def _():
        for h in range(nh):
            for o in range(n_near):
                vec = bvec_ref[h, o][0:1, :] * LOG2E
                for rc in range(t // rows):
                    x = pltpu.roll(jnp.broadcast_to(vec, (rows, 2 * t)), rc * rows, 1, stride=1, stride_axis=0)
                    bias_ref[h, o, rc * rows:(rc + 1) * rows, :] = x[:, :t]

    @pl.when((flag & 1) != 0)
    def _():
        m_ref[...] = jnp.full(m_ref.shape, M_INIT, F32)
        l_ref[...] = jnp.zeros(l_ref.shape, F32)
        acc_ref[...] = jnp.zeros(acc_ref.shape, F32)

    nt = (((1,), (1,)), ((), ()))
    reps = t // LANES

    def step(near):
        for h in range(nh):
            hs = slice(h * HEAD_DIM, (h + 1) * HEAD_DIM)
            kv = h // ratio
            ks = slice(kv * HEAD_DIM, (kv + 1) * HEAD_DIM)
            q = q_ref[:, hs]
            k = k_ref[:, ks]
            if nm:
                mneg = mem_ref[h // (nh // nm)] - 1.0
                q = jnp.concatenate([q, mneg.astype(BF16)], axis=1)
                k = jnp.concatenate([k, et_ref[...]], axis=1)
            sc = lax.dot_general(q, k, nt, preferred_element_type=F32)
            m_prev = m_ref[h]
            if near:
                sc = sc + bias_ref[h, bo]
                m_new = jnp.maximum(m_prev, jnp.max(sc, axis=-1, keepdims=True))
                sh_ref[h] = m_new
            else:
                cfar = bvec_ref[h, n_near][0:1, 0:LANES] * LOG2E
                m_new = jnp.maximum(m_prev, jnp.max(sc, axis=-1, keepdims=True) + cfar)
                sh_ref[h] = m_new - cfar
            s_ref[h] = sc
            al_ref[h] = jnp.exp2(m_prev - m_new)
            m_ref[h] = m_new
        for h in range(nh):
            ks = slice(h // ratio * HEAD_DIM, (h // ratio + 1) * HEAD_DIM)
            for rc in range(t // rows):
                rs = slice(rc * rows, (rc + 1) * rows)
                pm = jnp.exp2(s_ref[h, rs, :] - jnp.tile(sh_ref[h, rs, :], (1, reps)))
                l_ref[h, rs, :] = al_ref[h, rs, :] * l_ref[h, rs, :] + jnp.sum(pm, axis=-1, keepdims=True)
                p_ref[h, rs, :] = pm.astype(BF16)
            acc_ref[h] = al_ref[h] * acc_ref[h] + jnp.dot(p_ref[h], v_ref[:, ks], preferred_element_type=F32)

    if has_far:
        pl.when(bo < n_near)(lambda: step(True))
        pl.when(bo >= n_near)(lambda: step(False))
    else:
        step(True)

    @pl.when((flag & 2) != 0)
    def _():
        for h in range(nh):
            l = l_ref[h]
            o_ref[:, h * HEAD_DIM:(h + 1) * HEAD_DIM] = (acc_ref[h] / jnp.where(l > 0, l, 1.0)).astype(o_ref.dtype)


def _flash(q_arr, q_off, k_arr, k_off, v_arr, v_off, bias, n_heads, ratio, nh, pairs, has_far, member=None,
           et=None, name="flash"):
    b, s, _ = q_arr.shape
    t = ATT_TILE
    nkv = nh // ratio
    ng = n_heads // nh
    qi = jnp.asarray([p[0] for p in pairs], jnp.int32)
    ki = jnp.asarray([p[1] for p in pairs], jnp.int32)
    bo = jnp.asarray([p[2] for p in pairs], jnp.int32)
    fl = jnp.asarray([p[3] for p in pairs], jnp.int32)
    nb = bias.shape[1]
    nm = 0
    in_specs = [
        pl.BlockSpec((None, t, nh * HEAD_DIM), lambda bi, gi, p, qi, ki, bo, fl: (bi, qi[p], q_off + gi)),
        pl.BlockSpec((None, t, nkv * HEAD_DIM), lambda bi, gi, p, qi, ki, bo, fl: (bi, ki[p], k_off + gi)),
        pl.BlockSpec((None, t, nkv * HEAD_DIM), lambda bi, gi, p, qi, ki, bo, fl: (bi, ki[p], v_off + gi)),
        pl.BlockSpec((nh, nb, 8, 2 * t), lambda bi, gi, p, qi, ki, bo, fl: (gi, 0, 0, 0)),
    ]
    args = [q_arr, k_arr, v_arr, bias]
    if member is not None:
        nm = member.shape[1] // ng
        in_specs += [
            pl.BlockSpec((None, nm, t, LANES), lambda bi, gi, p, qi, ki, bo, fl: (bi, gi, qi[p], 0)),
            pl.BlockSpec((t, LANES), lambda bi, gi, p, qi, ki, bo, fl: (ki[p], 0)),
        ]
        args += [member, et]
    n_near = nb - 1 if has_far else nb
    body = functools.partial(_flash_body, nh=nh, ratio=ratio, nm=nm, n_near=n_near, has_far=has_far)
    return pl.pallas_call(
        body,
        grid_spec=pltpu.PrefetchScalarGridSpec(
            num_scalar_prefetch=4,
            grid=(b, ng, len(pairs)),
            in_specs=in_specs,
            out_specs=pl.BlockSpec((None, t, nh * HEAD_DIM), lambda bi, gi, p, qi, ki, bo, fl: (bi, qi[p], gi)),
            scratch_shapes=[pltpu.VMEM((nh, t, LANES), F32)] * 5 + [pltpu.VMEM((nh, n_near, t, t), F32),
                                                                    pltpu.VMEM((nh, t, t), F32),
                                                                    pltpu.VMEM((nh, t, t), BF16)],
        ),
        out_shape=jax.ShapeDtypeStruct((b, s, n_heads * HEAD_DIM), BF16),
        compiler_params=_cparams(("parallel", "parallel", "arbitrary")),
        name=name,
    )(qi, ki, bo, fl, *args)


def _rel_bucket(dist):
    n = jnp.maximum(jnp.asarray(dist, jnp.int32), 0)
    max_exact = REL_BUCKETS // 2
    nf = jnp.maximum(n, 1).astype(jnp.float32)
    large = max_exact + (jnp.log(nf / max_exact) / math.log(REL_MAX_DIST / max_exact)
                         * (REL_BUCKETS - max_exact)).astype(jnp.int32)
    return jnp.where(n < max_exact, n, jnp.minimum(large, REL_BUCKETS - 1))


def _n_near(t):
    return -(-(REL_MAX_DIST - 1 + t) // t)


def _bias_vecs(tab, t, n_off, window=None):
    k = np.arange(2 * t)[None, :]
    dist = np.arange(n_off)[:, None] * t + np.where(k < t, -k, 2 * t - k)
    ok = dist >= 0
    if window is not None:
        ok &= dist < window
    bias = jnp.where(jnp.asarray(ok)[..., None], tab[_rel_bucket(dist)], -MASK_BIG)
    bias = jnp.transpose(bias, (2, 0, 1)).astype(F32)
    return jnp.broadcast_to(bias[:, :, None, :], (bias.shape[0], n_off, 8, 2 * t))


def _causal_pairs(nq, n_near):
    pairs = []
    for qi in range(nq):
        for ki in range(qi + 1):
            pairs.append((qi, ki, min(qi - ki, n_near), (1 if ki == 0 else 0) | (2 if ki == qi else 0)))
    return pairs


def _window_pairs(nq, n_back):
    pairs = []
    for qi in range(nq):
        lo = max(0, qi - n_back)
        for ki in range(lo, qi + 1):
            pairs.append((qi, ki, qi - ki, (1 if ki == lo else 0) | (2 if ki == qi else 0)))
    return pairs


def _block_onehot(s, blk):
    return jnp.asarray(np.where(np.arange(s)[:, None] // blk == np.arange(LANES)[None, :], MASK_BIG, 0.0), BF16)


def _merge_body(oc_ref, os_ref, ow_ref, gl_ref, ob_ref, gma_ref, gmb_ref, wa_ref, wb_ref, o_ref, oa_ref):
    @pl.when(pl.program_id(1) == 0)
    def _():
        gates = jax.nn.sigmoid(gl_ref[...])
        for h in range(NSA_HEADS):
            hs = slice(h * HEAD_DIM, (h + 1) * HEAD_DIM)
            mix = (gates[:, 3 * h:3 * h + 1] * oc_ref[:, hs].astype(F32)
                   + gates[:, 3 * h + 1:3 * h + 2] * os_ref[:, hs].astype(F32)
                   + gates[:, 3 * h + 2:3 * h + 3] * ow_ref[:, hs].astype(F32))
            oa_ref[:, hs] = mix.astype(BF16)

    ya = jnp.dot(oa_ref[...], wa_ref[...].astype(BF16), preferred_element_type=F32)
    yb = jnp.dot(ob_ref[...], wb_ref[...].astype(BF16), preferred_element_type=F32)
    o_ref[...] = (jax.nn.sigmoid(gma_ref[...]) * ya + jax.nn.sigmoid(gmb_ref[...]) * yb).astype(o_ref.dtype)


def _merge(o_c, o_s, o_w, gate_logits, o_b, gm, w_up_a, w_up_b, tm=512, tn=512):
    t, ka = o_c.shape
    kb = o_b.shape[1]
    d = w_up_a.shape[1]
    nj = d // tn
    row = lambda i, j: (i, 0)
    return pl.pallas_call(
        _merge_body,
        grid=(t // tm, nj),
        in_specs=[pl.BlockSpec((tm, ka), row), pl.BlockSpec((tm, ka), row), pl.BlockSpec((tm, ka), row),
                  pl.BlockSpec((tm, LANES), row), pl.BlockSpec((tm, kb), row),
                  pl.BlockSpec((tm, tn), lambda i, j: (i, j)),
                  pl.BlockSpec((tm, tn), lambda i, j: (i, j + nj)),
                  pl.BlockSpec((ka, tn), lambda i, j: (0, j)),
                  pl.BlockSpec((kb, tn), lambda i, j: (0, j))],
        out_specs=pl.BlockSpec((tm, tn), lambda i, j: (i, j)),
        out_shape=jax.ShapeDtypeStruct((t, d), BF16),
        scratch_shapes=[pltpu.VMEM((tm, ka), BF16)],
        compiler_params=_cparams(("parallel", "arbitrary")),
        name="merge_up",
    )(o_c, o_s, o_w, gate_logits, o_b, gm, gm, w_up_a, w_up_b)


def _route_body(x_ref, g_ref, w_ref, b_ref, h_ref, info_ref, cnt_ref, carry_ref, *, tm):
    @pl.when(pl.program_id(0) == 0)
    def _():
        carry_ref[...] = jnp.zeros(carry_ref.shape, F32)

    x = x_ref[...]
    ms = jnp.mean(x * x, axis=-1, keepdims=True)
    h = x * lax.rsqrt(ms + RMS_EPS) * g_ref[...]
    h_ref[...] = h
    w = w_ref[...]
    h1 = h.astype(BF16)
    h2 = (h - h1.astype(F32)).astype(BF16)
    w1 = w.astype(BF16)
    w2 = (w - w1.astype(F32)).astype(BF16)
    logits = (jnp.dot(h1, w1, preferred_element_type=F32) + jnp.dot(h1, w2, preferred_element_type=F32)
              + jnp.dot(h2, w1, preferred_element_type=F32)) + b_ref[...]
    lane = lax.broadcasted_iota(jnp.int32, (tm, LANES), 1)
    lanef = lane.astype(F32)

    is_g = lane < N_GROUPS
    gl = jnp.where(is_g, logits, -MASK_BIG)
    ge = jnp.where(is_g, jnp.exp(gl - jnp.max(gl, axis=-1, keepdims=True)), 0.0)
    gp = ge / jnp.sum(ge, axis=-1, keepdims=True)
    g_val = jnp.max(gp, axis=-1, keepdims=True)
    g_idx = jnp.min(jnp.where(gp == g_val, lanef, float(LANES)), axis=-1, keepdims=True)

    lane_grp = ((lane - N_GROUPS) // EXPERTS_PER_GROUP).astype(F32)
    in_e = jnp.where(lane >= N_GROUPS, jnp.where(lane < N_GROUPS + N_EXPERTS, 1.0, 0.0), 0.0)
    is_e = jnp.where(lane_grp == g_idx, in_e, 0.0) > 0.5
    el = jnp.where(is_e, logits, -MASK_BIG)
    ee = jnp.where(is_e, jnp.exp(el - jnp.max(el, axis=-1, keepdims=True)), 0.0)
    ep = jnp.where(is_e, ee / jnp.sum(ee, axis=-1, keepdims=True), -1.0)
    v1 = jnp.max(ep, axis=-1, keepdims=True)
    l1 = jnp.min(jnp.where(ep == v1, lanef, float(LANES)), axis=-1, keepdims=True)
    ep2 = jnp.where(lanef == l1, -1.0, ep)
    v2 = jnp.max(ep2, axis=-1, keepdims=True)
    l2 = jnp.min(jnp.where(ep2 == v2, lanef, float(LANES)), axis=-1, keepdims=True)
    vs = v1 + v2
    wt1 = g_val * v1 / vs
    wt2 = g_val * v2 / vs
    e1 = l1 - float(N_GROUPS)
    e2 = l2 - float(N_GROUPS)

    oh = jnp.where(lanef == e1, 1.0, jnp.where(lanef == e2, 1.0, 0.0))
    r_i = lax.broadcasted_iota(jnp.int32, (tm, tm), 0)
    c_i = lax.broadcasted_iota(jnp.int32, (tm, tm), 1)
    tri = jnp.where(r_i > c_i, 1.0, 0.0).astype(BF16)
    base = jnp.dot(tri, oh.astype(BF16), preferred_element_type=F32) + carry_ref[...]
    r1 = jnp.sum(jnp.where(lanef == e1, base, 0.0), axis=-1, keepdims=True)
    r2 = jnp.sum(jnp.where(lanef == e2, base, 0.0), axis=-1, keepdims=True)
    carry_ref[...] = carry_ref[...] + jnp.sum(oh, axis=0, keepdims=True)
    cnt_ref[...] = jnp.broadcast_to(carry_ref[...], cnt_ref.shape)
    info = jnp.where(lane == 0, e1, jnp.where(lane == 1, e2, jnp.where(lane == 2, wt1, jnp.where(
        lane == 3, wt2, jnp.where(lane == 4, r1, jnp.where(lane == 5, r2, 0.0))))))
    info_ref[...] = info


def _route(x1, g, w_gr, b_gr, tm=512):
    t, d = x1.shape
    return pl.pallas_call(
        functools.partial(_route_body, tm=tm),
        grid=(t // tm,),
        in_specs=[pl.BlockSpec((tm, d), lambda i: (i, 0)),
                  pl.BlockSpec((1, d), lambda i: (0, 0)),
                  pl.BlockSpec((d, LANES), lambda i: (0, 0)),
                  pl.BlockSpec((1, LANES), lambda i: (0, 0))],
        out_specs=[pl.BlockSpec((tm, d), lambda i: (i, 0)),
                   pl.BlockSpec((tm, LANES), lambda i: (i, 0)),
                   pl.BlockSpec((8, LANES), lambda i: (0, 0))],
        out_shape=[jax.ShapeDtypeStruct((t, d), F32),
                   jax.ShapeDtypeStruct((t, LANES), F32),
                   jax.ShapeDtypeStruct((8, LANES), F32)],
        scratch_shapes=[pltpu.VMEM((1, LANES), F32)],
        compiler_params=_cparams(("arbitrary",)),
        name="moe_route",
    )(x1, g.reshape(1, d), w_gr, b_gr)


def _row_copy(src_ref, src_row, dst_ref, dst_row, sem):
    return pltpu.make_async_copy(src_ref.at[pl.ds(src_row, 1)], dst_ref.at[pl.ds(dst_row, 1)], sem)


def _dispatch_body(dest_ref, h_ref, xs_in_ref, xs_ref, sem, *, tm):
    del xs_in_ref
    base = pl.program_id(0) * tm

    def issue(r, c):
        for k in range(EXPERT_TOPK):
            _row_copy(h_ref, r, xs_ref, dest_ref[EXPERT_TOPK * (base + r) + k], sem).start()
        return c

    lax.fori_loop(0, tm, issue, 0)

    def drain(r, c):
        for k in range(EXPERT_TOPK):
            _row_copy(h_ref, r, xs_ref, dest_ref[EXPERT_TOPK * (base + r) + k], sem).wait()
        return c

    lax.fori_loop(0, tm, drain, 0)


def _dispatch(dest, h, n_rows, tm=512):
    t, d = h.shape
    xs0 = jnp.zeros((n_rows, d), h.dtype)
    return pl.pallas_call(
        functools.partial(_dispatch_body, tm=tm),
        grid_spec=pltpu.PrefetchScalarGridSpec(
            num_scalar_prefetch=1,
            grid=(t // tm,),
            in_specs=[pl.BlockSpec((tm, d), lambda i, dest: (i, 0)),
                      pl.BlockSpec(memory_space=pl.ANY)],
            out_specs=pl.BlockSpec(memory_space=pl.ANY),
            scratch_shapes=[pltpu.SemaphoreType.DMA(())],
        ),
        out_shape=jax.ShapeDtypeStruct((n_rows, d), h.dtype),
        input_output_aliases={2: 0},
        compiler_params=_cparams(("arbitrary",)),
        name="moe_dispatch",
    )(dest, h, xs0)


def _expert_body(be_ref, nu_ref, x_ref, wg_ref, wu_ref, wd_ref, y_ref, wg_s, wu_s, wd_s):
    i = pl.program_id(0)
    e = be_ref[i]
    prev = be_ref[jnp.maximum(i - 1, 0)]

    @pl.when((i == 0) | (e != prev))
    def _():
        wg_s[...] = wg_ref[...].astype(BF16)
        wu_s[...] = wu_ref[...].astype(BF16)
        wd_s[...] = wd_ref[...].astype(BF16)

    @pl.when(i < nu_ref[0])
    def _():
        x = x_ref[...].astype(BF16)
        g = jnp.dot(x, wg_s[...], preferred_element_type=F32)
        u = jnp.dot(x, wu_s[...], preferred_element_type=F32)
        mid = (jax.nn.silu(g) * u).astype(BF16)
        y_ref[...] = jnp.dot(mid, wd_s[...], preferred_element_type=F32)

    @pl.when(i >= nu_ref[0])
    def _():
        y_ref[...] = jnp.zeros(y_ref.shape, y_ref.dtype)


def _experts(block_e, n_used, xs, w_gate, w_up, w_down):
    n_rows, d = xs.shape
    n_blocks = n_rows // MOE_ROWS
    ff = w_gate.shape[2]
    blk = lambda i, be, nu: (jnp.minimum(i, nu[0] - 1), 0)
    wsel = lambda i, be, nu: (be[i], 0, 0)
    return pl.pallas_call(
        _expert_body,
        grid_spec=pltpu.PrefetchScalarGridSpec(
            num_scalar_prefetch=2,
            grid=(n_blocks,),
            in_specs=[pl.BlockSpec((MOE_ROWS, d), blk),
                      pl.BlockSpec((None, d, ff), wsel),
                      pl.BlockSpec((None, d, ff), wsel),
                      pl.BlockSpec((None, ff, d), wsel)],
            out_specs=pl.BlockSpec((MOE_ROWS, d), lambda i, be, nu: (i, 0)),
            scratch_shapes=[pltpu.VMEM((d, ff), BF16), pltpu.VMEM((d, ff), BF16), pltpu.VMEM((ff, d), BF16)],
        ),
        out_shape=jax.ShapeDtypeStruct((n_rows, d), F32),
        compiler_params=_cparams(("arbitrary",)),
        name="moe_experts",
    )(block_e, n_used, xs, w_gate, w_up, w_down)


def _combine_body(dest_ref, x_ref, info_ref, g_ref, ys_ref, o_ref, buf0, buf1, sem, *, tm):
    base = pl.program_id(0) * tm
    bufs = (buf0, buf1)

    def issue(r, c):
        for k in range(EXPERT_TOPK):
            _row_copy(ys_ref, dest_ref[EXPERT_TOPK * (base + r) + k], bufs[k], r, sem).start()
        return c

    lax.fori_loop(0, tm, issue, 0)

    def drain(r, c):
        for k in range(EXPERT_TOPK):
            _row_copy(ys_ref, dest_ref[EXPERT_TOPK * (base + r) + k], bufs[k], r, sem).wait()
        return c

    lax.fori_loop(0, tm, drain, 0)
    info = info_ref[...]
    y = x_ref[...] + (info[:, 2:3] * buf0[...] + info[:, 3:4] * buf1[...])
    ms = jnp.mean(y * y, axis=-1, keepdims=True)
    o_ref[...] = y * lax.rsqrt(ms + RMS_EPS) * g_ref[...]


def _combine(dest, x1, info, g, ys, tm=256):
    t, d = x1.shape
    return pl.pallas_call(
        functools.partial(_combine_body, tm=tm),
        grid_spec=pltpu.PrefetchScalarGridSpec(
            num_scalar_prefetch=1,
            grid=(t // tm,),
            in_specs=[pl.BlockSpec((tm, d), lambda i, dest: (i, 0)),
                      pl.BlockSpec((tm, LANES), lambda i, dest: (i, 0)),
                      pl.BlockSpec((1, d), lambda i, dest: (0, 0)),
                      pl.BlockSpec(memory_space=pl.ANY)],
            out_specs=pl.BlockSpec((tm, d), lambda i, dest: (i, 0)),
            scratch_shapes=[pltpu.VMEM((tm, d), F32), pltpu.VMEM((tm, d), F32), pltpu.SemaphoreType.DMA(())],
        ),
        out_shape=jax.ShapeDtypeStruct((t, d), F32),
        compiler_params=_cparams(("arbitrary",)),
        name="moe_combine",
    )(dest, x1, info, g.reshape(1, d), ys)


def _nsa(proj, gate_cols, pe_k, w1_k, w2_k, pe_v, w1_v, w2_v, tab, b, s):
    del gate_cols
    g, dh = NSA_KV_HEADS, HEAD_DIM
    qw = NSA_HEADS * dh
    nc = s // CMP_STRIDE

    def blocks16(col0):
        a = proj[:, :, col0:col0 + g * dh].reshape(b, nc, CMP_STRIDE, g, dh)
        return a.transpose(0, 3, 1, 2, 4).reshape(b * g, nc, CMP_STRIDE * dh)

    kc = _compress(blocks16(qw), pe_k, w1_k, w2_k).reshape(b, g, nc, dh)
    vc = _compress(blocks16(qw + g * dh), pe_v, w1_v, w2_v).reshape(b, g, nc, dh)

    c_start = np.arange(nc)[None, :] * CMP_STRIDE
    n_sel = s // SEL_BLOCK
    sb = np.arange(n_sel)[:, None] * SEL_BLOCK
    overlap = jnp.asarray((c_start < sb + SEL_BLOCK) & (c_start + CMP_BLOCK > sb), BF16)
    o_c, member = _nsa_cmp(proj, kc, vc, tab, overlap, b, s)

    t = ATT_TILE
    nq = s // t
    nn = _n_near(t)
    kblk = qw // dh
    bias_d = _bias_vecs(tab, t, nn + 1)
    o_s = _flash(proj, 0, proj, kblk + 2 * g, proj, kblk + 3 * g, bias_d, NSA_HEADS, NSA_GROUP, NSA_GROUP,
                 _causal_pairs(nq, nn), True, member=member, et=_block_onehot(s, SEL_BLOCK), name="nsa_selected")
    n_back = -(-WINDOW // t)
    bias_w = _bias_vecs(tab, t, n_back + 1, window=WINDOW)
    o_w = _flash(proj, 0, proj, kblk + 4 * g, proj, kblk + 5 * g, bias_w, NSA_HEADS, NSA_GROUP, NSA_GROUP,
                 _window_pairs(nq, n_back), False, name="nsa_window")
    return o_c, o_s, o_w


def _moba(proj, tab, b, s):
    member = _moba_gate(proj, b, s)
    t = ATT_TILE
    nn = _n_near(t)
    nh = 4
    ng = MOBA_HEADS // nh
    bias_d = _bias_vecs(tab, t, nn + 1)
    return _flash(proj, 0, proj, ng, proj, 2 * ng, bias_d, MOBA_HEADS, 1, nh, _causal_pairs(s // t, nn),
                  True, member=member, et=_block_onehot(s, MOBA_BLOCK), name="moba_attn")


def _moe(x1, g_ffn, w_group, b_group, w_router, b_router, w_gate, w_up, w_down, g_final):
    t, d = x1.shape
    ng, _, epg = w_router.shape
    w_gr = jnp.concatenate([w_group, jnp.transpose(w_router, (1, 0, 2)).reshape(d, ng * epg),
                            jnp.zeros((d, LANES - ng - ng * epg), F32)], axis=1)
    b_gr = jnp.concatenate([b_group, b_router.reshape(-1), jnp.zeros((LANES - ng - ng * epg,), F32)]).reshape(1, LANES)
    h, info, cnt = _route(x1, g_ffn, w_gr, b_gr)
    n_e = ng * epg
    n_assign = t * EXPERT_TOPK
    n_blocks = -(-(n_assign + n_e * (MOE_ROWS - 1)) // MOE_ROWS)
    counts = cnt[0, :n_e].astype(jnp.int32)
    padded = (counts + MOE_ROWS - 1) // MOE_ROWS * MOE_ROWS
    pad_end = jnp.cumsum(padded)
    pad_start = pad_end - padded
    expert = info[:, 0:EXPERT_TOPK].astype(jnp.int32)
    rank = info[:, 4:4 + EXPERT_TOPK].astype(jnp.int32)
    e_ids = jnp.arange(n_e, dtype=jnp.int32)
    dest = (jnp.sum(jnp.where(expert[..., None] == e_ids, pad_start, 0), axis=-1) + rank).reshape(-1)
    block_row0 = jnp.arange(n_blocks, dtype=jnp.int32) * MOE_ROWS
    block_e = jnp.minimum(jnp.sum((pad_end[None, :] <= block_row0[:, None]).astype(jnp.int32), axis=1), n_e - 1)
    n_used = (pad_end[-1:] // MOE_ROWS).astype(jnp.int32)
    xs = _dispatch(dest, h, n_blocks * MOE_ROWS)
    ys = _experts(block_e, n_used, xs, w_gate, w_up, w_down)
    return _combine(dest, x1, info, g_final, ys)


def kernel(x, rel_bias, norm_mix, w_in, cmp_pe_k, cmp_w1_k, cmp_w2_k, cmp_pe_v, cmp_w1_v, cmp_w2_v, w_up_nsa,
           w_up_moba, w_out, norm_ffn, w_group, b_group, w_router, b_router, w_exp_gate, w_exp_up, w_exp_down,
           final_norm):
    b, s, d = x.shape
    t = b * s
    depth = w_in.shape[0]
    tab_a = rel_bias[:, :NSA_HEADS]
    tab_b = rel_bias[:, NSA_HEADS:]
    a_cols = NSA_HEADS * HEAD_DIM + 6 * NSA_KV_HEADS * HEAD_DIM
    gate_cols = 3 * NSA_HEADS
    b_cols = 3 * MOBA_HEADS * HEAD_DIM
    xt = x.reshape(t, d)
    out = None
    for l in range(depth):
        h = _rmsnorm(xt, norm_mix[l], BF16)
        wt = jnp.swapaxes(w_in[l], 0, 1)
        b_col0 = a_cols + gate_cols
        proj_a = _matmul(h, wt, 0, a_cols, BF16, scaled_cols=NSA_HEADS * HEAD_DIM, col_scale=Q_SCALE,
                         w_transposed=True, name="in_proj_a").reshape(b, s, a_cols)
        gate_a = _matmul(h, wt, a_cols, LANES, F32, w_transposed=True, tn=LANES, name="in_proj_gate")
        proj_b = _matmul(h, wt, b_col0, b_cols, BF16, scaled_cols=MOBA_HEADS * HEAD_DIM, col_scale=Q_SCALE,
                         w_transposed=True, name="in_proj_b").reshape(b, s, b_cols)
        gm = _matmul(h, wt, b_col0 + b_cols, 2 * d, F32, w_transposed=True, name="in_proj_gm")
        o_c, o_s, o_w = _nsa(proj_a, gate_cols, cmp_pe_k[l], cmp_w1_k[l], cmp_w2_k[l],
                             cmp_pe_v[l], cmp_w1_v[l], cmp_w2_v[l], tab_a, b, s)
        o_b = _moba(proj_b, tab_b, b, s)
        merged = _merge(o_c.reshape(t, -1), o_s.reshape(t, -1), o_w.reshape(t, -1), gate_a,
                        o_b.reshape(t, -1), gm, w_up_nsa[l], w_up_moba[l])
        x1 = _matmul(merged, w_out[l], 0, d, F32, res=xt, name="out_proj")
        assert l == depth - 1, "only the last layer's MoE is fused with the final norm"
        out = _moe(x1, norm_ffn[l], w_group[l], b_group[l], w_router[l], b_router[l],
                   w_exp_gate[l], w_exp_up[l], w_exp_down[l], final_norm)
    return out.reshape(b, s, d)
```

```python
import functools
import math

import numpy as np
import jax
import jax.numpy as jnp
from jax import lax
from jax.experimental import pallas as pl
from jax.experimental.pallas import tpu as pltpu

F32 = jnp.float32
BF16 = jnp.bfloat16

HEAD_DIM = 128
NSA_HEADS = 8
NSA_KV_HEADS = 2
NSA_GROUP = NSA_HEADS // NSA_KV_HEADS
CMP_BLOCK = 32
CMP_STRIDE = 16
SEL_BLOCK = 64
SEL_TOPN = 16
WINDOW = 512
FORCED_SCORE = 1e4
MOBA_HEADS = 8
MOBA_BLOCK = 256
MOBA_TOPK = 3
REL_BUCKETS = 32
REL_MAX_DIST = 128
N_GROUPS = 8
EXPERTS_PER_GROUP = 8
N_EXPERTS = N_GROUPS * EXPERTS_PER_GROUP
EXPERT_TOPK = 2
RMS_EPS = 1e-6

LANES = 128
ATT_TILE = 512
MOE_ROWS = 256
MASK_BIG = 1e30
M_INIT = -3e38
LOG2E = math.log2(math.e)
Q_SCALE = HEAD_DIM ** -0.5 * LOG2E
VMEM_LIMIT = 48 * 1024 * 1024


def _cparams(sem, vmem=VMEM_LIMIT, flags=None):
    return pltpu.CompilerParams(dimension_semantics=sem, vmem_limit_bytes=vmem, flags=flags)


def _rmsnorm_body(x_ref, g_ref, o_ref):
    x = x_ref[...]
    ms = jnp.mean(x * x, axis=-1, keepdims=True)
    o_ref[...] = (x * lax.rsqrt(ms + RMS_EPS) * g_ref[...]).astype(o_ref.dtype)


def _rmsnorm(x, g, out_dtype, tm=512):
    t, d = x.shape
    return pl.pallas_call(
        _rmsnorm_body,
        grid=(t // tm,),
        in_specs=[pl.BlockSpec((tm, d), lambda i: (i, 0)),
                  pl.BlockSpec((1, d), lambda i: (0, 0))],
        out_specs=pl.BlockSpec((tm, d), lambda i: (i, 0)),
        out_shape=jax.ShapeDtypeStruct((t, d), out_dtype),
        compiler_params=_cparams(("parallel",)),
        name="rmsnorm",
    )(x, g.reshape(1, d))


def _mm_body(*refs, has_res, n_scaled, col_scale, w_transposed):
    if has_res:
        a_ref, w_ref, r_ref, o_ref = refs
    else:
        a_ref, w_ref, o_ref = refs
    w = w_ref[...].astype(BF16)
    if w_transposed:
        acc = lax.dot_general(a_ref[...], w, (((1,), (1,)), ((), ())), preferred_element_type=F32)
    else:
        acc = jnp.dot(a_ref[...], w, preferred_element_type=F32)
    if n_scaled:
        acc = acc * jnp.where(pl.program_id(1) < n_scaled, col_scale, 1.0)
    if has_res:
        acc = acc + r_ref[...]
    o_ref[...] = acc.astype(o_ref.dtype)


def _matmul(a, w, col0, ncols, out_dtype, res=None, scaled_cols=0, col_scale=1.0, w_transposed=False,
            tm=1024, tn=512, name="matmul"):
    t, k = a.shape
    tn = min(tn, ncols)
    assert ncols % tn == 0 and t % tm == 0 and scaled_cols % tn == 0
    if w_transposed:
        assert col0 % 8 == 0
        w_spec = pl.BlockSpec((pl.Element(tn), pl.Element(k)),
                              lambda i, j: (pl.multiple_of(col0 + j * tn, 8), 0))
    else:
        assert col0 % tn == 0
        off = col0 // tn
        w_spec = pl.BlockSpec((k, tn), lambda i, j: (0, j + off))
    in_specs = [pl.BlockSpec((tm, k), lambda i, j: (i, 0)), w_spec]
    args = [a, w]
    if res is not None:
        in_specs.append(pl.BlockSpec((tm, tn), lambda i, j: (i, j)))
        args.append(res)
    return pl.pallas_call(
        functools.partial(_mm_body, has_res=res is not None, n_scaled=scaled_cols // tn, col_scale=col_scale,
                          w_transposed=w_transposed),
        grid=(t // tm, ncols // tn),
        in_specs=in_specs,
        out_specs=pl.BlockSpec((tm, tn), lambda i, j: (i, j)),
        out_shape=jax.ShapeDtypeStruct((t, ncols), out_dtype),
        compiler_params=_cparams(("parallel", "parallel")),
        name=name,
    )(*args)


def _compress_body(u_ref, pe_ref, w1_ref, w2_ref, o_ref, *, nc):
    u = u_ref[...]
    w1 = w1_ref[...].astype(BF16)
    half = u.shape[1]
    a = jnp.dot(u, w1[:half], preferred_element_type=F32)
    b = jnp.dot(u, w1[half:], preferred_element_type=F32)
    peb = jnp.dot(pe_ref[...].astype(BF16), w1, preferred_element_type=F32)[0:1]
    pre = a + pltpu.roll(b, nc - 1, 0) + peb
    hid = jax.nn.gelu(pre)
    o_ref[...] = jnp.dot(hid.astype(BF16), w2_ref[...].astype(BF16),
                         preferred_element_type=F32).astype(o_ref.dtype)


def _compress(u, pe, w1, w2):
    bg, nc, kk = u.shape
    hid = w1.shape[1]
    dh = w2.shape[1]
    pe8 = jnp.broadcast_to(pe.reshape(1, -1), (16, pe.size))
    return pl.pallas_call(
        functools.partial(_compress_body, nc=nc),
        grid=(bg,),
        in_specs=[pl.BlockSpec((None, nc, kk), lambda i: (i, 0, 0)),
                  pl.BlockSpec((16, 2 * kk), lambda i: (0, 0)),
                  pl.BlockSpec((2 * kk, hid), lambda i: (0, 0)),
                  pl.BlockSpec((hid, dh), lambda i: (0, 0))],
        out_specs=pl.BlockSpec((None, nc, dh), lambda i: (i, 0, 0)),
        out_shape=jax.ShapeDtypeStruct((bg, nc, dh), BF16),
        compiler_params=_cparams(("parallel",)),
        name="nsa_compress",
    )(u, pe8, w1, w2)


def _split3(x):
    p1 = x.astype(BF16)
    r = x - p1.astype(F32)
    p2 = r.astype(BF16)
    p3 = (r - p2.astype(F32)).astype(BF16)
    return p1, p2, p3


def _rank_count(score, n_rows):
    groups = []
    for g0 in range(0, n_rows, 8):
        sg = score[g0:min(g0 + 8, n_rows), :]
        n_iota = g0 + lax.broadcasted_iota(jnp.int32, sg.shape, 0)
        cnt = jnp.zeros(sg.shape, F32)
        for m in range(n_rows):
            row = score[m:m + 1, :]
            if m < g0:
                beats = row >= sg
            elif m >= g0 + 8:
                beats = row > sg
            else:
                tie = jnp.where(n_iota > m, 1.0, 0.0)
                beats = jnp.where(row > sg, 1.0, jnp.where(row == sg, tie, 0.0)) > 0.5
            cnt = cnt + jnp.where(beats, 1.0, 0.0)
        groups.append(cnt)
    return jnp.concatenate(groups, axis=0) if len(groups) > 1 else groups[0]


def _nsa_cmp_body(q_ref, kc_ref, vc_ref, bias_ref, ov_ref, oc_ref, mem_ref, *, tq, nc, n_sel):
    t0 = pl.program_id(2) * tq
    kc = kc_ref[...]
    vc = vc_ref[...]
    t_idx = t0 + lax.broadcasted_iota(jnp.int32, (tq, nc), 0)
    c_idx = lax.broadcasted_iota(jnp.int32, (tq, nc), 1)
    dist = t_idx - (c_idx * CMP_STRIDE + (CMP_BLOCK - 1))
    valid = dist >= 0
    n_k = REL_MAX_DIST // CMP_STRIDE
    kk = dist // CMP_STRIDE
    k_is = [kk == k for k in range(n_k)]
    psum = jnp.zeros((tq, nc), F32)
    for j in range(NSA_GROUP):
        hs = slice(j * HEAD_DIM, (j + 1) * HEAD_DIM)
        gt = bias_ref[j] * LOG2E
        bias = jnp.broadcast_to(gt[:, n_k:n_k + 1], (tq, nc))
        for k in range(n_k):
            bias = jnp.where(k_is[k], gt[:, k:k + 1], bias)
        s = lax.dot_general(q_ref[:, hs], kc, (((1,), (1,)), ((), ())), preferred_element_type=F32) + bias
        s = jnp.where(valid, s, -MASK_BIG)
        m = jnp.max(s, axis=-1, keepdims=True)
        m = jnp.where(m > -0.5 * MASK_BIG, m, 0.0)
        p = jnp.where(valid, jnp.exp2(s - m), 0.0)
        d = jnp.sum(p, axis=-1, keepdims=True)
        p = p / jnp.where(d > 0, d, 1.0)
        oc_ref[:, hs] = jnp.dot(p.astype(BF16), vc, preferred_element_type=F32).astype(oc_ref.dtype)
        psum = psum + p
    ov = ov_ref[...]
    nt = (((1,), (1,)), ((), ()))
    p1, p2, p3 = _split3(psum)
    psel = (lax.dot_general(ov, p1, nt, preferred_element_type=F32)
            + lax.dot_general(ov, p2, nt, preferred_element_type=F32)
            + lax.dot_general(ov, p3, nt, preferred_element_type=F32))
    n_idx = lax.broadcasted_iota(jnp.int32, (n_sel, tq), 0)
    tt = t0 + lax.broadcasted_iota(jnp.int32, (n_sel, tq), 1)
    cur = tt // SEL_BLOCK
    forced = jnp.where(n_idx == 0, 1.0, jnp.where(n_idx == cur, 1.0, jnp.where(n_idx == cur - 1, 1.0, 0.0)))
    score = jnp.where(forced > 0.5, FORCED_SCORE, jnp.where(n_idx * SEL_BLOCK <= tt, psel, -1.0))
    cnt = _rank_count(score, n_sel)
    member = jnp.where(cnt < float(min(SEL_TOPN, n_sel)), 1.0, 0.0)
    if n_sel < LANES:
        member = jnp.concatenate([member, jnp.zeros((LANES - n_sel, tq), F32)], axis=0)
    mem_ref[...] = member.T.astype(mem_ref.dtype)


def _nsa_cmp(proj, kc, vc, tab, overlap, b, s, tq=256):
    g = NSA_KV_HEADS
    nc = kc.shape[2]
    n_sel = s // SEL_BLOCK
    gw = NSA_GROUP * HEAD_DIM
    assert tq % CMP_STRIDE == 0 and REL_MAX_DIST % CMP_STRIDE == 0
    n_k = REL_MAX_DIST // CMP_STRIDE
    rho = (np.arange(tq)[:, None] - (CMP_BLOCK - 1)) % CMP_STRIDE
    dd = np.concatenate([rho + CMP_STRIDE * np.arange(n_k)[None, :], np.full((tq, 1), REL_MAX_DIST)], axis=1)
    gtab = jnp.transpose(tab[_rel_bucket(dd)], (2, 0, 1)).astype(F32)
    bias_c = jnp.pad(gtab, ((0, 0), (0, 0), (0, LANES - n_k - 1)))
    body = functools.partial(_nsa_cmp_body, tq=tq, nc=nc, n_sel=n_sel)
    return pl.pallas_call(
        body,
        grid=(b, g, s // tq),
        in_specs=[pl.BlockSpec((None, tq, gw), lambda bi, gi, i: (bi, i, gi)),
                  pl.BlockSpec((None, None, nc, HEAD_DIM), lambda bi, gi, i: (bi, gi, 0, 0)),
                  pl.BlockSpec((None, None, nc, HEAD_DIM), lambda bi, gi, i: (bi, gi, 0, 0)),
                  pl.BlockSpec((NSA_GROUP, tq, LANES), lambda bi, gi, i: (gi, 0, 0)),
                  pl.BlockSpec((n_sel, nc), lambda bi, gi, i: (0, 0))],
        out_specs=[pl.BlockSpec((None, tq, gw), lambda bi, gi, i: (bi, i, gi)),
                   pl.BlockSpec((None, None, tq, LANES), lambda bi, gi, i: (bi, gi, i, 0))],
        out_shape=[jax.ShapeDtypeStruct((b, s, NSA_HEADS * HEAD_DIM), BF16),
                   jax.ShapeDtypeStruct((b, g, s, LANES), BF16)],
        compiler_params=_cparams(("parallel", "parallel", "parallel")),
        name="nsa_cmp_select",
    )(proj, kc, vc, bias_c, overlap)


def _moba_gate_body(q_ref, k_ref, mem_ref, *, s, nblk):
    k = k_ref[...].astype(F32)
    kmean = jnp.mean(k.reshape(nblk, MOBA_BLOCK, HEAD_DIM), axis=1)
    k1 = kmean.astype(BF16)
    k2 = (kmean - k1.astype(F32)).astype(BF16)
    q = q_ref[...]
    nt = (((1,), (1,)), ((), ()))
    gate = (lax.dot_general(k1, q, nt, preferred_element_type=F32)
            + lax.dot_general(k2, q, nt, preferred_element_type=F32))
    n_idx = lax.broadcasted_iota(jnp.int32, (nblk, s), 0)
    own = lax.broadcasted_iota(jnp.int32, (nblk, s), 1) // MOBA_BLOCK
    past = n_idx < own
    score = jnp.where(past, gate, -MASK_BIG)
    cnt = _rank_count(score, nblk)
    n_top = max(1, min(MOBA_TOPK, nblk - 1))
    sel = jnp.where(past, jnp.where(cnt < float(n_top), 1.0, 0.0), 0.0)
    member = jnp.where(n_idx == own, 1.0, sel)
    member = jnp.concatenate([member, jnp.zeros((LANES - nblk, s), F32)], axis=0)
    mem_ref[...] = member.T.astype(mem_ref.dtype)


def _moba_gate(proj, b, s):
    h = MOBA_HEADS
    nblk = s // MOBA_BLOCK
    return pl.pallas_call(
        functools.partial(_moba_gate_body, s=s, nblk=nblk),
        grid=(b, h),
        in_specs=[pl.BlockSpec((None, s, HEAD_DIM), lambda bi, hi: (bi, 0, hi)),
                  pl.BlockSpec((None, s, HEAD_DIM), lambda bi, hi: (bi, 0, h + hi))],
        out_specs=pl.BlockSpec((None, None, s, LANES), lambda bi, hi: (bi, hi, 0, 0)),
        out_shape=jax.ShapeDtypeStruct((b, h, s, LANES), BF16),
        compiler_params=_cparams(("parallel", "parallel")),
        name="moba_gate",
    )(proj, proj)


def _flash_body(qi_ref, ki_ref, bo_ref, fl_ref, *refs, nh, ratio, nm, n_near, has_far):
    if nm:
        (q_ref, k_ref, v_ref, bvec_ref, mem_ref, et_ref, o_ref,
         m_ref, l_ref, acc_ref, sh_ref, al_ref, bias_ref, s_ref, p_ref) = refs
    else:
        q_ref, k_ref, v_ref, bvec_ref, o_ref, m_ref, l_ref, acc_ref, sh_ref, al_ref, bias_ref, s_ref, p_ref = refs
    del qi_ref, ki_ref
    p = pl.program_id(2)
    flag = fl_ref[p]
    bo = bo_ref[p]
    t = q_ref.shape[0]
    rows = 64

    @pl.when(p == 0)
    def _():
        for h in range(nh):
            for o in range(n_near):
                vec = bvec_ref[h, o][0:1, :] * LOG2E
                for rc in range(t // rows):
                    x = pltpu.roll(jnp.broadcast_to(vec, (rows, 2 * t)), rc * rows, 1, stride=1, stride_axis=0)
                    bias_ref[h, o, rc * rows:(rc + 1) * rows, :] = x[:, :t]

    @pl.when((flag & 1) != 0)
    def _():
        m_ref[...] = jnp.full(m_ref.shape, M_INIT, F32)
        l_ref[...] = jnp.zeros(l_ref.shape, F32)
        acc_ref[...] = jnp.zeros(acc_ref.shape, F32)

    nt = (((1,), (1,)), ((), ()))
    reps = t // LANES

    def step(near):
        for h in range(nh):
            hs = slice(h * HEAD_DIM, (h + 1) * HEAD_DIM)
            kv = h // ratio
            ks = slice(kv * HEAD_DIM, (kv + 1) * HEAD_DIM)
            q = q_ref[:, hs]
            k = k_ref[:, ks]
            if nm:
                mneg = mem_ref[h // (nh // nm)] - 1.0
                q = jnp.concatenate([q, mneg.astype(BF16)], axis=1)
                k = jnp.concatenate([k, et_ref[...]], axis=1)
            sc = lax.dot_general(q, k, nt, preferred_element_type=F32)
            m_prev = m_ref[h]
            if near:
                sc = sc + bias_ref[h, bo]
                m_new = jnp.maximum(m_prev, jnp.max(sc, axis=-1, keepdims=True))
                sh_ref[h] = m_new
            else:
                cfar = bvec_ref[h, n_near][0:1, 0:LANES] * LOG2E
                m_new = jnp.maximum(m_prev, jnp.max(sc, axis=-1, keepdims=True) + cfar)
                sh_ref[h] = m_new - cfar
            s_ref[h] = sc
            al_ref[h] = jnp.exp2(m_prev - m_new)
            m_ref[h] = m_new
        for h in range(nh):
            ks = slice(h // ratio * HEAD_DIM, (h // ratio + 1) * HEAD_DIM)
            for rc in range(t // rows):
                rs = slice(rc * rows, (rc + 1) * rows)
                pm = jnp.exp2(s_ref[h, rs, :] - jnp.tile(sh_ref[h, rs, :], (1, reps)))
                l_ref[h, rs, :] = al_ref[h, rs, :] * l_ref[h, rs, :] + jnp.sum(pm, axis=-1, keepdims=True)
                p_ref[h, rs, :] = pm.astype(BF16)
            acc_ref[h] = al_ref[h] * acc_ref[h] + jnp.dot(p_ref[h], v_ref[:, ks], preferred_element_type=F32)

    if has_far:
        pl.when(bo < n_near)(lambda: step(True))
        pl.when(bo >= n_near)(lambda: step(False))
    else:
        step(True)

    @pl.when((flag & 2) != 0)
    def _():
        for h in range(nh):
            l = l_ref[h]
            o_ref[:, h * HEAD_DIM:(h + 1) * HEAD_DIM] = (acc_ref[h] / jnp.where(l > 0, l, 1.0)).astype(o_ref.dtype)


def _flash(q_arr, q_off, k_arr, k_off, v_arr, v_off, bias, n_heads, ratio, nh, pairs, has_far, member=None,
           et=None, name="flash"):
    b, s, _ = q_arr.shape
    t = ATT_TILE
    nkv = nh // ratio
    ng = n_heads // nh
    qi = jnp.asarray([p[0] for p in pairs], jnp.int32)
    ki = jnp.asarray([p[1] for p in pairs], jnp.int32)
    bo = jnp.asarray([p[2] for p in pairs], jnp.int32)
    fl = jnp.asarray([p[3] for p in pairs], jnp.int32)
    nb = bias.shape[1]
    nm = 0
    in_specs = [
        pl.BlockSpec((None, t, nh * HEAD_DIM), lambda bi, gi, p, qi, ki, bo, fl: (bi, qi[p], q_off + gi)),
        pl.BlockSpec((None, t, nkv * HEAD_DIM), lambda bi, gi, p, qi, ki, bo, fl: (bi, ki[p], k_off + gi)),
        pl.BlockSpec((None, t, nkv * HEAD_DIM), lambda bi, gi, p, qi, ki, bo, fl: (bi, ki[p], v_off + gi)),
        pl.BlockSpec((nh, nb, 8, 2 * t), lambda bi, gi, p, qi, ki, bo, fl: (gi, 0, 0, 0)),
    ]
    args = [q_arr, k_arr, v_arr, bias]
    if member is not None:
        nm = member.shape[1] // ng
        in_specs += [
            pl.BlockSpec((None, nm, t, LANES), lambda bi, gi, p, qi, ki, bo, fl: (bi, gi, qi[p], 0)),
            pl.BlockSpec((t, LANES), lambda bi, gi, p, qi, ki, bo, fl: (ki[p], 0)),
        ]
        args += [member, et]
    n_near = nb - 1 if has_far else nb
    body = functools.partial(_flash_body, nh=nh, ratio=ratio, nm=nm, n_near=n_near, has_far=has_far)
    return pl.pallas_call(
        body,
        grid_spec=pltpu.PrefetchScalarGridSpec(
            num_scalar_prefetch=4,
            grid=(b, ng, len(pairs)),
            in_specs=in_specs,
            out_specs=pl.BlockSpec((None, t, nh * HEAD_DIM), lambda bi, gi, p, qi, ki, bo, fl: (bi, qi[p], gi)),
            scratch_shapes=[pltpu.VMEM((nh, t, LANES), F32)] * 5 + [pltpu.VMEM((nh, n_near, t, t), F32),
                                                                    pltpu.VMEM((nh, t, t), F32),
                                                                    pltpu.VMEM((nh, t, t), BF16)],
        ),
        out_shape=jax.ShapeDtypeStruct((b, s, n_heads * HEAD_DIM), BF16),
        compiler_params=_cparams(("parallel", "parallel", "arbitrary")),
        name=name,
    )(qi, ki, bo, fl, *args)


def _rel_bucket(dist):
    n = jnp.maximum(jnp.asarray(dist, jnp.int32), 0)
    max_exact = REL_BUCKETS // 2
    nf = jnp.maximum(n, 1).astype(jnp.float32)
    large = max_exact + (jnp.log(nf / max_exact) / math.log(REL_MAX_DIST / max_exact)
                         * (REL_BUCKETS - max_exact)).astype(jnp.int32)
    return jnp.where(n < max_exact, n, jnp.minimum(large, REL_BUCKETS - 1))


def _n_near(t):
    return -(-(REL_MAX_DIST - 1 + t) // t)


def _bias_vecs(tab, t, n_off, window=None):
    k = np.arange(2 * t)[None, :]
    dist = np.arange(n_off)[:, None] * t + np.where(k < t, -k, 2 * t - k)
    ok = dist >= 0
    if window is not None:
        ok &= dist < window
    bias = jnp.where(jnp.asarray(ok)[..., None], tab[_rel_bucket(dist)], -MASK_BIG)
    bias = jnp.transpose(bias, (2, 0, 1)).astype(F32)
    return jnp.broadcast_to(bias[:, :, None, :], (bias.shape[0], n_off, 8, 2 * t))


def _causal_pairs(nq, n_near):
    pairs = []
    for qi in range(nq):
        for ki in range(qi + 1):
            pairs.append((qi, ki, min(qi - ki, n_near), (1 if ki == 0 else 0) | (2 if ki == qi else 0)))
    return pairs


def _window_pairs(nq, n_back):
    pairs = []
    for qi in range(nq):
        lo = max(0, qi - n_back)
        for ki in range(lo, qi + 1):
            pairs.append((qi, ki, qi - ki, (1 if ki == lo else 0) | (2 if ki == qi else 0)))
    return pairs


def _block_onehot(s, blk):
    return jnp.asarray(np.where(np.arange(s)[:, None] // blk == np.arange(LANES)[None, :], MASK_BIG, 0.0), BF16)


def _merge_body(oc_ref, os_ref, ow_ref, gl_ref, ob_ref, gma_ref, gmb_ref, wa_ref, wb_ref, o_ref, oa_ref):
    @pl.when(pl.program_id(1) == 0)
    def _():
        gates = jax.nn.sigmoid(gl_ref[...])
        for h in range(NSA_HEADS):
            hs = slice(h * HEAD_DIM, (h + 1) * HEAD_DIM)
            mix = (gates[:, 3 * h:3 * h + 1] * oc_ref[:, hs].astype(F32)
                   + gates[:, 3 * h + 1:3 * h + 2] * os_ref[:, hs].astype(F32)
                   + gates[:, 3 * h + 2:3 * h + 3] * ow_ref[:, hs].astype(F32))
            oa_ref[:, hs] = mix.astype(BF16)

    ya = jnp.dot(oa_ref[...], wa_ref[...].astype(BF16), preferred_element_type=F32)
    yb = jnp.dot(ob_ref[...], wb_ref[...].astype(BF16), preferred_element_type=F32)
    o_ref[...] = (jax.nn.sigmoid(gma_ref[...]) * ya + jax.nn.sigmoid(gmb_ref[...]) * yb).astype(o_ref.dtype)


def _merge(o_c, o_s, o_w, gate_logits, o_b, gm, w_up_a, w_up_b, tm=512, tn=512):
    t, ka = o_c.shape
    kb = o_b.shape[1]
    d = w_up_a.shape[1]
    nj = d // tn
    row = lambda i, j: (i, 0)
    return pl.pallas_call(
        _merge_body,
        grid=(t // tm, nj),
        in_specs=[pl.BlockSpec((tm, ka), row), pl.BlockSpec((tm, ka), row), pl.BlockSpec((tm, ka), row),
                  pl.BlockSpec((tm, LANES), row), pl.BlockSpec((tm, kb), row),
                  pl.BlockSpec((tm, tn), lambda i, j: (i, j)),
                  pl.BlockSpec((tm, tn), lambda i, j: (i, j + nj)),
                  pl.BlockSpec((ka, tn), lambda i, j: (0, j)),
                  pl.BlockSpec((kb, tn), lambda i, j: (0, j))],
        out_specs=pl.BlockSpec((tm, tn), lambda i, j: (i, j)),
        out_shape=jax.ShapeDtypeStruct((t, d), BF16),
        scratch_shapes=[pltpu.VMEM((tm, ka), BF16)],
        compiler_params=_cparams(("parallel", "arbitrary")),
        name="merge_up",
    )(o_c, o_s, o_w, gate_logits, o_b, gm, gm, w_up_a, w_up_b)


def _route_body(x_ref, g_ref, w_ref, b_ref, h_ref, info_ref, cnt_ref, carry_ref, *, tm):
    @pl.when(pl.program_id(0) == 0)
    def _():
        carry_ref[...] = jnp.zeros(carry_ref.shape, F32)

    x = x_ref[...]
    ms = jnp.mean(x * x, axis=-1, keepdims=True)
    h = x * lax.rsqrt(ms + RMS_EPS) * g_ref[...]
    h_ref[...] = h
    w = w_ref[...]
    h1 = h.astype(BF16)
    h2 = (h - h1.astype(F32)).astype(BF16)
    w1 = w.astype(BF16)
    w2 = (w - w1.astype(F32)).astype(BF16)
    logits = (jnp.dot(h1, w1, preferred_element_type=F32) + jnp.dot(h1, w2, preferred_element_type=F32)
              + jnp.dot(h2, w1, preferred_element_type=F32)) + b_ref[...]
    lane = lax.broadcasted_iota(jnp.int32, (tm, LANES), 1)
    lanef = lane.astype(F32)

    is_g = lane < N_GROUPS
    gl = jnp.where(is_g, logits, -MASK_BIG)
    ge = jnp.where(is_g, jnp.exp(gl - jnp.max(gl, axis=-1, keepdims=True)), 0.0)
    gp = ge / jnp.sum(ge, axis=-1, keepdims=True)
    g_val = jnp.max(gp, axis=-1, keepdims=True)
    g_idx = jnp.min(jnp.where(gp == g_val, lanef, float(LANES)), axis=-1, keepdims=True)

    lane_grp = ((lane - N_GROUPS) // EXPERTS_PER_GROUP).astype(F32)
    in_e = jnp.where(lane >= N_GROUPS, jnp.where(lane < N_GROUPS + N_EXPERTS, 1.0, 0.0), 0.0)
    is_e = jnp.where(lane_grp == g_idx, in_e, 0.0) > 0.5
    el = jnp.where(is_e, logits, -MASK_BIG)
    ee = jnp.where(is_e, jnp.exp(el - jnp.max(el, axis=-1, keepdims=True)), 0.0)
    ep = jnp.where(is_e, ee / jnp.sum(ee, axis=-1, keepdims=True), -1.0)
    v1 = jnp.max(ep, axis=-1, keepdims=True)
    l1 = jnp.min(jnp.where(ep == v1, lanef, float(LANES)), axis=-1, keepdims=True)
    ep2 = jnp.where(lanef == l1, -1.0, ep)
    v2 = jnp.max(ep2, axis=-1, keepdims=True)
    l2 = jnp.min(jnp.where(ep2 == v2, lanef, float(LANES)), axis=-1, keepdims=True)
    vs = v1 + v2
    wt1 = g_val * v1 / vs
    wt2 = g_val * v2 / vs
    e1 = l1 - float(N_GROUPS)
    e2 = l2 - float(N_GROUPS)

    oh = jnp.where(lanef == e1, 1.0, jnp.where(lanef == e2, 1.0, 0.0))
    r_i = lax.broadcasted_iota(jnp.int32, (tm, tm), 0)
    c_i = lax.broadcasted_iota(jnp.int32, (tm, tm), 1)
    tri = jnp.where(r_i > c_i, 1.0, 0.0).astype(BF16)
    base = jnp.dot(tri, oh.astype(BF16), preferred_element_type=F32) + carry_ref[...]
    r1 = jnp.sum(jnp.where(lanef == e1, base, 0.0), axis=-1, keepdims=True)
    r2 = jnp.sum(jnp.where(lanef == e2, base, 0.0), axis=-1, keepdims=True)
    carry_ref[...] = carry_ref[...] + jnp.sum(oh, axis=0, keepdims=True)
    cnt_ref[...] = jnp.broadcast_to(carry_ref[...], cnt_ref.shape)
    info = jnp.where(lane == 0, e1, jnp.where(lane == 1, e2, jnp.where(lane == 2, wt1, jnp.where(
        lane == 3, wt2, jnp.where(lane == 4, r1, jnp.where(lane == 5, r2, 0.0))))))
    info_ref[...] = info


def _route(x1, g, w_gr, b_gr, tm=512):
    t, d = x1.shape
    return pl.pallas_call(
        functools.partial(_route_body, tm=tm),
        grid=(t // tm,),
        in_specs=[pl.BlockSpec((tm, d), lambda i: (i, 0)),
                  pl.BlockSpec((1, d), lambda i: (0, 0)),
                  pl.BlockSpec((d, LANES), lambda i: (0, 0)),
                  pl.BlockSpec((1, LANES), lambda i: (0, 0))],
        out_specs=[pl.BlockSpec((tm, d), lambda i: (i, 0)),
                   pl.BlockSpec((tm, LANES), lambda i: (i, 0)),
                   pl.BlockSpec((8, LANES), lambda i: (0, 0))],
        out_shape=[jax.ShapeDtypeStruct((t, d), F32),
                   jax.ShapeDtypeStruct((t, LANES), F32),
                   jax.ShapeDtypeStruct((8, LANES), F32)],
        scratch_shapes=[pltpu.VMEM((1, LANES), F32)],
        compiler_params=_cparams(("arbitrary",)),
        name="moe_route",
    )(x1, g.reshape(1, d), w_gr, b_gr)


def _rowmap_body(dest_ref, tok_ref, dst_ref, *, n_tok, n_rows):
    rb = MOE_ROWS
    trash0 = EXPERT_TOPK * n_tok

    def fill(j, c):
        tok_ref[j] = 0
        dst_ref[j] = trash0 + (j & (2 * rb - 1))
        return c

    lax.fori_loop(0, n_rows + rb, fill, 0, unroll=8)

    def place(a, c):
        row = dest_ref[a]
        tok = lax.shift_right_logical(a, 1)
        tok_ref[row] = tok
        dst_ref[row + rb] = (a & 1) * n_tok + tok
        return c

    lax.fori_loop(0, EXPERT_TOPK * n_tok, place, 0, unroll=4)


def _rowmap(dest, n_rows):
    n_tok = dest.shape[0] // EXPERT_TOPK
    return pl.pallas_call(
        functools.partial(_rowmap_body, n_tok=n_tok, n_rows=n_rows),
        in_specs=[pl.BlockSpec(memory_space=pltpu.SMEM)],
        out_specs=[pl.BlockSpec(memory_space=pltpu.SMEM), pl.BlockSpec(memory_space=pltpu.SMEM)],
        out_shape=[jax.ShapeDtypeStruct((n_rows + MOE_ROWS,), jnp.int32)] * 2,
        name="moe_rowmap",
    )(dest)


def _row_copy(src_ref, src_row, dst_ref, dst_row, sem):
    return pltpu.make_async_copy(src_ref.at[pl.ds(src_row, 1)], dst_ref.at[pl.ds(dst_row, 1)], sem)


def _expert_body(be_ref, nu_ref, tok_ref, dst_ref, h_ref, wg_ref, wu_ref, wd_ref, o_ref,
                 xbuf, ybuf, wg_s, wu_s, wd_s, gsem, ssem, *, n_tok, n_blocks):
    rb = MOE_ROWS
    i = pl.program_id(0)
    nu = nu_ref[0]
    slot = lax.rem(i, 2)
    other = 1 - slot
    trash0 = EXPERT_TOPK * n_tok

    def gather_start(blk, sl, r):
        _row_copy(h_ref, tok_ref[blk * rb + r], xbuf.at[sl], r, gsem.at[sl]).start()

    def gather_wait(sl, r):
        _row_copy(h_ref, 0, xbuf.at[sl], r, gsem.at[sl]).wait()

    def scatter_start(sl, r):
        _row_copy(ybuf.at[sl], r, o_ref, dst_ref[i * rb + r], ssem.at[sl]).start()

    def scatter_wait(sl, r):
        _row_copy(ybuf.at[sl], r, o_ref, trash0, ssem.at[sl]).wait()

    def looped(fn):
        def body(r, c):
            fn(r)
            return c
        lax.fori_loop(0, rb, body, 0)

    @pl.when(i == 0)
    def _():
        ybuf[...] = jnp.zeros(ybuf.shape, ybuf.dtype)
        looped(lambda r: _row_copy(ybuf.at[0], r, o_ref, trash0 + rb + r, ssem.at[0]).start())
        looped(lambda r: scatter_wait(0, r))
        looped(lambda r: gather_start(0, 0, r))

    e = be_ref[jnp.minimum(i, n_blocks - 1)]
    prev = be_ref[jnp.maximum(jnp.minimum(i, n_blocks - 1) - 1, 0)]

    @pl.when((i < nu) & ((i == 0) | (e != prev)))
    def _():
        wg_s[...] = wg_ref[...].astype(BF16)
        wu_s[...] = wu_ref[...].astype(BF16)
        wd_s[...] = wd_ref[...].astype(BF16)

    @pl.when((i < nu) & (i >= 1))
    def _():
        for r in range(rb):
            scatter_wait(slot, r)

    @pl.when(i < nu)
    def _():
        for r in range(rb):
            gather_wait(slot, r)
        nxt = jnp.minimum(i + 1, n_blocks - 1)

        def dma_group(lo, hi):
            for r in range(lo, hi):
                gather_start(nxt, other, r)
                scatter_start(other, r)

        dma_group(0, rb // 4)
        x = xbuf[slot].astype(BF16)
        g = jnp.dot(x, wg_s[...], preferred_element_type=F32)
        dma_group(rb // 4, rb // 2)
        u = jnp.dot(x, wu_s[...], preferred_element_type=F32)
        mid = (jax.nn.silu(g) * u).astype(BF16)
        dma_group(rb // 2, rb)
        ybuf[slot] = jnp.dot(mid, wd_s[...], preferred_element_type=F32)

    @pl.when(i == nu)
    def _():
        looped(lambda r: gather_wait(slot, r))
        looped(lambda r: scatter_wait(slot, r))
        looped(lambda r: scatter_start(other, r))
        looped(lambda r: scatter_wait(other, r))


def _experts(block_e, n_used, row_tok, row_dst, h, w_gate, w_up, w_down):
    t, d = h.shape
    n_blocks = row_tok.shape[0] // MOE_ROWS - 1
    ff = w_gate.shape[2]
    wsel = lambda i, be, nu, tok, dst: (be[jnp.minimum(i, n_blocks - 1)], 0, 0)
    return pl.pallas_call(
        functools.partial(_expert_body, n_tok=t, n_blocks=n_blocks),
        grid_spec=pltpu.PrefetchScalarGridSpec(
            num_scalar_prefetch=4,
            grid=(n_blocks + 1,),
            in_specs=[pl.BlockSpec(memory_space=pl.ANY),
                      pl.BlockSpec((None, d, ff), wsel),
                      pl.BlockSpec((None, d, ff), wsel),
                      pl.BlockSpec((None, ff, d), wsel)],
            out_specs=pl.BlockSpec(memory_space=pl.ANY),
            scratch_shapes=[pltpu.VMEM((2, MOE_ROWS, d), F32), pltpu.VMEM((2, MOE_ROWS, d), F32),
                            pltpu.VMEM((d, ff), BF16), pltpu.VMEM((d, ff), BF16), pltpu.VMEM((ff, d), BF16),
                            pltpu.SemaphoreType.DMA((2,)), pltpu.SemaphoreType.DMA((2,))],
        ),
        out_shape=jax.ShapeDtypeStruct((EXPERT_TOPK * t + 2 * MOE_ROWS, d), F32),
        compiler_params=_cparams(("arbitrary",)),
        name="moe_experts",
    )(block_e, n_used, row_tok, row_dst, h, w_gate, w_up, w_down)


def _combine_body(x_ref, y0_ref, y1_ref, info_ref, g_ref, o_ref):
    info = info_ref[...]
    y = x_ref[...] + (info[:, 2:3] * y0_ref[...] + info[:, 3:4] * y1_ref[...])
    ms = jnp.mean(y * y, axis=-1, keepdims=True)
    o_ref[...] = y * lax.rsqrt(ms + RMS_EPS) * g_ref[...]


def _combine(x1, ys, info, g, tm=512):
    t, d = x1.shape
    nt = t // tm
    return pl.pallas_call(
        _combine_body,
        grid=(nt,),
        in_specs=[pl.BlockSpec((tm, d), lambda i: (i, 0)),
                  pl.BlockSpec((tm, d), lambda i: (i, 0)),
                  pl.BlockSpec((tm, d), lambda i: (nt + i, 0)),
                  pl.BlockSpec((tm, LANES), lambda i: (i, 0)),
                  pl.BlockSpec((1, d), lambda i: (0, 0))],
        out_specs=pl.BlockSpec((tm, d), lambda i: (i, 0)),
        out_shape=jax.ShapeDtypeStruct((t, d), F32),
        compiler_params=_cparams(("parallel",)),
        name="moe_combine",
    )(x1, ys, ys, info, g.reshape(1, d))


def _nsa(proj, gate_cols, pe_k, w1_k, w2_k, pe_v, w1_v, w2_v, tab, b, s):
    del gate_cols
    g, dh = NSA_KV_HEADS, HEAD_DIM
    qw = NSA_HEADS * dh
    nc = s // CMP_STRIDE

    def blocks16(col0):
        a = proj[:, :, col0:col0 + g * dh].reshape(b, nc, CMP_STRIDE, g, dh)
        return a.transpose(0, 3, 1, 2, 4).reshape(b * g, nc, CMP_STRIDE * dh)

    kc = _compress(blocks16(qw), pe_k, w1_k, w2_k).reshape(b, g, nc, dh)
    vc = _compress(blocks16(qw + g * dh), pe_v, w1_v, w2_v).reshape(b, g, nc, dh)

    c_start = np.arange(nc)[None, :] * CMP_STRIDE
    n_sel = s // SEL_BLOCK
    sb = np.arange(n_sel)[:, None] * SEL_BLOCK
    overlap = jnp.asarray((c_start < sb + SEL_BLOCK) & (c_start + CMP_BLOCK > sb), BF16)
    o_c, member = _nsa_cmp(proj, kc, vc, tab, overlap, b, s)

    t = ATT_TILE
    nq = s // t
    nn = _n_near(t)
    kblk = qw // dh
    bias_d = _bias_vecs(tab, t, nn + 1)
    o_s = _flash(proj, 0, proj, kblk + 2 * g, proj, kblk + 3 * g, bias_d, NSA_HEADS, NSA_GROUP, NSA_GROUP,
                 _causal_pairs(nq, nn), True, member=member, et=_block_onehot(s, SEL_BLOCK), name="nsa_selected")
    n_back = -(-WINDOW // t)
    bias_w = _bias_vecs(tab, t, n_back + 1, window=WINDOW)
    o_w = _flash(proj, 0, proj, kblk + 4 * g, proj, kblk + 5 * g, bias_w, NSA_HEADS, NSA_GROUP, NSA_GROUP,
                 _window_pairs(nq, n_back), False, name="nsa_window")
    return o_c, o_s, o_w


def _moba(proj, tab, b, s):
    member = _moba_gate(proj, b, s)
    t = ATT_TILE
    nn = _n_near(t)
    nh = 4
    ng = MOBA_HEADS // nh
    bias_d = _bias_vecs(tab, t, nn + 1)
    return _flash(proj, 0, proj, ng, proj, 2 * ng, bias_d, MOBA_HEADS, 1, nh, _causal_pairs(s // t, nn),
                  True, member=member, et=_block_onehot(s, MOBA_BLOCK), name="moba_attn")


def _moe(x1, g_ffn, w_group, b_group, w_router, b_router, w_gate, w_up, w_down, g_final):
    t, d = x1.shape
    ng, _, epg = w_router.shape
    w_gr = jnp.concatenate([w_group, jnp.transpose(w_router, (1, 0, 2)).reshape(d, ng * epg),
                            jnp.zeros((d, LANES - ng - ng * epg), F32)], axis=1)
    b_gr = jnp.concatenate([b_group, b_router.reshape(-1), jnp.zeros((LANES - ng - ng * epg,), F32)]).reshape(1, LANES)
    h, info, cnt = _route(x1, g_ffn, w_gr, b_gr)
    n_e = ng * epg
    n_assign = t * EXPERT_TOPK
    n_blocks = -(-(n_assign + n_e * (MOE_ROWS - 1)) // MOE_ROWS)
    counts = cnt[0, :n_e].astype(jnp.int32)
    padded = (counts + MOE_ROWS - 1) // MOE_ROWS * MOE_ROWS
    pad_end = jnp.cumsum(padded)
    pad_start = pad_end - padded
    expert = info[:, 0:EXPERT_TOPK].astype(jnp.int32)
    rank = info[:, 4:4 + EXPERT_TOPK].astype(jnp.int32)
    e_ids = jnp.arange(n_e, dtype=jnp.int32)
    dest = (jnp.sum(jnp.where(expert[..., None] == e_ids, pad_start, 0), axis=-1) + rank).reshape(-1)
    block_row0 = jnp.arange(n_blocks, dtype=jnp.int32) * MOE_ROWS
    block_e = jnp.minimum(jnp.sum((pad_end[None, :] <= block_row0[:, None]).astype(jnp.int32), axis=1), n_e - 1)
    n_used = (pad_end[-1:] // MOE_ROWS).astype(jnp.int32)
    row_tok, row_dst = _rowmap(dest, n_blocks * MOE_ROWS)
    ys = _experts(block_e, n_used, row_tok, row_dst, h, w_gate, w_up, w_down)
    return _combine(x1, ys, info, g_final)


def kernel(x, rel_bias, norm_mix, w_in, cmp_pe_k, cmp_w1_k, cmp_w2_k, cmp_pe_v, cmp_w1_v, cmp_w2_v, w_up_nsa,
           w_up_moba, w_out, norm_ffn, w_group, b_group, w_router, b_router, w_exp_gate, w_exp_up, w_exp_down,
           final_norm):
    b, s, d = x.shape
    t = b * s
    depth = w_in.shape[0]
    tab_a = rel_bias[:, :NSA_HEADS]
    tab_b = rel_bias[:, NSA_HEADS:]
    a_cols = NSA_HEADS * HEAD_DIM + 6 * NSA_KV_HEADS * HEAD_DIM
    gate_cols = 3 * NSA_HEADS
    b_cols = 3 * MOBA_HEADS * HEAD_DIM
    xt = x.reshape(t, d)
    out = None
    for l in range(depth):
        h = _rmsnorm(xt, norm_mix[l], BF16)
        wt = jnp.swapaxes(w_in[l], 0, 1)
        b_col0 = a_cols + gate_cols
        proj_a = _matmul(h, wt, 0, a_cols, BF16, scaled_cols=NSA_HEADS * HEAD_DIM, col_scale=Q_SCALE,
                         w_transposed=True, name="in_proj_a").reshape(b, s, a_cols)
        gate_a = _matmul(h, wt, a_cols, LANES, F32, w_transposed=True, tn=LANES, name="in_proj_gate")
        proj_b = _matmul(h, wt, b_col0, b_cols, BF16, scaled_cols=MOBA_HEADS * HEAD_DIM, col_scale=Q_SCALE,
                         w_transposed=True, name="in_proj_b").reshape(b, s, b_cols)
        gm = _matmul(h, wt, b_col0 + b_cols, 2 * d, F32, w_transposed=True, name="in_proj_gm")
        o_c, o_s, o_w = _nsa(proj_a, gate_cols, cmp_pe_k[l], cmp_w1_k[l], cmp_w2_k[l],
                             cmp_pe_v[l], cmp_w1_v[l], cmp_w2_v[l], tab_a, b, s)
        o_b = _moba(proj_b, tab_b, b, s)
        merged = _merge(o_c.reshape(t, -1), o_s.reshape(t, -1), o_w.reshape(t, -1), gate_a,
                        o_b.reshape(t, -1), gm, w_up_nsa[l], w_up_moba[l])
        x1 = _matmul(merged, w_out[l], 0, d, F32, res=xt, name="out_proj")
        assert l == depth - 1, "only the last layer's MoE is fused with the final norm"
        out = _moe(x1, norm_ffn[l], w_group[l], b_group[l], w_router[l], b_router[l],
                   w_exp_gate[l], w_exp_up[l], w_exp_down[l], final_norm)
    return out.reshape(b, s, d)
```

```python
import functools
import math

import numpy as np
import jax
import jax.numpy as jnp
from jax import lax
from jax.experimental import pallas as pl
from jax.experimental.pallas import tpu as pltpu

F32 = jnp.float32
BF16 = jnp.bfloat16

HEAD_DIM = 128
NSA_HEADS = 8
NSA_KV_HEADS = 2
NSA_GROUP = NSA_HEADS // NSA_KV_HEADS
CMP_BLOCK = 32
CMP_STRIDE = 16
SEL_BLOCK = 64
SEL_TOPN = 16
WINDOW = 512
FORCED_SCORE = 1e4
MOBA_HEADS = 8
MOBA_BLOCK = 256
MOBA_TOPK = 3
REL_BUCKETS = 32
REL_MAX_DIST = 128
N_GROUPS = 8
EXPERTS_PER_GROUP = 8
N_EXPERTS = N_GROUPS * EXPERTS_PER_GROUP
EXPERT_TOPK = 2
RMS_EPS = 1e-6

LANES = 128
ATT_TILE = 512
MOE_ROWS = 256
MASK_BIG = 1e30
M_INIT = -3e38
LOG2E = math.log2(math.e)
Q_SCALE = HEAD_DIM ** -0.5 * LOG2E
VMEM_LIMIT = 48 * 1024 * 1024


def _cparams(sem, vmem=VMEM_LIMIT, flags=None):
    return pltpu.CompilerParams(dimension_semantics=sem, vmem_limit_bytes=vmem, flags=flags)


def _rmsnorm_body(x_ref, g_ref, o_ref):
    x = x_ref[...]
    ms = jnp.mean(x * x, axis=-1, keepdims=True)
    o_ref[...] = (x * lax.rsqrt(ms + RMS_EPS) * g_ref[...]).astype(o_ref.dtype)


def _rmsnorm(x, g, out_dtype, tm=512):
    t, d = x.shape
    return pl.pallas_call(
        _rmsnorm_body,
        grid=(t // tm,),
        in_specs=[pl.BlockSpec((tm, d), lambda i: (i, 0)),
                  pl.BlockSpec((1, d), lambda i: (0, 0))],
        out_specs=pl.BlockSpec((tm, d), lambda i: (i, 0)),
        out_shape=jax.ShapeDtypeStruct((t, d), out_dtype),
        compiler_params=_cparams(("parallel",)),
        name="rmsnorm",
    )(x, g.reshape(1, d))


def _mm_body(*refs, has_res, n_scaled, col_scale, w_transposed):
    if has_res:
        a_ref, w_ref, r_ref, o_ref = refs
    else:
        a_ref, w_ref, o_ref = refs
    w = w_ref[...].astype(BF16)
    if w_transposed:
        acc = lax.dot_general(a_ref[...], w, (((1,), (1,)), ((), ())), preferred_element_type=F32)
    else:
        acc = jnp.dot(a_ref[...], w, preferred_element_type=F32)
    if n_scaled:
        acc = acc * jnp.where(pl.program_id(1) < n_scaled, col_scale, 1.0)
    if has_res:
        acc = acc + r_ref[...]
    o_ref[...] = acc.astype(o_ref.dtype)


def _matmul(a, w, col0, ncols, out_dtype, res=None, scaled_cols=0, col_scale=1.0, w_transposed=False,
            tm=2048, tn=512, name="matmul"):
    t, k = a.shape
    tn = min(tn, ncols)
    tm = min(tm, t)
    assert ncols % tn == 0 and t % tm == 0 and scaled_cols % tn == 0
    if w_transposed:
        assert col0 % 8 == 0
        w_spec = pl.BlockSpec((pl.Element(tn), pl.Element(k)),
                              lambda i, j: (pl.multiple_of(col0 + j * tn, 8), 0))
    else:
        assert col0 % tn == 0
        off = col0 // tn
        w_spec = pl.BlockSpec((k, tn), lambda i, j: (0, j + off))
    in_specs = [pl.BlockSpec((tm, k), lambda i, j: (i, 0)), w_spec]
    args = [a, w]
    if res is not None:
        in_specs.append(pl.BlockSpec((tm, tn), lambda i, j: (i, j)))
        args.append(res)
    return pl.pallas_call(
        functools.partial(_mm_body, has_res=res is not None, n_scaled=scaled_cols // tn, col_scale=col_scale,
                          w_transposed=w_transposed),
        grid=(t // tm, ncols // tn),
        in_specs=in_specs,
        out_specs=pl.BlockSpec((tm, tn), lambda i, j: (i, j)),
        out_shape=jax.ShapeDtypeStruct((t, ncols), out_dtype),
        compiler_params=_cparams(("parallel", "parallel")),
        name=name,
    )(*args)


def _compress_body(u_ref, pe_ref, w1_ref, w2_ref, o_ref, *, nc):
    u = u_ref[...]
    w1 = w1_ref[...].astype(BF16)
    half = u.shape[1]
    a = jnp.dot(u, w1[:half], preferred_element_type=F32)
    b = jnp.dot(u, w1[half:], preferred_element_type=F32)
    peb = jnp.dot(pe_ref[...].astype(BF16), w1, preferred_element_type=F32)[0:1]
    pre = a + pltpu.roll(b, nc - 1, 0) + peb
    hid = jax.nn.gelu(pre)
    o_ref[...] = jnp.dot(hid.astype(BF16), w2_ref[...].astype(BF16),
                         preferred_element_type=F32).astype(o_ref.dtype)


def _compress(u, pe, w1, w2):
    bg, nc, kk = u.shape
    hid = w1.shape[1]
    dh = w2.shape[1]
    pe8 = jnp.broadcast_to(pe.reshape(1, -1), (16, pe.size))
    return pl.pallas_call(
        functools.partial(_compress_body, nc=nc),
        grid=(bg,),
        in_specs=[pl.BlockSpec((None, nc, kk), lambda i: (i, 0, 0)),
                  pl.BlockSpec((16, 2 * kk), lambda i: (0, 0)),
                  pl.BlockSpec((2 * kk, hid), lambda i: (0, 0)),
                  pl.BlockSpec((hid, dh), lambda i: (0, 0))],
        out_specs=pl.BlockSpec((None, nc, dh), lambda i: (i, 0, 0)),
        out_shape=jax.ShapeDtypeStruct((bg, nc, dh), BF16),
        compiler_params=_cparams(("parallel",)),
        name="nsa_compress",
    )(u, pe8, w1, w2)


def _split3(x):
    p1 = x.astype(BF16)
    r = x - p1.astype(F32)
    p2 = r.astype(BF16)
    p3 = (r - p2.astype(F32)).astype(BF16)
    return p1, p2, p3


def _rank_count(score, n_rows):
    groups = []
    for g0 in range(0, n_rows, 8):
        sg = score[g0:min(g0 + 8, n_rows), :]
        n_iota = g0 + lax.broadcasted_iota(jnp.int32, sg.shape, 0)
        cnt = jnp.zeros(sg.shape, F32)
        for m in range(n_rows):
            row = score[m:m + 1, :]
            if m < g0:
                beats = row >= sg
            elif m >= g0 + 8:
                beats = row > sg
            else:
                tie = jnp.where(n_iota > m, 1.0, 0.0)
                beats = jnp.where(row > sg, 1.0, jnp.where(row == sg, tie, 0.0)) > 0.5
            cnt = cnt + jnp.where(beats, 1.0, 0.0)
        groups.append(cnt)
    return jnp.concatenate(groups, axis=0) if len(groups) > 1 else groups[0]


def _nsa_cmp_body(q_ref, kc_ref, vc_ref, bias_ref, ov_ref, oc_ref, mem_ref, *, tq, nc, n_sel):
    t0 = pl.program_id(2) * tq
    kc = kc_ref[...]
    vc = vc_ref[...]
    t_idx = t0 + lax.broadcasted_iota(jnp.int32, (tq, nc), 0)
    c_idx = lax.broadcasted_iota(jnp.int32, (tq, nc), 1)
    dist = t_idx - (c_idx * CMP_STRIDE + (CMP_BLOCK - 1))
    valid = dist >= 0
    n_k = REL_MAX_DIST // CMP_STRIDE
    kk = dist // CMP_STRIDE
    k_is = [kk == k for k in range(n_k)]
    psum = jnp.zeros((tq, nc), F32)
    for j in range(NSA_GROUP):
        hs = slice(j * HEAD_DIM, (j + 1) * HEAD_DIM)
        gt = bias_ref[j] * LOG2E
        bias = jnp.broadcast_to(gt[:, n_k:n_k + 1], (tq, nc))
        for k in range(n_k):
            bias = jnp.where(k_is[k], gt[:, k:k + 1], bias)
        s = lax.dot_general(q_ref[:, hs], kc, (((1,), (1,)), ((), ())), preferred_element_type=F32) + bias
        s = jnp.where(valid, s, -MASK_BIG)
        m = jnp.max(s, axis=-1, keepdims=True)
        m = jnp.where(m > -0.5 * MASK_BIG, m, 0.0)
        p = jnp.where(valid, jnp.exp2(s - m), 0.0)
        d = jnp.sum(p, axis=-1, keepdims=True)
        p = p / jnp.where(d > 0, d, 1.0)
        oc_ref[:, hs] = jnp.dot(p.astype(BF16), vc, preferred_element_type=F32).astype(oc_ref.dtype)
        psum = psum + p
    ov = ov_ref[...]
    nt = (((1,), (1,)), ((), ()))
    p1, p2, p3 = _split3(psum)
    psel = (lax.dot_general(ov, p1, nt, preferred_element_type=F32)
            + lax.dot_general(ov, p2, nt, preferred_element_type=F32)
            + lax.dot_general(ov, p3, nt, preferred_element_type=F32))
    n_idx = lax.broadcasted_iota(jnp.int32, (n_sel, tq), 0)
    tt = t0 + lax.broadcasted_iota(jnp.int32, (n_sel, tq), 1)
    cur = tt // SEL_BLOCK
    forced = jnp.where(n_idx == 0, 1.0, jnp.where(n_idx == cur, 1.0, jnp.where(n_idx == cur - 1, 1.0, 0.0)))
    score = jnp.where(forced > 0.5, FORCED_SCORE, jnp.where(n_idx * SEL_BLOCK <= tt, psel, -1.0))
    cnt = _rank_count(score, n_sel)
    member = jnp.where(cnt < float(min(SEL_TOPN, n_sel)), 1.0, 0.0)
    if n_sel < LANES:
        member = jnp.concatenate([member, jnp.zeros((LANES - n_sel, tq), F32)], axis=0)
    mem_ref[...] = member.T.astype(mem_ref.dtype)


def _nsa_cmp(proj, kc, vc, tab, overlap, b, s, tq=256):
    g = NSA_KV_HEADS
    nc = kc.shape[2]
    n_sel = s // SEL_BLOCK
    gw = NSA_GROUP * HEAD_DIM
    assert tq % CMP_STRIDE == 0 and REL_MAX_DIST % CMP_STRIDE == 0
    n_k = REL_MAX_DIST // CMP_STRIDE
    rho = (np.arange(tq)[:, None] - (CMP_BLOCK - 1)) % CMP_STRIDE
    dd = np.concatenate([rho + CMP_STRIDE * np.arange(n_k)[None, :], np.full((tq, 1), REL_MAX_DIST)], axis=1)
    gtab = jnp.transpose(tab[_rel_bucket(dd)], (2, 0, 1)).astype(F32)
    bias_c = jnp.pad(gtab, ((0, 0), (0, 0), (0, LANES - n_k - 1)))
    body = functools.partial(_nsa_cmp_body, tq=tq, nc=nc, n_sel=n_sel)
    return pl.pallas_call(
        body,
        grid=(b, g, s // tq),
        in_specs=[pl.BlockSpec((None, tq, gw), lambda bi, gi, i: (bi, i, gi)),
                  pl.BlockSpec((None, None, nc, HEAD_DIM), lambda bi, gi, i: (bi, gi, 0, 0)),
                  pl.BlockSpec((None, None, nc, HEAD_DIM), lambda bi, gi, i: (bi, gi, 0, 0)),
                  pl.BlockSpec((NSA_GROUP, tq, LANES), lambda bi, gi, i: (gi, 0, 0)),
                  pl.BlockSpec((n_sel, nc), lambda bi, gi, i: (0, 0))],
        out_specs=[pl.BlockSpec((None, tq, gw), lambda bi, gi, i: (bi, i, gi)),
                   pl.BlockSpec((None, None, tq, LANES), lambda bi, gi, i: (bi, gi, i, 0))],
        out_shape=[jax.ShapeDtypeStruct((b, s, NSA_HEADS * HEAD_DIM), BF16),
                   jax.ShapeDtypeStruct((b, g, s, LANES), BF16)],
        compiler_params=_cparams(("parallel", "parallel", "parallel")),
        name="nsa_cmp_select",
    )(proj, kc, vc, bias_c, overlap)


def _moba_gate_body(q_ref, k_ref, mem_ref, *, s, nblk):
    k = k_ref[...].astype(F32)
    kmean = jnp.mean(k.reshape(nblk, MOBA_BLOCK, HEAD_DIM), axis=1)
    k1 = kmean.astype(BF16)
    k2 = (kmean - k1.astype(F32)).astype(BF16)
    q = q_ref[...]
    nt = (((1,), (1,)), ((), ()))
    gate = (lax.dot_general(k1, q, nt, preferred_element_type=F32)
            + lax.dot_general(k2, q, nt, preferred_element_type=F32))
    n_idx = lax.broadcasted_iota(jnp.int32, (nblk, s), 0)
    own = lax.broadcasted_iota(jnp.int32, (nblk, s), 1) // MOBA_BLOCK
    past = n_idx < own
    score = jnp.where(past, gate, -MASK_BIG)
    cnt = _rank_count(score, nblk)
    n_top = max(1, min(MOBA_TOPK, nblk - 1))
    sel = jnp.where(past, jnp.where(cnt < float(n_top), 1.0, 0.0), 0.0)
    member = jnp.where(n_idx == own, 1.0, sel)
    member = jnp.concatenate([member, jnp.zeros((LANES - nblk, s), F32)], axis=0)
    mem_ref[...] = member.T.astype(mem_ref.dtype)


def _moba_gate(proj, b, s):
    h = MOBA_HEADS
    nblk = s // MOBA_BLOCK
    return pl.pallas_call(
        functools.partial(_moba_gate_body, s=s, nblk=nblk),
        grid=(b, h),
        in_specs=[pl.BlockSpec((None, s, HEAD_DIM), lambda bi, hi: (bi, 0, hi)),
                  pl.BlockSpec((None, s, HEAD_DIM), lambda bi, hi: (bi, 0, h + hi))],
        out_specs=pl.BlockSpec((None, None, s, LANES), lambda bi, hi: (bi, hi, 0, 0)),
        out_shape=jax.ShapeDtypeStruct((b, h, s, LANES), BF16),
        compiler_params=_cparams(("parallel", "parallel")),
        name="moba_gate",
    )(proj, proj)


def _flash_body(qi_ref, ki_ref, bo_ref, fl_ref, *refs, nh, ratio, nm, n_near, has_far):
    if nm:
        (q_ref, k_ref, v_ref, bvec_ref, mem_ref, et_ref, o_ref,
         m_ref, l_ref, acc_ref, sh_ref, al_ref, bias_ref, s_ref, p_ref) = refs
    else:
        q_ref, k_ref, v_ref, bvec_ref, o_ref, m_ref, l_ref, acc_ref, sh_ref, al_ref, bias_ref, s_ref, p_ref = refs
    del qi_ref, ki_ref
    p = pl.program_id(2)
    flag = fl_ref[p]
    bo = bo_ref[p]
    t = q_ref.shape[0]
    rows = 64

    @pl.when(p == 0)
    def _():
        for h in range(nh):
            for o in range(n_near):
                vec = bvec_ref[h, o][0:1, :] * LOG2E
                for rc in range(t // rows):
                    x = pltpu.roll(jnp.broadcast_to(vec, (rows, 2 * t)), rc * rows, 1, stride=1, stride_axis=0)
                    bias_ref[h, o, rc * rows:(rc + 1) * rows, :] = x[:, :t]

    @pl.when((flag & 1) != 0)
    def _():
        m_ref[...] = jnp.full(m_ref.shape, M_INIT, F32)
        l_ref[...] = jnp.zeros(l_ref.shape, F32)
        acc_ref[...] = jnp.zeros(acc_ref.shape, F32)

    nt = (((1,), (1,)), ((), ()))
    reps = t // LANES

    def step(near):
        for h in range(nh):
            hs = slice(h * HEAD_DIM, (h + 1) * HEAD_DIM)
            kv = h // ratio
            ks = slice(kv * HEAD_DIM, (kv + 1) * HEAD_DIM)
            q = q_ref[:, hs]
            k = k_ref[:, ks]
            if nm:
                mneg = mem_ref[h // (nh // nm)] - 1.0
                q = jnp.concatenate([q, mneg.astype(BF16)], axis=1)
                k = jnp.concatenate([k, et_ref[...]], axis=1)
            sc = lax.dot_general(q, k, nt, preferred_element_type=F32)
            m_prev = m_ref[h]
            if near:
                sc = sc + bias_ref[h, bo]
                m_new = jnp.maximum(m_prev, jnp.max(sc, axis=-1, keepdims=True))
                sh_ref[h] = m_new
            else:
                cfar = bvec_ref[h, n_near][0:1, 0:LANES] * LOG2E
                m_new = jnp.maximum(m_prev, jnp.max(sc, axis=-1, keepdims=True) + cfar)
                sh_ref[h] = m_new - cfar
            s_ref[h] = sc
            al_ref[h] = jnp.exp2(m_prev - m_new)
            m_ref[h] = m_new
        for h in range(nh):
            ks = slice(h // ratio * HEAD_DIM, (h // ratio + 1) * HEAD_DIM)
            for rc in range(t // rows):
                rs = slice(rc * rows, (rc + 1) * rows)
                pm = jnp.exp2(s_ref[h, rs, :] - jnp.tile(sh_ref[h, rs, :], (1, reps)))
                l_ref[h, rs, :] = al_ref[h, rs, :] * l_ref[h, rs, :] + jnp.sum(pm, axis=-1, keepdims=True)
                p_ref[h, rs, :] = pm.astype(BF16)
            acc_ref[h] = al_ref[h] * acc_ref[h] + jnp.dot(p_ref[h], v_ref[:, ks], preferred_element_type=F32)

    if has_far:
        pl.when(bo < n_near)(lambda: step(True))
        pl.when(bo >= n_near)(lambda: step(False))
    else:
        step(True)

    @pl.when((flag & 2) != 0)
    def _():
        for h in range(nh):
            l = l_ref[h]
            o_ref[:, h * HEAD_DIM:(h + 1) * HEAD_DIM] = (acc_ref[h] / jnp.where(l > 0, l, 1.0)).astype(o_ref.dtype)


def _flash(q_arr, q_off, k_arr, k_off, v_arr, v_off, bias, n_heads, ratio, nh, pairs, has_far, member=None,
           et=None, name="flash"):
    b, s, _ = q_arr.shape
    t = ATT_TILE
    nkv = nh // ratio
    ng = n_heads // nh
    qi = jnp.asarray([p[0] for p in pairs], jnp.int32)
    ki = jnp.asarray([p[1] for p in pairs], jnp.int32)
    bo = jnp.asarray([p[2] for p in pairs], jnp.int32)
    fl = jnp.asarray([p[3] for p in pairs], jnp.int32)
    nb = bias.shape[1]
    nm = 0
    in_specs = [
        pl.BlockSpec((None, t, nh * HEAD_DIM), lambda bi, gi, p, qi, ki, bo, fl: (bi, qi[p], q_off + gi)),
        pl.BlockSpec((None, t, nkv * HEAD_DIM), lambda bi, gi, p, qi, ki, bo, fl: (bi, ki[p], k_off + gi)),
        pl.BlockSpec((None, t, nkv * HEAD_DIM), lambda bi, gi, p, qi, ki, bo, fl: (bi, ki[p], v_off + gi)),
        pl.BlockSpec((nh, nb, 8, 2 * t), lambda bi, gi, p, qi, ki, bo, fl: (gi, 0, 0, 0)),
    ]
    args = [q_arr, k_arr, v_arr, bias]
    if member is not None:
        nm = member.shape[1] // ng
        in_specs += [
            pl.BlockSpec((None, nm, t, LANES), lambda bi, gi, p, qi, ki, bo, fl: (bi, gi, qi[p], 0)),
            pl.BlockSpec((t, LANES), lambda bi, gi, p, qi, ki, bo, fl: (ki[p], 0)),
        ]
        args += [member, et]
    n_near = nb - 1 if has_far else nb
    body = functools.partial(_flash_body, nh=nh, ratio=ratio, nm=nm, n_near=n_near, has_far=has_far)
    return pl.pallas_call(
        body,
        grid_spec=pltpu.PrefetchScalarGridSpec(
            num_scalar_prefetch=4,
            grid=(b, ng, len(pairs)),
            in_specs=in_specs,
            out_specs=pl.BlockSpec((None, t, nh * HEAD_DIM), lambda bi, gi, p, qi, ki, bo, fl: (bi, qi[p], gi)),
            scratch_shapes=[pltpu.VMEM((nh, t, LANES), F32)] * 5 + [pltpu.VMEM((nh, n_near, t, t), F32),
                                                                    pltpu.VMEM((nh, t, t), F32),
                                                                    pltpu.VMEM((nh, t, t), BF16)],
        ),
        out_shape=jax.ShapeDtypeStruct((b, s, n_heads * HEAD_DIM), BF16),
        compiler_params=_cparams(("parallel", "parallel", "arbitrary")),
        name=name,
    )(qi, ki, bo, fl, *args)


def _rel_bucket(dist):
    n = jnp.maximum(jnp.asarray(dist, jnp.int32), 0)
    max_exact = REL_BUCKETS // 2
    nf = jnp.maximum(n, 1).astype(jnp.float32)
    large = max_exact + (jnp.log(nf / max_exact) / math.log(REL_MAX_DIST / max_exact)
                         * (REL_BUCKETS - max_exact)).astype(jnp.int32)
    return jnp.where(n < max_exact, n, jnp.minimum(large, REL_BUCKETS - 1))


def _n_near(t):
    return -(-(REL_MAX_DIST - 1 + t) // t)


def _bias_vecs(tab, t, n_off, window=None):
    k = np.arange(2 * t)[None, :]
    dist = np.arange(n_off)[:, None] * t + np.where(k < t, -k, 2 * t - k)
    ok = dist >= 0
    if window is not None:
        ok &= dist < window
    bias = jnp.where(jnp.asarray(ok)[..., None], tab[_rel_bucket(dist)], -MASK_BIG)
    bias = jnp.transpose(bias, (2, 0, 1)).astype(F32)
    return jnp.broadcast_to(bias[:, :, None, :], (bias.shape[0], n_off, 8, 2 * t))


def _causal_pairs(nq, n_near):
    pairs = []
    for qi in range(nq):
        for ki in range(qi + 1):
            pairs.append((qi, ki, min(qi - ki, n_near), (1 if ki == 0 else 0) | (2 if ki == qi else 0)))
    return pairs


def _window_pairs(nq, n_back):
    pairs = []
    for qi in range(nq):
        lo = max(0, qi - n_back)
        for ki in range(lo, qi + 1):
            pairs.append((qi, ki, qi - ki, (1 if ki == lo else 0) | (2 if ki == qi else 0)))
    return pairs


def _block_onehot(s, blk):
    return jnp.asarray(np.where(np.arange(s)[:, None] // blk == np.arange(LANES)[None, :], MASK_BIG, 0.0), BF16)


def _merge_body(oc_ref, os_ref, ow_ref, gl_ref, ob_ref, gma_ref, gmb_ref, wa_ref, wb_ref, o_ref, oa_ref):
    @pl.when(pl.program_id(1) == 0)
    def _():
        gates = jax.nn.sigmoid(gl_ref[...])
        for h in range(NSA_HEADS):
            hs = slice(h * HEAD_DIM, (h + 1) * HEAD_DIM)
            mix = (gates[:, 3 * h:3 * h + 1] * oc_ref[:, hs].astype(F32)
                   + gates[:, 3 * h + 1:3 * h + 2] * os_ref[:, hs].astype(F32)
                   + gates[:, 3 * h + 2:3 * h + 3] * ow_ref[:, hs].astype(F32))
            oa_ref[:, hs] = mix.astype(BF16)

    ya = jnp.dot(oa_ref[...], wa_ref[...].astype(BF16), preferred_element_type=F32)
    yb = jnp.dot(ob_ref[...], wb_ref[...].astype(BF16), preferred_element_type=F32)
    o_ref[...] = (jax.nn.sigmoid(gma_ref[...]) * ya + jax.nn.sigmoid(gmb_ref[...]) * yb).astype(o_ref.dtype)


def _merge(o_c, o_s, o_w, gate_logits, o_b, gm, w_up_a, w_up_b, tm=1024, tn=512):
    t, ka = o_c.shape
    kb = o_b.shape[1]
    d = w_up_a.shape[1]
    tm = min(tm, t)
    nj = d // tn
    row = lambda i, j: (i, 0)
    return pl.pallas_call(
        _merge_body,
        grid=(t // tm, nj),
        in_specs=[pl.BlockSpec((tm, ka), row), pl.BlockSpec((tm, ka), row), pl.BlockSpec((tm, ka), row),
                  pl.BlockSpec((tm, LANES), row), pl.BlockSpec((tm, kb), row),
                  pl.BlockSpec((tm, tn), lambda i, j: (i, j)),
                  pl.BlockSpec((tm, tn), lambda i, j: (i, j + nj)),
                  pl.BlockSpec((ka, tn), lambda i, j: (0, j)),
                  pl.BlockSpec((kb, tn), lambda i, j: (0, j))],
        out_specs=pl.BlockSpec((tm, tn), lambda i, j: (i, j)),
        out_shape=jax.ShapeDtypeStruct((t, d), BF16),
        scratch_shapes=[pltpu.VMEM((tm, ka), BF16)],
        compiler_params=_cparams(("parallel", "arbitrary")),
        name="merge_up",
    )(o_c, o_s, o_w, gate_logits, o_b, gm, gm, w_up_a, w_up_b)


def _route_body(x_ref, g_ref, w_ref, b_ref, h_ref, info_ref, cnt_ref, carry_ref, *, tm):
    @pl.when(pl.program_id(0) == 0)
    def _():
        carry_ref[...] = jnp.zeros(carry_ref.shape, F32)

    x = x_ref[...]
    ms = jnp.mean(x * x, axis=-1, keepdims=True)
    h = x * lax.rsqrt(ms + RMS_EPS) * g_ref[...]
    _store_packed(h_ref, h, tm)
    w = w_ref[...]
    h1 = h.astype(BF16)
    h2 = (h - h1.astype(F32)).astype(BF16)
    w1 = w.astype(BF16)
    w2 = (w - w1.astype(F32)).astype(BF16)
    logits = (jnp.dot(h1, w1, preferred_element_type=F32) + jnp.dot(h1, w2, preferred_element_type=F32)
              + jnp.dot(h2, w1, preferred_element_type=F32)) + b_ref[...]
    lane = lax.broadcasted_iota(jnp.int32, (tm, LANES), 1)
    lanef = lane.astype(F32)

    is_g = lane < N_GROUPS
    gl = jnp.where(is_g, logits, -MASK_BIG)
    ge = jnp.where(is_g, jnp.exp(gl - jnp.max(gl, axis=-1, keepdims=True)), 0.0)
    gp = ge / jnp.sum(ge, axis=-1, keepdims=True)
    g_val = jnp.max(gp, axis=-1, keepdims=True)
    g_idx = jnp.min(jnp.where(gp == g_val, lanef, float(LANES)), axis=-1, keepdims=True)

    lane_grp = ((lane - N_GROUPS) // EXPERTS_PER_GROUP).astype(F32)
    in_e = jnp.where(lane >= N_GROUPS, jnp.where(lane < N_GROUPS + N_EXPERTS, 1.0, 0.0), 0.0)
    is_e = jnp.where(lane_grp == g_idx, in_e, 0.0) > 0.5
    el = jnp.where(is_e, logits, -MASK_BIG)
    ee = jnp.where(is_e, jnp.exp(el - jnp.max(el, axis=-1, keepdims=True)), 0.0)
    ep = jnp.where(is_e, ee / jnp.sum(ee, axis=-1, keepdims=True), -1.0)
    v1 = jnp.max(ep, axis=-1, keepdims=True)
    l1 = jnp.min(jnp.where(ep == v1, lanef, float(LANES)), axis=-1, keepdims=True)
    ep2 = jnp.where(lanef == l1, -1.0, ep)
    v2 = jnp.max(ep2, axis=-1, keepdims=True)
    l2 = jnp.min(jnp.where(ep2 == v2, lanef, float(LANES)), axis=-1, keepdims=True)
    vs = v1 + v2
    wt1 = g_val * v1 / vs
    wt2 = g_val * v2 / vs
    e1 = l1 - float(N_GROUPS)
    e2 = l2 - float(N_GROUPS)

    oh = jnp.where(lanef == e1, 1.0, jnp.where(lanef == e2, 1.0, 0.0))
    r_i = lax.broadcasted_iota(jnp.int32, (tm, tm), 0)
    c_i = lax.broadcasted_iota(jnp.int32, (tm, tm), 1)
    tri = jnp.where(r_i > c_i, 1.0, 0.0).astype(BF16)
    base = jnp.dot(tri, oh.astype(BF16), preferred_element_type=F32) + carry_ref[...]
    r1 = jnp.sum(jnp.where(lanef == e1, base, 0.0), axis=-1, keepdims=True)
    r2 = jnp.sum(jnp.where(lanef == e2, base, 0.0), axis=-1, keepdims=True)
    carry_ref[...] = carry_ref[...] + jnp.sum(oh, axis=0, keepdims=True)
    cnt_ref[...] = jnp.broadcast_to(carry_ref[...], cnt_ref.shape)
    info = jnp.where(lane == 0, e1, jnp.where(lane == 1, e2, jnp.where(lane == 2, wt1, jnp.where(
        lane == 3, wt2, jnp.where(lane == 4, r1, jnp.where(lane == 5, r2, 0.0))))))
    info_ref[...] = info


def _route(x1, g, w_gr, b_gr, tm=512):
    t, d = x1.shape
    return pl.pallas_call(
        functools.partial(_route_body, tm=tm),
        grid=(t // tm,),
        in_specs=[pl.BlockSpec((tm, d), lambda i: (i, 0)),
                  pl.BlockSpec((1, d), lambda i: (0, 0)),
                  pl.BlockSpec((d, LANES), lambda i: (0, 0)),
                  pl.BlockSpec((1, LANES), lambda i: (0, 0))],
        out_specs=[pl.BlockSpec((tm * ROW_SUB, LANES), lambda i: (i, 0)),
                   pl.BlockSpec((tm, LANES), lambda i: (i, 0)),
                   pl.BlockSpec((8, LANES), lambda i: (0, 0))],
        out_shape=[jax.ShapeDtypeStruct((t * ROW_SUB, LANES), jnp.uint32),
                   jax.ShapeDtypeStruct((t, LANES), F32),
                   jax.ShapeDtypeStruct((8, LANES), F32)],
        scratch_shapes=[pltpu.VMEM((1, LANES), F32)],
        compiler_params=_cparams(("arbitrary",)),
        name="moe_route",
    )(x1, g.reshape(1, d), w_gr, b_gr)


ROW_SUB = 8
U32 = jnp.uint32


def _pack_pairs(lo, hi):
    lo_b = lax.bitcast_convert_type(lo.astype(BF16).astype(F32), U32)
    hi_b = lax.bitcast_convert_type(hi.astype(BF16).astype(F32), U32)
    return lax.shift_right_logical(lo_b, U32(16)) | (hi_b & U32(0xFFFF0000))


def _unpack_pairs(w):
    lo = lax.bitcast_convert_type(lax.shift_left(w, U32(16)), F32)
    hi = lax.bitcast_convert_type(w & U32(0xFFFF0000), F32)
    return lo, hi


def _store_packed(ref, y, n):
    half = y.shape[1] // 2
    for s in range(ROW_SUB):
        cs = slice(s * LANES, (s + 1) * LANES)
        ref[pl.ds(s, n, stride=ROW_SUB), :] = _pack_pairs(y[:, cs], y[:, half + s * LANES:half + (s + 1) * LANES])


def _load_packed(ref, n):
    los, his = [], []
    for s in range(ROW_SUB):
        lo, hi = _unpack_pairs(ref[pl.ds(s, n, stride=ROW_SUB), :])
        los.append(lo)
        his.append(hi)
    return jnp.concatenate(los + his, axis=1)


def _row_copy(src_ref, src_row, dst_ref, dst_row, sem):
    return pltpu.make_async_copy(src_ref.at[pl.ds(pl.multiple_of(src_row * ROW_SUB, ROW_SUB), ROW_SUB)],
                                 dst_ref.at[pl.ds(pl.multiple_of(dst_row * ROW_SUB, ROW_SUB), ROW_SUB)], sem)


def _dispatch_body(dest_ref, h_ref, xs_in_ref, xs_ref, sem, *, tm):
    del xs_in_ref
    base = pl.program_id(0) * tm

    def issue(r, c):
        for k in range(EXPERT_TOPK):
            _row_copy(h_ref, r, xs_ref, dest_ref[EXPERT_TOPK * (base + r) + k], sem).start()
        return c

    lax.fori_loop(0, tm, issue, 0, unroll=4)

    def drain(r, c):
        for k in range(EXPERT_TOPK):
            _row_copy(h_ref, r, xs_ref, dest_ref[EXPERT_TOPK * (base + r) + k], sem).wait()
        return c

    lax.fori_loop(0, tm, drain, 0, unroll=4)


def _dispatch(dest, hp, n_rows, tm=512):
    t = hp.shape[0] // ROW_SUB
    xs0 = jnp.zeros((n_rows * ROW_SUB, LANES), U32)
    return pl.pallas_call(
        functools.partial(_dispatch_body, tm=tm),
        grid_spec=pltpu.PrefetchScalarGridSpec(
            num_scalar_prefetch=1,
            grid=(t // tm,),
            in_specs=[pl.BlockSpec((tm * ROW_SUB, LANES), lambda i, dest: (i, 0)),
                      pl.BlockSpec(memory_space=pl.ANY)],
            out_specs=pl.BlockSpec(memory_space=pl.ANY),
            scratch_shapes=[pltpu.SemaphoreType.DMA(())],
        ),
        out_shape=jax.ShapeDtypeStruct((n_rows * ROW_SUB, LANES), U32),
        input_output_aliases={2: 0},
        compiler_params=_cparams(("arbitrary",)),
        name="moe_dispatch",
    )(dest, hp, xs0)


def _expert_body(be_ref, nu_ref, x_ref, wg_ref, wu_ref, wd_ref, y_ref, wg_s, wu_s, wd_s):
    i = pl.program_id(0)
    e = be_ref[i]
    prev = be_ref[jnp.maximum(i - 1, 0)]

    @pl.when((i == 0) | (e != prev))
    def _():
        wg_s[...] = wg_ref[...].astype(BF16)
        wu_s[...] = wu_ref[...].astype(BF16)
        wd_s[...] = wd_ref[...].astype(BF16)

    @pl.when(i < nu_ref[0])
    def _():
        x = _load_packed(x_ref, MOE_ROWS).astype(BF16)
        g = jnp.dot(x, wg_s[...], preferred_element_type=F32)
        u = jnp.dot(x, wu_s[...], preferred_element_type=F32)
        mid = (jax.nn.silu(g) * u).astype(BF16)
        _store_packed(y_ref, jnp.dot(mid, wd_s[...], preferred_element_type=F32), MOE_ROWS)

    @pl.when(i >= nu_ref[0])
    def _():
        y_ref[...] = jnp.zeros(y_ref.shape, y_ref.dtype)


def _experts(block_e, n_used, xs, w_gate, w_up, w_down):
    n_blocks = xs.shape[0] // (MOE_ROWS * ROW_SUB)
    _, d, ff = w_gate.shape
    assert d == 2 * ROW_SUB * LANES, "a packed row must be exactly one (8,128) tile"
    blk = lambda i, be, nu: (jnp.minimum(i, nu[0] - 1), 0)
    wsel = lambda i, be, nu: (be[i], 0, 0)
    return pl.pallas_call(
        _expert_body,
        grid_spec=pltpu.PrefetchScalarGridSpec(
            num_scalar_prefetch=2,
            grid=(n_blocks,),
            in_specs=[pl.BlockSpec((MOE_ROWS * ROW_SUB, LANES), blk),
                      pl.BlockSpec((None, d, ff), wsel),
                      pl.BlockSpec((None, d, ff), wsel),
                      pl.BlockSpec((None, ff, d), wsel)],
            out_specs=pl.BlockSpec((MOE_ROWS * ROW_SUB, LANES), lambda i, be, nu: (i, 0)),
            scratch_shapes=[pltpu.VMEM((d, ff), BF16), pltpu.VMEM((d, ff), BF16), pltpu.VMEM((ff, d), BF16)],
        ),
        out_shape=jax.ShapeDtypeStruct(xs.shape, U32),
        compiler_params=_cparams(("arbitrary",)),
        name="moe_experts",
    )(block_e, n_used, xs, w_gate, w_up, w_down)


def _combine_body(dest_ref, x_ref, info_ref, g_ref, ys_ref, o_ref, buf0, buf1, sem, *, tm):
    base = pl.program_id(0) * tm
    bufs = (buf0, buf1)

    def issue(r, c):
        for k in range(EXPERT_TOPK):
            _row_copy(ys_ref, dest_ref[EXPERT_TOPK * (base + r) + k], bufs[k], r, sem).start()
        return c

    lax.fori_loop(0, tm, issue, 0, unroll=4)

    def drain(r, c):
        for k in range(EXPERT_TOPK):
            _row_copy(ys_ref, dest_ref[EXPERT_TOPK * (base + r) + k], bufs[k], r, sem).wait()
        return c

    lax.fori_loop(0, tm, drain, 0, unroll=4)
    info = info_ref[...]
    y = x_ref[...] + (info[:, 2:3] * _load_packed(buf0, tm) + info[:, 3:4] * _load_packed(buf1, tm))
    ms = jnp.mean(y * y, axis=-1, keepdims=True)
    o_ref[...] = y * lax.rsqrt(ms + RMS_EPS) * g_ref[...]


def _combine(dest, x1, info, g, ys, tm=256):
    t, d = x1.shape
    return pl.pallas_call(
        functools.partial(_combine_body, tm=tm),
        grid_spec=pltpu.PrefetchScalarGridSpec(
            num_scalar_prefetch=1,
            grid=(t // tm,),
            in_specs=[pl.BlockSpec((tm, d), lambda i, dest: (i, 0)),
                      pl.BlockSpec((tm, LANES), lambda i, dest: (i, 0)),
                      pl.BlockSpec((1, d), lambda i, dest: (0, 0)),
                      pl.BlockSpec(memory_space=pl.ANY)],
            out_specs=pl.BlockSpec((tm, d), lambda i, dest: (i, 0)),
            scratch_shapes=[pltpu.VMEM((tm * ROW_SUB, LANES), U32), pltpu.VMEM((tm * ROW_SUB, LANES), U32),
                            pltpu.SemaphoreType.DMA(())],
        ),
        out_shape=jax.ShapeDtypeStruct((t, d), F32),
        compiler_params=_cparams(("arbitrary",)),
        name="moe_combine",
    )(dest, x1, info, g.reshape(1, d), ys)


def _nsa(proj, gate_cols, pe_k, w1_k, w2_k, pe_v, w1_v, w2_v, tab, b, s):
    del gate_cols
    g, dh = NSA_KV_HEADS, HEAD_DIM
    qw = NSA_HEADS * dh
    nc = s // CMP_STRIDE

    def blocks16(col0):
        a = proj[:, :, col0:col0 + g * dh].reshape(b, nc, CMP_STRIDE, g, dh)
        return a.transpose(0, 3, 1, 2, 4).reshape(b * g, nc, CMP_STRIDE * dh)

    kc = _compress(blocks16(qw), pe_k, w1_k, w2_k).reshape(b, g, nc, dh)
    vc = _compress(blocks16(qw + g * dh), pe_v, w1_v, w2_v).reshape(b, g, nc, dh)

    c_start = np.arange(nc)[None, :] * CMP_STRIDE
    n_sel = s // SEL_BLOCK
    sb = np.arange(n_sel)[:, None] * SEL_BLOCK
    overlap = jnp.asarray((c_start < sb + SEL_BLOCK) & (c_start + CMP_BLOCK > sb), BF16)
    o_c, member = _nsa_cmp(proj, kc, vc, tab, overlap, b, s)

    t = ATT_TILE
    nq = s // t
    nn = _n_near(t)
    kblk = qw // dh
    bias_d = _bias_vecs(tab, t, nn + 1)
    o_s = _flash(proj, 0, proj, kblk + 2 * g, proj, kblk + 3 * g, bias_d, NSA_HEADS, NSA_GROUP, NSA_GROUP,
                 _causal_pairs(nq, nn), True, member=member, et=_block_onehot(s, SEL_BLOCK), name="nsa_selected")
    n_back = -(-WINDOW // t)
    bias_w = _bias_vecs(tab, t, n_back + 1, window=WINDOW)
    o_w = _flash(proj, 0, proj, kblk + 4 * g, proj, kblk + 5 * g, bias_w, NSA_HEADS, NSA_GROUP, NSA_GROUP,
                 _window_pairs(nq, n_back), False, name="nsa_window")
    return o_c, o_s, o_w


def _moba(proj, tab, b, s):
    member = _moba_gate(proj, b, s)
    t = ATT_TILE
    nn = _n_near(t)
    nh = 4
    ng = MOBA_HEADS // nh
    bias_d = _bias_vecs(tab, t, nn + 1)
    return _flash(proj, 0, proj, ng, proj, 2 * ng, bias_d, MOBA_HEADS, 1, nh, _causal_pairs(s // t, nn),
                  True, member=member, et=_block_onehot(s, MOBA_BLOCK), name="moba_attn")


def _moe(x1, g_ffn, w_group, b_group, w_router, b_router, w_gate, w_up, w_down, g_final):
    t, d = x1.shape
    ng, _, epg = w_router.shape
    w_gr = jnp.concatenate([w_group, jnp.transpose(w_router, (1, 0, 2)).reshape(d, ng * epg),
                            jnp.zeros((d, LANES - ng - ng * epg), F32)], axis=1)
    b_gr = jnp.concatenate([b_group, b_router.reshape(-1), jnp.zeros((LANES - ng - ng * epg,), F32)]).reshape(1, LANES)
    h, info, cnt = _route(x1, g_ffn, w_gr, b_gr)
    n_e = ng * epg
    n_assign = t * EXPERT_TOPK
    n_blocks = -(-(n_assign + n_e * (MOE_ROWS - 1)) // MOE_ROWS)
    counts = cnt[0, :n_e].astype(jnp.int32)
    padded = (counts + MOE_ROWS - 1) // MOE_ROWS * MOE_ROWS
    pad_end = jnp.cumsum(padded)
    pad_start = pad_end - padded
    expert = info[:, 0:EXPERT_TOPK].astype(jnp.int32)
    rank = info[:, 4:4 + EXPERT_TOPK].astype(jnp.int32)
    e_ids = jnp.arange(n_e, dtype=jnp.int32)
    dest = (jnp.sum(jnp.where(expert[..., None] == e_ids, pad_start, 0), axis=-1) + rank).reshape(-1)
    block_row0 = jnp.arange(n_blocks, dtype=jnp.int32) * MOE_ROWS
    block_e = jnp.minimum(jnp.sum((pad_end[None, :] <= block_row0[:, None]).astype(jnp.int32), axis=1), n_e - 1)
    n_used = (pad_end[-1:] // MOE_ROWS).astype(jnp.int32)
    xs = _dispatch(dest, h, n_blocks * MOE_ROWS)
    ys = _experts(block_e, n_used, xs, w_gate, w_up, w_down)
    return _combine(dest, x1, info, g_final, ys)


def kernel(x, rel_bias, norm_mix, w_in, cmp_pe_k, cmp_w1_k, cmp_w2_k, cmp_pe_v, cmp_w1_v, cmp_w2_v, w_up_nsa,
           w_up_moba, w_out, norm_ffn, w_group, b_group, w_router, b_router, w_exp_gate, w_exp_up, w_exp_down,
           final_norm):
    b, s, d = x.shape
    t = b * s
    depth = w_in.shape[0]
    tab_a = rel_bias[:, :NSA_HEADS]
    tab_b = rel_bias[:, NSA_HEADS:]
    a_cols = NSA_HEADS * HEAD_DIM + 6 * NSA_KV_HEADS * HEAD_DIM
    gate_cols = 3 * NSA_HEADS
    b_cols = 3 * MOBA_HEADS * HEAD_DIM
    xt = x.reshape(t, d)
    out = None
    for l in range(depth):
        h = _rmsnorm(xt, norm_mix[l], BF16)
        wt = jnp.swapaxes(w_in[l], 0, 1)
        b_col0 = a_cols + gate_cols
        proj_a = _matmul(h, wt, 0, a_cols, BF16, scaled_cols=NSA_HEADS * HEAD_DIM, col_scale=Q_SCALE,
                         w_transposed=True, name="in_proj_a").reshape(b, s, a_cols)
        gate_a = _matmul(h, wt, a_cols, LANES, F32, w_transposed=True, tn=LANES, name="in_proj_gate")
        proj_b = _matmul(h, wt, b_col0, b_cols, BF16, scaled_cols=MOBA_HEADS * HEAD_DIM, col_scale=Q_SCALE,
                         w_transposed=True, name="in_proj_b").reshape(b, s, b_cols)
        gm = _matmul(h, wt, b_col0 + b_cols, 2 * d, F32, w_transposed=True, name="in_proj_gm")
        o_c, o_s, o_w = _nsa(proj_a, gate_cols, cmp_pe_k[l], cmp_w1_k[l], cmp_w2_k[l],
                             cmp_pe_v[l], cmp_w1_v[l], cmp_w2_v[l], tab_a, b, s)
        o_b = _moba(proj_b, tab_b, b, s)
        merged = _merge(o_c.reshape(t, -1), o_s.reshape(t, -1), o_w.reshape(t, -1), gate_a,
                        o_b.reshape(t, -1), gm, w_up_nsa[l], w_up_moba[l])
        x1 = _matmul(merged, w_out[l], 0, d, F32, res=xt, name="out_proj")
        assert l == depth - 1, "only the last layer's MoE is fused with the final norm"
        out = _moe(x1, norm_ffn[l], w_group[l], b_group[l], w_router[l], b_router[l],
                   w_exp_gate[l], w_exp_up[l], w_exp_down[l], final_norm)
    return out.reshape(b, s, d)
```

```python
import functools
import math

import numpy as np
import jax
import jax.numpy as jnp
from jax import lax
from jax.experimental import pallas as pl
from jax.experimental.pallas import tpu as pltpu

F32 = jnp.float32
BF16 = jnp.bfloat16

HEAD_DIM = 128
NSA_HEADS = 8
NSA_KV_HEADS = 2
NSA_GROUP = NSA_HEADS // NSA_KV_HEADS
CMP_BLOCK = 32
CMP_STRIDE = 16
SEL_BLOCK = 64
SEL_TOPN = 16
WINDOW = 512
FORCED_SCORE = 1e4
MOBA_HEADS = 8
MOBA_BLOCK = 256
MOBA_TOPK = 3
REL_BUCKETS = 32
REL_MAX_DIST = 128
N_GROUPS = 8
EXPERTS_PER_GROUP = 8
N_EXPERTS = N_GROUPS * EXPERTS_PER_GROUP
EXPERT_TOPK = 2
RMS_EPS = 1e-6

LANES = 128
ATT_TILE = 512
MOE_ROWS = 256
MASK_BIG = 1e30
M_INIT = -3e38
LOG2E = math.log2(math.e)
Q_SCALE = HEAD_DIM ** -0.5 * LOG2E
VMEM_LIMIT = 48 * 1024 * 1024


def _cparams(sem, vmem=VMEM_LIMIT, flags=None):
    return pltpu.CompilerParams(dimension_semantics=sem, vmem_limit_bytes=vmem, flags=flags)


def _rmsnorm_body(x_ref, g_ref, o_ref):
    x = x_ref[...]
    ms = jnp.mean(x * x, axis=-1, keepdims=True)
    o_ref[...] = (x * lax.rsqrt(ms + RMS_EPS) * g_ref[...]).astype(o_ref.dtype)


def _rmsnorm(x, g, out_dtype, tm=512):
    t, d = x.shape
    return pl.pallas_call(
        _rmsnorm_body,
        grid=(t // tm,),
        in_specs=[pl.BlockSpec((tm, d), lambda i: (i, 0)),
                  pl.BlockSpec((1, d), lambda i: (0, 0))],
        out_specs=pl.BlockSpec((tm, d), lambda i: (i, 0)),
        out_shape=jax.ShapeDtypeStruct((t, d), out_dtype),
        compiler_params=_cparams(("parallel",)),
        name="rmsnorm",
    )(x, g.reshape(1, d))


def _mm_body(*refs, has_res, n_scaled, col_scale, w_transposed):
    if has_res:
        a_ref, w_ref, r_ref, o_ref = refs
    else:
        a_ref, w_ref, o_ref = refs
    w = w_ref[...].astype(BF16)
    if w_transposed:
        acc = lax.dot_general(a_ref[...], w, (((1,), (1,)), ((), ())), preferred_element_type=F32)
    else:
        acc = jnp.dot(a_ref[...], w, preferred_element_type=F32)
    if n_scaled:
        acc = acc * jnp.where(pl.program_id(1) < n_scaled, col_scale, 1.0)
    if has_res:
        acc = acc + r_ref[...]
    o_ref[...] = acc.astype(o_ref.dtype)


def _matmul(a, w, col0, ncols, out_dtype, res=None, scaled_cols=0, col_scale=1.0, w_transposed=False,
            tm=2048, tn=512, name="matmul"):
    t, k = a.shape
    tn = min(tn, ncols)
    tm = min(tm, t)
    assert ncols % tn == 0 and t % tm == 0 and scaled_cols % tn == 0
    if w_transposed:
        assert col0 % 8 == 0
        w_spec = pl.BlockSpec((pl.Element(tn), pl.Element(k)),
                              lambda i, j: (pl.multiple_of(col0 + j * tn, 8), 0))
    else:
        assert col0 % tn == 0
        off = col0 // tn
        w_spec = pl.BlockSpec((k, tn), lambda i, j: (0, j + off))
    in_specs = [pl.BlockSpec((tm, k), lambda i, j: (i, 0)), w_spec]
    args = [a, w]
    if res is not None:
        in_specs.append(pl.BlockSpec((tm, tn), lambda i, j: (i, j)))
        args.append(res)
    return pl.pallas_call(
        functools.partial(_mm_body, has_res=res is not None, n_scaled=scaled_cols // tn, col_scale=col_scale,
                          w_transposed=w_transposed),
        grid=(t // tm, ncols // tn),
        in_specs=in_specs,
        out_specs=pl.BlockSpec((tm, tn), lambda i, j: (i, j)),
        out_shape=jax.ShapeDtypeStruct((t, ncols), out_dtype),
        compiler_params=_cparams(("parallel", "parallel")),
        name=name,
    )(*args)


def _compress_body(u_ref, pe_ref, w1_ref, w2_ref, o_ref, *, nc):
    u = u_ref[...]
    w1 = w1_ref[...].astype(BF16)
    half = u.shape[1]
    a = jnp.dot(u, w1[:half], preferred_element_type=F32)
    b = jnp.dot(u, w1[half:], preferred_element_type=F32)
    peb = jnp.dot(pe_ref[...].astype(BF16), w1, preferred_element_type=F32)[0:1]
    pre = a + pltpu.roll(b, nc - 1, 0) + peb
    hid = jax.nn.gelu(pre)
    o_ref[...] = jnp.dot(hid.astype(BF16), w2_ref[...].astype(BF16),
                         preferred_element_type=F32).astype(o_ref.dtype)


def _compress(u, pe, w1, w2):
    bg, nc, kk = u.shape
    hid = w1.shape[1]
    dh = w2.shape[1]
    pe8 = jnp.broadcast_to(pe.reshape(1, -1), (16, pe.size))
    return pl.pallas_call(
        functools.partial(_compress_body, nc=nc),
        grid=(bg,),
        in_specs=[pl.BlockSpec((None, nc, kk), lambda i: (i, 0, 0)),
                  pl.BlockSpec((16, 2 * kk), lambda i: (0, 0)),
                  pl.BlockSpec((2 * kk, hid), lambda i: (0, 0)),
                  pl.BlockSpec((hid, dh), lambda i: (0, 0))],
        out_specs=pl.BlockSpec((None, nc, dh), lambda i: (i, 0, 0)),
        out_shape=jax.ShapeDtypeStruct((bg, nc, dh), BF16),
        compiler_params=_cparams(("parallel",)),
        name="nsa_compress",
    )(u, pe8, w1, w2)


def _split3(x):
    p1 = x.astype(BF16)
    r = x - p1.astype(F32)
    p2 = r.astype(BF16)
    p3 = (r - p2.astype(F32)).astype(BF16)
    return p1, p2, p3


def _rank_count(score, n_rows):
    groups = []
    for g0 in range(0, n_rows, 8):
        sg = score[g0:min(g0 + 8, n_rows), :]
        n_iota = g0 + lax.broadcasted_iota(jnp.int32, sg.shape, 0)
        cnt = jnp.zeros(sg.shape, F32)
        for m in range(n_rows):
            row = score[m:m + 1, :]
            if m < g0:
                beats = row >= sg
            elif m >= g0 + 8:
                beats = row > sg
            else:
                tie = jnp.where(n_iota > m, 1.0, 0.0)
                beats = jnp.where(row > sg, 1.0, jnp.where(row == sg, tie, 0.0)) > 0.5
            cnt = cnt + jnp.where(beats, 1.0, 0.0)
        groups.append(cnt)
    return jnp.concatenate(groups, axis=0) if len(groups) > 1 else groups[0]


def _nsa_cmp_body(q_ref, kc_ref, vc_ref, bias_ref, ov_ref, oc_ref, mem_ref, *, tq, nc, n_sel):
    t0 = pl.program_id(2) * tq
    kc = kc_ref[...]
    vc = vc_ref[...]
    t_idx = t0 + lax.broadcasted_iota(jnp.int32, (tq, nc), 0)
    c_idx = lax.broadcasted_iota(jnp.int32, (tq, nc), 1)
    dist = t_idx - (c_idx * CMP_STRIDE + (CMP_BLOCK - 1))
    n_k = REL_MAX_DIST // CMP_STRIDE
    kidx = jnp.where(dist < 0, n_k + 1, jnp.minimum(lax.shift_right_logical(dist, 4), n_k))
    assert CMP_STRIDE == 16
    psum = jnp.zeros((tq, nc), F32)
    for j in range(NSA_GROUP):
        hs = slice(j * HEAD_DIM, (j + 1) * HEAD_DIM)
        gt = bias_ref[j] * LOG2E
        bias = jnp.concatenate([jnp.take_along_axis(gt, kidx[:, c0:c0 + LANES], axis=1)
                                for c0 in range(0, nc, LANES)], axis=1)
        s = lax.dot_general(q_ref[:, hs], kc, (((1,), (1,)), ((), ())), preferred_element_type=F32) + bias
        m = jnp.max(s, axis=-1, keepdims=True)
        m = jnp.where(m > -0.5 * MASK_BIG, m, 0.0)
        p = jnp.exp2(s - m)
        d = jnp.sum(p, axis=-1, keepdims=True)
        p = p / jnp.where(d > 0, d, 1.0)
        oc_ref[:, hs] = jnp.dot(p.astype(BF16), vc, preferred_element_type=F32).astype(oc_ref.dtype)
        psum = psum + p
    ov = ov_ref[...]
    nt = (((1,), (1,)), ((), ()))
    p1, p2, p3 = _split3(psum)
    psel = (lax.dot_general(ov, p1, nt, preferred_element_type=F32)
            + lax.dot_general(ov, p2, nt, preferred_element_type=F32)
            + lax.dot_general(ov, p3, nt, preferred_element_type=F32))
    n_idx = lax.broadcasted_iota(jnp.int32, (n_sel, tq), 0)
    tt = t0 + lax.broadcasted_iota(jnp.int32, (n_sel, tq), 1)
    cur = tt // SEL_BLOCK
    forced = jnp.where(n_idx == 0, 1.0, jnp.where(n_idx == cur, 1.0, jnp.where(n_idx == cur - 1, 1.0, 0.0)))
    score = jnp.where(forced > 0.5, FORCED_SCORE, jnp.where(n_idx * SEL_BLOCK <= tt, psel, -1.0))
    cnt = _rank_count(score, n_sel)
    member = jnp.where(cnt < float(min(SEL_TOPN, n_sel)), 1.0, 0.0)
    if n_sel < LANES:
        member = jnp.concatenate([member, jnp.zeros((LANES - n_sel, tq), F32)], axis=0)
    mem_ref[...] = member.T.astype(mem_ref.dtype)


def _nsa_cmp(proj, kc, vc, tab, overlap, b, s, tq=256):
    g = NSA_KV_HEADS
    nc = kc.shape[2]
    n_sel = s // SEL_BLOCK
    gw = NSA_GROUP * HEAD_DIM
    assert tq % CMP_STRIDE == 0 and REL_MAX_DIST % CMP_STRIDE == 0
    n_k = REL_MAX_DIST // CMP_STRIDE
    rho = (np.arange(tq)[:, None] - (CMP_BLOCK - 1)) % CMP_STRIDE
    dd = np.concatenate([rho + CMP_STRIDE * np.arange(n_k)[None, :], np.full((tq, 1), REL_MAX_DIST)], axis=1)
    gtab = jnp.transpose(tab[_rel_bucket(dd)], (2, 0, 1)).astype(F32)
    bias_c = jnp.concatenate([gtab, jnp.full(gtab.shape[:2] + (1,), -MASK_BIG, F32),
                              jnp.zeros(gtab.shape[:2] + (LANES - n_k - 2,), F32)], axis=2)
    body = functools.partial(_nsa_cmp_body, tq=tq, nc=nc, n_sel=n_sel)
    return pl.pallas_call(
        body,
        grid=(b, g, s // tq),
        in_specs=[pl.BlockSpec((None, tq, gw), lambda bi, gi, i: (bi, i, gi)),
                  pl.BlockSpec((None, None, nc, HEAD_DIM), lambda bi, gi, i: (bi, gi, 0, 0)),
                  pl.BlockSpec((None, None, nc, HEAD_DIM), lambda bi, gi, i: (bi, gi, 0, 0)),
                  pl.BlockSpec((NSA_GROUP, tq, LANES), lambda bi, gi, i: (gi, 0, 0)),
                  pl.BlockSpec((n_sel, nc), lambda bi, gi, i: (0, 0))],
        out_specs=[pl.BlockSpec((None, tq, gw), lambda bi, gi, i: (bi, i, gi)),
                   pl.BlockSpec((None, None, tq, LANES), lambda bi, gi, i: (bi, gi, i, 0))],
        out_shape=[jax.ShapeDtypeStruct((b, s, NSA_HEADS * HEAD_DIM), BF16),
                   jax.ShapeDtypeStruct((b, g, s, LANES), BF16)],
        compiler_params=_cparams(("parallel", "parallel", "parallel")),
        name="nsa_cmp_select",
    )(proj, kc, vc, bias_c, overlap)


def _moba_gate_body(q_ref, k_ref, mem_ref, *, s, nblk):
    k = k_ref[...].astype(F32)
    kmean = jnp.mean(k.reshape(nblk, MOBA_BLOCK, HEAD_DIM), axis=1)
    k1 = kmean.astype(BF16)
    k2 = (kmean - k1.astype(F32)).astype(BF16)
    q = q_ref[...]
    nt = (((1,), (1,)), ((), ()))
    gate = (lax.dot_general(k1, q, nt, preferred_element_type=F32)
            + lax.dot_general(k2, q, nt, preferred_element_type=F32))
    n_idx = lax.broadcasted_iota(jnp.int32, (nblk, s), 0)
    own = lax.broadcasted_iota(jnp.int32, (nblk, s), 1) // MOBA_BLOCK
    past = n_idx < own
    score = jnp.where(past, gate, -MASK_BIG)
    cnt = _rank_count(score, nblk)
    n_top = max(1, min(MOBA_TOPK, nblk - 1))
    sel = jnp.where(past, jnp.where(cnt < float(n_top), 1.0, 0.0), 0.0)
    member = jnp.where(n_idx == own, 1.0, sel)
    member = jnp.concatenate([member, jnp.zeros((LANES - nblk, s), F32)], axis=0)
    mem_ref[...] = member.T.astype(mem_ref.dtype)


def _moba_gate(proj, b, s):
    h = MOBA_HEADS
    nblk = s // MOBA_BLOCK
    return pl.pallas_call(
        functools.partial(_moba_gate_body, s=s, nblk=nblk),
        grid=(b, h),
        in_specs=[pl.BlockSpec((None, s, HEAD_DIM), lambda bi, hi: (bi, 0, hi)),
                  pl.BlockSpec((None, s, HEAD_DIM), lambda bi, hi: (bi, 0, h + hi))],
        out_specs=pl.BlockSpec((None, None, s, LANES), lambda bi, hi: (bi, hi, 0, 0)),
        out_shape=jax.ShapeDtypeStruct((b, h, s, LANES), BF16),
        compiler_params=_cparams(("parallel", "parallel")),
        name="moba_gate",
    )(proj, proj)


def _flash_body(qi_ref, ki_ref, bo_ref, fl_ref, *refs, nh, ratio, nm, n_near, has_far):
    if nm:
        (q_ref, k_ref, v_ref, bvec_ref, mem_ref, et_ref, o_ref,
         m_ref, l_ref, acc_ref, sh_ref, al_ref, bias_ref, s_ref, p_ref) = refs
    else:
        q_ref, k_ref, v_ref, bvec_ref, o_ref, m_ref, l_ref, acc_ref, sh_ref, al_ref, bias_ref, s_ref, p_ref = refs
    del qi_ref, ki_ref
    p = pl.program_id(2)
    flag = fl_ref[p]
    bo = bo_ref[p]
    t = q_ref.shape[0]
    rows = 64

    @pl.when(p == 0)
    def _():
        for h in range(nh):
            for o in range(n_near):
                vec = bvec_ref[h, o][0:1, :] * LOG2E
                for rc in range(t // rows):
                    x = pltpu.roll(jnp.broadcast_to(vec, (rows, 2 * t)), rc * rows, 1, stride=1, stride_axis=0)
                    bias_ref[h, o, rc * rows:(rc + 1) * rows, :] = x[:, :t]

    @pl.when((flag & 1) != 0)
    def _():
        m_ref[...] = jnp.full(m_ref.shape, M_INIT, F32)
        l_ref[...] = jnp.zeros(l_ref.shape, F32)
        acc_ref[...] = jnp.zeros(acc_ref.shape, F32)

    nt = (((1,), (1,)), ((), ()))
    reps = t // LANES

    def step(near):
        for h in range(nh):
            hs = slice(h * HEAD_DIM, (h + 1) * HEAD_DIM)
            kv = h // ratio
            ks = slice(kv * HEAD_DIM, (kv + 1) * HEAD_DIM)
            q = q_ref[:, hs]
            k = k_ref[:, ks]
            if nm:
                mneg = mem_ref[h // (nh // nm)] - 1.0
                q = jnp.concatenate([q, mneg.astype(BF16)], axis=1)
                k = jnp.concatenate([k, et_ref[...]], axis=1)
            sc = lax.dot_general(q, k, nt, preferred_element_type=F32)
            m_prev = m_ref[h]
            if near:
                sc = sc + bias_ref[h, bo]
                m_new = jnp.maximum(m_prev, jnp.max(sc, axis=-1, keepdims=True))
                sh_ref[h] = m_new
            else:
                cfar = bvec_ref[h, n_near][0:1, 0:LANES] * LOG2E
                m_new = jnp.maximum(m_prev, jnp.max(sc, axis=-1, keepdims=True) + cfar)
                sh_ref[h] = m_new - cfar
            s_ref[h] = sc
            al_ref[h] = jnp.exp2(m_prev - m_new)
            m_ref[h] = m_new
        for h in range(nh):
            ks = slice(h // ratio * HEAD_DIM, (h // ratio + 1) * HEAD_DIM)
            for rc in range(t // rows):
                rs = slice(rc * rows, (rc + 1) * rows)
                pm = jnp.exp2(s_ref[h, rs, :] - jnp.tile(sh_ref[h, rs, :], (1, reps)))
                l_ref[h, rs, :] = al_ref[h, rs, :] * l_ref[h, rs, :] + jnp.sum(pm, axis=-1, keepdims=True)
                p_ref[h, rs, :] = pm.astype(BF16)
            acc_ref[h] = al_ref[h] * acc_ref[h] + jnp.dot(p_ref[h], v_ref[:, ks], preferred_element_type=F32)

    if has_far:
        pl.when(bo < n_near)(lambda: step(True))
        pl.when(bo >= n_near)(lambda: step(False))
    else:
        step(True)

    @pl.when((flag & 2) != 0)
    def _():
        for h in range(nh):
            l = l_ref[h]
            o_ref[:, h * HEAD_DIM:(h + 1) * HEAD_DIM] = (acc_ref[h] / jnp.where(l > 0, l, 1.0)).astype(o_ref.dtype)


def _flash(q_arr, q_off, k_arr, k_off, v_arr, v_off, bias, n_heads, ratio, nh, pairs, has_far, member=None,
           et=None, name="flash"):
    b, s, _ = q_arr.shape
    t = ATT_TILE
    nkv = nh // ratio
    ng = n_heads // nh
    qi = jnp.asarray([p[0] for p in pairs], jnp.int32)
    ki = jnp.asarray([p[1] for p in pairs], jnp.int32)
    bo = jnp.asarray([p[2] for p in pairs], jnp.int32)
    fl = jnp.asarray([p[3] for p in pairs], jnp.int32)
    nb = bias.shape[1]
    nm = 0
    in_specs = [
        pl.BlockSpec((None, t, nh * HEAD_DIM), lambda bi, gi, p, qi, ki, bo, fl: (bi, qi[p], q_off + gi)),
        pl.BlockSpec((None, t, nkv * HEAD_DIM), lambda bi, gi, p, qi, ki, bo, fl: (bi, ki[p], k_off + gi)),
        pl.BlockSpec((None, t, nkv * HEAD_DIM), lambda bi, gi, p, qi, ki, bo, fl: (bi, ki[p], v_off + gi)),
        pl.BlockSpec((nh, nb, 8, 2 * t), lambda bi, gi, p, qi, ki, bo, fl: (gi, 0, 0, 0)),
    ]
    args = [q_arr, k_arr, v_arr, bias]
    if member is not None:
        nm = member.shape[1] // ng
        in_specs += [
            pl.BlockSpec((None, nm, t, LANES), lambda bi, gi, p, qi, ki, bo, fl: (bi, gi, qi[p], 0)),
            pl.BlockSpec((t, LANES), lambda bi, gi, p, qi, ki, bo, fl: (ki[p], 0)),
        ]
        args += [member, et]
    n_near = nb - 1 if has_far else nb
    body = functools.partial(_flash_body, nh=nh, ratio=ratio, nm=nm, n_near=n_near, has_far=has_far)
    return pl.pallas_call(
        body,
        grid_spec=pltpu.PrefetchScalarGridSpec(
            num_scalar_prefetch=4,
            grid=(b, ng, len(pairs)),
            in_specs=in_specs,
            out_specs=pl.BlockSpec((None, t, nh * HEAD_DIM), lambda bi, gi, p, qi, ki, bo, fl: (bi, qi[p], gi)),
            scratch_shapes=[pltpu.VMEM((nh, t, LANES), F32)] * 5 + [pltpu.VMEM((nh, n_near, t, t), F32),
                                                                    pltpu.VMEM((nh, t, t), F32),
                                                                    pltpu.VMEM((nh, t, t), BF16)],
        ),
        out_shape=jax.ShapeDtypeStruct((b, s, n_heads * HEAD_DIM), BF16),
        compiler_params=_cparams(("parallel", "parallel", "arbitrary")),
        name=name,
    )(qi, ki, bo, fl, *args)


def _rel_bucket(dist):
    n = jnp.maximum(jnp.asarray(dist, jnp.int32), 0)
    max_exact = REL_BUCKETS // 2
    nf = jnp.maximum(n, 1).astype(jnp.float32)
    large = max_exact + (jnp.log(nf / max_exact) / math.log(REL_MAX_DIST / max_exact)
                         * (REL_BUCKETS - max_exact)).astype(jnp.int32)
    return jnp.where(n < max_exact, n, jnp.minimum(large, REL_BUCKETS - 1))


def _n_near(t):
    return -(-(REL_MAX_DIST - 1 + t) // t)


def _bias_vecs(tab, t, n_off, window=None):
    k = np.arange(2 * t)[None, :]
    dist = np.arange(n_off)[:, None] * t + np.where(k < t, -k, 2 * t - k)
    ok = dist >= 0
    if window is not None:
        ok &= dist < window
    bias = jnp.where(jnp.asarray(ok)[..., None], tab[_rel_bucket(dist)], -MASK_BIG)
    bias = jnp.transpose(bias, (2, 0, 1)).astype(F32)
    return jnp.broadcast_to(bias[:, :, None, :], (bias.shape[0], n_off, 8, 2 * t))


def _causal_pairs(nq, n_near):
    pairs = []
    for qi in range(nq):
        for ki in range(qi + 1):
            pairs.append((qi, ki, min(qi - ki, n_near), (1 if ki == 0 else 0) | (2 if ki == qi else 0)))
    return pairs


def _window_pairs(nq, n_back):
    pairs = []
    for qi in range(nq):
        lo = max(0, qi - n_back)
        for ki in range(lo, qi + 1):
            pairs.append((qi, ki, qi - ki, (1 if ki == lo else 0) | (2 if ki == qi else 0)))
    return pairs


def _block_onehot(s, blk):
    return jnp.asarray(np.where(np.arange(s)[:, None] // blk == np.arange(LANES)[None, :], MASK_BIG, 0.0), BF16)


def _merge_body(oc_ref, os_ref, ow_ref, gl_ref, ob_ref, gma_ref, gmb_ref, wa_ref, wb_ref, o_ref, oa_ref):
    @pl.when(pl.program_id(1) == 0)
    def _():
        gates = jax.nn.sigmoid(gl_ref[...])
        for h in range(NSA_HEADS):
            hs = slice(h * HEAD_DIM, (h + 1) * HEAD_DIM)
            mix = (gates[:, 3 * h:3 * h + 1] * oc_ref[:, hs].astype(F32)
                   + gates[:, 3 * h + 1:3 * h + 2] * os_ref[:, hs].astype(F32)
                   + gates[:, 3 * h + 2:3 * h + 3] * ow_ref[:, hs].astype(F32))
            oa_ref[:, hs] = mix.astype(BF16)

    ya = jnp.dot(oa_ref[...], wa_ref[...].astype(BF16), preferred_element_type=F32)
    yb = jnp.dot(ob_ref[...], wb_ref[...].astype(BF16), preferred_element_type=F32)
    o_ref[...] = (jax.nn.sigmoid(gma_ref[...]) * ya + jax.nn.sigmoid(gmb_ref[...]) * yb).astype(o_ref.dtype)


def _merge(o_c, o_s, o_w, gate_logits, o_b, gm, w_up_a, w_up_b, tm=1024, tn=512):
    t, ka = o_c.shape
    kb = o_b.shape[1]
    d = w_up_a.shape[1]
    tm = min(tm, t)
    nj = d // tn
    row = lambda i, j: (i, 0)
    return pl.pallas_call(
        _merge_body,
        grid=(t // tm, nj),
        in_specs=[pl.BlockSpec((tm, ka), row), pl.BlockSpec((tm, ka), row), pl.BlockSpec((tm, ka), row),
                  pl.BlockSpec((tm, LANES), row), pl.BlockSpec((tm, kb), row),
                  pl.BlockSpec((tm, tn), lambda i, j: (i, j)),
                  pl.BlockSpec((tm, tn), lambda i, j: (i, j + nj)),
                  pl.BlockSpec((ka, tn), lambda i, j: (0, j)),
                  pl.BlockSpec((kb, tn), lambda i, j: (0, j))],
        out_specs=pl.BlockSpec((tm, tn), lambda i, j: (i, j)),
        out_shape=jax.ShapeDtypeStruct((t, d), BF16),
        scratch_shapes=[pltpu.VMEM((tm, ka), BF16)],
        compiler_params=_cparams(("parallel", "arbitrary")),
        name="merge_up",
    )(o_c, o_s, o_w, gate_logits, o_b, gm, gm, w_up_a, w_up_b)


def _route_body(x_ref, g_ref, w_ref, b_ref, h_ref, info_ref, cnt_ref, carry_ref, *, tm):
    @pl.when(pl.program_id(0) == 0)
    def _():
        carry_ref[...] = jnp.zeros(carry_ref.shape, F32)

    x = x_ref[...]
    ms = jnp.mean(x * x, axis=-1, keepdims=True)
    h = x * lax.rsqrt(ms + RMS_EPS) * g_ref[...]
    _store_packed(h_ref, h, tm)
    w = w_ref[...]
    h1 = h.astype(BF16)
    h2 = (h - h1.astype(F32)).astype(BF16)
    w1 = w.astype(BF16)
    w2 = (w - w1.astype(F32)).astype(BF16)
    logits = (jnp.dot(h1, w1, preferred_element_type=F32) + jnp.dot(h1, w2, preferred_element_type=F32)
              + jnp.dot(h2, w1, preferred_element_type=F32)) + b_ref[...]
    lane = lax.broadcasted_iota(jnp.int32, (tm, LANES), 1)
    lanef = lane.astype(F32)

    is_g = lane < N_GROUPS
    gl = jnp.where(is_g, logits, -MASK_BIG)
    ge = jnp.where(is_g, jnp.exp(gl - jnp.max(gl, axis=-1, keepdims=True)), 0.0)
    gp = ge / jnp.sum(ge, axis=-1, keepdims=True)
    g_val = jnp.max(gp, axis=-1, keepdims=True)
    g_idx = jnp.min(jnp.where(gp == g_val, lanef, float(LANES)), axis=-1, keepdims=True)

    lane_grp = ((lane - N_GROUPS) // EXPERTS_PER_GROUP).astype(F32)
    in_e = jnp.where(lane >= N_GROUPS, jnp.where(lane < N_GROUPS + N_EXPERTS, 1.0, 0.0), 0.0)
    is_e = jnp.where(lane_grp == g_idx, in_e, 0.0) > 0.5
    el = jnp.where(is_e, logits, -MASK_BIG)
    ee = jnp.where(is_e, jnp.exp(el - jnp.max(el, axis=-1, keepdims=True)), 0.0)
    ep = jnp.where(is_e, ee / jnp.sum(ee, axis=-1, keepdims=True), -1.0)
    v1 = jnp.max(ep, axis=-1, keepdims=True)
    l1 = jnp.min(jnp.where(ep == v1, lanef, float(LANES)), axis=-1, keepdims=True)
    ep2 = jnp.where(lanef == l1, -1.0, ep)
    v2 = jnp.max(ep2, axis=-1, keepdims=True)
    l2 = jnp.min(jnp.where(ep2 == v2, lanef, float(LANES)), axis=-1, keepdims=True)
    vs = v1 + v2
    wt1 = g_val * v1 / vs
    wt2 = g_val * v2 / vs
    e1 = l1 - float(N_GROUPS)
    e2 = l2 - float(N_GROUPS)

    oh = jnp.where(lanef == e1, 1.0, jnp.where(lanef == e2, 1.0, 0.0))
    r_i = lax.broadcasted_iota(jnp.int32, (tm, tm), 0)
    c_i = lax.broadcasted_iota(jnp.int32, (tm, tm), 1)
    tri = jnp.where(r_i > c_i, 1.0, 0.0).astype(BF16)
    base = jnp.dot(tri, oh.astype(BF16), preferred_element_type=F32) + carry_ref[...]
    r1 = jnp.sum(jnp.where(lanef == e1, base, 0.0), axis=-1, keepdims=True)
    r2 = jnp.sum(jnp.where(lanef == e2, base, 0.0), axis=-1, keepdims=True)
    carry_ref[...] = carry_ref[...] + jnp.sum(oh, axis=0, keepdims=True)
    cnt_ref[...] = jnp.broadcast_to(carry_ref[...], cnt_ref.shape)
    info = jnp.where(lane == 0, e1, jnp.where(lane == 1, e2, jnp.where(lane == 2, wt1, jnp.where(
        lane == 3, wt2, jnp.where(lane == 4, r1, jnp.where(lane == 5, r2, 0.0))))))
    info_ref[...] = info


def _route(x1, g, w_gr, b_gr, tm=512):
    t, d = x1.shape
    return pl.pallas_call(
        functools.partial(_route_body, tm=tm),
        grid=(t // tm,),
        in_specs=[pl.BlockSpec((tm, d), lambda i: (i, 0)),
                  pl.BlockSpec((1, d), lambda i: (0, 0)),
                  pl.BlockSpec((d, LANES), lambda i: (0, 0)),
                  pl.BlockSpec((1, LANES), lambda i: (0, 0))],
        out_specs=[pl.BlockSpec((tm * ROW_SUB, LANES), lambda i: (i, 0)),
                   pl.BlockSpec((tm, LANES), lambda i: (i, 0)),
                   pl.BlockSpec((8, LANES), lambda i: (0, 0))],
        out_shape=[jax.ShapeDtypeStruct((t * ROW_SUB, LANES), jnp.uint32),
                   jax.ShapeDtypeStruct((t, LANES), F32),
                   jax.ShapeDtypeStruct((8, LANES), F32)],
        scratch_shapes=[pltpu.VMEM((1, LANES), F32)],
        compiler_params=_cparams(("arbitrary",)),
        name="moe_route",
    )(x1, g.reshape(1, d), w_gr, b_gr)


ROW_SUB = 8
U32 = jnp.uint32


def _pack_pairs(lo, hi):
    lo_b = lax.bitcast_convert_type(lo.astype(BF16).astype(F32), U32)
    hi_b = lax.bitcast_convert_type(hi.astype(BF16).astype(F32), U32)
    return lax.shift_right_logical(lo_b, U32(16)) | (hi_b & U32(0xFFFF0000))


def _unpack_pairs(w):
    lo = lax.bitcast_convert_type(lax.shift_left(w, U32(16)), F32)
    hi = lax.bitcast_convert_type(w & U32(0xFFFF0000), F32)
    return lo, hi


def _store_packed(ref, y, n):
    half = y.shape[1] // 2
    for s in range(ROW_SUB):
        cs = slice(s * LANES, (s + 1) * LANES)
        ref[pl.ds(s, n, stride=ROW_SUB), :] = _pack_pairs(y[:, cs], y[:, half + s * LANES:half + (s + 1) * LANES])


def _load_packed(ref, n):
    los, his = [], []
    for s in range(ROW_SUB):
        lo, hi = _unpack_pairs(ref[pl.ds(s, n, stride=ROW_SUB), :])
        los.append(lo)
        his.append(hi)
    return jnp.concatenate(los + his, axis=1)


def _row_copy(src_ref, src_row, dst_ref, dst_row, sem):
    return pltpu.make_async_copy(src_ref.at[pl.ds(pl.multiple_of(src_row * ROW_SUB, ROW_SUB), ROW_SUB)],
                                 dst_ref.at[pl.ds(pl.multiple_of(dst_row * ROW_SUB, ROW_SUB), ROW_SUB)], sem)


def _dispatch_body(dest_ref, h_ref, xs_in_ref, xs_ref, sem, *, tm):
    del xs_in_ref
    base = pl.program_id(0) * tm

    def issue(r, c):
        for k in range(EXPERT_TOPK):
            _row_copy(h_ref, r, xs_ref, dest_ref[EXPERT_TOPK * (base + r) + k], sem).start()
        return c

    lax.fori_loop(0, tm, issue, 0, unroll=4)

    def drain(r, c):
        for k in range(EXPERT_TOPK):
            _row_copy(h_ref, r, xs_ref, dest_ref[EXPERT_TOPK * (base + r) + k], sem).wait()
        return c

    lax.fori_loop(0, tm, drain, 0, unroll=4)


def _dispatch(dest, hp, n_rows, tm=512):
    t = hp.shape[0] // ROW_SUB
    xs0 = jnp.zeros((n_rows * ROW_SUB, LANES), U32)
    return pl.pallas_call(
        functools.partial(_dispatch_body, tm=tm),
        grid_spec=pltpu.PrefetchScalarGridSpec(
            num_scalar_prefetch=1,
            grid=(t // tm,),
            in_specs=[pl.BlockSpec((tm * ROW_SUB, LANES), lambda i, dest: (i, 0)),
                      pl.BlockSpec(memory_space=pl.ANY)],
            out_specs=pl.BlockSpec(memory_space=pl.ANY),
            scratch_shapes=[pltpu.SemaphoreType.DMA(())],
        ),
        out_shape=jax.ShapeDtypeStruct((n_rows * ROW_SUB, LANES), U32),
        input_output_aliases={2: 0},
        compiler_params=_cparams(("arbitrary",)),
        name="moe_dispatch",
    )(dest, hp, xs0)


def _expert_body(nu_ref, sq_ref, es_ref, ns_ref, x_ref, wg_hbm, wu_hbm, wd_hbm, y_ref,
                 wg_b, wu_b, wd_b, wg_s, wu_s, wd_s, sem):
    i = pl.program_id(0)
    nu = nu_ref[0]
    ns = ns_ref[0]

    def weight_copies(seq, slot):
        e = es_ref[seq]
        return (pltpu.make_async_copy(wg_hbm.at[e], wg_b.at[slot], sem.at[slot]),
                pltpu.make_async_copy(wu_hbm.at[e], wu_b.at[slot], sem.at[slot]),
                pltpu.make_async_copy(wd_hbm.at[e], wd_b.at[slot], sem.at[slot]))

    def start_weights(seq, slot):
        for c in weight_copies(seq, slot):
            c.start()

    @pl.when(i == 0)
    def _():
        start_weights(0, 0)

        @pl.when(ns > 1)
        def _():
            start_weights(1, 1)

    s = sq_ref[i]
    first = (i == 0) | (s != sq_ref[jnp.maximum(i - 1, 0)])

    @pl.when((i < nu) & first)
    def _():
        slot = lax.rem(s, 2)
        for c in weight_copies(s, slot):
            c.wait()
        wg_s[...] = wg_b[slot].astype(BF16)
        wu_s[...] = wu_b[slot].astype(BF16)
        wd_s[...] = wd_b[slot].astype(BF16)

        @pl.when(s + 2 < ns)
        def _():
            start_weights(s + 2, slot)

    @pl.when(i < nu)
    def _():
        x = _load_packed(x_ref, MOE_ROWS).astype(BF16)
        g = jnp.dot(x, wg_s[...], preferred_element_type=F32)
        u = jnp.dot(x, wu_s[...], preferred_element_type=F32)
        mid = (jax.nn.silu(g) * u).astype(BF16)
        _store_packed(y_ref, jnp.dot(mid, wd_s[...], preferred_element_type=F32), MOE_ROWS)

    @pl.when(i >= nu_ref[0])
    def _():
        y_ref[...] = jnp.zeros(y_ref.shape, y_ref.dtype)


def _experts(n_used, seq_of_block, expert_of_seq, n_seq, xs, w_gate, w_up, w_down):
    n_blocks = xs.shape[0] // (MOE_ROWS * ROW_SUB)
    _, d, ff = w_gate.shape
    assert d == 2 * ROW_SUB * LANES, "a packed row must be exactly one (8,128) tile"
    blk = lambda i, nu, sq, es, ns: (jnp.minimum(i, nu[0] - 1), 0)
    hbm = pl.BlockSpec(memory_space=pl.ANY)
    return pl.pallas_call(
        _expert_body,
        grid_spec=pltpu.PrefetchScalarGridSpec(
            num_scalar_prefetch=4,
            grid=(n_blocks,),
            in_specs=[pl.BlockSpec((MOE_ROWS * ROW_SUB, LANES), blk), hbm, hbm, hbm],
            out_specs=pl.BlockSpec((MOE_ROWS * ROW_SUB, LANES), lambda i, nu, sq, es, ns: (i, 0)),
            scratch_shapes=[pltpu.VMEM((2, d, ff), F32), pltpu.VMEM((2, d, ff), F32), pltpu.VMEM((2, ff, d), F32),
                            pltpu.VMEM((d, ff), BF16), pltpu.VMEM((d, ff), BF16), pltpu.VMEM((ff, d), BF16),
                            pltpu.SemaphoreType.DMA((2,))],
        ),
        out_shape=jax.ShapeDtypeStruct(xs.shape, U32),
        compiler_params=_cparams(("arbitrary",)),
        name="moe_experts",
    )(n_used, seq_of_block, expert_of_seq, n_seq, xs, w_gate, w_up, w_down)


def _combine_body(dest_ref, x_ref, info_ref, g_ref, ys_ref, o_ref, buf0, buf1, sem, *, tm):
    i = pl.program_id(0)
    slot = lax.rem(i, 2)
    bufs = (buf0, buf1)

    def gather(tile, sl, wait):
        def body(r, c):
            for k in range(EXPERT_TOPK):
                cp = _row_copy(ys_ref, dest_ref[EXPERT_TOPK * (tile * tm + r) + k], bufs[k].at[sl], r, sem.at[sl])
                if wait:
                    cp.wait()
                else:
                    cp.start()
            return c
        lax.fori_loop(0, tm, body, 0, unroll=4)

    @pl.when(i == 0)
    def _():
        gather(0, 0, False)

    @pl.when(i + 1 < pl.num_programs(0))
    def _():
        gather(i + 1, 1 - slot, False)

    gather(i, slot, True)
    info = info_ref[...]
    y = x_ref[...] + (info[:, 2:3] * _load_packed(buf0.at[slot], tm) + info[:, 3:4] * _load_packed(buf1.at[slot], tm))
    ms = jnp.mean(y * y, axis=-1, keepdims=True)
    o_ref[...] = y * lax.rsqrt(ms + RMS_EPS) * g_ref[...]


def _combine(dest, x1, info, g, ys, tm=256):
    t, d = x1.shape
    return pl.pallas_call(
        functools.partial(_combine_body, tm=tm),
        grid_spec=pltpu.PrefetchScalarGridSpec(
            num_scalar_prefetch=1,
            grid=(t // tm,),
            in_specs=[pl.BlockSpec((tm, d), lambda i, dest: (i, 0)),
                      pl.BlockSpec((tm, LANES), lambda i, dest: (i, 0)),
                      pl.BlockSpec((1, d), lambda i, dest: (0, 0)),
                      pl.BlockSpec(memory_space=pl.ANY)],
            out_specs=pl.BlockSpec((tm, d), lambda i, dest: (i, 0)),
            scratch_shapes=[pltpu.VMEM((2, tm * ROW_SUB, LANES), U32), pltpu.VMEM((2, tm * ROW_SUB, LANES), U32),
                            pltpu.SemaphoreType.DMA((2,))],
        ),
        out_shape=jax.ShapeDtypeStruct((t, d), F32),
        compiler_params=_cparams(("arbitrary",)),
        name="moe_combine",
    )(dest, x1, info, g.reshape(1, d), ys)


def _nsa(proj, gate_cols, pe_k, w1_k, w2_k, pe_v, w1_v, w2_v, tab, b, s):
    del gate_cols
    g, dh = NSA_KV_HEADS, HEAD_DIM
    qw = NSA_HEADS * dh
    nc = s // CMP_STRIDE

    def blocks16(col0):
        a = proj[:, :, col0:col0 + g * dh].reshape(b, nc, CMP_STRIDE, g, dh)
        return a.transpose(0, 3, 1, 2, 4).reshape(b * g, nc, CMP_STRIDE * dh)

    kc = _compress(blocks16(qw), pe_k, w1_k, w2_k).reshape(b, g, nc, dh)
    vc = _compress(blocks16(qw + g * dh), pe_v, w1_v, w2_v).reshape(b, g, nc, dh)

    c_start = np.arange(nc)[None, :] * CMP_STRIDE
    n_sel = s // SEL_BLOCK
    sb = np.arange(n_sel)[:, None] * SEL_BLOCK
    overlap = jnp.asarray((c_start < sb + SEL_BLOCK) & (c_start + CMP_BLOCK > sb), BF16)
    o_c, member = _nsa_cmp(proj, kc, vc, tab, overlap, b, s)

    t = ATT_TILE
    nq = s // t
    nn = _n_near(t)
    kblk = qw // dh
    bias_d = _bias_vecs(tab, t, nn + 1)
    o_s = _flash(proj, 0, proj, kblk + 2 * g, proj, kblk + 3 * g, bias_d, NSA_HEADS, NSA_GROUP, NSA_GROUP,
                 _causal_pairs(nq, nn), True, member=member, et=_block_onehot(s, SEL_BLOCK), name="nsa_selected")
    n_back = -(-WINDOW // t)
    bias_w = _bias_vecs(tab, t, n_back + 1, window=WINDOW)
    o_w = _flash(proj, 0, proj, kblk + 4 * g, proj, kblk + 5 * g, bias_w, NSA_HEADS, NSA_GROUP, NSA_GROUP,
                 _window_pairs(nq, n_back), False, name="nsa_window")
    return o_c, o_s, o_w


def _moba(proj, tab, b, s):
    member = _moba_gate(proj, b, s)
    t = ATT_TILE
    nn = _n_near(t)
    nh = 4
    ng = MOBA_HEADS // nh
    bias_d = _bias_vecs(tab, t, nn + 1)
    return _flash(proj, 0, proj, ng, proj, 2 * ng, bias_d, MOBA_HEADS, 1, nh, _causal_pairs(s // t, nn),
                  True, member=member, et=_block_onehot(s, MOBA_BLOCK), name="moba_attn")


def _moe(x1, g_ffn, w_group, b_group, w_router, b_router, w_gate, w_up, w_down, g_final):
    t, d = x1.shape
    ng, _, epg = w_router.shape
    w_gr = jnp.concatenate([w_group, jnp.transpose(w_router, (1, 0, 2)).reshape(d, ng * epg),
                            jnp.zeros((d, LANES - ng - ng * epg), F32)], axis=1)
    b_gr = jnp.concatenate([b_group, b_router.reshape(-1), jnp.zeros((LANES - ng - ng * epg,), F32)]).reshape(1, LANES)
    h, info, cnt = _route(x1, g_ffn, w_gr, b_gr)
    n_e = ng * epg
    n_assign = t * EXPERT_TOPK
    n_blocks = -(-(n_assign + n_e * (MOE_ROWS - 1)) // MOE_ROWS)
    counts = cnt[0, :n_e].astype(jnp.int32)
    padded = (counts + MOE_ROWS - 1) // MOE_ROWS * MOE_ROWS
    pad_end = jnp.cumsum(padded)
    pad_start = pad_end - padded
    expert = info[:, 0:EXPERT_TOPK].astype(jnp.int32)
    rank = info[:, 4:4 + EXPERT_TOPK].astype(jnp.int32)
    e_ids = jnp.arange(n_e, dtype=jnp.int32)
    dest = (jnp.sum(jnp.where(expert[..., None] == e_ids, pad_start, 0), axis=-1) + rank).reshape(-1)
    block_row0 = jnp.arange(n_blocks, dtype=jnp.int32) * MOE_ROWS
    block_e = jnp.minimum(jnp.sum((pad_end[None, :] <= block_row0[:, None]).astype(jnp.int32), axis=1), n_e - 1)
    n_used = (pad_end[-1:] // MOE_ROWS).astype(jnp.int32)
    owns = counts > 0
    seq_of_expert = jnp.cumsum(owns.astype(jnp.int32)) - 1
    n_seq = jnp.sum(owns.astype(jnp.int32)).reshape(1)
    expert_of_seq = jnp.sum(jnp.where(owns[None, :] & (seq_of_expert[None, :] == e_ids[:, None]), e_ids[None, :], 0),
                            axis=1)
    seq_of_block = jnp.sum(jnp.where(block_e[:, None] == e_ids[None, :], seq_of_expert[None, :], 0), axis=1)
    xs = _dispatch(dest, h, n_blocks * MOE_ROWS)
    ys = _experts(n_used, seq_of_block, expert_of_seq, n_seq, xs, w_gate, w_up, w_down)
    return _combine(dest, x1, info, g_final, ys)


def kernel(x, rel_bias, norm_mix, w_in, cmp_pe_k, cmp_w1_k, cmp_w2_k, cmp_pe_v, cmp_w1_v, cmp_w2_v, w_up_nsa,
           w_up_moba, w_out, norm_ffn, w_group, b_group, w_router, b_router, w_exp_gate, w_exp_up, w_exp_down,
           final_norm):
    b, s, d = x.shape
    t = b * s
    depth = w_in.shape[0]
    tab_a = rel_bias[:, :NSA_HEADS]
    tab_b = rel_bias[:, NSA_HEADS:]
    a_cols = NSA_HEADS * HEAD_DIM + 6 * NSA_KV_HEADS * HEAD_DIM
    gate_cols = 3 * NSA_HEADS
    b_cols = 3 * MOBA_HEADS * HEAD_DIM
    xt = x.reshape(t, d)
    out = None
    for l in range(depth):
        h = _rmsnorm(xt, norm_mix[l], BF16)
        wt = jnp.swapaxes(w_in[l], 0, 1)
        b_col0 = a_cols + gate_cols
        proj_a = _matmul(h, wt, 0, a_cols, BF16, scaled_cols=NSA_HEADS * HEAD_DIM, col_scale=Q_SCALE,
                         w_transposed=True, name="in_proj_a").reshape(b, s, a_cols)
        gate_a = _matmul(h, wt, a_cols, LANES, F32, w_transposed=True, tn=LANES, name="in_proj_gate")
        proj_b = _matmul(h, wt, b_col0, b_cols, BF16, scaled_cols=MOBA_HEADS * HEAD_DIM, col_scale=Q_SCALE,
                         w_transposed=True, name="in_proj_b").reshape(b, s, b_cols)
        gm = _matmul(h, wt, b_col0 + b_cols, 2 * d, F32, w_transposed=True, name="in_proj_gm")
        o_c, o_s, o_w = _nsa(proj_a, gate_cols, cmp_pe_k[l], cmp_w1_k[l], cmp_w2_k[l],
                             cmp_pe_v[l], cmp_w1_v[l], cmp_w2_v[l], tab_a, b, s)
        o_b = _moba(proj_b, tab_b, b, s)
        merged = _merge(o_c.reshape(t, -1), o_s.reshape(t, -1), o_w.reshape(t, -1), gate_a,
                        o_b.reshape(t, -1), gm, w_up_nsa[l], w_up_moba[l])
        x1 = _matmul(merged, w_out[l], 0, d, F32, res=xt, name="out_proj")
        assert l == depth - 1, "only the last layer's MoE is fused with the final norm"
        out = _moe(x1, norm_ffn[l], w_group[l], b_group[l], w_router[l], b_router[l],
                   w_exp_gate[l], w_exp_up[l], w_exp_down[l], final_norm)
    return out.reshape(b, s, d)
```

```python
import functools
import math

import numpy as np
import jax
import jax.numpy as jnp
from jax import lax
from jax.experimental import pallas as pl
from jax.experimental.pallas import tpu as pltpu

F32 = jnp.float32
BF16 = jnp.bfloat16

HEAD_DIM = 128
NSA_HEADS = 8
NSA_KV_HEADS = 2
NSA_GROUP = NSA_HEADS // NSA_KV_HEADS
CMP_BLOCK = 32
CMP_STRIDE = 16
SEL_BLOCK = 64
SEL_TOPN = 16
WINDOW = 512
FORCED_SCORE = 1e4
MOBA_HEADS = 8
MOBA_BLOCK = 256
MOBA_TOPK = 3
REL_BUCKETS = 32
REL_MAX_DIST = 128
N_GROUPS = 8
EXPERTS_PER_GROUP = 8
N_EXPERTS = N_GROUPS * EXPERTS_PER_GROUP
EXPERT_TOPK = 2
RMS_EPS = 1e-6

LANES = 128
ATT_TILE = 512
ATT_HEADS_PER_STEP = 8
ATT_VMEM_LIMIT = 56 * 1024 * 1024
MOE_ROWS = 256
MASK_BIG = 1e30
M_INIT = -3e38
LOG2E = math.log2(math.e)
Q_SCALE = HEAD_DIM ** -0.5 * LOG2E
VMEM_LIMIT = 48 * 1024 * 1024


def _cparams(sem, vmem=VMEM_LIMIT, flags=None):
    return pltpu.CompilerParams(dimension_semantics=sem, vmem_limit_bytes=vmem, flags=flags)


def _rmsnorm_body(x_ref, g_ref, o_ref):
    x = x_ref[...]
    ms = jnp.mean(x * x, axis=-1, keepdims=True)
    o_ref[...] = (x * lax.rsqrt(ms + RMS_EPS) * g_ref[...]).astype(o_ref.dtype)


def _rmsnorm(x, g, out_dtype, tm=512):
    t, d = x.shape
    return pl.pallas_call(
        _rmsnorm_body,
        grid=(t // tm,),
        in_specs=[pl.BlockSpec((tm, d), lambda i: (i, 0)),
                  pl.BlockSpec((1, d), lambda i: (0, 0))],
        out_specs=pl.BlockSpec((tm, d), lambda i: (i, 0)),
        out_shape=jax.ShapeDtypeStruct((t, d), out_dtype),
        compiler_params=_cparams(("parallel",)),
        name="rmsnorm",
    )(x, g.reshape(1, d))


def _mm_body(*refs, has_res, n_scaled, col_scale, w_transposed):
    if has_res:
        a_ref, w_ref, r_ref, o_ref = refs
    else:
        a_ref, w_ref, o_ref = refs
    w = w_ref[...].astype(BF16)
    if w_transposed:
        acc = lax.dot_general(a_ref[...], w, (((1,), (1,)), ((), ())), preferred_element_type=F32)
    else:
        acc = jnp.dot(a_ref[...], w, preferred_element_type=F32)
    if n_scaled:
        acc = acc * jnp.where(pl.program_id(1) < n_scaled, col_scale, 1.0)
    if has_res:
        acc = acc + r_ref[...]
    o_ref[...] = acc.astype(o_ref.dtype)


def _matmul(a, w, col0, ncols, out_dtype, res=None, scaled_cols=0, col_scale=1.0, w_transposed=False,
            tm=2048, tn=512, name="matmul"):
    t, k = a.shape
    tn = min(tn, ncols)
    tm = min(tm, t)
    assert ncols % tn == 0 and t % tm == 0 and scaled_cols % tn == 0
    if w_transposed:
        assert col0 % 8 == 0
        w_spec = pl.BlockSpec((pl.Element(tn), pl.Element(k)),
                              lambda i, j: (pl.multiple_of(col0 + j * tn, 8), 0))
    else:
        assert col0 % tn == 0
        off = col0 // tn
        w_spec = pl.BlockSpec((k, tn), lambda i, j: (0, j + off))
    in_specs = [pl.BlockSpec((tm, k), lambda i, j: (i, 0)), w_spec]
    args = [a, w]
    if res is not None:
        in_specs.append(pl.BlockSpec((tm, tn), lambda i, j: (i, j)))
        args.append(res)
    return pl.pallas_call(
        functools.partial(_mm_body, has_res=res is not None, n_scaled=scaled_cols // tn, col_scale=col_scale,
                          w_transposed=w_transposed),
        grid=(t // tm, ncols // tn),
        in_specs=in_specs,
        out_specs=pl.BlockSpec((tm, tn), lambda i, j: (i, j)),
        out_shape=jax.ShapeDtypeStruct((t, ncols), out_dtype),
        compiler_params=_cparams(("parallel", "parallel")),
        name=name,
    )(*args)


def _compress_body(u_ref, pe_ref, w1_ref, w2_ref, o_ref, *, nc):
    u = u_ref[...]
    w1 = w1_ref[...].astype(BF16)
    half = u.shape[1]
    a = jnp.dot(u, w1[:half], preferred_element_type=F32)
    b = jnp.dot(u, w1[half:], preferred_element_type=F32)
    peb = jnp.dot(pe_ref[...].astype(BF16), w1, preferred_element_type=F32)[0:1]
    pre = a + pltpu.roll(b, nc - 1, 0) + peb
    hid = jax.nn.gelu(pre)
    o_ref[...] = jnp.dot(hid.astype(BF16), w2_ref[...].astype(BF16),
                         preferred_element_type=F32).astype(o_ref.dtype)


def _compress(u, pe, w1, w2):
    bg, nc, kk = u.shape
    hid = w1.shape[1]
    dh = w2.shape[1]
    pe8 = jnp.broadcast_to(pe.reshape(1, -1), (16, pe.size))
    return pl.pallas_call(
        functools.partial(_compress_body, nc=nc),
        grid=(bg,),
        in_specs=[pl.BlockSpec((None, nc, kk), lambda i: (i, 0, 0)),
                  pl.BlockSpec((16, 2 * kk), lambda i: (0, 0)),
                  pl.BlockSpec((2 * kk, hid), lambda i: (0, 0)),
                  pl.BlockSpec((hid, dh), lambda i: (0, 0))],
        out_specs=pl.BlockSpec((None, nc, dh), lambda i: (i, 0, 0)),
        out_shape=jax.ShapeDtypeStruct((bg, nc, dh), BF16),
        compiler_params=_cparams(("parallel",)),
        name="nsa_compress",
    )(u, pe8, w1, w2)


def _split3(x):
    p1 = x.astype(BF16)
    r = x - p1.astype(F32)
    p2 = r.astype(BF16)
    p3 = (r - p2.astype(F32)).astype(BF16)
    return p1, p2, p3


def _rank_count(score, n_rows):
    groups = []
    for g0 in range(0, n_rows, 8):
        sg = score[g0:min(g0 + 8, n_rows), :]
        n_iota = g0 + lax.broadcasted_iota(jnp.int32, sg.shape, 0)
        cnt = jnp.zeros(sg.shape, F32)
        for m in range(n_rows):
            row = score[m:m + 1, :]
            if m < g0:
                beats = row >= sg
            elif m >= g0 + 8:
                beats = row > sg
            else:
                tie = jnp.where(n_iota > m, 1.0, 0.0)
                beats = jnp.where(row > sg, 1.0, jnp.where(row == sg, tie, 0.0)) > 0.5
            cnt = cnt + jnp.where(beats, 1.0, 0.0)
        groups.append(cnt)
    return jnp.concatenate(groups, axis=0) if len(groups) > 1 else groups[0]


def _nsa_cmp_body(q_ref, kc_ref, vc_ref, bias_ref, ov_ref, oc_ref, mem_ref, *, tq, nc, n_sel):
    t0 = pl.program_id(2) * tq
    kc = kc_ref[...]
    vc = vc_ref[...]
    t_idx = t0 + lax.broadcasted_iota(jnp.int32, (tq, nc), 0)
    c_idx = lax.broadcasted_iota(jnp.int32, (tq, nc), 1)
    dist = t_idx - (c_idx * CMP_STRIDE + (CMP_BLOCK - 1))
    n_k = REL_MAX_DIST // CMP_STRIDE
    kidx = jnp.where(dist < 0, n_k + 1, jnp.minimum(lax.shift_right_logical(dist, 4), n_k))
    assert CMP_STRIDE == 16
    psum = jnp.zeros((tq, nc), F32)
    for j in range(NSA_GROUP):
        hs = slice(j * HEAD_DIM, (j + 1) * HEAD_DIM)
        gt = bias_ref[j] * LOG2E
        bias = jnp.concatenate([jnp.take_along_axis(gt, kidx[:, c0:c0 + LANES], axis=1)
                                for c0 in range(0, nc, LANES)], axis=1)
        s = lax.dot_general(q_ref[:, hs], kc, (((1,), (1,)), ((), ())), preferred_element_type=F32) + bias
        m = jnp.max(s, axis=-1, keepdims=True)
        m = jnp.where(m > -0.5 * MASK_BIG, m, 0.0)
        p = jnp.exp2(s - m)
        d = jnp.sum(p, axis=-1, keepdims=True)
        p = p / jnp.where(d > 0, d, 1.0)
        oc_ref[:, hs] = jnp.dot(p.astype(BF16), vc, preferred_element_type=F32).astype(oc_ref.dtype)
        psum = psum + p
    ov = ov_ref[...]
    nt = (((1,), (1,)), ((), ()))
    p1, p2, p3 = _split3(psum)
    psel = (lax.dot_general(ov, p1, nt, preferred_element_type=F32)
            + lax.dot_general(ov, p2, nt, preferred_element_type=F32)
            + lax.dot_general(ov, p3, nt, preferred_element_type=F32))
    n_idx = lax.broadcasted_iota(jnp.int32, (n_sel, tq), 0)
    tt = t0 + lax.broadcasted_iota(jnp.int32, (n_sel, tq), 1)
    cur = tt // SEL_BLOCK
    forced = jnp.where(n_idx == 0, 1.0, jnp.where(n_idx == cur, 1.0, jnp.where(n_idx == cur - 1, 1.0, 0.0)))
    score = jnp.where(forced > 0.5, FORCED_SCORE, jnp.where(n_idx * SEL_BLOCK <= tt, psel, -1.0))
    cnt = _rank_count(score, n_sel)
    member = jnp.where(cnt < float(min(SEL_TOPN, n_sel)), 1.0, 0.0)
    if n_sel < LANES:
        member = jnp.concatenate([member, jnp.zeros((LANES - n_sel, tq), F32)], axis=0)
    mem_ref[...] = member.T.astype(mem_ref.dtype)


def _nsa_cmp(proj, kc, vc, tab, overlap, b, s, tq=256):
    g = NSA_KV_HEADS
    nc = kc.shape[2]
    n_sel = s // SEL_BLOCK
    gw = NSA_GROUP * HEAD_DIM
    assert tq % CMP_STRIDE == 0 and REL_MAX_DIST % CMP_STRIDE == 0
    n_k = REL_MAX_DIST // CMP_STRIDE
    rho = (np.arange(tq)[:, None] - (CMP_BLOCK - 1)) % CMP_STRIDE
    dd = np.concatenate([rho + CMP_STRIDE * np.arange(n_k)[None, :], np.full((tq, 1), REL_MAX_DIST)], axis=1)
    gtab = jnp.transpose(_bias_of_dist(tab, dd), (2, 0, 1)).astype(F32)
    bias_c = jnp.concatenate([gtab, jnp.full(gtab.shape[:2] + (1,), -MASK_BIG, F32),
                              jnp.zeros(gtab.shape[:2] + (LANES - n_k - 2,), F32)], axis=2)
    body = functools.partial(_nsa_cmp_body, tq=tq, nc=nc, n_sel=n_sel)
    return pl.pallas_call(
        body,
        grid=(b, g, s // tq),
        in_specs=[pl.BlockSpec((None, tq, gw), lambda bi, gi, i: (bi, i, gi)),
                  pl.BlockSpec((None, None, nc, HEAD_DIM), lambda bi, gi, i: (bi, gi, 0, 0)),
                  pl.BlockSpec((None, None, nc, HEAD_DIM), lambda bi, gi, i: (bi, gi, 0, 0)),
                  pl.BlockSpec((NSA_GROUP, tq, LANES), lambda bi, gi, i: (gi, 0, 0)),
                  pl.BlockSpec((n_sel, nc), lambda bi, gi, i: (0, 0))],
        out_specs=[pl.BlockSpec((None, tq, gw), lambda bi, gi, i: (bi, i, gi)),
                   pl.BlockSpec((None, None, tq, LANES), lambda bi, gi, i: (bi, gi, i, 0))],
        out_shape=[jax.ShapeDtypeStruct((b, s, NSA_HEADS * HEAD_DIM), BF16),
                   jax.ShapeDtypeStruct((b, g, s, LANES), BF16)],
        compiler_params=_cparams(("parallel", "parallel", "parallel")),
        name="nsa_cmp_select",
    )(proj, kc, vc, bias_c, overlap)


def _moba_gate_body(q_ref, k_ref, mem_ref, *, s, nblk):
    k = k_ref[...].astype(F32)
    kmean = jnp.mean(k.reshape(nblk, MOBA_BLOCK, HEAD_DIM), axis=1)
    k1 = kmean.astype(BF16)
    k2 = (kmean - k1.astype(F32)).astype(BF16)
    q = q_ref[...]
    nt = (((1,), (1,)), ((), ()))
    gate = (lax.dot_general(k1, q, nt, preferred_element_type=F32)
            + lax.dot_general(k2, q, nt, preferred_element_type=F32))
    n_idx = lax.broadcasted_iota(jnp.int32, (nblk, s), 0)
    own = lax.broadcasted_iota(jnp.int32, (nblk, s), 1) // MOBA_BLOCK
    past = n_idx < own
    score = jnp.where(past, gate, -MASK_BIG)
    cnt = _rank_count(score, nblk)
    n_top = max(1, min(MOBA_TOPK, nblk - 1))
    sel = jnp.where(past, jnp.where(cnt < float(n_top), 1.0, 0.0), 0.0)
    member = jnp.where(n_idx == own, 1.0, sel)
    member = jnp.concatenate([member, jnp.zeros((LANES - nblk, s), F32)], axis=0)
    mem_ref[...] = member.T.astype(mem_ref.dtype)


def _moba_gate(proj, b, s):
    h = MOBA_HEADS
    nblk = s // MOBA_BLOCK
    return pl.pallas_call(
        functools.partial(_moba_gate_body, s=s, nblk=nblk),
        grid=(b, h),
        in_specs=[pl.BlockSpec((None, s, HEAD_DIM), lambda bi, hi: (bi, 0, hi)),
                  pl.BlockSpec((None, s, HEAD_DIM), lambda bi, hi: (bi, 0, h + hi))],
        out_specs=pl.BlockSpec((None, None, s, LANES), lambda bi, hi: (bi, hi, 0, 0)),
        out_shape=jax.ShapeDtypeStruct((b, h, s, LANES), BF16),
        compiler_params=_cparams(("parallel", "parallel")),
        name="moba_gate",
    )(proj, proj)


def _flash_body(qi_ref, ki_ref, bo_ref, fl_ref, *refs, nh, ratio, nm, n_near, has_far):
    if nm:
        (q_ref, k_ref, v_ref, bvec_ref, mem_ref, et_ref, o_ref,
         m_ref, l_ref, acc_ref, sh_ref, al_ref, bias_ref, s_ref, p_ref) = refs
    else:
        q_ref, k_ref, v_ref, bvec_ref, o_ref, m_ref, l_ref, acc_ref, sh_ref, al_ref, bias_ref, s_ref, p_ref = refs
    del qi_ref, ki_ref
    p = pl.program_id(2)
    flag = fl_ref[p]
    bo = bo_ref[p]
    t = q_ref.shape[0]
    rows = 64

    @pl.when(p == 0)
    def _():
        for h in range(nh):
            for o in range(n_near):
                vec = bvec_ref[h, o][0:1, :] * LOG2E
                for rc in range(t // rows):
                    x = pltpu.roll(jnp.broadcast_to(vec, (rows, 2 * t)), rc * rows, 1, stride=1, stride_axis=0)
                    bias_ref[h, o, rc * rows:(rc + 1) * rows, :] = x[:, :t]

    @pl.when((flag & 1) != 0)
    def _():
        m_ref[...] = jnp.full(m_ref.shape, M_INIT, F32)
        l_ref[...] = jnp.zeros(l_ref.shape, F32)
        acc_ref[...] = jnp.zeros(acc_ref.shape, F32)

    nt = (((1,), (1,)), ((), ()))
    reps = t // LANES

    def step(near):
        for h in range(nh):
            hs = slice(h * HEAD_DIM, (h + 1) * HEAD_DIM)
            kv = h // ratio
            ks = slice(kv * HEAD_DIM, (kv + 1) * HEAD_DIM)
            q = q_ref[:, hs]
            k = k_ref[:, ks]
            if nm:
                mneg = mem_ref[h // (nh // nm)] - 1.0
                q = jnp.concatenate([q, mneg.astype(BF16)], axis=1)
                k = jnp.concatenate([k, et_ref[...]], axis=1)
            sc = lax.dot_general(q, k, nt, preferred_element_type=F32)
            m_prev = m_ref[h]
            if near:
                sc = sc + bias_ref[h, bo]
                m_new = jnp.maximum(m_prev, jnp.max(sc, axis=-1, keepdims=True))
                sh_ref[h] = m_new
            else:
                cfar = bvec_ref[h, n_near][0:1, 0:LANES] * LOG2E
                m_new = jnp.maximum(m_prev, jnp.max(sc, axis=-1, keepdims=True) + cfar)
                sh_ref[h] = m_new - cfar
            s_ref[h] = sc
            al_ref[h] = jnp.exp2(m_prev - m_new)
            m_ref[h] = m_new
        for h in range(nh):
            ks = slice(h // ratio * HEAD_DIM, (h // ratio + 1) * HEAD_DIM)
            for rc in range(t // rows):
                rs = slice(rc * rows, (rc + 1) * rows)
                pm = jnp.exp2(s_ref[h, rs, :] - jnp.tile(sh_ref[h, rs, :], (1, reps)))
                l_ref[h, rs, :] = al_ref[h, rs, :] * l_ref[h, rs, :] + jnp.sum(pm, axis=-1, keepdims=True)
                p_ref[h, rs, :] = pm.astype(BF16)
            acc_ref[h] = al_ref[h] * acc_ref[h] + jnp.dot(p_ref[h], v_ref[:, ks], preferred_element_type=F32)

    if has_far:
        pl.when(bo < n_near)(lambda: step(True))
        pl.when(bo >= n_near)(lambda: step(False))
    else:
        step(True)

    @pl.when((flag & 2) != 0)
    def _():
        for h in range(nh):
            l = l_ref[h]
            o_ref[:, h * HEAD_DIM:(h + 1) * HEAD_DIM] = (acc_ref[h] / jnp.where(l > 0, l, 1.0)).astype(o_ref.dtype)


def _flash(q_arr, q_off, k_arr, k_off, v_arr, v_off, bias, n_heads, ratio, nh, pairs, has_far, member=None,
           et=None, name="flash"):
    b, s, _ = q_arr.shape
    t = ATT_TILE
    nkv = nh // ratio
    ng = n_heads // nh
    qi = jnp.asarray([p[0] for p in pairs], jnp.int32)
    ki = jnp.asarray([p[1] for p in pairs], jnp.int32)
    bo = jnp.asarray([p[2] for p in pairs], jnp.int32)
    fl = jnp.asarray([p[3] for p in pairs], jnp.int32)
    nb = bias.shape[1]
    nm = 0
    in_specs = [
        pl.BlockSpec((None, t, nh * HEAD_DIM), lambda bi, gi, p, qi, ki, bo, fl: (bi, qi[p], q_off + gi)),
        pl.BlockSpec((None, t, nkv * HEAD_DIM), lambda bi, gi, p, qi, ki, bo, fl: (bi, ki[p], k_off + gi)),
        pl.BlockSpec((None, t, nkv * HEAD_DIM), lambda bi, gi, p, qi, ki, bo, fl: (bi, ki[p], v_off + gi)),
        pl.BlockSpec((nh, nb, 8, 2 * t), lambda bi, gi, p, qi, ki, bo, fl: (gi, 0, 0, 0)),
    ]
    args = [q_arr, k_arr, v_arr, bias]
    if member is not None:
        nm = member.shape[1] // ng
        in_specs += [
            pl.BlockSpec((None, nm, t, LANES), lambda bi, gi, p, qi, ki, bo, fl: (bi, gi, qi[p], 0)),
            pl.BlockSpec((t, LANES), lambda bi, gi, p, qi, ki, bo, fl: (ki[p], 0)),
        ]
        args += [member, et]
    n_near = nb - 1 if has_far else nb
    body = functools.partial(_flash_body, nh=nh, ratio=ratio, nm=nm, n_near=n_near, has_far=has_far)
    return pl.pallas_call(
        body,
        grid_spec=pltpu.PrefetchScalarGridSpec(
            num_scalar_prefetch=4,
            grid=(b, ng, len(pairs)),
            in_specs=in_specs,
            out_specs=pl.BlockSpec((None, t, nh * HEAD_DIM), lambda bi, gi, p, qi, ki, bo, fl: (bi, qi[p], gi)),
            scratch_shapes=[pltpu.VMEM((nh, t, LANES), F32)] * 5 + [pltpu.VMEM((nh, n_near, t, t), F32),
                                                                    pltpu.VMEM((nh, t, t), F32),
                                                                    pltpu.VMEM((nh, t, t), BF16)],
        ),
        out_shape=jax.ShapeDtypeStruct((b, s, n_heads * HEAD_DIM), BF16),
        compiler_params=_cparams(("parallel", "parallel", "arbitrary"), vmem=ATT_VMEM_LIMIT),
        name=name,
    )(qi, ki, bo, fl, *args)


def _rel_bucket(dist):
    n = jnp.maximum(jnp.asarray(dist, jnp.int32), 0)
    max_exact = REL_BUCKETS // 2
    nf = jnp.maximum(n, 1).astype(jnp.float32)
    large = max_exact + (jnp.log(nf / max_exact) / math.log(REL_MAX_DIST / max_exact)
                         * (REL_BUCKETS - max_exact)).astype(jnp.int32)
    return jnp.where(n < max_exact, n, jnp.minimum(large, REL_BUCKETS - 1))


def _bias_of_dist(tab, dist):
    hit = _rel_bucket(dist)[..., None, None] == jnp.arange(REL_BUCKETS)[:, None]
    return jnp.sum(jnp.where(hit, tab, 0.0), axis=-2)


def _n_near(t):
    return -(-(REL_MAX_DIST - 1 + t) // t)


def _bias_vecs(tab, t, n_off, window=None):
    k = np.arange(2 * t)[None, :]
    dist = np.arange(n_off)[:, None] * t + np.where(k < t, -k, 2 * t - k)
    ok = dist >= 0
    if window is not None:
        ok &= dist < window
    bias = jnp.where(jnp.asarray(ok)[..., None], _bias_of_dist(tab, dist), -MASK_BIG)
    bias = jnp.transpose(bias, (2, 0, 1)).astype(F32)
    return jnp.broadcast_to(bias[:, :, None, :], (bias.shape[0], n_off, 8, 2 * t))


def _causal_pairs(nq, n_near):
    pairs = []
    for qi in range(nq):
        for ki in range(qi + 1):
            pairs.append((qi, ki, min(qi - ki, n_near), (1 if ki == 0 else 0) | (2 if ki == qi else 0)))
    return pairs


def _window_pairs(nq, n_back):
    pairs = []
    for qi in range(nq):
        lo = max(0, qi - n_back)
        for ki in range(lo, qi + 1):
            pairs.append((qi, ki, qi - ki, (1 if ki == lo else 0) | (2 if ki == qi else 0)))
    return pairs


def _block_onehot(s, blk):
    return jnp.asarray(np.where(np.arange(s)[:, None] // blk == np.arange(LANES)[None, :], MASK_BIG, 0.0), BF16)


def _merge_body(oc_ref, os_ref, ow_ref, gl_ref, ob_ref, gma_ref, gmb_ref, wa_ref, wb_ref, o_ref, oa_ref):
    @pl.when(pl.program_id(1) == 0)
    def _():
        gates = jax.nn.sigmoid(gl_ref[...])
        for h in range(NSA_HEADS):
            hs = slice(h * HEAD_DIM, (h + 1) * HEAD_DIM)
            mix = (gates[:, 3 * h:3 * h + 1] * oc_ref[:, hs].astype(F32)
                   + gates[:, 3 * h + 1:3 * h + 2] * os_ref[:, hs].astype(F32)
                   + gates[:, 3 * h + 2:3 * h + 3] * ow_ref[:, hs].astype(F32))
            oa_ref[:, hs] = mix.astype(BF16)

    ya = jnp.dot(oa_ref[...], wa_ref[...].astype(BF16), preferred_element_type=F32)
    yb = jnp.dot(ob_ref[...], wb_ref[...].astype(BF16), preferred_element_type=F32)
    o_ref[...] = (jax.nn.sigmoid(gma_ref[...]) * ya + jax.nn.sigmoid(gmb_ref[...]) * yb).astype(o_ref.dtype)


def _merge(o_c, o_s, o_w, gate_logits, o_b, gm, w_up_a, w_up_b, tm=1024, tn=512):
    t, ka = o_c.shape
    kb = o_b.shape[1]
    d = w_up_a.shape[1]
    tm = min(tm, t)
    nj = d // tn
    row = lambda i, j: (i, 0)
    return pl.pallas_call(
        _merge_body,
        grid=(t // tm, nj),
        in_specs=[pl.BlockSpec((tm, ka), row), pl.BlockSpec((tm, ka), row), pl.BlockSpec((tm, ka), row),
                  pl.BlockSpec((tm, LANES), row), pl.BlockSpec((tm, kb), row),
                  pl.BlockSpec((tm, tn), lambda i, j: (i, j)),
                  pl.BlockSpec((tm, tn), lambda i, j: (i, j + nj)),
                  pl.BlockSpec((ka, tn), lambda i, j: (0, j)),
                  pl.BlockSpec((kb, tn), lambda i, j: (0, j))],
        out_specs=pl.BlockSpec((tm, tn), lambda i, j: (i, j)),
        out_shape=jax.ShapeDtypeStruct((t, d), BF16),
        scratch_shapes=[pltpu.VMEM((tm, ka), BF16)],
        compiler_params=_cparams(("parallel", "arbitrary")),
        name="merge_up",
    )(o_c, o_s, o_w, gate_logits, o_b, gm, gm, w_up_a, w_up_b)


def _route_body(x_ref, g_ref, w_ref, b_ref, h_ref, info_ref, cnt_ref, carry_ref, *, tm):
    @pl.when(pl.program_id(0) == 0)
    def _():
        carry_ref[...] = jnp.zeros(carry_ref.shape, F32)

    x = x_ref[...]
    ms = jnp.mean(x * x, axis=-1, keepdims=True)
    h = x * lax.rsqrt(ms + RMS_EPS) * g_ref[...]
    _store_packed(h_ref, h, tm)
    w = w_ref[...]
    h1 = h.astype(BF16)
    h2 = (h - h1.astype(F32)).astype(BF16)
    w1 = w.astype(BF16)
    w2 = (w - w1.astype(F32)).astype(BF16)
    logits = (jnp.dot(h1, w1, preferred_element_type=F32) + jnp.dot(h1, w2, preferred_element_type=F32)
              + jnp.dot(h2, w1, preferred_element_type=F32)) + b_ref[...]
    lane = lax.broadcasted_iota(jnp.int32, (tm, LANES), 1)
    lanef = lane.astype(F32)

    is_g = lane < N_GROUPS
    gl = jnp.where(is_g, logits, -MASK_BIG)
    ge = jnp.where(is_g, jnp.exp(gl - jnp.max(gl, axis=-1, keepdims=True)), 0.0)
    gp = ge / jnp.sum(ge, axis=-1, keepdims=True)
    g_val = jnp.max(gp, axis=-1, keepdims=True)
    g_idx = jnp.min(jnp.where(gp == g_val, lanef, float(LANES)), axis=-1, keepdims=True)

    lane_grp = ((lane - N_GROUPS) // EXPERTS_PER_GROUP).astype(F32)
    in_e = jnp.where(lane >= N_GROUPS, jnp.where(lane < N_GROUPS + N_EXPERTS, 1.0, 0.0), 0.0)
    is_e = jnp.where(lane_grp == g_idx, in_e, 0.0) > 0.5
    el = jnp.where(is_e, logits, -MASK_BIG)
    ee = jnp.where(is_e, jnp.exp(el - jnp.max(el, axis=-1, keepdims=True)), 0.0)
    ep = jnp.where(is_e, ee / jnp.sum(ee, axis=-1, keepdims=True), -1.0)
    v1 = jnp.max(ep, axis=-1, keepdims=True)
    l1 = jnp.min(jnp.where(ep == v1, lanef, float(LANES)), axis=-1, keepdims=True)
    ep2 = jnp.where(lanef == l1, -1.0, ep)
    v2 = jnp.max(ep2, axis=-1, keepdims=True)
    l2 = jnp.min(jnp.where(ep2 == v2, lanef, float(LANES)), axis=-1, keepdims=True)
    vs = v1 + v2
    wt1 = g_val * v1 / vs
    wt2 = g_val * v2 / vs
    e1 = l1 - float(N_GROUPS)
    e2 = l2 - float(N_GROUPS)

    oh = jnp.where(lanef == e1, 1.0, jnp.where(lanef == e2, 1.0, 0.0))
    r_i = lax.broadcasted_iota(jnp.int32, (tm, tm), 0)
    c_i = lax.broadcasted_iota(jnp.int32, (tm, tm), 1)
    tri = jnp.where(r_i > c_i, 1.0, 0.0).astype(BF16)
    base = jnp.dot(tri, oh.astype(BF16), preferred_element_type=F32) + carry_ref[...]
    r1 = jnp.sum(jnp.where(lanef == e1, base, 0.0), axis=-1, keepdims=True)
    r2 = jnp.sum(jnp.where(lanef == e2, base, 0.0), axis=-1, keepdims=True)
    carry_ref[...] = carry_ref[...] + jnp.sum(oh, axis=0, keepdims=True)
    cnt_ref[...] = jnp.broadcast_to(carry_ref[...], cnt_ref.shape)
    info = jnp.where(lane == 0, e1, jnp.where(lane == 1, e2, jnp.where(lane == 2, wt1, jnp.where(
        lane == 3, wt2, jnp.where(lane == 4, r1, jnp.where(lane == 5, r2, 0.0))))))
    info_ref[...] = info


def _route(x1, g, w_gr, b_gr, tm=512):
    t, d = x1.shape
    return pl.pallas_call(
        functools.partial(_route_body, tm=tm),
        grid=(t // tm,),
        in_specs=[pl.BlockSpec((tm, d), lambda i: (i, 0)),
                  pl.BlockSpec((1, d), lambda i: (0, 0)),
                  pl.BlockSpec((d, LANES), lambda i: (0, 0)),
                  pl.BlockSpec((1, LANES), lambda i: (0, 0))],
        out_specs=[pl.BlockSpec((tm * ROW_SUB, LANES), lambda i: (i, 0)),
                   pl.BlockSpec((tm, LANES), lambda i: (i, 0)),
                   pl.BlockSpec((8, LANES), lambda i: (0, 0))],
        out_shape=[jax.ShapeDtypeStruct((t * ROW_SUB, LANES), jnp.uint32),
                   jax.ShapeDtypeStruct((t, LANES), F32),
                   jax.ShapeDtypeStruct((8, LANES), F32)],
        scratch_shapes=[pltpu.VMEM((1, LANES), F32)],
        compiler_params=_cparams(("arbitrary",)),
        name="moe_route",
    )(x1, g.reshape(1, d), w_gr, b_gr)


ROW_SUB = 8
U32 = jnp.uint32


def _pack_pairs(lo, hi):
    lo_b = lax.bitcast_convert_type(lo.astype(BF16).astype(F32), U32)
    hi_b = lax.bitcast_convert_type(hi.astype(BF16).astype(F32), U32)
    return lax.shift_right_logical(lo_b, U32(16)) | (hi_b & U32(0xFFFF0000))


def _unpack_pairs(w):
    lo = lax.bitcast_convert_type(lax.shift_left(w, U32(16)), F32)
    hi = lax.bitcast_convert_type(w & U32(0xFFFF0000), F32)
    return lo, hi


def _store_packed(ref, y, n):
    half = y.shape[1] // 2
    for s in range(ROW_SUB):
        cs = slice(s * LANES, (s + 1) * LANES)
        ref[pl.ds(s, n, stride=ROW_SUB), :] = _pack_pairs(y[:, cs], y[:, half + s * LANES:half + (s + 1) * LANES])


def _load_packed(ref, n):
    los, his = [], []
    for s in range(ROW_SUB):
        lo, hi = _unpack_pairs(ref[pl.ds(s, n, stride=ROW_SUB), :])
        los.append(lo)
        his.append(hi)
    return jnp.concatenate(los + his, axis=1)


def _row_copy(src_ref, src_row, dst_ref, dst_row, sem):
    return pltpu.make_async_copy(src_ref.at[pl.ds(pl.multiple_of(src_row * ROW_SUB, ROW_SUB), ROW_SUB)],
                                 dst_ref.at[pl.ds(pl.multiple_of(dst_row * ROW_SUB, ROW_SUB), ROW_SUB)], sem)


def _dispatch_body(dest_ref, h_ref, xs_in_ref, xs_ref, sem, *, tm):
    del xs_in_ref
    base = pl.program_id(0) * tm

    def issue(r, c):
        for k in range(EXPERT_TOPK):
            _row_copy(h_ref, r, xs_ref, dest_ref[EXPERT_TOPK * (base + r) + k], sem).start()
        return c

    lax.fori_loop(0, tm, issue, 0, unroll=4)

    def drain(r, c):
        for k in range(EXPERT_TOPK):
            _row_copy(h_ref, r, xs_ref, dest_ref[EXPERT_TOPK * (base + r) + k], sem).wait()
        return c

    lax.fori_loop(0, tm, drain, 0, unroll=4)


def _dispatch(dest, hp, n_rows, tm=512):
    t = hp.shape[0] // ROW_SUB
    xs0 = jnp.zeros((n_rows * ROW_SUB, LANES), U32)
    return pl.pallas_call(
        functools.partial(_dispatch_body, tm=tm),
        grid_spec=pltpu.PrefetchScalarGridSpec(
            num_scalar_prefetch=1,
            grid=(t // tm,),
            in_specs=[pl.BlockSpec((tm * ROW_SUB, LANES), lambda i, dest: (i, 0)),
                      pl.BlockSpec(memory_space=pl.ANY)],
            out_specs=pl.BlockSpec(memory_space=pl.ANY),
            scratch_shapes=[pltpu.SemaphoreType.DMA(())],
        ),
        out_shape=jax.ShapeDtypeStruct((n_rows * ROW_SUB, LANES), U32),
        input_output_aliases={2: 0},
        compiler_params=_cparams(("arbitrary",)),
        name="moe_dispatch",
    )(dest, hp, xs0)


def _expert_body(nu_ref, sq_ref, es_ref, ns_ref, x_ref, wg_hbm, wu_hbm, wd_hbm, y_ref,
                 wg_b, wu_b, wd_b, wg_s, wu_s, wd_s, sem):
    i = pl.program_id(0)
    nu = nu_ref[0]
    ns = ns_ref[0]

    def weight_copies(seq, slot):
        e = es_ref[seq]
        return (pltpu.make_async_copy(wg_hbm.at[e], wg_b.at[slot], sem.at[slot]),
                pltpu.make_async_copy(wu_hbm.at[e], wu_b.at[slot], sem.at[slot]),
                pltpu.make_async_copy(wd_hbm.at[e], wd_b.at[slot], sem.at[slot]))

    def start_weights(seq, slot):
        for c in weight_copies(seq, slot):
            c.start()

    @pl.when(i == 0)
    def _():
        start_weights(0, 0)

        @pl.when(ns > 1)
        def _():
            start_weights(1, 1)

    s = sq_ref[i]
    first = (i == 0) | (s != sq_ref[jnp.maximum(i - 1, 0)])

    @pl.when((i < nu) & first)
    def _():
        slot = lax.rem(s, 2)
        for c in weight_copies(s, slot):
            c.wait()
        wg_s[...] = wg_b[slot].astype(BF16)
        wu_s[...] = wu_b[slot].astype(BF16)
        wd_s[...] = wd_b[slot].astype(BF16)

        @pl.when(s + 2 < ns)
        def _():
            start_weights(s + 2, slot)

    @pl.when(i < nu)
    def _():
        x = _load_packed(x_ref, MOE_ROWS).astype(BF16)
        g = jnp.dot(x, wg_s[...], preferred_element_type=F32)
        u = jnp.dot(x, wu_s[...], preferred_element_type=F32)
        mid = (jax.nn.silu(g) * u).astype(BF16)
        _store_packed(y_ref, jnp.dot(mid, wd_s[...], preferred_element_type=F32), MOE_ROWS)

    @pl.when(i >= nu_ref[0])
    def _():
        y_ref[...] = jnp.zeros(y_ref.shape, y_ref.dtype)


def _experts(n_used, seq_of_block, expert_of_seq, n_seq, xs, w_gate, w_up, w_down):
    n_blocks = xs.shape[0] // (MOE_ROWS * ROW_SUB)
    _, d, ff = w_gate.shape
    assert d == 2 * ROW_SUB * LANES, "a packed row must be exactly one (8,128) tile"
    blk = lambda i, nu, sq, es, ns: (jnp.minimum(i, nu[0] - 1), 0)
    hbm = pl.BlockSpec(memory_space=pl.ANY)
    return pl.pallas_call(
        _expert_body,
        grid_spec=pltpu.PrefetchScalarGridSpec(
            num_scalar_prefetch=4,
            grid=(n_blocks,),
            in_specs=[pl.BlockSpec((MOE_ROWS * ROW_SUB, LANES), blk), hbm, hbm, hbm],
            out_specs=pl.BlockSpec((MOE_ROWS * ROW_SUB, LANES), lambda i, nu, sq, es, ns: (i, 0)),
            scratch_shapes=[pltpu.VMEM((2, d, ff), F32), pltpu.VMEM((2, d, ff), F32), pltpu.VMEM((2, ff, d), F32),
                            pltpu.VMEM((d, ff), BF16), pltpu.VMEM((d, ff), BF16), pltpu.VMEM((ff, d), BF16),
                            pltpu.SemaphoreType.DMA((2,))],
        ),
        out_shape=jax.ShapeDtypeStruct(xs.shape, U32),
        compiler_params=_cparams(("arbitrary",)),
        name="moe_experts",
    )(n_used, seq_of_block, expert_of_seq, n_seq, xs, w_gate, w_up, w_down)


def _combine_body(dest_ref, x_ref, info_ref, g_ref, ys_ref, o_ref, buf0, buf1, sem, *, tm):
    i = pl.program_id(0)
    slot = lax.rem(i, 2)
    bufs = (buf0, buf1)

    def gather(tile, sl, wait):
        def body(r, c):
            for k in range(EXPERT_TOPK):
                cp = _row_copy(ys_ref, dest_ref[EXPERT_TOPK * (tile * tm + r) + k], bufs[k].at[sl], r, sem.at[sl])
                if wait:
                    cp.wait()
                else:
                    cp.start()
            return c
        lax.fori_loop(0, tm, body, 0, unroll=4)

    @pl.when(i == 0)
    def _():
        gather(0, 0, False)

    @pl.when(i + 1 < pl.num_programs(0))
    def _():
        gather(i + 1, 1 - slot, False)

    gather(i, slot, True)
    info = info_ref[...]
    y = x_ref[...] + (info[:, 2:3] * _load_packed(buf0.at[slot], tm) + info[:, 3:4] * _load_packed(buf1.at[slot], tm))
    ms = jnp.mean(y * y, axis=-1, keepdims=True)
    o_ref[...] = y * lax.rsqrt(ms + RMS_EPS) * g_ref[...]


def _combine(dest, x1, info, g, ys, tm=256):
    t, d = x1.shape
    return pl.pallas_call(
        functools.partial(_combine_body, tm=tm),
        grid_spec=pltpu.PrefetchScalarGridSpec(
            num_scalar_prefetch=1,
            grid=(t // tm,),
            in_specs=[pl.BlockSpec((tm, d), lambda i, dest: (i, 0)),
                      pl.BlockSpec((tm, LANES), lambda i, dest: (i, 0)),
                      pl.BlockSpec((1, d), lambda i, dest: (0, 0)),
                      pl.BlockSpec(memory_space=pl.ANY)],
            out_specs=pl.BlockSpec((tm, d), lambda i, dest: (i, 0)),
            scratch_shapes=[pltpu.VMEM((2, tm * ROW_SUB, LANES), U32), pltpu.VMEM((2, tm * ROW_SUB, LANES), U32),
                            pltpu.SemaphoreType.DMA((2,))],
        ),
        out_shape=jax.ShapeDtypeStruct((t, d), F32),
        compiler_params=_cparams(("arbitrary",)),
        name="moe_combine",
    )(dest, x1, info, g.reshape(1, d), ys)


def _nsa(proj, gate_cols, pe_k, w1_k, w2_k, pe_v, w1_v, w2_v, tab, b, s):
    del gate_cols
    g, dh = NSA_KV_HEADS, HEAD_DIM
    qw = NSA_HEADS * dh
    nc = s // CMP_STRIDE

    def blocks16(col0):
        a = proj[:, :, col0:col0 + g * dh].reshape(b, nc, CMP_STRIDE, g, dh)
        return a.transpose(0, 3, 1, 2, 4).reshape(b * g, nc, CMP_STRIDE * dh)

    kc = _compress(blocks16(qw), pe_k, w1_k, w2_k).reshape(b, g, nc, dh)
    vc = _compress(blocks16(qw + g * dh), pe_v, w1_v, w2_v).reshape(b, g, nc, dh)

    c_start = np.arange(nc)[None, :] * CMP_STRIDE
    n_sel = s // SEL_BLOCK
    sb = np.arange(n_sel)[:, None] * SEL_BLOCK
    overlap = jnp.asarray((c_start < sb + SEL_BLOCK) & (c_start + CMP_BLOCK > sb), BF16)
    o_c, member = _nsa_cmp(proj, kc, vc, tab, overlap, b, s)

    t = ATT_TILE
    nq = s // t
    nn = _n_near(t)
    nh = ATT_HEADS_PER_STEP
    kblk = qw // (nh // NSA_GROUP * dh)
    per = g // (nh // NSA_GROUP)
    bias_d = _bias_vecs(tab, t, nn + 1)
    o_s = _flash(proj, 0, proj, kblk + 2 * per, proj, kblk + 3 * per, bias_d, NSA_HEADS, NSA_GROUP, nh,
                 _causal_pairs(nq, nn), True, member=member, et=_block_onehot(s, SEL_BLOCK), name="nsa_selected")
    n_back = -(-WINDOW // t)
    bias_w = _bias_vecs(tab, t, n_back + 1, window=WINDOW)
    o_w = _flash(proj, 0, proj, kblk + 4 * per, proj, kblk + 5 * per, bias_w, NSA_HEADS, NSA_GROUP, nh,
                 _window_pairs(nq, n_back), False, name="nsa_window")
    return o_c, o_s, o_w


def _moba(proj, tab, b, s):
    member = _moba_gate(proj, b, s)
    t = ATT_TILE
    nn = _n_near(t)
    nh = ATT_HEADS_PER_STEP
    ng = MOBA_HEADS // nh
    bias_d = _bias_vecs(tab, t, nn + 1)
    return _flash(proj, 0, proj, ng, proj, 2 * ng, bias_d, MOBA_HEADS, 1, nh, _causal_pairs(s // t, nn),
                  True, member=member, et=_block_onehot(s, MOBA_BLOCK), name="moba_attn")


def _moe(x1, g_ffn, w_group, b_group, w_router, b_router, w_gate, w_up, w_down, g_final):
    t, d = x1.shape
    ng, _, epg = w_router.shape
    w_gr = jnp.concatenate([w_group, jnp.transpose(w_router, (1, 0, 2)).reshape(d, ng * epg),
                            jnp.zeros((d, LANES - ng - ng * epg), F32)], axis=1)
    b_gr = jnp.concatenate([b_group, b_router.reshape(-1), jnp.zeros((LANES - ng - ng * epg,), F32)]).reshape(1, LANES)
    h, info, cnt = _route(x1, g_ffn, w_gr, b_gr)
    n_e = ng * epg
    n_assign = t * EXPERT_TOPK
    n_blocks = -(-(n_assign + n_e * (MOE_ROWS - 1)) // MOE_ROWS)
    counts = cnt[0, :n_e].astype(jnp.int32)
    padded = (counts + MOE_ROWS - 1) // MOE_ROWS * MOE_ROWS
    pad_end = jnp.cumsum(padded)
    pad_start = pad_end - padded
    expert = info[:, 0:EXPERT_TOPK].astype(jnp.int32)
    rank = info[:, 4:4 + EXPERT_TOPK].astype(jnp.int32)
    e_ids = jnp.arange(n_e, dtype=jnp.int32)
    dest = (jnp.sum(jnp.where(expert[..., None] == e_ids, pad_start, 0), axis=-1) + rank).reshape(-1)
    block_row0 = jnp.arange(n_blocks, dtype=jnp.int32) * MOE_ROWS
    block_e = jnp.minimum(jnp.sum((pad_end[None, :] <= block_row0[:, None]).astype(jnp.int32), axis=1), n_e - 1)
    n_used = (pad_end[-1:] // MOE_ROWS).astype(jnp.int32)
    owns = counts > 0
    seq_of_expert = jnp.cumsum(owns.astype(jnp.int32)) - 1
    n_seq = jnp.sum(owns.astype(jnp.int32)).reshape(1)
    expert_of_seq = jnp.sum(jnp.where(owns[None, :] & (seq_of_expert[None, :] == e_ids[:, None]), e_ids[None, :], 0),
                            axis=1)
    seq_of_block = jnp.sum(jnp.where(block_e[:, None] == e_ids[None, :], seq_of_expert[None, :], 0), axis=1)
    xs = _dispatch(dest, h, n_blocks * MOE_ROWS)
    ys = _experts(n_used, seq_of_block, expert_of_seq, n_seq, xs, w_gate, w_up, w_down)
    return _combine(dest, x1, info, g_final, ys)


def kernel(x, rel_bias, norm_mix, w_in, cmp_pe_k, cmp_w1_k, cmp_w2_k, cmp_pe_v, cmp_w1_v, cmp_w2_v, w_up_nsa,
           w_up_moba, w_out, norm_ffn, w_group, b_group, w_router, b_router, w_exp_gate, w_exp_up, w_exp_down,
           final_norm):
    b, s, d = x.shape
    t = b * s
    depth = w_in.shape[0]
    tab_a = rel_bias[:, :NSA_HEADS]
    tab_b = rel_bias[:, NSA_HEADS:]
    a_cols = NSA_HEADS * HEAD_DIM + 6 * NSA_KV_HEADS * HEAD_DIM
    gate_cols = 3 * NSA_HEADS
    b_cols = 3 * MOBA_HEADS * HEAD_DIM
    xt = x.reshape(t, d)
    out = None
    for l in range(depth):
        h = _rmsnorm(xt, norm_mix[l], BF16)
        wt = jnp.swapaxes(w_in[l], 0, 1)
        b_col0 = a_cols + gate_cols
        proj_a = _matmul(h, wt, 0, a_cols, BF16, scaled_cols=NSA_HEADS * HEAD_DIM, col_scale=Q_SCALE,
                         w_transposed=True, name="in_proj_a").reshape(b, s, a_cols)
        gate_a = _matmul(h, wt, a_cols, LANES, F32, w_transposed=True, tn=LANES, name="in_proj_gate")
        proj_b = _matmul(h, wt, b_col0, b_cols, BF16, scaled_cols=MOBA_HEADS * HEAD_DIM, col_scale=Q_SCALE,
                         w_transposed=True, name="in_proj_b").reshape(b, s, b_cols)
        gm = _matmul(h, wt, b_col0 + b_cols, 2 * d, F32, w_transposed=True, name="in_proj_gm")
        o_c, o_s, o_w = _nsa(proj_a, gate_cols, cmp_pe_k[l], cmp_w1_k[l], cmp_w2_k[l],
                             cmp_pe_v[l], cmp_w1_v[l], cmp_w2_v[l], tab_a, b, s)
        o_b = _moba(proj_b, tab_b, b, s)
        merged = _merge(o_c.reshape(t, -1), o_s.reshape(t, -1), o_w.reshape(t, -1), gate_a,
                        o_b.reshape(t, -1), gm, w_up_nsa[l], w_up_moba[l])
        x1 = _matmul(merged, w_out[l], 0, d, F32, res=xt, name="out_proj")
        assert l == depth - 1, "only the last layer's MoE is fused with the final norm"
        out = _moe(x1, norm_ffn[l], w_group[l], b_group[l], w_router[l], b_router[l],
                   w_exp_gate[l], w_exp_up[l], w_exp_down[l], final_norm)
    return out.reshape(b, s, d)
```

```python
import functools
import math

import numpy as np
import jax
import jax.numpy as jnp
from jax import lax
from jax.experimental import pallas as pl
from jax.experimental.pallas import tpu as pltpu

F32 = jnp.float32
BF16 = jnp.bfloat16

HEAD_DIM = 128
NSA_HEADS = 8
NSA_KV_HEADS = 2
NSA_GROUP = NSA_HEADS // NSA_KV_HEADS
CMP_BLOCK = 32
CMP_STRIDE = 16
SEL_BLOCK = 64
SEL_TOPN = 16
WINDOW = 512
FORCED_SCORE = 1e4
MOBA_HEADS = 8
MOBA_BLOCK = 256
MOBA_TOPK = 3
REL_BUCKETS = 32
REL_MAX_DIST = 128
N_GROUPS = 8
EXPERTS_PER_GROUP = 8
N_EXPERTS = N_GROUPS * EXPERTS_PER_GROUP
EXPERT_TOPK = 2
RMS_EPS = 1e-6

LANES = 128
ATT_TILE = 512
ATT_HEADS_PER_STEP = 8
ATT_VMEM_LIMIT = 56 * 1024 * 1024
MOE_ROWS = 256
MASK_BIG = 1e30
M_INIT = -3e38
LOG2E = math.log2(math.e)
Q_SCALE = HEAD_DIM ** -0.5 * LOG2E
VMEM_LIMIT = 48 * 1024 * 1024


def _cparams(sem, vmem=VMEM_LIMIT, flags=None):
    return pltpu.CompilerParams(dimension_semantics=sem, vmem_limit_bytes=vmem, flags=flags)


def _rmsnorm_body(x_ref, g_ref, o_ref):
    x = x_ref[...]
    ms = jnp.mean(x * x, axis=-1, keepdims=True)
    o_ref[...] = (x * lax.rsqrt(ms + RMS_EPS) * g_ref[...]).astype(o_ref.dtype)


def _rmsnorm(x, g, out_dtype, tm=512):
    t, d = x.shape
    return pl.pallas_call(
        _rmsnorm_body,
        grid=(t // tm,),
        in_specs=[pl.BlockSpec((tm, d), lambda i: (i, 0)),
                  pl.BlockSpec((1, d), lambda i: (0, 0))],
        out_specs=pl.BlockSpec((tm, d), lambda i: (i, 0)),
        out_shape=jax.ShapeDtypeStruct((t, d), out_dtype),
        compiler_params=_cparams(("parallel",)),
        name="rmsnorm",
    )(x, g.reshape(1, d))


def _mm_body(*refs, has_res, n_scaled, col_scale, w_transposed):
    if has_res:
        a_ref, w_ref, r_ref, o_ref = refs
    else:
        a_ref, w_ref, o_ref = refs
    w = w_ref[...].astype(BF16)
    if w_transposed:
        acc = lax.dot_general(a_ref[...], w, (((1,), (1,)), ((), ())), preferred_element_type=F32)
    else:
        acc = jnp.dot(a_ref[...], w, preferred_element_type=F32)
    if n_scaled:
        acc = acc * jnp.where(pl.program_id(1) < n_scaled, col_scale, 1.0)
    if has_res:
        acc = acc + r_ref[...]
    o_ref[...] = acc.astype(o_ref.dtype)


def _matmul(a, w, col0, ncols, out_dtype, res=None, scaled_cols=0, col_scale=1.0, w_transposed=False,
            tm=2048, tn=512, name="matmul"):
    t, k = a.shape
    tn = min(tn, ncols)
    tm = min(tm, t)
    assert ncols % tn == 0 and t % tm == 0 and scaled_cols % tn == 0
    if w_transposed:
        assert col0 % 8 == 0
        w_spec = pl.BlockSpec((pl.Element(tn), pl.Element(k)),
                              lambda i, j: (pl.multiple_of(col0 + j * tn, 8), 0))
    else:
        assert col0 % tn == 0
        off = col0 // tn
        w_spec = pl.BlockSpec((k, tn), lambda i, j: (0, j + off))
    in_specs = [pl.BlockSpec((tm, k), lambda i, j: (i, 0)), w_spec]
    args = [a, w]
    if res is not None:
        in_specs.append(pl.BlockSpec((tm, tn), lambda i, j: (i, j)))
        args.append(res)
    return pl.pallas_call(
        functools.partial(_mm_body, has_res=res is not None, n_scaled=scaled_cols // tn, col_scale=col_scale,
                          w_transposed=w_transposed),
        grid=(t // tm, ncols // tn),
        in_specs=in_specs,
        out_specs=pl.BlockSpec((tm, tn), lambda i, j: (i, j)),
        out_shape=jax.ShapeDtypeStruct((t, ncols), out_dtype),
        compiler_params=_cparams(("parallel", "parallel")),
        name=name,
    )(*args)


def _compress_body(u_ref, pe_ref, w1_ref, w2_ref, o_ref, *, nc):
    u = u_ref[...]
    w1 = w1_ref[...].astype(BF16)
    half = u.shape[1]
    a = jnp.dot(u, w1[:half], preferred_element_type=F32)
    b = jnp.dot(u, w1[half:], preferred_element_type=F32)
    peb = jnp.dot(pe_ref[...].astype(BF16), w1, preferred_element_type=F32)[0:1]
    pre = a + pltpu.roll(b, nc - 1, 0) + peb
    hid = jax.nn.gelu(pre)
    o_ref[...] = jnp.dot(hid.astype(BF16), w2_ref[...].astype(BF16),
                         preferred_element_type=F32).astype(o_ref.dtype)


def _compress(u, pe, w1, w2):
    bg, nc, kk = u.shape
    hid = w1.shape[1]
    dh = w2.shape[1]
    pe8 = jnp.broadcast_to(pe.reshape(1, -1), (16, pe.size))
    return pl.pallas_call(
        functools.partial(_compress_body, nc=nc),
        grid=(bg,),
        in_specs=[pl.BlockSpec((None, nc, kk), lambda i: (i, 0, 0)),
                  pl.BlockSpec((16, 2 * kk), lambda i: (0, 0)),
                  pl.BlockSpec((2 * kk, hid), lambda i: (0, 0)),
                  pl.BlockSpec((hid, dh), lambda i: (0, 0))],
        out_specs=pl.BlockSpec((None, nc, dh), lambda i: (i, 0, 0)),
        out_shape=jax.ShapeDtypeStruct((bg, nc, dh), BF16),
        compiler_params=_cparams(("parallel",)),
        name="nsa_compress",
    )(u, pe8, w1, w2)


def _split3(x):
    p1 = x.astype(BF16)
    r = x - p1.astype(F32)
    p2 = r.astype(BF16)
    p3 = (r - p2.astype(F32)).astype(BF16)
    return p1, p2, p3


def _rank_count(score, n_rows):
    groups = []
    for g0 in range(0, n_rows, 8):
        sg = score[g0:min(g0 + 8, n_rows), :]
        n_iota = g0 + lax.broadcasted_iota(jnp.int32, sg.shape, 0)
        cnt = jnp.zeros(sg.shape, F32)
        for m in range(n_rows):
            row = score[m:m + 1, :]
            if m < g0:
                beats = row >= sg
            elif m >= g0 + 8:
                beats = row > sg
            else:
                tie = jnp.where(n_iota > m, 1.0, 0.0)
                beats = jnp.where(row > sg, 1.0, jnp.where(row == sg, tie, 0.0)) > 0.5
            cnt = cnt + jnp.where(beats, 1.0, 0.0)
        groups.append(cnt)
    return jnp.concatenate(groups, axis=0) if len(groups) > 1 else groups[0]


def _nsa_cmp_body(q_ref, kc_ref, vc_ref, bias_ref, ov_ref, oc_ref, mem_ref, *, tq, nc, n_sel):
    t0 = pl.program_id(2) * tq
    kc = kc_ref[...]
    vc = vc_ref[...]
    t_idx = t0 + lax.broadcasted_iota(jnp.int32, (tq, nc), 0)
    c_idx = lax.broadcasted_iota(jnp.int32, (tq, nc), 1)
    dist = t_idx - (c_idx * CMP_STRIDE + (CMP_BLOCK - 1))
    n_k = REL_MAX_DIST // CMP_STRIDE
    kidx = jnp.where(dist < 0, n_k + 1, jnp.minimum(lax.shift_right_logical(dist, 4), n_k))
    assert CMP_STRIDE == 16
    psum = jnp.zeros((tq, nc), F32)
    for j in range(NSA_GROUP):
        hs = slice(j * HEAD_DIM, (j + 1) * HEAD_DIM)
        gt = bias_ref[j] * LOG2E
        bias = jnp.concatenate([jnp.take_along_axis(gt, kidx[:, c0:c0 + LANES], axis=1)
                                for c0 in range(0, nc, LANES)], axis=1)
        s = lax.dot_general(q_ref[:, hs], kc, (((1,), (1,)), ((), ())), preferred_element_type=F32) + bias
        m = jnp.max(s, axis=-1, keepdims=True)
        m = jnp.where(m > -0.5 * MASK_BIG, m, 0.0)
        p = jnp.exp2(s - m)
        d = jnp.sum(p, axis=-1, keepdims=True)
        p = p / jnp.where(d > 0, d, 1.0)
        oc_ref[:, hs] = jnp.dot(p.astype(BF16), vc, preferred_element_type=F32).astype(oc_ref.dtype)
        psum = psum + p
    ov = ov_ref[...]
    nt = (((1,), (1,)), ((), ()))
    p1, p2, p3 = _split3(psum)
    psel = (lax.dot_general(ov, p1, nt, preferred_element_type=F32)
            + lax.dot_general(ov, p2, nt, preferred_element_type=F32)
            + lax.dot_general(ov, p3, nt, preferred_element_type=F32))
    n_idx = lax.broadcasted_iota(jnp.int32, (n_sel, tq), 0)
    tt = t0 + lax.broadcasted_iota(jnp.int32, (n_sel, tq), 1)
    cur = tt // SEL_BLOCK
    forced = jnp.where(n_idx == 0, 1.0, jnp.where(n_idx == cur, 1.0, jnp.where(n_idx == cur - 1, 1.0, 0.0)))
    score = jnp.where(forced > 0.5, FORCED_SCORE, jnp.where(n_idx * SEL_BLOCK <= tt, psel, -1.0))
    cnt = _rank_count(score, n_sel)
    member = jnp.where(cnt < float(min(SEL_TOPN, n_sel)), 1.0, 0.0)
    if n_sel < LANES:
        member = jnp.concatenate([member, jnp.zeros((LANES - n_sel, tq), F32)], axis=0)
    mem_ref[...] = member.T.astype(mem_ref.dtype)


def _nsa_cmp(proj, kc, vc, tab, overlap, b, s, tq=256):
    g = NSA_KV_HEADS
    nc = kc.shape[2]
    n_sel = s // SEL_BLOCK
    gw = NSA_GROUP * HEAD_DIM
    assert tq % CMP_STRIDE == 0 and REL_MAX_DIST % CMP_STRIDE == 0
    n_k = REL_MAX_DIST // CMP_STRIDE
    rho = (np.arange(tq)[:, None] - (CMP_BLOCK - 1)) % CMP_STRIDE
    dd = np.concatenate([rho + CMP_STRIDE * np.arange(n_k)[None, :], np.full((tq, 1), REL_MAX_DIST)], axis=1)
    gtab = jnp.transpose(_bias_of_dist(tab, dd), (2, 0, 1)).astype(F32)
    bias_c = jnp.concatenate([gtab, jnp.full(gtab.shape[:2] + (1,), -MASK_BIG, F32),
                              jnp.zeros(gtab.shape[:2] + (LANES - n_k - 2,), F32)], axis=2)
    body = functools.partial(_nsa_cmp_body, tq=tq, nc=nc, n_sel=n_sel)
    return pl.pallas_call(
        body,
        grid=(b, g, s // tq),
        in_specs=[pl.BlockSpec((None, tq, gw), lambda bi, gi, i: (bi, i, gi)),
                  pl.BlockSpec((None, None, nc, HEAD_DIM), lambda bi, gi, i: (bi, gi, 0, 0)),
                  pl.BlockSpec((None, None, nc, HEAD_DIM), lambda bi, gi, i: (bi, gi, 0, 0)),
                  pl.BlockSpec((NSA_GROUP, tq, LANES), lambda bi, gi, i: (gi, 0, 0)),
                  pl.BlockSpec((n_sel, nc), lambda bi, gi, i: (0, 0))],
        out_specs=[pl.BlockSpec((None, tq, gw), lambda bi, gi, i: (bi, i, gi)),
                   pl.BlockSpec((None, None, tq, LANES), lambda bi, gi, i: (bi, gi, i, 0))],
        out_shape=[jax.ShapeDtypeStruct((b, s, NSA_HEADS * HEAD_DIM), BF16),
                   jax.ShapeDtypeStruct((b, g, s, LANES), BF16)],
        compiler_params=_cparams(("parallel", "parallel", "parallel")),
        name="nsa_cmp_select",
    )(proj, kc, vc, bias_c, overlap)


def _moba_gate_body(q_ref, k_ref, mem_ref, *, s, nblk):
    k = k_ref[...].astype(F32)
    kmean = jnp.mean(k.reshape(nblk, MOBA_BLOCK, HEAD_DIM), axis=1)
    k1 = kmean.astype(BF16)
    k2 = (kmean - k1.astype(F32)).astype(BF16)
    q = q_ref[...]
    nt = (((1,), (1,)), ((), ()))
    gate = (lax.dot_general(k1, q, nt, preferred_element_type=F32)
            + lax.dot_general(k2, q, nt, preferred_element_type=F32))
    n_idx = lax.broadcasted_iota(jnp.int32, (nblk, s), 0)
    own = lax.broadcasted_iota(jnp.int32, (nblk, s), 1) // MOBA_BLOCK
    past = n_idx < own
    score = jnp.where(past, gate, -MASK_BIG)
    cnt = _rank_count(score, nblk)
    n_top = max(1, min(MOBA_TOPK, nblk - 1))
    sel = jnp.where(past, jnp.where(cnt < float(n_top), 1.0, 0.0), 0.0)
    member = jnp.where(n_idx == own, 1.0, sel)
    member = jnp.concatenate([member, jnp.zeros((LANES - nblk, s), F32)], axis=0)
    mem_ref[...] = member.T.astype(mem_ref.dtype)


def _moba_gate(proj, b, s):
    h = MOBA_HEADS
    nblk = s // MOBA_BLOCK
    return pl.pallas_call(
        functools.partial(_moba_gate_body, s=s, nblk=nblk),
        grid=(b, h),
        in_specs=[pl.BlockSpec((None, s, HEAD_DIM), lambda bi, hi: (bi, 0, hi)),
                  pl.BlockSpec((None, s, HEAD_DIM), lambda bi, hi: (bi, 0, h + hi))],
        out_specs=pl.BlockSpec((None, None, s, LANES), lambda bi, hi: (bi, hi, 0, 0)),
        out_shape=jax.ShapeDtypeStruct((b, h, s, LANES), BF16),
        compiler_params=_cparams(("parallel", "parallel")),
        name="moba_gate",
    )(proj, proj)


def _flash_body(qi_ref, ki_ref, bo_ref, fl_ref, *refs, nh, ratio, nm, n_near, has_far):
    if nm:
        (q_ref, k_ref, v_ref, bvec_ref, mem_ref, et_ref, o_ref,
         m_ref, l_ref, acc_ref, sh_ref, al_ref, bias_ref, s_ref, p_ref) = refs
    else:
        q_ref, k_ref, v_ref, bvec_ref, o_ref, m_ref, l_ref, acc_ref, sh_ref, al_ref, bias_ref, s_ref, p_ref = refs
    del qi_ref, ki_ref
    p = pl.program_id(2)
    flag = fl_ref[p]
    bo = bo_ref[p]
    t = q_ref.shape[0]
    rows = 64

    @pl.when(p == 0)
    def _():
        for h in range(nh):
            for o in range(n_near):
                vec = bvec_ref[h, o][0:1, :] * LOG2E
                for rc in range(t // rows):
                    x = pltpu.roll(jnp.broadcast_to(vec, (rows, 2 * t)), rc * rows, 1, stride=1, stride_axis=0)
                    bias_ref[h, o, rc * rows:(rc + 1) * rows, :] = x[:, :t]

    @pl.when((flag & 1) != 0)
    def _():
        m_ref[...] = jnp.full(m_ref.shape, M_INIT, F32)
        l_ref[...] = jnp.zeros(l_ref.shape, F32)
        acc_ref[...] = jnp.zeros(acc_ref.shape, F32)

    nt = (((1,), (1,)), ((), ()))
    reps = t // LANES

    def step(near):
        def pass1(h):
            hs = slice(h * HEAD_DIM, (h + 1) * HEAD_DIM)
            kv = h // ratio
            ks = slice(kv * HEAD_DIM, (kv + 1) * HEAD_DIM)
            q = q_ref[:, hs]
            k = k_ref[:, ks]
            if nm:
                mneg = mem_ref[h // (nh // nm)] - 1.0
                q = jnp.concatenate([q, mneg.astype(BF16)], axis=1)
                k = jnp.concatenate([k, et_ref[...]], axis=1)
            sc = lax.dot_general(q, k, nt, preferred_element_type=F32)
            m_prev = m_ref[h]
            if near:
                sc = sc + bias_ref[h, bo]
                m_new = jnp.maximum(m_prev, jnp.max(sc, axis=-1, keepdims=True))
                sh_ref[h] = m_new
            else:
                cfar = bvec_ref[h, n_near][0:1, 0:LANES] * LOG2E
                m_new = jnp.maximum(m_prev, jnp.max(sc, axis=-1, keepdims=True) + cfar)
                sh_ref[h] = m_new - cfar
            s_ref[h] = sc
            al_ref[h] = jnp.exp2(m_prev - m_new)
            m_ref[h] = m_new

        def pass2(h):
            ks = slice(h // ratio * HEAD_DIM, (h // ratio + 1) * HEAD_DIM)
            for rc in range(t // rows):
                rs = slice(rc * rows, (rc + 1) * rows)
                pm = jnp.exp2(s_ref[h, rs, :] - jnp.tile(sh_ref[h, rs, :], (1, reps)))
                l_ref[h, rs, :] = al_ref[h, rs, :] * l_ref[h, rs, :] + jnp.sum(pm, axis=-1, keepdims=True)
                p_ref[h, rs, :] = pm.astype(BF16)
            acc_ref[h] = al_ref[h] * acc_ref[h] + jnp.dot(p_ref[h], v_ref[:, ks], preferred_element_type=F32)

        lead = 2
        for h in range(min(lead, nh)):
            pass1(h)
        for h in range(nh):
            if h + lead < nh:
                pass1(h + lead)
            pass2(h)

    if has_far:
        pl.when(bo < n_near)(lambda: step(True))
        pl.when(bo >= n_near)(lambda: step(False))
    else:
        step(True)

    @pl.when((flag & 2) != 0)
    def _():
        for h in range(nh):
            l = l_ref[h]
            o_ref[:, h * HEAD_DIM:(h + 1) * HEAD_DIM] = (acc_ref[h] / jnp.where(l > 0, l, 1.0)).astype(o_ref.dtype)


def _flash(q_arr, q_off, k_arr, k_off, v_arr, v_off, bias, n_heads, ratio, nh, pairs, has_far, member=None,
           et=None, name="flash"):
    b, s, _ = q_arr.shape
    t = ATT_TILE
    nkv = nh // ratio
    ng = n_heads // nh
    qi = jnp.asarray([p[0] for p in pairs], jnp.int32)
    ki = jnp.asarray([p[1] for p in pairs], jnp.int32)
    bo = jnp.asarray([p[2] for p in pairs], jnp.int32)
    fl = jnp.asarray([p[3] for p in pairs], jnp.int32)
    nb = bias.shape[1]
    nm = 0
    in_specs = [
        pl.BlockSpec((None, t, nh * HEAD_DIM), lambda bi, gi, p, qi, ki, bo, fl: (bi, qi[p], q_off + gi)),
        pl.BlockSpec((None, t, nkv * HEAD_DIM), lambda bi, gi, p, qi, ki, bo, fl: (bi, ki[p], k_off + gi)),
        pl.BlockSpec((None, t, nkv * HEAD_DIM), lambda bi, gi, p, qi, ki, bo, fl: (bi, ki[p], v_off + gi)),
        pl.BlockSpec((nh, nb, 8, 2 * t), lambda bi, gi, p, qi, ki, bo, fl: (gi, 0, 0, 0)),
    ]
    args = [q_arr, k_arr, v_arr, bias]
    if member is not None:
        nm = member.shape[1] // ng
        in_specs += [
            pl.BlockSpec((None, nm, t, LANES), lambda bi, gi, p, qi, ki, bo, fl: (bi, gi, qi[p], 0)),
            pl.BlockSpec((t, LANES), lambda bi, gi, p, qi, ki, bo, fl: (ki[p], 0)),
        ]
        args += [member, et]
    n_near = nb - 1 if has_far else nb
    body = functools.partial(_flash_body, nh=nh, ratio=ratio, nm=nm, n_near=n_near, has_far=has_far)
    return pl.pallas_call(
        body,
        grid_spec=pltpu.PrefetchScalarGridSpec(
            num_scalar_prefetch=4,
            grid=(b, ng, len(pairs)),
            in_specs=in_specs,
            out_specs=pl.BlockSpec((None, t, nh * HEAD_DIM), lambda bi, gi, p, qi, ki, bo, fl: (bi, qi[p], gi)),
            scratch_shapes=[pltpu.VMEM((nh, t, LANES), F32)] * 5 + [pltpu.VMEM((nh, n_near, t, t), F32),
                                                                    pltpu.VMEM((nh, t, t), F32),
                                                                    pltpu.VMEM((nh, t, t), BF16)],
        ),
        out_shape=jax.ShapeDtypeStruct((b, s, n_heads * HEAD_DIM), BF16),
        compiler_params=_cparams(("parallel", "parallel", "arbitrary"), vmem=ATT_VMEM_LIMIT),
        name=name,
    )(qi, ki, bo, fl, *args)


def _rel_bucket(dist):
    n = jnp.maximum(jnp.asarray(dist, jnp.int32), 0)
    max_exact = REL_BUCKETS // 2
    nf = jnp.maximum(n, 1).astype(jnp.float32)
    large = max_exact + (jnp.log(nf / max_exact) / math.log(REL_MAX_DIST / max_exact)
                         * (REL_BUCKETS - max_exact)).astype(jnp.int32)
    return jnp.where(n < max_exact, n, jnp.minimum(large, REL_BUCKETS - 1))


def _bias_of_dist(tab, dist):
    hit = _rel_bucket(dist)[..., None, None] == jnp.arange(REL_BUCKETS)[:, None]
    return jnp.sum(jnp.where(hit, tab, 0.0), axis=-2)


def _n_near(t):
    return -(-(REL_MAX_DIST - 1 + t) // t)


def _bias_vecs(tab, t, n_off, window=None):
    k = np.arange(2 * t)[None, :]
    dist = np.arange(n_off)[:, None] * t + np.where(k < t, -k, 2 * t - k)
    ok = dist >= 0
    if window is not None:
        ok &= dist < window
    bias = jnp.where(jnp.asarray(ok)[..., None], _bias_of_dist(tab, dist), -MASK_BIG)
    bias = jnp.transpose(bias, (2, 0, 1)).astype(F32)
    return jnp.broadcast_to(bias[:, :, None, :], (bias.shape[0], n_off, 8, 2 * t))


def _causal_pairs(nq, n_near):
    pairs = []
    for qi in range(nq):
        for ki in range(qi + 1):
            pairs.append((qi, ki, min(qi - ki, n_near), (1 if ki == 0 else 0) | (2 if ki == qi else 0)))
    return pairs


def _window_pairs(nq, n_back):
    pairs = []
    for qi in range(nq):
        lo = max(0, qi - n_back)
        for ki in range(lo, qi + 1):
            pairs.append((qi, ki, qi - ki, (1 if ki == lo else 0) | (2 if ki == qi else 0)))
    return pairs


def _block_onehot(s, blk):
    return jnp.asarray(np.where(np.arange(s)[:, None] // blk == np.arange(LANES)[None, :], MASK_BIG, 0.0), BF16)


def _merge_body(oc_ref, os_ref, ow_ref, gl_ref, ob_ref, gma_ref, gmb_ref, wa_ref, wb_ref, o_ref, oa_ref):
    @pl.when(pl.program_id(1) == 0)
    def _():
        gates = jax.nn.sigmoid(gl_ref[...])
        for h in range(NSA_HEADS):
            hs = slice(h * HEAD_DIM, (h + 1) * HEAD_DIM)
            mix = (gates[:, 3 * h:3 * h + 1] * oc_ref[:, hs].astype(F32)
                   + gates[:, 3 * h + 1:3 * h + 2] * os_ref[:, hs].astype(F32)
                   + gates[:, 3 * h + 2:3 * h + 3] * ow_ref[:, hs].astype(F32))
            oa_ref[:, hs] = mix.astype(BF16)

    ya = jnp.dot(oa_ref[...], wa_ref[...].astype(BF16), preferred_element_type=F32)
    yb = jnp.dot(ob_ref[...], wb_ref[...].astype(BF16), preferred_element_type=F32)
    ga = jax.nn.sigmoid(gma_ref[...].astype(F32))
    gb = jax.nn.sigmoid(gmb_ref[...].astype(F32))
    o_ref[...] = (ga * ya + gb * yb).astype(o_ref.dtype)


def _merge(o_c, o_s, o_w, gate_logits, o_b, gm, w_up_a, w_up_b, tm=1024, tn=512):
    t, ka = o_c.shape
    kb = o_b.shape[1]
    d = w_up_a.shape[1]
    tm = min(tm, t)
    nj = d // tn
    row = lambda i, j: (i, 0)
    return pl.pallas_call(
        _merge_body,
        grid=(t // tm, nj),
        in_specs=[pl.BlockSpec((tm, ka), row), pl.BlockSpec((tm, ka), row), pl.BlockSpec((tm, ka), row),
                  pl.BlockSpec((tm, LANES), row), pl.BlockSpec((tm, kb), row),
                  pl.BlockSpec((tm, tn), lambda i, j: (i, j)),
                  pl.BlockSpec((tm, tn), lambda i, j: (i, j + nj)),
                  pl.BlockSpec((ka, tn), lambda i, j: (0, j)),
                  pl.BlockSpec((kb, tn), lambda i, j: (0, j))],
        out_specs=pl.BlockSpec((tm, tn), lambda i, j: (i, j)),
        out_shape=jax.ShapeDtypeStruct((t, d), BF16),
        scratch_shapes=[pltpu.VMEM((tm, ka), BF16)],
        compiler_params=_cparams(("parallel", "arbitrary")),
        name="merge_up",
    )(o_c, o_s, o_w, gate_logits, o_b, gm, gm, w_up_a, w_up_b)


def _route_body(x_ref, g_ref, w_ref, b_ref, h_ref, info_ref, cnt_ref, carry_ref, *, tm):
    @pl.when(pl.program_id(0) == 0)
    def _():
        carry_ref[...] = jnp.zeros(carry_ref.shape, F32)

    x = x_ref[...]
    ms = jnp.mean(x * x, axis=-1, keepdims=True)
    h = x * lax.rsqrt(ms + RMS_EPS) * g_ref[...]
    _store_packed(h_ref, h, tm)
    w = w_ref[...]
    h1 = h.astype(BF16)
    h2 = (h - h1.astype(F32)).astype(BF16)
    w1 = w.astype(BF16)
    w2 = (w - w1.astype(F32)).astype(BF16)
    logits = (jnp.dot(h1, w1, preferred_element_type=F32) + jnp.dot(h1, w2, preferred_element_type=F32)
              + jnp.dot(h2, w1, preferred_element_type=F32)) + b_ref[...]
    lane = lax.broadcasted_iota(jnp.int32, (tm, LANES), 1)
    lanef = lane.astype(F32)

    is_g = lane < N_GROUPS
    gl = jnp.where(is_g, logits, -MASK_BIG)
    ge = jnp.where(is_g, jnp.exp(gl - jnp.max(gl, axis=-1, keepdims=True)), 0.0)
    gp = ge / jnp.sum(ge, axis=-1, keepdims=True)
    g_val = jnp.max(gp, axis=-1, keepdims=True)
    g_idx = jnp.min(jnp.where(gp == g_val, lanef, float(LANES)), axis=-1, keepdims=True)

    lane_grp = ((lane - N_GROUPS) // EXPERTS_PER_GROUP).astype(F32)
    in_e = jnp.where(lane >= N_GROUPS, jnp.where(lane < N_GROUPS + N_EXPERTS, 1.0, 0.0), 0.0)
    is_e = jnp.where(lane_grp == g_idx, in_e, 0.0) > 0.5
    el = jnp.where(is_e, logits, -MASK_BIG)
    ee = jnp.where(is_e, jnp.exp(el - jnp.max(el, axis=-1, keepdims=True)), 0.0)
    ep = jnp.where(is_e, ee / jnp.sum(ee, axis=-1, keepdims=True), -1.0)
    v1 = jnp.max(ep, axis=-1, keepdims=True)
    l1 = jnp.min(jnp.where(ep == v1, lanef, float(LANES)), axis=-1, keepdims=True)
    ep2 = jnp.where(lanef == l1, -1.0, ep)
    v2 = jnp.max(ep2, axis=-1, keepdims=True)
    l2 = jnp.min(jnp.where(ep2 == v2, lanef, float(LANES)), axis=-1, keepdims=True)
    vs = v1 + v2
    wt1 = g_val * v1 / vs
    wt2 = g_val * v2 / vs
    e1 = l1 - float(N_GROUPS)
    e2 = l2 - float(N_GROUPS)

    oh = jnp.where(lanef == e1, 1.0, jnp.where(lanef == e2, 1.0, 0.0))
    r_i = lax.broadcasted_iota(jnp.int32, (tm, tm), 0)
    c_i = lax.broadcasted_iota(jnp.int32, (tm, tm), 1)
    tri = jnp.where(r_i > c_i, 1.0, 0.0).astype(BF16)
    base = jnp.dot(tri, oh.astype(BF16), preferred_element_type=F32) + carry_ref[...]
    r1 = jnp.sum(jnp.where(lanef == e1, base, 0.0), axis=-1, keepdims=True)
    r2 = jnp.sum(jnp.where(lanef == e2, base, 0.0), axis=-1, keepdims=True)
    carry_ref[...] = carry_ref[...] + jnp.sum(oh, axis=0, keepdims=True)
    cnt_ref[...] = jnp.broadcast_to(carry_ref[...], cnt_ref.shape)
    info = jnp.where(lane == 0, e1, jnp.where(lane == 1, e2, jnp.where(lane == 2, wt1, jnp.where(
        lane == 3, wt2, jnp.where(lane == 4, r1, jnp.where(lane == 5, r2, 0.0))))))
    info_ref[...] = info


def _route(x1, g, w_gr, b_gr, tm=512):
    t, d = x1.shape
    return pl.pallas_call(
        functools.partial(_route_body, tm=tm),
        grid=(t // tm,),
        in_specs=[pl.BlockSpec((tm, d), lambda i: (i, 0)),
                  pl.BlockSpec((1, d), lambda i: (0, 0)),
                  pl.BlockSpec((d, LANES), lambda i: (0, 0)),
                  pl.BlockSpec((1, LANES), lambda i: (0, 0))],
        out_specs=[pl.BlockSpec((tm * ROW_SUB, LANES), lambda i: (i, 0)),
                   pl.BlockSpec((tm, LANES), lambda i: (i, 0)),
                   pl.BlockSpec((8, LANES), lambda i: (0, 0))],
        out_shape=[jax.ShapeDtypeStruct((t * ROW_SUB, LANES), jnp.uint32),
                   jax.ShapeDtypeStruct((t, LANES), F32),
                   jax.ShapeDtypeStruct((8, LANES), F32)],
        scratch_shapes=[pltpu.VMEM((1, LANES), F32)],
        compiler_params=_cparams(("arbitrary",)),
        name="moe_route",
    )(x1, g.reshape(1, d), w_gr, b_gr)


ROW_SUB = 8
U32 = jnp.uint32


def _pack_pairs(lo, hi):
    lo_b = lax.bitcast_convert_type(lo.astype(BF16).astype(F32), U32)
    hi_b = lax.bitcast_convert_type(hi.astype(BF16).astype(F32), U32)
    return lax.shift_right_logical(lo_b, U32(16)) | (hi_b & U32(0xFFFF0000))


def _unpack_pairs(w):
    lo = lax.bitcast_convert_type(lax.shift_left(w, U32(16)), F32)
    hi = lax.bitcast_convert_type(w & U32(0xFFFF0000), F32)
    return lo, hi


def _store_packed(ref, y, n):
    half = y.shape[1] // 2
    for s in range(ROW_SUB):
        cs = slice(s * LANES, (s + 1) * LANES)
        ref[pl.ds(s, n, stride=ROW_SUB), :] = _pack_pairs(y[:, cs], y[:, half + s * LANES:half + (s + 1) * LANES])


def _load_packed(ref, n):
    los, his = [], []
    for s in range(ROW_SUB):
        lo, hi = _unpack_pairs(ref[pl.ds(s, n, stride=ROW_SUB), :])
        los.append(lo)
        his.append(hi)
    return jnp.concatenate(los + his, axis=1)


def _row_copy(src_ref, src_row, dst_ref, dst_row, sem):
    return pltpu.make_async_copy(src_ref.at[pl.ds(pl.multiple_of(src_row * ROW_SUB, ROW_SUB), ROW_SUB)],
                                 dst_ref.at[pl.ds(pl.multiple_of(dst_row * ROW_SUB, ROW_SUB), ROW_SUB)], sem)


def _dispatch_body(dest_ref, h_ref, xs_in_ref, xs_ref, sem, *, tm):
    del xs_in_ref
    base = pl.program_id(0) * tm

    def issue(r, c):
        for k in range(EXPERT_TOPK):
            _row_copy(h_ref, r, xs_ref, dest_ref[EXPERT_TOPK * (base + r) + k], sem).start(priority=k % 2)
        return c

    lax.fori_loop(0, tm, issue, 0, unroll=4)

    def drain(r, c):
        for k in range(EXPERT_TOPK):
            _row_copy(h_ref, r, xs_ref, dest_ref[EXPERT_TOPK * (base + r) + k], sem).wait()
        return c

    lax.fori_loop(0, tm, drain, 0, unroll=4)


def _dispatch(dest, hp, n_rows, tm=512):
    t = hp.shape[0] // ROW_SUB
    xs0 = jnp.zeros((n_rows * ROW_SUB, LANES), U32)
    return pl.pallas_call(
        functools.partial(_dispatch_body, tm=tm),
        grid_spec=pltpu.PrefetchScalarGridSpec(
            num_scalar_prefetch=1,
            grid=(t // tm,),
            in_specs=[pl.BlockSpec((tm * ROW_SUB, LANES), lambda i, dest: (i, 0)),
                      pl.BlockSpec(memory_space=pl.ANY)],
            out_specs=pl.BlockSpec(memory_space=pl.ANY),
            scratch_shapes=[pltpu.SemaphoreType.DMA(())],
        ),
        out_shape=jax.ShapeDtypeStruct((n_rows * ROW_SUB, LANES), U32),
        input_output_aliases={2: 0},
        compiler_params=_cparams(("arbitrary",)),
        name="moe_dispatch",
    )(dest, hp, xs0)


def _expert_body(nu_ref, sq_ref, es_ref, ns_ref, x_ref, wg_hbm, wu_hbm, wd_hbm, y_ref,
                 wg_b, wu_b, wd_b, wg_s, wu_s, wd_s, sem):
    i = pl.program_id(0)
    nu = nu_ref[0]
    ns = ns_ref[0]

    def weight_copies(seq, slot):
        e = es_ref[seq]
        return (pltpu.make_async_copy(wg_hbm.at[e], wg_b.at[slot], sem.at[slot]),
                pltpu.make_async_copy(wu_hbm.at[e], wu_b.at[slot], sem.at[slot]),
                pltpu.make_async_copy(wd_hbm.at[e], wd_b.at[slot], sem.at[slot]))

    def start_weights(seq, slot):
        for c in weight_copies(seq, slot):
            c.start()

    @pl.when(i == 0)
    def _():
        start_weights(0, 0)

        @pl.when(ns > 1)
        def _():
            start_weights(1, 1)

    s = sq_ref[i]
    first = (i == 0) | (s != sq_ref[jnp.maximum(i - 1, 0)])

    @pl.when((i < nu) & first)
    def _():
        slot = lax.rem(s, 2)
        for c in weight_copies(s, slot):
            c.wait()
        wg_s[...] = wg_b[slot].astype(BF16)
        wu_s[...] = wu_b[slot].astype(BF16)
        wd_s[...] = wd_b[slot].astype(BF16)

        @pl.when(s + 2 < ns)
        def _():
            start_weights(s + 2, slot)

    @pl.when(i < nu)
    def _():
        x = _load_packed(x_ref, MOE_ROWS).astype(BF16)
        g = jnp.dot(x, wg_s[...], preferred_element_type=F32)
        u = jnp.dot(x, wu_s[...], preferred_element_type=F32)
        mid = (jax.nn.silu(g) * u).astype(BF16)
        _store_packed(y_ref, jnp.dot(mid, wd_s[...], preferred_element_type=F32), MOE_ROWS)

    @pl.when(i >= nu_ref[0])
    def _():
        y_ref[...] = jnp.zeros(y_ref.shape, y_ref.dtype)


def _experts(n_used, seq_of_block, expert_of_seq, n_seq, xs, w_gate, w_up, w_down):
    n_blocks = xs.shape[0] // (MOE_ROWS * ROW_SUB)
    _, d, ff = w_gate.shape
    assert d == 2 * ROW_SUB * LANES, "a packed row must be exactly one (8,128) tile"
    blk = lambda i, nu, sq, es, ns: (jnp.minimum(i, nu[0] - 1), 0)
    hbm = pl.BlockSpec(memory_space=pl.ANY)
    return pl.pallas_call(
        _expert_body,
        grid_spec=pltpu.PrefetchScalarGridSpec(
            num_scalar_prefetch=4,
            grid=(n_blocks,),
            in_specs=[pl.BlockSpec((MOE_ROWS * ROW_SUB, LANES), blk), hbm, hbm, hbm],
            out_specs=pl.BlockSpec((MOE_ROWS * ROW_SUB, LANES), lambda i, nu, sq, es, ns: (i, 0)),
            scratch_shapes=[pltpu.VMEM((2, d, ff), F32), pltpu.VMEM((2, d, ff), F32), pltpu.VMEM((2, ff, d), F32),
                            pltpu.VMEM((d, ff), BF16), pltpu.VMEM((d, ff), BF16), pltpu.VMEM((ff, d), BF16),
                            pltpu.SemaphoreType.DMA((2,))],
        ),
        out_shape=jax.ShapeDtypeStruct(xs.shape, U32),
        compiler_params=_cparams(("arbitrary",)),
        name="moe_experts",
    )(n_used, seq_of_block, expert_of_seq, n_seq, xs, w_gate, w_up, w_down)


def _combine_body(dest_ref, x_ref, info_ref, g_ref, ys_ref, o_ref, buf0, buf1, sem, *, tm):
    i = pl.program_id(0)
    slot = lax.rem(i, 2)
    bufs = (buf0, buf1)

    def gather(tile, sl, wait):
        def body(r, c):
            for k in range(EXPERT_TOPK):
                cp = _row_copy(ys_ref, dest_ref[EXPERT_TOPK * (tile * tm + r) + k], bufs[k].at[sl], r, sem.at[sl])
                if wait:
                    cp.wait()
                else:
                    cp.start(priority=k % 2)
            return c
        lax.fori_loop(0, tm, body, 0, unroll=4)

    @pl.when(i == 0)
    def _():
        gather(0, 0, False)

    @pl.when(i + 1 < pl.num_programs(0))
    def _():
        gather(i + 1, 1 - slot, False)

    gather(i, slot, True)
    info = info_ref[...]
    y = x_ref[...] + (info[:, 2:3] * _load_packed(buf0.at[slot], tm) + info[:, 3:4] * _load_packed(buf1.at[slot], tm))
    ms = jnp.mean(y * y, axis=-1, keepdims=True)
    o_ref[...] = y * lax.rsqrt(ms + RMS_EPS) * g_ref[...]


def _combine(dest, x1, info, g, ys, tm=256):
    t, d = x1.shape
    return pl.pallas_call(
        functools.partial(_combine_body, tm=tm),
        grid_spec=pltpu.PrefetchScalarGridSpec(
            num_scalar_prefetch=1,
            grid=(t // tm,),
            in_specs=[pl.BlockSpec((tm, d), lambda i, dest: (i, 0)),
                      pl.BlockSpec((tm, LANES), lambda i, dest: (i, 0)),
                      pl.BlockSpec((1, d), lambda i, dest: (0, 0)),
                      pl.BlockSpec(memory_space=pl.ANY)],
            out_specs=pl.BlockSpec((tm, d), lambda i, dest: (i, 0)),
            scratch_shapes=[pltpu.VMEM((2, tm * ROW_SUB, LANES), U32), pltpu.VMEM((2, tm * ROW_SUB, LANES), U32),
                            pltpu.SemaphoreType.DMA((2,))],
        ),
        out_shape=jax.ShapeDtypeStruct((t, d), F32),
        compiler_params=_cparams(("arbitrary",)),
        name="moe_combine",
    )(dest, x1, info, g.reshape(1, d), ys)


def _nsa(proj, gate_cols, pe_k, w1_k, w2_k, pe_v, w1_v, w2_v, tab, b, s):
    del gate_cols
    g, dh = NSA_KV_HEADS, HEAD_DIM
    qw = NSA_HEADS * dh
    nc = s // CMP_STRIDE

    def blocks16(col0):
        a = proj[:, :, col0:col0 + g * dh].reshape(b, nc, CMP_STRIDE, g, dh)
        return a.transpose(0, 3, 1, 2, 4).reshape(b * g, nc, CMP_STRIDE * dh)

    kc = _compress(blocks16(qw), pe_k, w1_k, w2_k).reshape(b, g, nc, dh)
    vc = _compress(blocks16(qw + g * dh), pe_v, w1_v, w2_v).reshape(b, g, nc, dh)

    c_start = np.arange(nc)[None, :] * CMP_STRIDE
    n_sel = s // SEL_BLOCK
    sb = np.arange(n_sel)[:, None] * SEL_BLOCK
    overlap = jnp.asarray((c_start < sb + SEL_BLOCK) & (c_start + CMP_BLOCK > sb), BF16)
    o_c, member = _nsa_cmp(proj, kc, vc, tab, overlap, b, s)

    t = ATT_TILE
    nq = s // t
    nn = _n_near(t)
    nh = ATT_HEADS_PER_STEP
    kblk = qw // (nh // NSA_GROUP * dh)
    per = g // (nh // NSA_GROUP)
    bias_d = _bias_vecs(tab, t, nn + 1)
    o_s = _flash(proj, 0, proj, kblk + 2 * per, proj, kblk + 3 * per, bias_d, NSA_HEADS, NSA_GROUP, nh,
                 _causal_pairs(nq, nn), True, member=member, et=_block_onehot(s, SEL_BLOCK), name="nsa_selected")
    n_back = -(-WINDOW // t)
    bias_w = _bias_vecs(tab, t, n_back + 1, window=WINDOW)
    o_w = _flash(proj, 0, proj, kblk + 4 * per, proj, kblk + 5 * per, bias_w, NSA_HEADS, NSA_GROUP, nh,
                 _window_pairs(nq, n_back), False, name="nsa_window")
    return o_c, o_s, o_w


def _moba(proj, tab, b, s):
    member = _moba_gate(proj, b, s)
    t = ATT_TILE
    nn = _n_near(t)
    nh = ATT_HEADS_PER_STEP
    ng = MOBA_HEADS // nh
    bias_d = _bias_vecs(tab, t, nn + 1)
    return _flash(proj, 0, proj, ng, proj, 2 * ng, bias_d, MOBA_HEADS, 1, nh, _causal_pairs(s // t, nn),
                  True, member=member, et=_block_onehot(s, MOBA_BLOCK), name="moba_attn")


def _moe(x1, g_ffn, w_group, b_group, w_router, b_router, w_gate, w_up, w_down, g_final):
    t, d = x1.shape
    ng, _, epg = w_router.shape
    w_gr = jnp.concatenate([w_group, jnp.transpose(w_router, (1, 0, 2)).reshape(d, ng * epg),
                            jnp.zeros((d, LANES - ng - ng * epg), F32)], axis=1)
    b_gr = jnp.concatenate([b_group, b_router.reshape(-1), jnp.zeros((LANES - ng - ng * epg,), F32)]).reshape(1, LANES)
    h, info, cnt = _route(x1, g_ffn, w_gr, b_gr)
    n_e = ng * epg
    n_assign = t * EXPERT_TOPK
    n_blocks = -(-(n_assign + n_e * (MOE_ROWS - 1)) // MOE_ROWS)
    counts = cnt[0, :n_e].astype(jnp.int32)
    padded = (counts + MOE_ROWS - 1) // MOE_ROWS * MOE_ROWS
    pad_end = jnp.cumsum(padded)
    pad_start = pad_end - padded
    expert = info[:, 0:EXPERT_TOPK].astype(jnp.int32)
    rank = info[:, 4:4 + EXPERT_TOPK].astype(jnp.int32)
    e_ids = jnp.arange(n_e, dtype=jnp.int32)
    dest = (jnp.sum(jnp.where(expert[..., None] == e_ids, pad_start, 0), axis=-1) + rank).reshape(-1)
    block_row0 = jnp.arange(n_blocks, dtype=jnp.int32) * MOE_ROWS
    block_e = jnp.minimum(jnp.sum((pad_end[None, :] <= block_row0[:, None]).astype(jnp.int32), axis=1), n_e - 1)
    n_used = (pad_end[-1:] // MOE_ROWS).astype(jnp.int32)
    owns = counts > 0
    seq_of_expert = jnp.cumsum(owns.astype(jnp.int32)) - 1
    n_seq = jnp.sum(owns.astype(jnp.int32)).reshape(1)
    expert_of_seq = jnp.sum(jnp.where(owns[None, :] & (seq_of_expert[None, :] == e_ids[:, None]), e_ids[None, :], 0),
                            axis=1)
    seq_of_block = jnp.sum(jnp.where(block_e[:, None] == e_ids[None, :], seq_of_expert[None, :], 0), axis=1)
    xs = _dispatch(dest, h, n_blocks * MOE_ROWS)
    ys = _experts(n_used, seq_of_block, expert_of_seq, n_seq, xs, w_gate, w_up, w_down)
    return _combine(dest, x1, info, g_final, ys)


def kernel(x, rel_bias, norm_mix, w_in, cmp_pe_k, cmp_w1_k, cmp_w2_k, cmp_pe_v, cmp_w1_v, cmp_w2_v, w_up_nsa,
           w_up_moba, w_out, norm_ffn, w_group, b_group, w_router, b_router, w_exp_gate, w_exp_up, w_exp_down,
           final_norm):
    b, s, d = x.shape
    t = b * s
    depth = w_in.shape[0]
    tab_a = rel_bias[:, :NSA_HEADS]
    tab_b = rel_bias[:, NSA_HEADS:]
    a_cols = NSA_HEADS * HEAD_DIM + 6 * NSA_KV_HEADS * HEAD_DIM
    gate_cols = 3 * NSA_HEADS
    b_cols = 3 * MOBA_HEADS * HEAD_DIM
    xt = x.reshape(t, d)
    out = None
    for l in range(depth):
        h = _rmsnorm(xt, norm_mix[l], BF16)
        wt = jnp.swapaxes(w_in[l], 0, 1)
        b_col0 = a_cols + gate_cols
        proj_a = _matmul(h, wt, 0, a_cols, BF16, scaled_cols=NSA_HEADS * HEAD_DIM, col_scale=Q_SCALE,
                         w_transposed=True, name="in_proj_a").reshape(b, s, a_cols)
        gate_a = _matmul(h, wt, a_cols, LANES, F32, w_transposed=True, tn=LANES, name="in_proj_gate")
        proj_b = _matmul(h, wt, b_col0, b_cols, BF16, scaled_cols=MOBA_HEADS * HEAD_DIM, col_scale=Q_SCALE,
                         w_transposed=True, name="in_proj_b").reshape(b, s, b_cols)
        gm = _matmul(h, wt, b_col0 + b_cols, 2 * d, BF16, w_transposed=True, name="in_proj_gm")
        o_c, o_s, o_w = _nsa(proj_a, gate_cols, cmp_pe_k[l], cmp_w1_k[l], cmp_w2_k[l],
                             cmp_pe_v[l], cmp_w1_v[l], cmp_w2_v[l], tab_a, b, s)
        o_b = _moba(proj_b, tab_b, b, s)
        merged = _merge(o_c.reshape(t, -1), o_s.reshape(t, -1), o_w.reshape(t, -1), gate_a,
                        o_b.reshape(t, -1), gm, w_up_nsa[l], w_up_moba[l])
        x1 = _matmul(merged, w_out[l], 0, d, F32, res=xt, name="out_proj")
        assert l == depth - 1, "only the last layer's MoE is fused with the final norm"
        out = _moe(x1, norm_ffn[l], w_group[l], b_group[l], w_router[l], b_router[l],
                   w_exp_gate[l], w_exp_up[l], w_exp_down[l], final_norm)
    return out.reshape(b, s, d)
```

```python
import functools
import math

import numpy as np
import jax
import jax.numpy as jnp
from jax import lax
from jax.experimental import pallas as pl
from jax.experimental.pallas import tpu as pltpu

F32 = jnp.float32
BF16 = jnp.bfloat16

HEAD_DIM = 128
NSA_HEADS = 8
NSA_KV_HEADS = 2
NSA_GROUP = NSA_HEADS // NSA_KV_HEADS
CMP_BLOCK = 32
CMP_STRIDE = 16
SEL_BLOCK = 64
SEL_TOPN = 16
WINDOW = 512
FORCED_SCORE = 1e4
MOBA_HEADS = 8
MOBA_BLOCK = 256
MOBA_TOPK = 3
REL_BUCKETS = 32
REL_MAX_DIST = 128
N_GROUPS = 8
EXPERTS_PER_GROUP = 8
N_EXPERTS = N_GROUPS * EXPERTS_PER_GROUP
EXPERT_TOPK = 2
RMS_EPS = 1e-6

LANES = 128
ATT_TILE = 512
ATT_HEADS_PER_STEP = 8
ATT_VMEM_LIMIT = 56 * 1024 * 1024
MOE_ROWS = 256
MASK_BIG = 1e30
M_INIT = -3e38
LOG2E = math.log2(math.e)
Q_SCALE = HEAD_DIM ** -0.5 * LOG2E
VMEM_LIMIT = 48 * 1024 * 1024


def _cparams(sem, vmem=VMEM_LIMIT, flags=None):
    return pltpu.CompilerParams(dimension_semantics=sem, vmem_limit_bytes=vmem, flags=flags)


def _rmsnorm_body(x_ref, g_ref, o_ref):
    x = x_ref[...]
    ms = jnp.mean(x * x, axis=-1, keepdims=True)
    o_ref[...] = (x * lax.rsqrt(ms + RMS_EPS) * g_ref[...]).astype(o_ref.dtype)


def _rmsnorm(x, g, out_dtype, tm=512):
    t, d = x.shape
    return pl.pallas_call(
        _rmsnorm_body,
        grid=(t // tm,),
        in_specs=[pl.BlockSpec((tm, d), lambda i: (i, 0)),
                  pl.BlockSpec((1, d), lambda i: (0, 0))],
        out_specs=pl.BlockSpec((tm, d), lambda i: (i, 0)),
        out_shape=jax.ShapeDtypeStruct((t, d), out_dtype),
        compiler_params=_cparams(("parallel",)),
        name="rmsnorm",
    )(x, g.reshape(1, d))


def _mm_body(*refs, has_res, n_scaled, col_scale, w_transposed):
    if has_res:
        a_ref, w_ref, r_ref, o_ref = refs
    else:
        a_ref, w_ref, o_ref = refs
    w = w_ref[...].astype(BF16)
    if w_transposed:
        acc = lax.dot_general(a_ref[...], w, (((1,), (1,)), ((), ())), preferred_element_type=F32)
    else:
        acc = jnp.dot(a_ref[...], w, preferred_element_type=F32)
    if n_scaled:
        acc = acc * jnp.where(pl.program_id(1) < n_scaled, col_scale, 1.0)
    if has_res:
        acc = acc + r_ref[...]
    o_ref[...] = acc.astype(o_ref.dtype)


def _matmul(a, w, col0, ncols, out_dtype, res=None, scaled_cols=0, col_scale=1.0, w_transposed=False,
            tm=2048, tn=512, name="matmul"):
    t, k = a.shape
    tn = min(tn, ncols)
    tm = min(tm, t)
    assert ncols % tn == 0 and t % tm == 0 and scaled_cols % tn == 0
    if w_transposed:
        assert col0 % 8 == 0
        w_spec = pl.BlockSpec((pl.Element(tn), pl.Element(k)),
                              lambda i, j: (pl.multiple_of(col0 + j * tn, 8), 0))
    else:
        assert col0 % tn == 0
        off = col0 // tn
        w_spec = pl.BlockSpec((k, tn), lambda i, j: (0, j + off))
    in_specs = [pl.BlockSpec((tm, k), lambda i, j: (i, 0)), w_spec]
    args = [a, w]
    if res is not None:
        in_specs.append(pl.BlockSpec((tm, tn), lambda i, j: (i, j)))
        args.append(res)
    return pl.pallas_call(
        functools.partial(_mm_body, has_res=res is not None, n_scaled=scaled_cols // tn, col_scale=col_scale,
                          w_transposed=w_transposed),
        grid=(t // tm, ncols // tn),
        in_specs=in_specs,
        out_specs=pl.BlockSpec((tm, tn), lambda i, j: (i, j)),
        out_shape=jax.ShapeDtypeStruct((t, ncols), out_dtype),
        compiler_params=_cparams(("parallel", "parallel")),
        name=name,
    )(*args)


def _compress_body(u_ref, pe_ref, w1_ref, w2_ref, o_ref, *, nc):
    u = u_ref[...]
    w1 = w1_ref[...].astype(BF16)
    half = u.shape[1]
    a = jnp.dot(u, w1[:half], preferred_element_type=F32)
    b = jnp.dot(u, w1[half:], preferred_element_type=F32)
    peb = jnp.dot(pe_ref[...].astype(BF16), w1, preferred_element_type=F32)[0:1]
    pre = a + pltpu.roll(b, nc - 1, 0) + peb
    hid = jax.nn.gelu(pre)
    o_ref[...] = jnp.dot(hid.astype(BF16), w2_ref[...].astype(BF16),
                         preferred_element_type=F32).astype(o_ref.dtype)


def _compress(u, pe, w1, w2):
    bg, nc, kk = u.shape
    hid = w1.shape[1]
    dh = w2.shape[1]
    pe8 = jnp.broadcast_to(pe.reshape(1, -1), (16, pe.size))
    return pl.pallas_call(
        functools.partial(_compress_body, nc=nc),
        grid=(bg,),
        in_specs=[pl.BlockSpec((None, nc, kk), lambda i: (i, 0, 0)),
                  pl.BlockSpec((16, 2 * kk), lambda i: (0, 0)),
                  pl.BlockSpec((2 * kk, hid), lambda i: (0, 0)),
                  pl.BlockSpec((hid, dh), lambda i: (0, 0))],
        out_specs=pl.BlockSpec((None, nc, dh), lambda i: (i, 0, 0)),
        out_shape=jax.ShapeDtypeStruct((bg, nc, dh), BF16),
        compiler_params=_cparams(("parallel",)),
        name="nsa_compress",
    )(u, pe8, w1, w2)


def _split3(x):
    p1 = x.astype(BF16)
    r = x - p1.astype(F32)
    p2 = r.astype(BF16)
    p3 = (r - p2.astype(F32)).astype(BF16)
    return p1, p2, p3


def _rank_count(score, n_rows):
    groups = []
    for g0 in range(0, n_rows, 8):
        sg = score[g0:min(g0 + 8, n_rows), :]
        n_iota = g0 + lax.broadcasted_iota(jnp.int32, sg.shape, 0)
        cnt = jnp.zeros(sg.shape, F32)
        for m in range(n_rows):
            row = score[m:m + 1, :]
            if m < g0:
                beats = row >= sg
            elif m >= g0 + 8:
                beats = row > sg
            else:
                tie = jnp.where(n_iota > m, 1.0, 0.0)
                beats = jnp.where(row > sg, 1.0, jnp.where(row == sg, tie, 0.0)) > 0.5
            cnt = cnt + jnp.where(beats, 1.0, 0.0)
        groups.append(cnt)
    return jnp.concatenate(groups, axis=0) if len(groups) > 1 else groups[0]


def _nsa_cmp_body(q_ref, kc_ref, vc_ref, bias_ref, ov_ref, oc_ref, mem_ref, *, tq, nc, n_sel):
    t0 = pl.program_id(2) * tq
    kc = kc_ref[...]
    vc = vc_ref[...]
    t_idx = t0 + lax.broadcasted_iota(jnp.int32, (tq, nc), 0)
    c_idx = lax.broadcasted_iota(jnp.int32, (tq, nc), 1)
    dist = t_idx - (c_idx * CMP_STRIDE + (CMP_BLOCK - 1))
    n_k = REL_MAX_DIST // CMP_STRIDE
    kidx = jnp.where(dist < 0, n_k + 1, jnp.minimum(lax.shift_right_logical(dist, 4), n_k))
    assert CMP_STRIDE == 16
    psum = jnp.zeros((tq, nc), F32)
    for j in range(NSA_GROUP):
        hs = slice(j * HEAD_DIM, (j + 1) * HEAD_DIM)
        gt = bias_ref[j] * LOG2E
        bias = jnp.concatenate([jnp.take_along_axis(gt, kidx[:, c0:c0 + LANES], axis=1)
                                for c0 in range(0, nc, LANES)], axis=1)
        s = lax.dot_general(q_ref[:, hs], kc, (((1,), (1,)), ((), ())), preferred_element_type=F32) + bias
        m = jnp.max(s, axis=-1, keepdims=True)
        m = jnp.where(m > -0.5 * MASK_BIG, m, 0.0)
        p = jnp.exp2(s - m)
        d = jnp.sum(p, axis=-1, keepdims=True)
        p = p / jnp.where(d > 0, d, 1.0)
        oc_ref[:, hs] = jnp.dot(p.astype(BF16), vc, preferred_element_type=F32).astype(oc_ref.dtype)
        psum = psum + p
    ov = ov_ref[...]
    nt = (((1,), (1,)), ((), ()))
    p1, p2, p3 = _split3(psum)
    psel = (lax.dot_general(ov, p1, nt, preferred_element_type=F32)
            + lax.dot_general(ov, p2, nt, preferred_element_type=F32)
            + lax.dot_general(ov, p3, nt, preferred_element_type=F32))
    n_idx = lax.broadcasted_iota(jnp.int32, (n_sel, tq), 0)
    tt = t0 + lax.broadcasted_iota(jnp.int32, (n_sel, tq), 1)
    cur = tt // SEL_BLOCK
    forced = jnp.where(n_idx == 0, 1.0, jnp.where(n_idx == cur, 1.0, jnp.where(n_idx == cur - 1, 1.0, 0.0)))
    score = jnp.where(forced > 0.5, FORCED_SCORE, jnp.where(n_idx * SEL_BLOCK <= tt, psel, -1.0))
    cnt = _rank_count(score, n_sel)
    member = jnp.where(cnt < float(min(SEL_TOPN, n_sel)), 1.0, 0.0)
    if n_sel < LANES:
        member = jnp.concatenate([member, jnp.zeros((LANES - n_sel, tq), F32)], axis=0)
    mem_ref[...] = member.T.astype(mem_ref.dtype)


def _nsa_cmp(proj, kc, vc, tab, overlap, b, s, tq=256):
    g = NSA_KV_HEADS
    nc = kc.shape[2]
    n_sel = s // SEL_BLOCK
    gw = NSA_GROUP * HEAD_DIM
    assert tq % CMP_STRIDE == 0 and REL_MAX_DIST % CMP_STRIDE == 0
    n_k = REL_MAX_DIST // CMP_STRIDE
    rho = (np.arange(tq)[:, None] - (CMP_BLOCK - 1)) % CMP_STRIDE
    dd = np.concatenate([rho + CMP_STRIDE * np.arange(n_k)[None, :], np.full((tq, 1), REL_MAX_DIST)], axis=1)
    gtab = jnp.transpose(_bias_of_dist(tab, dd), (2, 0, 1)).astype(F32)
    bias_c = jnp.concatenate([gtab, jnp.full(gtab.shape[:2] + (1,), -MASK_BIG, F32),
                              jnp.zeros(gtab.shape[:2] + (LANES - n_k - 2,), F32)], axis=2)
    body = functools.partial(_nsa_cmp_body, tq=tq, nc=nc, n_sel=n_sel)
    return pl.pallas_call(
        body,
        grid=(b, g, s // tq),
        in_specs=[pl.BlockSpec((None, tq, gw), lambda bi, gi, i: (bi, i, gi)),
                  pl.BlockSpec((None, None, nc, HEAD_DIM), lambda bi, gi, i: (bi, gi, 0, 0)),
                  pl.BlockSpec((None, None, nc, HEAD_DIM), lambda bi, gi, i: (bi, gi, 0, 0)),
                  pl.BlockSpec((NSA_GROUP, tq, LANES), lambda bi, gi, i: (gi, 0, 0)),
                  pl.BlockSpec((n_sel, nc), lambda bi, gi, i: (0, 0))],
        out_specs=[pl.BlockSpec((None, tq, gw), lambda bi, gi, i: (bi, i, gi)),
                   pl.BlockSpec((None, None, tq, LANES), lambda bi, gi, i: (bi, gi, i, 0))],
        out_shape=[jax.ShapeDtypeStruct((b, s, NSA_HEADS * HEAD_DIM), BF16),
                   jax.ShapeDtypeStruct((b, g, s, LANES), BF16)],
        compiler_params=_cparams(("parallel", "parallel", "parallel")),
        name="nsa_cmp_select",
    )(proj, kc, vc, bias_c, overlap)


def _moba_gate_body(q_ref, k_ref, mem_ref, *, s, nblk):
    k = k_ref[...].astype(F32)
    kmean = jnp.mean(k.reshape(nblk, MOBA_BLOCK, HEAD_DIM), axis=1)
    k1 = kmean.astype(BF16)
    k2 = (kmean - k1.astype(F32)).astype(BF16)
    q = q_ref[...]
    nt = (((1,), (1,)), ((), ()))
    gate = (lax.dot_general(k1, q, nt, preferred_element_type=F32)
            + lax.dot_general(k2, q, nt, preferred_element_type=F32))
    n_idx = lax.broadcasted_iota(jnp.int32, (nblk, s), 0)
    own = lax.broadcasted_iota(jnp.int32, (nblk, s), 1) // MOBA_BLOCK
    past = n_idx < own
    score = jnp.where(past, gate, -MASK_BIG)
    cnt = _rank_count(score, nblk)
    n_top = max(1, min(MOBA_TOPK, nblk - 1))
    sel = jnp.where(past, jnp.where(cnt < float(n_top), 1.0, 0.0), 0.0)
    member = jnp.where(n_idx == own, 1.0, sel)
    member = jnp.concatenate([member, jnp.zeros((LANES - nblk, s), F32)], axis=0)
    mem_ref[...] = member.T.astype(mem_ref.dtype)


def _moba_gate(proj, b, s):
    h = MOBA_HEADS
    nblk = s // MOBA_BLOCK
    return pl.pallas_call(
        functools.partial(_moba_gate_body, s=s, nblk=nblk),
        grid=(b, h),
        in_specs=[pl.BlockSpec((None, s, HEAD_DIM), lambda bi, hi: (bi, 0, hi)),
                  pl.BlockSpec((None, s, HEAD_DIM), lambda bi, hi: (bi, 0, h + hi))],
        out_specs=pl.BlockSpec((None, None, s, LANES), lambda bi, hi: (bi, hi, 0, 0)),
        out_shape=jax.ShapeDtypeStruct((b, h, s, LANES), BF16),
        compiler_params=_cparams(("parallel", "parallel")),
        name="moba_gate",
    )(proj, proj)


def _flash_body(qi_ref, ki_ref, bo_ref, fl_ref, *refs, nh, ratio, nm, n_near, has_far):
    if nm:
        (q_ref, k_ref, v_ref, bvec_ref, mem_ref, et_ref, o_ref,
         m_ref, l_ref, acc_ref, sh_ref, al_ref, bias_ref, s_ref, p_ref) = refs
    else:
        q_ref, k_ref, v_ref, bvec_ref, o_ref, m_ref, l_ref, acc_ref, sh_ref, al_ref, bias_ref, s_ref, p_ref = refs
    del qi_ref, ki_ref
    p = pl.program_id(2)
    flag = fl_ref[p]
    bo = bo_ref[p]
    t = q_ref.shape[0]
    rows = 64

    @pl.when(p == 0)
    def _():
        for h in range(nh):
            for o in range(n_near):
                vec = bvec_ref[h, o][0:1, :] * LOG2E
                for rc in range(t // rows):
                    x = pltpu.roll(jnp.broadcast_to(vec, (rows, 2 * t)), rc * rows, 1, stride=1, stride_axis=0)
                    bias_ref[h, o, rc * rows:(rc + 1) * rows, :] = x[:, :t]

    @pl.when((flag & 1) != 0)
    def _():
        m_ref[...] = jnp.full(m_ref.shape, M_INIT, F32)
        l_ref[...] = jnp.zeros(l_ref.shape, F32)
        acc_ref[...] = jnp.zeros(acc_ref.shape, F32)

    nt = (((1,), (1,)), ((), ()))
    reps = t // LANES

    def step(near):
        def pass1(h):
            hs = slice(h * HEAD_DIM, (h + 1) * HEAD_DIM)
            kv = h // ratio
            ks = slice(kv * HEAD_DIM, (kv + 1) * HEAD_DIM)
            q = q_ref[:, hs]
            k = k_ref[:, ks]
            if nm:
                mneg = mem_ref[h // (nh // nm)] - 1.0
                q = jnp.concatenate([q, mneg.astype(BF16)], axis=1)
                k = jnp.concatenate([k, et_ref[...]], axis=1)
            sc = lax.dot_general(q, k, nt, preferred_element_type=F32)
            m_prev = m_ref[h]
            if near:
                sc = sc + bias_ref[h, bo]
                m_new = jnp.maximum(m_prev, jnp.max(sc, axis=-1, keepdims=True))
                sh_ref[h] = m_new
            else:
                cfar = bvec_ref[h, n_near][0:1, 0:LANES] * LOG2E
                m_new = jnp.maximum(m_prev, jnp.max(sc, axis=-1, keepdims=True) + cfar)
                sh_ref[h] = m_new - cfar
            s_ref[h] = sc
            al_ref[h] = jnp.exp2(m_prev - m_new)
            m_ref[h] = m_new

        def pass2(h):
            ks = slice(h // ratio * HEAD_DIM, (h // ratio + 1) * HEAD_DIM)
            for rc in range(t // rows):
                rs = slice(rc * rows, (rc + 1) * rows)
                pm = jnp.exp2(s_ref[h, rs, :] - jnp.tile(sh_ref[h, rs, :], (1, reps)))
                l_ref[h, rs, :] = al_ref[h, rs, :] * l_ref[h, rs, :] + jnp.sum(pm, axis=-1, keepdims=True)
                p_ref[h, rs, :] = pm.astype(BF16)
            acc_ref[h] = al_ref[h] * acc_ref[h] + jnp.dot(p_ref[h], v_ref[:, ks], preferred_element_type=F32)

        for h in range(nh):
            pass1(h)
        for h in range(nh):
            pass2(h)

    if has_far:
        pl.when(bo < n_near)(lambda: step(True))
        pl.when(bo >= n_near)(lambda: step(False))
    else:
        step(True)

    @pl.when((flag & 2) != 0)
    def _():
        for h in range(nh):
            l = l_ref[h]
            o_ref[:, h * HEAD_DIM:(h + 1) * HEAD_DIM] = (acc_ref[h] / jnp.where(l > 0, l, 1.0)).astype(o_ref.dtype)


def _flash(q_arr, q_off, k_arr, k_off, v_arr, v_off, bias, n_heads, ratio, nh, pairs, has_far, member=None,
           et=None, name="flash"):
    b, s, _ = q_arr.shape
    t = ATT_TILE
    nkv = nh // ratio
    ng = n_heads // nh
    qi = jnp.asarray([p[0] for p in pairs], jnp.int32)
    ki = jnp.asarray([p[1] for p in pairs], jnp.int32)
    bo = jnp.asarray([p[2] for p in pairs], jnp.int32)
    fl = jnp.asarray([p[3] for p in pairs], jnp.int32)
    nb = bias.shape[1]
    nm = 0
    in_specs = [
        pl.BlockSpec((None, t, nh * HEAD_DIM), lambda bi, gi, p, qi, ki, bo, fl: (bi, qi[p], q_off + gi)),
        pl.BlockSpec((None, t, nkv * HEAD_DIM), lambda bi, gi, p, qi, ki, bo, fl: (bi, ki[p], k_off + gi)),
        pl.BlockSpec((None, t, nkv * HEAD_DIM), lambda bi, gi, p, qi, ki, bo, fl: (bi, ki[p], v_off + gi)),
        pl.BlockSpec((nh, nb, 8, 2 * t), lambda bi, gi, p, qi, ki, bo, fl: (gi, 0, 0, 0)),
    ]
    args = [q_arr, k_arr, v_arr, bias]
    if member is not None:
        nm = member.shape[1] // ng
        in_specs += [
            pl.BlockSpec((None, nm, t, LANES), lambda bi, gi, p, qi, ki, bo, fl: (bi, gi, qi[p], 0)),
            pl.BlockSpec((t, LANES), lambda bi, gi, p, qi, ki, bo, fl: (ki[p], 0)),
        ]
        args += [member, et]
    n_near = nb - 1 if has_far else nb
    body = functools.partial(_flash_body, nh=nh, ratio=ratio, nm=nm, n_near=n_near, has_far=has_far)
    return pl.pallas_call(
        body,
        grid_spec=pltpu.PrefetchScalarGridSpec(
            num_scalar_prefetch=4,
            grid=(b, ng, len(pairs)),
            in_specs=in_specs,
            out_specs=pl.BlockSpec((None, t, nh * HEAD_DIM), lambda bi, gi, p, qi, ki, bo, fl: (bi, qi[p], gi)),
            scratch_shapes=[pltpu.VMEM((nh, t, LANES), F32)] * 5 + [pltpu.VMEM((nh, n_near, t, t), F32),
                                                                    pltpu.VMEM((nh, t, t), F32),
                                                                    pltpu.VMEM((nh, t, t), BF16)],
        ),
        out_shape=jax.ShapeDtypeStruct((b, s, n_heads * HEAD_DIM), BF16),
        compiler_params=_cparams(("parallel", "parallel", "arbitrary"), vmem=ATT_VMEM_LIMIT),
        name=name,
    )(qi, ki, bo, fl, *args)


def _rel_bucket(dist):
    n = jnp.maximum(jnp.asarray(dist, jnp.int32), 0)
    max_exact = REL_BUCKETS // 2
    nf = jnp.maximum(n, 1).astype(jnp.float32)
    large = max_exact + (jnp.log(nf / max_exact) / math.log(REL_MAX_DIST / max_exact)
                         * (REL_BUCKETS - max_exact)).astype(jnp.int32)
    return jnp.where(n < max_exact, n, jnp.minimum(large, REL_BUCKETS - 1))


def _bias_of_dist(tab, dist):
    hit = _rel_bucket(dist)[..., None, None] == jnp.arange(REL_BUCKETS)[:, None]
    return jnp.sum(jnp.where(hit, tab, 0.0), axis=-2)


def _n_near(t):
    return -(-(REL_MAX_DIST - 1 + t) // t)


def _bias_vecs(tab, t, n_off, window=None):
    k = np.arange(2 * t)[None, :]
    dist = np.arange(n_off)[:, None] * t + np.where(k < t, -k, 2 * t - k)
    ok = dist >= 0
    if window is not None:
        ok &= dist < window
    bias = jnp.where(jnp.asarray(ok)[..., None], _bias_of_dist(tab, dist), -MASK_BIG)
    bias = jnp.transpose(bias, (2, 0, 1)).astype(F32)
    return jnp.broadcast_to(bias[:, :, None, :], (bias.shape[0], n_off, 8, 2 * t))


def _causal_pairs(nq, n_near):
    pairs = []
    for qi in range(nq):
        for ki in range(qi + 1):
            pairs.append((qi, ki, min(qi - ki, n_near), (1 if ki == 0 else 0) | (2 if ki == qi else 0)))
    return pairs


def _window_pairs(nq, n_back):
    pairs = []
    for qi in range(nq):
        lo = max(0, qi - n_back)
        for ki in range(lo, qi + 1):
            pairs.append((qi, ki, qi - ki, (1 if ki == lo else 0) | (2 if ki == qi else 0)))
    return pairs


def _block_onehot(s, blk):
    return jnp.asarray(np.where(np.arange(s)[:, None] // blk == np.arange(LANES)[None, :], MASK_BIG, 0.0), BF16)


def _merge_body(oc_ref, os_ref, ow_ref, gl_ref, ob_ref, gma_ref, gmb_ref, wa_ref, wb_ref, o_ref, oa_ref):
    @pl.when(pl.program_id(1) == 0)
    def _():
        gates = jax.nn.sigmoid(gl_ref[...])
        for h in range(NSA_HEADS):
            hs = slice(h * HEAD_DIM, (h + 1) * HEAD_DIM)
            mix = (gates[:, 3 * h:3 * h + 1] * oc_ref[:, hs].astype(F32)
                   + gates[:, 3 * h + 1:3 * h + 2] * os_ref[:, hs].astype(F32)
                   + gates[:, 3 * h + 2:3 * h + 3] * ow_ref[:, hs].astype(F32))
            oa_ref[:, hs] = mix.astype(BF16)

    ya = jnp.dot(oa_ref[...], wa_ref[...].astype(BF16), preferred_element_type=F32)
    yb = jnp.dot(ob_ref[...], wb_ref[...].astype(BF16), preferred_element_type=F32)
    ga = jax.nn.sigmoid(gma_ref[...].astype(F32))
    gb = jax.nn.sigmoid(gmb_ref[...].astype(F32))
    o_ref[...] = (ga * ya + gb * yb).astype(o_ref.dtype)


def _merge(o_c, o_s, o_w, gate_logits, o_b, gm, w_up_a, w_up_b, tm=1024, tn=512):
    t, ka = o_c.shape
    kb = o_b.shape[1]
    d = w_up_a.shape[1]
    tm = min(tm, t)
    nj = d // tn
    row = lambda i, j: (i, 0)
    return pl.pallas_call(
        _merge_body,
        grid=(t // tm, nj),
        in_specs=[pl.BlockSpec((tm, ka), row), pl.BlockSpec((tm, ka), row), pl.BlockSpec((tm, ka), row),
                  pl.BlockSpec((tm, LANES), row), pl.BlockSpec((tm, kb), row),
                  pl.BlockSpec((tm, tn), lambda i, j: (i, j)),
                  pl.BlockSpec((tm, tn), lambda i, j: (i, j + nj)),
                  pl.BlockSpec((ka, tn), lambda i, j: (0, j)),
                  pl.BlockSpec((kb, tn), lambda i, j: (0, j))],
        out_specs=pl.BlockSpec((tm, tn), lambda i, j: (i, j)),
        out_shape=jax.ShapeDtypeStruct((t, d), BF16),
        scratch_shapes=[pltpu.VMEM((tm, ka), BF16)],
        compiler_params=_cparams(("parallel", "arbitrary")),
        name="merge_up",
    )(o_c, o_s, o_w, gate_logits, o_b, gm, gm, w_up_a, w_up_b)


def _route_body(x_ref, g_ref, w_ref, b_ref, h_ref, info_ref, cnt_ref, carry_ref, *, tm):
    @pl.when(pl.program_id(0) == 0)
    def _():
        carry_ref[...] = jnp.zeros(carry_ref.shape, F32)

    x = x_ref[...]
    ms = jnp.mean(x * x, axis=-1, keepdims=True)
    h = x * lax.rsqrt(ms + RMS_EPS) * g_ref[...]
    _store_packed(h_ref, h, tm)
    w = w_ref[...]
    h1 = h.astype(BF16)
    h2 = (h - h1.astype(F32)).astype(BF16)
    w1 = w.astype(BF16)
    w2 = (w - w1.astype(F32)).astype(BF16)
    logits = (jnp.dot(h1, w1, preferred_element_type=F32) + jnp.dot(h1, w2, preferred_element_type=F32)
              + jnp.dot(h2, w1, preferred_element_type=F32)) + b_ref[...]
    lane = lax.broadcasted_iota(jnp.int32, (tm, LANES), 1)
    lanef = lane.astype(F32)

    is_g = lane < N_GROUPS
    gl = jnp.where(is_g, logits, -MASK_BIG)
    ge = jnp.where(is_g, jnp.exp(gl - jnp.max(gl, axis=-1, keepdims=True)), 0.0)
    gp = ge / jnp.sum(ge, axis=-1, keepdims=True)
    g_val = jnp.max(gp, axis=-1, keepdims=True)
    g_idx = jnp.min(jnp.where(gp == g_val, lanef, float(LANES)), axis=-1, keepdims=True)

    lane_grp = ((lane - N_GROUPS) // EXPERTS_PER_GROUP).astype(F32)
    in_e = jnp.where(lane >= N_GROUPS, jnp.where(lane < N_GROUPS + N_EXPERTS, 1.0, 0.0), 0.0)
    is_e = jnp.where(lane_grp == g_idx, in_e, 0.0) > 0.5
    el = jnp.where(is_e, logits, -MASK_BIG)
    ee = jnp.where(is_e, jnp.exp(el - jnp.max(el, axis=-1, keepdims=True)), 0.0)
    ep = jnp.where(is_e, ee / jnp.sum(ee, axis=-1, keepdims=True), -1.0)
    v1 = jnp.max(ep, axis=-1, keepdims=True)
    l1 = jnp.min(jnp.where(ep == v1, lanef, float(LANES)), axis=-1, keepdims=True)
    ep2 = jnp.where(lanef == l1, -1.0, ep)
    v2 = jnp.max(ep2, axis=-1, keepdims=True)
    l2 = jnp.min(jnp.where(ep2 == v2, lanef, float(LANES)), axis=-1, keepdims=True)
    vs = v1 + v2
    wt1 = g_val * v1 / vs
    wt2 = g_val * v2 / vs
    e1 = l1 - float(N_GROUPS)
    e2 = l2 - float(N_GROUPS)

    oh = jnp.where(lanef == e1, 1.0, jnp.where(lanef == e2, 1.0, 0.0))
    r_i = lax.broadcasted_iota(jnp.int32, (tm, tm), 0)
    c_i = lax.broadcasted_iota(jnp.int32, (tm, tm), 1)
    tri = jnp.where(r_i > c_i, 1.0, 0.0).astype(BF16)
    base = jnp.dot(tri, oh.astype(BF16), preferred_element_type=F32) + carry_ref[...]
    r1 = jnp.sum(jnp.where(lanef == e1, base, 0.0), axis=-1, keepdims=True)
    r2 = jnp.sum(jnp.where(lanef == e2, base, 0.0), axis=-1, keepdims=True)
    carry_ref[...] = carry_ref[...] + jnp.sum(oh, axis=0, keepdims=True)
    cnt_ref[...] = jnp.broadcast_to(carry_ref[...], cnt_ref.shape)
    info = jnp.where(lane == 0, e1, jnp.where(lane == 1, e2, jnp.where(lane == 2, wt1, jnp.where(
        lane == 3, wt2, jnp.where(lane == 4, r1, jnp.where(lane == 5, r2, 0.0))))))
    info_ref[...] = info


def _route(x1, g, w_gr, b_gr, tm=512):
    t, d = x1.shape
    return pl.pallas_call(
        functools.partial(_route_body, tm=tm),
        grid=(t // tm,),
        in_specs=[pl.BlockSpec((tm, d), lambda i: (i, 0)),
                  pl.BlockSpec((1, d), lambda i: (0, 0)),
                  pl.BlockSpec((d, LANES), lambda i: (0, 0)),
                  pl.BlockSpec((1, LANES), lambda i: (0, 0))],
        out_specs=[pl.BlockSpec((tm * ROW_SUB, LANES), lambda i: (i, 0)),
                   pl.BlockSpec((tm, LANES), lambda i: (i, 0)),
                   pl.BlockSpec((8, LANES), lambda i: (0, 0))],
        out_shape=[jax.ShapeDtypeStruct((t * ROW_SUB, LANES), jnp.uint32),
                   jax.ShapeDtypeStruct((t, LANES), F32),
                   jax.ShapeDtypeStruct((8, LANES), F32)],
        scratch_shapes=[pltpu.VMEM((1, LANES), F32)],
        compiler_params=_cparams(("arbitrary",)),
        name="moe_route",
    )(x1, g.reshape(1, d), w_gr, b_gr)


ROW_SUB = 8
U32 = jnp.uint32


def _pack_pairs(lo, hi):
    lo_b = lax.bitcast_convert_type(lo.astype(BF16).astype(F32), U32)
    hi_b = lax.bitcast_convert_type(hi.astype(BF16).astype(F32), U32)
    return lax.shift_right_logical(lo_b, U32(16)) | (hi_b & U32(0xFFFF0000))


def _unpack_pairs(w):
    lo = lax.bitcast_convert_type(lax.shift_left(w, U32(16)), F32)
    hi = lax.bitcast_convert_type(w & U32(0xFFFF0000), F32)
    return lo, hi


def _store_packed(ref, y, n):
    half = y.shape[1] // 2
    for s in range(ROW_SUB):
        cs = slice(s * LANES, (s + 1) * LANES)
        ref[pl.ds(s, n, stride=ROW_SUB), :] = _pack_pairs(y[:, cs], y[:, half + s * LANES:half + (s + 1) * LANES])


def _load_packed(ref, n):
    los, his = [], []
    for s in range(ROW_SUB):
        lo, hi = _unpack_pairs(ref[pl.ds(s, n, stride=ROW_SUB), :])
        los.append(lo)
        his.append(hi)
    return jnp.concatenate(los + his, axis=1)


def _row_copy(src_ref, src_row, dst_ref, dst_row, sem):
    return pltpu.make_async_copy(src_ref.at[pl.ds(pl.multiple_of(src_row * ROW_SUB, ROW_SUB), ROW_SUB)],
                                 dst_ref.at[pl.ds(pl.multiple_of(dst_row * ROW_SUB, ROW_SUB), ROW_SUB)], sem)


def _dispatch_body(dest_ref, h_ref, xs_in_ref, xs_ref, sem, *, tm):
    del xs_in_ref
    base = pl.program_id(0) * tm

    def issue(r, c):
        for k in range(EXPERT_TOPK):
            _row_copy(h_ref, r, xs_ref, dest_ref[EXPERT_TOPK * (base + r) + k], sem).start(priority=k % 2)
        return c

    lax.fori_loop(0, tm, issue, 0, unroll=4)

    def drain(r, c):
        for k in range(EXPERT_TOPK):
            _row_copy(h_ref, r, xs_ref, dest_ref[EXPERT_TOPK * (base + r) + k], sem).wait()
        return c

    lax.fori_loop(0, tm, drain, 0, unroll=4)


def _dispatch(dest, hp, n_rows, tm=512):
    t = hp.shape[0] // ROW_SUB
    xs0 = jnp.zeros((n_rows * ROW_SUB, LANES), U32)
    return pl.pallas_call(
        functools.partial(_dispatch_body, tm=tm),
        grid_spec=pltpu.PrefetchScalarGridSpec(
            num_scalar_prefetch=1,
            grid=(t // tm,),
            in_specs=[pl.BlockSpec((tm * ROW_SUB, LANES), lambda i, dest: (i, 0)),
                      pl.BlockSpec(memory_space=pl.ANY)],
            out_specs=pl.BlockSpec(memory_space=pl.ANY),
            scratch_shapes=[pltpu.SemaphoreType.DMA(())],
        ),
        out_shape=jax.ShapeDtypeStruct((n_rows * ROW_SUB, LANES), U32),
        input_output_aliases={2: 0},
        compiler_params=_cparams(("arbitrary",)),
        name="moe_dispatch",
    )(dest, hp, xs0)


def _expert_body(nu_ref, sq_ref, es_ref, ns_ref, x_ref, wg_hbm, wu_hbm, wd_hbm, y_ref,
                 wg_b, wu_b, wd_b, wg_s, wu_s, wd_s, sem):
    i = pl.program_id(0)
    nu = nu_ref[0]
    ns = ns_ref[0]

    def weight_copies(seq, slot):
        e = es_ref[seq]
        return (pltpu.make_async_copy(wg_hbm.at[e], wg_b.at[slot], sem.at[slot]),
                pltpu.make_async_copy(wu_hbm.at[e], wu_b.at[slot], sem.at[slot]),
                pltpu.make_async_copy(wd_hbm.at[e], wd_b.at[slot], sem.at[slot]))

    def start_weights(seq, slot):
        for c in weight_copies(seq, slot):
            c.start()

    @pl.when(i == 0)
    def _():
        start_weights(0, 0)

        @pl.when(ns > 1)
        def _():
            start_weights(1, 1)

    s = sq_ref[i]
    first = (i == 0) | (s != sq_ref[jnp.maximum(i - 1, 0)])

    @pl.when((i < nu) & first)
    def _():
        slot = lax.rem(s, 2)
        for c in weight_copies(s, slot):
            c.wait()
        wg_s[...] = wg_b[slot].astype(BF16)
        wu_s[...] = wu_b[slot].astype(BF16)
        wd_s[...] = wd_b[slot].astype(BF16)

        @pl.when(s + 2 < ns)
        def _():
            start_weights(s + 2, slot)

    @pl.when(i < nu)
    def _():
        x = _load_packed(x_ref, MOE_ROWS).astype(BF16)
        g = jnp.dot(x, wg_s[...], preferred_element_type=F32)
        u = jnp.dot(x, wu_s[...], preferred_element_type=F32)
        mid = (jax.nn.silu(g) * u).astype(BF16)
        _store_packed(y_ref, jnp.dot(mid, wd_s[...], preferred_element_type=F32), MOE_ROWS)

    @pl.when(i >= nu_ref[0])
    def _():
        y_ref[...] = jnp.zeros(y_ref.shape, y_ref.dtype)


def _experts(n_used, seq_of_block, expert_of_seq, n_seq, xs, w_gate, w_up, w_down):
    n_blocks = xs.shape[0] // (MOE_ROWS * ROW_SUB)
    _, d, ff = w_gate.shape
    assert d == 2 * ROW_SUB * LANES, "a packed row must be exactly one (8,128) tile"
    blk = lambda i, nu, sq, es, ns: (jnp.minimum(i, nu[0] - 1), 0)
    hbm = pl.BlockSpec(memory_space=pl.ANY)
    return pl.pallas_call(
        _expert_body,
        grid_spec=pltpu.PrefetchScalarGridSpec(
            num_scalar_prefetch=4,
            grid=(n_blocks,),
            in_specs=[pl.BlockSpec((MOE_ROWS * ROW_SUB, LANES), blk), hbm, hbm, hbm],
            out_specs=pl.BlockSpec((MOE_ROWS * ROW_SUB, LANES), lambda i, nu, sq, es, ns: (i, 0)),
            scratch_shapes=[pltpu.VMEM((2, d, ff), F32), pltpu.VMEM((2, d, ff), F32), pltpu.VMEM((2, ff, d), F32),
                            pltpu.VMEM((d, ff), BF16), pltpu.VMEM((d, ff), BF16), pltpu.VMEM((ff, d), BF16),
                            pltpu.SemaphoreType.DMA((2,))],
        ),
        out_shape=jax.ShapeDtypeStruct(xs.shape, U32),
        compiler_params=_cparams(("arbitrary",)),
        name="moe_experts",
    )(n_used, seq_of_block, expert_of_seq, n_seq, xs, w_gate, w_up, w_down)


def _combine_body(dest_ref, x_ref, info_ref, g_ref, ys_ref, o_ref, buf0, buf1, sem, *, tm):
    i = pl.program_id(0)
    slot = lax.rem(i, 2)
    bufs = (buf0, buf1)

    def gather(tile, sl, wait):
        def body(r, c):
            for k in range(EXPERT_TOPK):
                cp = _row_copy(ys_ref, dest_ref[EXPERT_TOPK * (tile * tm + r) + k], bufs[k].at[sl], r, sem.at[sl])
                if wait:
                    cp.wait()
                else:
                    cp.start(priority=k % 2)
            return c
        lax.fori_loop(0, tm, body, 0, unroll=4)

    @pl.when(i == 0)
    def _():
        gather(0, 0, False)

    @pl.when(i + 1 < pl.num_programs(0))
    def _():
        gather(i + 1, 1 - slot, False)

    gather(i, slot, True)
    info = info_ref[...]
    y = x_ref[...] + (info[:, 2:3] * _load_packed(buf0.at[slot], tm) + info[:, 3:4] * _load_packed(buf1.at[slot], tm))
    ms = jnp.mean(y * y, axis=-1, keepdims=True)
    o_ref[...] = y * lax.rsqrt(ms + RMS_EPS) * g_ref[...]


def _combine(dest, x1, info, g, ys, tm=256):
    t, d = x1.shape
    return pl.pallas_call(
        functools.partial(_combine_body, tm=tm),
        grid_spec=pltpu.PrefetchScalarGridSpec(
            num_scalar_prefetch=1,
            grid=(t // tm,),
            in_specs=[pl.BlockSpec((tm, d), lambda i, dest: (i, 0)),
                      pl.BlockSpec((tm, LANES), lambda i, dest: (i, 0)),
                      pl.BlockSpec((1, d), lambda i, dest: (0, 0)),
                      pl.BlockSpec(memory_space=pl.ANY)],
            out_specs=pl.BlockSpec((tm, d), lambda i, dest: (i, 0)),
            scratch_shapes=[pltpu.VMEM((2, tm * ROW_SUB, LANES), U32), pltpu.VMEM((2, tm * ROW_SUB, LANES), U32),
                            pltpu.SemaphoreType.DMA((2,))],
        ),
        out_shape=jax.ShapeDtypeStruct((t, d), F32),
        compiler_params=_cparams(("arbitrary",)),
        name="moe_combine",
    )(dest, x1, info, g.reshape(1, d), ys)


def _nsa(proj, gate_cols, pe_k, w1_k, w2_k, pe_v, w1_v, w2_v, tab, b, s):
    del gate_cols
    g, dh = NSA_KV_HEADS, HEAD_DIM
    qw = NSA_HEADS * dh
    nc = s // CMP_STRIDE

    def blocks16(col0):
        a = proj[:, :, col0:col0 + g * dh].reshape(b, nc, CMP_STRIDE, g, dh)
        return a.transpose(0, 3, 1, 2, 4).reshape(b * g, nc, CMP_STRIDE * dh)

    kc = _compress(blocks16(qw), pe_k, w1_k, w2_k).reshape(b, g, nc, dh)
    vc = _compress(blocks16(qw + g * dh), pe_v, w1_v, w2_v).reshape(b, g, nc, dh)

    c_start = np.arange(nc)[None, :] * CMP_STRIDE
    n_sel = s // SEL_BLOCK
    sb = np.arange(n_sel)[:, None] * SEL_BLOCK
    overlap = jnp.asarray((c_start < sb + SEL_BLOCK) & (c_start + CMP_BLOCK > sb), BF16)
    o_c, member = _nsa_cmp(proj, kc, vc, tab, overlap, b, s)

    t = ATT_TILE
    nq = s // t
    nn = _n_near(t)
    nh = ATT_HEADS_PER_STEP
    kblk = qw // (nh // NSA_GROUP * dh)
    per = g // (nh // NSA_GROUP)
    bias_d = _bias_vecs(tab, t, nn + 1)
    o_s = _flash(proj, 0, proj, kblk + 2 * per, proj, kblk + 3 * per, bias_d, NSA_HEADS, NSA_GROUP, nh,
                 _causal_pairs(nq, nn), True, member=member, et=_block_onehot(s, SEL_BLOCK), name="nsa_selected")
    n_back = -(-WINDOW // t)
    bias_w = _bias_vecs(tab, t, n_back + 1, window=WINDOW)
    o_w = _flash(proj, 0, proj, kblk + 4 * per, proj, kblk + 5 * per, bias_w, NSA_HEADS, NSA_GROUP, nh,
                 _window_pairs(nq, n_back), False, name="nsa_window")
    return o_c, o_s, o_w


def _moba(proj, tab, b, s):
    member = _moba_gate(proj, b, s)
    t = ATT_TILE
    nn = _n_near(t)
    nh = ATT_HEADS_PER_STEP
    ng = MOBA_HEADS // nh
    bias_d = _bias_vecs(tab, t, nn + 1)
    return _flash(proj, 0, proj, ng, proj, 2 * ng, bias_d, MOBA_HEADS, 1, nh, _causal_pairs(s // t, nn),
                  True, member=member, et=_block_onehot(s, MOBA_BLOCK), name="moba_attn")


def _moe(x1, g_ffn, w_group, b_group, w_router, b_router, w_gate, w_up, w_down, g_final):
    t, d = x1.shape
    ng, _, epg = w_router.shape
    w_gr = jnp.concatenate([w_group, jnp.transpose(w_router, (1, 0, 2)).reshape(d, ng * epg),
                            jnp.zeros((d, LANES - ng - ng * epg), F32)], axis=1)
    b_gr = jnp.concatenate([b_group, b_router.reshape(-1), jnp.zeros((LANES - ng - ng * epg,), F32)]).reshape(1, LANES)
    h, info, cnt = _route(x1, g_ffn, w_gr, b_gr)
    n_e = ng * epg
    n_assign = t * EXPERT_TOPK
    n_blocks = -(-(n_assign + n_e * (MOE_ROWS - 1)) // MOE_ROWS)
    counts = cnt[0, :n_e].astype(jnp.int32)
    padded = (counts + MOE_ROWS - 1) // MOE_ROWS * MOE_ROWS
    pad_end = jnp.cumsum(padded)
    pad_start = pad_end - padded
    expert = info[:, 0:EXPERT_TOPK].astype(jnp.int32)
    rank = info[:, 4:4 + EXPERT_TOPK].astype(jnp.int32)
    e_ids = jnp.arange(n_e, dtype=jnp.int32)
    dest = (jnp.sum(jnp.where(expert[..., None] == e_ids, pad_start, 0), axis=-1) + rank).reshape(-1)
    block_row0 = jnp.arange(n_blocks, dtype=jnp.int32) * MOE_ROWS
    block_e = jnp.minimum(jnp.sum((pad_end[None, :] <= block_row0[:, None]).astype(jnp.int32), axis=1), n_e - 1)
    n_used = (pad_end[-1:] // MOE_ROWS).astype(jnp.int32)
    owns = counts > 0
    seq_of_expert = jnp.cumsum(owns.astype(jnp.int32)) - 1
    n_seq = jnp.sum(owns.astype(jnp.int32)).reshape(1)
    expert_of_seq = jnp.sum(jnp.where(owns[None, :] & (seq_of_expert[None, :] == e_ids[:, None]), e_ids[None, :], 0),
                            axis=1)
    seq_of_block = jnp.sum(jnp.where(block_e[:, None] == e_ids[None, :], seq_of_expert[None, :], 0), axis=1)
    xs = _dispatch(dest, h, n_blocks * MOE_ROWS)
    ys = _experts(n_used, seq_of_block, expert_of_seq, n_seq, xs, w_gate, w_up, w_down)
    return _combine(dest, x1, info, g_final, ys)


def kernel(x, rel_bias, norm_mix, w_in, cmp_pe_k, cmp_w1_k, cmp_w2_k, cmp_pe_v, cmp_w1_v, cmp_w2_v, w_up_nsa,
           w_up_moba, w_out, norm_ffn, w_group, b_group, w_router, b_router, w_exp_gate, w_exp_up, w_exp_down,
           final_norm):
    b, s, d = x.shape
    t = b * s
    depth = w_in.shape[0]
    tab_a = rel_bias[:, :NSA_HEADS]
    tab_b = rel_bias[:, NSA_HEADS:]
    a_cols = NSA_HEADS * HEAD_DIM + 6 * NSA_KV_HEADS * HEAD_DIM
    gate_cols = 3 * NSA_HEADS
    b_cols = 3 * MOBA_HEADS * HEAD_DIM
    xt = x.reshape(t, d)
    out = None
    for l in range(depth):
        h = _rmsnorm(xt, norm_mix[l], BF16)
        wt = jnp.swapaxes(w_in[l], 0, 1)
        b_col0 = a_cols + gate_cols
        proj_a = _matmul(h, wt, 0, a_cols, BF16, scaled_cols=NSA_HEADS * HEAD_DIM, col_scale=Q_SCALE,
                         w_transposed=True, name="in_proj_a").reshape(b, s, a_cols)
        gate_a = _matmul(h, wt, a_cols, LANES, F32, w_transposed=True, tn=LANES, name="in_proj_gate")
        proj_b = _matmul(h, wt, b_col0, b_cols, BF16, scaled_cols=MOBA_HEADS * HEAD_DIM, col_scale=Q_SCALE,
                         w_transposed=True, name="in_proj_b").reshape(b, s, b_cols)
        gm = _matmul(h, wt, b_col0 + b_cols, 2 * d, BF16, w_transposed=True, name="in_proj_gm")
        o_c, o_s, o_w = _nsa(proj_a, gate_cols, cmp_pe_k[l], cmp_w1_k[l], cmp_w2_k[l],
                             cmp_pe_v[l], cmp_w1_v[l], cmp_w2_v[l], tab_a, b, s)
        o_b = _moba(proj_b, tab_b, b, s)
        merged = _merge(o_c.reshape(t, -1), o_s.reshape(t, -1), o_w.reshape(t, -1), gate_a,
                        o_b.reshape(t, -1), gm, w_up_nsa[l], w_up_moba[l])
        x1 = _matmul(merged, w_out[l], 0, d, F32, res=xt, name="out_proj")
        assert l == depth - 1, "only the last layer's MoE is fused with the final norm"
        out = _moe(x1, norm_ffn[l], w_group[l], b_group[l], w_router[l], b_router[l],
                   w_exp_gate[l], w_exp_up[l], w_exp_down[l], final_norm)
    return out.reshape(b, s, d)
```

```python
import functools
import math

import numpy as np
import jax
import jax.numpy as jnp
from jax import lax
from jax.experimental import pallas as pl
from jax.experimental.pallas import tpu as pltpu

F32 = jnp.float32
BF16 = jnp.bfloat16

HEAD_DIM = 128
NSA_HEADS = 8
NSA_KV_HEADS = 2
NSA_GROUP = NSA_HEADS // NSA_KV_HEADS
CMP_BLOCK = 32
CMP_STRIDE = 16
SEL_BLOCK = 64
SEL_TOPN = 16
WINDOW = 512
FORCED_SCORE = 1e4
MOBA_HEADS = 8
MOBA_BLOCK = 256
MOBA_TOPK = 3
REL_BUCKETS = 32
REL_MAX_DIST = 128
N_GROUPS = 8
EXPERTS_PER_GROUP = 8
N_EXPERTS = N_GROUPS * EXPERTS_PER_GROUP
EXPERT_TOPK = 2
RMS_EPS = 1e-6

LANES = 128
ATT_TILE = 512
ATT_HEADS_PER_STEP = 8
ATT_VMEM_LIMIT = 56 * 1024 * 1024
MOE_ROWS = 256
MASK_BIG = 1e30
M_INIT = -3e38
LOG2E = math.log2(math.e)
Q_SCALE = HEAD_DIM ** -0.5 * LOG2E
VMEM_LIMIT = 48 * 1024 * 1024


def _cparams(sem, vmem=VMEM_LIMIT, flags=None):
    return pltpu.CompilerParams(dimension_semantics=sem, vmem_limit_bytes=vmem, flags=flags)


def _rmsnorm_body(x_ref, g_ref, o_ref):
    x = x_ref[...]
    ms = jnp.mean(x * x, axis=-1, keepdims=True)
    o_ref[...] = (x * lax.rsqrt(ms + RMS_EPS) * g_ref[...]).astype(o_ref.dtype)


def _rmsnorm(x, g, out_dtype, tm=512):
    t, d = x.shape
    return pl.pallas_call(
        _rmsnorm_body,
        grid=(t // tm,),
        in_specs=[pl.BlockSpec((tm, d), lambda i: (i, 0)),
                  pl.BlockSpec((1, d), lambda i: (0, 0))],
        out_specs=pl.BlockSpec((tm, d), lambda i: (i, 0)),
        out_shape=jax.ShapeDtypeStruct((t, d), out_dtype),
        compiler_params=_cparams(("parallel",)),
        name="rmsnorm",
    )(x, g.reshape(1, d))


def _mm_body(*refs, has_res, n_scaled, col_scale, w_transposed):
    if has_res:
        a_ref, w_ref, r_ref, o_ref = refs
    else:
        a_ref, w_ref, o_ref = refs
    w = w_ref[...].astype(BF16)
    if w_transposed:
        acc = lax.dot_general(a_ref[...], w, (((1,), (1,)), ((), ())), preferred_element_type=F32)
    else:
        acc = jnp.dot(a_ref[...], w, preferred_element_type=F32)
    if n_scaled:
        acc = acc * jnp.where(pl.program_id(1) < n_scaled, col_scale, 1.0)
    if has_res:
        acc = acc + r_ref[...]
    o_ref[...] = acc.astype(o_ref.dtype)


def _matmul(a, w, col0, ncols, out_dtype, res=None, scaled_cols=0, col_scale=1.0, w_transposed=False,
            tm=2048, tn=512, name="matmul"):
    t, k = a.shape
    tn = min(tn, ncols)
    tm = min(tm, t)
    assert ncols % tn == 0 and t % tm == 0 and scaled_cols % tn == 0
    if w_transposed:
        assert col0 % 8 == 0
        w_spec = pl.BlockSpec((pl.Element(tn), pl.Element(k)),
                              lambda i, j: (pl.multiple_of(col0 + j * tn, 8), 0))
    else:
        assert col0 % tn == 0
        off = col0 // tn
        w_spec = pl.BlockSpec((k, tn), lambda i, j: (0, j + off))
    in_specs = [pl.BlockSpec((tm, k), lambda i, j: (i, 0)), w_spec]
    args = [a, w]
    if res is not None:
        in_specs.append(pl.BlockSpec((tm, tn), lambda i, j: (i, j)))
        args.append(res)
    return pl.pallas_call(
        functools.partial(_mm_body, has_res=res is not None, n_scaled=scaled_cols // tn, col_scale=col_scale,
                          w_transposed=w_transposed),
        grid=(t // tm, ncols // tn),
        in_specs=in_specs,
        out_specs=pl.BlockSpec((tm, tn), lambda i, j: (i, j)),
        out_shape=jax.ShapeDtypeStruct((t, ncols), out_dtype),
        compiler_params=_cparams(("parallel", "parallel")),
        name=name,
    )(*args)


def _compress_body(u_ref, pe_ref, w1_ref, w2_ref, o_ref, *, nc):
    u = u_ref[...]
    w1 = w1_ref[...].astype(BF16)
    half = u.shape[1]
    a = jnp.dot(u, w1[:half], preferred_element_type=F32)
    b = jnp.dot(u, w1[half:], preferred_element_type=F32)
    peb = jnp.dot(pe_ref[...].astype(BF16), w1, preferred_element_type=F32)[0:1]
    pre = a + pltpu.roll(b, nc - 1, 0) + peb
    hid = jax.nn.gelu(pre)
    o_ref[...] = jnp.dot(hid.astype(BF16), w2_ref[...].astype(BF16),
                         preferred_element_type=F32).astype(o_ref.dtype)


def _compress(u, pe, w1, w2):
    bg, nc, kk = u.shape
    hid = w1.shape[1]
    dh = w2.shape[1]
    pe8 = jnp.broadcast_to(pe.reshape(1, -1), (16, pe.size))
    return pl.pallas_call(
        functools.partial(_compress_body, nc=nc),
        grid=(bg,),
        in_specs=[pl.BlockSpec((None, nc, kk), lambda i: (i, 0, 0)),
                  pl.BlockSpec((16, 2 * kk), lambda i: (0, 0)),
                  pl.BlockSpec((2 * kk, hid), lambda i: (0, 0)),
                  pl.BlockSpec((hid, dh), lambda i: (0, 0))],
        out_specs=pl.BlockSpec((None, nc, dh), lambda i: (i, 0, 0)),
        out_shape=jax.ShapeDtypeStruct((bg, nc, dh), BF16),
        compiler_params=_cparams(("parallel",)),
        name="nsa_compress",
    )(u, pe8, w1, w2)


def _split3(x):
    p1 = x.astype(BF16)
    r = x - p1.astype(F32)
    p2 = r.astype(BF16)
    p3 = (r - p2.astype(F32)).astype(BF16)
    return p1, p2, p3


def _rank_count(score, n_rows):
    groups = []
    for g0 in range(0, n_rows, 8):
        sg = score[g0:min(g0 + 8, n_rows), :]
        n_iota = g0 + lax.broadcasted_iota(jnp.int32, sg.shape, 0)
        cnt = jnp.zeros(sg.shape, F32)
        for m in range(n_rows):
            row = score[m:m + 1, :]
            if m < g0:
                beats = row >= sg
            elif m >= g0 + 8:
                beats = row > sg
            else:
                tie = jnp.where(n_iota > m, 1.0, 0.0)
                beats = jnp.where(row > sg, 1.0, jnp.where(row == sg, tie, 0.0)) > 0.5
            cnt = cnt + jnp.where(beats, 1.0, 0.0)
        groups.append(cnt)
    return jnp.concatenate(groups, axis=0) if len(groups) > 1 else groups[0]


def _nsa_cmp_body(q_ref, kc_ref, vc_ref, bias_ref, ov_ref, oc_ref, mem_ref, *, tq, nc, n_sel):
    t0 = pl.program_id(2) * tq
    kc = kc_ref[...]
    vc = vc_ref[...]
    t_idx = t0 + lax.broadcasted_iota(jnp.int32, (tq, nc), 0)
    c_idx = lax.broadcasted_iota(jnp.int32, (tq, nc), 1)
    dist = t_idx - (c_idx * CMP_STRIDE + (CMP_BLOCK - 1))
    n_k = REL_MAX_DIST // CMP_STRIDE
    kidx = jnp.where(dist < 0, n_k + 1, jnp.minimum(lax.shift_right_logical(dist, 4), n_k))
    assert CMP_STRIDE == 16
    psum = jnp.zeros((tq, nc), F32)
    for j in range(NSA_GROUP):
        hs = slice(j * HEAD_DIM, (j + 1) * HEAD_DIM)
        gt = bias_ref[j] * LOG2E
        bias = jnp.concatenate([jnp.take_along_axis(gt, kidx[:, c0:c0 + LANES], axis=1)
                                for c0 in range(0, nc, LANES)], axis=1)
        s = lax.dot_general(q_ref[:, hs], kc, (((1,), (1,)), ((), ())), preferred_element_type=F32) + bias
        m = jnp.max(s, axis=-1, keepdims=True)
        m = jnp.where(m > -0.5 * MASK_BIG, m, 0.0)
        p = jnp.exp2(s - m)
        d = jnp.sum(p, axis=-1, keepdims=True)
        p = p / jnp.where(d > 0, d, 1.0)
        oc_ref[:, hs] = jnp.dot(p.astype(BF16), vc, preferred_element_type=F32).astype(oc_ref.dtype)
        psum = psum + p
    ov = ov_ref[...]
    nt = (((1,), (1,)), ((), ()))
    p1, p2, p3 = _split3(psum)
    psel = (lax.dot_general(ov, p1, nt, preferred_element_type=F32)
            + lax.dot_general(ov, p2, nt, preferred_element_type=F32)
            + lax.dot_general(ov, p3, nt, preferred_element_type=F32))
    n_idx = lax.broadcasted_iota(jnp.int32, (n_sel, tq), 0)
    tt = t0 + lax.broadcasted_iota(jnp.int32, (n_sel, tq), 1)
    cur = tt // SEL_BLOCK
    forced = jnp.where(n_idx == 0, 1.0, jnp.where(n_idx == cur, 1.0, jnp.where(n_idx == cur - 1, 1.0, 0.0)))
    score = jnp.where(forced > 0.5, FORCED_SCORE, jnp.where(n_idx * SEL_BLOCK <= tt, psel, -1.0))
    cnt = _rank_count(score, n_sel)
    member = jnp.where(cnt < float(min(SEL_TOPN, n_sel)), 1.0, 0.0)
    if n_sel < LANES:
        member = jnp.concatenate([member, jnp.zeros((LANES - n_sel, tq), F32)], axis=0)
    mem_ref[...] = member.T.astype(mem_ref.dtype)


def _nsa_cmp(proj, kc, vc, tab, overlap, b, s, tq=256):
    g = NSA_KV_HEADS
    nc = kc.shape[2]
    n_sel = s // SEL_BLOCK
    gw = NSA_GROUP * HEAD_DIM
    assert tq % CMP_STRIDE == 0 and REL_MAX_DIST % CMP_STRIDE == 0
    n_k = REL_MAX_DIST // CMP_STRIDE
    rho = (np.arange(tq)[:, None] - (CMP_BLOCK - 1)) % CMP_STRIDE
    dd = np.concatenate([rho + CMP_STRIDE * np.arange(n_k)[None, :], np.full((tq, 1), REL_MAX_DIST)], axis=1)
    gtab = jnp.transpose(_bias_of_dist(tab, dd), (2, 0, 1)).astype(F32)
    bias_c = jnp.concatenate([gtab, jnp.full(gtab.shape[:2] + (1,), -MASK_BIG, F32),
                              jnp.zeros(gtab.shape[:2] + (LANES - n_k - 2,), F32)], axis=2)
    body = functools.partial(_nsa_cmp_body, tq=tq, nc=nc, n_sel=n_sel)
    return pl.pallas_call(
        body,
        grid=(b, g, s // tq),
        in_specs=[pl.BlockSpec((None, tq, gw), lambda bi, gi, i: (bi, i, gi)),
                  pl.BlockSpec((None, None, nc, HEAD_DIM), lambda bi, gi, i: (bi, gi, 0, 0)),
                  pl.BlockSpec((None, None, nc, HEAD_DIM), lambda bi, gi, i: (bi, gi, 0, 0)),
                  pl.BlockSpec((NSA_GROUP, tq, LANES), lambda bi, gi, i: (gi, 0, 0)),
                  pl.BlockSpec((n_sel, nc), lambda bi, gi, i: (0, 0))],
        out_specs=[pl.BlockSpec((None, tq, gw), lambda bi, gi, i: (bi, i, gi)),
                   pl.BlockSpec((None, None, tq, LANES), lambda bi, gi, i: (bi, gi, i, 0))],
        out_shape=[jax.ShapeDtypeStruct((b, s, NSA_HEADS * HEAD_DIM), BF16),
                   jax.ShapeDtypeStruct((b, g, s, LANES), BF16)],
        compiler_params=_cparams(("parallel", "parallel", "parallel")),
        name="nsa_cmp_select",
    )(proj, kc, vc, bias_c, overlap)


def _moba_gate_body(q_ref, k_ref, mem_ref, *, s, nblk):
    k = k_ref[...].astype(F32)
    kmean = jnp.mean(k.reshape(nblk, MOBA_BLOCK, HEAD_DIM), axis=1)
    k1 = kmean.astype(BF16)
    k2 = (kmean - k1.astype(F32)).astype(BF16)
    q = q_ref[...]
    nt = (((1,), (1,)), ((), ()))
    gate = (lax.dot_general(k1, q, nt, preferred_element_type=F32)
            + lax.dot_general(k2, q, nt, preferred_element_type=F32))
    n_idx = lax.broadcasted_iota(jnp.int32, (nblk, s), 0)
    own = lax.broadcasted_iota(jnp.int32, (nblk, s), 1) // MOBA_BLOCK
    past = n_idx < own
    score = jnp.where(past, gate, -MASK_BIG)
    cnt = _rank_count(score, nblk)
    n_top = max(1, min(MOBA_TOPK, nblk - 1))
    sel = jnp.where(past, jnp.where(cnt < float(n_top), 1.0, 0.0), 0.0)
    member = jnp.where(n_idx == own, 1.0, sel)
    member = jnp.concatenate([member, jnp.zeros((LANES - nblk, s), F32)], axis=0)
    mem_ref[...] = member.T.astype(mem_ref.dtype)


def _moba_gate(proj, b, s):
    h = MOBA_HEADS
    nblk = s // MOBA_BLOCK
    return pl.pallas_call(
        functools.partial(_moba_gate_body, s=s, nblk=nblk),
        grid=(b, h),
        in_specs=[pl.BlockSpec((None, s, HEAD_DIM), lambda bi, hi: (bi, 0, hi)),
                  pl.BlockSpec((None, s, HEAD_DIM), lambda bi, hi: (bi, 0, h + hi))],
        out_specs=pl.BlockSpec((None, None, s, LANES), lambda bi, hi: (bi, hi, 0, 0)),
        out_shape=jax.ShapeDtypeStruct((b, h, s, LANES), BF16),
        compiler_params=_cparams(("parallel", "parallel")),
        name="moba_gate",
    )(proj, proj)


def _flash_body(qi_ref, ki_ref, bo_ref, fl_ref, *refs, nh, ratio, nm, n_near, has_far):
    if nm:
        (q_ref, k_ref, v_ref, bvec_ref, mem_ref, et_ref, o_ref,
         m_ref, l_ref, acc_ref, sh_ref, al_ref, bias_ref, s_ref, p_ref) = refs
    else:
        q_ref, k_ref, v_ref, bvec_ref, o_ref, m_ref, l_ref, acc_ref, sh_ref, al_ref, bias_ref, s_ref, p_ref = refs
    del qi_ref, ki_ref
    p = pl.program_id(2)
    flag = fl_ref[p]
    bo = bo_ref[p]
    t = q_ref.shape[0]
    rows = 64

    @pl.when(p == 0)
    def _():
        for h in range(nh):
            for o in range(n_near):
                vec = bvec_ref[h, o][0:1, :] * LOG2E
                for rc in range(t // rows):
                    x = pltpu.roll(jnp.broadcast_to(vec, (rows, 2 * t)), rc * rows, 1, stride=1, stride_axis=0)
                    bias_ref[h, o, rc * rows:(rc + 1) * rows, :] = x[:, :t]

    @pl.when((flag & 1) != 0)
    def _():
        m_ref[...] = jnp.full(m_ref.shape, M_INIT, F32)
        l_ref[...] = jnp.zeros(l_ref.shape, F32)
        acc_ref[...] = jnp.zeros(acc_ref.shape, F32)

    nt = (((1,), (1,)), ((), ()))
    reps = t // LANES

    def step(near):
        def pass1(h):
            hs = slice(h * HEAD_DIM, (h + 1) * HEAD_DIM)
            kv = h // ratio
            ks = slice(kv * HEAD_DIM, (kv + 1) * HEAD_DIM)
            q = q_ref[:, hs]
            k = k_ref[:, ks]
            if nm:
                mneg = mem_ref[h // (nh // nm)] - 1.0
                q = jnp.concatenate([q, mneg.astype(BF16)], axis=1)
                k = jnp.concatenate([k, et_ref[...]], axis=1)
            sc = lax.dot_general(q, k, nt, preferred_element_type=F32)
            m_prev = m_ref[h]
            if near:
                sc = sc + bias_ref[h, bo]
                m_new = jnp.maximum(m_prev, jnp.max(sc, axis=-1, keepdims=True))
                sh_ref[h] = m_new
            else:
                cfar = bvec_ref[h, n_near][0:1, 0:LANES] * LOG2E
                m_new = jnp.maximum(m_prev, jnp.max(sc, axis=-1, keepdims=True) + cfar)
                sh_ref[h] = m_new - cfar
            s_ref[h] = sc
            al_ref[h] = jnp.exp2(m_prev - m_new)
            m_ref[h] = m_new

        def pass2(h):
            ks = slice(h // ratio * HEAD_DIM, (h // ratio + 1) * HEAD_DIM)
            for rc in range(t // rows):
                rs = slice(rc * rows, (rc + 1) * rows)
                pm = jnp.exp2(s_ref[h, rs, :] - jnp.tile(sh_ref[h, rs, :], (1, reps)))
                l_ref[h, rs, :] = al_ref[h, rs, :] * l_ref[h, rs, :] + jnp.sum(pm, axis=-1, keepdims=True)
                p_ref[h, rs, :] = pm.astype(BF16)
            acc_ref[h] = al_ref[h] * acc_ref[h] + jnp.dot(p_ref[h], v_ref[:, ks], preferred_element_type=F32)

        for h in range(nh):
            pass1(h)
        for h in range(nh):
            pass2(h)

    if has_far:
        pl.when(bo < n_near)(lambda: step(True))
        pl.when(bo >= n_near)(lambda: step(False))
    else:
        step(True)

    @pl.when((flag & 2) != 0)
    def _():
        for h in range(nh):
            l = l_ref[h]
            o_ref[:, h * HEAD_DIM:(h + 1) * HEAD_DIM] = (acc_ref[h] / jnp.where(l > 0, l, 1.0)).astype(o_ref.dtype)


def _flash(q_arr, q_off, k_arr, k_off, v_arr, v_off, bias, n_heads, ratio, nh, pairs, has_far, member=None,
           et=None, name="flash"):
    b, s, _ = q_arr.shape
    t = ATT_TILE
    nkv = nh // ratio
    ng = n_heads // nh
    qi = jnp.asarray([p[0] for p in pairs], jnp.int32)
    ki = jnp.asarray([p[1] for p in pairs], jnp.int32)
    bo = jnp.asarray([p[2] for p in pairs], jnp.int32)
    fl = jnp.asarray([p[3] for p in pairs], jnp.int32)
    nb = bias.shape[1]
    nm = 0
    in_specs = [
        pl.BlockSpec((None, t, nh * HEAD_DIM), lambda bi, gi, p, qi, ki, bo, fl: (bi, qi[p], q_off + gi)),
        pl.BlockSpec((None, t, nkv * HEAD_DIM), lambda bi, gi, p, qi, ki, bo, fl: (bi, ki[p], k_off + gi)),
        pl.BlockSpec((None, t, nkv * HEAD_DIM), lambda bi, gi, p, qi, ki, bo, fl: (bi, ki[p], v_off + gi)),
        pl.BlockSpec((nh, nb, 8, 2 * t), lambda bi, gi, p, qi, ki, bo, fl: (gi, 0, 0, 0)),
    ]
    args = [q_arr, k_arr, v_arr, bias]
    if member is not None:
        nm = member.shape[1] // ng
        in_specs += [
            pl.BlockSpec((None, nm, t, LANES), lambda bi, gi, p, qi, ki, bo, fl: (bi, gi, qi[p], 0)),
            pl.BlockSpec((t, LANES), lambda bi, gi, p, qi, ki, bo, fl: (ki[p], 0)),
        ]
        args += [member, et]
    n_near = nb - 1 if has_far else nb
    body = functools.partial(_flash_body, nh=nh, ratio=ratio, nm=nm, n_near=n_near, has_far=has_far)
    return pl.pallas_call(
        body,
        grid_spec=pltpu.PrefetchScalarGridSpec(
            num_scalar_prefetch=4,
            grid=(b, ng, len(pairs)),
            in_specs=in_specs,
            out_specs=pl.BlockSpec((None, t, nh * HEAD_DIM), lambda bi, gi, p, qi, ki, bo, fl: (bi, qi[p], gi)),
            scratch_shapes=[pltpu.VMEM((nh, t, LANES), F32)] * 5 + [pltpu.VMEM((nh, n_near, t, t), F32),
                                                                    pltpu.VMEM((nh, t, t), F32),
                                                                    pltpu.VMEM((nh, t, t), BF16)],
        ),
        out_shape=jax.ShapeDtypeStruct((b, s, n_heads * HEAD_DIM), BF16),
        compiler_params=_cparams(("parallel", "parallel", "arbitrary"), vmem=ATT_VMEM_LIMIT),
        name=name,
    )(qi, ki, bo, fl, *args)


def _rel_bucket(dist):
    n = jnp.maximum(jnp.asarray(dist, jnp.int32), 0)
    max_exact = REL_BUCKETS // 2
    nf = jnp.maximum(n, 1).astype(jnp.float32)
    large = max_exact + (jnp.log(nf / max_exact) / math.log(REL_MAX_DIST / max_exact)
                         * (REL_BUCKETS - max_exact)).astype(jnp.int32)
    return jnp.where(n < max_exact, n, jnp.minimum(large, REL_BUCKETS - 1))


def _bias_of_dist(tab, dist):
    hit = _rel_bucket(dist)[..., None, None] == jnp.arange(REL_BUCKETS)[:, None]
    return jnp.sum(jnp.where(hit, tab, 0.0), axis=-2)


def _n_near(t):
    return -(-(REL_MAX_DIST - 1 + t) // t)


def _bias_vecs(tab, t, n_off, window=None):
    k = np.arange(2 * t)[None, :]
    dist = np.arange(n_off)[:, None] * t + np.where(k < t, -k, 2 * t - k)
    ok = dist >= 0
    if window is not None:
        ok &= dist < window
    bias = jnp.where(jnp.asarray(ok)[..., None], _bias_of_dist(tab, dist), -MASK_BIG)
    bias = jnp.transpose(bias, (2, 0, 1)).astype(F32)
    return jnp.broadcast_to(bias[:, :, None, :], (bias.shape[0], n_off, 8, 2 * t))


def _causal_pairs(nq, n_near):
    pairs = []
    for qi in range(nq):
        for ki in range(qi + 1):
            pairs.append((qi, ki, min(qi - ki, n_near), (1 if ki == 0 else 0) | (2 if ki == qi else 0)))
    return pairs


def _window_pairs(nq, n_back):
    pairs = []
    for qi in range(nq):
        lo = max(0, qi - n_back)
        for ki in range(lo, qi + 1):
            pairs.append((qi, ki, qi - ki, (1 if ki == lo else 0) | (2 if ki == qi else 0)))
    return pairs


def _block_onehot(s, blk):
    return jnp.asarray(np.where(np.arange(s)[:, None] // blk == np.arange(LANES)[None, :], MASK_BIG, 0.0), BF16)


def _merge_body(oc_ref, os_ref, ow_ref, gl_ref, ob_ref, gma_ref, gmb_ref, wa_ref, wb_ref, o_ref, oa_ref):
    @pl.when(pl.program_id(1) == 0)
    def _():
        gates = jax.nn.sigmoid(gl_ref[...])
        for h in range(NSA_HEADS):
            hs = slice(h * HEAD_DIM, (h + 1) * HEAD_DIM)
            mix = (gates[:, 3 * h:3 * h + 1] * oc_ref[:, hs].astype(F32)
                   + gates[:, 3 * h + 1:3 * h + 2] * os_ref[:, hs].astype(F32)
                   + gates[:, 3 * h + 2:3 * h + 3] * ow_ref[:, hs].astype(F32))
            oa_ref[:, hs] = mix.astype(BF16)

    ya = jnp.dot(oa_ref[...], wa_ref[...].astype(BF16), preferred_element_type=F32)
    yb = jnp.dot(ob_ref[...], wb_ref[...].astype(BF16), preferred_element_type=F32)
    ga = jax.nn.sigmoid(gma_ref[...].astype(F32))
    gb = jax.nn.sigmoid(gmb_ref[...].astype(F32))
    o_ref[...] = (ga * ya + gb * yb).astype(o_ref.dtype)


def _merge(o_c, o_s, o_w, gate_logits, o_b, gm, w_up_a, w_up_b, tm=1024, tn=512):
    t, ka = o_c.shape
    kb = o_b.shape[1]
    d = w_up_a.shape[1]
    tm = min(tm, t)
    nj = d // tn
    row = lambda i, j: (i, 0)
    return pl.pallas_call(
        _merge_body,
        grid=(t // tm, nj),
        in_specs=[pl.BlockSpec((tm, ka), row), pl.BlockSpec((tm, ka), row), pl.BlockSpec((tm, ka), row),
                  pl.BlockSpec((tm, LANES), row), pl.BlockSpec((tm, kb), row),
                  pl.BlockSpec((tm, tn), lambda i, j: (i, j)),
                  pl.BlockSpec((tm, tn), lambda i, j: (i, j + nj)),
                  pl.BlockSpec((ka, tn), lambda i, j: (0, j)),
                  pl.BlockSpec((kb, tn), lambda i, j: (0, j))],
        out_specs=pl.BlockSpec((tm, tn), lambda i, j: (i, j)),
        out_shape=jax.ShapeDtypeStruct((t, d), BF16),
        scratch_shapes=[pltpu.VMEM((tm, ka), BF16)],
        compiler_params=_cparams(("parallel", "arbitrary")),
        name="merge_up",
    )(o_c, o_s, o_w, gate_logits, o_b, gm, gm, w_up_a, w_up_b)


def _route_body(x_ref, g_ref, w_ref, b_ref, h_ref, info_ref, cnt_ref, carry_ref, *, tm):
    @pl.when(pl.program_id(0) == 0)
    def _():
        carry_ref[...] = jnp.zeros(carry_ref.shape, F32)

    x = x_ref[...]
    ms = jnp.mean(x * x, axis=-1, keepdims=True)
    h = x * lax.rsqrt(ms + RMS_EPS) * g_ref[...]
    _store_packed(h_ref, h, tm)
    w = w_ref[...]
    h1 = h.astype(BF16)
    h2 = (h - h1.astype(F32)).astype(BF16)
    w1 = w.astype(BF16)
    w2 = (w - w1.astype(F32)).astype(BF16)
    logits = (jnp.dot(h1, w1, preferred_element_type=F32) + jnp.dot(h1, w2, preferred_element_type=F32)
              + jnp.dot(h2, w1, preferred_element_type=F32)) + b_ref[...]
    lane = lax.broadcasted_iota(jnp.int32, (tm, LANES), 1)
    lanef = lane.astype(F32)

    is_g = lane < N_GROUPS
    gl = jnp.where(is_g, logits, -MASK_BIG)
    ge = jnp.where(is_g, jnp.exp(gl - jnp.max(gl, axis=-1, keepdims=True)), 0.0)
    gp = ge / jnp.sum(ge, axis=-1, keepdims=True)
    g_val = jnp.max(gp, axis=-1, keepdims=True)
    g_idx = jnp.min(jnp.where(gp == g_val, lanef, float(LANES)), axis=-1, keepdims=True)

    lane_grp = ((lane - N_GROUPS) // EXPERTS_PER_GROUP).astype(F32)
    in_e = jnp.where(lane >= N_GROUPS, jnp.where(lane < N_GROUPS + N_EXPERTS, 1.0, 0.0), 0.0)
    is_e = jnp.where(lane_grp == g_idx, in_e, 0.0) > 0.5
    el = jnp.where(is_e, logits, -MASK_BIG)
    ee = jnp.where(is_e, jnp.exp(el - jnp.max(el, axis=-1, keepdims=True)), 0.0)
    ep = jnp.where(is_e, ee / jnp.sum(ee, axis=-1, keepdims=True), -1.0)
    v1 = jnp.max(ep, axis=-1, keepdims=True)
    l1 = jnp.min(jnp.where(ep == v1, lanef, float(LANES)), axis=-1, keepdims=True)
    ep2 = jnp.where(lanef == l1, -1.0, ep)
    v2 = jnp.max(ep2, axis=-1, keepdims=True)
    l2 = jnp.min(jnp.where(ep2 == v2, lanef, float(LANES)), axis=-1, keepdims=True)
    vs = v1 + v2
    wt1 = g_val * v1 / vs
    wt2 = g_val * v2 / vs
    e1 = l1 - float(N_GROUPS)
    e2 = l2 - float(N_GROUPS)

    oh = jnp.where(lanef == e1, 1.0, jnp.where(lanef == e2, 1.0, 0.0))
    r_i = lax.broadcasted_iota(jnp.int32, (tm, tm), 0)
    c_i = lax.broadcasted_iota(jnp.int32, (tm, tm), 1)
    tri = jnp.where(r_i > c_i, 1.0, 0.0).astype(BF16)
    base = jnp.dot(tri, oh.astype(BF16), preferred_element_type=F32) + carry_ref[...]
    r1 = jnp.sum(jnp.where(lanef == e1, base, 0.0), axis=-1, keepdims=True)
    r2 = jnp.sum(jnp.where(lanef == e2, base, 0.0), axis=-1, keepdims=True)
    carry_ref[...] = carry_ref[...] + jnp.sum(oh, axis=0, keepdims=True)
    cnt_ref[...] = jnp.broadcast_to(carry_ref[...], cnt_ref.shape)
    info = jnp.where(lane == 0, e1, jnp.where(lane == 1, e2, jnp.where(lane == 2, wt1, jnp.where(
        lane == 3, wt2, jnp.where(lane == 4, r1, jnp.where(lane == 5, r2, 0.0))))))
    info_ref[...] = info


def _route(x1, g, w_gr, b_gr, tm=512):
    t, d = x1.shape
    return pl.pallas_call(
        functools.partial(_route_body, tm=tm),
        grid=(t // tm,),
        in_specs=[pl.BlockSpec((tm, d), lambda i: (i, 0)),
                  pl.BlockSpec((1, d), lambda i: (0, 0)),
                  pl.BlockSpec((d, LANES), lambda i: (0, 0)),
                  pl.BlockSpec((1, LANES), lambda i: (0, 0))],
        out_specs=[pl.BlockSpec((tm * ROW_SUB, LANES), lambda i: (i, 0)),
                   pl.BlockSpec((tm, LANES), lambda i: (i, 0)),
                   pl.BlockSpec((8, LANES), lambda i: (0, 0))],
        out_shape=[jax.ShapeDtypeStruct((t * ROW_SUB, LANES), jnp.uint32),
                   jax.ShapeDtypeStruct((t, LANES), F32),
                   jax.ShapeDtypeStruct((8, LANES), F32)],
        scratch_shapes=[pltpu.VMEM((1, LANES), F32)],
        compiler_params=_cparams(("arbitrary",)),
        name="moe_route",
    )(x1, g.reshape(1, d), w_gr, b_gr)


ROW_SUB = 8
U32 = jnp.uint32


def _pack_pairs(lo, hi):
    lo_b = lax.bitcast_convert_type(lo.astype(BF16).astype(F32), U32)
    hi_b = lax.bitcast_convert_type(hi.astype(BF16).astype(F32), U32)
    return lax.shift_right_logical(lo_b, U32(16)) | (hi_b & U32(0xFFFF0000))


def _unpack_pairs(w):
    lo = lax.bitcast_convert_type(lax.shift_left(w, U32(16)), F32)
    hi = lax.bitcast_convert_type(w & U32(0xFFFF0000), F32)
    return lo, hi


def _store_packed(ref, y, n):
    half = y.shape[1] // 2
    for s in range(ROW_SUB):
        cs = slice(s * LANES, (s + 1) * LANES)
        ref[pl.ds(s, n, stride=ROW_SUB), :] = _pack_pairs(y[:, cs], y[:, half + s * LANES:half + (s + 1) * LANES])


def _load_packed(ref, n):
    los, his = [], []
    for s in range(ROW_SUB):
        lo, hi = _unpack_pairs(ref[pl.ds(s, n, stride=ROW_SUB), :])
        los.append(lo)
        his.append(hi)
    return jnp.concatenate(los + his, axis=1)


def _row_copy(src_ref, src_row, dst_ref, dst_row, sem):
    return pltpu.make_async_copy(src_ref.at[pl.ds(pl.multiple_of(src_row * ROW_SUB, ROW_SUB), ROW_SUB)],
                                 dst_ref.at[pl.ds(pl.multiple_of(dst_row * ROW_SUB, ROW_SUB), ROW_SUB)], sem)


def _dispatch_body(dest_ref, h_ref, xs_ref, sem, *, tm):
    base = pl.program_id(0) * tm

    def issue(r, c):
        for k in range(EXPERT_TOPK):
            _row_copy(h_ref, r, xs_ref, dest_ref[EXPERT_TOPK * (base + r) + k], sem).start(priority=k % 2)
        return c

    lax.fori_loop(0, tm, issue, 0, unroll=4)

    def drain(r, c):
        for k in range(EXPERT_TOPK):
            _row_copy(h_ref, r, xs_ref, dest_ref[EXPERT_TOPK * (base + r) + k], sem).wait()
        return c

    lax.fori_loop(0, tm, drain, 0, unroll=4)


def _dispatch(dest, hp, tm=512):
    t = hp.shape[0] // ROW_SUB
    n_rows = dest.shape[0]
    return pl.pallas_call(
        functools.partial(_dispatch_body, tm=tm),
        grid_spec=pltpu.PrefetchScalarGridSpec(
            num_scalar_prefetch=1,
            grid=(t // tm,),
            in_specs=[pl.BlockSpec((tm * ROW_SUB, LANES), lambda i, dest: (i, 0))],
            out_specs=pl.BlockSpec(memory_space=pl.ANY),
            scratch_shapes=[pltpu.SemaphoreType.DMA(())],
        ),
        out_shape=jax.ShapeDtypeStruct((n_rows * ROW_SUB, LANES), U32),
        compiler_params=_cparams(("arbitrary",)),
        name="moe_dispatch",
    )(dest, hp)


def _expert_body(nu_ref, sq_ref, es_ref, ns_ref, blk_ref, lo_ref, hi_ref, x_ref, wg_hbm, wu_hbm, wd_hbm, y_ref,
                 wg_b, wu_b, wd_b, wg_s, wu_s, wd_s, y_acc, sem):
    i = pl.program_id(0)
    nu = nu_ref[0]
    ns = ns_ref[0]

    def weight_copies(seq, slot):
        e = es_ref[seq]
        return (pltpu.make_async_copy(wg_hbm.at[e], wg_b.at[slot], sem.at[slot]),
                pltpu.make_async_copy(wu_hbm.at[e], wu_b.at[slot], sem.at[slot]),
                pltpu.make_async_copy(wd_hbm.at[e], wd_b.at[slot], sem.at[slot]))

    def start_weights(seq, slot):
        for c in weight_copies(seq, slot):
            c.start()

    @pl.when(i == 0)
    def _():
        y_acc[...] = jnp.zeros(y_acc.shape, F32)
        start_weights(0, 0)

        @pl.when(ns > 1)
        def _():
            start_weights(1, 1)

    s = sq_ref[i]
    first = (i == 0) | (s != sq_ref[jnp.maximum(i - 1, 0)])

    @pl.when((i < nu) & first)
    def _():
        slot = lax.rem(s, 2)
        for c in weight_copies(s, slot):
            c.wait()
        wg_s[...] = wg_b[slot].astype(BF16)
        wu_s[...] = wu_b[slot].astype(BF16)
        wd_s[...] = wd_b[slot].astype(BF16)

        @pl.when(s + 2 < ns)
        def _():
            start_weights(s + 2, slot)

    @pl.when(i < nu)
    def _():
        x = _load_packed(x_ref, MOE_ROWS).astype(BF16)
        g = jnp.dot(x, wg_s[...], preferred_element_type=F32)
        u = jnp.dot(x, wu_s[...], preferred_element_type=F32)
        mid = (jax.nn.silu(g) * u).astype(BF16)
        y = jnp.dot(mid, wd_s[...], preferred_element_type=F32)
        row = lax.broadcasted_iota(jnp.int32, (MOE_ROWS, 1), 0)
        mine = (row >= lo_ref[i]) & (row < hi_ref[i])
        new_block = (i == 0) | (blk_ref[i] != blk_ref[jnp.maximum(i - 1, 0)])
        y_acc[...] = jnp.where(mine, y, jnp.where(new_block, 0.0, y_acc[...]))
        _store_packed(y_ref, y_acc[...], MOE_ROWS)


def _experts(n_pairs, seq_of_pair, expert_of_seq, n_seq, blk_of_pair, lo, hi, xs, w_gate, w_up, w_down):
    n_steps = seq_of_pair.shape[0]
    _, d, ff = w_gate.shape
    assert d == 2 * ROW_SUB * LANES, "a packed row must be exactly one (8,128) tile"
    blk = lambda i, nu, sq, es, ns, bk, lo, hi: (bk[jnp.minimum(i, nu[0] - 1)], 0)
    hbm = pl.BlockSpec(memory_space=pl.ANY)
    return pl.pallas_call(
        _expert_body,
        grid_spec=pltpu.PrefetchScalarGridSpec(
            num_scalar_prefetch=7,
            grid=(n_steps,),
            in_specs=[pl.BlockSpec((MOE_ROWS * ROW_SUB, LANES), blk), hbm, hbm, hbm],
            out_specs=pl.BlockSpec((MOE_ROWS * ROW_SUB, LANES), blk),
            scratch_shapes=[pltpu.VMEM((2, d, ff), F32), pltpu.VMEM((2, d, ff), F32), pltpu.VMEM((2, ff, d), F32),
                            pltpu.VMEM((d, ff), BF16), pltpu.VMEM((d, ff), BF16), pltpu.VMEM((ff, d), BF16),
                            pltpu.VMEM((MOE_ROWS, d), F32), pltpu.SemaphoreType.DMA((2,))],
        ),
        out_shape=jax.ShapeDtypeStruct(xs.shape, U32),
        compiler_params=_cparams(("arbitrary",)),
        name="moe_experts",
    )(n_pairs, seq_of_pair, expert_of_seq, n_seq, blk_of_pair, lo, hi, xs, w_gate, w_up, w_down)


def _combine_body(dest_ref, x_ref, info_ref, g_ref, ys_ref, o_ref, buf0, buf1, sem, *, tm):
    i = pl.program_id(0)
    slot = lax.rem(i, 2)
    bufs = (buf0, buf1)

    def gather(tile, sl, wait):
        def body(r, c):
            for k in range(EXPERT_TOPK):
                cp = _row_copy(ys_ref, dest_ref[EXPERT_TOPK * (tile * tm + r) + k], bufs[k].at[sl], r, sem.at[sl])
                if wait:
                    cp.wait()
                else:
                    cp.start(priority=k % 2)
            return c
        lax.fori_loop(0, tm, body, 0, unroll=4)

    @pl.when(i == 0)
    def _():
        gather(0, 0, False)

    @pl.when(i + 1 < pl.num_programs(0))
    def _():
        gather(i + 1, 1 - slot, False)

    gather(i, slot, True)
    info = info_ref[...]
    y = x_ref[...] + (info[:, 2:3] * _load_packed(buf0.at[slot], tm) + info[:, 3:4] * _load_packed(buf1.at[slot], tm))
    ms = jnp.mean(y * y, axis=-1, keepdims=True)
    o_ref[...] = y * lax.rsqrt(ms + RMS_EPS) * g_ref[...]


def _combine(dest, x1, info, g, ys, tm=256):
    t, d = x1.shape
    return pl.pallas_call(
        functools.partial(_combine_body, tm=tm),
        grid_spec=pltpu.PrefetchScalarGridSpec(
            num_scalar_prefetch=1,
            grid=(t // tm,),
            in_specs=[pl.BlockSpec((tm, d), lambda i, dest: (i, 0)),
                      pl.BlockSpec((tm, LANES), lambda i, dest: (i, 0)),
                      pl.BlockSpec((1, d), lambda i, dest: (0, 0)),
                      pl.BlockSpec(memory_space=pl.ANY)],
            out_specs=pl.BlockSpec((tm, d), lambda i, dest: (i, 0)),
            scratch_shapes=[pltpu.VMEM((2, tm * ROW_SUB, LANES), U32), pltpu.VMEM((2, tm * ROW_SUB, LANES), U32),
                            pltpu.SemaphoreType.DMA((2,))],
        ),
        out_shape=jax.ShapeDtypeStruct((t, d), F32),
        compiler_params=_cparams(("arbitrary",)),
        name="moe_combine",
    )(dest, x1, info, g.reshape(1, d), ys)


def _nsa(proj, gate_cols, pe_k, w1_k, w2_k, pe_v, w1_v, w2_v, tab, b, s):
    del gate_cols
    g, dh = NSA_KV_HEADS, HEAD_DIM
    qw = NSA_HEADS * dh
    nc = s // CMP_STRIDE

    def blocks16(col0):
        a = proj[:, :, col0:col0 + g * dh].reshape(b, nc, CMP_STRIDE, g, dh)
        return a.transpose(0, 3, 1, 2, 4).reshape(b * g, nc, CMP_STRIDE * dh)

    kc = _compress(blocks16(qw), pe_k, w1_k, w2_k).reshape(b, g, nc, dh)
    vc = _compress(blocks16(qw + g * dh), pe_v, w1_v, w2_v).reshape(b, g, nc, dh)

    c_start = np.arange(nc)[None, :] * CMP_STRIDE
    n_sel = s // SEL_BLOCK
    sb = np.arange(n_sel)[:, None] * SEL_BLOCK
    overlap = jnp.asarray((c_start < sb + SEL_BLOCK) & (c_start + CMP_BLOCK > sb), BF16)
    o_c, member = _nsa_cmp(proj, kc, vc, tab, overlap, b, s)

    t = ATT_TILE
    nq = s // t
    nn = _n_near(t)
    nh = ATT_HEADS_PER_STEP
    kblk = qw // (nh // NSA_GROUP * dh)
    per = g // (nh // NSA_GROUP)
    bias_d = _bias_vecs(tab, t, nn + 1)
    o_s = _flash(proj, 0, proj, kblk + 2 * per, proj, kblk + 3 * per, bias_d, NSA_HEADS, NSA_GROUP, nh,
                 _causal_pairs(nq, nn), True, member=member, et=_block_onehot(s, SEL_BLOCK), name="nsa_selected")
    n_back = -(-WINDOW // t)
    bias_w = _bias_vecs(tab, t, n_back + 1, window=WINDOW)
    o_w = _flash(proj, 0, proj, kblk + 4 * per, proj, kblk + 5 * per, bias_w, NSA_HEADS, NSA_GROUP, nh,
                 _window_pairs(nq, n_back), False, name="nsa_window")
    return o_c, o_s, o_w


def _moba(proj, tab, b, s):
    member = _moba_gate(proj, b, s)
    t = ATT_TILE
    nn = _n_near(t)
    nh = ATT_HEADS_PER_STEP
    ng = MOBA_HEADS // nh
    bias_d = _bias_vecs(tab, t, nn + 1)
    return _flash(proj, 0, proj, ng, proj, 2 * ng, bias_d, MOBA_HEADS, 1, nh, _causal_pairs(s // t, nn),
                  True, member=member, et=_block_onehot(s, MOBA_BLOCK), name="moba_attn")


def _moe(x1, g_ffn, w_group, b_group, w_router, b_router, w_gate, w_up, w_down, g_final):
    t, d = x1.shape
    ng, _, epg = w_router.shape
    w_gr = jnp.concatenate([w_group, jnp.transpose(w_router, (1, 0, 2)).reshape(d, ng * epg),
                            jnp.zeros((d, LANES - ng - ng * epg), F32)], axis=1)
    b_gr = jnp.concatenate([b_group, b_router.reshape(-1), jnp.zeros((LANES - ng - ng * epg,), F32)]).reshape(1, LANES)
    h, info, cnt = _route(x1, g_ffn, w_gr, b_gr)
    n_e = ng * epg
    n_assign = t * EXPERT_TOPK
    assert n_assign % MOE_ROWS == 0
    n_blocks = n_assign // MOE_ROWS
    counts = cnt[0, :n_e].astype(jnp.int32)
    end = jnp.cumsum(counts)
    start = end - counts
    expert = info[:, 0:EXPERT_TOPK].astype(jnp.int32)
    rank = info[:, 4:4 + EXPERT_TOPK].astype(jnp.int32)
    e_ids = jnp.arange(n_e, dtype=jnp.int32)
    dest = (jnp.sum(jnp.where(expert[..., None] == e_ids, start, 0), axis=-1) + rank).reshape(-1)
    owns = counts > 0
    seq_of_expert = jnp.cumsum(owns.astype(jnp.int32)) - 1
    n_seq = jnp.sum(owns.astype(jnp.int32)).reshape(1)
    expert_of_seq = jnp.sum(jnp.where(owns[None, :] & (seq_of_expert[None, :] == e_ids[:, None]), e_ids[None, :], 0),
                            axis=1)
    first = start // MOE_ROWS
    last = jnp.where(owns, (end - 1) // MOE_ROWS, first - 1)
    pair_end = jnp.cumsum(last - first + 1)
    pair_start = pair_end - (last - first + 1)
    n_steps = n_blocks + n_e
    p_ids = jnp.arange(n_steps, dtype=jnp.int32)
    e_of_pair = jnp.minimum(jnp.sum((pair_end[None, :] <= p_ids[:, None]).astype(jnp.int32), axis=1), n_e - 1)
    pick = e_of_pair[:, None] == e_ids[None, :]
    lookup = lambda v: jnp.sum(jnp.where(pick, v[None, :], 0), axis=1)
    blk_of_pair = jnp.clip(lookup(first) + p_ids - lookup(pair_start), 0, n_blocks - 1)
    lo = jnp.clip(lookup(start) - blk_of_pair * MOE_ROWS, 0, MOE_ROWS)
    hi = jnp.clip(lookup(end) - blk_of_pair * MOE_ROWS, 0, MOE_ROWS)
    seq_of_pair = lookup(seq_of_expert)
    n_pairs = pair_end[-1:].astype(jnp.int32)
    xs = _dispatch(dest, h)
    ys = _experts(n_pairs, seq_of_pair, expert_of_seq, n_seq, blk_of_pair, lo, hi, xs, w_gate, w_up, w_down)
    return _combine(dest, x1, info, g_final, ys)


def kernel(x, rel_bias, norm_mix, w_in, cmp_pe_k, cmp_w1_k, cmp_w2_k, cmp_pe_v, cmp_w1_v, cmp_w2_v, w_up_nsa,
           w_up_moba, w_out, norm_ffn, w_group, b_group, w_router, b_router, w_exp_gate, w_exp_up, w_exp_down,
           final_norm):
    b, s, d = x.shape
    t = b * s
    depth = w_in.shape[0]
    tab_a = rel_bias[:, :NSA_HEADS]
    tab_b = rel_bias[:, NSA_HEADS:]
    a_cols = NSA_HEADS * HEAD_DIM + 6 * NSA_KV_HEADS * HEAD_DIM
    gate_cols = 3 * NSA_HEADS
    b_cols = 3 * MOBA_HEADS * HEAD_DIM
    xt = x.reshape(t, d)
    out = None
    for l in range(depth):
        h = _rmsnorm(xt, norm_mix[l], BF16)
        wt = jnp.swapaxes(w_in[l], 0, 1)
        b_col0 = a_cols + gate_cols
        proj_a = _matmul(h, wt, 0, a_cols, BF16, scaled_cols=NSA_HEADS * HEAD_DIM, col_scale=Q_SCALE,
                         w_transposed=True, name="in_proj_a").reshape(b, s, a_cols)
        gate_a = _matmul(h, wt, a_cols, LANES, F32, w_transposed=True, tn=LANES, name="in_proj_gate")
        proj_b = _matmul(h, wt, b_col0, b_cols, BF16, scaled_cols=MOBA_HEADS * HEAD_DIM, col_scale=Q_SCALE,
                         w_transposed=True, name="in_proj_b").reshape(b, s, b_cols)
        gm = _matmul(h, wt, b_col0 + b_cols, 2 * d, BF16, w_transposed=True, name="in_proj_gm")
        o_c, o_s, o_w = _nsa(proj_a, gate_cols, cmp_pe_k[l], cmp_w1_k[l], cmp_w2_k[l],
                             cmp_pe_v[l], cmp_w1_v[l], cmp_w2_v[l], tab_a, b, s)
        o_b = _moba(proj_b, tab_b, b, s)
        merged = _merge(o_c.reshape(t, -1), o_s.reshape(t, -1), o_w.reshape(t, -1), gate_a,
                        o_b.reshape(t, -1), gm, w_up_nsa[l], w_up_moba[l])
        x1 = _matmul(merged, w_out[l], 0, d, F32, res=xt, name="out_proj")
        assert l == depth - 1, "only the last layer's MoE is fused with the final norm"
        out = _moe(x1, norm_ffn[l], w_group[l], b_group[l], w_router[l], b_router[l],
                   w_exp_gate[l], w_exp_up[l], w_exp_down[l], final_norm)
    return out.reshape(b, s, d)
```

```python
import functools
import math

import numpy as np
import jax
import jax.numpy as jnp
from jax import lax
from jax.experimental import pallas as pl
from jax.experimental.pallas import tpu as pltpu

F32 = jnp.float32
BF16 = jnp.bfloat16

HEAD_DIM = 128
NSA_HEADS = 8
NSA_KV_HEADS = 2
NSA_GROUP = NSA_HEADS // NSA_KV_HEADS
CMP_BLOCK = 32
CMP_STRIDE = 16
SEL_BLOCK = 64
SEL_TOPN = 16
WINDOW = 512
FORCED_SCORE = 1e4
MOBA_HEADS = 8
MOBA_BLOCK = 256
MOBA_TOPK = 3
REL_BUCKETS = 32
REL_MAX_DIST = 128
N_GROUPS = 8
EXPERTS_PER_GROUP = 8
N_EXPERTS = N_GROUPS * EXPERTS_PER_GROUP
EXPERT_TOPK = 2
RMS_EPS = 1e-6

LANES = 128
ATT_TILE = 512
ATT_HEADS_PER_STEP = 8
ATT_VMEM_LIMIT = 56 * 1024 * 1024
MOE_ROWS = 128
MASK_BIG = 1e30
M_INIT = -3e38
LOG2E = math.log2(math.e)
Q_SCALE = HEAD_DIM ** -0.5 * LOG2E
VMEM_LIMIT = 48 * 1024 * 1024


def _cparams(sem, vmem=VMEM_LIMIT, flags=None):
    return pltpu.CompilerParams(dimension_semantics=sem, vmem_limit_bytes=vmem, flags=flags)


def _rmsnorm_body(x_ref, g_ref, o_ref):
    x = x_ref[...]
    ms = jnp.mean(x * x, axis=-1, keepdims=True)
    o_ref[...] = (x * lax.rsqrt(ms + RMS_EPS) * g_ref[...]).astype(o_ref.dtype)


def _rmsnorm(x, g, out_dtype, tm=512):
    t, d = x.shape
    return pl.pallas_call(
        _rmsnorm_body,
        grid=(t // tm,),
        in_specs=[pl.BlockSpec((tm, d), lambda i: (i, 0)),
                  pl.BlockSpec((1, d), lambda i: (0, 0))],
        out_specs=pl.BlockSpec((tm, d), lambda i: (i, 0)),
        out_shape=jax.ShapeDtypeStruct((t, d), out_dtype),
        compiler_params=_cparams(("parallel",)),
        name="rmsnorm",
    )(x, g.reshape(1, d))


def _mm_body(*refs, has_res, n_scaled, col_scale, w_transposed):
    if has_res:
        a_ref, w_ref, r_ref, o_ref = refs
    else:
        a_ref, w_ref, o_ref = refs
    w = w_ref[...].astype(BF16)
    if w_transposed:
        acc = lax.dot_general(a_ref[...], w, (((1,), (1,)), ((), ())), preferred_element_type=F32)
    else:
        acc = jnp.dot(a_ref[...], w, preferred_element_type=F32)
    if n_scaled:
        acc = acc * jnp.where(pl.program_id(1) < n_scaled, col_scale, 1.0)
    if has_res:
        acc = acc + r_ref[...]
    o_ref[...] = acc.astype(o_ref.dtype)


def _matmul(a, w, col0, ncols, out_dtype, res=None, scaled_cols=0, col_scale=1.0, w_transposed=False,
            tm=2048, tn=512, name="matmul"):
    t, k = a.shape
    tn = min(tn, ncols)
    tm = min(tm, t)
    assert ncols % tn == 0 and t % tm == 0 and scaled_cols % tn == 0
    if w_transposed:
        assert col0 % 8 == 0
        w_spec = pl.BlockSpec((pl.Element(tn), pl.Element(k)),
                              lambda i, j: (pl.multiple_of(col0 + j * tn, 8), 0))
    else:
        assert col0 % tn == 0
        off = col0 // tn
        w_spec = pl.BlockSpec((k, tn), lambda i, j: (0, j + off))
    in_specs = [pl.BlockSpec((tm, k), lambda i, j: (i, 0)), w_spec]
    args = [a, w]
    if res is not None:
        in_specs.append(pl.BlockSpec((tm, tn), lambda i, j: (i, j)))
        args.append(res)
    return pl.pallas_call(
        functools.partial(_mm_body, has_res=res is not None, n_scaled=scaled_cols // tn, col_scale=col_scale,
                          w_transposed=w_transposed),
        grid=(t // tm, ncols // tn),
        in_specs=in_specs,
        out_specs=pl.BlockSpec((tm, tn), lambda i, j: (i, j)),
        out_shape=jax.ShapeDtypeStruct((t, ncols), out_dtype),
        compiler_params=_cparams(("parallel", "parallel")),
        name=name,
    )(*args)


def _compress_body(u_ref, pe_ref, w1_ref, w2_ref, o_ref, *, nc):
    u = u_ref[...]
    w1 = w1_ref[...].astype(BF16)
    half = u.shape[1]
    a = jnp.dot(u, w1[:half], preferred_element_type=F32)
    b = jnp.dot(u, w1[half:], preferred_element_type=F32)
    peb = jnp.dot(pe_ref[...].astype(BF16), w1, preferred_element_type=F32)[0:1]
    pre = a + pltpu.roll(b, nc - 1, 0) + peb
    hid = jax.nn.gelu(pre)
    o_ref[...] = jnp.dot(hid.astype(BF16), w2_ref[...].astype(BF16),
                         preferred_element_type=F32).astype(o_ref.dtype)


def _compress(u, pe, w1, w2):
    bg, nc, kk = u.shape
    hid = w1.shape[1]
    dh = w2.shape[1]
    pe8 = jnp.broadcast_to(pe.reshape(1, -1), (16, pe.size))
    return pl.pallas_call(
        functools.partial(_compress_body, nc=nc),
        grid=(bg,),
        in_specs=[pl.BlockSpec((None, nc, kk), lambda i: (i, 0, 0)),
                  pl.BlockSpec((16, 2 * kk), lambda i: (0, 0)),
                  pl.BlockSpec((2 * kk, hid), lambda i: (0, 0)),
                  pl.BlockSpec((hid, dh), lambda i: (0, 0))],
        out_specs=pl.BlockSpec((None, nc, dh), lambda i: (i, 0, 0)),
        out_shape=jax.ShapeDtypeStruct((bg, nc, dh), BF16),
        compiler_params=_cparams(("parallel",)),
        name="nsa_compress",
    )(u, pe8, w1, w2)


def _split3(x):
    p1 = x.astype(BF16)
    r = x - p1.astype(F32)
    p2 = r.astype(BF16)
    p3 = (r - p2.astype(F32)).astype(BF16)
    return p1, p2, p3


def _rank_count(score, n_rows):
    groups = []
    for g0 in range(0, n_rows, 8):
        sg = score[g0:min(g0 + 8, n_rows), :]
        n_iota = g0 + lax.broadcasted_iota(jnp.int32, sg.shape, 0)
        cnt = jnp.zeros(sg.shape, F32)
        for m in range(n_rows):
            row = score[m:m + 1, :]
            if m < g0:
                beats = row >= sg
            elif m >= g0 + 8:
                beats = row > sg
            else:
                tie = jnp.where(n_iota > m, 1.0, 0.0)
                beats = jnp.where(row > sg, 1.0, jnp.where(row == sg, tie, 0.0)) > 0.5
            cnt = cnt + jnp.where(beats, 1.0, 0.0)
        groups.append(cnt)
    return jnp.concatenate(groups, axis=0) if len(groups) > 1 else groups[0]


def _nsa_cmp_body(q_ref, kc_ref, vc_ref, bias_ref, ov_ref, oc_ref, mem_ref, *, tq, nc, n_sel):
    t0 = pl.program_id(2) * tq
    kc = kc_ref[...]
    vc = vc_ref[...]
    t_idx = t0 + lax.broadcasted_iota(jnp.int32, (tq, nc), 0)
    c_idx = lax.broadcasted_iota(jnp.int32, (tq, nc), 1)
    dist = t_idx - (c_idx * CMP_STRIDE + (CMP_BLOCK - 1))
    n_k = REL_MAX_DIST // CMP_STRIDE
    kidx = jnp.where(dist < 0, n_k + 1, jnp.minimum(lax.shift_right_logical(dist, 4), n_k))
    assert CMP_STRIDE == 16
    psum = jnp.zeros((tq, nc), F32)
    for j in range(NSA_GROUP):
        hs = slice(j * HEAD_DIM, (j + 1) * HEAD_DIM)
        gt = bias_ref[j] * LOG2E
        bias = jnp.concatenate([jnp.take_along_axis(gt, kidx[:, c0:c0 + LANES], axis=1)
                                for c0 in range(0, nc, LANES)], axis=1)
        s = lax.dot_general(q_ref[:, hs], kc, (((1,), (1,)), ((), ())), preferred_element_type=F32) + bias
        m = jnp.max(s, axis=-1, keepdims=True)
        m = jnp.where(m > -0.5 * MASK_BIG, m, 0.0)
        p = jnp.exp2(s - m)
        d = jnp.sum(p, axis=-1, keepdims=True)
        p = p / jnp.where(d > 0, d, 1.0)
        oc_ref[:, hs] = jnp.dot(p.astype(BF16), vc, preferred_element_type=F32).astype(oc_ref.dtype)
        psum = psum + p
    ov = ov_ref[...]
    nt = (((1,), (1,)), ((), ()))
    p1, p2, p3 = _split3(psum)
    psel = (lax.dot_general(ov, p1, nt, preferred_element_type=F32)
            + lax.dot_general(ov, p2, nt, preferred_element_type=F32)
            + lax.dot_general(ov, p3, nt, preferred_element_type=F32))
    n_idx = lax.broadcasted_iota(jnp.int32, (n_sel, tq), 0)
    tt = t0 + lax.broadcasted_iota(jnp.int32, (n_sel, tq), 1)
    cur = tt // SEL_BLOCK
    forced = jnp.where(n_idx == 0, 1.0, jnp.where(n_idx == cur, 1.0, jnp.where(n_idx == cur - 1, 1.0, 0.0)))
    score = jnp.where(forced > 0.5, FORCED_SCORE, jnp.where(n_idx * SEL_BLOCK <= tt, psel, -1.0))
    cnt = _rank_count(score, n_sel)
    member = jnp.where(cnt < float(min(SEL_TOPN, n_sel)), 1.0, 0.0)
    if n_sel < LANES:
        member = jnp.concatenate([member, jnp.zeros((LANES - n_sel, tq), F32)], axis=0)
    mem_ref[...] = member.T.astype(mem_ref.dtype)


def _nsa_cmp(proj, kc, vc, tab, overlap, b, s, tq=256):
    g = NSA_KV_HEADS
    nc = kc.shape[2]
    n_sel = s // SEL_BLOCK
    gw = NSA_GROUP * HEAD_DIM
    assert tq % CMP_STRIDE == 0 and REL_MAX_DIST % CMP_STRIDE == 0
    n_k = REL_MAX_DIST // CMP_STRIDE
    rho = (np.arange(tq)[:, None] - (CMP_BLOCK - 1)) % CMP_STRIDE
    dd = np.concatenate([rho + CMP_STRIDE * np.arange(n_k)[None, :], np.full((tq, 1), REL_MAX_DIST)], axis=1)
    gtab = jnp.transpose(_bias_of_dist(tab, dd), (2, 0, 1)).astype(F32)
    bias_c = jnp.concatenate([gtab, jnp.full(gtab.shape[:2] + (1,), -MASK_BIG, F32),
                              jnp.zeros(gtab.shape[:2] + (LANES - n_k - 2,), F32)], axis=2)
    body = functools.partial(_nsa_cmp_body, tq=tq, nc=nc, n_sel=n_sel)
    return pl.pallas_call(
        body,
        grid=(b, g, s // tq),
        in_specs=[pl.BlockSpec((None, tq, gw), lambda bi, gi, i: (bi, i, gi)),
                  pl.BlockSpec((None, None, nc, HEAD_DIM), lambda bi, gi, i: (bi, gi, 0, 0)),
                  pl.BlockSpec((None, None, nc, HEAD_DIM), lambda bi, gi, i: (bi, gi, 0, 0)),
                  pl.BlockSpec((NSA_GROUP, tq, LANES), lambda bi, gi, i: (gi, 0, 0)),
                  pl.BlockSpec((n_sel, nc), lambda bi, gi, i: (0, 0))],
        out_specs=[pl.BlockSpec((None, tq, gw), lambda bi, gi, i: (bi, i, gi)),
                   pl.BlockSpec((None, None, tq, LANES), lambda bi, gi, i: (bi, gi, i, 0))],
        out_shape=[jax.ShapeDtypeStruct((b, s, NSA_HEADS * HEAD_DIM), BF16),
                   jax.ShapeDtypeStruct((b, g, s, LANES), BF16)],
        compiler_params=_cparams(("parallel", "parallel", "parallel")),
        name="nsa_cmp_select",
    )(proj, kc, vc, bias_c, overlap)


def _moba_gate_body(q_ref, k_ref, mem_ref, *, s, nblk):
    k = k_ref[...].astype(F32)
    kmean = jnp.mean(k.reshape(nblk, MOBA_BLOCK, HEAD_DIM), axis=1)
    k1 = kmean.astype(BF16)
    k2 = (kmean - k1.astype(F32)).astype(BF16)
    q = q_ref[...]
    nt = (((1,), (1,)), ((), ()))
    gate = (lax.dot_general(k1, q, nt, preferred_element_type=F32)
            + lax.dot_general(k2, q, nt, preferred_element_type=F32))
    n_idx = lax.broadcasted_iota(jnp.int32, (nblk, s), 0)
    own = lax.broadcasted_iota(jnp.int32, (nblk, s), 1) // MOBA_BLOCK
    past = n_idx < own
    score = jnp.where(past, gate, -MASK_BIG)
    cnt = _rank_count(score, nblk)
    n_top = max(1, min(MOBA_TOPK, nblk - 1))
    sel = jnp.where(past, jnp.where(cnt < float(n_top), 1.0, 0.0), 0.0)
    member = jnp.where(n_idx == own, 1.0, sel)
    member = jnp.concatenate([member, jnp.zeros((LANES - nblk, s), F32)], axis=0)
    mem_ref[...] = member.T.astype(mem_ref.dtype)


def _moba_gate(proj, b, s):
    h = MOBA_HEADS
    nblk = s // MOBA_BLOCK
    return pl.pallas_call(
        functools.partial(_moba_gate_body, s=s, nblk=nblk),
        grid=(b, h),
        in_specs=[pl.BlockSpec((None, s, HEAD_DIM), lambda bi, hi: (bi, 0, hi)),
                  pl.BlockSpec((None, s, HEAD_DIM), lambda bi, hi: (bi, 0, h + hi))],
        out_specs=pl.BlockSpec((None, None, s, LANES), lambda bi, hi: (bi, hi, 0, 0)),
        out_shape=jax.ShapeDtypeStruct((b, h, s, LANES), BF16),
        compiler_params=_cparams(("parallel", "parallel")),
        name="moba_gate",
    )(proj, proj)


def _flash_body(qi_ref, ki_ref, bo_ref, fl_ref, *refs, nh, ratio, nm, n_near, has_far):
    if nm:
        (q_ref, k_ref, v_ref, bvec_ref, mem_ref, et_ref, o_ref,
         m_ref, l_ref, acc_ref, sh_ref, al_ref, bias_ref, s_ref, p_ref) = refs
    else:
        q_ref, k_ref, v_ref, bvec_ref, o_ref, m_ref, l_ref, acc_ref, sh_ref, al_ref, bias_ref, s_ref, p_ref = refs
    del qi_ref, ki_ref
    p = pl.program_id(2)
    flag = fl_ref[p]
    bo = bo_ref[p]
    t = q_ref.shape[0]
    rows = 64

    @pl.when(p == 0)
    def _():
        for h in range(nh):
            for o in range(n_near):
                vec = bvec_ref[h, o][0:1, :] * LOG2E
                for rc in range(t // rows):
                    x = pltpu.roll(jnp.broadcast_to(vec, (rows, 2 * t)), rc * rows, 1, stride=1, stride_axis=0)
                    bias_ref[h, o, rc * rows:(rc + 1) * rows, :] = x[:, :t]

    @pl.when((flag & 1) != 0)
    def _():
        m_ref[...] = jnp.full(m_ref.shape, M_INIT, F32)
        l_ref[...] = jnp.zeros(l_ref.shape, F32)
        acc_ref[...] = jnp.zeros(acc_ref.shape, F32)

    nt = (((1,), (1,)), ((), ()))
    reps = t // LANES

    def step(near):
        def pass1(h):
            hs = slice(h * HEAD_DIM, (h + 1) * HEAD_DIM)
            kv = h // ratio
            ks = slice(kv * HEAD_DIM, (kv + 1) * HEAD_DIM)
            q = q_ref[:, hs]
            k = k_ref[:, ks]
            if nm:
                mneg = mem_ref[h // (nh // nm)] - 1.0
                q = jnp.concatenate([q, mneg.astype(BF16)], axis=1)
                k = jnp.concatenate([k, et_ref[...]], axis=1)
            sc = lax.dot_general(q, k, nt, preferred_element_type=F32)
            m_prev = m_ref[h]
            if near:
                sc = sc + bias_ref[h, bo]
                m_new = jnp.maximum(m_prev, jnp.max(sc, axis=-1, keepdims=True))
                sh_ref[h] = m_new
            else:
                cfar = bvec_ref[h, n_near][0:1, 0:LANES] * LOG2E
                m_new = jnp.maximum(m_prev, jnp.max(sc, axis=-1, keepdims=True) + cfar)
                sh_ref[h] = m_new - cfar
            s_ref[h] = sc
            al_ref[h] = jnp.exp2(m_prev - m_new)
            m_ref[h] = m_new

        def pass2(h):
            ks = slice(h // ratio * HEAD_DIM, (h // ratio + 1) * HEAD_DIM)
            for rc in range(t // rows):
                rs = slice(rc * rows, (rc + 1) * rows)
                pm = jnp.exp2(s_ref[h, rs, :] - jnp.tile(sh_ref[h, rs, :], (1, reps)))
                l_ref[h, rs, :] = al_ref[h, rs, :] * l_ref[h, rs, :] + jnp.sum(pm, axis=-1, keepdims=True)
                p_ref[h, rs, :] = pm.astype(BF16)
            acc_ref[h] = al_ref[h] * acc_ref[h] + jnp.dot(p_ref[h], v_ref[:, ks], preferred_element_type=F32)

        for h in range(nh):
            pass1(h)
        for h in range(nh):
            pass2(h)

    if has_far:
        pl.when(bo < n_near)(lambda: step(True))
        pl.when(bo >= n_near)(lambda: step(False))
    else:
        step(True)

    @pl.when((flag & 2) != 0)
    def _():
        for h in range(nh):
            l = l_ref[h]
            o_ref[:, h * HEAD_DIM:(h + 1) * HEAD_DIM] = (acc_ref[h] / jnp.where(l > 0, l, 1.0)).astype(o_ref.dtype)


def _flash(q_arr, q_off, k_arr, k_off, v_arr, v_off, bias, n_heads, ratio, nh, pairs, has_far, member=None,
           et=None, name="flash"):
    b, s, _ = q_arr.shape
    t = ATT_TILE
    nkv = nh // ratio
    ng = n_heads // nh
    qi = jnp.asarray([p[0] for p in pairs], jnp.int32)
    ki = jnp.asarray([p[1] for p in pairs], jnp.int32)
    bo = jnp.asarray([p[2] for p in pairs], jnp.int32)
    fl = jnp.asarray([p[3] for p in pairs], jnp.int32)
    nb = bias.shape[1]
    nm = 0
    in_specs = [
        pl.BlockSpec((None, t, nh * HEAD_DIM), lambda bi, gi, p, qi, ki, bo, fl: (bi, qi[p], q_off + gi)),
        pl.BlockSpec((None, t, nkv * HEAD_DIM), lambda bi, gi, p, qi, ki, bo, fl: (bi, ki[p], k_off + gi)),
        pl.BlockSpec((None, t, nkv * HEAD_DIM), lambda bi, gi, p, qi, ki, bo, fl: (bi, ki[p], v_off + gi)),
        pl.BlockSpec((nh, nb, 8, 2 * t), lambda bi, gi, p, qi, ki, bo, fl: (gi, 0, 0, 0)),
    ]
    args = [q_arr, k_arr, v_arr, bias]
    if member is not None:
        nm = member.shape[1] // ng
        in_specs += [
            pl.BlockSpec((None, nm, t, LANES), lambda bi, gi, p, qi, ki, bo, fl: (bi, gi, qi[p], 0)),
            pl.BlockSpec((t, LANES), lambda bi, gi, p, qi, ki, bo, fl: (ki[p], 0)),
        ]
        args += [member, et]
    n_near = nb - 1 if has_far else nb
    body = functools.partial(_flash_body, nh=nh, ratio=ratio, nm=nm, n_near=n_near, has_far=has_far)
    return pl.pallas_call(
        body,
        grid_spec=pltpu.PrefetchScalarGridSpec(
            num_scalar_prefetch=4,
            grid=(b, ng, len(pairs)),
            in_specs=in_specs,
            out_specs=pl.BlockSpec((None, t, nh * HEAD_DIM), lambda bi, gi, p, qi, ki, bo, fl: (bi, qi[p], gi)),
            scratch_shapes=[pltpu.VMEM((nh, t, LANES), F32)] * 5 + [pltpu.VMEM((nh, n_near, t, t), F32),
                                                                    pltpu.VMEM((nh, t, t), F32),
                                                                    pltpu.VMEM((nh, t, t), BF16)],
        ),
        out_shape=jax.ShapeDtypeStruct((b, s, n_heads * HEAD_DIM), BF16),
        compiler_params=_cparams(("parallel", "parallel", "arbitrary"), vmem=ATT_VMEM_LIMIT),
        name=name,
    )(qi, ki, bo, fl, *args)


def _rel_bucket(dist):
    n = jnp.maximum(jnp.asarray(dist, jnp.int32), 0)
    max_exact = REL_BUCKETS // 2
    nf = jnp.maximum(n, 1).astype(jnp.float32)
    large = max_exact + (jnp.log(nf / max_exact) / math.log(REL_MAX_DIST / max_exact)
                         * (REL_BUCKETS - max_exact)).astype(jnp.int32)
    return jnp.where(n < max_exact, n, jnp.minimum(large, REL_BUCKETS - 1))


def _bias_of_dist(tab, dist):
    hit = _rel_bucket(dist)[..., None, None] == jnp.arange(REL_BUCKETS)[:, None]
    return jnp.sum(jnp.where(hit, tab, 0.0), axis=-2)


def _n_near(t):
    return -(-(REL_MAX_DIST - 1 + t) // t)


def _bias_vecs(tab, t, n_off, window=None):
    k = np.arange(2 * t)[None, :]
    dist = np.arange(n_off)[:, None] * t + np.where(k < t, -k, 2 * t - k)
    ok = dist >= 0
    if window is not None:
        ok &= dist < window
    bias = jnp.where(jnp.asarray(ok)[..., None], _bias_of_dist(tab, dist), -MASK_BIG)
    bias = jnp.transpose(bias, (2, 0, 1)).astype(F32)
    return jnp.broadcast_to(bias[:, :, None, :], (bias.shape[0], n_off, 8, 2 * t))


def _causal_pairs(nq, n_near):
    pairs = []
    for qi in range(nq):
        for ki in range(qi + 1):
            pairs.append((qi, ki, min(qi - ki, n_near), (1 if ki == 0 else 0) | (2 if ki == qi else 0)))
    return pairs


def _window_pairs(nq, n_back):
    pairs = []
    for qi in range(nq):
        lo = max(0, qi - n_back)
        for ki in range(lo, qi + 1):
            pairs.append((qi, ki, qi - ki, (1 if ki == lo else 0) | (2 if ki == qi else 0)))
    return pairs


def _block_onehot(s, blk):
    return jnp.asarray(np.where(np.arange(s)[:, None] // blk == np.arange(LANES)[None, :], MASK_BIG, 0.0), BF16)


def _merge_body(oc_ref, os_ref, ow_ref, gl_ref, ob_ref, gma_ref, gmb_ref, wa_ref, wb_ref, o_ref, oa_ref):
    @pl.when(pl.program_id(1) == 0)
    def _():
        gates = jax.nn.sigmoid(gl_ref[...])
        for h in range(NSA_HEADS):
            hs = slice(h * HEAD_DIM, (h + 1) * HEAD_DIM)
            mix = (gates[:, 3 * h:3 * h + 1] * oc_ref[:, hs].astype(F32)
                   + gates[:, 3 * h + 1:3 * h + 2] * os_ref[:, hs].astype(F32)
                   + gates[:, 3 * h + 2:3 * h + 3] * ow_ref[:, hs].astype(F32))
            oa_ref[:, hs] = mix.astype(BF16)

    ya = jnp.dot(oa_ref[...], wa_ref[...].astype(BF16), preferred_element_type=F32)
    yb = jnp.dot(ob_ref[...], wb_ref[...].astype(BF16), preferred_element_type=F32)
    ga = jax.nn.sigmoid(gma_ref[...].astype(F32))
    gb = jax.nn.sigmoid(gmb_ref[...].astype(F32))
    o_ref[...] = (ga * ya + gb * yb).astype(o_ref.dtype)


def _merge(o_c, o_s, o_w, gate_logits, o_b, gm, w_up_a, w_up_b, tm=1024, tn=512):
    t, ka = o_c.shape
    kb = o_b.shape[1]
    d = w_up_a.shape[1]
    tm = min(tm, t)
    nj = d // tn
    row = lambda i, j: (i, 0)
    return pl.pallas_call(
        _merge_body,
        grid=(t // tm, nj),
        in_specs=[pl.BlockSpec((tm, ka), row), pl.BlockSpec((tm, ka), row), pl.BlockSpec((tm, ka), row),
                  pl.BlockSpec((tm, LANES), row), pl.BlockSpec((tm, kb), row),
                  pl.BlockSpec((tm, tn), lambda i, j: (i, j)),
                  pl.BlockSpec((tm, tn), lambda i, j: (i, j + nj)),
                  pl.BlockSpec((ka, tn), lambda i, j: (0, j)),
                  pl.BlockSpec((kb, tn), lambda i, j: (0, j))],
        out_specs=pl.BlockSpec((tm, tn), lambda i, j: (i, j)),
        out_shape=jax.ShapeDtypeStruct((t, d), BF16),
        scratch_shapes=[pltpu.VMEM((tm, ka), BF16)],
        compiler_params=_cparams(("parallel", "arbitrary")),
        name="merge_up",
    )(o_c, o_s, o_w, gate_logits, o_b, gm, gm, w_up_a, w_up_b)


def _route_body(x_ref, g_ref, w_ref, b_ref, h_ref, info_ref, cnt_ref, carry_ref, *, tm):
    @pl.when(pl.program_id(0) == 0)
    def _():
        carry_ref[...] = jnp.zeros(carry_ref.shape, F32)

    x = x_ref[...]
    ms = jnp.mean(x * x, axis=-1, keepdims=True)
    h = x * lax.rsqrt(ms + RMS_EPS) * g_ref[...]
    _store_packed(h_ref, h, tm)
    w = w_ref[...]
    h1 = h.astype(BF16)
    h2 = (h - h1.astype(F32)).astype(BF16)
    w1 = w.astype(BF16)
    w2 = (w - w1.astype(F32)).astype(BF16)
    logits = (jnp.dot(h1, w1, preferred_element_type=F32) + jnp.dot(h1, w2, preferred_element_type=F32)
              + jnp.dot(h2, w1, preferred_element_type=F32)) + b_ref[...]
    lane = lax.broadcasted_iota(jnp.int32, (tm, LANES), 1)
    lanef = lane.astype(F32)

    is_g = lane < N_GROUPS
    gl = jnp.where(is_g, logits, -MASK_BIG)
    ge = jnp.where(is_g, jnp.exp(gl - jnp.max(gl, axis=-1, keepdims=True)), 0.0)
    gp = ge / jnp.sum(ge, axis=-1, keepdims=True)
    g_val = jnp.max(gp, axis=-1, keepdims=True)
    g_idx = jnp.min(jnp.where(gp == g_val, lanef, float(LANES)), axis=-1, keepdims=True)

    lane_grp = ((lane - N_GROUPS) // EXPERTS_PER_GROUP).astype(F32)
    in_e = jnp.where(lane >= N_GROUPS, jnp.where(lane < N_GROUPS + N_EXPERTS, 1.0, 0.0), 0.0)
    is_e = jnp.where(lane_grp == g_idx, in_e, 0.0) > 0.5
    el = jnp.where(is_e, logits, -MASK_BIG)
    ee = jnp.where(is_e, jnp.exp(el - jnp.max(el, axis=-1, keepdims=True)), 0.0)
    ep = jnp.where(is_e, ee / jnp.sum(ee, axis=-1, keepdims=True), -1.0)
    v1 = jnp.max(ep, axis=-1, keepdims=True)
    l1 = jnp.min(jnp.where(ep == v1, lanef, float(LANES)), axis=-1, keepdims=True)
    ep2 = jnp.where(lanef == l1, -1.0, ep)
    v2 = jnp.max(ep2, axis=-1, keepdims=True)
    l2 = jnp.min(jnp.where(ep2 == v2, lanef, float(LANES)), axis=-1, keepdims=True)
    vs = v1 + v2
    wt1 = g_val * v1 / vs
    wt2 = g_val * v2 / vs
    e1 = l1 - float(N_GROUPS)
    e2 = l2 - float(N_GROUPS)

    oh = jnp.where(lanef == e1, 1.0, jnp.where(lanef == e2, 1.0, 0.0))
    r_i = lax.broadcasted_iota(jnp.int32, (tm, tm), 0)
    c_i = lax.broadcasted_iota(jnp.int32, (tm, tm), 1)
    tri = jnp.where(r_i > c_i, 1.0, 0.0).astype(BF16)
    base = jnp.dot(tri, oh.astype(BF16), preferred_element_type=F32) + carry_ref[...]
    r1 = jnp.sum(jnp.where(lanef == e1, base, 0.0), axis=-1, keepdims=True)
    r2 = jnp.sum(jnp.where(lanef == e2, base, 0.0), axis=-1, keepdims=True)
    carry_ref[...] = carry_ref[...] + jnp.sum(oh, axis=0, keepdims=True)
    cnt_ref[...] = jnp.broadcast_to(carry_ref[...], cnt_ref.shape)
    info = jnp.where(lane == 0, e1, jnp.where(lane == 1, e2, jnp.where(lane == 2, wt1, jnp.where(
        lane == 3, wt2, jnp.where(lane == 4, r1, jnp.where(lane == 5, r2, 0.0))))))
    info_ref[...] = info


def _route(x1, g, w_gr, b_gr, tm=512):
    t, d = x1.shape
    return pl.pallas_call(
        functools.partial(_route_body, tm=tm),
        grid=(t // tm,),
        in_specs=[pl.BlockSpec((tm, d), lambda i: (i, 0)),
                  pl.BlockSpec((1, d), lambda i: (0, 0)),
                  pl.BlockSpec((d, LANES), lambda i: (0, 0)),
                  pl.BlockSpec((1, LANES), lambda i: (0, 0))],
        out_specs=[pl.BlockSpec((tm * ROW_SUB, LANES), lambda i: (i, 0)),
                   pl.BlockSpec((tm, LANES), lambda i: (i, 0)),
                   pl.BlockSpec((8, LANES), lambda i: (0, 0))],
        out_shape=[jax.ShapeDtypeStruct((t * ROW_SUB, LANES), jnp.uint32),
                   jax.ShapeDtypeStruct((t, LANES), F32),
                   jax.ShapeDtypeStruct((8, LANES), F32)],
        scratch_shapes=[pltpu.VMEM((1, LANES), F32)],
        compiler_params=_cparams(("arbitrary",)),
        name="moe_route",
    )(x1, g.reshape(1, d), w_gr, b_gr)


ROW_SUB = 8
U32 = jnp.uint32


def _pack_pairs(lo, hi):
    lo_b = lax.bitcast_convert_type(lo.astype(BF16).astype(F32), U32)
    hi_b = lax.bitcast_convert_type(hi.astype(BF16).astype(F32), U32)
    return lax.shift_right_logical(lo_b, U32(16)) | (hi_b & U32(0xFFFF0000))


def _unpack_pairs(w):
    lo = lax.bitcast_convert_type(lax.shift_left(w, U32(16)), F32)
    hi = lax.bitcast_convert_type(w & U32(0xFFFF0000), F32)
    return lo, hi


def _store_packed(ref, y, n):
    half = y.shape[1] // 2
    for s in range(ROW_SUB):
        cs = slice(s * LANES, (s + 1) * LANES)
        ref[pl.ds(s, n, stride=ROW_SUB), :] = _pack_pairs(y[:, cs], y[:, half + s * LANES:half + (s + 1) * LANES])


def _load_packed(ref, n):
    los, his = [], []
    for s in range(ROW_SUB):
        lo, hi = _unpack_pairs(ref[pl.ds(s, n, stride=ROW_SUB), :])
        los.append(lo)
        his.append(hi)
    return jnp.concatenate(los + his, axis=1)


def _row_copy(src_ref, src_row, dst_ref, dst_row, sem):
    return pltpu.make_async_copy(src_ref.at[pl.ds(pl.multiple_of(src_row * ROW_SUB, ROW_SUB), ROW_SUB)],
                                 dst_ref.at[pl.ds(pl.multiple_of(dst_row * ROW_SUB, ROW_SUB), ROW_SUB)], sem)


def _dispatch_body(dest_ref, h_ref, xs_ref, sem, *, tm):
    base = pl.program_id(0) * tm

    def issue(r, c):
        for k in range(EXPERT_TOPK):
            _row_copy(h_ref, r, xs_ref, dest_ref[EXPERT_TOPK * (base + r) + k], sem).start(priority=k % 2)
        return c

    lax.fori_loop(0, tm, issue, 0, unroll=4)

    def drain(r, c):
        for k in range(EXPERT_TOPK):
            _row_copy(h_ref, r, xs_ref, dest_ref[EXPERT_TOPK * (base + r) + k], sem).wait()
        return c

    lax.fori_loop(0, tm, drain, 0, unroll=4)


def _dispatch(dest, hp, tm=512):
    t = hp.shape[0] // ROW_SUB
    n_rows = dest.shape[0]
    return pl.pallas_call(
        functools.partial(_dispatch_body, tm=tm),
        grid_spec=pltpu.PrefetchScalarGridSpec(
            num_scalar_prefetch=1,
            grid=(t // tm,),
            in_specs=[pl.BlockSpec((tm * ROW_SUB, LANES), lambda i, dest: (i, 0))],
            out_specs=pl.BlockSpec(memory_space=pl.ANY),
            scratch_shapes=[pltpu.SemaphoreType.DMA(())],
        ),
        out_shape=jax.ShapeDtypeStruct((n_rows * ROW_SUB, LANES), U32),
        compiler_params=_cparams(("arbitrary",)),
        name="moe_dispatch",
    )(dest, hp)


def _expert_body(nu_ref, sq_ref, es_ref, ns_ref, blk_ref, lo_ref, hi_ref, x_ref, wg_hbm, wu_hbm, wd_hbm, y_ref,
                 wg_b, wu_b, wd_b, wg_s, wu_s, wd_s, y_acc, sem):
    i = pl.program_id(0)
    nu = nu_ref[0]
    ns = ns_ref[0]

    def weight_copies(seq, slot):
        e = es_ref[seq]
        return (pltpu.make_async_copy(wg_hbm.at[e], wg_b.at[slot], sem.at[slot]),
                pltpu.make_async_copy(wu_hbm.at[e], wu_b.at[slot], sem.at[slot]),
                pltpu.make_async_copy(wd_hbm.at[e], wd_b.at[slot], sem.at[slot]))

    def start_weights(seq, slot):
        for c in weight_copies(seq, slot):
            c.start()

    @pl.when(i == 0)
    def _():
        y_acc[...] = jnp.zeros(y_acc.shape, F32)
        start_weights(0, 0)

        @pl.when(ns > 1)
        def _():
            start_weights(1, 1)

    s = sq_ref[i]
    first = (i == 0) | (s != sq_ref[jnp.maximum(i - 1, 0)])

    @pl.when((i < nu) & first)
    def _():
        slot = lax.rem(s, 2)
        for c in weight_copies(s, slot):
            c.wait()
        def cast_rows(dst, src, chunk):
            def body(c, carry):
                rs = pl.ds(pl.multiple_of(c * chunk, chunk), chunk)
                dst[rs, :] = src[slot, rs, :].astype(BF16)
                return carry
            lax.fori_loop(0, dst.shape[0] // chunk, body, 0)

        cast_rows(wg_s, wg_b, 256)
        cast_rows(wu_s, wu_b, 256)
        cast_rows(wd_s, wd_b, 64)

        @pl.when(s + 2 < ns)
        def _():
            start_weights(s + 2, slot)

    @pl.when(i < nu)
    def _():
        x = _load_packed(x_ref, MOE_ROWS).astype(BF16)
        g = jnp.dot(x, wg_s[...], preferred_element_type=F32)
        u = jnp.dot(x, wu_s[...], preferred_element_type=F32)
        mid = (jax.nn.silu(g) * u).astype(BF16)
        y = jnp.dot(mid, wd_s[...], preferred_element_type=F32)
        row = lax.broadcasted_iota(jnp.int32, (MOE_ROWS, 1), 0)
        mine = (row >= lo_ref[i]) & (row < hi_ref[i])
        new_block = (i == 0) | (blk_ref[i] != blk_ref[jnp.maximum(i - 1, 0)])
        y_acc[...] = jnp.where(mine, y, jnp.where(new_block, 0.0, y_acc[...]))
        _store_packed(y_ref, y_acc[...], MOE_ROWS)


def _experts(n_pairs, seq_of_pair, expert_of_seq, n_seq, blk_of_pair, lo, hi, xs, w_gate, w_up, w_down):
    n_steps = seq_of_pair.shape[0]
    _, d, ff = w_gate.shape
    assert d == 2 * ROW_SUB * LANES, "a packed row must be exactly one (8,128) tile"
    blk = lambda i, nu, sq, es, ns, bk, lo, hi: (bk[jnp.minimum(i, nu[0] - 1)], 0)
    hbm = pl.BlockSpec(memory_space=pl.ANY)
    return pl.pallas_call(
        _expert_body,
        grid_spec=pltpu.PrefetchScalarGridSpec(
            num_scalar_prefetch=7,
            grid=(n_steps,),
            in_specs=[pl.BlockSpec((MOE_ROWS * ROW_SUB, LANES), blk), hbm, hbm, hbm],
            out_specs=pl.BlockSpec((MOE_ROWS * ROW_SUB, LANES), blk),
            scratch_shapes=[pltpu.VMEM((2, d, ff), F32), pltpu.VMEM((2, d, ff), F32), pltpu.VMEM((2, ff, d), F32),
                            pltpu.VMEM((d, ff), BF16), pltpu.VMEM((d, ff), BF16), pltpu.VMEM((ff, d), BF16),
                            pltpu.VMEM((MOE_ROWS, d), F32), pltpu.SemaphoreType.DMA((2,))],
        ),
        out_shape=jax.ShapeDtypeStruct(xs.shape, U32),
        compiler_params=_cparams(("arbitrary",)),
        name="moe_experts",
    )(n_pairs, seq_of_pair, expert_of_seq, n_seq, blk_of_pair, lo, hi, xs, w_gate, w_up, w_down)


def _combine_body(dest_ref, x_ref, info_ref, g_ref, ys_ref, o_ref, buf0, buf1, sem, *, tm):
    i = pl.program_id(0)
    slot = lax.rem(i, 2)
    bufs = (buf0, buf1)

    def gather(tile, sl, wait):
        def body(r, c):
            for k in range(EXPERT_TOPK):
                cp = _row_copy(ys_ref, dest_ref[EXPERT_TOPK * (tile * tm + r) + k], bufs[k].at[sl], r, sem.at[sl])
                if wait:
                    cp.wait()
                else:
                    cp.start(priority=k % 2)
            return c
        lax.fori_loop(0, tm, body, 0, unroll=4)

    @pl.when(i == 0)
    def _():
        gather(0, 0, False)

    @pl.when(i + 1 < pl.num_programs(0))
    def _():
        gather(i + 1, 1 - slot, False)

    gather(i, slot, True)
    info = info_ref[...]
    y = x_ref[...] + (info[:, 2:3] * _load_packed(buf0.at[slot], tm) + info[:, 3:4] * _load_packed(buf1.at[slot], tm))
    ms = jnp.mean(y * y, axis=-1, keepdims=True)
    o_ref[...] = y * lax.rsqrt(ms + RMS_EPS) * g_ref[...]


def _combine(dest, x1, info, g, ys, tm=256):
    t, d = x1.shape
    return pl.pallas_call(
        functools.partial(_combine_body, tm=tm),
        grid_spec=pltpu.PrefetchScalarGridSpec(
            num_scalar_prefetch=1,
            grid=(t // tm,),
            in_specs=[pl.BlockSpec((tm, d), lambda i, dest: (i, 0)),
                      pl.BlockSpec((tm, LANES), lambda i, dest: (i, 0)),
                      pl.BlockSpec((1, d), lambda i, dest: (0, 0)),
                      pl.BlockSpec(memory_space=pl.ANY)],
            out_specs=pl.BlockSpec((tm, d), lambda i, dest: (i, 0)),
            scratch_shapes=[pltpu.VMEM((2, tm * ROW_SUB, LANES), U32), pltpu.VMEM((2, tm * ROW_SUB, LANES), U32),
                            pltpu.SemaphoreType.DMA((2,))],
        ),
        out_shape=jax.ShapeDtypeStruct((t, d), F32),
        compiler_params=_cparams(("arbitrary",)),
        name="moe_combine",
    )(dest, x1, info, g.reshape(1, d), ys)


def _nsa(proj, gate_cols, pe_k, w1_k, w2_k, pe_v, w1_v, w2_v, tab, b, s):
    del gate_cols
    g, dh = NSA_KV_HEADS, HEAD_DIM
    qw = NSA_HEADS * dh
    nc = s // CMP_STRIDE

    def blocks16(col0):
        a = proj[:, :, col0:col0 + g * dh].reshape(b, nc, CMP_STRIDE, g, dh)
        return a.transpose(0, 3, 1, 2, 4).reshape(b * g, nc, CMP_STRIDE * dh)

    kc = _compress(blocks16(qw), pe_k, w1_k, w2_k).reshape(b, g, nc, dh)
    vc = _compress(blocks16(qw + g * dh), pe_v, w1_v, w2_v).reshape(b, g, nc, dh)

    c_start = np.arange(nc)[None, :] * CMP_STRIDE
    n_sel = s // SEL_BLOCK
    sb = np.arange(n_sel)[:, None] * SEL_BLOCK
    overlap = jnp.asarray((c_start < sb + SEL_BLOCK) & (c_start + CMP_BLOCK > sb), BF16)
    o_c, member = _nsa_cmp(proj, kc, vc, tab, overlap, b, s)

    t = ATT_TILE
    nq = s // t
    nn = _n_near(t)
    nh = ATT_HEADS_PER_STEP
    kblk = qw // (nh // NSA_GROUP * dh)
    per = g // (nh // NSA_GROUP)
    bias_d = _bias_vecs(tab, t, nn + 1)
    o_s = _flash(proj, 0, proj, kblk + 2 * per, proj, kblk + 3 * per, bias_d, NSA_HEADS, NSA_GROUP, nh,
                 _causal_pairs(nq, nn), True, member=member, et=_block_onehot(s, SEL_BLOCK), name="nsa_selected")
    n_back = -(-WINDOW // t)
    bias_w = _bias_vecs(tab, t, n_back + 1, window=WINDOW)
    o_w = _flash(proj, 0, proj, kblk + 4 * per, proj, kblk + 5 * per, bias_w, NSA_HEADS, NSA_GROUP, nh,
                 _window_pairs(nq, n_back), False, name="nsa_window")
    return o_c, o_s, o_w


def _moba(proj, tab, b, s):
    member = _moba_gate(proj, b, s)
    t = ATT_TILE
    nn = _n_near(t)
    nh = ATT_HEADS_PER_STEP
    ng = MOBA_HEADS // nh
    bias_d = _bias_vecs(tab, t, nn + 1)
    return _flash(proj, 0, proj, ng, proj, 2 * ng, bias_d, MOBA_HEADS, 1, nh, _causal_pairs(s // t, nn),
                  True, member=member, et=_block_onehot(s, MOBA_BLOCK), name="moba_attn")


def _moe(x1, g_ffn, w_group, b_group, w_router, b_router, w_gate, w_up, w_down, g_final):
    t, d = x1.shape
    ng, _, epg = w_router.shape
    w_gr = jnp.concatenate([w_group, jnp.transpose(w_router, (1, 0, 2)).reshape(d, ng * epg),
                            jnp.zeros((d, LANES - ng - ng * epg), F32)], axis=1)
    b_gr = jnp.concatenate([b_group, b_router.reshape(-1), jnp.zeros((LANES - ng - ng * epg,), F32)]).reshape(1, LANES)
    h, info, cnt = _route(x1, g_ffn, w_gr, b_gr)
    n_e = ng * epg
    n_assign = t * EXPERT_TOPK
    assert n_assign % MOE_ROWS == 0
    n_blocks = n_assign // MOE_ROWS
    counts = cnt[0, :n_e].astype(jnp.int32)
    end = jnp.cumsum(counts)
    start = end - counts
    expert = info[:, 0:EXPERT_TOPK].astype(jnp.int32)
    rank = info[:, 4:4 + EXPERT_TOPK].astype(jnp.int32)
    e_ids = jnp.arange(n_e, dtype=jnp.int32)
    dest = (jnp.sum(jnp.where(expert[..., None] == e_ids, start, 0), axis=-1) + rank).reshape(-1)
    owns = counts > 0
    seq_of_expert = jnp.cumsum(owns.astype(jnp.int32)) - 1
    n_seq = jnp.sum(owns.astype(jnp.int32)).reshape(1)
    expert_of_seq = jnp.sum(jnp.where(owns[None, :] & (seq_of_expert[None, :] == e_ids[:, None]), e_ids[None, :], 0),
                            axis=1)
    first = start // MOE_ROWS
    last = jnp.where(owns, (end - 1) // MOE_ROWS, first - 1)
    pair_end = jnp.cumsum(last - first + 1)
    pair_start = pair_end - (last - first + 1)
    n_steps = n_blocks + n_e
    p_ids = jnp.arange(n_steps, dtype=jnp.int32)
    e_of_pair = jnp.minimum(jnp.sum((pair_end[None, :] <= p_ids[:, None]).astype(jnp.int32), axis=1), n_e - 1)
    pick = e_of_pair[:, None] == e_ids[None, :]
    lookup = lambda v: jnp.sum(jnp.where(pick, v[None, :], 0), axis=1)
    blk_of_pair = jnp.clip(lookup(first) + p_ids - lookup(pair_start), 0, n_blocks - 1)
    lo = jnp.clip(lookup(start) - blk_of_pair * MOE_ROWS, 0, MOE_ROWS)
    hi = jnp.clip(lookup(end) - blk_of_pair * MOE_ROWS, 0, MOE_ROWS)
    seq_of_pair = lookup(seq_of_expert)
    n_pairs = pair_end[-1:].astype(jnp.int32)
    xs = _dispatch(dest, h)
    ys = _experts(n_pairs, seq_of_pair, expert_of_seq, n_seq, blk_of_pair, lo, hi, xs, w_gate, w_up, w_down)
    return _combine(dest, x1, info, g_final, ys)


def kernel(x, rel_bias, norm_mix, w_in, cmp_pe_k, cmp_w1_k, cmp_w2_k, cmp_pe_v, cmp_w1_v, cmp_w2_v, w_up_nsa,
           w_up_moba, w_out, norm_ffn, w_group, b_group, w_router, b_router, w_exp_gate, w_exp_up, w_exp_down,
           final_norm):
    b, s, d = x.shape
    t = b * s
    depth = w_in.shape[0]
    tab_a = rel_bias[:, :NSA_HEADS]
    tab_b = rel_bias[:, NSA_HEADS:]
    a_cols = NSA_HEADS * HEAD_DIM + 6 * NSA_KV_HEADS * HEAD_DIM
    gate_cols = 3 * NSA_HEADS
    b_cols = 3 * MOBA_HEADS * HEAD_DIM
    xt = x.reshape(t, d)
    out = None
    for l in range(depth):
        h = _rmsnorm(xt, norm_mix[l], BF16)
        wt = jnp.swapaxes(w_in[l], 0, 1)
        b_col0 = a_cols + gate_cols
        proj_a = _matmul(h, wt, 0, a_cols, BF16, scaled_cols=NSA_HEADS * HEAD_DIM, col_scale=Q_SCALE,
                         w_transposed=True, name="in_proj_a").reshape(b, s, a_cols)
        gate_a = _matmul(h, wt, a_cols, LANES, F32, w_transposed=True, tn=LANES, name="in_proj_gate")
        proj_b = _matmul(h, wt, b_col0, b_cols, BF16, scaled_cols=MOBA_HEADS * HEAD_DIM, col_scale=Q_SCALE,
                         w_transposed=True, name="in_proj_b").reshape(b, s, b_cols)
        gm = _matmul(h, wt, b_col0 + b_cols, 2 * d, BF16, w_transposed=True, name="in_proj_gm")
        o_c, o_s, o_w = _nsa(proj_a, gate_cols, cmp_pe_k[l], cmp_w1_k[l], cmp_w2_k[l],
                             cmp_pe_v[l], cmp_w1_v[l], cmp_w2_v[l], tab_a, b, s)
        o_b = _moba(proj_b, tab_b, b, s)
        merged = _merge(o_c.reshape(t, -1), o_s.reshape(t, -1), o_w.reshape(t, -1), gate_a,
                        o_b.reshape(t, -1), gm, w_up_nsa[l], w_up_moba[l])
        x1 = _matmul(merged, w_out[l], 0, d, F32, res=xt, name="out_proj")
        assert l == depth - 1, "only the last layer's MoE is fused with the final norm"
        out = _moe(x1, norm_ffn[l], w_group[l], b_group[l], w_router[l], b_router[l],
                   w_exp_gate[l], w_exp_up[l], w_exp_down[l], final_norm)
    return out.reshape(b, s, d)
```

```python
import functools
import math

import numpy as np
import jax
import jax.numpy as jnp
from jax import lax
from jax.experimental import pallas as pl
from jax.experimental.pallas import tpu as pltpu

F32 = jnp.float32
BF16 = jnp.bfloat16

HEAD_DIM = 128
NSA_HEADS = 8
NSA_KV_HEADS = 2
NSA_GROUP = NSA_HEADS // NSA_KV_HEADS
CMP_BLOCK = 32
CMP_STRIDE = 16
SEL_BLOCK = 64
SEL_TOPN = 16
WINDOW = 512
FORCED_SCORE = 1e4
MOBA_HEADS = 8
MOBA_BLOCK = 256
MOBA_TOPK = 3
REL_BUCKETS = 32
REL_MAX_DIST = 128
N_GROUPS = 8
EXPERTS_PER_GROUP = 8
N_EXPERTS = N_GROUPS * EXPERTS_PER_GROUP
EXPERT_TOPK = 2
RMS_EPS = 1e-6

LANES = 128
ATT_TILE = 512
ATT_HEADS_PER_STEP = 8
ATT_VMEM_LIMIT = 56 * 1024 * 1024
MOE_ROWS = 256
MASK_BIG = 1e30
M_INIT = -3e38
LOG2E = math.log2(math.e)
Q_SCALE = HEAD_DIM ** -0.5 * LOG2E
VMEM_LIMIT = 48 * 1024 * 1024


def _cparams(sem, vmem=VMEM_LIMIT, flags=None):
    return pltpu.CompilerParams(dimension_semantics=sem, vmem_limit_bytes=vmem, flags=flags)


def _rmsnorm_body(x_ref, g_ref, o_ref):
    x = x_ref[...]
    ms = jnp.mean(x * x, axis=-1, keepdims=True)
    o_ref[...] = (x * lax.rsqrt(ms + RMS_EPS) * g_ref[...]).astype(o_ref.dtype)


def _rmsnorm(x, g, out_dtype, tm=512):
    t, d = x.shape
    return pl.pallas_call(
        _rmsnorm_body,
        grid=(t // tm,),
        in_specs=[pl.BlockSpec((tm, d), lambda i: (i, 0)),
                  pl.BlockSpec((1, d), lambda i: (0, 0))],
        out_specs=pl.BlockSpec((tm, d), lambda i: (i, 0)),
        out_shape=jax.ShapeDtypeStruct((t, d), out_dtype),
        compiler_params=_cparams(("parallel",)),
        name="rmsnorm",
    )(x, g.reshape(1, d))


def _mm_body(*refs, has_res, n_scaled, col_scale, w_transposed):
    if has_res:
        a_ref, w_ref, r_ref, o_ref = refs
    else:
        a_ref, w_ref, o_ref = refs
    w = w_ref[...].astype(BF16)
    if w_transposed:
        acc = lax.dot_general(a_ref[...], w, (((1,), (1,)), ((), ())), preferred_element_type=F32)
    else:
        acc = jnp.dot(a_ref[...], w, preferred_element_type=F32)
    if n_scaled:
        acc = acc * jnp.where(pl.program_id(1) < n_scaled, col_scale, 1.0)
    if has_res:
        acc = acc + r_ref[...]
    o_ref[...] = acc.astype(o_ref.dtype)


def _matmul(a, w, col0, ncols, out_dtype, res=None, scaled_cols=0, col_scale=1.0, w_transposed=False,
            tm=2048, tn=512, name="matmul"):
    t, k = a.shape
    tn = min(tn, ncols)
    tm = min(tm, t)
    assert ncols % tn == 0 and t % tm == 0 and scaled_cols % tn == 0
    if w_transposed:
        assert col0 % 8 == 0
        w_spec = pl.BlockSpec((pl.Element(tn), pl.Element(k)),
                              lambda i, j: (pl.multiple_of(col0 + j * tn, 8), 0))
    else:
        assert col0 % tn == 0
        off = col0 // tn
        w_spec = pl.BlockSpec((k, tn), lambda i, j: (0, j + off))
    in_specs = [pl.BlockSpec((tm, k), lambda i, j: (i, 0)), w_spec]
    args = [a, w]
    if res is not None:
        in_specs.append(pl.BlockSpec((tm, tn), lambda i, j: (i, j)))
        args.append(res)
    return pl.pallas_call(
        functools.partial(_mm_body, has_res=res is not None, n_scaled=scaled_cols // tn, col_scale=col_scale,
                          w_transposed=w_transposed),
        grid=(t // tm, ncols // tn),
        in_specs=in_specs,
        out_specs=pl.BlockSpec((tm, tn), lambda i, j: (i, j)),
        out_shape=jax.ShapeDtypeStruct((t, ncols), out_dtype),
        compiler_params=_cparams(("parallel", "parallel")),
        name=name,
    )(*args)


def _compress_body(u_ref, pek_ref, w1k_ref, w2k_ref, pev_ref, w1v_ref, w2v_ref, o_ref, *, nc, g):
    dh = HEAD_DIM
    tok_w = 2 * g * dh
    for kv, (pe_ref, w1_ref, w2_ref) in enumerate(((pek_ref, w1k_ref, w2k_ref), (pev_ref, w1v_ref, w2v_ref))):
        w1 = w1_ref[...].astype(BF16)
        w2 = w2_ref[...].astype(BF16)
        half = CMP_STRIDE * dh
        peb = jnp.dot(pe_ref[...].astype(BF16), w1, preferred_element_type=F32)[0:1]
        for gi in range(g):
            a = b = None
            for r in range(CMP_STRIDE):
                c0 = r * tok_w + (kv * g + gi) * dh
                piece = u_ref[:, c0:c0 + dh]
                da = jnp.dot(piece, w1[r * dh:(r + 1) * dh], preferred_element_type=F32)
                db = jnp.dot(piece, w1[half + r * dh:half + (r + 1) * dh], preferred_element_type=F32)
                a = da if a is None else a + da
                b = db if b is None else b + db
            pre = a + pltpu.roll(b, nc - 1, 0) + peb
            hid = jax.nn.gelu(pre)
            o_ref[kv * g + gi] = jnp.dot(hid.astype(BF16), w2, preferred_element_type=F32).astype(o_ref.dtype)


def _compress(u, pe_k, w1_k, w2_k, pe_v, w1_v, w2_v, g):
    b, nc, kk = u.shape
    hid = w1_k.shape[1]
    dh = w2_k.shape[1]
    pe16 = lambda pe: jnp.broadcast_to(pe.reshape(1, -1), (16, pe.size))
    full = lambda shape: pl.BlockSpec(shape, lambda i: (0,) * len(shape))
    return pl.pallas_call(
        functools.partial(_compress_body, nc=nc, g=g),
        grid=(b,),
        in_specs=[pl.BlockSpec((None, nc, kk), lambda i: (i, 0, 0)),
                  full((16, CMP_BLOCK * dh)), full((CMP_BLOCK * dh, hid)), full((hid, dh)),
                  full((16, CMP_BLOCK * dh)), full((CMP_BLOCK * dh, hid)), full((hid, dh))],
        out_specs=pl.BlockSpec((None, 2 * g, nc, dh), lambda i: (i, 0, 0, 0)),
        out_shape=jax.ShapeDtypeStruct((b, 2 * g, nc, dh), BF16),
        compiler_params=_cparams(("parallel",)),
        name="nsa_compress",
    )(u, pe16(pe_k), w1_k, w2_k, pe16(pe_v), w1_v, w2_v)


def _split3(x):
    p1 = x.astype(BF16)
    r = x - p1.astype(F32)
    p2 = r.astype(BF16)
    p3 = (r - p2.astype(F32)).astype(BF16)
    return p1, p2, p3


def _rank_count(score, n_rows):
    groups = []
    for g0 in range(0, n_rows, 8):
        sg = score[g0:min(g0 + 8, n_rows), :]
        n_iota = g0 + lax.broadcasted_iota(jnp.int32, sg.shape, 0)
        cnt = jnp.zeros(sg.shape, F32)
        for m in range(n_rows):
            row = score[m:m + 1, :]
            if m < g0:
                beats = row >= sg
            elif m >= g0 + 8:
                beats = row > sg
            else:
                tie = jnp.where(n_iota > m, 1.0, 0.0)
                beats = jnp.where(row > sg, 1.0, jnp.where(row == sg, tie, 0.0)) > 0.5
            cnt = cnt + jnp.where(beats, 1.0, 0.0)
        groups.append(cnt)
    return jnp.concatenate(groups, axis=0) if len(groups) > 1 else groups[0]


def _nsa_cmp_body(q_ref, kc_ref, vc_ref, bias_ref, ov_ref, oc_ref, mem_ref, *, tq, nc, n_sel):
    t0 = pl.program_id(2) * tq
    kc = kc_ref[...]
    vc = vc_ref[...]
    t_idx = t0 + lax.broadcasted_iota(jnp.int32, (tq, nc), 0)
    c_idx = lax.broadcasted_iota(jnp.int32, (tq, nc), 1)
    dist = t_idx - (c_idx * CMP_STRIDE + (CMP_BLOCK - 1))
    n_k = REL_MAX_DIST // CMP_STRIDE
    kidx = jnp.where(dist < 0, n_k + 1, jnp.minimum(lax.shift_right_logical(dist, 4), n_k))
    assert CMP_STRIDE == 16
    psum = jnp.zeros((tq, nc), F32)
    for j in range(NSA_GROUP):
        hs = slice(j * HEAD_DIM, (j + 1) * HEAD_DIM)
        gt = bias_ref[j] * LOG2E
        bias = jnp.concatenate([jnp.take_along_axis(gt, kidx[:, c0:c0 + LANES], axis=1)
                                for c0 in range(0, nc, LANES)], axis=1)
        s = lax.dot_general(q_ref[:, hs], kc, (((1,), (1,)), ((), ())), preferred_element_type=F32) + bias
        m = jnp.max(s, axis=-1, keepdims=True)
        m = jnp.where(m > -0.5 * MASK_BIG, m, 0.0)
        p = jnp.exp2(s - m)
        d = jnp.sum(p, axis=-1, keepdims=True)
        p = p / jnp.where(d > 0, d, 1.0)
        oc_ref[:, hs] = jnp.dot(p.astype(BF16), vc, preferred_element_type=F32).astype(oc_ref.dtype)
        psum = psum + p
    ov = ov_ref[...]
    nt = (((1,), (1,)), ((), ()))
    p1, p2, p3 = _split3(psum)
    psel = (lax.dot_general(ov, p1, nt, preferred_element_type=F32)
            + lax.dot_general(ov, p2, nt, preferred_element_type=F32)
            + lax.dot_general(ov, p3, nt, preferred_element_type=F32))
    n_idx = lax.broadcasted_iota(jnp.int32, (n_sel, tq), 0)
    tt = t0 + lax.broadcasted_iota(jnp.int32, (n_sel, tq), 1)
    cur = tt // SEL_BLOCK
    forced = jnp.where(n_idx == 0, 1.0, jnp.where(n_idx == cur, 1.0, jnp.where(n_idx == cur - 1, 1.0, 0.0)))
    score = jnp.where(forced > 0.5, FORCED_SCORE, jnp.where(n_idx * SEL_BLOCK <= tt, psel, -1.0))
    cnt = _rank_count(score, n_sel)
    member = jnp.where(cnt < float(min(SEL_TOPN, n_sel)), 1.0, 0.0)
    if n_sel < LANES:
        member = jnp.concatenate([member, jnp.zeros((LANES - n_sel, tq), F32)], axis=0)
    mem_ref[...] = member.T.astype(mem_ref.dtype)


def _nsa_cmp(proj, kvc, tab, overlap, b, s, tq=256):
    g = NSA_KV_HEADS
    nc = kvc.shape[2]
    n_sel = s // SEL_BLOCK
    gw = NSA_GROUP * HEAD_DIM
    assert tq % CMP_STRIDE == 0 and REL_MAX_DIST % CMP_STRIDE == 0
    n_k = REL_MAX_DIST // CMP_STRIDE
    rho = (np.arange(tq)[:, None] - (CMP_BLOCK - 1)) % CMP_STRIDE
    dd = np.concatenate([rho + CMP_STRIDE * np.arange(n_k)[None, :], np.full((tq, 1), REL_MAX_DIST)], axis=1)
    gtab = jnp.transpose(_bias_of_dist(tab, dd), (2, 0, 1)).astype(F32)
    bias_c = jnp.concatenate([gtab, jnp.full(gtab.shape[:2] + (1,), -MASK_BIG, F32),
                              jnp.zeros(gtab.shape[:2] + (LANES - n_k - 2,), F32)], axis=2)
    body = functools.partial(_nsa_cmp_body, tq=tq, nc=nc, n_sel=n_sel)
    return pl.pallas_call(
        body,
        grid=(b, g, s // tq),
        in_specs=[pl.BlockSpec((None, tq, gw), lambda bi, gi, i: (bi, i, gi)),
                  pl.BlockSpec((None, None, nc, HEAD_DIM), lambda bi, gi, i: (bi, gi, 0, 0)),
                  pl.BlockSpec((None, None, nc, HEAD_DIM), lambda bi, gi, i: (bi, g + gi, 0, 0)),
                  pl.BlockSpec((NSA_GROUP, tq, LANES), lambda bi, gi, i: (gi, 0, 0)),
                  pl.BlockSpec((n_sel, nc), lambda bi, gi, i: (0, 0))],
        out_specs=[pl.BlockSpec((None, tq, gw), lambda bi, gi, i: (bi, i, gi)),
                   pl.BlockSpec((None, None, tq, LANES), lambda bi, gi, i: (bi, gi, i, 0))],
        out_shape=[jax.ShapeDtypeStruct((b, s, NSA_HEADS * HEAD_DIM), BF16),
                   jax.ShapeDtypeStruct((b, g, s, LANES), BF16)],
        compiler_params=_cparams(("parallel", "parallel", "parallel")),
        name="nsa_cmp_select",
    )(proj, kvc, kvc, bias_c, overlap)


def _moba_gate_body(q_ref, k_ref, mem_ref, *, s, nblk):
    k = k_ref[...].astype(F32)
    kmean = jnp.mean(k.reshape(nblk, MOBA_BLOCK, HEAD_DIM), axis=1)
    k1 = kmean.astype(BF16)
    k2 = (kmean - k1.astype(F32)).astype(BF16)
    q = q_ref[...]
    nt = (((1,), (1,)), ((), ()))
    gate = (lax.dot_general(k1, q, nt, preferred_element_type=F32)
            + lax.dot_general(k2, q, nt, preferred_element_type=F32))
    n_idx = lax.broadcasted_iota(jnp.int32, (nblk, s), 0)
    own = lax.broadcasted_iota(jnp.int32, (nblk, s), 1) // MOBA_BLOCK
    past = n_idx < own
    score = jnp.where(past, gate, -MASK_BIG)
    cnt = _rank_count(score, nblk)
    n_top = max(1, min(MOBA_TOPK, nblk - 1))
    sel = jnp.where(past, jnp.where(cnt < float(n_top), 1.0, 0.0), 0.0)
    member = jnp.where(n_idx == own, 1.0, sel)
    member = jnp.concatenate([member, jnp.zeros((LANES - nblk, s), F32)], axis=0)
    mem_ref[...] = member.T.astype(mem_ref.dtype)


def _moba_gate(proj, b, s):
    h = MOBA_HEADS
    nblk = s // MOBA_BLOCK
    return pl.pallas_call(
        functools.partial(_moba_gate_body, s=s, nblk=nblk),
        grid=(b, h),
        in_specs=[pl.BlockSpec((None, s, HEAD_DIM), lambda bi, hi: (bi, 0, hi)),
                  pl.BlockSpec((None, s, HEAD_DIM), lambda bi, hi: (bi, 0, h + hi))],
        out_specs=pl.BlockSpec((None, None, s, LANES), lambda bi, hi: (bi, hi, 0, 0)),
        out_shape=jax.ShapeDtypeStruct((b, h, s, LANES), BF16),
        compiler_params=_cparams(("parallel", "parallel")),
        name="moba_gate",
    )(proj, proj)


def _flash_body(qi_ref, ki_ref, bo_ref, fl_ref, *refs, nh, ratio, nm, n_near, has_far):
    if nm:
        (q_ref, k_ref, v_ref, bvec_ref, mem_ref, et_ref, o_ref,
         m_ref, l_ref, acc_ref, sh_ref, al_ref, bias_ref, s_ref, p_ref) = refs
    else:
        q_ref, k_ref, v_ref, bvec_ref, o_ref, m_ref, l_ref, acc_ref, sh_ref, al_ref, bias_ref, s_ref, p_ref = refs
    del qi_ref, ki_ref
    p = pl.program_id(2)
    flag = fl_ref[p]
    bo = bo_ref[p]
    t = q_ref.shape[0]
    rows = 64

    @pl.when(p == 0)
    def _():
        for h in range(nh):
            for o in range(n_near):
                vec = bvec_ref[h, o][0:1, :] * LOG2E
                for rc in range(t // rows):
                    x = pltpu.roll(jnp.broadcast_to(vec, (rows, 2 * t)), rc * rows, 1, stride=1, stride_axis=0)
                    bias_ref[h, o, rc * rows:(rc + 1) * rows, :] = x[:, :t]

    @pl.when((flag & 1) != 0)
    def _():
        m_ref[...] = jnp.full(m_ref.shape, M_INIT, F32)
        l_ref[...] = jnp.zeros(l_ref.shape, F32)
        acc_ref[...] = jnp.zeros(acc_ref.shape, F32)

    nt = (((1,), (1,)), ((), ()))
    reps = t // LANES

    def step(near):
        def pass1(h):
            hs = slice(h * HEAD_DIM, (h + 1) * HEAD_DIM)
            kv = h // ratio
            ks = slice(kv * HEAD_DIM, (kv + 1) * HEAD_DIM)
            q = q_ref[:, hs]
            k = k_ref[:, ks]
            if nm:
                mneg = mem_ref[h // (nh // nm)] - 1.0
                q = jnp.concatenate([q, mneg.astype(BF16)], axis=1)
                k = jnp.concatenate([k, et_ref[...]], axis=1)
            sc = lax.dot_general(q, k, nt, preferred_element_type=F32)
            m_prev = m_ref[h]
            if near:
                sc = sc + bias_ref[h, bo]
                m_new = jnp.maximum(m_prev, jnp.max(sc, axis=-1, keepdims=True))
                sh_ref[h] = m_new
            else:
                cfar = bvec_ref[h, n_near][0:1, 0:LANES] * LOG2E
                m_new = jnp.maximum(m_prev, jnp.max(sc, axis=-1, keepdims=True) + cfar)
                sh_ref[h] = m_new - cfar
            s_ref[h] = sc
            al_ref[h] = jnp.exp2(m_prev - m_new)
            m_ref[h] = m_new

        def pass2(h):
            ks = slice(h // ratio * HEAD_DIM, (h // ratio + 1) * HEAD_DIM)
            for rc in range(t // rows):
                rs = slice(rc * rows, (rc + 1) * rows)
                pm = jnp.exp2(s_ref[h, rs, :] - jnp.tile(sh_ref[h, rs, :], (1, reps)))
                l_ref[h, rs, :] = al_ref[h, rs, :] * l_ref[h, rs, :] + jnp.sum(pm, axis=-1, keepdims=True)
                p_ref[h, rs, :] = pm.astype(BF16)
            acc_ref[h] = al_ref[h] * acc_ref[h] + jnp.dot(p_ref[h], v_ref[:, ks], preferred_element_type=F32)

        for h in range(nh):
            pass1(h)
        for h in range(nh):
            pass2(h)

    if has_far:
        pl.when(bo < n_near)(lambda: step(True))
        pl.when(bo >= n_near)(lambda: step(False))
    else:
        step(True)

    @pl.when((flag & 2) != 0)
    def _():
        for h in range(nh):
            l = l_ref[h]
            o_ref[:, h * HEAD_DIM:(h + 1) * HEAD_DIM] = (acc_ref[h] / jnp.where(l > 0, l, 1.0)).astype(o_ref.dtype)


def _flash(q_arr, q_off, k_arr, k_off, v_arr, v_off, bias, n_heads, ratio, nh, pairs, has_far, member=None,
           et=None, name="flash"):
    b, s, _ = q_arr.shape
    t = ATT_TILE
    nkv = nh // ratio
    ng = n_heads // nh
    qi = jnp.asarray([p[0] for p in pairs], jnp.int32)
    ki = jnp.asarray([p[1] for p in pairs], jnp.int32)
    bo = jnp.asarray([p[2] for p in pairs], jnp.int32)
    fl = jnp.asarray([p[3] for p in pairs], jnp.int32)
    nb = bias.shape[1]
    nm = 0
    in_specs = [
        pl.BlockSpec((None, t, nh * HEAD_DIM), lambda bi, gi, p, qi, ki, bo, fl: (bi, qi[p], q_off + gi)),
        pl.BlockSpec((None, t, nkv * HEAD_DIM), lambda bi, gi, p, qi, ki, bo, fl: (bi, ki[p], k_off + gi)),
        pl.BlockSpec((None, t, nkv * HEAD_DIM), lambda bi, gi, p, qi, ki, bo, fl: (bi, ki[p], v_off + gi)),
        pl.BlockSpec((nh, nb, 8, 2 * t), lambda bi, gi, p, qi, ki, bo, fl: (gi, 0, 0, 0)),
    ]
    args = [q_arr, k_arr, v_arr, bias]
    if member is not None:
        nm = member.shape[1] // ng
        in_specs += [
            pl.BlockSpec((None, nm, t, LANES), lambda bi, gi, p, qi, ki, bo, fl: (bi, gi, qi[p], 0)),
            pl.BlockSpec((t, LANES), lambda bi, gi, p, qi, ki, bo, fl: (ki[p], 0)),
        ]
        args += [member, et]
    n_near = nb - 1 if has_far else nb
    body = functools.partial(_flash_body, nh=nh, ratio=ratio, nm=nm, n_near=n_near, has_far=has_far)
    return pl.pallas_call(
        body,
        grid_spec=pltpu.PrefetchScalarGridSpec(
            num_scalar_prefetch=4,
            grid=(b, ng, len(pairs)),
            in_specs=in_specs,
            out_specs=pl.BlockSpec((None, t, nh * HEAD_DIM), lambda bi, gi, p, qi, ki, bo, fl: (bi, qi[p], gi)),
            scratch_shapes=[pltpu.VMEM((nh, t, LANES), F32)] * 5 + [pltpu.VMEM((nh, n_near, t, t), F32),
                                                                    pltpu.VMEM((nh, t, t), F32),
                                                                    pltpu.VMEM((nh, t, t), BF16)],
        ),
        out_shape=jax.ShapeDtypeStruct((b, s, n_heads * HEAD_DIM), BF16),
        compiler_params=_cparams(("parallel", "parallel", "arbitrary"), vmem=ATT_VMEM_LIMIT),
        name=name,
    )(qi, ki, bo, fl, *args)


def _rel_bucket(dist):
    n = jnp.maximum(jnp.asarray(dist, jnp.int32), 0)
    max_exact = REL_BUCKETS // 2
    nf = jnp.maximum(n, 1).astype(jnp.float32)
    large = max_exact + (jnp.log(nf / max_exact) / math.log(REL_MAX_DIST / max_exact)
                         * (REL_BUCKETS - max_exact)).astype(jnp.int32)
    return jnp.where(n < max_exact, n, jnp.minimum(large, REL_BUCKETS - 1))


def _bias_of_dist(tab, dist):
    hit = _rel_bucket(dist)[..., None, None] == jnp.arange(REL_BUCKETS)[:, None]
    return jnp.sum(jnp.where(hit, tab, 0.0), axis=-2)


def _n_near(t):
    return -(-(REL_MAX_DIST - 1 + t) // t)


def _bias_vecs(tab, t, n_off, window=None):
    k = np.arange(2 * t)[None, :]
    dist = np.arange(n_off)[:, None] * t + np.where(k < t, -k, 2 * t - k)
    ok = dist >= 0
    if window is not None:
        ok &= dist < window
    bias = jnp.where(jnp.asarray(ok)[..., None], _bias_of_dist(tab, dist), -MASK_BIG)
    bias = jnp.transpose(bias, (2, 0, 1)).astype(F32)
    return jnp.broadcast_to(bias[:, :, None, :], (bias.shape[0], n_off, 8, 2 * t))


def _causal_pairs(nq, n_near):
    pairs = []
    for qi in range(nq):
        for ki in range(qi + 1):
            pairs.append((qi, ki, min(qi - ki, n_near), (1 if ki == 0 else 0) | (2 if ki == qi else 0)))
    return pairs


def _window_pairs(nq, n_back):
    pairs = []
    for qi in range(nq):
        lo = max(0, qi - n_back)
        for ki in range(lo, qi + 1):
            pairs.append((qi, ki, qi - ki, (1 if ki == lo else 0) | (2 if ki == qi else 0)))
    return pairs


def _block_onehot(s, blk):
    return jnp.asarray(np.where(np.arange(s)[:, None] // blk == np.arange(LANES)[None, :], MASK_BIG, 0.0), BF16)


def _merge_body(oc_ref, os_ref, ow_ref, gl_ref, ob_ref, gma_ref, gmb_ref, wa_ref, wb_ref, o_ref, oa_ref):
    @pl.when(pl.program_id(1) == 0)
    def _():
        gates = jax.nn.sigmoid(gl_ref[...])
        for h in range(NSA_HEADS):
            hs = slice(h * HEAD_DIM, (h + 1) * HEAD_DIM)
            mix = (gates[:, 3 * h:3 * h + 1] * oc_ref[:, hs].astype(F32)
                   + gates[:, 3 * h + 1:3 * h + 2] * os_ref[:, hs].astype(F32)
                   + gates[:, 3 * h + 2:3 * h + 3] * ow_ref[:, hs].astype(F32))
            oa_ref[:, hs] = mix.astype(BF16)

    ya = jnp.dot(oa_ref[...], wa_ref[...].astype(BF16), preferred_element_type=F32)
    yb = jnp.dot(ob_ref[...], wb_ref[...].astype(BF16), preferred_element_type=F32)
    ga = jax.nn.sigmoid(gma_ref[...].astype(F32))
    gb = jax.nn.sigmoid(gmb_ref[...].astype(F32))
    o_ref[...] = (ga * ya + gb * yb).astype(o_ref.dtype)


def _merge(o_c, o_s, o_w, gate_logits, o_b, gm, w_up_a, w_up_b, tm=1024, tn=512):
    t, ka = o_c.shape
    kb = o_b.shape[1]
    d = w_up_a.shape[1]
    tm = min(tm, t)
    nj = d // tn
    row = lambda i, j: (i, 0)
    return pl.pallas_call(
        _merge_body,
        grid=(t // tm, nj),
        in_specs=[pl.BlockSpec((tm, ka), row), pl.BlockSpec((tm, ka), row), pl.BlockSpec((tm, ka), row),
                  pl.BlockSpec((tm, LANES), row), pl.BlockSpec((tm, kb), row),
                  pl.BlockSpec((tm, tn), lambda i, j: (i, j)),
                  pl.BlockSpec((tm, tn), lambda i, j: (i, j + nj)),
                  pl.BlockSpec((ka, tn), lambda i, j: (0, j)),
                  pl.BlockSpec((kb, tn), lambda i, j: (0, j))],
        out_specs=pl.BlockSpec((tm, tn), lambda i, j: (i, j)),
        out_shape=jax.ShapeDtypeStruct((t, d), BF16),
        scratch_shapes=[pltpu.VMEM((tm, ka), BF16)],
        compiler_params=_cparams(("parallel", "arbitrary")),
        name="merge_up",
    )(o_c, o_s, o_w, gate_logits, o_b, gm, gm, w_up_a, w_up_b)


def _route_body(x_ref, g_ref, w_ref, b_ref, h_ref, info_ref, cnt_ref, carry_ref, *, tm):
    @pl.when(pl.program_id(0) == 0)
    def _():
        carry_ref[...] = jnp.zeros(carry_ref.shape, F32)

    x = x_ref[...]
    ms = jnp.mean(x * x, axis=-1, keepdims=True)
    h = x * lax.rsqrt(ms + RMS_EPS) * g_ref[...]
    _store_packed(h_ref, h, tm)
    w = w_ref[...]
    h1 = h.astype(BF16)
    h2 = (h - h1.astype(F32)).astype(BF16)
    w1 = w.astype(BF16)
    w2 = (w - w1.astype(F32)).astype(BF16)
    logits = (jnp.dot(h1, w1, preferred_element_type=F32) + jnp.dot(h1, w2, preferred_element_type=F32)
              + jnp.dot(h2, w1, preferred_element_type=F32)) + b_ref[...]
    lane = lax.broadcasted_iota(jnp.int32, (tm, LANES), 1)
    lanef = lane.astype(F32)

    is_g = lane < N_GROUPS
    gl = jnp.where(is_g, logits, -MASK_BIG)
    ge = jnp.where(is_g, jnp.exp(gl - jnp.max(gl, axis=-1, keepdims=True)), 0.0)
    gp = ge / jnp.sum(ge, axis=-1, keepdims=True)
    g_val = jnp.max(gp, axis=-1, keepdims=True)
    g_idx = jnp.min(jnp.where(gp == g_val, lanef, float(LANES)), axis=-1, keepdims=True)

    lane_grp = ((lane - N_GROUPS) // EXPERTS_PER_GROUP).astype(F32)
    in_e = jnp.where(lane >= N_GROUPS, jnp.where(lane < N_GROUPS + N_EXPERTS, 1.0, 0.0), 0.0)
    is_e = jnp.where(lane_grp == g_idx, in_e, 0.0) > 0.5
    el = jnp.where(is_e, logits, -MASK_BIG)
    ee = jnp.where(is_e, jnp.exp(el - jnp.max(el, axis=-1, keepdims=True)), 0.0)
    ep = jnp.where(is_e, ee / jnp.sum(ee, axis=-1, keepdims=True), -1.0)
    v1 = jnp.max(ep, axis=-1, keepdims=True)
    l1 = jnp.min(jnp.where(ep == v1, lanef, float(LANES)), axis=-1, keepdims=True)
    ep2 = jnp.where(lanef == l1, -1.0, ep)
    v2 = jnp.max(ep2, axis=-1, keepdims=True)
    l2 = jnp.min(jnp.where(ep2 == v2, lanef, float(LANES)), axis=-1, keepdims=True)
    vs = v1 + v2
    wt1 = g_val * v1 / vs
    wt2 = g_val * v2 / vs
    e1 = l1 - float(N_GROUPS)
    e2 = l2 - float(N_GROUPS)

    oh = jnp.where(lanef == e1, 1.0, jnp.where(lanef == e2, 1.0, 0.0))
    r_i = lax.broadcasted_iota(jnp.int32, (tm, tm), 0)
    c_i = lax.broadcasted_iota(jnp.int32, (tm, tm), 1)
    tri = jnp.where(r_i > c_i, 1.0, 0.0).astype(BF16)
    base = jnp.dot(tri, oh.astype(BF16), preferred_element_type=F32) + carry_ref[...]
    r1 = jnp.sum(jnp.where(lanef == e1, base, 0.0), axis=-1, keepdims=True)
    r2 = jnp.sum(jnp.where(lanef == e2, base, 0.0), axis=-1, keepdims=True)
    carry_ref[...] = carry_ref[...] + jnp.sum(oh, axis=0, keepdims=True)
    cnt_ref[...] = jnp.broadcast_to(carry_ref[...], cnt_ref.shape)
    info = jnp.where(lane == 0, e1, jnp.where(lane == 1, e2, jnp.where(lane == 2, wt1, jnp.where(
        lane == 3, wt2, jnp.where(lane == 4, r1, jnp.where(lane == 5, r2, 0.0))))))
    info_ref[...] = info


def _route(x1, g, w_gr, b_gr, tm=512):
    t, d = x1.shape
    return pl.pallas_call(
        functools.partial(_route_body, tm=tm),
        grid=(t // tm,),
        in_specs=[pl.BlockSpec((tm, d), lambda i: (i, 0)),
                  pl.BlockSpec((1, d), lambda i: (0, 0)),
                  pl.BlockSpec((d, LANES), lambda i: (0, 0)),
                  pl.BlockSpec((1, LANES), lambda i: (0, 0))],
        out_specs=[pl.BlockSpec((tm * ROW_SUB, LANES), lambda i: (i, 0)),
                   pl.BlockSpec((tm, LANES), lambda i: (i, 0)),
                   pl.BlockSpec((8, LANES), lambda i: (0, 0))],
        out_shape=[jax.ShapeDtypeStruct((t * ROW_SUB, LANES), jnp.uint32),
                   jax.ShapeDtypeStruct((t, LANES), F32),
                   jax.ShapeDtypeStruct((8, LANES), F32)],
        scratch_shapes=[pltpu.VMEM((1, LANES), F32)],
        compiler_params=_cparams(("arbitrary",)),
        name="moe_route",
    )(x1, g.reshape(1, d), w_gr, b_gr)


ROW_SUB = 8
U32 = jnp.uint32


def _pack_pairs(lo, hi):
    lo_b = lax.bitcast_convert_type(lo.astype(BF16).astype(F32), U32)
    hi_b = lax.bitcast_convert_type(hi.astype(BF16).astype(F32), U32)
    return lax.shift_right_logical(lo_b, U32(16)) | (hi_b & U32(0xFFFF0000))


def _unpack_pairs(w):
    lo = lax.bitcast_convert_type(lax.shift_left(w, U32(16)), F32)
    hi = lax.bitcast_convert_type(w & U32(0xFFFF0000), F32)
    return lo, hi


def _store_packed(ref, y, n):
    half = y.shape[1] // 2
    for s in range(ROW_SUB):
        cs = slice(s * LANES, (s + 1) * LANES)
        ref[pl.ds(s, n, stride=ROW_SUB), :] = _pack_pairs(y[:, cs], y[:, half + s * LANES:half + (s + 1) * LANES])


def _load_packed(ref, n):
    los, his = [], []
    for s in range(ROW_SUB):
        lo, hi = _unpack_pairs(ref[pl.ds(s, n, stride=ROW_SUB), :])
        los.append(lo)
        his.append(hi)
    return jnp.concatenate(los + his, axis=1)


def _row_copy(src_ref, src_row, dst_ref, dst_row, sem):
    return pltpu.make_async_copy(src_ref.at[pl.ds(pl.multiple_of(src_row * ROW_SUB, ROW_SUB), ROW_SUB)],
                                 dst_ref.at[pl.ds(pl.multiple_of(dst_row * ROW_SUB, ROW_SUB), ROW_SUB)], sem)


def _dispatch_body(dest_ref, h_ref, xs_ref, sem, *, tm):
    base = pl.program_id(0) * tm

    def issue(r, c):
        for k in range(EXPERT_TOPK):
            _row_copy(h_ref, r, xs_ref, dest_ref[EXPERT_TOPK * (base + r) + k], sem).start(priority=k % 2)
        return c

    lax.fori_loop(0, tm, issue, 0, unroll=4)

    def drain(r, c):
        for k in range(EXPERT_TOPK):
            _row_copy(h_ref, r, xs_ref, dest_ref[EXPERT_TOPK * (base + r) + k], sem).wait()
        return c

    lax.fori_loop(0, tm, drain, 0, unroll=4)


def _dispatch(dest, hp, tm=512):
    t = hp.shape[0] // ROW_SUB
    n_rows = dest.shape[0]
    return pl.pallas_call(
        functools.partial(_dispatch_body, tm=tm),
        grid_spec=pltpu.PrefetchScalarGridSpec(
            num_scalar_prefetch=1,
            grid=(t // tm,),
            in_specs=[pl.BlockSpec((tm * ROW_SUB, LANES), lambda i, dest: (i, 0))],
            out_specs=pl.BlockSpec(memory_space=pl.ANY),
            scratch_shapes=[pltpu.SemaphoreType.DMA(())],
        ),
        out_shape=jax.ShapeDtypeStruct((n_rows * ROW_SUB, LANES), U32),
        compiler_params=_cparams(("arbitrary",)),
        name="moe_dispatch",
    )(dest, hp)


def _expert_body(nu_ref, sq_ref, es_ref, ns_ref, blk_ref, lo_ref, hi_ref, x_ref, wg_hbm, wu_hbm, wd_hbm, y_ref,
                 wg_b, wu_b, wd_b, wg_s, wu_s, wd_s, y_acc, sem):
    i = pl.program_id(0)
    nu = nu_ref[0]
    ns = ns_ref[0]

    def weight_copies(seq, slot):
        e = es_ref[seq]
        return (pltpu.make_async_copy(wg_hbm.at[e], wg_b.at[slot], sem.at[slot]),
                pltpu.make_async_copy(wu_hbm.at[e], wu_b.at[slot], sem.at[slot]),
                pltpu.make_async_copy(wd_hbm.at[e], wd_b.at[slot], sem.at[slot]))

    def start_weights(seq, slot):
        for c in weight_copies(seq, slot):
            c.start()

    @pl.when(i == 0)
    def _():
        y_acc[...] = jnp.zeros(y_acc.shape, F32)
        start_weights(0, 0)

        @pl.when(ns > 1)
        def _():
            start_weights(1, 1)

    s = sq_ref[i]
    first = (i == 0) | (s != sq_ref[jnp.maximum(i - 1, 0)])

    @pl.when((i < nu) & first)
    def _():
        slot = lax.rem(s, 2)
        for c in weight_copies(s, slot):
            c.wait()
        def cast_rows(dst, src, chunk):
            def body(c, carry):
                rs = pl.ds(pl.multiple_of(c * chunk, chunk), chunk)
                dst[rs, :] = src[slot, rs, :].astype(BF16)
                return carry
            lax.fori_loop(0, dst.shape[0] // chunk, body, 0)

        cast_rows(wg_s, wg_b, 256)
        cast_rows(wu_s, wu_b, 256)
        cast_rows(wd_s, wd_b, 64)

        @pl.when(s + 2 < ns)
        def _():
            start_weights(s + 2, slot)

    @pl.when(i < nu)
    def _():
        x = _load_packed(x_ref, MOE_ROWS).astype(BF16)
        g = jnp.dot(x, wg_s[...], preferred_element_type=F32)
        u = jnp.dot(x, wu_s[...], preferred_element_type=F32)
        mid = (jax.nn.silu(g) * u).astype(BF16)
        y = jnp.dot(mid, wd_s[...], preferred_element_type=F32)
        row = lax.broadcasted_iota(jnp.int32, (MOE_ROWS, 1), 0)
        mine = (row >= lo_ref[i]) & (row < hi_ref[i])
        new_block = (i == 0) | (blk_ref[i] != blk_ref[jnp.maximum(i - 1, 0)])
        y_acc[...] = jnp.where(mine, y, jnp.where(new_block, 0.0, y_acc[...]))
        _store_packed(y_ref, y_acc[...], MOE_ROWS)


def _experts(n_pairs, seq_of_pair, expert_of_seq, n_seq, blk_of_pair, lo, hi, xs, w_gate, w_up, w_down):
    n_steps = seq_of_pair.shape[0]
    _, d, ff = w_gate.shape
    assert d == 2 * ROW_SUB * LANES, "a packed row must be exactly one (8,128) tile"
    blk = lambda i, nu, sq, es, ns, bk, lo, hi: (bk[jnp.minimum(i, nu[0] - 1)], 0)
    hbm = pl.BlockSpec(memory_space=pl.ANY)
    return pl.pallas_call(
        _expert_body,
        grid_spec=pltpu.PrefetchScalarGridSpec(
            num_scalar_prefetch=7,
            grid=(n_steps,),
            in_specs=[pl.BlockSpec((MOE_ROWS * ROW_SUB, LANES), blk), hbm, hbm, hbm],
            out_specs=pl.BlockSpec((MOE_ROWS * ROW_SUB, LANES), blk),
            scratch_shapes=[pltpu.VMEM((2, d, ff), F32), pltpu.VMEM((2, d, ff), F32), pltpu.VMEM((2, ff, d), F32),
                            pltpu.VMEM((d, ff), BF16), pltpu.VMEM((d, ff), BF16), pltpu.VMEM((ff, d), BF16),
                            pltpu.VMEM((MOE_ROWS, d), F32), pltpu.SemaphoreType.DMA((2,))],
        ),
        out_shape=jax.ShapeDtypeStruct(xs.shape, U32),
        compiler_params=_cparams(("arbitrary",)),
        name="moe_experts",
    )(n_pairs, seq_of_pair, expert_of_seq, n_seq, blk_of_pair, lo, hi, xs, w_gate, w_up, w_down)


def _combine_body(dest_ref, x_ref, info_ref, g_ref, ys_ref, o_ref, buf0, buf1, sem, *, tm):
    i = pl.program_id(0)
    slot = lax.rem(i, 2)
    bufs = (buf0, buf1)

    def gather(tile, sl, wait):
        def body(r, c):
            for k in range(EXPERT_TOPK):
                cp = _row_copy(ys_ref, dest_ref[EXPERT_TOPK * (tile * tm + r) + k], bufs[k].at[sl], r, sem.at[sl])
                if wait:
                    cp.wait()
                else:
                    cp.start(priority=k % 2)
            return c
        lax.fori_loop(0, tm, body, 0, unroll=4)

    @pl.when(i == 0)
    def _():
        gather(0, 0, False)

    @pl.when(i + 1 < pl.num_programs(0))
    def _():
        gather(i + 1, 1 - slot, False)

    gather(i, slot, True)
    info = info_ref[...]
    y = x_ref[...] + (info[:, 2:3] * _load_packed(buf0.at[slot], tm) + info[:, 3:4] * _load_packed(buf1.at[slot], tm))
    ms = jnp.mean(y * y, axis=-1, keepdims=True)
    o_ref[...] = y * lax.rsqrt(ms + RMS_EPS) * g_ref[...]


def _combine(dest, x1, info, g, ys, tm=256):
    t, d = x1.shape
    return pl.pallas_call(
        functools.partial(_combine_body, tm=tm),
        grid_spec=pltpu.PrefetchScalarGridSpec(
            num_scalar_prefetch=1,
            grid=(t // tm,),
            in_specs=[pl.BlockSpec((tm, d), lambda i, dest: (i, 0)),
                      pl.BlockSpec((tm, LANES), lambda i, dest: (i, 0)),
                      pl.BlockSpec((1, d), lambda i, dest: (0, 0)),
                      pl.BlockSpec(memory_space=pl.ANY)],
            out_specs=pl.BlockSpec((tm, d), lambda i, dest: (i, 0)),
            scratch_shapes=[pltpu.VMEM((2, tm * ROW_SUB, LANES), U32), pltpu.VMEM((2, tm * ROW_SUB, LANES), U32),
                            pltpu.SemaphoreType.DMA((2,))],
        ),
        out_shape=jax.ShapeDtypeStruct((t, d), F32),
        compiler_params=_cparams(("arbitrary",)),
        name="moe_combine",
    )(dest, x1, info, g.reshape(1, d), ys)


def _nsa(proj, gate_cols, pe_k, w1_k, w2_k, pe_v, w1_v, w2_v, tab, b, s):
    del gate_cols
    g, dh = NSA_KV_HEADS, HEAD_DIM
    qw = NSA_HEADS * dh
    nc = s // CMP_STRIDE

    slab = proj[:, :, qw:qw + 2 * g * dh].reshape(b, nc, CMP_STRIDE * 2 * g * dh)
    kvc = _compress(slab, pe_k, w1_k, w2_k, pe_v, w1_v, w2_v, g)

    c_start = np.arange(nc)[None, :] * CMP_STRIDE
    n_sel = s // SEL_BLOCK
    sb = np.arange(n_sel)[:, None] * SEL_BLOCK
    overlap = jnp.asarray((c_start < sb + SEL_BLOCK) & (c_start + CMP_BLOCK > sb), BF16)
    o_c, member = _nsa_cmp(proj, kvc, tab, overlap, b, s)

    t = ATT_TILE
    nq = s // t
    nn = _n_near(t)
    nh = ATT_HEADS_PER_STEP
    kblk = qw // (nh // NSA_GROUP * dh)
    per = g // (nh // NSA_GROUP)
    bias_d = _bias_vecs(tab, t, nn + 1)
    o_s = _flash(proj, 0, proj, kblk + 2 * per, proj, kblk + 3 * per, bias_d, NSA_HEADS, NSA_GROUP, nh,
                 _causal_pairs(nq, nn), True, member=member, et=_block_onehot(s, SEL_BLOCK), name="nsa_selected")
    n_back = -(-WINDOW // t)
    bias_w = _bias_vecs(tab, t, n_back + 1, window=WINDOW)
    o_w = _flash(proj, 0, proj, kblk + 4 * per, proj, kblk + 5 * per, bias_w, NSA_HEADS, NSA_GROUP, nh,
                 _window_pairs(nq, n_back), False, name="nsa_window")
    return o_c, o_s, o_w


def _moba(proj, tab, b, s):
    member = _moba_gate(proj, b, s)
    t = ATT_TILE
    nn = _n_near(t)
    nh = ATT_HEADS_PER_STEP
    ng = MOBA_HEADS // nh
    bias_d = _bias_vecs(tab, t, nn + 1)
    return _flash(proj, 0, proj, ng, proj, 2 * ng, bias_d, MOBA_HEADS, 1, nh, _causal_pairs(s // t, nn),
                  True, member=member, et=_block_onehot(s, MOBA_BLOCK), name="moba_attn")


def _moe(x1, g_ffn, w_group, b_group, w_router, b_router, w_gate, w_up, w_down, g_final):
    t, d = x1.shape
    ng, _, epg = w_router.shape
    w_gr = jnp.concatenate([w_group, jnp.transpose(w_router, (1, 0, 2)).reshape(d, ng * epg),
                            jnp.zeros((d, LANES - ng - ng * epg), F32)], axis=1)
    b_gr = jnp.concatenate([b_group, b_router.reshape(-1), jnp.zeros((LANES - ng - ng * epg,), F32)]).reshape(1, LANES)
    h, info, cnt = _route(x1, g_ffn, w_gr, b_gr)
    n_e = ng * epg
    n_assign = t * EXPERT_TOPK
    assert n_assign % MOE_ROWS == 0
    n_blocks = n_assign // MOE_ROWS
    counts = cnt[0, :n_e].astype(jnp.int32)
    end = jnp.cumsum(counts)
    start = end - counts
    expert = info[:, 0:EXPERT_TOPK].astype(jnp.int32)
    rank = info[:, 4:4 + EXPERT_TOPK].astype(jnp.int32)
    e_ids = jnp.arange(n_e, dtype=jnp.int32)
    dest = (jnp.sum(jnp.where(expert[..., None] == e_ids, start, 0), axis=-1) + rank).reshape(-1)
    owns = counts > 0
    seq_of_expert = jnp.cumsum(owns.astype(jnp.int32)) - 1
    n_seq = jnp.sum(owns.astype(jnp.int32)).reshape(1)
    expert_of_seq = jnp.sum(jnp.where(owns[None, :] & (seq_of_expert[None, :] == e_ids[:, None]), e_ids[None, :], 0),
                            axis=1)
    first = start // MOE_ROWS
    last = jnp.where(owns, (end - 1) // MOE_ROWS, first - 1)
    pair_end = jnp.cumsum(last - first + 1)
    pair_start = pair_end - (last - first + 1)
    n_steps = n_blocks + n_e
    p_ids = jnp.arange(n_steps, dtype=jnp.int32)
    e_of_pair = jnp.minimum(jnp.sum((pair_end[None, :] <= p_ids[:, None]).astype(jnp.int32), axis=1), n_e - 1)
    pick = e_of_pair[:, None] == e_ids[None, :]
    lookup = lambda v: jnp.sum(jnp.where(pick, v[None, :], 0), axis=1)
    blk_of_pair = jnp.clip(lookup(first) + p_ids - lookup(pair_start), 0, n_blocks - 1)
    lo = jnp.clip(lookup(start) - blk_of_pair * MOE_ROWS, 0, MOE_ROWS)
    hi = jnp.clip(lookup(end) - blk_of_pair * MOE_ROWS, 0, MOE_ROWS)
    seq_of_pair = lookup(seq_of_expert)
    n_pairs = pair_end[-1:].astype(jnp.int32)
    xs = _dispatch(dest, h)
    ys = _experts(n_pairs, seq_of_pair, expert_of_seq, n_seq, blk_of_pair, lo, hi, xs, w_gate, w_up, w_down)
    return _combine(dest, x1, info, g_final, ys)


def kernel(x, rel_bias, norm_mix, w_in, cmp_pe_k, cmp_w1_k, cmp_w2_k, cmp_pe_v, cmp_w1_v, cmp_w2_v, w_up_nsa,
           w_up_moba, w_out, norm_ffn, w_group, b_group, w_router, b_router, w_exp_gate, w_exp_up, w_exp_down,
           final_norm):
    b, s, d = x.shape
    t = b * s
    depth = w_in.shape[0]
    tab_a = rel_bias[:, :NSA_HEADS]
    tab_b = rel_bias[:, NSA_HEADS:]
    a_cols = NSA_HEADS * HEAD_DIM + 6 * NSA_KV_HEADS * HEAD_DIM
    gate_cols = 3 * NSA_HEADS
    b_cols = 3 * MOBA_HEADS * HEAD_DIM
    xt = x.reshape(t, d)
    out = None
    for l in range(depth):
        h = _rmsnorm(xt, norm_mix[l], BF16)
        wt = jnp.swapaxes(w_in[l], 0, 1)
        b_col0 = a_cols + gate_cols
        proj_a = _matmul(h, wt, 0, a_cols, BF16, scaled_cols=NSA_HEADS * HEAD_DIM, col_scale=Q_SCALE,
                         w_transposed=True, name="in_proj_a").reshape(b, s, a_cols)
        gate_a = _matmul(h, wt, a_cols, LANES, F32, w_transposed=True, tn=LANES, name="in_proj_gate")
        proj_b = _matmul(h, wt, b_col0, b_cols, BF16, scaled_cols=MOBA_HEADS * HEAD_DIM, col_scale=Q_SCALE,
                         w_transposed=True, name="in_proj_b").reshape(b, s, b_cols)
        gm = _matmul(h, wt, b_col0 + b_cols, 2 * d, BF16, w_transposed=True, name="in_proj_gm")
        o_c, o_s, o_w = _nsa(proj_a, gate_cols, cmp_pe_k[l], cmp_w1_k[l], cmp_w2_k[l],
                             cmp_pe_v[l], cmp_w1_v[l], cmp_w2_v[l], tab_a, b, s)
        o_b = _moba(proj_b, tab_b, b, s)
        merged = _merge(o_c.reshape(t, -1), o_s.reshape(t, -1), o_w.reshape(t, -1), gate_a,
                        o_b.reshape(t, -1), gm, w_up_nsa[l], w_up_moba[l])
        x1 = _matmul(merged, w_out[l], 0, d, F32, res=xt, name="out_proj")
        assert l == depth - 1, "only the last layer's MoE is fused with the final norm"
        out = _moe(x1, norm_ffn[l], w_group[l], b_group[l], w_router[l], b_router[l],
                   w_exp_gate[l], w_exp_up[l], w_exp_down[l], final_norm)
    return out.reshape(b, s, d)
```

```python
import functools
import math

import numpy as np
import jax
import jax.numpy as jnp
from jax import lax
from jax.experimental import pallas as pl
from jax.experimental.pallas import tpu as pltpu

F32 = jnp.float32
BF16 = jnp.bfloat16

HEAD_DIM = 128
NSA_HEADS = 8
NSA_KV_HEADS = 2
NSA_GROUP = NSA_HEADS // NSA_KV_HEADS
CMP_BLOCK = 32
CMP_STRIDE = 16
SEL_BLOCK = 64
SEL_TOPN = 16
WINDOW = 512
FORCED_SCORE = 1e4
MOBA_HEADS = 8
MOBA_BLOCK = 256
MOBA_TOPK = 3
REL_BUCKETS = 32
REL_MAX_DIST = 128
N_GROUPS = 8
EXPERTS_PER_GROUP = 8
N_EXPERTS = N_GROUPS * EXPERTS_PER_GROUP
EXPERT_TOPK = 2
RMS_EPS = 1e-6

LANES = 128
ATT_TILE = 512
ATT_HEADS_PER_STEP = 8
ATT_VMEM_LIMIT = 56 * 1024 * 1024
MOE_ROWS = 256
MASK_BIG = 1e30
M_INIT = -3e38
LOG2E = math.log2(math.e)
Q_SCALE = HEAD_DIM ** -0.5 * LOG2E
VMEM_LIMIT = 48 * 1024 * 1024


def _cparams(sem, vmem=VMEM_LIMIT, flags=None):
    return pltpu.CompilerParams(dimension_semantics=sem, vmem_limit_bytes=vmem, flags=flags)


def _rmsnorm_body(x_ref, g_ref, o_ref):
    x = x_ref[...]
    ms = jnp.mean(x * x, axis=-1, keepdims=True)
    o_ref[...] = (x * lax.rsqrt(ms + RMS_EPS) * g_ref[...]).astype(o_ref.dtype)


def _rmsnorm(x, g, out_dtype, tm=512):
    t, d = x.shape
    return pl.pallas_call(
        _rmsnorm_body,
        grid=(t // tm,),
        in_specs=[pl.BlockSpec((tm, d), lambda i: (i, 0)),
                  pl.BlockSpec((1, d), lambda i: (0, 0))],
        out_specs=pl.BlockSpec((tm, d), lambda i: (i, 0)),
        out_shape=jax.ShapeDtypeStruct((t, d), out_dtype),
        compiler_params=_cparams(("parallel",)),
        name="rmsnorm",
    )(x, g.reshape(1, d))


def _mm_body(*refs, has_res, n_scaled, col_scale, w_transposed):
    if has_res:
        a_ref, w_ref, r_ref, o_ref = refs
    else:
        a_ref, w_ref, o_ref = refs
    w = w_ref[...].astype(BF16)
    if w_transposed:
        acc = lax.dot_general(a_ref[...], w, (((1,), (1,)), ((), ())), preferred_element_type=F32)
    else:
        acc = jnp.dot(a_ref[...], w, preferred_element_type=F32)
    if n_scaled:
        acc = acc * jnp.where(pl.program_id(1) < n_scaled, col_scale, 1.0)
    if has_res:
        acc = acc + r_ref[...]
    o_ref[...] = acc.astype(o_ref.dtype)


def _matmul(a, w, col0, ncols, out_dtype, res=None, scaled_cols=0, col_scale=1.0, w_transposed=False,
            skip_at=0, skip_cols=0, tm=2048, tn=512, name="matmul"):
    t, k = a.shape
    tn = min(tn, ncols)
    tm = min(tm, t)
    assert ncols % tn == 0 and t % tm == 0 and scaled_cols % tn == 0 and skip_at % tn == 0
    if w_transposed:
        assert col0 % 8 == 0 and skip_cols % 8 == 0
        gap_block = skip_at // tn if skip_cols else ncols // tn
        w_spec = pl.BlockSpec(
            (pl.Element(tn), pl.Element(k)),
            lambda i, j: (pl.multiple_of(col0 + j * tn + jnp.where(j >= gap_block, skip_cols, 0), 8), 0))
    else:
        assert col0 % tn == 0
        off = col0 // tn
        w_spec = pl.BlockSpec((k, tn), lambda i, j: (0, j + off))
    in_specs = [pl.BlockSpec((tm, k), lambda i, j: (i, 0)), w_spec]
    args = [a, w]
    if res is not None:
        in_specs.append(pl.BlockSpec((tm, tn), lambda i, j: (i, j)))
        args.append(res)
    return pl.pallas_call(
        functools.partial(_mm_body, has_res=res is not None, n_scaled=scaled_cols // tn, col_scale=col_scale,
                          w_transposed=w_transposed),
        grid=(t // tm, ncols // tn),
        in_specs=in_specs,
        out_specs=pl.BlockSpec((tm, tn), lambda i, j: (i, j)),
        out_shape=jax.ShapeDtypeStruct((t, ncols), out_dtype),
        compiler_params=_cparams(("parallel", "parallel")),
        name=name,
    )(*args)


def _mm_slab_body(a_ref, w_ref, o_ref, acc_ref, *, per):
    acc = lax.dot_general(a_ref[...], w_ref[...].astype(BF16), (((1,), (1,)), ((), ())),
                          preferred_element_type=F32)
    n_chunks, tm, _ = acc_ref.shape
    tn = n_chunks * LANES
    for c in range(n_chunks):
        acc_ref[c] = acc[:, c * LANES:(c + 1) * LANES]
    for r in range(per):
        for c in range(n_chunks):
            o_ref[:, r * tn + c * LANES:r * tn + (c + 1) * LANES] = (
                acc_ref[c, pl.ds(r, tm // per, stride=per), :].astype(o_ref.dtype))


def _matmul_slab(a, wt, col0, ncols, per, tm=2048):
    t, k = a.shape
    tm = min(tm, t)
    assert t % tm == 0 and tm % (per * 8) == 0 and col0 % 8 == 0
    return pl.pallas_call(
        functools.partial(_mm_slab_body, per=per),
        grid=(t // tm,),
        in_specs=[pl.BlockSpec((tm, k), lambda i: (i, 0)),
                  pl.BlockSpec((pl.Element(ncols), pl.Element(k)), lambda i: (col0, 0))],
        out_specs=pl.BlockSpec((tm // per, per * ncols), lambda i: (i, 0)),
        out_shape=jax.ShapeDtypeStruct((t // per, per * ncols), BF16),
        scratch_shapes=[pltpu.VMEM((ncols // LANES, tm, LANES), F32)],
        compiler_params=_cparams(("parallel",)),
        name="in_proj_slab",
    )(a, wt)


def _compress_body(u_ref, pek_ref, w1k_ref, w2k_ref, pev_ref, w1v_ref, w2v_ref, o_ref, *, nc, g):
    dh = HEAD_DIM
    tok_w = 2 * g * dh
    for kv, (pe_ref, w1_ref, w2_ref) in enumerate(((pek_ref, w1k_ref, w2k_ref), (pev_ref, w1v_ref, w2v_ref))):
        w1 = w1_ref[...].astype(BF16)
        w2 = w2_ref[...].astype(BF16)
        half = CMP_STRIDE * dh
        peb = jnp.dot(pe_ref[...].astype(BF16), w1, preferred_element_type=F32)[0:1]
        for gi in range(g):
            a = b = None
            for r in range(CMP_STRIDE):
                c0 = r * tok_w + (kv * g + gi) * dh
                piece = u_ref[:, c0:c0 + dh]
                da = jnp.dot(piece, w1[r * dh:(r + 1) * dh], preferred_element_type=F32)
                db = jnp.dot(piece, w1[half + r * dh:half + (r + 1) * dh], preferred_element_type=F32)
                a = da if a is None else a + da
                b = db if b is None else b + db
            pre = a + pltpu.roll(b, nc - 1, 0) + peb
            hid = jax.nn.gelu(pre)
            o_ref[kv * g + gi] = jnp.dot(hid.astype(BF16), w2, preferred_element_type=F32).astype(o_ref.dtype)


def _compress(u, pe_k, w1_k, w2_k, pe_v, w1_v, w2_v, g):
    b, nc, kk = u.shape
    hid = w1_k.shape[1]
    dh = w2_k.shape[1]
    pe16 = lambda pe: jnp.broadcast_to(pe.reshape(1, -1), (16, pe.size))
    full = lambda shape: pl.BlockSpec(shape, lambda i: (0,) * len(shape))
    return pl.pallas_call(
        functools.partial(_compress_body, nc=nc, g=g),
        grid=(b,),
        in_specs=[pl.BlockSpec((None, nc, kk), lambda i: (i, 0, 0)),
                  full((16, CMP_BLOCK * dh)), full((CMP_BLOCK * dh, hid)), full((hid, dh)),
                  full((16, CMP_BLOCK * dh)), full((CMP_BLOCK * dh, hid)), full((hid, dh))],
        out_specs=pl.BlockSpec((None, 2 * g, nc, dh), lambda i: (i, 0, 0, 0)),
        out_shape=jax.ShapeDtypeStruct((b, 2 * g, nc, dh), BF16),
        compiler_params=_cparams(("parallel",)),
        name="nsa_compress",
    )(u, pe16(pe_k), w1_k, w2_k, pe16(pe_v), w1_v, w2_v)


def _split3(x):
    p1 = x.astype(BF16)
    r = x - p1.astype(F32)
    p2 = r.astype(BF16)
    p3 = (r - p2.astype(F32)).astype(BF16)
    return p1, p2, p3


def _rank_count(score, n_rows):
    groups = []
    for g0 in range(0, n_rows, 8):
        sg = score[g0:min(g0 + 8, n_rows), :]
        n_iota = g0 + lax.broadcasted_iota(jnp.int32, sg.shape, 0)
        cnt = jnp.zeros(sg.shape, F32)
        for m in range(n_rows):
            row = score[m:m + 1, :]
            if m < g0:
                beats = row >= sg
            elif m >= g0 + 8:
                beats = row > sg
            else:
                tie = jnp.where(n_iota > m, 1.0, 0.0)
                beats = jnp.where(row > sg, 1.0, jnp.where(row == sg, tie, 0.0)) > 0.5
            cnt = cnt + jnp.where(beats, 1.0, 0.0)
        groups.append(cnt)
    return jnp.concatenate(groups, axis=0) if len(groups) > 1 else groups[0]


def _nsa_cmp_body(q_ref, kc_ref, vc_ref, bias_ref, ov_ref, oc_ref, mem_ref, *, tq, nc, n_sel):
    t0 = pl.program_id(2) * tq
    kc = kc_ref[...]
    vc = vc_ref[...]
    t_idx = t0 + lax.broadcasted_iota(jnp.int32, (tq, nc), 0)
    c_idx = lax.broadcasted_iota(jnp.int32, (tq, nc), 1)
    dist = t_idx - (c_idx * CMP_STRIDE + (CMP_BLOCK - 1))
    n_k = REL_MAX_DIST // CMP_STRIDE
    kidx = jnp.where(dist < 0, n_k + 1, jnp.minimum(lax.shift_right_logical(dist, 4), n_k))
    assert CMP_STRIDE == 16
    psum = jnp.zeros((tq, nc), F32)
    for j in range(NSA_GROUP):
        hs = slice(j * HEAD_DIM, (j + 1) * HEAD_DIM)
        gt = bias_ref[j] * LOG2E
        bias = jnp.concatenate([jnp.take_along_axis(gt, kidx[:, c0:c0 + LANES], axis=1)
                                for c0 in range(0, nc, LANES)], axis=1)
        s = lax.dot_general(q_ref[:, hs], kc, (((1,), (1,)), ((), ())), preferred_element_type=F32) + bias
        m = jnp.max(s, axis=-1, keepdims=True)
        m = jnp.where(m > -0.5 * MASK_BIG, m, 0.0)
        p = jnp.exp2(s - m)
        d = jnp.sum(p, axis=-1, keepdims=True)
        p = p / jnp.where(d > 0, d, 1.0)
        oc_ref[:, hs] = jnp.dot(p.astype(BF16), vc, preferred_element_type=F32).astype(oc_ref.dtype)
        psum = psum + p
    ov = ov_ref[...]
    nt = (((1,), (1,)), ((), ()))
    p1, p2, p3 = _split3(psum)
    psel = (lax.dot_general(ov, p1, nt, preferred_element_type=F32)
            + lax.dot_general(ov, p2, nt, preferred_element_type=F32)
            + lax.dot_general(ov, p3, nt, preferred_element_type=F32))
    n_idx = lax.broadcasted_iota(jnp.int32, (n_sel, tq), 0)
    tt = t0 + lax.broadcasted_iota(jnp.int32, (n_sel, tq), 1)
    cur = tt // SEL_BLOCK
    forced = jnp.where(n_idx == 0, 1.0, jnp.where(n_idx == cur, 1.0, jnp.where(n_idx == cur - 1, 1.0, 0.0)))
    score = jnp.where(forced > 0.5, FORCED_SCORE, jnp.where(n_idx * SEL_BLOCK <= tt, psel, -1.0))
    cnt = _rank_count(score, n_sel)
    member = jnp.where(cnt < float(min(SEL_TOPN, n_sel)), 1.0, 0.0)
    if n_sel < LANES:
        member = jnp.concatenate([member, jnp.zeros((LANES - n_sel, tq), F32)], axis=0)
    mem_ref[...] = member.T.astype(mem_ref.dtype)


def _nsa_cmp(proj, kvc, tab, overlap, b, s, tq=256):
    g = NSA_KV_HEADS
    nc = kvc.shape[2]
    n_sel = s // SEL_BLOCK
    gw = NSA_GROUP * HEAD_DIM
    assert tq % CMP_STRIDE == 0 and REL_MAX_DIST % CMP_STRIDE == 0
    n_k = REL_MAX_DIST // CMP_STRIDE
    rho = (np.arange(tq)[:, None] - (CMP_BLOCK - 1)) % CMP_STRIDE
    dd = np.concatenate([rho + CMP_STRIDE * np.arange(n_k)[None, :], np.full((tq, 1), REL_MAX_DIST)], axis=1)
    gtab = jnp.transpose(_bias_of_dist(tab, dd), (2, 0, 1)).astype(F32)
    bias_c = jnp.concatenate([gtab, jnp.full(gtab.shape[:2] + (1,), -MASK_BIG, F32),
                              jnp.zeros(gtab.shape[:2] + (LANES - n_k - 2,), F32)], axis=2)
    body = functools.partial(_nsa_cmp_body, tq=tq, nc=nc, n_sel=n_sel)
    return pl.pallas_call(
        body,
        grid=(b, g, s // tq),
        in_specs=[pl.BlockSpec((None, tq, gw), lambda bi, gi, i: (bi, i, gi)),
                  pl.BlockSpec((None, None, nc, HEAD_DIM), lambda bi, gi, i: (bi, gi, 0, 0)),
                  pl.BlockSpec((None, None, nc, HEAD_DIM), lambda bi, gi, i: (bi, g + gi, 0, 0)),
                  pl.BlockSpec((NSA_GROUP, tq, LANES), lambda bi, gi, i: (gi, 0, 0)),
                  pl.BlockSpec((n_sel, nc), lambda bi, gi, i: (0, 0))],
        out_specs=[pl.BlockSpec((None, tq, gw), lambda bi, gi, i: (bi, i, gi)),
                   pl.BlockSpec((None, None, tq, LANES), lambda bi, gi, i: (bi, gi, i, 0))],
        out_shape=[jax.ShapeDtypeStruct((b, s, NSA_HEADS * HEAD_DIM), BF16),
                   jax.ShapeDtypeStruct((b, g, s, LANES), BF16)],
        compiler_params=_cparams(("parallel", "parallel", "parallel")),
        name="nsa_cmp_select",
    )(proj, kvc, kvc, bias_c, overlap)


def _moba_gate_body(q_ref, k_ref, mem_ref, *, s, nblk):
    k = k_ref[...].astype(F32)
    kmean = jnp.mean(k.reshape(nblk, MOBA_BLOCK, HEAD_DIM), axis=1)
    k1 = kmean.astype(BF16)
    k2 = (kmean - k1.astype(F32)).astype(BF16)
    q = q_ref[...]
    nt = (((1,), (1,)), ((), ()))
    gate = (lax.dot_general(k1, q, nt, preferred_element_type=F32)
            + lax.dot_general(k2, q, nt, preferred_element_type=F32))
    n_idx = lax.broadcasted_iota(jnp.int32, (nblk, s), 0)
    own = lax.broadcasted_iota(jnp.int32, (nblk, s), 1) // MOBA_BLOCK
    past = n_idx < own
    score = jnp.where(past, gate, -MASK_BIG)
    cnt = _rank_count(score, nblk)
    n_top = max(1, min(MOBA_TOPK, nblk - 1))
    sel = jnp.where(past, jnp.where(cnt < float(n_top), 1.0, 0.0), 0.0)
    member = jnp.where(n_idx == own, 1.0, sel)
    member = jnp.concatenate([member, jnp.zeros((LANES - nblk, s), F32)], axis=0)
    mem_ref[...] = member.T.astype(mem_ref.dtype)


def _moba_gate(proj, b, s):
    h = MOBA_HEADS
    nblk = s // MOBA_BLOCK
    return pl.pallas_call(
        functools.partial(_moba_gate_body, s=s, nblk=nblk),
        grid=(b, h),
        in_specs=[pl.BlockSpec((None, s, HEAD_DIM), lambda bi, hi: (bi, 0, hi)),
                  pl.BlockSpec((None, s, HEAD_DIM), lambda bi, hi: (bi, 0, h + hi))],
        out_specs=pl.BlockSpec((None, None, s, LANES), lambda bi, hi: (bi, hi, 0, 0)),
        out_shape=jax.ShapeDtypeStruct((b, h, s, LANES), BF16),
        compiler_params=_cparams(("parallel", "parallel")),
        name="moba_gate",
    )(proj, proj)


def _flash_body(qi_ref, ki_ref, bo_ref, fl_ref, *refs, nh, ratio, nm, n_near, has_far):
    if nm:
        (q_ref, k_ref, v_ref, bvec_ref, mem_ref, et_ref, o_ref,
         m_ref, l_ref, acc_ref, sh_ref, al_ref, bias_ref, s_ref, p_ref) = refs
    else:
        q_ref, k_ref, v_ref, bvec_ref, o_ref, m_ref, l_ref, acc_ref, sh_ref, al_ref, bias_ref, s_ref, p_ref = refs
    del qi_ref, ki_ref
    p = pl.program_id(2)
    flag = fl_ref[p]
    bo = bo_ref[p]
    t = q_ref.shape[0]
    rows = 64

    @pl.when(p == 0)
    def _():
        for h in range(nh):
            for o in range(n_near):
                vec = bvec_ref[h, o][0:1, :] * LOG2E
                for rc in range(t // rows):
                    x = pltpu.roll(jnp.broadcast_to(vec, (rows, 2 * t)), rc * rows, 1, stride=1, stride_axis=0)
                    bias_ref[h, o, rc * rows:(rc + 1) * rows, :] = x[:, :t]

    @pl.when((flag & 1) != 0)
    def _():
        m_ref[...] = jnp.full(m_ref.shape, M_INIT, F32)
        l_ref[...] = jnp.zeros(l_ref.shape, F32)
        acc_ref[...] = jnp.zeros(acc_ref.shape, F32)

    nt = (((1,), (1,)), ((), ()))
    reps = t // LANES

    def step(near):
        def pass1(h):
            hs = slice(h * HEAD_DIM, (h + 1) * HEAD_DIM)
            kv = h // ratio
            ks = slice(kv * HEAD_DIM, (kv + 1) * HEAD_DIM)
            q = q_ref[:, hs]
            k = k_ref[:, ks]
            if nm:
                mneg = mem_ref[h // (nh // nm)] - 1.0
                q = jnp.concatenate([q, mneg.astype(BF16)], axis=1)
                k = jnp.concatenate([k, et_ref[...]], axis=1)
            sc = lax.dot_general(q, k, nt, preferred_element_type=F32)
            m_prev = m_ref[h]
            if near:
                sc = sc + bias_ref[h, bo]
                m_new = jnp.maximum(m_prev, jnp.max(sc, axis=-1, keepdims=True))
                sh_ref[h] = m_new
            else:
                cfar = bvec_ref[h, n_near][0:1, 0:LANES] * LOG2E
                m_new = jnp.maximum(m_prev, jnp.max(sc, axis=-1, keepdims=True) + cfar)
                sh_ref[h] = m_new - cfar
            s_ref[h] = sc
            al_ref[h] = jnp.exp2(m_prev - m_new)
            m_ref[h] = m_new

        def pass2(h):
            ks = slice(h // ratio * HEAD_DIM, (h // ratio + 1) * HEAD_DIM)
            for rc in range(t // rows):
                rs = slice(rc * rows, (rc + 1) * rows)
                pm = jnp.exp2(s_ref[h, rs, :] - jnp.tile(sh_ref[h, rs, :], (1, reps)))
                l_ref[h, rs, :] = al_ref[h, rs, :] * l_ref[h, rs, :] + jnp.sum(pm, axis=-1, keepdims=True)
                p_ref[h, rs, :] = pm.astype(BF16)
            acc_ref[h] = al_ref[h] * acc_ref[h] + jnp.dot(p_ref[h], v_ref[:, ks], preferred_element_type=F32)

        for h in range(nh):
            pass1(h)
        for h in range(nh):
            pass2(h)

    if has_far:
        pl.when(bo < n_near)(lambda: step(True))
        pl.when(bo >= n_near)(lambda: step(False))
    else:
        step(True)

    @pl.when((flag & 2) != 0)
    def _():
        for h in range(nh):
            l = l_ref[h]
            o_ref[:, h * HEAD_DIM:(h + 1) * HEAD_DIM] = (acc_ref[h] / jnp.where(l > 0, l, 1.0)).astype(o_ref.dtype)


def _flash(q_arr, q_off, k_arr, k_off, v_arr, v_off, bias, n_heads, ratio, nh, pairs, has_far, member=None,
           et=None, name="flash"):
    b, s, _ = q_arr.shape
    t = ATT_TILE
    nkv = nh // ratio
    ng = n_heads // nh
    qi = jnp.asarray([p[0] for p in pairs], jnp.int32)
    ki = jnp.asarray([p[1] for p in pairs], jnp.int32)
    bo = jnp.asarray([p[2] for p in pairs], jnp.int32)
    fl = jnp.asarray([p[3] for p in pairs], jnp.int32)
    nb = bias.shape[1]
    nm = 0
    in_specs = [
        pl.BlockSpec((None, t, nh * HEAD_DIM), lambda bi, gi, p, qi, ki, bo, fl: (bi, qi[p], q_off + gi)),
        pl.BlockSpec((None, t, nkv * HEAD_DIM), lambda bi, gi, p, qi, ki, bo, fl: (bi, ki[p], k_off + gi)),
        pl.BlockSpec((None, t, nkv * HEAD_DIM), lambda bi, gi, p, qi, ki, bo, fl: (bi, ki[p], v_off + gi)),
        pl.BlockSpec((nh, nb, 8, 2 * t), lambda bi, gi, p, qi, ki, bo, fl: (gi, 0, 0, 0)),
    ]
    args = [q_arr, k_arr, v_arr, bias]
    if member is not None:
        nm = member.shape[1] // ng
        in_specs += [
            pl.BlockSpec((None, nm, t, LANES), lambda bi, gi, p, qi, ki, bo, fl: (bi, gi, qi[p], 0)),
            pl.BlockSpec((t, LANES), lambda bi, gi, p, qi, ki, bo, fl: (ki[p], 0)),
        ]
        args += [member, et]
    n_near = nb - 1 if has_far else nb
    body = functools.partial(_flash_body, nh=nh, ratio=ratio, nm=nm, n_near=n_near, has_far=has_far)
    return pl.pallas_call(
        body,
        grid_spec=pltpu.PrefetchScalarGridSpec(
            num_scalar_prefetch=4,
            grid=(b, ng, len(pairs)),
            in_specs=in_specs,
            out_specs=pl.BlockSpec((None, t, nh * HEAD_DIM), lambda bi, gi, p, qi, ki, bo, fl: (bi, qi[p], gi)),
            scratch_shapes=[pltpu.VMEM((nh, t, LANES), F32)] * 5 + [pltpu.VMEM((nh, n_near, t, t), F32),
                                                                    pltpu.VMEM((nh, t, t), F32),
                                                                    pltpu.VMEM((nh, t, t), BF16)],
        ),
        out_shape=jax.ShapeDtypeStruct((b, s, n_heads * HEAD_DIM), BF16),
        compiler_params=_cparams(("parallel", "parallel", "arbitrary"), vmem=ATT_VMEM_LIMIT),
        name=name,
    )(qi, ki, bo, fl, *args)


def _rel_bucket(dist):
    n = jnp.maximum(jnp.asarray(dist, jnp.int32), 0)
    max_exact = REL_BUCKETS // 2
    nf = jnp.maximum(n, 1).astype(jnp.float32)
    large = max_exact + (jnp.log(nf / max_exact) / math.log(REL_MAX_DIST / max_exact)
                         * (REL_BUCKETS - max_exact)).astype(jnp.int32)
    return jnp.where(n < max_exact, n, jnp.minimum(large, REL_BUCKETS - 1))


def _bias_of_dist(tab, dist):
    hit = _rel_bucket(dist)[..., None, None] == jnp.arange(REL_BUCKETS)[:, None]
    return jnp.sum(jnp.where(hit, tab, 0.0), axis=-2)


def _n_near(t):
    return -(-(REL_MAX_DIST - 1 + t) // t)


def _bias_vecs(tab, t, n_off, window=None):
    k = np.arange(2 * t)[None, :]
    dist = np.arange(n_off)[:, None] * t + np.where(k < t, -k, 2 * t - k)
    ok = dist >= 0
    if window is not None:
        ok &= dist < window
    bias = jnp.where(jnp.asarray(ok)[..., None], _bias_of_dist(tab, dist), -MASK_BIG)
    bias = jnp.transpose(bias, (2, 0, 1)).astype(F32)
    return jnp.broadcast_to(bias[:, :, None, :], (bias.shape[0], n_off, 8, 2 * t))


def _causal_pairs(nq, n_near):
    pairs = []
    for qi in range(nq):
        for ki in range(qi + 1):
            pairs.append((qi, ki, min(qi - ki, n_near), (1 if ki == 0 else 0) | (2 if ki == qi else 0)))
    return pairs


def _window_pairs(nq, n_back):
    pairs = []
    for qi in range(nq):
        lo = max(0, qi - n_back)
        for ki in range(lo, qi + 1):
            pairs.append((qi, ki, qi - ki, (1 if ki == lo else 0) | (2 if ki == qi else 0)))
    return pairs


def _block_onehot(s, blk):
    return jnp.asarray(np.where(np.arange(s)[:, None] // blk == np.arange(LANES)[None, :], MASK_BIG, 0.0), BF16)


def _merge_body(oc_ref, os_ref, ow_ref, gl_ref, ob_ref, gma_ref, gmb_ref, wa_ref, wb_ref, o_ref, oa_ref):
    @pl.when(pl.program_id(1) == 0)
    def _():
        gates = jax.nn.sigmoid(gl_ref[...])
        for h in range(NSA_HEADS):
            hs = slice(h * HEAD_DIM, (h + 1) * HEAD_DIM)
            mix = (gates[:, 3 * h:3 * h + 1] * oc_ref[:, hs].astype(F32)
                   + gates[:, 3 * h + 1:3 * h + 2] * os_ref[:, hs].astype(F32)
                   + gates[:, 3 * h + 2:3 * h + 3] * ow_ref[:, hs].astype(F32))
            oa_ref[:, hs] = mix.astype(BF16)

    ya = jnp.dot(oa_ref[...], wa_ref[...].astype(BF16), preferred_element_type=F32)
    yb = jnp.dot(ob_ref[...], wb_ref[...].astype(BF16), preferred_element_type=F32)
    ga = jax.nn.sigmoid(gma_ref[...].astype(F32))
    gb = jax.nn.sigmoid(gmb_ref[...].astype(F32))
    o_ref[...] = (ga * ya + gb * yb).astype(o_ref.dtype)


def _merge(o_c, o_s, o_w, gate_logits, o_b, gm, w_up_a, w_up_b, tm=1024, tn=512):
    t, ka = o_c.shape
    kb = o_b.shape[1]
    d = w_up_a.shape[1]
    tm = min(tm, t)
    nj = d // tn
    row = lambda i, j: (i, 0)
    return pl.pallas_call(
        _merge_body,
        grid=(t // tm, nj),
        in_specs=[pl.BlockSpec((tm, ka), row), pl.BlockSpec((tm, ka), row), pl.BlockSpec((tm, ka), row),
                  pl.BlockSpec((tm, LANES), row), pl.BlockSpec((tm, kb), row),
                  pl.BlockSpec((tm, tn), lambda i, j: (i, j)),
                  pl.BlockSpec((tm, tn), lambda i, j: (i, j + nj)),
                  pl.BlockSpec((ka, tn), lambda i, j: (0, j)),
                  pl.BlockSpec((kb, tn), lambda i, j: (0, j))],
        out_specs=pl.BlockSpec((tm, tn), lambda i, j: (i, j)),
        out_shape=jax.ShapeDtypeStruct((t, d), BF16),
        scratch_shapes=[pltpu.VMEM((tm, ka), BF16)],
        compiler_params=_cparams(("parallel", "arbitrary")),
        name="merge_up",
    )(o_c, o_s, o_w, gate_logits, o_b, gm, gm, w_up_a, w_up_b)


def _route_body(x_ref, g_ref, w_ref, b_ref, h_ref, info_ref, cnt_ref, carry_ref, *, tm):
    @pl.when(pl.program_id(0) == 0)
    def _():
        carry_ref[...] = jnp.zeros(carry_ref.shape, F32)

    x = x_ref[...]
    ms = jnp.mean(x * x, axis=-1, keepdims=True)
    h = x * lax.rsqrt(ms + RMS_EPS) * g_ref[...]
    _store_packed(h_ref, h, tm)
    w = w_ref[...]
    h1 = h.astype(BF16)
    h2 = (h - h1.astype(F32)).astype(BF16)
    w1 = w.astype(BF16)
    w2 = (w - w1.astype(F32)).astype(BF16)
    logits = (jnp.dot(h1, w1, preferred_element_type=F32) + jnp.dot(h1, w2, preferred_element_type=F32)
              + jnp.dot(h2, w1, preferred_element_type=F32)) + b_ref[...]
    lane = lax.broadcasted_iota(jnp.int32, (tm, LANES), 1)
    lanef = lane.astype(F32)

    is_g = lane < N_GROUPS
    gl = jnp.where(is_g, logits, -MASK_BIG)
    ge = jnp.where(is_g, jnp.exp(gl - jnp.max(gl, axis=-1, keepdims=True)), 0.0)
    gp = ge / jnp.sum(ge, axis=-1, keepdims=True)
    g_val = jnp.max(gp, axis=-1, keepdims=True)
    g_idx = jnp.min(jnp.where(gp == g_val, lanef, float(LANES)), axis=-1, keepdims=True)

    lane_grp = ((lane - N_GROUPS) // EXPERTS_PER_GROUP).astype(F32)
    in_e = jnp.where(lane >= N_GROUPS, jnp.where(lane < N_GROUPS + N_EXPERTS, 1.0, 0.0), 0.0)
    is_e = jnp.where(lane_grp == g_idx, in_e, 0.0) > 0.5
    el = jnp.where(is_e, logits, -MASK_BIG)
    ee = jnp.where(is_e, jnp.exp(el - jnp.max(el, axis=-1, keepdims=True)), 0.0)
    ep = jnp.where(is_e, ee / jnp.sum(ee, axis=-1, keepdims=True), -1.0)
    v1 = jnp.max(ep, axis=-1, keepdims=True)
    l1 = jnp.min(jnp.where(ep == v1, lanef, float(LANES)), axis=-1, keepdims=True)
    ep2 = jnp.where(lanef == l1, -1.0, ep)
    v2 = jnp.max(ep2, axis=-1, keepdims=True)
    l2 = jnp.min(jnp.where(ep2 == v2, lanef, float(LANES)), axis=-1, keepdims=True)
    vs = v1 + v2
    wt1 = g_val * v1 / vs
    wt2 = g_val * v2 / vs
    e1 = l1 - float(N_GROUPS)
    e2 = l2 - float(N_GROUPS)

    oh = jnp.where(lanef == e1, 1.0, jnp.where(lanef == e2, 1.0, 0.0))
    r_i = lax.broadcasted_iota(jnp.int32, (tm, tm), 0)
    c_i = lax.broadcasted_iota(jnp.int32, (tm, tm), 1)
    tri = jnp.where(r_i > c_i, 1.0, 0.0).astype(BF16)
    base = jnp.dot(tri, oh.astype(BF16), preferred_element_type=F32) + carry_ref[...]
    r1 = jnp.sum(jnp.where(lanef == e1, base, 0.0), axis=-1, keepdims=True)
    r2 = jnp.sum(jnp.where(lanef == e2, base, 0.0), axis=-1, keepdims=True)
    carry_ref[...] = carry_ref[...] + jnp.sum(oh, axis=0, keepdims=True)
    cnt_ref[...] = jnp.broadcast_to(carry_ref[...], cnt_ref.shape)
    info = jnp.where(lane == 0, e1, jnp.where(lane == 1, e2, jnp.where(lane == 2, wt1, jnp.where(
        lane == 3, wt2, jnp.where(lane == 4, r1, jnp.where(lane == 5, r2, 0.0))))))
    info_ref[...] = info


def _route(x1, g, w_gr, b_gr, tm=512):
    t, d = x1.shape
    return pl.pallas_call(
        functools.partial(_route_body, tm=tm),
        grid=(t // tm,),
        in_specs=[pl.BlockSpec((tm, d), lambda i: (i, 0)),
                  pl.BlockSpec((1, d), lambda i: (0, 0)),
                  pl.BlockSpec((d, LANES), lambda i: (0, 0)),
                  pl.BlockSpec((1, LANES), lambda i: (0, 0))],
        out_specs=[pl.BlockSpec((tm * ROW_SUB, LANES), lambda i: (i, 0)),
                   pl.BlockSpec((tm, LANES), lambda i: (i, 0)),
                   pl.BlockSpec((8, LANES), lambda i: (0, 0))],
        out_shape=[jax.ShapeDtypeStruct((t * ROW_SUB, LANES), jnp.uint32),
                   jax.ShapeDtypeStruct((t, LANES), F32),
                   jax.ShapeDtypeStruct((8, LANES), F32)],
        scratch_shapes=[pltpu.VMEM((1, LANES), F32)],
        compiler_params=_cparams(("arbitrary",)),
        name="moe_route",
    )(x1, g.reshape(1, d), w_gr, b_gr)


ROW_SUB = 8
U32 = jnp.uint32


def _pack_pairs(lo, hi):
    lo_b = lax.bitcast_convert_type(lo.astype(BF16).astype(F32), U32)
    hi_b = lax.bitcast_convert_type(hi.astype(BF16).astype(F32), U32)
    return lax.shift_right_logical(lo_b, U32(16)) | (hi_b & U32(0xFFFF0000))


def _unpack_pairs(w):
    lo = lax.bitcast_convert_type(lax.shift_left(w, U32(16)), F32)
    hi = lax.bitcast_convert_type(w & U32(0xFFFF0000), F32)
    return lo, hi


def _store_packed(ref, y, n):
    half = y.shape[1] // 2
    for s in range(ROW_SUB):
        cs = slice(s * LANES, (s + 1) * LANES)
        ref[pl.ds(s, n, stride=ROW_SUB), :] = _pack_pairs(y[:, cs], y[:, half + s * LANES:half + (s + 1) * LANES])


def _load_packed(ref, n):
    los, his = [], []
    for s in range(ROW_SUB):
        lo, hi = _unpack_pairs(ref[pl.ds(s, n, stride=ROW_SUB), :])
        los.append(lo)
        his.append(hi)
    return jnp.concatenate(los + his, axis=1)


def _row_copy(src_ref, src_row, dst_ref, dst_row, sem):
    return pltpu.make_async_copy(src_ref.at[pl.ds(pl.multiple_of(src_row * ROW_SUB, ROW_SUB), ROW_SUB)],
                                 dst_ref.at[pl.ds(pl.multiple_of(dst_row * ROW_SUB, ROW_SUB), ROW_SUB)], sem)


def _dispatch_body(dest_ref, h_ref, xs_ref, sem, *, tm):
    base = pl.program_id(0) * tm

    def issue(r, c):
        for k in range(EXPERT_TOPK):
            _row_copy(h_ref, r, xs_ref, dest_ref[EXPERT_TOPK * (base + r) + k], sem).start(priority=k % 2)
        return c

    lax.fori_loop(0, tm, issue, 0, unroll=4)

    def drain(r, c):
        for k in range(EXPERT_TOPK):
            _row_copy(h_ref, r, xs_ref, dest_ref[EXPERT_TOPK * (base + r) + k], sem).wait()
        return c

    lax.fori_loop(0, tm, drain, 0, unroll=4)


def _dispatch(dest, hp, tm=512):
    t = hp.shape[0] // ROW_SUB
    n_rows = dest.shape[0]
    return pl.pallas_call(
        functools.partial(_dispatch_body, tm=tm),
        grid_spec=pltpu.PrefetchScalarGridSpec(
            num_scalar_prefetch=1,
            grid=(t // tm,),
            in_specs=[pl.BlockSpec((tm * ROW_SUB, LANES), lambda i, dest: (i, 0))],
            out_specs=pl.BlockSpec(memory_space=pl.ANY),
            scratch_shapes=[pltpu.SemaphoreType.DMA(())],
        ),
        out_shape=jax.ShapeDtypeStruct((n_rows * ROW_SUB, LANES), U32),
        compiler_params=_cparams(("arbitrary",)),
        name="moe_dispatch",
    )(dest, hp)


def _expert_body(nu_ref, sq_ref, es_ref, ns_ref, blk_ref, lo_ref, hi_ref, x_ref, wg_hbm, wu_hbm, wd_hbm, y_ref,
                 wg_b, wu_b, wd_b, wg_s, wu_s, wd_s, y_acc, sem):
    i = pl.program_id(0)
    nu = nu_ref[0]
    ns = ns_ref[0]

    def weight_copies(seq, slot):
        e = es_ref[seq]
        return (pltpu.make_async_copy(wg_hbm.at[e], wg_b.at[slot], sem.at[slot]),
                pltpu.make_async_copy(wu_hbm.at[e], wu_b.at[slot], sem.at[slot]),
                pltpu.make_async_copy(wd_hbm.at[e], wd_b.at[slot], sem.at[slot]))

    def start_weights(seq, slot):
        for c in weight_copies(seq, slot):
            c.start()

    @pl.when(i == 0)
    def _():
        y_acc[...] = jnp.zeros(y_acc.shape, F32)
        start_weights(0, 0)

        @pl.when(ns > 1)
        def _():
            start_weights(1, 1)

    s = sq_ref[i]
    first = (i == 0) | (s != sq_ref[jnp.maximum(i - 1, 0)])

    @pl.when((i < nu) & first)
    def _():
        slot = lax.rem(s, 2)
        for c in weight_copies(s, slot):
            c.wait()
        def cast_rows(dst, src, chunk):
            def body(c, carry):
                rs = pl.ds(pl.multiple_of(c * chunk, chunk), chunk)
                dst[rs, :] = src[slot, rs, :].astype(BF16)
                return carry
            lax.fori_loop(0, dst.shape[0] // chunk, body, 0)

        cast_rows(wg_s, wg_b, 256)
        cast_rows(wu_s, wu_b, 256)
        cast_rows(wd_s, wd_b, 64)

        @pl.when(s + 2 < ns)
        def _():
            start_weights(s + 2, slot)

    @pl.when(i < nu)
    def _():
        x = _load_packed(x_ref, MOE_ROWS).astype(BF16)
        g = jnp.dot(x, wg_s[...], preferred_element_type=F32)
        u = jnp.dot(x, wu_s[...], preferred_element_type=F32)
        mid = (jax.nn.silu(g) * u).astype(BF16)
        y = jnp.dot(mid, wd_s[...], preferred_element_type=F32)
        row = lax.broadcasted_iota(jnp.int32, (MOE_ROWS, 1), 0)
        mine = (row >= lo_ref[i]) & (row < hi_ref[i])
        new_block = (i == 0) | (blk_ref[i] != blk_ref[jnp.maximum(i - 1, 0)])
        y_acc[...] = jnp.where(mine, y, jnp.where(new_block, 0.0, y_acc[...]))
        _store_packed(y_ref, y_acc[...], MOE_ROWS)


def _experts(n_pairs, seq_of_pair, expert_of_seq, n_seq, blk_of_pair, lo, hi, xs, w_gate, w_up, w_down):
    n_steps = seq_of_pair.shape[0]
    _, d, ff = w_gate.shape
    assert d == 2 * ROW_SUB * LANES, "a packed row must be exactly one (8,128) tile"
    blk = lambda i, nu, sq, es, ns, bk, lo, hi: (bk[jnp.minimum(i, nu[0] - 1)], 0)
    hbm = pl.BlockSpec(memory_space=pl.ANY)
    return pl.pallas_call(
        _expert_body,
        grid_spec=pltpu.PrefetchScalarGridSpec(
            num_scalar_prefetch=7,
            grid=(n_steps,),
            in_specs=[pl.BlockSpec((MOE_ROWS * ROW_SUB, LANES), blk), hbm, hbm, hbm],
            out_specs=pl.BlockSpec((MOE_ROWS * ROW_SUB, LANES), blk),
            scratch_shapes=[pltpu.VMEM((2, d, ff), F32), pltpu.VMEM((2, d, ff), F32), pltpu.VMEM((2, ff, d), F32),
                            pltpu.VMEM((d, ff), BF16), pltpu.VMEM((d, ff), BF16), pltpu.VMEM((ff, d), BF16),
                            pltpu.VMEM((MOE_ROWS, d), F32), pltpu.SemaphoreType.DMA((2,))],
        ),
        out_shape=jax.ShapeDtypeStruct(xs.shape, U32),
        compiler_params=_cparams(("arbitrary",)),
        name="moe_experts",
    )(n_pairs, seq_of_pair, expert_of_seq, n_seq, blk_of_pair, lo, hi, xs, w_gate, w_up, w_down)


def _combine_body(dest_ref, x_ref, info_ref, g_ref, ys_ref, o_ref, buf0, buf1, sem, *, tm):
    i = pl.program_id(0)
    slot = lax.rem(i, 2)
    bufs = (buf0, buf1)

    def gather(tile, sl, wait):
        def body(r, c):
            for k in range(EXPERT_TOPK):
                cp = _row_copy(ys_ref, dest_ref[EXPERT_TOPK * (tile * tm + r) + k], bufs[k].at[sl], r, sem.at[sl])
                if wait:
                    cp.wait()
                else:
                    cp.start(priority=k % 2)
            return c
        lax.fori_loop(0, tm, body, 0, unroll=4)

    @pl.when(i == 0)
    def _():
        gather(0, 0, False)

    @pl.when(i + 1 < pl.num_programs(0))
    def _():
        gather(i + 1, 1 - slot, False)

    gather(i, slot, True)
    info = info_ref[...]
    y = x_ref[...] + (info[:, 2:3] * _load_packed(buf0.at[slot], tm) + info[:, 3:4] * _load_packed(buf1.at[slot], tm))
    ms = jnp.mean(y * y, axis=-1, keepdims=True)
    o_ref[...] = y * lax.rsqrt(ms + RMS_EPS) * g_ref[...]


def _combine(dest, x1, info, g, ys, tm=256):
    t, d = x1.shape
    return pl.pallas_call(
        functools.partial(_combine_body, tm=tm),
        grid_spec=pltpu.PrefetchScalarGridSpec(
            num_scalar_prefetch=1,
            grid=(t // tm,),
            in_specs=[pl.BlockSpec((tm, d), lambda i, dest: (i, 0)),
                      pl.BlockSpec((tm, LANES), lambda i, dest: (i, 0)),
                      pl.BlockSpec((1, d), lambda i, dest: (0, 0)),
                      pl.BlockSpec(memory_space=pl.ANY)],
            out_specs=pl.BlockSpec((tm, d), lambda i, dest: (i, 0)),
            scratch_shapes=[pltpu.VMEM((2, tm * ROW_SUB, LANES), U32), pltpu.VMEM((2, tm * ROW_SUB, LANES), U32),
                            pltpu.SemaphoreType.DMA((2,))],
        ),
        out_shape=jax.ShapeDtypeStruct((t, d), F32),
        compiler_params=_cparams(("arbitrary",)),
        name="moe_combine",
    )(dest, x1, info, g.reshape(1, d), ys)


def _nsa(proj, slab, pe_k, w1_k, w2_k, pe_v, w1_v, w2_v, tab, b, s):
    g, dh = NSA_KV_HEADS, HEAD_DIM
    qw = NSA_HEADS * dh
    nc = s // CMP_STRIDE
    kvc = _compress(slab, pe_k, w1_k, w2_k, pe_v, w1_v, w2_v, g)

    c_start = np.arange(nc)[None, :] * CMP_STRIDE
    n_sel = s // SEL_BLOCK
    sb = np.arange(n_sel)[:, None] * SEL_BLOCK
    overlap = jnp.asarray((c_start < sb + SEL_BLOCK) & (c_start + CMP_BLOCK > sb), BF16)
    o_c, member = _nsa_cmp(proj, kvc, tab, overlap, b, s)

    t = ATT_TILE
    nq = s // t
    nn = _n_near(t)
    nh = ATT_HEADS_PER_STEP
    kblk = qw // (nh // NSA_GROUP * dh)
    per = g // (nh // NSA_GROUP)
    bias_d = _bias_vecs(tab, t, nn + 1)
    o_s = _flash(proj, 0, proj, kblk, proj, kblk + per, bias_d, NSA_HEADS, NSA_GROUP, nh,
                 _causal_pairs(nq, nn), True, member=member, et=_block_onehot(s, SEL_BLOCK), name="nsa_selected")
    n_back = -(-WINDOW // t)
    bias_w = _bias_vecs(tab, t, n_back + 1, window=WINDOW)
    o_w = _flash(proj, 0, proj, kblk + 2 * per, proj, kblk + 3 * per, bias_w, NSA_HEADS, NSA_GROUP, nh,
                 _window_pairs(nq, n_back), False, name="nsa_window")
    return o_c, o_s, o_w


def _moba(proj, tab, b, s):
    member = _moba_gate(proj, b, s)
    t = ATT_TILE
    nn = _n_near(t)
    nh = ATT_HEADS_PER_STEP
    ng = MOBA_HEADS // nh
    bias_d = _bias_vecs(tab, t, nn + 1)
    return _flash(proj, 0, proj, ng, proj, 2 * ng, bias_d, MOBA_HEADS, 1, nh, _causal_pairs(s // t, nn),
                  True, member=member, et=_block_onehot(s, MOBA_BLOCK), name="moba_attn")


def _moe(x1, g_ffn, w_group, b_group, w_router, b_router, w_gate, w_up, w_down, g_final):
    t, d = x1.shape
    ng, _, epg = w_router.shape
    w_gr = jnp.concatenate([w_group, jnp.transpose(w_router, (1, 0, 2)).reshape(d, ng * epg),
                            jnp.zeros((d, LANES - ng - ng * epg), F32)], axis=1)
    b_gr = jnp.concatenate([b_group, b_router.reshape(-1), jnp.zeros((LANES - ng - ng * epg,), F32)]).reshape(1, LANES)
    h, info, cnt = _route(x1, g_ffn, w_gr, b_gr)
    n_e = ng * epg
    n_assign = t * EXPERT_TOPK
    assert n_assign % MOE_ROWS == 0
    n_blocks = n_assign // MOE_ROWS
    counts = cnt[0, :n_e].astype(jnp.int32)
    end = jnp.cumsum(counts)
    start = end - counts
    expert = info[:, 0:EXPERT_TOPK].astype(jnp.int32)
    rank = info[:, 4:4 + EXPERT_TOPK].astype(jnp.int32)
    e_ids = jnp.arange(n_e, dtype=jnp.int32)
    dest = (jnp.sum(jnp.where(expert[..., None] == e_ids, start, 0), axis=-1) + rank).reshape(-1)
    owns = counts > 0
    seq_of_expert = jnp.cumsum(owns.astype(jnp.int32)) - 1
    n_seq = jnp.sum(owns.astype(jnp.int32)).reshape(1)
    expert_of_seq = jnp.sum(jnp.where(owns[None, :] & (seq_of_expert[None, :] == e_ids[:, None]), e_ids[None, :], 0),
                            axis=1)
    first = start // MOE_ROWS
    last = jnp.where(owns, (end - 1) // MOE_ROWS, first - 1)
    pair_end = jnp.cumsum(last - first + 1)
    pair_start = pair_end - (last - first + 1)
    n_steps = n_blocks + n_e
    p_ids = jnp.arange(n_steps, dtype=jnp.int32)
    e_of_pair = jnp.minimum(jnp.sum((pair_end[None, :] <= p_ids[:, None]).astype(jnp.int32), axis=1), n_e - 1)
    pick = e_of_pair[:, None] == e_ids[None, :]
    lookup = lambda v: jnp.sum(jnp.where(pick, v[None, :], 0), axis=1)
    blk_of_pair = jnp.clip(lookup(first) + p_ids - lookup(pair_start), 0, n_blocks - 1)
    lo = jnp.clip(lookup(start) - blk_of_pair * MOE_ROWS, 0, MOE_ROWS)
    hi = jnp.clip(lookup(end) - blk_of_pair * MOE_ROWS, 0, MOE_ROWS)
    seq_of_pair = lookup(seq_of_expert)
    n_pairs = pair_end[-1:].astype(jnp.int32)
    xs = _dispatch(dest, h)
    ys = _experts(n_pairs, seq_of_pair, expert_of_seq, n_seq, blk_of_pair, lo, hi, xs, w_gate, w_up, w_down)
    return _combine(dest, x1, info, g_final, ys)


def kernel(x, rel_bias, norm_mix, w_in, cmp_pe_k, cmp_w1_k, cmp_w2_k, cmp_pe_v, cmp_w1_v, cmp_w2_v, w_up_nsa,
           w_up_moba, w_out, norm_ffn, w_group, b_group, w_router, b_router, w_exp_gate, w_exp_up, w_exp_down,
           final_norm):
    b, s, d = x.shape
    t = b * s
    depth = w_in.shape[0]
    tab_a = rel_bias[:, :NSA_HEADS]
    tab_b = rel_bias[:, NSA_HEADS:]
    a_cols = NSA_HEADS * HEAD_DIM + 6 * NSA_KV_HEADS * HEAD_DIM
    gate_cols = 3 * NSA_HEADS
    b_cols = 3 * MOBA_HEADS * HEAD_DIM
    xt = x.reshape(t, d)
    out = None
    for l in range(depth):
        h = _rmsnorm(xt, norm_mix[l], BF16)
        wt = jnp.swapaxes(w_in[l], 0, 1)
        b_col0 = a_cols + gate_cols
        q_cols = NSA_HEADS * HEAD_DIM
        cmp_cols = 2 * NSA_KV_HEADS * HEAD_DIM
        proj_a = _matmul(h, wt, 0, a_cols - cmp_cols, BF16, scaled_cols=q_cols, col_scale=Q_SCALE,
                         w_transposed=True, skip_at=q_cols, skip_cols=cmp_cols,
                         name="in_proj_a").reshape(b, s, a_cols - cmp_cols)
        slab = _matmul_slab(h, wt, q_cols, cmp_cols, CMP_STRIDE).reshape(b, s // CMP_STRIDE, CMP_STRIDE * cmp_cols)
        gate_a = _matmul(h, wt, a_cols, LANES, F32, w_transposed=True, tn=LANES, name="in_proj_gate")
        proj_b = _matmul(h, wt, b_col0, b_cols, BF16, scaled_cols=MOBA_HEADS * HEAD_DIM, col_scale=Q_SCALE,
                         w_transposed=True, name="in_proj_b").reshape(b, s, b_cols)
        gm = _matmul(h, wt, b_col0 + b_cols, 2 * d, BF16, w_transposed=True, name="in_proj_gm")
        o_c, o_s, o_w = _nsa(proj_a, slab, cmp_pe_k[l], cmp_w1_k[l], cmp_w2_k[l],
                             cmp_pe_v[l], cmp_w1_v[l], cmp_w2_v[l], tab_a, b, s)
        o_b = _moba(proj_b, tab_b, b, s)
        merged = _merge(o_c.reshape(t, -1), o_s.reshape(t, -1), o_w.reshape(t, -1), gate_a,
                        o_b.reshape(t, -1), gm, w_up_nsa[l], w_up_moba[l])
        x1 = _matmul(merged, w_out[l], 0, d, F32, res=xt, name="out_proj")
        assert l == depth - 1, "only the last layer's MoE is fused with the final norm"
        out = _moe(x1, norm_ffn[l], w_group[l], b_group[l], w_router[l], b_router[l],
                   w_exp_gate[l], w_exp_up[l], w_exp_down[l], final_norm)
    return out.reshape(b, s, d)
```

```python
import functools
import math

import numpy as np
import jax
import jax.numpy as jnp
from jax import lax
from jax.experimental import pallas as pl
from jax.experimental.pallas import tpu as pltpu

F32 = jnp.float32
BF16 = jnp.bfloat16

HEAD_DIM = 128
NSA_HEADS = 8
NSA_KV_HEADS = 2
NSA_GROUP = NSA_HEADS // NSA_KV_HEADS
CMP_BLOCK = 32
CMP_STRIDE = 16
SEL_BLOCK = 64
SEL_TOPN = 16
WINDOW = 512
FORCED_SCORE = 1e4
MOBA_HEADS = 8
MOBA_BLOCK = 256
MOBA_TOPK = 3
REL_BUCKETS = 32
REL_MAX_DIST = 128
N_GROUPS = 8
EXPERTS_PER_GROUP = 8
N_EXPERTS = N_GROUPS * EXPERTS_PER_GROUP
EXPERT_TOPK = 2
RMS_EPS = 1e-6

LANES = 128
ATT_TILE = 512
ATT_HEADS_PER_STEP = 8
ATT_VMEM_LIMIT = 56 * 1024 * 1024
MOE_ROWS = 256
MASK_BIG = 1e30
M_INIT = -3e38
LOG2E = math.log2(math.e)
Q_SCALE = HEAD_DIM ** -0.5 * LOG2E
VMEM_LIMIT = 48 * 1024 * 1024


def _cparams(sem, vmem=VMEM_LIMIT):
    return pltpu.CompilerParams(dimension_semantics=sem, vmem_limit_bytes=vmem)


def _mm_body(*refs, has_res, n_scaled, col_scale, w_transposed):
    if has_res:
        a_ref, w_ref, r_ref, o_ref = refs
    else:
        a_ref, w_ref, o_ref = refs
    w = w_ref[...].astype(BF16)
    if w_transposed:
        acc = lax.dot_general(a_ref[...], w, (((1,), (1,)), ((), ())), preferred_element_type=F32)
    else:
        acc = jnp.dot(a_ref[...], w, preferred_element_type=F32)
    if n_scaled:
        acc = acc * jnp.where(pl.program_id(1) < n_scaled, col_scale, 1.0)
    if has_res:
        acc = acc + r_ref[...]
    o_ref[...] = acc.astype(o_ref.dtype)


def _matmul(a, w, col0, ncols, out_dtype, res=None, scaled_cols=0, col_scale=1.0, w_transposed=False,
            skip_at=0, skip_cols=0, tm=2048, tn=512, name="matmul"):
    t, k = a.shape
    tn = min(tn, ncols)
    tm = min(tm, t)
    assert ncols % tn == 0 and t % tm == 0 and scaled_cols % tn == 0 and skip_at % tn == 0
    if w_transposed:
        assert col0 % 8 == 0 and skip_cols % 8 == 0
        gap_block = skip_at // tn if skip_cols else ncols // tn
        w_spec = pl.BlockSpec(
            (pl.Element(tn), pl.Element(k)),
            lambda i, j: (pl.multiple_of(col0 + j * tn + jnp.where(j >= gap_block, skip_cols, 0), 8), 0))
    else:
        assert col0 % tn == 0
        off = col0 // tn
        w_spec = pl.BlockSpec((k, tn), lambda i, j: (0, j + off))
    in_specs = [pl.BlockSpec((tm, k), lambda i, j: (i, 0)), w_spec]
    args = [a, w]
    if res is not None:
        in_specs.append(pl.BlockSpec((tm, tn), lambda i, j: (i, j)))
        args.append(res)
    return pl.pallas_call(
        functools.partial(_mm_body, has_res=res is not None, n_scaled=scaled_cols // tn, col_scale=col_scale,
                          w_transposed=w_transposed),
        grid=(t // tm, ncols // tn),
        in_specs=in_specs,
        out_specs=pl.BlockSpec((tm, tn), lambda i, j: (i, j)),
        out_shape=jax.ShapeDtypeStruct((t, ncols), out_dtype),
        compiler_params=_cparams(("parallel", "parallel")),
        name=name,
    )(*args)


def _proj_head_body(x_ref, g_ref, ws_ref, wg_ref, h_ref, slab_ref, gate_ref, acc_ref, *, per):
    x = x_ref[...]
    ms = jnp.mean(x * x, axis=-1, keepdims=True)
    h = (x * lax.rsqrt(ms + RMS_EPS) * g_ref[...]).astype(BF16)
    h_ref[...] = h
    nt = (((1,), (1,)), ((), ()))
    gate_ref[...] = lax.dot_general(h, wg_ref[...].astype(BF16), nt, preferred_element_type=F32)
    acc = lax.dot_general(h, ws_ref[...].astype(BF16), nt, preferred_element_type=F32)
    n_chunks, tm, _ = acc_ref.shape
    tn = n_chunks * LANES
    for c in range(n_chunks):
        acc_ref[c] = acc[:, c * LANES:(c + 1) * LANES]
    for r in range(per):
        for c in range(n_chunks):
            slab_ref[:, r * tn + c * LANES:r * tn + (c + 1) * LANES] = (
                acc_ref[c, pl.ds(r, tm // per, stride=per), :].astype(slab_ref.dtype))


def _proj_head(x, g, wt, slab_col0, slab_cols, gate_col0, per, tm=1024):
    t, k = x.shape
    tm = min(tm, t)
    assert t % tm == 0 and tm % (per * 8) == 0 and slab_col0 % 8 == 0 and gate_col0 % 8 == 0
    return pl.pallas_call(
        functools.partial(_proj_head_body, per=per),
        grid=(t // tm,),
        in_specs=[pl.BlockSpec((tm, k), lambda i: (i, 0)),
                  pl.BlockSpec((1, k), lambda i: (0, 0)),
                  pl.BlockSpec((pl.Element(slab_cols), pl.Element(k)), lambda i: (slab_col0, 0)),
                  pl.BlockSpec((pl.Element(LANES), pl.Element(k)), lambda i: (gate_col0, 0))],
        out_specs=[pl.BlockSpec((tm, k), lambda i: (i, 0)),
                   pl.BlockSpec((tm // per, per * slab_cols), lambda i: (i, 0)),
                   pl.BlockSpec((tm, LANES), lambda i: (i, 0))],
        out_shape=[jax.ShapeDtypeStruct((t, k), BF16),
                   jax.ShapeDtypeStruct((t // per, per * slab_cols), BF16),
                   jax.ShapeDtypeStruct((t, LANES), F32)],
        scratch_shapes=[pltpu.VMEM((slab_cols // LANES, tm, LANES), F32)],
        compiler_params=_cparams(("parallel",)),
        name="in_proj_head",
    )(x, g.reshape(1, k), wt, wt)


def _compress_body(u_ref, pek_ref, w1k_ref, w2k_ref, pev_ref, w1v_ref, w2v_ref, o_ref, *, nc, g):
    dh = HEAD_DIM
    tok_w = 2 * g * dh
    for kv, (pe_ref, w1_ref, w2_ref) in enumerate(((pek_ref, w1k_ref, w2k_ref), (pev_ref, w1v_ref, w2v_ref))):
        w1 = w1_ref[...].astype(BF16)
        w2 = w2_ref[...].astype(BF16)
        half = CMP_STRIDE * dh
        peb = jnp.dot(pe_ref[...].astype(BF16), w1, preferred_element_type=F32)[0:1]
        for gi in range(g):
            a = b = None
            for r in range(CMP_STRIDE):
                c0 = r * tok_w + (kv * g + gi) * dh
                piece = u_ref[:, c0:c0 + dh]
                da = jnp.dot(piece, w1[r * dh:(r + 1) * dh], preferred_element_type=F32)
                db = jnp.dot(piece, w1[half + r * dh:half + (r + 1) * dh], preferred_element_type=F32)
                a = da if a is None else a + da
                b = db if b is None else b + db
            pre = a + pltpu.roll(b, nc - 1, 0) + peb
            hid = jax.nn.gelu(pre)
            o_ref[kv * g + gi] = jnp.dot(hid.astype(BF16), w2, preferred_element_type=F32).astype(o_ref.dtype)


def _compress(u, pe_k, w1_k, w2_k, pe_v, w1_v, w2_v, g):
    b, nc, kk = u.shape
    hid = w1_k.shape[1]
    dh = w2_k.shape[1]
    pe16 = lambda pe: jnp.broadcast_to(pe.reshape(1, -1), (16, pe.size))
    full = lambda shape: pl.BlockSpec(shape, lambda i: (0,) * len(shape))
    return pl.pallas_call(
        functools.partial(_compress_body, nc=nc, g=g),
        grid=(b,),
        in_specs=[pl.BlockSpec((None, nc, kk), lambda i: (i, 0, 0)),
                  full((16, CMP_BLOCK * dh)), full((CMP_BLOCK * dh, hid)), full((hid, dh)),
                  full((16, CMP_BLOCK * dh)), full((CMP_BLOCK * dh, hid)), full((hid, dh))],
        out_specs=pl.BlockSpec((None, 2 * g, nc, dh), lambda i: (i, 0, 0, 0)),
        out_shape=jax.ShapeDtypeStruct((b, 2 * g, nc, dh), BF16),
        compiler_params=_cparams(("parallel",)),
        name="nsa_compress",
    )(u, pe16(pe_k), w1_k, w2_k, pe16(pe_v), w1_v, w2_v)


def _split3(x):
    p1 = x.astype(BF16)
    r = x - p1.astype(F32)
    p2 = r.astype(BF16)
    p3 = (r - p2.astype(F32)).astype(BF16)
    return p1, p2, p3


def _rank_count(score, n_rows):
    groups = []
    for g0 in range(0, n_rows, 8):
        sg = score[g0:min(g0 + 8, n_rows), :]
        n_iota = g0 + lax.broadcasted_iota(jnp.int32, sg.shape, 0)
        cnt = jnp.zeros(sg.shape, F32)
        for m in range(n_rows):
            row = score[m:m + 1, :]
            if m < g0:
                beats = row >= sg
            elif m >= g0 + 8:
                beats = row > sg
            else:
                tie = jnp.where(n_iota > m, 1.0, 0.0)
                beats = jnp.where(row > sg, 1.0, jnp.where(row == sg, tie, 0.0)) > 0.5
            cnt = cnt + jnp.where(beats, 1.0, 0.0)
        groups.append(cnt)
    return jnp.concatenate(groups, axis=0) if len(groups) > 1 else groups[0]


def _nsa_cmp_body(q_ref, kc_ref, vc_ref, bias_ref, ov_ref, oc_ref, mem_ref, *, tq, nc, n_sel):
    t0 = pl.program_id(2) * tq
    kc = kc_ref[...]
    vc = vc_ref[...]
    t_idx = t0 + lax.broadcasted_iota(jnp.int32, (tq, nc), 0)
    c_idx = lax.broadcasted_iota(jnp.int32, (tq, nc), 1)
    dist = t_idx - (c_idx * CMP_STRIDE + (CMP_BLOCK - 1))
    n_k = REL_MAX_DIST // CMP_STRIDE
    kidx = jnp.where(dist < 0, n_k + 1, jnp.minimum(lax.shift_right_logical(dist, 4), n_k))
    assert CMP_STRIDE == 16
    psum = jnp.zeros((tq, nc), F32)
    for j in range(NSA_GROUP):
        hs = slice(j * HEAD_DIM, (j + 1) * HEAD_DIM)
        gt = bias_ref[j] * LOG2E
        bias = jnp.concatenate([jnp.take_along_axis(gt, kidx[:, c0:c0 + LANES], axis=1)
                                for c0 in range(0, nc, LANES)], axis=1)
        s = lax.dot_general(q_ref[:, hs], kc, (((1,), (1,)), ((), ())), preferred_element_type=F32) + bias
        m = jnp.max(s, axis=-1, keepdims=True)
        m = jnp.where(m > -0.5 * MASK_BIG, m, 0.0)
        p = jnp.exp2(s - m)
        d = jnp.sum(p, axis=-1, keepdims=True)
        p = p / jnp.where(d > 0, d, 1.0)
        oc_ref[:, hs] = jnp.dot(p.astype(BF16), vc, preferred_element_type=F32).astype(oc_ref.dtype)
        psum = psum + p
    ov = ov_ref[...]
    nt = (((1,), (1,)), ((), ()))
    p1, p2, p3 = _split3(psum)
    psel = (lax.dot_general(ov, p1, nt, preferred_element_type=F32)
            + lax.dot_general(ov, p2, nt, preferred_element_type=F32)
            + lax.dot_general(ov, p3, nt, preferred_element_type=F32))
    n_idx = lax.broadcasted_iota(jnp.int32, (n_sel, tq), 0)
    tt = t0 + lax.broadcasted_iota(jnp.int32, (n_sel, tq), 1)
    cur = tt // SEL_BLOCK
    forced = jnp.where(n_idx == 0, 1.0, jnp.where(n_idx == cur, 1.0, jnp.where(n_idx == cur - 1, 1.0, 0.0)))
    score = jnp.where(forced > 0.5, FORCED_SCORE, jnp.where(n_idx * SEL_BLOCK <= tt, psel, -1.0))
    cnt = _rank_count(score, n_sel)
    member = jnp.where(cnt < float(min(SEL_TOPN, n_sel)), 1.0, 0.0)
    if n_sel < LANES:
        member = jnp.concatenate([member, jnp.zeros((LANES - n_sel, tq), F32)], axis=0)
    mem_ref[...] = member.T.astype(mem_ref.dtype)


def _nsa_cmp(proj, kvc, tab, overlap, b, s, tq=256):
    g = NSA_KV_HEADS
    nc = kvc.shape[2]
    n_sel = s // SEL_BLOCK
    gw = NSA_GROUP * HEAD_DIM
    assert tq % CMP_STRIDE == 0 and REL_MAX_DIST % CMP_STRIDE == 0
    n_k = REL_MAX_DIST // CMP_STRIDE
    rho = (np.arange(tq)[:, None] - (CMP_BLOCK - 1)) % CMP_STRIDE
    dd = np.concatenate([rho + CMP_STRIDE * np.arange(n_k)[None, :], np.full((tq, 1), REL_MAX_DIST)], axis=1)
    gtab = jnp.transpose(_bias_of_dist(tab, dd), (2, 0, 1)).astype(F32)
    bias_c = jnp.concatenate([gtab, jnp.full(gtab.shape[:2] + (1,), -MASK_BIG, F32),
                              jnp.zeros(gtab.shape[:2] + (LANES - n_k - 2,), F32)], axis=2)
    body = functools.partial(_nsa_cmp_body, tq=tq, nc=nc, n_sel=n_sel)
    return pl.pallas_call(
        body,
        grid=(b, g, s // tq),
        in_specs=[pl.BlockSpec((None, tq, gw), lambda bi, gi, i: (bi, i, gi)),
                  pl.BlockSpec((None, None, nc, HEAD_DIM), lambda bi, gi, i: (bi, gi, 0, 0)),
                  pl.BlockSpec((None, None, nc, HEAD_DIM), lambda bi, gi, i: (bi, g + gi, 0, 0)),
                  pl.BlockSpec((NSA_GROUP, tq, LANES), lambda bi, gi, i: (gi, 0, 0)),
                  pl.BlockSpec((n_sel, nc), lambda bi, gi, i: (0, 0))],
        out_specs=[pl.BlockSpec((None, tq, gw), lambda bi, gi, i: (bi, i, gi)),
                   pl.BlockSpec((None, None, tq, LANES), lambda bi, gi, i: (bi, gi, i, 0))],
        out_shape=[jax.ShapeDtypeStruct((b, s, NSA_HEADS * HEAD_DIM), BF16),
                   jax.ShapeDtypeStruct((b, g, s, LANES), BF16)],
        compiler_params=_cparams(("parallel", "parallel", "parallel")),
        name="nsa_cmp_select",
    )(proj, kvc, kvc, bias_c, overlap)


def _moba_gate_body(q_ref, k_ref, mem_ref, *, s, nblk):
    k = k_ref[...].astype(F32)
    kmean = jnp.mean(k.reshape(nblk, MOBA_BLOCK, HEAD_DIM), axis=1)
    k1 = kmean.astype(BF16)
    k2 = (kmean - k1.astype(F32)).astype(BF16)
    q = q_ref[...]
    nt = (((1,), (1,)), ((), ()))
    gate = (lax.dot_general(k1, q, nt, preferred_element_type=F32)
            + lax.dot_general(k2, q, nt, preferred_element_type=F32))
    n_idx = lax.broadcasted_iota(jnp.int32, (nblk, s), 0)
    own = lax.broadcasted_iota(jnp.int32, (nblk, s), 1) // MOBA_BLOCK
    past = n_idx < own
    score = jnp.where(past, gate, -MASK_BIG)
    cnt = _rank_count(score, nblk)
    n_top = max(1, min(MOBA_TOPK, nblk - 1))
    sel = jnp.where(past, jnp.where(cnt < float(n_top), 1.0, 0.0), 0.0)
    member = jnp.where(n_idx == own, 1.0, sel)
    member = jnp.concatenate([member, jnp.zeros((LANES - nblk, s), F32)], axis=0)
    mem_ref[...] = member.T.astype(mem_ref.dtype)


def _moba_gate(proj, b, s):
    h = MOBA_HEADS
    nblk = s // MOBA_BLOCK
    return pl.pallas_call(
        functools.partial(_moba_gate_body, s=s, nblk=nblk),
        grid=(b, h),
        in_specs=[pl.BlockSpec((None, s, HEAD_DIM), lambda bi, hi: (bi, 0, hi)),
                  pl.BlockSpec((None, s, HEAD_DIM), lambda bi, hi: (bi, 0, h + hi))],
        out_specs=pl.BlockSpec((None, None, s, LANES), lambda bi, hi: (bi, hi, 0, 0)),
        out_shape=jax.ShapeDtypeStruct((b, h, s, LANES), BF16),
        compiler_params=_cparams(("parallel", "parallel")),
        name="moba_gate",
    )(proj, proj)


def _flash_body(qi_ref, ki_ref, bo_ref, fl_ref, *refs, nh, ratio, nm, n_near, has_far):
    if nm:
        (q_ref, k_ref, v_ref, bvec_ref, mem_ref, et_ref, o_ref,
         m_ref, l_ref, acc_ref, sh_ref, al_ref, bias_ref, s_ref, p_ref) = refs
    else:
        q_ref, k_ref, v_ref, bvec_ref, o_ref, m_ref, l_ref, acc_ref, sh_ref, al_ref, bias_ref, s_ref, p_ref = refs
    del qi_ref, ki_ref
    p = pl.program_id(2)
    flag = fl_ref[p]
    bo = bo_ref[p]
    t = q_ref.shape[0]
    rows = 64

    @pl.when(p == 0)
    def _():
        for h in range(nh):
            for o in range(n_near):
                vec = bvec_ref[h, o][0:1, :] * LOG2E
                for rc in range(t // rows):
                    x = pltpu.roll(jnp.broadcast_to(vec, (rows, 2 * t)), rc * rows, 1, stride=1, stride_axis=0)
                    bias_ref[h, o, rc * rows:(rc + 1) * rows, :] = x[:, :t]

    @pl.when((flag & 1) != 0)
    def _():
        m_ref[...] = jnp.full(m_ref.shape, M_INIT, F32)
        l_ref[...] = jnp.zeros(l_ref.shape, F32)
        acc_ref[...] = jnp.zeros(acc_ref.shape, F32)

    nt = (((1,), (1,)), ((), ()))
    reps = t // LANES

    def step(near):
        def pass1(h):
            hs = slice(h * HEAD_DIM, (h + 1) * HEAD_DIM)
            kv = h // ratio
            ks = slice(kv * HEAD_DIM, (kv + 1) * HEAD_DIM)
            q = q_ref[:, hs]
            k = k_ref[:, ks]
            if nm:
                mneg = mem_ref[h // (nh // nm)] - 1.0
                q = jnp.concatenate([q, mneg.astype(BF16)], axis=1)
                k = jnp.concatenate([k, et_ref[...]], axis=1)
            sc = lax.dot_general(q, k, nt, preferred_element_type=F32)
            m_prev = m_ref[h]
            if near:
                sc = sc + bias_ref[h, bo]
                m_new = jnp.maximum(m_prev, jnp.max(sc, axis=-1, keepdims=True))
                sh_ref[h] = m_new
            else:
                cfar = bvec_ref[h, n_near][0:1, 0:LANES] * LOG2E
                m_new = jnp.maximum(m_prev, jnp.max(sc, axis=-1, keepdims=True) + cfar)
                sh_ref[h] = m_new - cfar
            s_ref[h] = sc
            al_ref[h] = jnp.exp2(m_prev - m_new)
            m_ref[h] = m_new

        def pass2(h):
            ks = slice(h // ratio * HEAD_DIM, (h // ratio + 1) * HEAD_DIM)
            for rc in range(t // rows):
                rs = slice(rc * rows, (rc + 1) * rows)
                pm = jnp.exp2(s_ref[h, rs, :] - jnp.tile(sh_ref[h, rs, :], (1, reps)))
                l_ref[h, rs, :] = al_ref[h, rs, :] * l_ref[h, rs, :] + jnp.sum(pm, axis=-1, keepdims=True)
                p_ref[h, rs, :] = pm.astype(BF16)
            acc_ref[h] = al_ref[h] * acc_ref[h] + jnp.dot(p_ref[h], v_ref[:, ks], preferred_element_type=F32)

        for h in range(nh):
            pass1(h)
        for h in range(nh):
            pass2(h)

    if has_far:
        pl.when(bo < n_near)(lambda: step(True))
        pl.when(bo >= n_near)(lambda: step(False))
    else:
        step(True)

    @pl.when((flag & 2) != 0)
    def _():
        for h in range(nh):
            l = l_ref[h]
            o_ref[:, h * HEAD_DIM:(h + 1) * HEAD_DIM] = (acc_ref[h] / jnp.where(l > 0, l, 1.0)).astype(o_ref.dtype)


def _flash(q_arr, q_off, k_arr, k_off, v_arr, v_off, bias, n_heads, ratio, nh, pairs, has_far, member=None,
           et=None, name="flash"):
    b, s, _ = q_arr.shape
    t = ATT_TILE
    nkv = nh // ratio
    ng = n_heads // nh
    qi = jnp.asarray([p[0] for p in pairs], jnp.int32)
    ki = jnp.asarray([p[1] for p in pairs], jnp.int32)
    bo = jnp.asarray([p[2] for p in pairs], jnp.int32)
    fl = jnp.asarray([p[3] for p in pairs], jnp.int32)
    nb = bias.shape[1]
    nm = 0
    in_specs = [
        pl.BlockSpec((None, t, nh * HEAD_DIM), lambda bi, gi, p, qi, ki, bo, fl: (bi, qi[p], q_off + gi)),
        pl.BlockSpec((None, t, nkv * HEAD_DIM), lambda bi, gi, p, qi, ki, bo, fl: (bi, ki[p], k_off + gi)),
        pl.BlockSpec((None, t, nkv * HEAD_DIM), lambda bi, gi, p, qi, ki, bo, fl: (bi, ki[p], v_off + gi)),
        pl.BlockSpec((nh, nb, 8, 2 * t), lambda bi, gi, p, qi, ki, bo, fl: (gi, 0, 0, 0)),
    ]
    args = [q_arr, k_arr, v_arr, bias]
    if member is not None:
        nm = member.shape[1] // ng
        in_specs += [
            pl.BlockSpec((None, nm, t, LANES), lambda bi, gi, p, qi, ki, bo, fl: (bi, gi, qi[p], 0)),
            pl.BlockSpec((t, LANES), lambda bi, gi, p, qi, ki, bo, fl: (ki[p], 0)),
        ]
        args += [member, et]
    n_near = nb - 1 if has_far else nb
    body = functools.partial(_flash_body, nh=nh, ratio=ratio, nm=nm, n_near=n_near, has_far=has_far)
    return pl.pallas_call(
        body,
        grid_spec=pltpu.PrefetchScalarGridSpec(
            num_scalar_prefetch=4,
            grid=(b, ng, len(pairs)),
            in_specs=in_specs,
            out_specs=pl.BlockSpec((None, t, nh * HEAD_DIM), lambda bi, gi, p, qi, ki, bo, fl: (bi, qi[p], gi)),
            scratch_shapes=[pltpu.VMEM((nh, t, LANES), F32)] * 5 + [pltpu.VMEM((nh, n_near, t, t), F32),
                                                                    pltpu.VMEM((nh, t, t), F32),
                                                                    pltpu.VMEM((nh, t, t), BF16)],
        ),
        out_shape=jax.ShapeDtypeStruct((b, s, n_heads * HEAD_DIM), BF16),
        compiler_params=_cparams(("parallel", "parallel", "arbitrary"), vmem=ATT_VMEM_LIMIT),
        name=name,
    )(qi, ki, bo, fl, *args)


def _rel_bucket(dist):
    n = jnp.maximum(jnp.asarray(dist, jnp.int32), 0)
    max_exact = REL_BUCKETS // 2
    nf = jnp.maximum(n, 1).astype(jnp.float32)
    large = max_exact + (jnp.log(nf / max_exact) / math.log(REL_MAX_DIST / max_exact)
                         * (REL_BUCKETS - max_exact)).astype(jnp.int32)
    return jnp.where(n < max_exact, n, jnp.minimum(large, REL_BUCKETS - 1))


def _bias_of_dist(tab, dist):
    hit = _rel_bucket(dist)[..., None, None] == jnp.arange(REL_BUCKETS)[:, None]
    return jnp.sum(jnp.where(hit, tab, 0.0), axis=-2)


def _n_near(t):
    return -(-(REL_MAX_DIST - 1 + t) // t)


def _bias_vecs(tab, t, n_off, window=None):
    k = np.arange(2 * t)[None, :]
    dist = np.arange(n_off)[:, None] * t + np.where(k < t, -k, 2 * t - k)
    ok = dist >= 0
    if window is not None:
        ok &= dist < window
    bias = jnp.where(jnp.asarray(ok)[..., None], _bias_of_dist(tab, dist), -MASK_BIG)
    bias = jnp.transpose(bias, (2, 0, 1)).astype(F32)
    return jnp.broadcast_to(bias[:, :, None, :], (bias.shape[0], n_off, 8, 2 * t))


def _causal_pairs(nq, n_near):
    pairs = []
    for qi in range(nq):
        for ki in range(qi + 1):
            pairs.append((qi, ki, min(qi - ki, n_near), (1 if ki == 0 else 0) | (2 if ki == qi else 0)))
    return pairs


def _window_pairs(nq, n_back):
    pairs = []
    for qi in range(nq):
        lo = max(0, qi - n_back)
        for ki in range(lo, qi + 1):
            pairs.append((qi, ki, qi - ki, (1 if ki == lo else 0) | (2 if ki == qi else 0)))
    return pairs


def _block_onehot(s, blk):
    return jnp.asarray(np.where(np.arange(s)[:, None] // blk == np.arange(LANES)[None, :], MASK_BIG, 0.0), BF16)


def _merge_body(oc_ref, os_ref, ow_ref, gl_ref, ob_ref, gma_ref, gmb_ref, wa_ref, wb_ref, o_ref, oa_ref):
    @pl.when(pl.program_id(1) == 0)
    def _():
        gates = jax.nn.sigmoid(gl_ref[...])
        for h in range(NSA_HEADS):
            hs = slice(h * HEAD_DIM, (h + 1) * HEAD_DIM)
            mix = (gates[:, 3 * h:3 * h + 1] * oc_ref[:, hs].astype(F32)
                   + gates[:, 3 * h + 1:3 * h + 2] * os_ref[:, hs].astype(F32)
                   + gates[:, 3 * h + 2:3 * h + 3] * ow_ref[:, hs].astype(F32))
            oa_ref[:, hs] = mix.astype(BF16)

    ya = jnp.dot(oa_ref[...], wa_ref[...].astype(BF16), preferred_element_type=F32)
    yb = jnp.dot(ob_ref[...], wb_ref[...].astype(BF16), preferred_element_type=F32)
    ga = jax.nn.sigmoid(gma_ref[...].astype(F32))
    gb = jax.nn.sigmoid(gmb_ref[...].astype(F32))
    o_ref[...] = (ga * ya + gb * yb).astype(o_ref.dtype)


def _merge(o_c, o_s, o_w, gate_logits, o_b, gm, w_up_a, w_up_b, tm=1024, tn=512):
    t, ka = o_c.shape
    kb = o_b.shape[1]
    d = w_up_a.shape[1]
    tm = min(tm, t)
    nj = d // tn
    row = lambda i, j: (i, 0)
    return pl.pallas_call(
        _merge_body,
        grid=(t // tm, nj),
        in_specs=[pl.BlockSpec((tm, ka), row), pl.BlockSpec((tm, ka), row), pl.BlockSpec((tm, ka), row),
                  pl.BlockSpec((tm, LANES), row), pl.BlockSpec((tm, kb), row),
                  pl.BlockSpec((tm, tn), lambda i, j: (i, j)),
                  pl.BlockSpec((tm, tn), lambda i, j: (i, j + nj)),
                  pl.BlockSpec((ka, tn), lambda i, j: (0, j)),
                  pl.BlockSpec((kb, tn), lambda i, j: (0, j))],
        out_specs=pl.BlockSpec((tm, tn), lambda i, j: (i, j)),
        out_shape=jax.ShapeDtypeStruct((t, d), BF16),
        scratch_shapes=[pltpu.VMEM((tm, ka), BF16)],
        compiler_params=_cparams(("parallel", "arbitrary")),
        name="merge_up",
    )(o_c, o_s, o_w, gate_logits, o_b, gm, gm, w_up_a, w_up_b)


def _route_body(x_ref, g_ref, w_ref, b_ref, h_ref, info_ref, cnt_ref, carry_ref, *, tm):
    @pl.when(pl.program_id(0) == 0)
    def _():
        carry_ref[...] = jnp.zeros(carry_ref.shape, F32)

    x = x_ref[...]
    ms = jnp.mean(x * x, axis=-1, keepdims=True)
    h = x * lax.rsqrt(ms + RMS_EPS) * g_ref[...]
    _store_packed(h_ref, h, tm)
    w = w_ref[...]
    h1 = h.astype(BF16)
    h2 = (h - h1.astype(F32)).astype(BF16)
    w1 = w.astype(BF16)
    w2 = (w - w1.astype(F32)).astype(BF16)
    logits = (jnp.dot(h1, w1, preferred_element_type=F32) + jnp.dot(h1, w2, preferred_element_type=F32)
              + jnp.dot(h2, w1, preferred_element_type=F32)) + b_ref[...]
    lane = lax.broadcasted_iota(jnp.int32, (tm, LANES), 1)
    lanef = lane.astype(F32)

    is_g = lane < N_GROUPS
    gl = jnp.where(is_g, logits, -MASK_BIG)
    ge = jnp.where(is_g, jnp.exp(gl - jnp.max(gl, axis=-1, keepdims=True)), 0.0)
    gp = ge / jnp.sum(ge, axis=-1, keepdims=True)
    g_val = jnp.max(gp, axis=-1, keepdims=True)
    g_idx = jnp.min(jnp.where(gp == g_val, lanef, float(LANES)), axis=-1, keepdims=True)

    lane_grp = ((lane - N_GROUPS) // EXPERTS_PER_GROUP).astype(F32)
    in_e = jnp.where(lane >= N_GROUPS, jnp.where(lane < N_GROUPS + N_EXPERTS, 1.0, 0.0), 0.0)
    is_e = jnp.where(lane_grp == g_idx, in_e, 0.0) > 0.5
    el = jnp.where(is_e, logits, -MASK_BIG)
    ee = jnp.where(is_e, jnp.exp(el - jnp.max(el, axis=-1, keepdims=True)), 0.0)
    ep = jnp.where(is_e, ee / jnp.sum(ee, axis=-1, keepdims=True), -1.0)
    v1 = jnp.max(ep, axis=-1, keepdims=True)
    l1 = jnp.min(jnp.where(ep == v1, lanef, float(LANES)), axis=-1, keepdims=True)
    ep2 = jnp.where(lanef == l1, -1.0, ep)
    v2 = jnp.max(ep2, axis=-1, keepdims=True)
    l2 = jnp.min(jnp.where(ep2 == v2, lanef, float(LANES)), axis=-1, keepdims=True)
    vs = v1 + v2
    wt1 = g_val * v1 / vs
    wt2 = g_val * v2 / vs
    e1 = l1 - float(N_GROUPS)
    e2 = l2 - float(N_GROUPS)

    oh = jnp.where(lanef == e1, 1.0, jnp.where(lanef == e2, 1.0, 0.0))
    r_i = lax.broadcasted_iota(jnp.int32, (tm, tm), 0)
    c_i = lax.broadcasted_iota(jnp.int32, (tm, tm), 1)
    tri = jnp.where(r_i > c_i, 1.0, 0.0).astype(BF16)
    base = jnp.dot(tri, oh.astype(BF16), preferred_element_type=F32) + carry_ref[...]
    r1 = jnp.sum(jnp.where(lanef == e1, base, 0.0), axis=-1, keepdims=True)
    r2 = jnp.sum(jnp.where(lanef == e2, base, 0.0), axis=-1, keepdims=True)
    carry_ref[...] = carry_ref[...] + jnp.sum(oh, axis=0, keepdims=True)
    cnt_ref[...] = jnp.broadcast_to(carry_ref[...], cnt_ref.shape)
    info = jnp.where(lane == 0, e1, jnp.where(lane == 1, e2, jnp.where(lane == 2, wt1, jnp.where(
        lane == 3, wt2, jnp.where(lane == 4, r1, jnp.where(lane == 5, r2, 0.0))))))
    info_ref[...] = info


def _route(x1, g, w_gr, b_gr, tm=512):
    t, d = x1.shape
    return pl.pallas_call(
        functools.partial(_route_body, tm=tm),
        grid=(t // tm,),
        in_specs=[pl.BlockSpec((tm, d), lambda i: (i, 0)),
                  pl.BlockSpec((1, d), lambda i: (0, 0)),
                  pl.BlockSpec((d, LANES), lambda i: (0, 0)),
                  pl.BlockSpec((1, LANES), lambda i: (0, 0))],
        out_specs=[pl.BlockSpec((tm * ROW_SUB, LANES), lambda i: (i, 0)),
                   pl.BlockSpec((tm, LANES), lambda i: (i, 0)),
                   pl.BlockSpec((8, LANES), lambda i: (0, 0))],
        out_shape=[jax.ShapeDtypeStruct((t * ROW_SUB, LANES), jnp.uint32),
                   jax.ShapeDtypeStruct((t, LANES), F32),
                   jax.ShapeDtypeStruct((8, LANES), F32)],
        scratch_shapes=[pltpu.VMEM((1, LANES), F32)],
        compiler_params=_cparams(("arbitrary",)),
        name="moe_route",
    )(x1, g.reshape(1, d), w_gr, b_gr)


ROW_SUB = 8
U32 = jnp.uint32


def _pack_pairs(lo, hi):
    lo_b = lax.bitcast_convert_type(lo.astype(BF16).astype(F32), U32)
    hi_b = lax.bitcast_convert_type(hi.astype(BF16).astype(F32), U32)
    return lax.shift_right_logical(lo_b, U32(16)) | (hi_b & U32(0xFFFF0000))


def _unpack_pairs(w):
    lo = lax.bitcast_convert_type(lax.shift_left(w, U32(16)), F32)
    hi = lax.bitcast_convert_type(w & U32(0xFFFF0000), F32)
    return lo, hi


def _store_packed(ref, y, n):
    half = y.shape[1] // 2
    for s in range(ROW_SUB):
        cs = slice(s * LANES, (s + 1) * LANES)
        ref[pl.ds(s, n, stride=ROW_SUB), :] = _pack_pairs(y[:, cs], y[:, half + s * LANES:half + (s + 1) * LANES])


def _load_packed(ref, n):
    los, his = [], []
    for s in range(ROW_SUB):
        lo, hi = _unpack_pairs(ref[pl.ds(s, n, stride=ROW_SUB), :])
        los.append(lo)
        his.append(hi)
    return jnp.concatenate(los + his, axis=1)


def _row_copy(src_ref, src_row, dst_ref, dst_row, sem):
    return pltpu.make_async_copy(src_ref.at[pl.ds(pl.multiple_of(src_row * ROW_SUB, ROW_SUB), ROW_SUB)],
                                 dst_ref.at[pl.ds(pl.multiple_of(dst_row * ROW_SUB, ROW_SUB), ROW_SUB)], sem)


def _dispatch_body(dest_ref, h_ref, xs_ref, sem, *, tm):
    base = pl.program_id(0) * tm

    def issue(r, c):
        for k in range(EXPERT_TOPK):
            _row_copy(h_ref, r, xs_ref, dest_ref[EXPERT_TOPK * (base + r) + k], sem).start(priority=k % 2)
        return c

    lax.fori_loop(0, tm, issue, 0, unroll=4)

    def drain(r, c):
        for k in range(EXPERT_TOPK):
            _row_copy(h_ref, r, xs_ref, dest_ref[EXPERT_TOPK * (base + r) + k], sem).wait()
        return c

    lax.fori_loop(0, tm, drain, 0, unroll=4)


def _dispatch(dest, hp, tm=512):
    t = hp.shape[0] // ROW_SUB
    n_rows = dest.shape[0]
    return pl.pallas_call(
        functools.partial(_dispatch_body, tm=tm),
        grid_spec=pltpu.PrefetchScalarGridSpec(
            num_scalar_prefetch=1,
            grid=(t // tm,),
            in_specs=[pl.BlockSpec((tm * ROW_SUB, LANES), lambda i, dest: (i, 0))],
            out_specs=pl.BlockSpec(memory_space=pl.ANY),
            scratch_shapes=[pltpu.SemaphoreType.DMA(())],
        ),
        out_shape=jax.ShapeDtypeStruct((n_rows * ROW_SUB, LANES), U32),
        compiler_params=_cparams(("arbitrary",)),
        name="moe_dispatch",
    )(dest, hp)


def _expert_body(nu_ref, sq_ref, es_ref, ns_ref, blk_ref, lo_ref, hi_ref, x_ref, wg_hbm, wu_hbm, wd_hbm, y_ref,
                 wg_b, wu_b, wd_b, wg_s, wu_s, wd_s, y_acc, sem):
    i = pl.program_id(0)
    nu = nu_ref[0]
    ns = ns_ref[0]

    def weight_copies(seq, slot):
        e = es_ref[seq]
        return (pltpu.make_async_copy(wg_hbm.at[e], wg_b.at[slot], sem.at[slot]),
                pltpu.make_async_copy(wu_hbm.at[e], wu_b.at[slot], sem.at[slot]),
                pltpu.make_async_copy(wd_hbm.at[e], wd_b.at[slot], sem.at[slot]))

    def start_weights(seq, slot):
        for c in weight_copies(seq, slot):
            c.start()

    @pl.when(i == 0)
    def _():
        y_acc[...] = jnp.zeros(y_acc.shape, F32)
        start_weights(0, 0)

        @pl.when(ns > 1)
        def _():
            start_weights(1, 1)

    s = sq_ref[i]
    first = (i == 0) | (s != sq_ref[jnp.maximum(i - 1, 0)])

    @pl.when((i < nu) & first)
    def _():
        slot = lax.rem(s, 2)
        for c in weight_copies(s, slot):
            c.wait()
        def cast_rows(dst, src, chunk):
            def body(c, carry):
                rs = pl.ds(pl.multiple_of(c * chunk, chunk), chunk)
                dst[rs, :] = src[slot, rs, :].astype(BF16)
                return carry
            lax.fori_loop(0, dst.shape[0] // chunk, body, 0)

        cast_rows(wg_s, wg_b, 256)
        cast_rows(wu_s, wu_b, 256)
        cast_rows(wd_s, wd_b, 64)

        @pl.when(s + 2 < ns)
        def _():
            start_weights(s + 2, slot)

    @pl.when(i < nu)
    def _():
        x = _load_packed(x_ref, MOE_ROWS).astype(BF16)
        g = jnp.dot(x, wg_s[...], preferred_element_type=F32)
        u = jnp.dot(x, wu_s[...], preferred_element_type=F32)
        mid = (jax.nn.silu(g) * u).astype(BF16)
        y = jnp.dot(mid, wd_s[...], preferred_element_type=F32)
        row = lax.broadcasted_iota(jnp.int32, (MOE_ROWS, 1), 0)
        mine = (row >= lo_ref[i]) & (row < hi_ref[i])
        new_block = (i == 0) | (blk_ref[i] != blk_ref[jnp.maximum(i - 1, 0)])
        y_acc[...] = jnp.where(mine, y, jnp.where(new_block, 0.0, y_acc[...]))
        _store_packed(y_ref, y_acc[...], MOE_ROWS)


def _experts(n_pairs, seq_of_pair, expert_of_seq, n_seq, blk_of_pair, lo, hi, xs, w_gate, w_up, w_down):
    n_steps = seq_of_pair.shape[0]
    _, d, ff = w_gate.shape
    assert d == 2 * ROW_SUB * LANES, "a packed row must be exactly one (8,128) tile"
    blk = lambda i, nu, sq, es, ns, bk, lo, hi: (bk[jnp.minimum(i, nu[0] - 1)], 0)
    hbm = pl.BlockSpec(memory_space=pl.ANY)
    return pl.pallas_call(
        _expert_body,
        grid_spec=pltpu.PrefetchScalarGridSpec(
            num_scalar_prefetch=7,
            grid=(n_steps,),
            in_specs=[pl.BlockSpec((MOE_ROWS * ROW_SUB, LANES), blk), hbm, hbm, hbm],
            out_specs=pl.BlockSpec((MOE_ROWS * ROW_SUB, LANES), blk),
            scratch_shapes=[pltpu.VMEM((2, d, ff), F32), pltpu.VMEM((2, d, ff), F32), pltpu.VMEM((2, ff, d), F32),
                            pltpu.VMEM((d, ff), BF16), pltpu.VMEM((d, ff), BF16), pltpu.VMEM((ff, d), BF16),
                            pltpu.VMEM((MOE_ROWS, d), F32), pltpu.SemaphoreType.DMA((2,))],
        ),
        out_shape=jax.ShapeDtypeStruct(xs.shape, U32),
        compiler_params=_cparams(("arbitrary",)),
        name="moe_experts",
    )(n_pairs, seq_of_pair, expert_of_seq, n_seq, blk_of_pair, lo, hi, xs, w_gate, w_up, w_down)


def _combine_body(dest_ref, x_ref, info_ref, g_ref, ys_ref, o_ref, buf0, buf1, sem, *, tm):
    i = pl.program_id(0)
    slot = lax.rem(i, 2)
    bufs = (buf0, buf1)

    def gather(tile, sl, wait):
        def body(r, c):
            for k in range(EXPERT_TOPK):
                cp = _row_copy(ys_ref, dest_ref[EXPERT_TOPK * (tile * tm + r) + k], bufs[k].at[sl], r, sem.at[sl])
                if wait:
                    cp.wait()
                else:
                    cp.start(priority=k % 2)
            return c
        lax.fori_loop(0, tm, body, 0, unroll=4)

    @pl.when(i == 0)
    def _():
        gather(0, 0, False)

    @pl.when(i + 1 < pl.num_programs(0))
    def _():
        gather(i + 1, 1 - slot, False)

    gather(i, slot, True)
    info = info_ref[...]
    y = x_ref[...] + (info[:, 2:3] * _load_packed(buf0.at[slot], tm) + info[:, 3:4] * _load_packed(buf1.at[slot], tm))
    ms = jnp.mean(y * y, axis=-1, keepdims=True)
    o_ref[...] = y * lax.rsqrt(ms + RMS_EPS) * g_ref[...]


def _combine(dest, x1, info, g, ys, tm=256):
    t, d = x1.shape
    return pl.pallas_call(
        functools.partial(_combine_body, tm=tm),
        grid_spec=pltpu.PrefetchScalarGridSpec(
            num_scalar_prefetch=1,
            grid=(t // tm,),
            in_specs=[pl.BlockSpec((tm, d), lambda i, dest: (i, 0)),
                      pl.BlockSpec((tm, LANES), lambda i, dest: (i, 0)),
                      pl.BlockSpec((1, d), lambda i, dest: (0, 0)),
                      pl.BlockSpec(memory_space=pl.ANY)],
            out_specs=pl.BlockSpec((tm, d), lambda i, dest: (i, 0)),
            scratch_shapes=[pltpu.VMEM((2, tm * ROW_SUB, LANES), U32), pltpu.VMEM((2, tm * ROW_SUB, LANES), U32),
                            pltpu.SemaphoreType.DMA((2,))],
        ),
        out_shape=jax.ShapeDtypeStruct((t, d), F32),
        compiler_params=_cparams(("arbitrary",)),
        name="moe_combine",
    )(dest, x1, info, g.reshape(1, d), ys)


def _nsa(proj, slab, pe_k, w1_k, w2_k, pe_v, w1_v, w2_v, tab, b, s):
    g, dh = NSA_KV_HEADS, HEAD_DIM
    qw = NSA_HEADS * dh
    nc = s // CMP_STRIDE
    kvc = _compress(slab, pe_k, w1_k, w2_k, pe_v, w1_v, w2_v, g)

    c_start = np.arange(nc)[None, :] * CMP_STRIDE
    n_sel = s // SEL_BLOCK
    sb = np.arange(n_sel)[:, None] * SEL_BLOCK
    overlap = jnp.asarray((c_start < sb + SEL_BLOCK) & (c_start + CMP_BLOCK > sb), BF16)
    o_c, member = _nsa_cmp(proj, kvc, tab, overlap, b, s)

    t = ATT_TILE
    nq = s // t
    nn = _n_near(t)
    nh = ATT_HEADS_PER_STEP
    kblk = qw // (nh // NSA_GROUP * dh)
    per = g // (nh // NSA_GROUP)
    bias_d = _bias_vecs(tab, t, nn + 1)
    o_s = _flash(proj, 0, proj, kblk, proj, kblk + per, bias_d, NSA_HEADS, NSA_GROUP, nh,
                 _causal_pairs(nq, nn), True, member=member, et=_block_onehot(s, SEL_BLOCK), name="nsa_selected")
    n_back = -(-WINDOW // t)
    bias_w = _bias_vecs(tab, t, n_back + 1, window=WINDOW)
    o_w = _flash(proj, 0, proj, kblk + 2 * per, proj, kblk + 3 * per, bias_w, NSA_HEADS, NSA_GROUP, nh,
                 _window_pairs(nq, n_back), False, name="nsa_window")
    return o_c, o_s, o_w


def _moba(proj, tab, b, s):
    member = _moba_gate(proj, b, s)
    t = ATT_TILE
    nn = _n_near(t)
    nh = ATT_HEADS_PER_STEP
    ng = MOBA_HEADS // nh
    bias_d = _bias_vecs(tab, t, nn + 1)
    return _flash(proj, 0, proj, ng, proj, 2 * ng, bias_d, MOBA_HEADS, 1, nh, _causal_pairs(s // t, nn),
                  True, member=member, et=_block_onehot(s, MOBA_BLOCK), name="moba_attn")


def _moe(x1, g_ffn, w_group, b_group, w_router, b_router, w_gate, w_up, w_down, g_final):
    t, d = x1.shape
    ng, _, epg = w_router.shape
    w_gr = jnp.concatenate([w_group, jnp.transpose(w_router, (1, 0, 2)).reshape(d, ng * epg),
                            jnp.zeros((d, LANES - ng - ng * epg), F32)], axis=1)
    b_gr = jnp.concatenate([b_group, b_router.reshape(-1), jnp.zeros((LANES - ng - ng * epg,), F32)]).reshape(1, LANES)
    h, info, cnt = _route(x1, g_ffn, w_gr, b_gr)
    n_e = ng * epg
    n_assign = t * EXPERT_TOPK
    assert n_assign % MOE_ROWS == 0
    n_blocks = n_assign // MOE_ROWS
    counts = cnt[0, :n_e].astype(jnp.int32)
    end = jnp.cumsum(counts)
    start = end - counts
    expert = info[:, 0:EXPERT_TOPK].astype(jnp.int32)
    rank = info[:, 4:4 + EXPERT_TOPK].astype(jnp.int32)
    e_ids = jnp.arange(n_e, dtype=jnp.int32)
    dest = (jnp.sum(jnp.where(expert[..., None] == e_ids, start, 0), axis=-1) + rank).reshape(-1)
    owns = counts > 0
    seq_of_expert = jnp.cumsum(owns.astype(jnp.int32)) - 1
    n_seq = jnp.sum(owns.astype(jnp.int32)).reshape(1)
    expert_of_seq = jnp.sum(jnp.where(owns[None, :] & (seq_of_expert[None, :] == e_ids[:, None]), e_ids[None, :], 0),
                            axis=1)
    first = start // MOE_ROWS
    last = jnp.where(owns, (end - 1) // MOE_ROWS, first - 1)
    pair_end = jnp.cumsum(last - first + 1)
    pair_start = pair_end - (last - first + 1)
    n_steps = n_blocks + n_e
    p_ids = jnp.arange(n_steps, dtype=jnp.int32)
    e_of_pair = jnp.minimum(jnp.sum((pair_end[None, :] <= p_ids[:, None]).astype(jnp.int32), axis=1), n_e - 1)
    pick = e_of_pair[:, None] == e_ids[None, :]
    lookup = lambda v: jnp.sum(jnp.where(pick, v[None, :], 0), axis=1)
    blk_of_pair = jnp.clip(lookup(first) + p_ids - lookup(pair_start), 0, n_blocks - 1)
    lo = jnp.clip(lookup(start) - blk_of_pair * MOE_ROWS, 0, MOE_ROWS)
    hi = jnp.clip(lookup(end) - blk_of_pair * MOE_ROWS, 0, MOE_ROWS)
    seq_of_pair = lookup(seq_of_expert)
    n_pairs = pair_end[-1:].astype(jnp.int32)
    xs = _dispatch(dest, h)
    ys = _experts(n_pairs, seq_of_pair, expert_of_seq, n_seq, blk_of_pair, lo, hi, xs, w_gate, w_up, w_down)
    return _combine(dest, x1, info, g_final, ys)


def kernel(x, rel_bias, norm_mix, w_in, cmp_pe_k, cmp_w1_k, cmp_w2_k, cmp_pe_v, cmp_w1_v, cmp_w2_v, w_up_nsa,
           w_up_moba, w_out, norm_ffn, w_group, b_group, w_router, b_router, w_exp_gate, w_exp_up, w_exp_down,
           final_norm):
    b, s, d = x.shape
    t = b * s
    depth = w_in.shape[0]
    tab_a = rel_bias[:, :NSA_HEADS]
    tab_b = rel_bias[:, NSA_HEADS:]
    a_cols = NSA_HEADS * HEAD_DIM + 6 * NSA_KV_HEADS * HEAD_DIM
    gate_cols = 3 * NSA_HEADS
    b_cols = 3 * MOBA_HEADS * HEAD_DIM
    xt = x.reshape(t, d)
    out = None
    for l in range(depth):
        wt = jnp.swapaxes(w_in[l], 0, 1)
        b_col0 = a_cols + gate_cols
        q_cols = NSA_HEADS * HEAD_DIM
        cmp_cols = 2 * NSA_KV_HEADS * HEAD_DIM
        h, slab, gate_a = _proj_head(xt, norm_mix[l], wt, q_cols, cmp_cols, a_cols, CMP_STRIDE)
        slab = slab.reshape(b, s // CMP_STRIDE, CMP_STRIDE * cmp_cols)
        proj_a = _matmul(h, wt, 0, a_cols - cmp_cols, BF16, scaled_cols=q_cols, col_scale=Q_SCALE,
                         w_transposed=True, skip_at=q_cols, skip_cols=cmp_cols,
                         name="in_proj_a").reshape(b, s, a_cols - cmp_cols)
        proj_b = _matmul(h, wt, b_col0, b_cols, BF16, scaled_cols=MOBA_HEADS * HEAD_DIM, col_scale=Q_SCALE,
                         w_transposed=True, name="in_proj_b").reshape(b, s, b_cols)
        gm = _matmul(h, wt, b_col0 + b_cols, 2 * d, BF16, w_transposed=True, name="in_proj_gm")
        o_c, o_s, o_w = _nsa(proj_a, slab, cmp_pe_k[l], cmp_w1_k[l], cmp_w2_k[l],
                             cmp_pe_v[l], cmp_w1_v[l], cmp_w2_v[l], tab_a, b, s)
        o_b = _moba(proj_b, tab_b, b, s)
        merged = _merge(o_c.reshape(t, -1), o_s.reshape(t, -1), o_w.reshape(t, -1), gate_a,
                        o_b.reshape(t, -1), gm, w_up_nsa[l], w_up_moba[l])
        x1 = _matmul(merged, w_out[l], 0, d, F32, res=xt, name="out_proj")
        assert l == depth - 1, "only the last layer's MoE is fused with the final norm"
        out = _moe(x1, norm_ffn[l], w_group[l], b_group[l], w_router[l], b_router[l],
                   w_exp_gate[l], w_exp_up[l], w_exp_down[l], final_norm)
    return out.reshape(b, s, d)
```

```python
import functools
import math

import numpy as np
import jax
import jax.numpy as jnp
from jax import lax
from jax.experimental import pallas as pl
from jax.experimental.pallas import tpu as pltpu

F32 = jnp.float32
BF16 = jnp.bfloat16

HEAD_DIM = 128
NSA_HEADS = 8
NSA_KV_HEADS = 2
NSA_GROUP = NSA_HEADS // NSA_KV_HEADS
CMP_BLOCK = 32
CMP_STRIDE = 16
SEL_BLOCK = 64
SEL_TOPN = 16
WINDOW = 512
FORCED_SCORE = 1e4
MOBA_HEADS = 8
MOBA_BLOCK = 256
MOBA_TOPK = 3
REL_BUCKETS = 32
REL_MAX_DIST = 128
N_GROUPS = 8
EXPERTS_PER_GROUP = 8
N_EXPERTS = N_GROUPS * EXPERTS_PER_GROUP
EXPERT_TOPK = 2
RMS_EPS = 1e-6

LANES = 128
ATT_TILE = 512
ATT_HEADS_PER_STEP = 8
ATT_VMEM_LIMIT = 56 * 1024 * 1024
MOE_ROWS = 256
MASK_BIG = 1e30
M_INIT = -3e38
LOG2E = math.log2(math.e)
Q_SCALE = HEAD_DIM ** -0.5 * LOG2E
VMEM_LIMIT = 48 * 1024 * 1024


def _cparams(sem, vmem=VMEM_LIMIT):
    return pltpu.CompilerParams(dimension_semantics=sem, vmem_limit_bytes=vmem)


def _mm_body(*refs, has_res, n_scaled, col_scale, w_transposed):
    if has_res:
        a_ref, w_ref, r_ref, o_ref = refs
    else:
        a_ref, w_ref, o_ref = refs
    w = w_ref[...].astype(BF16)
    if w_transposed:
        acc = lax.dot_general(a_ref[...], w, (((1,), (1,)), ((), ())), preferred_element_type=F32)
    else:
        acc = jnp.dot(a_ref[...], w, preferred_element_type=F32)
    if n_scaled:
        acc = acc * jnp.where(pl.program_id(1) < n_scaled, col_scale, 1.0)
    if has_res:
        acc = acc + r_ref[...]
    o_ref[...] = acc.astype(o_ref.dtype)


def _matmul(a, w, col0, ncols, out_dtype, res=None, scaled_cols=0, col_scale=1.0, w_transposed=False,
            skip_at=0, skip_cols=0, tm=2048, tn=512, name="matmul"):
    t, k = a.shape
    tn = min(tn, ncols)
    tm = min(tm, t)
    assert ncols % tn == 0 and t % tm == 0 and scaled_cols % tn == 0 and skip_at % tn == 0
    if w_transposed:
        assert col0 % 8 == 0 and skip_cols % 8 == 0
        gap_block = skip_at // tn if skip_cols else ncols // tn
        w_spec = pl.BlockSpec(
            (pl.Element(tn), pl.Element(k)),
            lambda i, j: (pl.multiple_of(col0 + j * tn + jnp.where(j >= gap_block, skip_cols, 0), 8), 0))
    else:
        assert col0 % tn == 0
        off = col0 // tn
        w_spec = pl.BlockSpec((k, tn), lambda i, j: (0, j + off))
    in_specs = [pl.BlockSpec((tm, k), lambda i, j: (i, 0)), w_spec]
    args = [a, w]
    if res is not None:
        in_specs.append(pl.BlockSpec((tm, tn), lambda i, j: (i, j)))
        args.append(res)
    return pl.pallas_call(
        functools.partial(_mm_body, has_res=res is not None, n_scaled=scaled_cols // tn, col_scale=col_scale,
                          w_transposed=w_transposed),
        grid=(t // tm, ncols // tn),
        in_specs=in_specs,
        out_specs=pl.BlockSpec((tm, tn), lambda i, j: (i, j)),
        out_shape=jax.ShapeDtypeStruct((t, ncols), out_dtype),
        compiler_params=_cparams(("parallel", "parallel")),
        name=name,
    )(*args)


def _proj_head_body(x_ref, g_ref, ws_ref, wg_ref, h_ref, slab_ref, gate_ref, acc_ref, *, per):
    x = x_ref[...]
    ms = jnp.mean(x * x, axis=-1, keepdims=True)
    h = (x * lax.rsqrt(ms + RMS_EPS) * g_ref[...]).astype(BF16)
    h_ref[...] = h
    nt = (((1,), (1,)), ((), ()))
    gate_ref[...] = lax.dot_general(h, wg_ref[...].astype(BF16), nt, preferred_element_type=F32)
    acc = lax.dot_general(h, ws_ref[...].astype(BF16), nt, preferred_element_type=F32)
    n_chunks, tm, _ = acc_ref.shape
    tn = n_chunks * LANES
    for c in range(n_chunks):
        acc_ref[c] = acc[:, c * LANES:(c + 1) * LANES]
    for r in range(per):
        for c in range(n_chunks):
            slab_ref[:, r * tn + c * LANES:r * tn + (c + 1) * LANES] = (
                acc_ref[c, pl.ds(r, tm // per, stride=per), :].astype(slab_ref.dtype))


def _proj_head(x, g, wt, slab_col0, slab_cols, gate_col0, per, tm=1024):
    t, k = x.shape
    tm = min(tm, t)
    assert t % tm == 0 and tm % (per * 8) == 0 and slab_col0 % 8 == 0 and gate_col0 % 8 == 0
    return pl.pallas_call(
        functools.partial(_proj_head_body, per=per),
        grid=(t // tm,),
        in_specs=[pl.BlockSpec((tm, k), lambda i: (i, 0)),
                  pl.BlockSpec((1, k), lambda i: (0, 0)),
                  pl.BlockSpec((pl.Element(slab_cols), pl.Element(k)), lambda i: (slab_col0, 0)),
                  pl.BlockSpec((pl.Element(LANES), pl.Element(k)), lambda i: (gate_col0, 0))],
        out_specs=[pl.BlockSpec((tm, k), lambda i: (i, 0)),
                   pl.BlockSpec((tm // per, per * slab_cols), lambda i: (i, 0)),
                   pl.BlockSpec((tm, LANES), lambda i: (i, 0))],
        out_shape=[jax.ShapeDtypeStruct((t, k), BF16),
                   jax.ShapeDtypeStruct((t // per, per * slab_cols), BF16),
                   jax.ShapeDtypeStruct((t, LANES), F32)],
        scratch_shapes=[pltpu.VMEM((slab_cols // LANES, tm, LANES), F32)],
        compiler_params=_cparams(("parallel",)),
        name="in_proj_head",
    )(x, g.reshape(1, k), wt, wt)


def _compress_body(u_ref, pek_ref, w1k_ref, w2k_ref, pev_ref, w1v_ref, w2v_ref, o_ref, *, nc, g):
    dh = HEAD_DIM
    tok_w = 2 * g * dh
    for kv, (pe_ref, w1_ref, w2_ref) in enumerate(((pek_ref, w1k_ref, w2k_ref), (pev_ref, w1v_ref, w2v_ref))):
        w1 = w1_ref[...].astype(BF16)
        w2 = w2_ref[...].astype(BF16)
        half = CMP_STRIDE * dh
        peb = jnp.dot(pe_ref[...].astype(BF16), w1, preferred_element_type=F32)[0:1]
        for gi in range(g):
            a = b = None
            for r in range(CMP_STRIDE):
                c0 = r * tok_w + (kv * g + gi) * dh
                piece = u_ref[:, c0:c0 + dh]
                da = jnp.dot(piece, w1[r * dh:(r + 1) * dh], preferred_element_type=F32)
                db = jnp.dot(piece, w1[half + r * dh:half + (r + 1) * dh], preferred_element_type=F32)
                a = da if a is None else a + da
                b = db if b is None else b + db
            pre = a + pltpu.roll(b, nc - 1, 0) + peb
            hid = jax.nn.gelu(pre)
            o_ref[kv * g + gi] = jnp.dot(hid.astype(BF16), w2, preferred_element_type=F32).astype(o_ref.dtype)


def _compress(u, pe_k, w1_k, w2_k, pe_v, w1_v, w2_v, g):
    b, nc, kk = u.shape
    hid = w1_k.shape[1]
    dh = w2_k.shape[1]
    pe16 = lambda pe: jnp.broadcast_to(pe.reshape(1, -1), (16, pe.size))
    full = lambda shape: pl.BlockSpec(shape, lambda i: (0,) * len(shape))
    return pl.pallas_call(
        functools.partial(_compress_body, nc=nc, g=g),
        grid=(b,),
        in_specs=[pl.BlockSpec((None, nc, kk), lambda i: (i, 0, 0)),
                  full((16, CMP_BLOCK * dh)), full((CMP_BLOCK * dh, hid)), full((hid, dh)),
                  full((16, CMP_BLOCK * dh)), full((CMP_BLOCK * dh, hid)), full((hid, dh))],
        out_specs=pl.BlockSpec((None, 2 * g, nc, dh), lambda i: (i, 0, 0, 0)),
        out_shape=jax.ShapeDtypeStruct((b, 2 * g, nc, dh), BF16),
        compiler_params=_cparams(("parallel",)),
        name="nsa_compress",
    )(u, pe16(pe_k), w1_k, w2_k, pe16(pe_v), w1_v, w2_v)


def _split3(x):
    p1 = x.astype(BF16)
    r = x - p1.astype(F32)
    p2 = r.astype(BF16)
    p3 = (r - p2.astype(F32)).astype(BF16)
    return p1, p2, p3


def _rank_count(score, n_rows):
    groups = []
    for g0 in range(0, n_rows, 8):
        sg = score[g0:min(g0 + 8, n_rows), :]
        n_iota = g0 + lax.broadcasted_iota(jnp.int32, sg.shape, 0)
        cnt = jnp.zeros(sg.shape, F32)
        for m in range(n_rows):
            row = score[m:m + 1, :]
            if m < g0:
                beats = row >= sg
            elif m >= g0 + 8:
                beats = row > sg
            else:
                tie = jnp.where(n_iota > m, 1.0, 0.0)
                beats = jnp.where(row > sg, 1.0, jnp.where(row == sg, tie, 0.0)) > 0.5
            cnt = cnt + jnp.where(beats, 1.0, 0.0)
        groups.append(cnt)
    return jnp.concatenate(groups, axis=0) if len(groups) > 1 else groups[0]


def _nsa_cmp_body(q_ref, kc_ref, vc_ref, bias_ref, ov_ref, oc_ref, mem_ref, *, tq, nc, n_sel):
    t0 = pl.program_id(2) * tq
    kc = kc_ref[...]
    vc = vc_ref[...]
    t_idx = t0 + lax.broadcasted_iota(jnp.int32, (tq, nc), 0)
    c_idx = lax.broadcasted_iota(jnp.int32, (tq, nc), 1)
    dist = t_idx - (c_idx * CMP_STRIDE + (CMP_BLOCK - 1))
    n_k = REL_MAX_DIST // CMP_STRIDE
    kidx = jnp.where(dist < 0, n_k + 1, jnp.minimum(lax.shift_right_logical(dist, 4), n_k))
    assert CMP_STRIDE == 16
    psum = jnp.zeros((tq, nc), F32)
    for j in range(NSA_GROUP):
        hs = slice(j * HEAD_DIM, (j + 1) * HEAD_DIM)
        gt = bias_ref[j] * LOG2E
        bias = jnp.concatenate([jnp.take_along_axis(gt, kidx[:, c0:c0 + LANES], axis=1)
                                for c0 in range(0, nc, LANES)], axis=1)
        s = lax.dot_general(q_ref[:, hs], kc, (((1,), (1,)), ((), ())), preferred_element_type=F32) + bias
        m = jnp.max(s, axis=-1, keepdims=True)
        m = jnp.where(m > -0.5 * MASK_BIG, m, 0.0)
        p = jnp.exp2(s - m)
        d = jnp.sum(p, axis=-1, keepdims=True)
        p = p / jnp.where(d > 0, d, 1.0)
        oc_ref[:, hs] = jnp.dot(p.astype(BF16), vc, preferred_element_type=F32).astype(oc_ref.dtype)
        psum = psum + p
    ov = ov_ref[...]
    nt = (((1,), (1,)), ((), ()))
    p1, p2, p3 = _split3(psum)
    psel = (lax.dot_general(ov, p1, nt, preferred_element_type=F32)
            + lax.dot_general(ov, p2, nt, preferred_element_type=F32)
            + lax.dot_general(ov, p3, nt, preferred_element_type=F32))
    n_idx = lax.broadcasted_iota(jnp.int32, (n_sel, tq), 0)
    tt = t0 + lax.broadcasted_iota(jnp.int32, (n_sel, tq), 1)
    cur = tt // SEL_BLOCK
    forced = jnp.where(n_idx == 0, 1.0, jnp.where(n_idx == cur, 1.0, jnp.where(n_idx == cur - 1, 1.0, 0.0)))
    score = jnp.where(forced > 0.5, FORCED_SCORE, jnp.where(n_idx * SEL_BLOCK <= tt, psel, -1.0))
    cnt = _rank_count(score, n_sel)
    member = jnp.where(cnt < float(min(SEL_TOPN, n_sel)), 1.0, 0.0)
    if n_sel < LANES:
        member = jnp.concatenate([member, jnp.zeros((LANES - n_sel, tq), F32)], axis=0)
    mem_ref[...] = member.T.astype(mem_ref.dtype)


def _nsa_cmp(proj, kvc, tab, overlap, b, s, tq=256):
    g = NSA_KV_HEADS
    nc = kvc.shape[2]
    n_sel = s // SEL_BLOCK
    gw = NSA_GROUP * HEAD_DIM
    assert tq % CMP_STRIDE == 0 and REL_MAX_DIST % CMP_STRIDE == 0
    n_k = REL_MAX_DIST // CMP_STRIDE
    rho = (np.arange(tq)[:, None] - (CMP_BLOCK - 1)) % CMP_STRIDE
    dd = np.concatenate([rho + CMP_STRIDE * np.arange(n_k)[None, :], np.full((tq, 1), REL_MAX_DIST)], axis=1)
    gtab = jnp.transpose(_bias_of_dist(tab, dd), (2, 0, 1)).astype(F32)
    bias_c = jnp.concatenate([gtab, jnp.full(gtab.shape[:2] + (1,), -MASK_BIG, F32),
                              jnp.zeros(gtab.shape[:2] + (LANES - n_k - 2,), F32)], axis=2)
    body = functools.partial(_nsa_cmp_body, tq=tq, nc=nc, n_sel=n_sel)
    return pl.pallas_call(
        body,
        grid=(b, g, s // tq),
        in_specs=[pl.BlockSpec((None, tq, gw), lambda bi, gi, i: (bi, i, gi)),
                  pl.BlockSpec((None, None, nc, HEAD_DIM), lambda bi, gi, i: (bi, gi, 0, 0)),
                  pl.BlockSpec((None, None, nc, HEAD_DIM), lambda bi, gi, i: (bi, g + gi, 0, 0)),
                  pl.BlockSpec((NSA_GROUP, tq, LANES), lambda bi, gi, i: (gi, 0, 0)),
                  pl.BlockSpec((n_sel, nc), lambda bi, gi, i: (0, 0))],
        out_specs=[pl.BlockSpec((None, tq, gw), lambda bi, gi, i: (bi, i, gi)),
                   pl.BlockSpec((None, None, tq, LANES), lambda bi, gi, i: (bi, gi, i, 0))],
        out_shape=[jax.ShapeDtypeStruct((b, s, NSA_HEADS * HEAD_DIM), BF16),
                   jax.ShapeDtypeStruct((b, g, s, LANES), BF16)],
        compiler_params=_cparams(("parallel", "parallel", "parallel")),
        name="nsa_cmp_select",
    )(proj, kvc, kvc, bias_c, overlap)


def _moba_gate_body(q_ref, k_ref, mem_ref, *, s, nblk):
    k = k_ref[...].astype(F32)
    kmean = jnp.mean(k.reshape(nblk, MOBA_BLOCK, HEAD_DIM), axis=1)
    k1 = kmean.astype(BF16)
    k2 = (kmean - k1.astype(F32)).astype(BF16)
    q = q_ref[...]
    nt = (((1,), (1,)), ((), ()))
    gate = (lax.dot_general(k1, q, nt, preferred_element_type=F32)
            + lax.dot_general(k2, q, nt, preferred_element_type=F32))
    n_idx = lax.broadcasted_iota(jnp.int32, (nblk, s), 0)
    own = lax.broadcasted_iota(jnp.int32, (nblk, s), 1) // MOBA_BLOCK
    past = n_idx < own
    score = jnp.where(past, gate, -MASK_BIG)
    cnt = _rank_count(score, nblk)
    n_top = max(1, min(MOBA_TOPK, nblk - 1))
    sel = jnp.where(past, jnp.where(cnt < float(n_top), 1.0, 0.0), 0.0)
    member = jnp.where(n_idx == own, 1.0, sel)
    member = jnp.concatenate([member, jnp.zeros((LANES - nblk, s), F32)], axis=0)
    mem_ref[...] = member.T.astype(mem_ref.dtype)


def _moba_gate(proj, b, s):
    h = MOBA_HEADS
    nblk = s // MOBA_BLOCK
    return pl.pallas_call(
        functools.partial(_moba_gate_body, s=s, nblk=nblk),
        grid=(b, h),
        in_specs=[pl.BlockSpec((None, s, HEAD_DIM), lambda bi, hi: (bi, 0, hi)),
                  pl.BlockSpec((None, s, HEAD_DIM), lambda bi, hi: (bi, 0, h + hi))],
        out_specs=pl.BlockSpec((None, None, s, LANES), lambda bi, hi: (bi, hi, 0, 0)),
        out_shape=jax.ShapeDtypeStruct((b, h, s, LANES), BF16),
        compiler_params=_cparams(("parallel", "parallel")),
        name="moba_gate",
    )(proj, proj)


def _flash_body(qi_ref, ki_ref, bo_ref, fl_ref, *refs, nh, ratio, nm, n_near, has_far):
    if nm:
        (q_ref, k_ref, v_ref, bvec_ref, mem_ref, et_ref, o_ref,
         m_ref, l_ref, acc_ref, sh_ref, al_ref, bias_ref, s_ref, p_ref) = refs
    else:
        q_ref, k_ref, v_ref, bvec_ref, o_ref, m_ref, l_ref, acc_ref, sh_ref, al_ref, bias_ref, s_ref, p_ref = refs
    del qi_ref, ki_ref
    p = pl.program_id(2)
    flag = fl_ref[p]
    bo = bo_ref[p]
    t = q_ref.shape[0]
    rows = 64

    @pl.when(p == 0)
    def _():
        for h in range(nh):
            for o in range(n_near):
                vec = bvec_ref[h, o][0:1, :] * LOG2E
                for rc in range(t // rows):
                    x = pltpu.roll(jnp.broadcast_to(vec, (rows, 2 * t)), rc * rows, 1, stride=1, stride_axis=0)
                    bias_ref[h, o, rc * rows:(rc + 1) * rows, :] = x[:, :t]

    @pl.when((flag & 1) != 0)
    def _():
        m_ref[...] = jnp.full(m_ref.shape, M_INIT, F32)
        l_ref[...] = jnp.zeros(l_ref.shape, F32)
        acc_ref[...] = jnp.zeros(acc_ref.shape, F32)

    nt = (((1,), (1,)), ((), ()))
    reps = t // LANES

    def step(near):
        def pass1(h):
            hs = slice(h * HEAD_DIM, (h + 1) * HEAD_DIM)
            kv = h // ratio
            ks = slice(kv * HEAD_DIM, (kv + 1) * HEAD_DIM)
            q = q_ref[:, hs]
            k = k_ref[:, ks]
            if nm:
                mneg = mem_ref[h // (nh // nm)] - 1.0
                q = jnp.concatenate([q, mneg.astype(BF16)], axis=1)
                k = jnp.concatenate([k, et_ref[...]], axis=1)
            sc = lax.dot_general(q, k, nt, preferred_element_type=F32)
            m_prev = m_ref[h]
            if near:
                sc = sc + bias_ref[h, bo]
                m_new = jnp.maximum(m_prev, jnp.max(sc, axis=-1, keepdims=True))
                sh_ref[h] = m_new
            else:
                cfar = bvec_ref[h, n_near][0:1, 0:LANES] * LOG2E
                m_new = jnp.maximum(m_prev, jnp.max(sc, axis=-1, keepdims=True) + cfar)
                sh_ref[h] = m_new - cfar
            s_ref[h] = sc
            al_ref[h] = jnp.exp2(m_prev - m_new)
            m_ref[h] = m_new

        def pass2(h):
            ks = slice(h // ratio * HEAD_DIM, (h // ratio + 1) * HEAD_DIM)
            for rc in range(t // rows):
                rs = slice(rc * rows, (rc + 1) * rows)
                pm = jnp.exp2(s_ref[h, rs, :] - jnp.tile(sh_ref[h, rs, :], (1, reps)))
                l_ref[h, rs, :] = al_ref[h, rs, :] * l_ref[h, rs, :] + jnp.sum(pm, axis=-1, keepdims=True)
                p_ref[h, rs, :] = pm.astype(BF16)
            acc_ref[h] = al_ref[h] * acc_ref[h] + jnp.dot(p_ref[h], v_ref[:, ks], preferred_element_type=F32)

        for h in range(nh):
            pass1(h)
        for h in range(nh):
            pass2(h)

    if has_far:
        pl.when(bo < n_near)(lambda: step(True))
        pl.when(bo >= n_near)(lambda: step(False))
    else:
        step(True)

    @pl.when((flag & 2) != 0)
    def _():
        for h in range(nh):
            l = l_ref[h]
            o_ref[:, h * HEAD_DIM:(h + 1) * HEAD_DIM] = (acc_ref[h] / jnp.where(l > 0, l, 1.0)).astype(o_ref.dtype)


def _flash(q_arr, q_off, k_arr, k_off, v_arr, v_off, bias, n_heads, ratio, nh, pairs, has_far, member=None,
           et=None, name="flash"):
    b, s, _ = q_arr.shape
    t = ATT_TILE
    nkv = nh // ratio
    ng = n_heads // nh
    qi = jnp.asarray([p[0] for p in pairs], jnp.int32)
    ki = jnp.asarray([p[1] for p in pairs], jnp.int32)
    bo = jnp.asarray([p[2] for p in pairs], jnp.int32)
    fl = jnp.asarray([p[3] for p in pairs], jnp.int32)
    nb = bias.shape[1]
    nm = 0
    in_specs = [
        pl.BlockSpec((None, t, nh * HEAD_DIM), lambda bi, gi, p, qi, ki, bo, fl: (bi, qi[p], q_off + gi)),
        pl.BlockSpec((None, t, nkv * HEAD_DIM), lambda bi, gi, p, qi, ki, bo, fl: (bi, ki[p], k_off + gi)),
        pl.BlockSpec((None, t, nkv * HEAD_DIM), lambda bi, gi, p, qi, ki, bo, fl: (bi, ki[p], v_off + gi)),
        pl.BlockSpec((nh, nb, 8, 2 * t), lambda bi, gi, p, qi, ki, bo, fl: (gi, 0, 0, 0)),
    ]
    args = [q_arr, k_arr, v_arr, bias]
    if member is not None:
        nm = member.shape[1] // ng
        in_specs += [
            pl.BlockSpec((None, nm, t, LANES), lambda bi, gi, p, qi, ki, bo, fl: (bi, gi, qi[p], 0)),
            pl.BlockSpec((t, LANES), lambda bi, gi, p, qi, ki, bo, fl: (ki[p], 0)),
        ]
        args += [member, et]
    n_near = nb - 1 if has_far else nb
    body = functools.partial(_flash_body, nh=nh, ratio=ratio, nm=nm, n_near=n_near, has_far=has_far)
    return pl.pallas_call(
        body,
        grid_spec=pltpu.PrefetchScalarGridSpec(
            num_scalar_prefetch=4,
            grid=(b, ng, len(pairs)),
            in_specs=in_specs,
            out_specs=pl.BlockSpec((None, t, nh * HEAD_DIM), lambda bi, gi, p, qi, ki, bo, fl: (bi, qi[p], gi)),
            scratch_shapes=[pltpu.VMEM((nh, t, LANES), F32)] * 5 + [pltpu.VMEM((nh, n_near, t, t), F32),
                                                                    pltpu.VMEM((nh, t, t), F32),
                                                                    pltpu.VMEM((nh, t, t), BF16)],
        ),
        out_shape=jax.ShapeDtypeStruct((b, s, n_heads * HEAD_DIM), BF16),
        compiler_params=_cparams(("parallel", "parallel", "arbitrary"), vmem=ATT_VMEM_LIMIT),
        name=name,
    )(qi, ki, bo, fl, *args)


def _rel_bucket(dist):
    n = jnp.maximum(jnp.asarray(dist, jnp.int32), 0)
    max_exact = REL_BUCKETS // 2
    nf = jnp.maximum(n, 1).astype(jnp.float32)
    large = max_exact + (jnp.log(nf / max_exact) / math.log(REL_MAX_DIST / max_exact)
                         * (REL_BUCKETS - max_exact)).astype(jnp.int32)
    return jnp.where(n < max_exact, n, jnp.minimum(large, REL_BUCKETS - 1))


def _bias_of_dist(tab, dist):
    hit = _rel_bucket(dist)[..., None, None] == jnp.arange(REL_BUCKETS)[:, None]
    return jnp.sum(jnp.where(hit, tab, 0.0), axis=-2)


def _n_near(t):
    return -(-(REL_MAX_DIST - 1 + t) // t)


def _bias_vecs(tab, t, n_off, window=None):
    k = np.arange(2 * t)[None, :]
    dist = np.arange(n_off)[:, None] * t + np.where(k < t, -k, 2 * t - k)
    ok = dist >= 0
    if window is not None:
        ok &= dist < window
    bias = jnp.where(jnp.asarray(ok)[..., None], _bias_of_dist(tab, dist), -MASK_BIG)
    bias = jnp.transpose(bias, (2, 0, 1)).astype(F32)
    return jnp.broadcast_to(bias[:, :, None, :], (bias.shape[0], n_off, 8, 2 * t))


def _causal_pairs(nq, n_near):
    pairs = []
    for qi in range(nq):
        for ki in range(qi + 1):
            pairs.append((qi, ki, min(qi - ki, n_near), (1 if ki == 0 else 0) | (2 if ki == qi else 0)))
    return pairs


def _window_pairs(nq, n_back):
    pairs = []
    for qi in range(nq):
        lo = max(0, qi - n_back)
        for ki in range(lo, qi + 1):
            pairs.append((qi, ki, qi - ki, (1 if ki == lo else 0) | (2 if ki == qi else 0)))
    return pairs


def _block_onehot(s, blk):
    return jnp.asarray(np.where(np.arange(s)[:, None] // blk == np.arange(LANES)[None, :], MASK_BIG, 0.0), BF16)


def _merge_body(oc_ref, os_ref, ow_ref, gl_ref, ob_ref, h_ref, wga_ref, wgb_ref, wa_ref, wb_ref, o_ref, oa_ref):
    @pl.when(pl.program_id(1) == 0)
    def _():
        gates = jax.nn.sigmoid(gl_ref[...])
        for h in range(NSA_HEADS):
            hs = slice(h * HEAD_DIM, (h + 1) * HEAD_DIM)
            mix = (gates[:, 3 * h:3 * h + 1] * oc_ref[:, hs].astype(F32)
                   + gates[:, 3 * h + 1:3 * h + 2] * os_ref[:, hs].astype(F32)
                   + gates[:, 3 * h + 2:3 * h + 3] * ow_ref[:, hs].astype(F32))
            oa_ref[:, hs] = mix.astype(BF16)

    nt = (((1,), (1,)), ((), ()))
    h = h_ref[...]
    ya = jnp.dot(oa_ref[...], wa_ref[...].astype(BF16), preferred_element_type=F32)
    ga = jax.nn.sigmoid(lax.dot_general(h, wga_ref[...].astype(BF16), nt, preferred_element_type=F32))
    part = ga * ya
    yb = jnp.dot(ob_ref[...], wb_ref[...].astype(BF16), preferred_element_type=F32)
    gb = jax.nn.sigmoid(lax.dot_general(h, wgb_ref[...].astype(BF16), nt, preferred_element_type=F32))
    o_ref[...] = (part + gb * yb).astype(o_ref.dtype)


def _merge(o_c, o_s, o_w, gate_logits, o_b, h, wt, gm_col0, w_up_a, w_up_b, tm=1024, tn=256):
    t, ka = o_c.shape
    kb = o_b.shape[1]
    k = h.shape[1]
    d = w_up_a.shape[1]
    tm = min(tm, t)
    assert gm_col0 % 8 == 0 and d % 8 == 0
    row = lambda i, j: (i, 0)
    gwin = lambda off: pl.BlockSpec((pl.Element(tn), pl.Element(k)),
                                    lambda i, j: (pl.multiple_of(gm_col0 + off + j * tn, 8), 0))
    return pl.pallas_call(
        _merge_body,
        grid=(t // tm, d // tn),
        in_specs=[pl.BlockSpec((tm, ka), row), pl.BlockSpec((tm, ka), row), pl.BlockSpec((tm, ka), row),
                  pl.BlockSpec((tm, LANES), row), pl.BlockSpec((tm, kb), row), pl.BlockSpec((tm, k), row),
                  gwin(0), gwin(d),
                  pl.BlockSpec((ka, tn), lambda i, j: (0, j)),
                  pl.BlockSpec((kb, tn), lambda i, j: (0, j))],
        out_specs=pl.BlockSpec((tm, tn), lambda i, j: (i, j)),
        out_shape=jax.ShapeDtypeStruct((t, d), BF16),
        scratch_shapes=[pltpu.VMEM((tm, ka), BF16)],
        compiler_params=_cparams(("parallel", "arbitrary"), vmem=ATT_VMEM_LIMIT),
        name="merge_up",
    )(o_c, o_s, o_w, gate_logits, o_b, h, wt, wt, w_up_a, w_up_b)


def _route_body(x_ref, g_ref, w_ref, b_ref, h_ref, info_ref, cnt_ref, carry_ref, *, tm):
    @pl.when(pl.program_id(0) == 0)
    def _():
        carry_ref[...] = jnp.zeros(carry_ref.shape, F32)

    x = x_ref[...]
    ms = jnp.mean(x * x, axis=-1, keepdims=True)
    h = x * lax.rsqrt(ms + RMS_EPS) * g_ref[...]
    _store_packed(h_ref, h, tm)
    w = w_ref[...]
    h1 = h.astype(BF16)
    h2 = (h - h1.astype(F32)).astype(BF16)
    w1 = w.astype(BF16)
    w2 = (w - w1.astype(F32)).astype(BF16)
    logits = (jnp.dot(h1, w1, preferred_element_type=F32) + jnp.dot(h1, w2, preferred_element_type=F32)
              + jnp.dot(h2, w1, preferred_element_type=F32)) + b_ref[...]
    lane = lax.broadcasted_iota(jnp.int32, (tm, LANES), 1)
    lanef = lane.astype(F32)

    is_g = lane < N_GROUPS
    gl = jnp.where(is_g, logits, -MASK_BIG)
    ge = jnp.where(is_g, jnp.exp(gl - jnp.max(gl, axis=-1, keepdims=True)), 0.0)
    gp = ge / jnp.sum(ge, axis=-1, keepdims=True)
    g_val = jnp.max(gp, axis=-1, keepdims=True)
    g_idx = jnp.min(jnp.where(gp == g_val, lanef, float(LANES)), axis=-1, keepdims=True)

    lane_grp = ((lane - N_GROUPS) // EXPERTS_PER_GROUP).astype(F32)
    in_e = jnp.where(lane >= N_GROUPS, jnp.where(lane < N_GROUPS + N_EXPERTS, 1.0, 0.0), 0.0)
    is_e = jnp.where(lane_grp == g_idx, in_e, 0.0) > 0.5
    el = jnp.where(is_e, logits, -MASK_BIG)
    ee = jnp.where(is_e, jnp.exp(el - jnp.max(el, axis=-1, keepdims=True)), 0.0)
    ep = jnp.where(is_e, ee / jnp.sum(ee, axis=-1, keepdims=True), -1.0)
    v1 = jnp.max(ep, axis=-1, keepdims=True)
    l1 = jnp.min(jnp.where(ep == v1, lanef, float(LANES)), axis=-1, keepdims=True)
    ep2 = jnp.where(lanef == l1, -1.0, ep)
    v2 = jnp.max(ep2, axis=-1, keepdims=True)
    l2 = jnp.min(jnp.where(ep2 == v2, lanef, float(LANES)), axis=-1, keepdims=True)
    vs = v1 + v2
    wt1 = g_val * v1 / vs
    wt2 = g_val * v2 / vs
    e1 = l1 - float(N_GROUPS)
    e2 = l2 - float(N_GROUPS)

    oh = jnp.where(lanef == e1, 1.0, jnp.where(lanef == e2, 1.0, 0.0))
    r_i = lax.broadcasted_iota(jnp.int32, (tm, tm), 0)
    c_i = lax.broadcasted_iota(jnp.int32, (tm, tm), 1)
    tri = jnp.where(r_i > c_i, 1.0, 0.0).astype(BF16)
    base = jnp.dot(tri, oh.astype(BF16), preferred_element_type=F32) + carry_ref[...]
    r1 = jnp.sum(jnp.where(lanef == e1, base, 0.0), axis=-1, keepdims=True)
    r2 = jnp.sum(jnp.where(lanef == e2, base, 0.0), axis=-1, keepdims=True)
    carry_ref[...] = carry_ref[...] + jnp.sum(oh, axis=0, keepdims=True)
    cnt_ref[...] = jnp.broadcast_to(carry_ref[...], cnt_ref.shape)
    info = jnp.where(lane == 0, e1, jnp.where(lane == 1, e2, jnp.where(lane == 2, wt1, jnp.where(
        lane == 3, wt2, jnp.where(lane == 4, r1, jnp.where(lane == 5, r2, 0.0))))))
    info_ref[...] = info


def _route(x1, g, w_gr, b_gr, tm=512):
    t, d = x1.shape
    return pl.pallas_call(
        functools.partial(_route_body, tm=tm),
        grid=(t // tm,),
        in_specs=[pl.BlockSpec((tm, d), lambda i: (i, 0)),
                  pl.BlockSpec((1, d), lambda i: (0, 0)),
                  pl.BlockSpec((d, LANES), lambda i: (0, 0)),
                  pl.BlockSpec((1, LANES), lambda i: (0, 0))],
        out_specs=[pl.BlockSpec((tm * ROW_SUB, LANES), lambda i: (i, 0)),
                   pl.BlockSpec((tm, LANES), lambda i: (i, 0)),
                   pl.BlockSpec((8, LANES), lambda i: (0, 0))],
        out_shape=[jax.ShapeDtypeStruct((t * ROW_SUB, LANES), jnp.uint32),
                   jax.ShapeDtypeStruct((t, LANES), F32),
                   jax.ShapeDtypeStruct((8, LANES), F32)],
        scratch_shapes=[pltpu.VMEM((1, LANES), F32)],
        compiler_params=_cparams(("arbitrary",)),
        name="moe_route",
    )(x1, g.reshape(1, d), w_gr, b_gr)


ROW_SUB = 8
U32 = jnp.uint32


def _pack_pairs(lo, hi):
    lo_b = lax.bitcast_convert_type(lo.astype(BF16).astype(F32), U32)
    hi_b = lax.bitcast_convert_type(hi.astype(BF16).astype(F32), U32)
    return lax.shift_right_logical(lo_b, U32(16)) | (hi_b & U32(0xFFFF0000))


def _unpack_pairs(w):
    lo = lax.bitcast_convert_type(lax.shift_left(w, U32(16)), F32)
    hi = lax.bitcast_convert_type(w & U32(0xFFFF0000), F32)
    return lo, hi


def _store_packed(ref, y, n):
    half = y.shape[1] // 2
    for s in range(ROW_SUB):
        cs = slice(s * LANES, (s + 1) * LANES)
        ref[pl.ds(s, n, stride=ROW_SUB), :] = _pack_pairs(y[:, cs], y[:, half + s * LANES:half + (s + 1) * LANES])


def _load_packed(ref, n):
    los, his = [], []
    for s in range(ROW_SUB):
        lo, hi = _unpack_pairs(ref[pl.ds(s, n, stride=ROW_SUB), :])
        los.append(lo)
        his.append(hi)
    return jnp.concatenate(los + his, axis=1)


def _row_copy(src_ref, src_row, dst_ref, dst_row, sem):
    return pltpu.make_async_copy(src_ref.at[pl.ds(pl.multiple_of(src_row * ROW_SUB, ROW_SUB), ROW_SUB)],
                                 dst_ref.at[pl.ds(pl.multiple_of(dst_row * ROW_SUB, ROW_SUB), ROW_SUB)], sem)


def _dispatch_body(dest_ref, h_ref, xs_ref, sem, *, tm):
    base = pl.program_id(0) * tm

    def issue(r, c):
        for k in range(EXPERT_TOPK):
            _row_copy(h_ref, r, xs_ref, dest_ref[EXPERT_TOPK * (base + r) + k], sem).start(priority=k % 2)
        return c

    lax.fori_loop(0, tm, issue, 0, unroll=4)

    def drain(r, c):
        for k in range(EXPERT_TOPK):
            _row_copy(h_ref, r, xs_ref, dest_ref[EXPERT_TOPK * (base + r) + k], sem).wait()
        return c

    lax.fori_loop(0, tm, drain, 0, unroll=4)


def _dispatch(dest, hp, tm=512):
    t = hp.shape[0] // ROW_SUB
    n_rows = dest.shape[0]
    return pl.pallas_call(
        functools.partial(_dispatch_body, tm=tm),
        grid_spec=pltpu.PrefetchScalarGridSpec(
            num_scalar_prefetch=1,
            grid=(t // tm,),
            in_specs=[pl.BlockSpec((tm * ROW_SUB, LANES), lambda i, dest: (i, 0))],
            out_specs=pl.BlockSpec(memory_space=pl.ANY),
            scratch_shapes=[pltpu.SemaphoreType.DMA(())],
        ),
        out_shape=jax.ShapeDtypeStruct((n_rows * ROW_SUB, LANES), U32),
        compiler_params=_cparams(("arbitrary",)),
        name="moe_dispatch",
    )(dest, hp)


def _expert_body(nu_ref, sq_ref, es_ref, ns_ref, blk_ref, lo_ref, hi_ref, x_ref, wg_hbm, wu_hbm, wd_hbm, y_ref,
                 wg_b, wu_b, wd_b, wg_s, wu_s, wd_s, y_acc, sem):
    i = pl.program_id(0)
    nu = nu_ref[0]
    ns = ns_ref[0]

    def weight_copies(seq, slot):
        e = es_ref[seq]
        return (pltpu.make_async_copy(wg_hbm.at[e], wg_b.at[slot], sem.at[slot]),
                pltpu.make_async_copy(wu_hbm.at[e], wu_b.at[slot], sem.at[slot]),
                pltpu.make_async_copy(wd_hbm.at[e], wd_b.at[slot], sem.at[slot]))

    def start_weights(seq, slot):
        for c in weight_copies(seq, slot):
            c.start()

    @pl.when(i == 0)
    def _():
        y_acc[...] = jnp.zeros(y_acc.shape, F32)
        start_weights(0, 0)

        @pl.when(ns > 1)
        def _():
            start_weights(1, 1)

    s = sq_ref[i]
    first = (i == 0) | (s != sq_ref[jnp.maximum(i - 1, 0)])

    @pl.when((i < nu) & first)
    def _():
        slot = lax.rem(s, 2)
        for c in weight_copies(s, slot):
            c.wait()
        def cast_rows(dst, src, chunk):
            def body(c, carry):
                rs = pl.ds(pl.multiple_of(c * chunk, chunk), chunk)
                dst[rs, :] = src[slot, rs, :].astype(BF16)
                return carry
            lax.fori_loop(0, dst.shape[0] // chunk, body, 0)

        cast_rows(wg_s, wg_b, 256)
        cast_rows(wu_s, wu_b, 256)
        cast_rows(wd_s, wd_b, 64)

        @pl.when(s + 2 < ns)
        def _():
            start_weights(s + 2, slot)

    @pl.when(i < nu)
    def _():
        x = _load_packed(x_ref, MOE_ROWS).astype(BF16)
        g = jnp.dot(x, wg_s[...], preferred_element_type=F32)
        u = jnp.dot(x, wu_s[...], preferred_element_type=F32)
        mid = (jax.nn.silu(g) * u).astype(BF16)
        y = jnp.dot(mid, wd_s[...], preferred_element_type=F32)
        row = lax.broadcasted_iota(jnp.int32, (MOE_ROWS, 1), 0)
        mine = (row >= lo_ref[i]) & (row < hi_ref[i])
        new_block = (i == 0) | (blk_ref[i] != blk_ref[jnp.maximum(i - 1, 0)])
        y_acc[...] = jnp.where(mine, y, jnp.where(new_block, 0.0, y_acc[...]))
        _store_packed(y_ref, y_acc[...], MOE_ROWS)


def _experts(n_pairs, seq_of_pair, expert_of_seq, n_seq, blk_of_pair, lo, hi, xs, w_gate, w_up, w_down):
    n_steps = seq_of_pair.shape[0]
    _, d, ff = w_gate.shape
    assert d == 2 * ROW_SUB * LANES, "a packed row must be exactly one (8,128) tile"
    blk = lambda i, nu, sq, es, ns, bk, lo, hi: (bk[jnp.minimum(i, nu[0] - 1)], 0)
    hbm = pl.BlockSpec(memory_space=pl.ANY)
    return pl.pallas_call(
        _expert_body,
        grid_spec=pltpu.PrefetchScalarGridSpec(
            num_scalar_prefetch=7,
            grid=(n_steps,),
            in_specs=[pl.BlockSpec((MOE_ROWS * ROW_SUB, LANES), blk), hbm, hbm, hbm],
            out_specs=pl.BlockSpec((MOE_ROWS * ROW_SUB, LANES), blk),
            scratch_shapes=[pltpu.VMEM((2, d, ff), F32), pltpu.VMEM((2, d, ff), F32), pltpu.VMEM((2, ff, d), F32),
                            pltpu.VMEM((d, ff), BF16), pltpu.VMEM((d, ff), BF16), pltpu.VMEM((ff, d), BF16),
                            pltpu.VMEM((MOE_ROWS, d), F32), pltpu.SemaphoreType.DMA((2,))],
        ),
        out_shape=jax.ShapeDtypeStruct(xs.shape, U32),
        compiler_params=_cparams(("arbitrary",)),
        name="moe_experts",
    )(n_pairs, seq_of_pair, expert_of_seq, n_seq, blk_of_pair, lo, hi, xs, w_gate, w_up, w_down)


def _combine_body(dest_ref, x_ref, info_ref, g_ref, ys_ref, o_ref, buf0, buf1, sem, *, tm):
    i = pl.program_id(0)
    slot = lax.rem(i, 2)
    bufs = (buf0, buf1)

    def gather(tile, sl, wait):
        def body(r, c):
            for k in range(EXPERT_TOPK):
                cp = _row_copy(ys_ref, dest_ref[EXPERT_TOPK * (tile * tm + r) + k], bufs[k].at[sl], r, sem.at[sl])
                if wait:
                    cp.wait()
                else:
                    cp.start(priority=k % 2)
            return c
        lax.fori_loop(0, tm, body, 0, unroll=4)

    @pl.when(i == 0)
    def _():
        gather(0, 0, False)

    @pl.when(i + 1 < pl.num_programs(0))
    def _():
        gather(i + 1, 1 - slot, False)

    gather(i, slot, True)
    info = info_ref[...]
    y = x_ref[...] + (info[:, 2:3] * _load_packed(buf0.at[slot], tm) + info[:, 3:4] * _load_packed(buf1.at[slot], tm))
    ms = jnp.mean(y * y, axis=-1, keepdims=True)
    o_ref[...] = y * lax.rsqrt(ms + RMS_EPS) * g_ref[...]


def _combine(dest, x1, info, g, ys, tm=256):
    t, d = x1.shape
    return pl.pallas_call(
        functools.partial(_combine_body, tm=tm),
        grid_spec=pltpu.PrefetchScalarGridSpec(
            num_scalar_prefetch=1,
            grid=(t // tm,),
            in_specs=[pl.BlockSpec((tm, d), lambda i, dest: (i, 0)),
                      pl.BlockSpec((tm, LANES), lambda i, dest: (i, 0)),
                      pl.BlockSpec((1, d), lambda i, dest: (0, 0)),
                      pl.BlockSpec(memory_space=pl.ANY)],
            out_specs=pl.BlockSpec((tm, d), lambda i, dest: (i, 0)),
            scratch_shapes=[pltpu.VMEM((2, tm * ROW_SUB, LANES), U32), pltpu.VMEM((2, tm * ROW_SUB, LANES), U32),
                            pltpu.SemaphoreType.DMA((2,))],
        ),
        out_shape=jax.ShapeDtypeStruct((t, d), F32),
        compiler_params=_cparams(("arbitrary",)),
        name="moe_combine",
    )(dest, x1, info, g.reshape(1, d), ys)


def _nsa(proj, slab, pe_k, w1_k, w2_k, pe_v, w1_v, w2_v, tab, b, s):
    g, dh = NSA_KV_HEADS, HEAD_DIM
    qw = NSA_HEADS * dh
    nc = s // CMP_STRIDE
    kvc = _compress(slab, pe_k, w1_k, w2_k, pe_v, w1_v, w2_v, g)

    c_start = np.arange(nc)[None, :] * CMP_STRIDE
    n_sel = s // SEL_BLOCK
    sb = np.arange(n_sel)[:, None] * SEL_BLOCK
    overlap = jnp.asarray((c_start < sb + SEL_BLOCK) & (c_start + CMP_BLOCK > sb), BF16)
    o_c, member = _nsa_cmp(proj, kvc, tab, overlap, b, s)

    t = ATT_TILE
    nq = s // t
    nn = _n_near(t)
    nh = ATT_HEADS_PER_STEP
    kblk = qw // (nh // NSA_GROUP * dh)
    per = g // (nh // NSA_GROUP)
    bias_d = _bias_vecs(tab, t, nn + 1)
    o_s = _flash(proj, 0, proj, kblk, proj, kblk + per, bias_d, NSA_HEADS, NSA_GROUP, nh,
                 _causal_pairs(nq, nn), True, member=member, et=_block_onehot(s, SEL_BLOCK), name="nsa_selected")
    n_back = -(-WINDOW // t)
    bias_w = _bias_vecs(tab, t, n_back + 1, window=WINDOW)
    o_w = _flash(proj, 0, proj, kblk + 2 * per, proj, kblk + 3 * per, bias_w, NSA_HEADS, NSA_GROUP, nh,
                 _window_pairs(nq, n_back), False, name="nsa_window")
    return o_c, o_s, o_w


def _moba(proj, tab, b, s):
    member = _moba_gate(proj, b, s)
    t = ATT_TILE
    nn = _n_near(t)
    nh = ATT_HEADS_PER_STEP
    ng = MOBA_HEADS // nh
    bias_d = _bias_vecs(tab, t, nn + 1)
    return _flash(proj, 0, proj, ng, proj, 2 * ng, bias_d, MOBA_HEADS, 1, nh, _causal_pairs(s // t, nn),
                  True, member=member, et=_block_onehot(s, MOBA_BLOCK), name="moba_attn")


def _moe(x1, g_ffn, w_group, b_group, w_router, b_router, w_gate, w_up, w_down, g_final):
    t, d = x1.shape
    ng, _, epg = w_router.shape
    w_gr = jnp.concatenate([w_group, jnp.transpose(w_router, (1, 0, 2)).reshape(d, ng * epg),
                            jnp.zeros((d, LANES - ng - ng * epg), F32)], axis=1)
    b_gr = jnp.concatenate([b_group, b_router.reshape(-1), jnp.zeros((LANES - ng - ng * epg,), F32)]).reshape(1, LANES)
    h, info, cnt = _route(x1, g_ffn, w_gr, b_gr)
    n_e = ng * epg
    n_assign = t * EXPERT_TOPK
    assert n_assign % MOE_ROWS == 0
    n_blocks = n_assign // MOE_ROWS
    counts = cnt[0, :n_e].astype(jnp.int32)
    end = jnp.cumsum(counts)
    start = end - counts
    expert = info[:, 0:EXPERT_TOPK].astype(jnp.int32)
    rank = info[:, 4:4 + EXPERT_TOPK].astype(jnp.int32)
    e_ids = jnp.arange(n_e, dtype=jnp.int32)
    dest = (jnp.sum(jnp.where(expert[..., None] == e_ids, start, 0), axis=-1) + rank).reshape(-1)
    owns = counts > 0
    seq_of_expert = jnp.cumsum(owns.astype(jnp.int32)) - 1
    n_seq = jnp.sum(owns.astype(jnp.int32)).reshape(1)
    expert_of_seq = jnp.sum(jnp.where(owns[None, :] & (seq_of_expert[None, :] == e_ids[:, None]), e_ids[None, :], 0),
                            axis=1)
    first = start // MOE_ROWS
    last = jnp.where(owns, (end - 1) // MOE_ROWS, first - 1)
    pair_end = jnp.cumsum(last - first + 1)
    pair_start = pair_end - (last - first + 1)
    n_steps = n_blocks + n_e
    p_ids = jnp.arange(n_steps, dtype=jnp.int32)
    e_of_pair = jnp.minimum(jnp.sum((pair_end[None, :] <= p_ids[:, None]).astype(jnp.int32), axis=1), n_e - 1)
    pick = e_of_pair[:, None] == e_ids[None, :]
    lookup = lambda v: jnp.sum(jnp.where(pick, v[None, :], 0), axis=1)
    blk_of_pair = jnp.clip(lookup(first) + p_ids - lookup(pair_start), 0, n_blocks - 1)
    lo = jnp.clip(lookup(start) - blk_of_pair * MOE_ROWS, 0, MOE_ROWS)
    hi = jnp.clip(lookup(end) - blk_of_pair * MOE_ROWS, 0, MOE_ROWS)
    seq_of_pair = lookup(seq_of_expert)
    n_pairs = pair_end[-1:].astype(jnp.int32)
    xs = _dispatch(dest, h)
    ys = _experts(n_pairs, seq_of_pair, expert_of_seq, n_seq, blk_of_pair, lo, hi, xs, w_gate, w_up, w_down)
    return _combine(dest, x1, info, g_final, ys)


def kernel(x, rel_bias, norm_mix, w_in, cmp_pe_k, cmp_w1_k, cmp_w2_k, cmp_pe_v, cmp_w1_v, cmp_w2_v, w_up_nsa,
           w_up_moba, w_out, norm_ffn, w_group, b_group, w_router, b_router, w_exp_gate, w_exp_up, w_exp_down,
           final_norm):
    b, s, d = x.shape
    t = b * s
    depth = w_in.shape[0]
    tab_a = rel_bias[:, :NSA_HEADS]
    tab_b = rel_bias[:, NSA_HEADS:]
    a_cols = NSA_HEADS * HEAD_DIM + 6 * NSA_KV_HEADS * HEAD_DIM
    gate_cols = 3 * NSA_HEADS
    b_cols = 3 * MOBA_HEADS * HEAD_DIM
    xt = x.reshape(t, d)
    out = None
    for l in range(depth):
        wt = jnp.swapaxes(w_in[l], 0, 1)
        b_col0 = a_cols + gate_cols
        q_cols = NSA_HEADS * HEAD_DIM
        cmp_cols = 2 * NSA_KV_HEADS * HEAD_DIM
        h, slab, gate_a = _proj_head(xt, norm_mix[l], wt, q_cols, cmp_cols, a_cols, CMP_STRIDE)
        slab = slab.reshape(b, s // CMP_STRIDE, CMP_STRIDE * cmp_cols)
        proj_a = _matmul(h, wt, 0, a_cols - cmp_cols, BF16, scaled_cols=q_cols, col_scale=Q_SCALE,
                         w_transposed=True, skip_at=q_cols, skip_cols=cmp_cols,
                         name="in_proj_a").reshape(b, s, a_cols - cmp_cols)
        proj_b = _matmul(h, wt, b_col0, b_cols, BF16, scaled_cols=MOBA_HEADS * HEAD_DIM, col_scale=Q_SCALE,
                         w_transposed=True, name="in_proj_b").reshape(b, s, b_cols)
        o_c, o_s, o_w = _nsa(proj_a, slab, cmp_pe_k[l], cmp_w1_k[l], cmp_w2_k[l],
                             cmp_pe_v[l], cmp_w1_v[l], cmp_w2_v[l], tab_a, b, s)
        o_b = _moba(proj_b, tab_b, b, s)
        merged = _merge(o_c.reshape(t, -1), o_s.reshape(t, -1), o_w.reshape(t, -1), gate_a,
                        o_b.reshape(t, -1), h, wt, b_col0 + b_cols, w_up_nsa[l], w_up_moba[l])
        x1 = _matmul(merged, w_out[l], 0, d, F32, res=xt, name="out_proj")
        assert l == depth - 1, "only the last layer's MoE is fused with the final norm"
        out = _moe(x1, norm_ffn[l], w_group[l], b_group[l], w_router[l], b_router[l],
                   w_exp_gate[l], w_exp_up[l], w_exp_down[l], final_norm)
    return out.reshape(b, s, d)
```

```python
import functools
import math

import numpy as np
import jax
import jax.numpy as jnp
from jax import lax
from jax.experimental import pallas as pl
from jax.experimental.pallas import tpu as pltpu

F32 = jnp.float32
BF16 = jnp.bfloat16

HEAD_DIM = 128
NSA_HEADS = 8
NSA_KV_HEADS = 2
NSA_GROUP = NSA_HEADS // NSA_KV_HEADS
CMP_BLOCK = 32
CMP_STRIDE = 16
SEL_BLOCK = 64
SEL_TOPN = 16
WINDOW = 512
FORCED_SCORE = 1e4
MOBA_HEADS = 8
MOBA_BLOCK = 256
MOBA_TOPK = 3
REL_BUCKETS = 32
REL_MAX_DIST = 128
N_GROUPS = 8
EXPERTS_PER_GROUP = 8
N_EXPERTS = N_GROUPS * EXPERTS_PER_GROUP
EXPERT_TOPK = 2
RMS_EPS = 1e-6

LANES = 128
ATT_TILE = 512
ATT_HEADS_PER_STEP = 8
ATT_VMEM_LIMIT = 56 * 1024 * 1024
MOE_ROWS = 256
MASK_BIG = 1e30
M_INIT = -3e38
LOG2E = math.log2(math.e)
Q_SCALE = HEAD_DIM ** -0.5 * LOG2E
VMEM_LIMIT = 48 * 1024 * 1024


def _cparams(sem, vmem=VMEM_LIMIT):
    return pltpu.CompilerParams(dimension_semantics=sem, vmem_limit_bytes=vmem)


def _mm_body(*refs, has_res, n_scaled, col_scale, w_transposed):
    if has_res:
        a_ref, w_ref, r_ref, o_ref = refs
    else:
        a_ref, w_ref, o_ref = refs
    w = w_ref[...].astype(BF16)
    if w_transposed:
        acc = lax.dot_general(a_ref[...], w, (((1,), (1,)), ((), ())), preferred_element_type=F32)
    else:
        acc = jnp.dot(a_ref[...], w, preferred_element_type=F32)
    if n_scaled:
        acc = acc * jnp.where(pl.program_id(1) < n_scaled, col_scale, 1.0)
    if has_res:
        acc = acc + r_ref[...]
    o_ref[...] = acc.astype(o_ref.dtype)


def _matmul(a, w, col0, ncols, out_dtype, res=None, scaled_cols=0, col_scale=1.0, w_transposed=False,
            skip_at=0, skip_cols=0, tm=2048, tn=512, name="matmul"):
    t, k = a.shape
    tn = min(tn, ncols)
    tm = min(tm, t)
    assert ncols % tn == 0 and t % tm == 0 and scaled_cols % tn == 0 and skip_at % tn == 0
    if w_transposed:
        assert col0 % 8 == 0 and skip_cols % 8 == 0
        gap_block = skip_at // tn if skip_cols else ncols // tn
        w_spec = pl.BlockSpec(
            (pl.Element(tn), pl.Element(k)),
            lambda i, j: (pl.multiple_of(col0 + j * tn + jnp.where(j >= gap_block, skip_cols, 0), 8), 0))
    else:
        assert col0 % tn == 0
        off = col0 // tn
        w_spec = pl.BlockSpec((k, tn), lambda i, j: (0, j + off))
    in_specs = [pl.BlockSpec((tm, k), lambda i, j: (i, 0)), w_spec]
    args = [a, w]
    if res is not None:
        in_specs.append(pl.BlockSpec((tm, tn), lambda i, j: (i, j)))
        args.append(res)
    return pl.pallas_call(
        functools.partial(_mm_body, has_res=res is not None, n_scaled=scaled_cols // tn, col_scale=col_scale,
                          w_transposed=w_transposed),
        grid=(t // tm, ncols // tn),
        in_specs=in_specs,
        out_specs=pl.BlockSpec((tm, tn), lambda i, j: (i, j)),
        out_shape=jax.ShapeDtypeStruct((t, ncols), out_dtype),
        compiler_params=_cparams(("parallel", "parallel")),
        name=name,
    )(*args)


def _proj_head_body(x_ref, g_ref, ws_ref, wg_ref, h_ref, slab_ref, gate_ref, acc_ref, *, per):
    x = x_ref[...]
    ms = jnp.mean(x * x, axis=-1, keepdims=True)
    h = (x * lax.rsqrt(ms + RMS_EPS) * g_ref[...]).astype(BF16)
    h_ref[...] = h
    nt = (((1,), (1,)), ((), ()))
    gate_ref[...] = lax.dot_general(h, wg_ref[...].astype(BF16), nt, preferred_element_type=F32)
    acc = lax.dot_general(h, ws_ref[...].astype(BF16), nt, preferred_element_type=F32)
    n_chunks, tm, _ = acc_ref.shape
    tn = n_chunks * LANES
    for c in range(n_chunks):
        acc_ref[c] = acc[:, c * LANES:(c + 1) * LANES]
    for r in range(per):
        for c in range(n_chunks):
            slab_ref[:, r * tn + c * LANES:r * tn + (c + 1) * LANES] = (
                acc_ref[c, pl.ds(r, tm // per, stride=per), :].astype(slab_ref.dtype))


def _proj_head(x, g, wt, slab_col0, slab_cols, gate_col0, per, tm=1024):
    t, k = x.shape
    tm = min(tm, t)
    assert t % tm == 0 and tm % (per * 8) == 0 and slab_col0 % 8 == 0 and gate_col0 % 8 == 0
    return pl.pallas_call(
        functools.partial(_proj_head_body, per=per),
        grid=(t // tm,),
        in_specs=[pl.BlockSpec((tm, k), lambda i: (i, 0)),
                  pl.BlockSpec((1, k), lambda i: (0, 0)),
                  pl.BlockSpec((pl.Element(slab_cols), pl.Element(k)), lambda i: (slab_col0, 0)),
                  pl.BlockSpec((pl.Element(LANES), pl.Element(k)), lambda i: (gate_col0, 0))],
        out_specs=[pl.BlockSpec((tm, k), lambda i: (i, 0)),
                   pl.BlockSpec((tm // per, per * slab_cols), lambda i: (i, 0)),
                   pl.BlockSpec((tm, LANES), lambda i: (i, 0))],
        out_shape=[jax.ShapeDtypeStruct((t, k), BF16),
                   jax.ShapeDtypeStruct((t // per, per * slab_cols), BF16),
                   jax.ShapeDtypeStruct((t, LANES), F32)],
        scratch_shapes=[pltpu.VMEM((slab_cols // LANES, tm, LANES), F32)],
        compiler_params=_cparams(("parallel",)),
        name="in_proj_head",
    )(x, g.reshape(1, k), wt, wt)


def _compress_body(u_ref, pek_ref, w1k_ref, w2k_ref, pev_ref, w1v_ref, w2v_ref, o_ref, *, nc, g):
    dh = HEAD_DIM
    tok_w = 2 * g * dh
    for kv, (pe_ref, w1_ref, w2_ref) in enumerate(((pek_ref, w1k_ref, w2k_ref), (pev_ref, w1v_ref, w2v_ref))):
        w1 = w1_ref[...].astype(BF16)
        w2 = w2_ref[...].astype(BF16)
        half = CMP_STRIDE * dh
        peb = jnp.dot(pe_ref[...].astype(BF16), w1, preferred_element_type=F32)[0:1]
        for gi in range(g):
            a = b = None
            for r in range(CMP_STRIDE):
                c0 = r * tok_w + (kv * g + gi) * dh
                piece = u_ref[:, c0:c0 + dh]
                da = jnp.dot(piece, w1[r * dh:(r + 1) * dh], preferred_element_type=F32)
                db = jnp.dot(piece, w1[half + r * dh:half + (r + 1) * dh], preferred_element_type=F32)
                a = da if a is None else a + da
                b = db if b is None else b + db
            pre = a + pltpu.roll(b, nc - 1, 0) + peb
            hid = jax.nn.gelu(pre)
            o_ref[kv * g + gi] = jnp.dot(hid.astype(BF16), w2, preferred_element_type=F32).astype(o_ref.dtype)


def _compress(u, pe_k, w1_k, w2_k, pe_v, w1_v, w2_v, g):
    b, nc, kk = u.shape
    hid = w1_k.shape[1]
    dh = w2_k.shape[1]
    pe16 = lambda pe: jnp.broadcast_to(pe.reshape(1, -1), (16, pe.size))
    full = lambda shape: pl.BlockSpec(shape, lambda i: (0,) * len(shape))
    return pl.pallas_call(
        functools.partial(_compress_body, nc=nc, g=g),
        grid=(b,),
        in_specs=[pl.BlockSpec((None, nc, kk), lambda i: (i, 0, 0)),
                  full((16, CMP_BLOCK * dh)), full((CMP_BLOCK * dh, hid)), full((hid, dh)),
                  full((16, CMP_BLOCK * dh)), full((CMP_BLOCK * dh, hid)), full((hid, dh))],
        out_specs=pl.BlockSpec((None, 2 * g, nc, dh), lambda i: (i, 0, 0, 0)),
        out_shape=jax.ShapeDtypeStruct((b, 2 * g, nc, dh), BF16),
        compiler_params=_cparams(("parallel",)),
        name="nsa_compress",
    )(u, pe16(pe_k), w1_k, w2_k, pe16(pe_v), w1_v, w2_v)


def _split3(x):
    p1 = x.astype(BF16)
    r = x - p1.astype(F32)
    p2 = r.astype(BF16)
    p3 = (r - p2.astype(F32)).astype(BF16)
    return p1, p2, p3


def _rank_count(score, n_rows):
    groups = []
    for g0 in range(0, n_rows, 8):
        sg = score[g0:min(g0 + 8, n_rows), :]
        n_iota = g0 + lax.broadcasted_iota(jnp.int32, sg.shape, 0)
        cnt = jnp.zeros(sg.shape, F32)
        for m in range(n_rows):
            row = score[m:m + 1, :]
            if m < g0:
                beats = row >= sg
            elif m >= g0 + 8:
                beats = row > sg
            else:
                tie = jnp.where(n_iota > m, 1.0, 0.0)
                beats = jnp.where(row > sg, 1.0, jnp.where(row == sg, tie, 0.0)) > 0.5
            cnt = cnt + jnp.where(beats, 1.0, 0.0)
        groups.append(cnt)
    return jnp.concatenate(groups, axis=0) if len(groups) > 1 else groups[0]


def _nsa_cmp_body(q_ref, kc_ref, vc_ref, bias_ref, ov_ref, oc_ref, mem_ref, *, tq, nc, n_sel):
    t0 = pl.program_id(2) * tq
    kc = kc_ref[...]
    vc = vc_ref[...]
    t_idx = t0 + lax.broadcasted_iota(jnp.int32, (tq, nc), 0)
    c_idx = lax.broadcasted_iota(jnp.int32, (tq, nc), 1)
    dist = t_idx - (c_idx * CMP_STRIDE + (CMP_BLOCK - 1))
    n_k = REL_MAX_DIST // CMP_STRIDE
    kidx = jnp.where(dist < 0, n_k + 1, jnp.minimum(lax.shift_right_logical(dist, 4), n_k))
    assert CMP_STRIDE == 16
    psum = jnp.zeros((tq, nc), F32)
    for j in range(NSA_GROUP):
        hs = slice(j * HEAD_DIM, (j + 1) * HEAD_DIM)
        gt = bias_ref[j] * LOG2E
        bias = jnp.concatenate([jnp.take_along_axis(gt, kidx[:, c0:c0 + LANES], axis=1)
                                for c0 in range(0, nc, LANES)], axis=1)
        s = lax.dot_general(q_ref[:, hs], kc, (((1,), (1,)), ((), ())), preferred_element_type=F32) + bias
        m = jnp.max(s, axis=-1, keepdims=True)
        m = jnp.where(m > -0.5 * MASK_BIG, m, 0.0)
        p = jnp.exp2(s - m)
        d = jnp.sum(p, axis=-1, keepdims=True)
        p = p / jnp.where(d > 0, d, 1.0)
        oc_ref[:, hs] = jnp.dot(p.astype(BF16), vc, preferred_element_type=F32).astype(oc_ref.dtype)
        psum = psum + p
    ov = ov_ref[...]
    nt = (((1,), (1,)), ((), ()))
    p1, p2, p3 = _split3(psum)
    psel = (lax.dot_general(ov, p1, nt, preferred_element_type=F32)
            + lax.dot_general(ov, p2, nt, preferred_element_type=F32)
            + lax.dot_general(ov, p3, nt, preferred_element_type=F32))
    n_idx = lax.broadcasted_iota(jnp.int32, (n_sel, tq), 0)
    tt = t0 + lax.broadcasted_iota(jnp.int32, (n_sel, tq), 1)
    cur = tt // SEL_BLOCK
    forced = jnp.where(n_idx == 0, 1.0, jnp.where(n_idx == cur, 1.0, jnp.where(n_idx == cur - 1, 1.0, 0.0)))
    score = jnp.where(forced > 0.5, FORCED_SCORE, jnp.where(n_idx * SEL_BLOCK <= tt, psel, -1.0))
    cnt = _rank_count(score, n_sel)
    member = jnp.where(cnt < float(min(SEL_TOPN, n_sel)), 1.0, 0.0)
    if n_sel < LANES:
        member = jnp.concatenate([member, jnp.zeros((LANES - n_sel, tq), F32)], axis=0)
    mem_ref[...] = member.T.astype(mem_ref.dtype)


def _nsa_cmp(proj, kvc, tab, overlap, b, s, tq=256):
    g = NSA_KV_HEADS
    nc = kvc.shape[2]
    n_sel = s // SEL_BLOCK
    gw = NSA_GROUP * HEAD_DIM
    assert tq % CMP_STRIDE == 0 and REL_MAX_DIST % CMP_STRIDE == 0
    n_k = REL_MAX_DIST // CMP_STRIDE
    rho = (np.arange(tq)[:, None] - (CMP_BLOCK - 1)) % CMP_STRIDE
    dd = np.concatenate([rho + CMP_STRIDE * np.arange(n_k)[None, :], np.full((tq, 1), REL_MAX_DIST)], axis=1)
    gtab = jnp.transpose(_bias_of_dist(tab, dd), (2, 0, 1)).astype(F32)
    bias_c = jnp.concatenate([gtab, jnp.full(gtab.shape[:2] + (1,), -MASK_BIG, F32),
                              jnp.zeros(gtab.shape[:2] + (LANES - n_k - 2,), F32)], axis=2)
    body = functools.partial(_nsa_cmp_body, tq=tq, nc=nc, n_sel=n_sel)
    return pl.pallas_call(
        body,
        grid=(b, g, s // tq),
        in_specs=[pl.BlockSpec((None, tq, gw), lambda bi, gi, i: (bi, i, gi)),
                  pl.BlockSpec((None, None, nc, HEAD_DIM), lambda bi, gi, i: (bi, gi, 0, 0)),
                  pl.BlockSpec((None, None, nc, HEAD_DIM), lambda bi, gi, i: (bi, g + gi, 0, 0)),
                  pl.BlockSpec((NSA_GROUP, tq, LANES), lambda bi, gi, i: (gi, 0, 0)),
                  pl.BlockSpec((n_sel, nc), lambda bi, gi, i: (0, 0))],
        out_specs=[pl.BlockSpec((None, tq, gw), lambda bi, gi, i: (bi, i, gi)),
                   pl.BlockSpec((None, None, tq, LANES), lambda bi, gi, i: (bi, gi, i, 0))],
        out_shape=[jax.ShapeDtypeStruct((b, s, NSA_HEADS * HEAD_DIM), BF16),
                   jax.ShapeDtypeStruct((b, g, s, LANES), BF16)],
        compiler_params=_cparams(("parallel", "parallel", "parallel")),
        name="nsa_cmp_select",
    )(proj, kvc, kvc, bias_c, overlap)


def _moba_gate_body(q_ref, k_ref, mem_ref, *, s, nblk):
    k = k_ref[...].astype(F32)
    kmean = jnp.mean(k.reshape(nblk, MOBA_BLOCK, HEAD_DIM), axis=1)
    k1 = kmean.astype(BF16)
    k2 = (kmean - k1.astype(F32)).astype(BF16)
    q = q_ref[...]
    nt = (((1,), (1,)), ((), ()))
    gate = (lax.dot_general(k1, q, nt, preferred_element_type=F32)
            + lax.dot_general(k2, q, nt, preferred_element_type=F32))
    n_idx = lax.broadcasted_iota(jnp.int32, (nblk, s), 0)
    own = lax.broadcasted_iota(jnp.int32, (nblk, s), 1) // MOBA_BLOCK
    past = n_idx < own
    score = jnp.where(past, gate, -MASK_BIG)
    cnt = _rank_count(score, nblk)
    n_top = max(1, min(MOBA_TOPK, nblk - 1))
    sel = jnp.where(past, jnp.where(cnt < float(n_top), 1.0, 0.0), 0.0)
    member = jnp.where(n_idx == own, 1.0, sel)
    member = jnp.concatenate([member, jnp.zeros((LANES - nblk, s), F32)], axis=0)
    mem_ref[...] = member.T.astype(mem_ref.dtype)


def _moba_gate(proj, b, s):
    h = MOBA_HEADS
    nblk = s // MOBA_BLOCK
    return pl.pallas_call(
        functools.partial(_moba_gate_body, s=s, nblk=nblk),
        grid=(b, h),
        in_specs=[pl.BlockSpec((None, s, HEAD_DIM), lambda bi, hi: (bi, 0, hi)),
                  pl.BlockSpec((None, s, HEAD_DIM), lambda bi, hi: (bi, 0, h + hi))],
        out_specs=pl.BlockSpec((None, None, s, LANES), lambda bi, hi: (bi, hi, 0, 0)),
        out_shape=jax.ShapeDtypeStruct((b, h, s, LANES), BF16),
        compiler_params=_cparams(("parallel", "parallel")),
        name="moba_gate",
    )(proj, proj)


def _flash_body(qi_ref, ki_ref, bo_ref, fl_ref, *refs, nh, ratio, nm, n_near, has_far):
    if nm:
        (q_ref, k_ref, v_ref, bvec_ref, mem_ref, et_ref, o_ref,
         m_ref, l_ref, acc_ref, sh_ref, al_ref, bias_ref, s_ref, p_ref) = refs
    else:
        q_ref, k_ref, v_ref, bvec_ref, o_ref, m_ref, l_ref, acc_ref, sh_ref, al_ref, bias_ref, s_ref, p_ref = refs
    del qi_ref, ki_ref
    p = pl.program_id(2)
    flag = fl_ref[p]
    bo = bo_ref[p]
    t = q_ref.shape[0]
    rows = 64

    @pl.when(p == 0)
    def _():
        for h in range(nh):
            for o in range(n_near):
                vec = bvec_ref[h, o][0:1, :] * LOG2E
                for rc in range(t // rows):
                    x = pltpu.roll(jnp.broadcast_to(vec, (rows, 2 * t)), rc * rows, 1, stride=1, stride_axis=0)
                    bias_ref[h, o, rc * rows:(rc + 1) * rows, :] = x[:, :t]

    @pl.when((flag & 1) != 0)
    def _():
        m_ref[...] = jnp.full(m_ref.shape, M_INIT, F32)
        l_ref[...] = jnp.zeros(l_ref.shape, F32)
        acc_ref[...] = jnp.zeros(acc_ref.shape, F32)

    nt = (((1,), (1,)), ((), ()))
    reps = t // LANES

    def step(near):
        def pass1(h):
            hs = slice(h * HEAD_DIM, (h + 1) * HEAD_DIM)
            kv = h // ratio
            ks = slice(kv * HEAD_DIM, (kv + 1) * HEAD_DIM)
            q = q_ref[:, hs]
            k = k_ref[:, ks]
            if nm:
                mneg = mem_ref[h // (nh // nm)] - 1.0
                q = jnp.concatenate([q, mneg.astype(BF16)], axis=1)
                k = jnp.concatenate([k, et_ref[...]], axis=1)
            sc = lax.dot_general(q, k, nt, preferred_element_type=F32)
            m_prev = m_ref[h]
            if near:
                sc = sc + bias_ref[h, bo]
                m_new = jnp.maximum(m_prev, jnp.max(sc, axis=-1, keepdims=True))
                sh_ref[h] = m_new
            else:
                cfar = bvec_ref[h, n_near][0:1, 0:LANES] * LOG2E
                m_new = jnp.maximum(m_prev, jnp.max(sc, axis=-1, keepdims=True) + cfar)
                sh_ref[h] = m_new - cfar
            s_ref[h] = sc
            al_ref[h] = jnp.exp2(m_prev - m_new)
            m_ref[h] = m_new

        def pass2(h):
            ks = slice(h // ratio * HEAD_DIM, (h // ratio + 1) * HEAD_DIM)
            for rc in range(t // rows):
                rs = slice(rc * rows, (rc + 1) * rows)
                pm = jnp.exp2(s_ref[h, rs, :] - jnp.tile(sh_ref[h, rs, :], (1, reps)))
                l_ref[h, rs, :] = al_ref[h, rs, :] * l_ref[h, rs, :] + jnp.sum(pm, axis=-1, keepdims=True)
                p_ref[h, rs, :] = pm.astype(BF16)
            acc_ref[h] = al_ref[h] * acc_ref[h] + jnp.dot(p_ref[h], v_ref[:, ks], preferred_element_type=F32)

        for h in range(nh):
            pass1(h)
        for h in range(nh):
            pass2(h)

    if has_far:
        pl.when(bo < n_near)(lambda: step(True))
        pl.when(bo >= n_near)(lambda: step(False))
    else:
        step(True)

    @pl.when((flag & 2) != 0)
    def _():
        for h in range(nh):
            l = l_ref[h]
            o_ref[:, h * HEAD_DIM:(h + 1) * HEAD_DIM] = (acc_ref[h] / jnp.where(l > 0, l, 1.0)).astype(o_ref.dtype)


def _flash(q_arr, q_off, k_arr, k_off, v_arr, v_off, bias, n_heads, ratio, nh, pairs, has_far, member=None,
           et=None, name="flash"):
    b, s, _ = q_arr.shape
    t = ATT_TILE
    nkv = nh // ratio
    ng = n_heads // nh
    qi = jnp.asarray([p[0] for p in pairs], jnp.int32)
    ki = jnp.asarray([p[1] for p in pairs], jnp.int32)
    bo = jnp.asarray([p[2] for p in pairs], jnp.int32)
    fl = jnp.asarray([p[3] for p in pairs], jnp.int32)
    nb = bias.shape[1]
    nm = 0
    in_specs = [
        pl.BlockSpec((None, t, nh * HEAD_DIM), lambda bi, gi, p, qi, ki, bo, fl: (bi, qi[p], q_off + gi)),
        pl.BlockSpec((None, t, nkv * HEAD_DIM), lambda bi, gi, p, qi, ki, bo, fl: (bi, ki[p], k_off + gi)),
        pl.BlockSpec((None, t, nkv * HEAD_DIM), lambda bi, gi, p, qi, ki, bo, fl: (bi, ki[p], v_off + gi)),
        pl.BlockSpec((nh, nb, 8, 2 * t), lambda bi, gi, p, qi, ki, bo, fl: (gi, 0, 0, 0)),
    ]
    args = [q_arr, k_arr, v_arr, bias]
    if member is not None:
        nm = member.shape[1] // ng
        in_specs += [
            pl.BlockSpec((None, nm, t, LANES), lambda bi, gi, p, qi, ki, bo, fl: (bi, gi, qi[p], 0)),
            pl.BlockSpec((t, LANES), lambda bi, gi, p, qi, ki, bo, fl: (ki[p], 0)),
        ]
        args += [member, et]
    n_near = nb - 1 if has_far else nb
    body = functools.partial(_flash_body, nh=nh, ratio=ratio, nm=nm, n_near=n_near, has_far=has_far)
    return pl.pallas_call(
        body,
        grid_spec=pltpu.PrefetchScalarGridSpec(
            num_scalar_prefetch=4,
            grid=(b, ng, len(pairs)),
            in_specs=in_specs,
            out_specs=pl.BlockSpec((None, t, nh * HEAD_DIM), lambda bi, gi, p, qi, ki, bo, fl: (bi, qi[p], gi)),
            scratch_shapes=[pltpu.VMEM((nh, t, LANES), F32)] * 5 + [pltpu.VMEM((nh, n_near, t, t), F32),
                                                                    pltpu.VMEM((nh, t, t), F32),
                                                                    pltpu.VMEM((nh, t, t), BF16)],
        ),
        out_shape=jax.ShapeDtypeStruct((b, s, n_heads * HEAD_DIM), BF16),
        compiler_params=_cparams(("parallel", "parallel", "arbitrary"), vmem=ATT_VMEM_LIMIT),
        name=name,
    )(qi, ki, bo, fl, *args)


def _rel_bucket(dist):
    n = jnp.maximum(jnp.asarray(dist, jnp.int32), 0)
    max_exact = REL_BUCKETS // 2
    nf = jnp.maximum(n, 1).astype(jnp.float32)
    large = max_exact + (jnp.log(nf / max_exact) / math.log(REL_MAX_DIST / max_exact)
                         * (REL_BUCKETS - max_exact)).astype(jnp.int32)
    return jnp.where(n < max_exact, n, jnp.minimum(large, REL_BUCKETS - 1))


def _bias_of_dist(tab, dist):
    hit = _rel_bucket(dist)[..., None, None] == jnp.arange(REL_BUCKETS)[:, None]
    return jnp.sum(jnp.where(hit, tab, 0.0), axis=-2)


def _n_near(t):
    return -(-(REL_MAX_DIST - 1 + t) // t)


def _bias_vecs(tab, t, n_off, window=None):
    k = np.arange(2 * t)[None, :]
    dist = np.arange(n_off)[:, None] * t + np.where(k < t, -k, 2 * t - k)
    ok = dist >= 0
    if window is not None:
        ok &= dist < window
    bias = jnp.where(jnp.asarray(ok)[..., None], _bias_of_dist(tab, dist), -MASK_BIG)
    bias = jnp.transpose(bias, (2, 0, 1)).astype(F32)
    return jnp.broadcast_to(bias[:, :, None, :], (bias.shape[0], n_off, 8, 2 * t))


def _causal_pairs(nq, n_near):
    pairs = []
    for qi in range(nq):
        for ki in range(qi + 1):
            pairs.append((qi, ki, min(qi - ki, n_near), (1 if ki == 0 else 0) | (2 if ki == qi else 0)))
    return pairs


def _window_pairs(nq, n_back):
    pairs = []
    for qi in range(nq):
        lo = max(0, qi - n_back)
        for ki in range(lo, qi + 1):
            pairs.append((qi, ki, qi - ki, (1 if ki == lo else 0) | (2 if ki == qi else 0)))
    return pairs


def _block_onehot(s, blk):
    return jnp.asarray(np.where(np.arange(s)[:, None] // blk == np.arange(LANES)[None, :], MASK_BIG, 0.0), BF16)


def _merge_body(oc_ref, os_ref, ow_ref, gl_ref, ob_ref, gma_ref, gmb_ref, wa_ref, wb_ref, o_ref, oa_ref):
    @pl.when(pl.program_id(1) == 0)
    def _():
        gates = jax.nn.sigmoid(gl_ref[...])
        for h in range(NSA_HEADS):
            hs = slice(h * HEAD_DIM, (h + 1) * HEAD_DIM)
            mix = (gates[:, 3 * h:3 * h + 1] * oc_ref[:, hs].astype(F32)
                   + gates[:, 3 * h + 1:3 * h + 2] * os_ref[:, hs].astype(F32)
                   + gates[:, 3 * h + 2:3 * h + 3] * ow_ref[:, hs].astype(F32))
            oa_ref[:, hs] = mix.astype(BF16)

    ya = jnp.dot(oa_ref[...], wa_ref[...].astype(BF16), preferred_element_type=F32)
    yb = jnp.dot(ob_ref[...], wb_ref[...].astype(BF16), preferred_element_type=F32)
    ga = jax.nn.sigmoid(gma_ref[...].astype(F32))
    gb = jax.nn.sigmoid(gmb_ref[...].astype(F32))
    o_ref[...] = (ga * ya + gb * yb).astype(o_ref.dtype)


def _merge(o_c, o_s, o_w, gate_logits, o_b, gm, w_up_a, w_up_b, tm=1024, tn=512):
    t, ka = o_c.shape
    kb = o_b.shape[1]
    d = w_up_a.shape[1]
    tm = min(tm, t)
    nj = d // tn
    row = lambda i, j: (i, 0)
    return pl.pallas_call(
        _merge_body,
        grid=(t // tm, nj),
        in_specs=[pl.BlockSpec((tm, ka), row), pl.BlockSpec((tm, ka), row), pl.BlockSpec((tm, ka), row),
                  pl.BlockSpec((tm, LANES), row), pl.BlockSpec((tm, kb), row),
                  pl.BlockSpec((tm, tn), lambda i, j: (i, j)),
                  pl.BlockSpec((tm, tn), lambda i, j: (i, j + nj)),
                  pl.BlockSpec((ka, tn), lambda i, j: (0, j)),
                  pl.BlockSpec((kb, tn), lambda i, j: (0, j))],
        out_specs=pl.BlockSpec((tm, tn), lambda i, j: (i, j)),
        out_shape=jax.ShapeDtypeStruct((t, d), BF16),
        scratch_shapes=[pltpu.VMEM((tm, ka), BF16)],
        compiler_params=_cparams(("parallel", "arbitrary")),
        name="merge_up",
    )(o_c, o_s, o_w, gate_logits, o_b, gm, gm, w_up_a, w_up_b)


def _route_body(x_ref, g_ref, w_ref, b_ref, h_ref, info_ref, cnt_ref, carry_ref, *, tm):
    @pl.when(pl.program_id(0) == 0)
    def _():
        carry_ref[...] = jnp.zeros(carry_ref.shape, F32)

    x = x_ref[...]
    ms = jnp.mean(x * x, axis=-1, keepdims=True)
    h = x * lax.rsqrt(ms + RMS_EPS) * g_ref[...]
    _store_packed(h_ref, h, tm)
    w = w_ref[...]
    h1 = h.astype(BF16)
    h2 = (h - h1.astype(F32)).astype(BF16)
    w1 = w.astype(BF16)
    w2 = (w - w1.astype(F32)).astype(BF16)
    logits = (jnp.dot(h1, w1, preferred_element_type=F32) + jnp.dot(h1, w2, preferred_element_type=F32)
              + jnp.dot(h2, w1, preferred_element_type=F32)) + b_ref[...]
    lane = lax.broadcasted_iota(jnp.int32, (tm, LANES), 1)
    lanef = lane.astype(F32)

    is_g = lane < N_GROUPS
    gl = jnp.where(is_g, logits, -MASK_BIG)
    ge = jnp.where(is_g, jnp.exp(gl - jnp.max(gl, axis=-1, keepdims=True)), 0.0)
    gp = ge / jnp.sum(ge, axis=-1, keepdims=True)
    g_val = jnp.max(gp, axis=-1, keepdims=True)
    g_idx = jnp.min(jnp.where(gp == g_val, lanef, float(LANES)), axis=-1, keepdims=True)

    lane_grp = ((lane - N_GROUPS) // EXPERTS_PER_GROUP).astype(F32)
    in_e = jnp.where(lane >= N_GROUPS, jnp.where(lane < N_GROUPS + N_EXPERTS, 1.0, 0.0), 0.0)
    is_e = jnp.where(lane_grp == g_idx, in_e, 0.0) > 0.5
    el = jnp.where(is_e, logits, -MASK_BIG)
    ee = jnp.where(is_e, jnp.exp(el - jnp.max(el, axis=-1, keepdims=True)), 0.0)
    ep = jnp.where(is_e, ee / jnp.sum(ee, axis=-1, keepdims=True), -1.0)
    v1 = jnp.max(ep, axis=-1, keepdims=True)
    l1 = jnp.min(jnp.where(ep == v1, lanef, float(LANES)), axis=-1, keepdims=True)
    ep2 = jnp.where(lanef == l1, -1.0, ep)
    v2 = jnp.max(ep2, axis=-1, keepdims=True)
    l2 = jnp.min(jnp.where(ep2 == v2, lanef, float(LANES)), axis=-1, keepdims=True)
    vs = v1 + v2
    wt1 = g_val * v1 / vs
    wt2 = g_val * v2 / vs
    e1 = l1 - float(N_GROUPS)
    e2 = l2 - float(N_GROUPS)

    oh = jnp.where(lanef == e1, 1.0, jnp.where(lanef == e2, 1.0, 0.0))
    r_i = lax.broadcasted_iota(jnp.int32, (tm, tm), 0)
    c_i = lax.broadcasted_iota(jnp.int32, (tm, tm), 1)
    tri = jnp.where(r_i > c_i, 1.0, 0.0).astype(BF16)
    base = jnp.dot(tri, oh.astype(BF16), preferred_element_type=F32) + carry_ref[...]
    r1 = jnp.sum(jnp.where(lanef == e1, base, 0.0), axis=-1, keepdims=True)
    r2 = jnp.sum(jnp.where(lanef == e2, base, 0.0), axis=-1, keepdims=True)
    carry_ref[...] = carry_ref[...] + jnp.sum(oh, axis=0, keepdims=True)
    cnt_ref[...] = jnp.broadcast_to(carry_ref[...], cnt_ref.shape)
    info = jnp.where(lane == 0, e1, jnp.where(lane == 1, e2, jnp.where(lane == 2, wt1, jnp.where(
        lane == 3, wt2, jnp.where(lane == 4, r1, jnp.where(lane == 5, r2, 0.0))))))
    info_ref[...] = info


def _route(x1, g, w_gr, b_gr, tm=512):
    t, d = x1.shape
    return pl.pallas_call(
        functools.partial(_route_body, tm=tm),
        grid=(t // tm,),
        in_specs=[pl.BlockSpec((tm, d), lambda i: (i, 0)),
                  pl.BlockSpec((1, d), lambda i: (0, 0)),
                  pl.BlockSpec((d, LANES), lambda i: (0, 0)),
                  pl.BlockSpec((1, LANES), lambda i: (0, 0))],
        out_specs=[pl.BlockSpec((tm * ROW_SUB, LANES), lambda i: (i, 0)),
                   pl.BlockSpec((tm, LANES), lambda i: (i, 0)),
                   pl.BlockSpec((8, LANES), lambda i: (0, 0))],
        out_shape=[jax.ShapeDtypeStruct((t * ROW_SUB, LANES), jnp.uint32),
                   jax.ShapeDtypeStruct((t, LANES), F32),
                   jax.ShapeDtypeStruct((8, LANES), F32)],
        scratch_shapes=[pltpu.VMEM((1, LANES), F32)],
        compiler_params=_cparams(("arbitrary",)),
        name="moe_route",
    )(x1, g.reshape(1, d), w_gr, b_gr)


ROW_SUB = 8
U32 = jnp.uint32


def _pack_pairs(lo, hi):
    lo_b = lax.bitcast_convert_type(lo.astype(BF16).astype(F32), U32)
    hi_b = lax.bitcast_convert_type(hi.astype(BF16).astype(F32), U32)
    return lax.shift_right_logical(lo_b, U32(16)) | (hi_b & U32(0xFFFF0000))


def _unpack_pairs(w):
    lo = lax.bitcast_convert_type(lax.shift_left(w, U32(16)), F32)
    hi = lax.bitcast_convert_type(w & U32(0xFFFF0000), F32)
    return lo, hi


def _store_packed(ref, y, n):
    half = y.shape[1] // 2
    for s in range(ROW_SUB):
        cs = slice(s * LANES, (s + 1) * LANES)
        ref[pl.ds(s, n, stride=ROW_SUB), :] = _pack_pairs(y[:, cs], y[:, half + s * LANES:half + (s + 1) * LANES])


def _load_packed(ref, n):
    los, his = [], []
    for s in range(ROW_SUB):
        lo, hi = _unpack_pairs(ref[pl.ds(s, n, stride=ROW_SUB), :])
        los.append(lo)
        his.append(hi)
    return jnp.concatenate(los + his, axis=1)


def _row_copy(src_ref, src_row, dst_ref, dst_row, sem):
    return pltpu.make_async_copy(src_ref.at[pl.ds(pl.multiple_of(src_row * ROW_SUB, ROW_SUB), ROW_SUB)],
                                 dst_ref.at[pl.ds(pl.multiple_of(dst_row * ROW_SUB, ROW_SUB), ROW_SUB)], sem)


def _dispatch_body(dest_ref, h_ref, xs_ref, sem, *, tm):
    base = pl.program_id(0) * tm

    def issue(r, c):
        for k in range(EXPERT_TOPK):
            _row_copy(h_ref, r, xs_ref, dest_ref[EXPERT_TOPK * (base + r) + k], sem).start(priority=k % 2)
        return c

    lax.fori_loop(0, tm, issue, 0, unroll=4)

    def drain(r, c):
        for k in range(EXPERT_TOPK):
            _row_copy(h_ref, r, xs_ref, dest_ref[EXPERT_TOPK * (base + r) + k], sem).wait()
        return c

    lax.fori_loop(0, tm, drain, 0, unroll=4)


def _dispatch(dest, hp, tm=512):
    t = hp.shape[0] // ROW_SUB
    n_rows = dest.shape[0]
    return pl.pallas_call(
        functools.partial(_dispatch_body, tm=tm),
        grid_spec=pltpu.PrefetchScalarGridSpec(
            num_scalar_prefetch=1,
            grid=(t // tm,),
            in_specs=[pl.BlockSpec((tm * ROW_SUB, LANES), lambda i, dest: (i, 0))],
            out_specs=pl.BlockSpec(memory_space=pl.ANY),
            scratch_shapes=[pltpu.SemaphoreType.DMA(())],
        ),
        out_shape=jax.ShapeDtypeStruct((n_rows * ROW_SUB, LANES), U32),
        compiler_params=_cparams(("arbitrary",)),
        name="moe_dispatch",
    )(dest, hp)


def _expert_body(nu_ref, sq_ref, es_ref, ns_ref, blk_ref, lo_ref, hi_ref, x_ref, wg_hbm, wu_hbm, wd_hbm, y_ref,
                 wg_b, wu_b, wd_b, wg_s, wu_s, wd_s, x_s, y_acc, sem):
    i = pl.program_id(0)
    nu = nu_ref[0]
    ns = ns_ref[0]

    def weight_copies(seq, slot):
        e = es_ref[seq]
        return (pltpu.make_async_copy(wg_hbm.at[e], wg_b.at[slot], sem.at[slot]),
                pltpu.make_async_copy(wu_hbm.at[e], wu_b.at[slot], sem.at[slot]),
                pltpu.make_async_copy(wd_hbm.at[e], wd_b.at[slot], sem.at[slot]))

    def start_weights(seq, slot):
        for c in weight_copies(seq, slot):
            c.start()

    @pl.when(i == 0)
    def _():
        y_acc[...] = jnp.zeros(y_acc.shape, F32)
        start_weights(0, 0)

        @pl.when(ns > 1)
        def _():
            start_weights(1, 1)

    s = sq_ref[i]
    first = (i == 0) | (s != sq_ref[jnp.maximum(i - 1, 0)])

    @pl.when((i < nu) & first)
    def _():
        slot = lax.rem(s, 2)
        for c in weight_copies(s, slot):
            c.wait()
        def cast_rows(dst, src, chunk):
            def body(c, carry):
                rs = pl.ds(pl.multiple_of(c * chunk, chunk), chunk)
                dst[rs, :] = src[slot, rs, :].astype(BF16)
                return carry
            lax.fori_loop(0, dst.shape[0] // chunk, body, 0)

        cast_rows(wg_s, wg_b, 256)
        cast_rows(wu_s, wu_b, 256)
        cast_rows(wd_s, wd_b, 64)

        @pl.when(s + 2 < ns)
        def _():
            start_weights(s + 2, slot)

    new_block = (i == 0) | (blk_ref[i] != blk_ref[jnp.maximum(i - 1, 0)])
    last_of_block = (i == nu - 1) | (blk_ref[jnp.minimum(i + 1, pl.num_programs(0) - 1)] != blk_ref[i])

    @pl.when((i < nu) & new_block)
    def _():
        x_s[...] = _load_packed(x_ref, MOE_ROWS).astype(BF16)

    @pl.when(i < nu)
    def _():
        x = x_s[...]
        g = jnp.dot(x, wg_s[...], preferred_element_type=F32)
        u = jnp.dot(x, wu_s[...], preferred_element_type=F32)
        mid = (jax.nn.silu(g) * u).astype(BF16)
        y = jnp.dot(mid, wd_s[...], preferred_element_type=F32)
        row = lax.broadcasted_iota(jnp.int32, (MOE_ROWS, 1), 0)
        mine = (row >= lo_ref[i]) & (row < hi_ref[i])
        y_acc[...] = jnp.where(mine, y, jnp.where(new_block, 0.0, y_acc[...]))

    @pl.when((i < nu) & last_of_block)
    def _():
        _store_packed(y_ref, y_acc[...], MOE_ROWS)


def _experts(n_pairs, seq_of_pair, expert_of_seq, n_seq, blk_of_pair, lo, hi, xs, w_gate, w_up, w_down):
    n_steps = seq_of_pair.shape[0]
    _, d, ff = w_gate.shape
    assert d == 2 * ROW_SUB * LANES, "a packed row must be exactly one (8,128) tile"
    blk = lambda i, nu, sq, es, ns, bk, lo, hi: (bk[jnp.minimum(i, nu[0] - 1)], 0)
    hbm = pl.BlockSpec(memory_space=pl.ANY)
    return pl.pallas_call(
        _expert_body,
        grid_spec=pltpu.PrefetchScalarGridSpec(
            num_scalar_prefetch=7,
            grid=(n_steps,),
            in_specs=[pl.BlockSpec((MOE_ROWS * ROW_SUB, LANES), blk), hbm, hbm, hbm],
            out_specs=pl.BlockSpec((MOE_ROWS * ROW_SUB, LANES), blk),
            scratch_shapes=[pltpu.VMEM((2, d, ff), F32), pltpu.VMEM((2, d, ff), F32), pltpu.VMEM((2, ff, d), F32),
                            pltpu.VMEM((d, ff), BF16), pltpu.VMEM((d, ff), BF16), pltpu.VMEM((ff, d), BF16),
                            pltpu.VMEM((MOE_ROWS, d), BF16), pltpu.VMEM((MOE_ROWS, d), F32),
                            pltpu.SemaphoreType.DMA((2,))],
        ),
        out_shape=jax.ShapeDtypeStruct(xs.shape, U32),
        compiler_params=_cparams(("arbitrary",)),
        name="moe_experts",
    )(n_pairs, seq_of_pair, expert_of_seq, n_seq, blk_of_pair, lo, hi, xs, w_gate, w_up, w_down)


def _combine_body(dest_ref, x_ref, info_ref, g_ref, ys_ref, o_ref, buf0, buf1, sem, *, tm):
    i = pl.program_id(0)
    slot = lax.rem(i, 2)
    bufs = (buf0, buf1)

    def gather(tile, sl, wait):
        def body(r, c):
            for k in range(EXPERT_TOPK):
                cp = _row_copy(ys_ref, dest_ref[EXPERT_TOPK * (tile * tm + r) + k], bufs[k].at[sl], r, sem.at[sl])
                if wait:
                    cp.wait()
                else:
                    cp.start(priority=k % 2)
            return c
        lax.fori_loop(0, tm, body, 0, unroll=4)

    @pl.when(i == 0)
    def _():
        gather(0, 0, False)

    @pl.when(i + 1 < pl.num_programs(0))
    def _():
        gather(i + 1, 1 - slot, False)

    gather(i, slot, True)
    info = info_ref[...]
    y = x_ref[...] + (info[:, 2:3] * _load_packed(buf0.at[slot], tm) + info[:, 3:4] * _load_packed(buf1.at[slot], tm))
    ms = jnp.mean(y * y, axis=-1, keepdims=True)
    o_ref[...] = y * lax.rsqrt(ms + RMS_EPS) * g_ref[...]


def _combine(dest, x1, info, g, ys, tm=256):
    t, d = x1.shape
    return pl.pallas_call(
        functools.partial(_combine_body, tm=tm),
        grid_spec=pltpu.PrefetchScalarGridSpec(
            num_scalar_prefetch=1,
            grid=(t // tm,),
            in_specs=[pl.BlockSpec((tm, d), lambda i, dest: (i, 0)),
                      pl.BlockSpec((tm, LANES), lambda i, dest: (i, 0)),
                      pl.BlockSpec((1, d), lambda i, dest: (0, 0)),
                      pl.BlockSpec(memory_space=pl.ANY)],
            out_specs=pl.BlockSpec((tm, d), lambda i, dest: (i, 0)),
            scratch_shapes=[pltpu.VMEM((2, tm * ROW_SUB, LANES), U32), pltpu.VMEM((2, tm * ROW_SUB, LANES), U32),
                            pltpu.SemaphoreType.DMA((2,))],
        ),
        out_shape=jax.ShapeDtypeStruct((t, d), F32),
        compiler_params=_cparams(("arbitrary",)),
        name="moe_combine",
    )(dest, x1, info, g.reshape(1, d), ys)


def _nsa(proj, slab, pe_k, w1_k, w2_k, pe_v, w1_v, w2_v, tab, b, s):
    g, dh = NSA_KV_HEADS, HEAD_DIM
    qw = NSA_HEADS * dh
    nc = s // CMP_STRIDE
    kvc = _compress(slab, pe_k, w1_k, w2_k, pe_v, w1_v, w2_v, g)

    c_start = np.arange(nc)[None, :] * CMP_STRIDE
    n_sel = s // SEL_BLOCK
    sb = np.arange(n_sel)[:, None] * SEL_BLOCK
    overlap = jnp.asarray((c_start < sb + SEL_BLOCK) & (c_start + CMP_BLOCK > sb), BF16)
    o_c, member = _nsa_cmp(proj, kvc, tab, overlap, b, s)

    t = ATT_TILE
    nq = s // t
    nn = _n_near(t)
    nh = ATT_HEADS_PER_STEP
    kblk = qw // (nh // NSA_GROUP * dh)
    per = g // (nh // NSA_GROUP)
    bias_d = _bias_vecs(tab, t, nn + 1)
    o_s = _flash(proj, 0, proj, kblk, proj, kblk + per, bias_d, NSA_HEADS, NSA_GROUP, nh,
                 _causal_pairs(nq, nn), True, member=member, et=_block_onehot(s, SEL_BLOCK), name="nsa_selected")
    n_back = -(-WINDOW // t)
    bias_w = _bias_vecs(tab, t, n_back + 1, window=WINDOW)
    o_w = _flash(proj, 0, proj, kblk + 2 * per, proj, kblk + 3 * per, bias_w, NSA_HEADS, NSA_GROUP, nh,
                 _window_pairs(nq, n_back), False, name="nsa_window")
    return o_c, o_s, o_w


def _moba(proj, tab, b, s):
    member = _moba_gate(proj, b, s)
    t = ATT_TILE
    nn = _n_near(t)
    nh = ATT_HEADS_PER_STEP
    ng = MOBA_HEADS // nh
    bias_d = _bias_vecs(tab, t, nn + 1)
    return _flash(proj, 0, proj, ng, proj, 2 * ng, bias_d, MOBA_HEADS, 1, nh, _causal_pairs(s // t, nn),
                  True, member=member, et=_block_onehot(s, MOBA_BLOCK), name="moba_attn")


def _moe(x1, g_ffn, w_group, b_group, w_router, b_router, w_gate, w_up, w_down, g_final):
    t, d = x1.shape
    ng, _, epg = w_router.shape
    w_gr = jnp.concatenate([w_group, jnp.transpose(w_router, (1, 0, 2)).reshape(d, ng * epg),
                            jnp.zeros((d, LANES - ng - ng * epg), F32)], axis=1)
    b_gr = jnp.concatenate([b_group, b_router.reshape(-1), jnp.zeros((LANES - ng - ng * epg,), F32)]).reshape(1, LANES)
    h, info, cnt = _route(x1, g_ffn, w_gr, b_gr)
    n_e = ng * epg
    n_assign = t * EXPERT_TOPK
    assert n_assign % MOE_ROWS == 0
    n_blocks = n_assign // MOE_ROWS
    counts = cnt[0, :n_e].astype(jnp.int32)
    end = jnp.cumsum(counts)
    start = end - counts
    expert = info[:, 0:EXPERT_TOPK].astype(jnp.int32)
    rank = info[:, 4:4 + EXPERT_TOPK].astype(jnp.int32)
    e_ids = jnp.arange(n_e, dtype=jnp.int32)
    dest = (jnp.sum(jnp.where(expert[..., None] == e_ids, start, 0), axis=-1) + rank).reshape(-1)
    owns = counts > 0
    seq_of_expert = jnp.cumsum(owns.astype(jnp.int32)) - 1
    n_seq = jnp.sum(owns.astype(jnp.int32)).reshape(1)
    expert_of_seq = jnp.sum(jnp.where(owns[None, :] & (seq_of_expert[None, :] == e_ids[:, None]), e_ids[None, :], 0),
                            axis=1)
    first = start // MOE_ROWS
    last = jnp.where(owns, (end - 1) // MOE_ROWS, first - 1)
    pair_end = jnp.cumsum(last - first + 1)
    pair_start = pair_end - (last - first + 1)
    n_steps = n_blocks + n_e
    p_ids = jnp.arange(n_steps, dtype=jnp.int32)
    e_of_pair = jnp.minimum(jnp.sum((pair_end[None, :] <= p_ids[:, None]).astype(jnp.int32), axis=1), n_e - 1)
    pick = e_of_pair[:, None] == e_ids[None, :]
    lookup = lambda v: jnp.sum(jnp.where(pick, v[None, :], 0), axis=1)
    blk_of_pair = jnp.clip(lookup(first) + p_ids - lookup(pair_start), 0, n_blocks - 1)
    lo = jnp.clip(lookup(start) - blk_of_pair * MOE_ROWS, 0, MOE_ROWS)
    hi = jnp.clip(lookup(end) - blk_of_pair * MOE_ROWS, 0, MOE_ROWS)
    seq_of_pair = lookup(seq_of_expert)
    n_pairs = pair_end[-1:].astype(jnp.int32)
    xs = _dispatch(dest, h)
    ys = _experts(n_pairs, seq_of_pair, expert_of_seq, n_seq, blk_of_pair, lo, hi, xs, w_gate, w_up, w_down)
    return _combine(dest, x1, info, g_final, ys)


def kernel(x, rel_bias, norm_mix, w_in, cmp_pe_k, cmp_w1_k, cmp_w2_k, cmp_pe_v, cmp_w1_v, cmp_w2_v, w_up_nsa,
           w_up_moba, w_out, norm_ffn, w_group, b_group, w_router, b_router, w_exp_gate, w_exp_up, w_exp_down,
           final_norm):
    b, s, d = x.shape
    t = b * s
    depth = w_in.shape[0]
    tab_a = rel_bias[:, :NSA_HEADS]
    tab_b = rel_bias[:, NSA_HEADS:]
    a_cols = NSA_HEADS * HEAD_DIM + 6 * NSA_KV_HEADS * HEAD_DIM
    gate_cols = 3 * NSA_HEADS
    b_cols = 3 * MOBA_HEADS * HEAD_DIM
    xt = x.reshape(t, d)
    out = None
    for l in range(depth):
        wt = jnp.swapaxes(w_in[l], 0, 1)
        b_col0 = a_cols + gate_cols
        q_cols = NSA_HEADS * HEAD_DIM
        cmp_cols = 2 * NSA_KV_HEADS * HEAD_DIM
        h, slab, gate_a = _proj_head(xt, norm_mix[l], wt, q_cols, cmp_cols, a_cols, CMP_STRIDE)
        slab = slab.reshape(b, s // CMP_STRIDE, CMP_STRIDE * cmp_cols)
        proj_a = _matmul(h, wt, 0, a_cols - cmp_cols, BF16, scaled_cols=q_cols, col_scale=Q_SCALE,
                         w_transposed=True, skip_at=q_cols, skip_cols=cmp_cols,
                         name="in_proj_a").reshape(b, s, a_cols - cmp_cols)
        proj_b = _matmul(h, wt, b_col0, b_cols, BF16, scaled_cols=MOBA_HEADS * HEAD_DIM, col_scale=Q_SCALE,
                         w_transposed=True, name="in_proj_b").reshape(b, s, b_cols)
        gm = _matmul(h, wt, b_col0 + b_cols, 2 * d, BF16, w_transposed=True, name="in_proj_gm")
        o_c, o_s, o_w = _nsa(proj_a, slab, cmp_pe_k[l], cmp_w1_k[l], cmp_w2_k[l],
                             cmp_pe_v[l], cmp_w1_v[l], cmp_w2_v[l], tab_a, b, s)
        o_b = _moba(proj_b, tab_b, b, s)
        merged = _merge(o_c.reshape(t, -1), o_s.reshape(t, -1), o_w.reshape(t, -1), gate_a,
                        o_b.reshape(t, -1), gm, w_up_nsa[l], w_up_moba[l])
        x1 = _matmul(merged, w_out[l], 0, d, F32, res=xt, name="out_proj")
        assert l == depth - 1, "only the last layer's MoE is fused with the final norm"
        out = _moe(x1, norm_ffn[l], w_group[l], b_group[l], w_router[l], b_router[l],
                   w_exp_gate[l], w_exp_up[l], w_exp_down[l], final_norm)
    return out.reshape(b, s, d)
```

```python
import functools
import math

import numpy as np
import jax
import jax.numpy as jnp
from jax import lax
from jax.experimental import pallas as pl
from jax.experimental.pallas import tpu as pltpu

F32 = jnp.float32
BF16 = jnp.bfloat16

HEAD_DIM = 128
NSA_HEADS = 8
NSA_KV_HEADS = 2
NSA_GROUP = NSA_HEADS // NSA_KV_HEADS
CMP_BLOCK = 32
CMP_STRIDE = 16
SEL_BLOCK = 64
SEL_TOPN = 16
WINDOW = 512
FORCED_SCORE = 1e4
MOBA_HEADS = 8
MOBA_BLOCK = 256
MOBA_TOPK = 3
REL_BUCKETS = 32
REL_MAX_DIST = 128
N_GROUPS = 8
EXPERTS_PER_GROUP = 8
N_EXPERTS = N_GROUPS * EXPERTS_PER_GROUP
EXPERT_TOPK = 2
RMS_EPS = 1e-6

LANES = 128
ATT_TILE = 512
ATT_HEADS_PER_STEP = 8
ATT_VMEM_LIMIT = 56 * 1024 * 1024
MOE_ROWS = 256
MASK_BIG = 1e30
M_INIT = -3e38
LOG2E = math.log2(math.e)
Q_SCALE = HEAD_DIM ** -0.5 * LOG2E
VMEM_LIMIT = 48 * 1024 * 1024


def _cparams(sem, vmem=VMEM_LIMIT):
    return pltpu.CompilerParams(dimension_semantics=sem, vmem_limit_bytes=vmem)


W_RING = 3


def _mm_body(*refs, has_res, n_scaled, col_scale, w_transposed, window):
    if has_res:
        a_ref, w_hbm, r_ref, o_ref, w_buf, sem = refs
    else:
        a_ref, w_hbm, o_ref, w_buf, sem = refs
    nj = pl.num_programs(1)
    n_steps = pl.num_programs(0) * nj
    s = pl.program_id(0) * nj + pl.program_id(1)

    def tile_copy(step):
        slot = lax.rem(step, W_RING)
        return pltpu.make_async_copy(window(w_hbm, lax.rem(step, nj)), w_buf.at[slot], sem.at[slot])

    @pl.when(s == 0)
    def _():
        tile_copy(s).start()

        @pl.when(n_steps > 1)
        def _():
            tile_copy(s + 1).start()

    @pl.when(s + 2 < n_steps)
    def _():
        tile_copy(s + 2).start()

    tile_copy(s).wait()
    w = w_buf[lax.rem(s, W_RING)].astype(BF16)
    if w_transposed:
        acc = lax.dot_general(a_ref[...], w, (((1,), (1,)), ((), ())), preferred_element_type=F32)
    else:
        acc = jnp.dot(a_ref[...], w, preferred_element_type=F32)
    if n_scaled:
        acc = acc * jnp.where(pl.program_id(1) < n_scaled, col_scale, 1.0)
    if has_res:
        acc = acc + r_ref[...]
    o_ref[...] = acc.astype(o_ref.dtype)


def _matmul(a, w, col0, ncols, out_dtype, res=None, scaled_cols=0, col_scale=1.0, w_transposed=False,
            skip_at=0, skip_cols=0, tm=2048, tn=512, name="matmul"):
    t, k = a.shape
    tn = min(tn, ncols)
    tm = min(tm, t)
    assert ncols % tn == 0 and t % tm == 0 and scaled_cols % tn == 0 and skip_at % tn == 0
    if w_transposed:
        assert col0 % 8 == 0 and skip_cols % 8 == 0
        gap_block = skip_at // tn if skip_cols else ncols // tn

        def window(w_hbm, j):
            row0 = col0 + j * tn + jnp.where(j >= gap_block, skip_cols, 0)
            return w_hbm.at[pl.ds(pl.multiple_of(row0, 8), tn)]
        tile = (tn, k)
    else:
        assert col0 % LANES == 0 and tn % LANES == 0

        def window(w_hbm, j):
            return w_hbm.at[:, pl.ds(pl.multiple_of(col0 + j * tn, LANES), tn)]
        tile = (k, tn)
    in_specs = [pl.BlockSpec((tm, k), lambda i, j: (i, 0)), pl.BlockSpec(memory_space=pl.ANY)]
    args = [a, w]
    if res is not None:
        in_specs.append(pl.BlockSpec((tm, tn), lambda i, j: (i, j)))
        args.append(res)
    return pl.pallas_call(
        functools.partial(_mm_body, has_res=res is not None, n_scaled=scaled_cols // tn, col_scale=col_scale,
                          w_transposed=w_transposed, window=window),
        grid=(t // tm, ncols // tn),
        in_specs=in_specs,
        out_specs=pl.BlockSpec((tm, tn), lambda i, j: (i, j)),
        out_shape=jax.ShapeDtypeStruct((t, ncols), out_dtype),
        scratch_shapes=[pltpu.VMEM((W_RING,) + tile, w.dtype), pltpu.SemaphoreType.DMA((W_RING,))],
        compiler_params=_cparams(("arbitrary", "arbitrary")),
        name=name,
    )(*args)


def _proj_head_body(x_ref, g_ref, ws_ref, wg_ref, h_ref, slab_ref, gate_ref, acc_ref, *, per):
    x = x_ref[...]
    ms = jnp.mean(x * x, axis=-1, keepdims=True)
    h = (x * lax.rsqrt(ms + RMS_EPS) * g_ref[...]).astype(BF16)
    h_ref[...] = h
    nt = (((1,), (1,)), ((), ()))
    gate_ref[...] = lax.dot_general(h, wg_ref[...].astype(BF16), nt, preferred_element_type=F32)
    acc = lax.dot_general(h, ws_ref[...].astype(BF16), nt, preferred_element_type=F32)
    n_chunks, tm, _ = acc_ref.shape
    tn = n_chunks * LANES
    for c in range(n_chunks):
        acc_ref[c] = acc[:, c * LANES:(c + 1) * LANES]
    for r in range(per):
        for c in range(n_chunks):
            slab_ref[:, r * tn + c * LANES:r * tn + (c + 1) * LANES] = (
                acc_ref[c, pl.ds(r, tm // per, stride=per), :].astype(slab_ref.dtype))


def _proj_head(x, g, wt, slab_col0, slab_cols, gate_col0, per, tm=1024):
    t, k = x.shape
    tm = min(tm, t)
    assert t % tm == 0 and tm % (per * 8) == 0 and slab_col0 % 8 == 0 and gate_col0 % 8 == 0
    return pl.pallas_call(
        functools.partial(_proj_head_body, per=per),
        grid=(t // tm,),
        in_specs=[pl.BlockSpec((tm, k), lambda i: (i, 0)),
                  pl.BlockSpec((1, k), lambda i: (0, 0)),
                  pl.BlockSpec((pl.Element(slab_cols), pl.Element(k)), lambda i: (slab_col0, 0)),
                  pl.BlockSpec((pl.Element(LANES), pl.Element(k)), lambda i: (gate_col0, 0))],
        out_specs=[pl.BlockSpec((tm, k), lambda i: (i, 0)),
                   pl.BlockSpec((tm // per, per * slab_cols), lambda i: (i, 0)),
                   pl.BlockSpec((tm, LANES), lambda i: (i, 0))],
        out_shape=[jax.ShapeDtypeStruct((t, k), BF16),
                   jax.ShapeDtypeStruct((t // per, per * slab_cols), BF16),
                   jax.ShapeDtypeStruct((t, LANES), F32)],
        scratch_shapes=[pltpu.VMEM((slab_cols // LANES, tm, LANES), F32)],
        compiler_params=_cparams(("parallel",)),
        name="in_proj_head",
    )(x, g.reshape(1, k), wt, wt)


def _compress_body(u_ref, pek_ref, w1k_ref, w2k_ref, pev_ref, w1v_ref, w2v_ref, o_ref, *, nc, g):
    dh = HEAD_DIM
    tok_w = 2 * g * dh
    for kv, (pe_ref, w1_ref, w2_ref) in enumerate(((pek_ref, w1k_ref, w2k_ref), (pev_ref, w1v_ref, w2v_ref))):
        w1 = w1_ref[...].astype(BF16)
        w2 = w2_ref[...].astype(BF16)
        half = CMP_STRIDE * dh
        peb = jnp.dot(pe_ref[...].astype(BF16), w1, preferred_element_type=F32)[0:1]
        for gi in range(g):
            a = b = None
            for r in range(CMP_STRIDE):
                c0 = r * tok_w + (kv * g + gi) * dh
                piece = u_ref[:, c0:c0 + dh]
                da = jnp.dot(piece, w1[r * dh:(r + 1) * dh], preferred_element_type=F32)
                db = jnp.dot(piece, w1[half + r * dh:half + (r + 1) * dh], preferred_element_type=F32)
                a = da if a is None else a + da
                b = db if b is None else b + db
            pre = a + pltpu.roll(b, nc - 1, 0) + peb
            hid = jax.nn.gelu(pre)
            o_ref[kv * g + gi] = jnp.dot(hid.astype(BF16), w2, preferred_element_type=F32).astype(o_ref.dtype)


def _compress(u, pe_k, w1_k, w2_k, pe_v, w1_v, w2_v, g):
    b, nc, kk = u.shape
    hid = w1_k.shape[1]
    dh = w2_k.shape[1]
    pe16 = lambda pe: jnp.broadcast_to(pe.reshape(1, -1), (16, pe.size))
    full = lambda shape: pl.BlockSpec(shape, lambda i: (0,) * len(shape))
    return pl.pallas_call(
        functools.partial(_compress_body, nc=nc, g=g),
        grid=(b,),
        in_specs=[pl.BlockSpec((None, nc, kk), lambda i: (i, 0, 0)),
                  full((16, CMP_BLOCK * dh)), full((CMP_BLOCK * dh, hid)), full((hid, dh)),
                  full((16, CMP_BLOCK * dh)), full((CMP_BLOCK * dh, hid)), full((hid, dh))],
        out_specs=pl.BlockSpec((None, 2 * g, nc, dh), lambda i: (i, 0, 0, 0)),
        out_shape=jax.ShapeDtypeStruct((b, 2 * g, nc, dh), BF16),
        compiler_params=_cparams(("parallel",)),
        name="nsa_compress",
    )(u, pe16(pe_k), w1_k, w2_k, pe16(pe_v), w1_v, w2_v)


def _split3(x):
    p1 = x.astype(BF16)
    r = x - p1.astype(F32)
    p2 = r.astype(BF16)
    p3 = (r - p2.astype(F32)).astype(BF16)
    return p1, p2, p3


def _rank_count(score, n_rows):
    groups = []
    for g0 in range(0, n_rows, 8):
        sg = score[g0:min(g0 + 8, n_rows), :]
        n_iota = g0 + lax.broadcasted_iota(jnp.int32, sg.shape, 0)
        cnt = jnp.zeros(sg.shape, F32)
        for m in range(n_rows):
            row = score[m:m + 1, :]
            if m < g0:
                beats = row >= sg
            elif m >= g0 + 8:
                beats = row > sg
            else:
                tie = jnp.where(n_iota > m, 1.0, 0.0)
                beats = jnp.where(row > sg, 1.0, jnp.where(row == sg, tie, 0.0)) > 0.5
            cnt = cnt + jnp.where(beats, 1.0, 0.0)
        groups.append(cnt)
    return jnp.concatenate(groups, axis=0) if len(groups) > 1 else groups[0]


def _nsa_cmp_body(q_ref, kc_ref, vc_ref, bias_ref, ov_ref, oc_ref, mem_ref, *, tq, nc, n_sel):
    t0 = pl.program_id(2) * tq
    kc = kc_ref[...]
    vc = vc_ref[...]
    t_idx = t0 + lax.broadcasted_iota(jnp.int32, (tq, nc), 0)
    c_idx = lax.broadcasted_iota(jnp.int32, (tq, nc), 1)
    dist = t_idx - (c_idx * CMP_STRIDE + (CMP_BLOCK - 1))
    n_k = REL_MAX_DIST // CMP_STRIDE
    kidx = jnp.where(dist < 0, n_k + 1, jnp.minimum(lax.shift_right_logical(dist, 4), n_k))
    assert CMP_STRIDE == 16
    psum = jnp.zeros((tq, nc), F32)
    for j in range(NSA_GROUP):
        hs = slice(j * HEAD_DIM, (j + 1) * HEAD_DIM)
        gt = bias_ref[j] * LOG2E
        bias = jnp.concatenate([jnp.take_along_axis(gt, kidx[:, c0:c0 + LANES], axis=1)
                                for c0 in range(0, nc, LANES)], axis=1)
        s = lax.dot_general(q_ref[:, hs], kc, (((1,), (1,)), ((), ())), preferred_element_type=F32) + bias
        m = jnp.max(s, axis=-1, keepdims=True)
        m = jnp.where(m > -0.5 * MASK_BIG, m, 0.0)
        p = jnp.exp2(s - m)
        d = jnp.sum(p, axis=-1, keepdims=True)
        p = p / jnp.where(d > 0, d, 1.0)
        oc_ref[:, hs] = jnp.dot(p.astype(BF16), vc, preferred_element_type=F32).astype(oc_ref.dtype)
        psum = psum + p
    ov = ov_ref[...]
    nt = (((1,), (1,)), ((), ()))
    p1, p2, p3 = _split3(psum)
    psel = (lax.dot_general(ov, p1, nt, preferred_element_type=F32)
            + lax.dot_general(ov, p2, nt, preferred_element_type=F32)
            + lax.dot_general(ov, p3, nt, preferred_element_type=F32))
    n_idx = lax.broadcasted_iota(jnp.int32, (n_sel, tq), 0)
    tt = t0 + lax.broadcasted_iota(jnp.int32, (n_sel, tq), 1)
    cur = tt // SEL_BLOCK
    forced = jnp.where(n_idx == 0, 1.0, jnp.where(n_idx == cur, 1.0, jnp.where(n_idx == cur - 1, 1.0, 0.0)))
    score = jnp.where(forced > 0.5, FORCED_SCORE, jnp.where(n_idx * SEL_BLOCK <= tt, psel, -1.0))
    cnt = _rank_count(score, n_sel)
    member = jnp.where(cnt < float(min(SEL_TOPN, n_sel)), 1.0, 0.0)
    if n_sel < LANES:
        member = jnp.concatenate([member, jnp.zeros((LANES - n_sel, tq), F32)], axis=0)
    mem_ref[...] = member.T.astype(mem_ref.dtype)


def _nsa_cmp(proj, kvc, tab, overlap, b, s, tq=256):
    g = NSA_KV_HEADS
    nc = kvc.shape[2]
    n_sel = s // SEL_BLOCK
    gw = NSA_GROUP * HEAD_DIM
    assert tq % CMP_STRIDE == 0 and REL_MAX_DIST % CMP_STRIDE == 0
    n_k = REL_MAX_DIST // CMP_STRIDE
    rho = (np.arange(tq)[:, None] - (CMP_BLOCK - 1)) % CMP_STRIDE
    dd = np.concatenate([rho + CMP_STRIDE * np.arange(n_k)[None, :], np.full((tq, 1), REL_MAX_DIST)], axis=1)
    gtab = jnp.transpose(_bias_of_dist(tab, dd), (2, 0, 1)).astype(F32)
    bias_c = jnp.concatenate([gtab, jnp.full(gtab.shape[:2] + (1,), -MASK_BIG, F32),
                              jnp.zeros(gtab.shape[:2] + (LANES - n_k - 2,), F32)], axis=2)
    body = functools.partial(_nsa_cmp_body, tq=tq, nc=nc, n_sel=n_sel)
    return pl.pallas_call(
        body,
        grid=(b, g, s // tq),
        in_specs=[pl.BlockSpec((None, tq, gw), lambda bi, gi, i: (bi, i, gi)),
                  pl.BlockSpec((None, None, nc, HEAD_DIM), lambda bi, gi, i: (bi, gi, 0, 0)),
                  pl.BlockSpec((None, None, nc, HEAD_DIM), lambda bi, gi, i: (bi, g + gi, 0, 0)),
                  pl.BlockSpec((NSA_GROUP, tq, LANES), lambda bi, gi, i: (gi, 0, 0)),
                  pl.BlockSpec((n_sel, nc), lambda bi, gi, i: (0, 0))],
        out_specs=[pl.BlockSpec((None, tq, gw), lambda bi, gi, i: (bi, i, gi)),
                   pl.BlockSpec((None, None, tq, LANES), lambda bi, gi, i: (bi, gi, i, 0))],
        out_shape=[jax.ShapeDtypeStruct((b, s, NSA_HEADS * HEAD_DIM), BF16),
                   jax.ShapeDtypeStruct((b, g, s, LANES), BF16)],
        compiler_params=_cparams(("parallel", "parallel", "parallel")),
        name="nsa_cmp_select",
    )(proj, kvc, kvc, bias_c, overlap)


def _moba_gate_body(q_ref, k_ref, mem_ref, *, s, nblk):
    k = k_ref[...].astype(F32)
    kmean = jnp.mean(k.reshape(nblk, MOBA_BLOCK, HEAD_DIM), axis=1)
    k1 = kmean.astype(BF16)
    k2 = (kmean - k1.astype(F32)).astype(BF16)
    q = q_ref[...]
    nt = (((1,), (1,)), ((), ()))
    gate = (lax.dot_general(k1, q, nt, preferred_element_type=F32)
            + lax.dot_general(k2, q, nt, preferred_element_type=F32))
    n_idx = lax.broadcasted_iota(jnp.int32, (nblk, s), 0)
    own = lax.broadcasted_iota(jnp.int32, (nblk, s), 1) // MOBA_BLOCK
    past = n_idx < own
    score = jnp.where(past, gate, -MASK_BIG)
    cnt = _rank_count(score, nblk)
    n_top = max(1, min(MOBA_TOPK, nblk - 1))
    sel = jnp.where(past, jnp.where(cnt < float(n_top), 1.0, 0.0), 0.0)
    member = jnp.where(n_idx == own, 1.0, sel)
    member = jnp.concatenate([member, jnp.zeros((LANES - nblk, s), F32)], axis=0)
    mem_ref[...] = member.T.astype(mem_ref.dtype)


def _moba_gate(proj, b, s):
    h = MOBA_HEADS
    nblk = s // MOBA_BLOCK
    return pl.pallas_call(
        functools.partial(_moba_gate_body, s=s, nblk=nblk),
        grid=(b, h),
        in_specs=[pl.BlockSpec((None, s, HEAD_DIM), lambda bi, hi: (bi, 0, hi)),
                  pl.BlockSpec((None, s, HEAD_DIM), lambda bi, hi: (bi, 0, h + hi))],
        out_specs=pl.BlockSpec((None, None, s, LANES), lambda bi, hi: (bi, hi, 0, 0)),
        out_shape=jax.ShapeDtypeStruct((b, h, s, LANES), BF16),
        compiler_params=_cparams(("parallel", "parallel")),
        name="moba_gate",
    )(proj, proj)


def _flash_body(qi_ref, ki_ref, bo_ref, fl_ref, *refs, nh, ratio, nm, n_near, has_far):
    if nm:
        (q_ref, k_ref, v_ref, bvec_ref, mem_ref, et_ref, o_ref,
         m_ref, l_ref, acc_ref, sh_ref, al_ref, bias_ref, s_ref, p_ref) = refs
    else:
        q_ref, k_ref, v_ref, bvec_ref, o_ref, m_ref, l_ref, acc_ref, sh_ref, al_ref, bias_ref, s_ref, p_ref = refs
    del qi_ref, ki_ref
    p = pl.program_id(2)
    flag = fl_ref[p]
    bo = bo_ref[p]
    t = q_ref.shape[0]
    rows = 64

    @pl.when(p == 0)
    def _():
        for h in range(nh):
            for o in range(n_near):
                vec = bvec_ref[h, o][0:1, :] * LOG2E
                for rc in range(t // rows):
                    x = pltpu.roll(jnp.broadcast_to(vec, (rows, 2 * t)), rc * rows, 1, stride=1, stride_axis=0)
                    bias_ref[h, o, rc * rows:(rc + 1) * rows, :] = x[:, :t]

    @pl.when((flag & 1) != 0)
    def _():
        m_ref[...] = jnp.full(m_ref.shape, M_INIT, F32)
        l_ref[...] = jnp.zeros(l_ref.shape, F32)
        acc_ref[...] = jnp.zeros(acc_ref.shape, F32)

    nt = (((1,), (1,)), ((), ()))
    reps = t // LANES

    def step(near):
        def pass1(h):
            hs = slice(h * HEAD_DIM, (h + 1) * HEAD_DIM)
            kv = h // ratio
            ks = slice(kv * HEAD_DIM, (kv + 1) * HEAD_DIM)
            q = q_ref[:, hs]
            k = k_ref[:, ks]
            if nm:
                mneg = mem_ref[h // (nh // nm)] - 1.0
                q = jnp.concatenate([q, mneg.astype(BF16)], axis=1)
                k = jnp.concatenate([k, et_ref[...]], axis=1)
            sc = lax.dot_general(q, k, nt, preferred_element_type=F32)
            m_prev = m_ref[h]
            if near:
                sc = sc + bias_ref[h, bo]
                m_new = jnp.maximum(m_prev, jnp.max(sc, axis=-1, keepdims=True))
                sh_ref[h] = m_new
            else:
                cfar = bvec_ref[h, n_near][0:1, 0:LANES] * LOG2E
                m_new = jnp.maximum(m_prev, jnp.max(sc, axis=-1, keepdims=True) + cfar)
                sh_ref[h] = m_new - cfar
            s_ref[h] = sc
            al_ref[h] = jnp.exp2(m_prev - m_new)
            m_ref[h] = m_new

        def pass2(h):
            ks = slice(h // ratio * HEAD_DIM, (h // ratio + 1) * HEAD_DIM)
            for rc in range(t // rows):
                rs = slice(rc * rows, (rc + 1) * rows)
                pm = jnp.exp2(s_ref[h, rs, :] - jnp.tile(sh_ref[h, rs, :], (1, reps)))
                l_ref[h, rs, :] = al_ref[h, rs, :] * l_ref[h, rs, :] + jnp.sum(pm, axis=-1, keepdims=True)
                p_ref[h, rs, :] = pm.astype(BF16)
            acc_ref[h] = al_ref[h] * acc_ref[h] + jnp.dot(p_ref[h], v_ref[:, ks], preferred_element_type=F32)

        for h in range(nh):
            pass1(h)
        for h in range(nh):
            pass2(h)

    if has_far:
        pl.when(bo < n_near)(lambda: step(True))
        pl.when(bo >= n_near)(lambda: step(False))
    else:
        step(True)

    @pl.when((flag & 2) != 0)
    def _():
        for h in range(nh):
            l = l_ref[h]
            o_ref[:, h * HEAD_DIM:(h + 1) * HEAD_DIM] = (acc_ref[h] / jnp.where(l > 0, l, 1.0)).astype(o_ref.dtype)


def _flash(q_arr, q_off, k_arr, k_off, v_arr, v_off, bias, n_heads, ratio, nh, pairs, has_far, member=None,
           et=None, name="flash"):
    b, s, _ = q_arr.shape
    t = ATT_TILE
    nkv = nh // ratio
    ng = n_heads // nh
    qi = jnp.asarray([p[0] for p in pairs], jnp.int32)
    ki = jnp.asarray([p[1] for p in pairs], jnp.int32)
    bo = jnp.asarray([p[2] for p in pairs], jnp.int32)
    fl = jnp.asarray([p[3] for p in pairs], jnp.int32)
    nb = bias.shape[1]
    nm = 0
    in_specs = [
        pl.BlockSpec((None, t, nh * HEAD_DIM), lambda bi, gi, p, qi, ki, bo, fl: (bi, qi[p], q_off + gi)),
        pl.BlockSpec((None, t, nkv * HEAD_DIM), lambda bi, gi, p, qi, ki, bo, fl: (bi, ki[p], k_off + gi)),
        pl.BlockSpec((None, t, nkv * HEAD_DIM), lambda bi, gi, p, qi, ki, bo, fl: (bi, ki[p], v_off + gi)),
        pl.BlockSpec((nh, nb, 8, 2 * t), lambda bi, gi, p, qi, ki, bo, fl: (gi, 0, 0, 0)),
    ]
    args = [q_arr, k_arr, v_arr, bias]
    if member is not None:
        nm = member.shape[1] // ng
        in_specs += [
            pl.BlockSpec((None, nm, t, LANES), lambda bi, gi, p, qi, ki, bo, fl: (bi, gi, qi[p], 0)),
            pl.BlockSpec((t, LANES), lambda bi, gi, p, qi, ki, bo, fl: (ki[p], 0)),
        ]
        args += [member, et]
    n_near = nb - 1 if has_far else nb
    body = functools.partial(_flash_body, nh=nh, ratio=ratio, nm=nm, n_near=n_near, has_far=has_far)
    return pl.pallas_call(
        body,
        grid_spec=pltpu.PrefetchScalarGridSpec(
            num_scalar_prefetch=4,
            grid=(b, ng, len(pairs)),
            in_specs=in_specs,
            out_specs=pl.BlockSpec((None, t, nh * HEAD_DIM), lambda bi, gi, p, qi, ki, bo, fl: (bi, qi[p], gi)),
            scratch_shapes=[pltpu.VMEM((nh, t, LANES), F32)] * 5 + [pltpu.VMEM((nh, n_near, t, t), F32),
                                                                    pltpu.VMEM((nh, t, t), F32),
                                                                    pltpu.VMEM((nh, t, t), BF16)],
        ),
        out_shape=jax.ShapeDtypeStruct((b, s, n_heads * HEAD_DIM), BF16),
        compiler_params=_cparams(("parallel", "parallel", "arbitrary"), vmem=ATT_VMEM_LIMIT),
        name=name,
    )(qi, ki, bo, fl, *args)


def _rel_bucket(dist):
    n = jnp.maximum(jnp.asarray(dist, jnp.int32), 0)
    max_exact = REL_BUCKETS // 2
    nf = jnp.maximum(n, 1).astype(jnp.float32)
    large = max_exact + (jnp.log(nf / max_exact) / math.log(REL_MAX_DIST / max_exact)
                         * (REL_BUCKETS - max_exact)).astype(jnp.int32)
    return jnp.where(n < max_exact, n, jnp.minimum(large, REL_BUCKETS - 1))


def _bias_of_dist(tab, dist):
    hit = _rel_bucket(dist)[..., None, None] == jnp.arange(REL_BUCKETS)[:, None]
    return jnp.sum(jnp.where(hit, tab, 0.0), axis=-2)


def _n_near(t):
    return -(-(REL_MAX_DIST - 1 + t) // t)


def _bias_vecs(tab, t, n_off, window=None):
    k = np.arange(2 * t)[None, :]
    dist = np.arange(n_off)[:, None] * t + np.where(k < t, -k, 2 * t - k)
    ok = dist >= 0
    if window is not None:
        ok &= dist < window
    bias = jnp.where(jnp.asarray(ok)[..., None], _bias_of_dist(tab, dist), -MASK_BIG)
    bias = jnp.transpose(bias, (2, 0, 1)).astype(F32)
    return jnp.broadcast_to(bias[:, :, None, :], (bias.shape[0], n_off, 8, 2 * t))


def _causal_pairs(nq, n_near):
    pairs = []
    for qi in range(nq):
        for ki in range(qi + 1):
            pairs.append((qi, ki, min(qi - ki, n_near), (1 if ki == 0 else 0) | (2 if ki == qi else 0)))
    return pairs


def _window_pairs(nq, n_back):
    pairs = []
    for qi in range(nq):
        lo = max(0, qi - n_back)
        for ki in range(lo, qi + 1):
            pairs.append((qi, ki, qi - ki, (1 if ki == lo else 0) | (2 if ki == qi else 0)))
    return pairs


def _block_onehot(s, blk):
    return jnp.asarray(np.where(np.arange(s)[:, None] // blk == np.arange(LANES)[None, :], MASK_BIG, 0.0), BF16)


def _merge_body(oc_ref, os_ref, ow_ref, gl_ref, ob_ref, gma_ref, gmb_ref, wa_ref, wb_ref, o_ref, oa_ref):
    @pl.when(pl.program_id(1) == 0)
    def _():
        gates = jax.nn.sigmoid(gl_ref[...])
        for h in range(NSA_HEADS):
            hs = slice(h * HEAD_DIM, (h + 1) * HEAD_DIM)
            mix = (gates[:, 3 * h:3 * h + 1] * oc_ref[:, hs].astype(F32)
                   + gates[:, 3 * h + 1:3 * h + 2] * os_ref[:, hs].astype(F32)
                   + gates[:, 3 * h + 2:3 * h + 3] * ow_ref[:, hs].astype(F32))
            oa_ref[:, hs] = mix.astype(BF16)

    ya = jnp.dot(oa_ref[...], wa_ref[...].astype(BF16), preferred_element_type=F32)
    yb = jnp.dot(ob_ref[...], wb_ref[...].astype(BF16), preferred_element_type=F32)
    ga = jax.nn.sigmoid(gma_ref[...].astype(F32))
    gb = jax.nn.sigmoid(gmb_ref[...].astype(F32))
    o_ref[...] = (ga * ya + gb * yb).astype(o_ref.dtype)


def _merge(o_c, o_s, o_w, gate_logits, o_b, gm, w_up_a, w_up_b, tm=1024, tn=512):
    t, ka = o_c.shape
    kb = o_b.shape[1]
    d = w_up_a.shape[1]
    tm = min(tm, t)
    nj = d // tn
    row = lambda i, j: (i, 0)
    return pl.pallas_call(
        _merge_body,
        grid=(t // tm, nj),
        in_specs=[pl.BlockSpec((tm, ka), row), pl.BlockSpec((tm, ka), row), pl.BlockSpec((tm, ka), row),
                  pl.BlockSpec((tm, LANES), row), pl.BlockSpec((tm, kb), row),
                  pl.BlockSpec((tm, tn), lambda i, j: (i, j)),
                  pl.BlockSpec((tm, tn), lambda i, j: (i, j + nj)),
                  pl.BlockSpec((ka, tn), lambda i, j: (0, j)),
                  pl.BlockSpec((kb, tn), lambda i, j: (0, j))],
        out_specs=pl.BlockSpec((tm, tn), lambda i, j: (i, j)),
        out_shape=jax.ShapeDtypeStruct((t, d), BF16),
        scratch_shapes=[pltpu.VMEM((tm, ka), BF16)],
        compiler_params=_cparams(("parallel", "arbitrary")),
        name="merge_up",
    )(o_c, o_s, o_w, gate_logits, o_b, gm, gm, w_up_a, w_up_b)


def _route_body(x_ref, g_ref, w_ref, b_ref, h_ref, info_ref, cnt_ref, carry_ref, *, tm):
    @pl.when(pl.program_id(0) == 0)
    def _():
        carry_ref[...] = jnp.zeros(carry_ref.shape, F32)

    x = x_ref[...]
    ms = jnp.mean(x * x, axis=-1, keepdims=True)
    h = x * lax.rsqrt(ms + RMS_EPS) * g_ref[...]
    _store_packed(h_ref, h, tm)
    w = w_ref[...]
    h1 = h.astype(BF16)
    h2 = (h - h1.astype(F32)).astype(BF16)
    w1 = w.astype(BF16)
    w2 = (w - w1.astype(F32)).astype(BF16)
    logits = (jnp.dot(h1, w1, preferred_element_type=F32) + jnp.dot(h1, w2, preferred_element_type=F32)
              + jnp.dot(h2, w1, preferred_element_type=F32)) + b_ref[...]
    lane = lax.broadcasted_iota(jnp.int32, (tm, LANES), 1)
    lanef = lane.astype(F32)

    is_g = lane < N_GROUPS
    gl = jnp.where(is_g, logits, -MASK_BIG)
    ge = jnp.where(is_g, jnp.exp(gl - jnp.max(gl, axis=-1, keepdims=True)), 0.0)
    gp = ge / jnp.sum(ge, axis=-1, keepdims=True)
    g_val = jnp.max(gp, axis=-1, keepdims=True)
    g_idx = jnp.min(jnp.where(gp == g_val, lanef, float(LANES)), axis=-1, keepdims=True)

    lane_grp = ((lane - N_GROUPS) // EXPERTS_PER_GROUP).astype(F32)
    in_e = jnp.where(lane >= N_GROUPS, jnp.where(lane < N_GROUPS + N_EXPERTS, 1.0, 0.0), 0.0)
    is_e = jnp.where(lane_grp == g_idx, in_e, 0.0) > 0.5
    el = jnp.where(is_e, logits, -MASK_BIG)
    ee = jnp.where(is_e, jnp.exp(el - jnp.max(el, axis=-1, keepdims=True)), 0.0)
    ep = jnp.where(is_e, ee / jnp.sum(ee, axis=-1, keepdims=True), -1.0)
    v1 = jnp.max(ep, axis=-1, keepdims=True)
    l1 = jnp.min(jnp.where(ep == v1, lanef, float(LANES)), axis=-1, keepdims=True)
    ep2 = jnp.where(lanef == l1, -1.0, ep)
    v2 = jnp.max(ep2, axis=-1, keepdims=True)
    l2 = jnp.min(jnp.where(ep2 == v2, lanef, float(LANES)), axis=-1, keepdims=True)
    vs = v1 + v2
    wt1 = g_val * v1 / vs
    wt2 = g_val * v2 / vs
    e1 = l1 - float(N_GROUPS)
    e2 = l2 - float(N_GROUPS)

    oh = jnp.where(lanef == e1, 1.0, jnp.where(lanef == e2, 1.0, 0.0))
    r_i = lax.broadcasted_iota(jnp.int32, (tm, tm), 0)
    c_i = lax.broadcasted_iota(jnp.int32, (tm, tm), 1)
    tri = jnp.where(r_i > c_i, 1.0, 0.0).astype(BF16)
    base = jnp.dot(tri, oh.astype(BF16), preferred_element_type=F32) + carry_ref[...]
    r1 = jnp.sum(jnp.where(lanef == e1, base, 0.0), axis=-1, keepdims=True)
    r2 = jnp.sum(jnp.where(lanef == e2, base, 0.0), axis=-1, keepdims=True)
    carry_ref[...] = carry_ref[...] + jnp.sum(oh, axis=0, keepdims=True)
    cnt_ref[...] = jnp.broadcast_to(carry_ref[...], cnt_ref.shape)
    info = jnp.where(lane == 0, e1, jnp.where(lane == 1, e2, jnp.where(lane == 2, wt1, jnp.where(
        lane == 3, wt2, jnp.where(lane == 4, r1, jnp.where(lane == 5, r2, 0.0))))))
    info_ref[...] = info


def _route(x1, g, w_gr, b_gr, tm=512):
    t, d = x1.shape
    return pl.pallas_call(
        functools.partial(_route_body, tm=tm),
        grid=(t // tm,),
        in_specs=[pl.BlockSpec((tm, d), lambda i: (i, 0)),
                  pl.BlockSpec((1, d), lambda i: (0, 0)),
                  pl.BlockSpec((d, LANES), lambda i: (0, 0)),
                  pl.BlockSpec((1, LANES), lambda i: (0, 0))],
        out_specs=[pl.BlockSpec((tm * ROW_SUB, LANES), lambda i: (i, 0)),
                   pl.BlockSpec((tm, LANES), lambda i: (i, 0)),
                   pl.BlockSpec((8, LANES), lambda i: (0, 0))],
        out_shape=[jax.ShapeDtypeStruct((t * ROW_SUB, LANES), jnp.uint32),
                   jax.ShapeDtypeStruct((t, LANES), F32),
                   jax.ShapeDtypeStruct((8, LANES), F32)],
        scratch_shapes=[pltpu.VMEM((1, LANES), F32)],
        compiler_params=_cparams(("arbitrary",)),
        name="moe_route",
    )(x1, g.reshape(1, d), w_gr, b_gr)


ROW_SUB = 8
U32 = jnp.uint32


def _pack_pairs(lo, hi):
    lo_b = lax.bitcast_convert_type(lo.astype(BF16).astype(F32), U32)
    hi_b = lax.bitcast_convert_type(hi.astype(BF16).astype(F32), U32)
    return lax.shift_right_logical(lo_b, U32(16)) | (hi_b & U32(0xFFFF0000))


def _unpack_pairs(w):
    lo = lax.bitcast_convert_type(lax.shift_left(w, U32(16)), F32)
    hi = lax.bitcast_convert_type(w & U32(0xFFFF0000), F32)
    return lo, hi


def _store_packed(ref, y, n):
    half = y.shape[1] // 2
    for s in range(ROW_SUB):
        cs = slice(s * LANES, (s + 1) * LANES)
        ref[pl.ds(s, n, stride=ROW_SUB), :] = _pack_pairs(y[:, cs], y[:, half + s * LANES:half + (s + 1) * LANES])


def _load_packed(ref, n):
    los, his = [], []
    for s in range(ROW_SUB):
        lo, hi = _unpack_pairs(ref[pl.ds(s, n, stride=ROW_SUB), :])
        los.append(lo)
        his.append(hi)
    return jnp.concatenate(los + his, axis=1)


def _row_copy(src_ref, src_row, dst_ref, dst_row, sem):
    return pltpu.make_async_copy(src_ref.at[pl.ds(pl.multiple_of(src_row * ROW_SUB, ROW_SUB), ROW_SUB)],
                                 dst_ref.at[pl.ds(pl.multiple_of(dst_row * ROW_SUB, ROW_SUB), ROW_SUB)], sem)


def _dispatch_body(dest_ref, h_ref, xs_ref, sem, *, tm):
    base = pl.program_id(0) * tm

    def issue(r, c):
        for k in range(EXPERT_TOPK):
            _row_copy(h_ref, r, xs_ref, dest_ref[EXPERT_TOPK * (base + r) + k], sem).start(priority=k % 2)
        return c

    lax.fori_loop(0, tm, issue, 0, unroll=4)

    def drain(r, c):
        for k in range(EXPERT_TOPK):
            _row_copy(h_ref, r, xs_ref, dest_ref[EXPERT_TOPK * (base + r) + k], sem).wait()
        return c

    lax.fori_loop(0, tm, drain, 0, unroll=4)


def _dispatch(dest, hp, tm=512):
    t = hp.shape[0] // ROW_SUB
    n_rows = dest.shape[0]
    return pl.pallas_call(
        functools.partial(_dispatch_body, tm=tm),
        grid_spec=pltpu.PrefetchScalarGridSpec(
            num_scalar_prefetch=1,
            grid=(t // tm,),
            in_specs=[pl.BlockSpec((tm * ROW_SUB, LANES), lambda i, dest: (i, 0))],
            out_specs=pl.BlockSpec(memory_space=pl.ANY),
            scratch_shapes=[pltpu.SemaphoreType.DMA(())],
        ),
        out_shape=jax.ShapeDtypeStruct((n_rows * ROW_SUB, LANES), U32),
        compiler_params=_cparams(("arbitrary",)),
        name="moe_dispatch",
    )(dest, hp)


def _expert_body(nu_ref, sq_ref, es_ref, ns_ref, blk_ref, lo_ref, hi_ref, x_ref, wg_hbm, wu_hbm, wd_hbm, y_ref,
                 wg_b, wu_b, wd_b, wg_s, wu_s, wd_s, y_acc, sem):
    i = pl.program_id(0)
    nu = nu_ref[0]
    ns = ns_ref[0]

    def weight_copies(seq, slot):
        e = es_ref[seq]
        return (pltpu.make_async_copy(wg_hbm.at[e], wg_b.at[slot], sem.at[slot]),
                pltpu.make_async_copy(wu_hbm.at[e], wu_b.at[slot], sem.at[slot]),
                pltpu.make_async_copy(wd_hbm.at[e], wd_b.at[slot], sem.at[slot]))

    def start_weights(seq, slot):
        for c in weight_copies(seq, slot):
            c.start()

    @pl.when(i == 0)
    def _():
        y_acc[...] = jnp.zeros(y_acc.shape, F32)
        start_weights(0, 0)

        @pl.when(ns > 1)
        def _():
            start_weights(1, 1)

    s = sq_ref[i]
    first = (i == 0) | (s != sq_ref[jnp.maximum(i - 1, 0)])

    @pl.when((i < nu) & first)
    def _():
        slot = lax.rem(s, 2)
        for c in weight_copies(s, slot):
            c.wait()
        def cast_rows(dst, src, chunk):
            def body(c, carry):
                rs = pl.ds(pl.multiple_of(c * chunk, chunk), chunk)
                dst[rs, :] = src[slot, rs, :].astype(BF16)
                return carry
            lax.fori_loop(0, dst.shape[0] // chunk, body, 0)

        cast_rows(wg_s, wg_b, 256)
        cast_rows(wu_s, wu_b, 256)
        cast_rows(wd_s, wd_b, 64)

        @pl.when(s + 2 < ns)
        def _():
            start_weights(s + 2, slot)

    @pl.when(i < nu)
    def _():
        x = _load_packed(x_ref, MOE_ROWS).astype(BF16)
        g = jnp.dot(x, wg_s[...], preferred_element_type=F32)
        u = jnp.dot(x, wu_s[...], preferred_element_type=F32)
        mid = (jax.nn.silu(g) * u).astype(BF16)
        y = jnp.dot(mid, wd_s[...], preferred_element_type=F32)
        row = lax.broadcasted_iota(jnp.int32, (MOE_ROWS, 1), 0)
        mine = (row >= lo_ref[i]) & (row < hi_ref[i])
        new_block = (i == 0) | (blk_ref[i] != blk_ref[jnp.maximum(i - 1, 0)])
        y_acc[...] = jnp.where(mine, y, jnp.where(new_block, 0.0, y_acc[...]))
        _store_packed(y_ref, y_acc[...], MOE_ROWS)


def _experts(n_pairs, seq_of_pair, expert_of_seq, n_seq, blk_of_pair, lo, hi, xs, w_gate, w_up, w_down):
    n_steps = seq_of_pair.shape[0]
    _, d, ff = w_gate.shape
    assert d == 2 * ROW_SUB * LANES, "a packed row must be exactly one (8,128) tile"
    blk = lambda i, nu, sq, es, ns, bk, lo, hi: (bk[jnp.minimum(i, nu[0] - 1)], 0)
    hbm = pl.BlockSpec(memory_space=pl.ANY)
    return pl.pallas_call(
        _expert_body,
        grid_spec=pltpu.PrefetchScalarGridSpec(
            num_scalar_prefetch=7,
            grid=(n_steps,),
            in_specs=[pl.BlockSpec((MOE_ROWS * ROW_SUB, LANES), blk), hbm, hbm, hbm],
            out_specs=pl.BlockSpec((MOE_ROWS * ROW_SUB, LANES), blk),
            scratch_shapes=[pltpu.VMEM((2, d, ff), F32), pltpu.VMEM((2, d, ff), F32), pltpu.VMEM((2, ff, d), F32),
                            pltpu.VMEM((d, ff), BF16), pltpu.VMEM((d, ff), BF16), pltpu.VMEM((ff, d), BF16),
                            pltpu.VMEM((MOE_ROWS, d), F32), pltpu.SemaphoreType.DMA((2,))],
        ),
        out_shape=jax.ShapeDtypeStruct(xs.shape, U32),
        compiler_params=_cparams(("arbitrary",)),
        name="moe_experts",
    )(n_pairs, seq_of_pair, expert_of_seq, n_seq, blk_of_pair, lo, hi, xs, w_gate, w_up, w_down)


def _combine_body(dest_ref, x_ref, info_ref, g_ref, ys_ref, o_ref, buf0, buf1, sem, *, tm):
    i = pl.program_id(0)
    slot = lax.rem(i, 2)
    bufs = (buf0, buf1)

    def gather(tile, sl, wait):
        def body(r, c):
            for k in range(EXPERT_TOPK):
                cp = _row_copy(ys_ref, dest_ref[EXPERT_TOPK * (tile * tm + r) + k], bufs[k].at[sl], r, sem.at[sl])
                if wait:
                    cp.wait()
                else:
                    cp.start(priority=k % 2)
            return c
        lax.fori_loop(0, tm, body, 0, unroll=4)

    @pl.when(i == 0)
    def _():
        gather(0, 0, False)

    @pl.when(i + 1 < pl.num_programs(0))
    def _():
        gather(i + 1, 1 - slot, False)

    gather(i, slot, True)
    info = info_ref[...]
    y = x_ref[...] + (info[:, 2:3] * _load_packed(buf0.at[slot], tm) + info[:, 3:4] * _load_packed(buf1.at[slot], tm))
    ms = jnp.mean(y * y, axis=-1, keepdims=True)
    o_ref[...] = y * lax.rsqrt(ms + RMS_EPS) * g_ref[...]


def _combine(dest, x1, info, g, ys, tm=256):
    t, d = x1.shape
    return pl.pallas_call(
        functools.partial(_combine_body, tm=tm),
        grid_spec=pltpu.PrefetchScalarGridSpec(
            num_scalar_prefetch=1,
            grid=(t // tm,),
            in_specs=[pl.BlockSpec((tm, d), lambda i, dest: (i, 0)),
                      pl.BlockSpec((tm, LANES), lambda i, dest: (i, 0)),
                      pl.BlockSpec((1, d), lambda i, dest: (0, 0)),
                      pl.BlockSpec(memory_space=pl.ANY)],
            out_specs=pl.BlockSpec((tm, d), lambda i, dest: (i, 0)),
            scratch_shapes=[pltpu.VMEM((2, tm * ROW_SUB, LANES), U32), pltpu.VMEM((2, tm * ROW_SUB, LANES), U32),
                            pltpu.SemaphoreType.DMA((2,))],
        ),
        out_shape=jax.ShapeDtypeStruct((t, d), F32),
        compiler_params=_cparams(("arbitrary",)),
        name="moe_combine",
    )(dest, x1, info, g.reshape(1, d), ys)


def _nsa(proj, slab, pe_k, w1_k, w2_k, pe_v, w1_v, w2_v, tab, b, s):
    g, dh = NSA_KV_HEADS, HEAD_DIM
    qw = NSA_HEADS * dh
    nc = s // CMP_STRIDE
    kvc = _compress(slab, pe_k, w1_k, w2_k, pe_v, w1_v, w2_v, g)

    c_start = np.arange(nc)[None, :] * CMP_STRIDE
    n_sel = s // SEL_BLOCK
    sb = np.arange(n_sel)[:, None] * SEL_BLOCK
    overlap = jnp.asarray((c_start < sb + SEL_BLOCK) & (c_start + CMP_BLOCK > sb), BF16)
    o_c, member = _nsa_cmp(proj, kvc, tab, overlap, b, s)

    t = ATT_TILE
    nq = s // t
    nn = _n_near(t)
    nh = ATT_HEADS_PER_STEP
    kblk = qw // (nh // NSA_GROUP * dh)
    per = g // (nh // NSA_GROUP)
    bias_d = _bias_vecs(tab, t, nn + 1)
    o_s = _flash(proj, 0, proj, kblk, proj, kblk + per, bias_d, NSA_HEADS, NSA_GROUP, nh,
                 _causal_pairs(nq, nn), True, member=member, et=_block_onehot(s, SEL_BLOCK), name="nsa_selected")
    n_back = -(-WINDOW // t)
    bias_w = _bias_vecs(tab, t, n_back + 1, window=WINDOW)
    o_w = _flash(proj, 0, proj, kblk + 2 * per, proj, kblk + 3 * per, bias_w, NSA_HEADS, NSA_GROUP, nh,
                 _window_pairs(nq, n_back), False, name="nsa_window")
    return o_c, o_s, o_w


def _moba(proj, tab, b, s):
    member = _moba_gate(proj, b, s)
    t = ATT_TILE
    nn = _n_near(t)
    nh = ATT_HEADS_PER_STEP
    ng = MOBA_HEADS // nh
    bias_d = _bias_vecs(tab, t, nn + 1)
    return _flash(proj, 0, proj, ng, proj, 2 * ng, bias_d, MOBA_HEADS, 1, nh, _causal_pairs(s // t, nn),
                  True, member=member, et=_block_onehot(s, MOBA_BLOCK), name="moba_attn")


def _moe(x1, g_ffn, w_group, b_group, w_router, b_router, w_gate, w_up, w_down, g_final):
    t, d = x1.shape
    ng, _, epg = w_router.shape
    w_gr = jnp.concatenate([w_group, jnp.transpose(w_router, (1, 0, 2)).reshape(d, ng * epg),
                            jnp.zeros((d, LANES - ng - ng * epg), F32)], axis=1)
    b_gr = jnp.concatenate([b_group, b_router.reshape(-1), jnp.zeros((LANES - ng - ng * epg,), F32)]).reshape(1, LANES)
    h, info, cnt = _route(x1, g_ffn, w_gr, b_gr)
    n_e = ng * epg
    n_assign = t * EXPERT_TOPK
    assert n_assign % MOE_ROWS == 0
    n_blocks = n_assign // MOE_ROWS
    counts = cnt[0, :n_e].astype(jnp.int32)
    end = jnp.cumsum(counts)
    start = end - counts
    expert = info[:, 0:EXPERT_TOPK].astype(jnp.int32)
    rank = info[:, 4:4 + EXPERT_TOPK].astype(jnp.int32)
    e_ids = jnp.arange(n_e, dtype=jnp.int32)
    dest = (jnp.sum(jnp.where(expert[..., None] == e_ids, start, 0), axis=-1) + rank).reshape(-1)
    owns = counts > 0
    seq_of_expert = jnp.cumsum(owns.astype(jnp.int32)) - 1
    n_seq = jnp.sum(owns.astype(jnp.int32)).reshape(1)
    expert_of_seq = jnp.sum(jnp.where(owns[None, :] & (seq_of_expert[None, :] == e_ids[:, None]), e_ids[None, :], 0),
                            axis=1)
    first = start // MOE_ROWS
    last = jnp.where(owns, (end - 1) // MOE_ROWS, first - 1)
    pair_end = jnp.cumsum(last - first + 1)
    pair_start = pair_end - (last - first + 1)
    n_steps = n_blocks + n_e
    p_ids = jnp.arange(n_steps, dtype=jnp.int32)
    e_of_pair = jnp.minimum(jnp.sum((pair_end[None, :] <= p_ids[:, None]).astype(jnp.int32), axis=1), n_e - 1)
    pick = e_of_pair[:, None] == e_ids[None, :]
    lookup = lambda v: jnp.sum(jnp.where(pick, v[None, :], 0), axis=1)
    blk_of_pair = jnp.clip(lookup(first) + p_ids - lookup(pair_start), 0, n_blocks - 1)
    lo = jnp.clip(lookup(start) - blk_of_pair * MOE_ROWS, 0, MOE_ROWS)
    hi = jnp.clip(lookup(end) - blk_of_pair * MOE_ROWS, 0, MOE_ROWS)
    seq_of_pair = lookup(seq_of_expert)
    n_pairs = pair_end[-1:].astype(jnp.int32)
    xs = _dispatch(dest, h)
    ys = _experts(n_pairs, seq_of_pair, expert_of_seq, n_seq, blk_of_pair, lo, hi, xs, w_gate, w_up, w_down)
    return _combine(dest, x1, info, g_final, ys)


def kernel(x, rel_bias, norm_mix, w_in, cmp_pe_k, cmp_w1_k, cmp_w2_k, cmp_pe_v, cmp_w1_v, cmp_w2_v, w_up_nsa,
           w_up_moba, w_out, norm_ffn, w_group, b_group, w_router, b_router, w_exp_gate, w_exp_up, w_exp_down,
           final_norm):
    b, s, d = x.shape
    t = b * s
    depth = w_in.shape[0]
    tab_a = rel_bias[:, :NSA_HEADS]
    tab_b = rel_bias[:, NSA_HEADS:]
    a_cols = NSA_HEADS * HEAD_DIM + 6 * NSA_KV_HEADS * HEAD_DIM
    gate_cols = 3 * NSA_HEADS
    b_cols = 3 * MOBA_HEADS * HEAD_DIM
    xt = x.reshape(t, d)
    out = None
    for l in range(depth):
        wt = jnp.swapaxes(w_in[l], 0, 1)
        b_col0 = a_cols + gate_cols
        q_cols = NSA_HEADS * HEAD_DIM
        cmp_cols = 2 * NSA_KV_HEADS * HEAD_DIM
        h, slab, gate_a = _proj_head(xt, norm_mix[l], wt, q_cols, cmp_cols, a_cols, CMP_STRIDE)
        slab = slab.reshape(b, s // CMP_STRIDE, CMP_STRIDE * cmp_cols)
        proj_a = _matmul(h, wt, 0, a_cols - cmp_cols, BF16, scaled_cols=q_cols, col_scale=Q_SCALE,
                         w_transposed=True, skip_at=q_cols, skip_cols=cmp_cols,
                         name="in_proj_a").reshape(b, s, a_cols - cmp_cols)
        proj_b = _matmul(h, wt, b_col0, b_cols, BF16, scaled_cols=MOBA_HEADS * HEAD_DIM, col_scale=Q_SCALE,
                         w_transposed=True, name="in_proj_b").reshape(b, s, b_cols)
        gm = _matmul(h, wt, b_col0 + b_cols, 2 * d, BF16, w_transposed=True, name="in_proj_gm")
        o_c, o_s, o_w = _nsa(proj_a, slab, cmp_pe_k[l], cmp_w1_k[l], cmp_w2_k[l],
                             cmp_pe_v[l], cmp_w1_v[l], cmp_w2_v[l], tab_a, b, s)
        o_b = _moba(proj_b, tab_b, b, s)
        merged = _merge(o_c.reshape(t, -1), o_s.reshape(t, -1), o_w.reshape(t, -1), gate_a,
                        o_b.reshape(t, -1), gm, w_up_nsa[l], w_up_moba[l])
        x1 = _matmul(merged, w_out[l], 0, d, F32, res=xt, name="out_proj")
        assert l == depth - 1, "only the last layer's MoE is fused with the final norm"
        out = _moe(x1, norm_ffn[l], w_group[l], b_group[l], w_router[l], b_router[l],
                   w_exp_gate[l], w_exp_up[l], w_exp_down[l], final_norm)
    return out.reshape(b, s, d)
```

```python
import functools
import math

import numpy as np
import jax
import jax.numpy as jnp
from jax import lax
from jax.experimental import pallas as pl
from jax.experimental.pallas import tpu as pltpu

F32 = jnp.float32
BF16 = jnp.bfloat16

HEAD_DIM = 128
NSA_HEADS = 8
NSA_KV_HEADS = 2
NSA_GROUP = NSA_HEADS // NSA_KV_HEADS
CMP_BLOCK = 32
CMP_STRIDE = 16
SEL_BLOCK = 64
SEL_TOPN = 16
WINDOW = 512
FORCED_SCORE = 1e4
MOBA_HEADS = 8
MOBA_BLOCK = 256
MOBA_TOPK = 3
REL_BUCKETS = 32
REL_MAX_DIST = 128
N_GROUPS = 8
EXPERTS_PER_GROUP = 8
N_EXPERTS = N_GROUPS * EXPERTS_PER_GROUP
EXPERT_TOPK = 2
RMS_EPS = 1e-6

LANES = 128
ATT_TILE = 512
ATT_HEADS_PER_STEP = 8
ATT_VMEM_LIMIT = 56 * 1024 * 1024
MOE_ROWS = 256
MASK_BIG = 1e30
M_INIT = -3e38
LOG2E = math.log2(math.e)
Q_SCALE = HEAD_DIM ** -0.5 * LOG2E
VMEM_LIMIT = 48 * 1024 * 1024


def _cparams(sem, vmem=VMEM_LIMIT):
    return pltpu.CompilerParams(dimension_semantics=sem, vmem_limit_bytes=vmem)


def _mm_body(*refs, has_res, n_scaled, col_scale, w_transposed):
    if has_res:
        a_ref, w_ref, r_ref, o_ref = refs
    else:
        a_ref, w_ref, o_ref = refs
    w = w_ref[...].astype(BF16)
    if w_transposed:
        acc = lax.dot_general(a_ref[...], w, (((1,), (1,)), ((), ())), preferred_element_type=F32)
    else:
        acc = jnp.dot(a_ref[...], w, preferred_element_type=F32)
    if n_scaled:
        acc = acc * jnp.where(pl.program_id(1) < n_scaled, col_scale, 1.0)
    if has_res:
        acc = acc + r_ref[...]
    o_ref[...] = acc.astype(o_ref.dtype)


def _matmul(a, w, col0, ncols, out_dtype, res=None, scaled_cols=0, col_scale=1.0, w_transposed=False,
            skip_at=0, skip_cols=0, tm=2048, tn=512, name="matmul"):
    t, k = a.shape
    tn = min(tn, ncols)
    tm = min(tm, t)
    assert ncols % tn == 0 and t % tm == 0 and scaled_cols % tn == 0 and skip_at % tn == 0
    if w_transposed:
        assert col0 % 8 == 0 and skip_cols % 8 == 0
        gap_block = skip_at // tn if skip_cols else ncols // tn
        w_spec = pl.BlockSpec(
            (pl.Element(tn), pl.Element(k)),
            lambda i, j: (pl.multiple_of(col0 + j * tn + jnp.where(j >= gap_block, skip_cols, 0), 8), 0))
    else:
        assert col0 % tn == 0
        off = col0 // tn
        w_spec = pl.BlockSpec((k, tn), lambda i, j: (0, j + off))
    in_specs = [pl.BlockSpec((tm, k), lambda i, j: (i, 0)), w_spec]
    args = [a, w]
    if res is not None:
        in_specs.append(pl.BlockSpec((tm, tn), lambda i, j: (i, j)))
        args.append(res)
    return pl.pallas_call(
        functools.partial(_mm_body, has_res=res is not None, n_scaled=scaled_cols // tn, col_scale=col_scale,
                          w_transposed=w_transposed),
        grid=(t // tm, ncols // tn),
        in_specs=in_specs,
        out_specs=pl.BlockSpec((tm, tn), lambda i, j: (i, j)),
        out_shape=jax.ShapeDtypeStruct((t, ncols), out_dtype),
        compiler_params=_cparams(("parallel", "parallel")),
        name=name,
    )(*args)


def _proj_head_body(x_ref, g_ref, ws_ref, wg_ref, h_ref, slab_ref, gate_ref, acc_ref, *, per):
    x = x_ref[...]
    ms = jnp.mean(x * x, axis=-1, keepdims=True)
    h = (x * lax.rsqrt(ms + RMS_EPS) * g_ref[...]).astype(BF16)
    h_ref[...] = h
    nt = (((1,), (1,)), ((), ()))
    gate_ref[...] = lax.dot_general(h, wg_ref[...].astype(BF16), nt, preferred_element_type=F32)
    acc = lax.dot_general(h, ws_ref[...].astype(BF16), nt, preferred_element_type=F32)
    n_chunks, tm, _ = acc_ref.shape
    tn = n_chunks * LANES
    for c in range(n_chunks):
        acc_ref[c] = acc[:, c * LANES:(c + 1) * LANES]
    for r in range(per):
        for c in range(n_chunks):
            slab_ref[:, r * tn + c * LANES:r * tn + (c + 1) * LANES] = (
                acc_ref[c, pl.ds(r, tm // per, stride=per), :].astype(slab_ref.dtype))


def _proj_head(x, g, wt, slab_col0, slab_cols, gate_col0, per, tm=1024):
    t, k = x.shape
    tm = min(tm, t)
    assert t % tm == 0 and tm % (per * 8) == 0 and slab_col0 % 8 == 0 and gate_col0 % 8 == 0
    return pl.pallas_call(
        functools.partial(_proj_head_body, per=per),
        grid=(t // tm,),
        in_specs=[pl.BlockSpec((tm, k), lambda i: (i, 0)),
                  pl.BlockSpec((1, k), lambda i: (0, 0)),
                  pl.BlockSpec((pl.Element(slab_cols), pl.Element(k)), lambda i: (slab_col0, 0)),
                  pl.BlockSpec((pl.Element(LANES), pl.Element(k)), lambda i: (gate_col0, 0))],
        out_specs=[pl.BlockSpec((tm, k), lambda i: (i, 0)),
                   pl.BlockSpec((tm // per, per * slab_cols), lambda i: (i, 0)),
                   pl.BlockSpec((tm, LANES), lambda i: (i, 0))],
        out_shape=[jax.ShapeDtypeStruct((t, k), BF16),
                   jax.ShapeDtypeStruct((t // per, per * slab_cols), BF16),
                   jax.ShapeDtypeStruct((t, LANES), F32)],
        scratch_shapes=[pltpu.VMEM((slab_cols // LANES, tm, LANES), F32)],
        compiler_params=_cparams(("parallel",)),
        name="in_proj_head",
    )(x, g.reshape(1, k), wt, wt)


def _compress_body(u_ref, pek_ref, w1k_ref, w2k_ref, pev_ref, w1v_ref, w2v_ref, o_ref, *, nc, g):
    dh = HEAD_DIM
    tok_w = 2 * g * dh
    for kv, (pe_ref, w1_ref, w2_ref) in enumerate(((pek_ref, w1k_ref, w2k_ref), (pev_ref, w1v_ref, w2v_ref))):
        w1 = w1_ref[...].astype(BF16)
        w2 = w2_ref[...].astype(BF16)
        half = CMP_STRIDE * dh
        peb = jnp.dot(pe_ref[...].astype(BF16), w1, preferred_element_type=F32)[0:1]
        for gi in range(g):
            a = b = None
            for r in range(CMP_STRIDE):
                c0 = r * tok_w + (kv * g + gi) * dh
                piece = u_ref[:, c0:c0 + dh]
                da = jnp.dot(piece, w1[r * dh:(r + 1) * dh], preferred_element_type=F32)
                db = jnp.dot(piece, w1[half + r * dh:half + (r + 1) * dh], preferred_element_type=F32)
                a = da if a is None else a + da
                b = db if b is None else b + db
            pre = a + pltpu.roll(b, nc - 1, 0) + peb
            hid = jax.nn.gelu(pre)
            o_ref[kv * g + gi] = jnp.dot(hid.astype(BF16), w2, preferred_element_type=F32).astype(o_ref.dtype)


def _compress(u, pe_k, w1_k, w2_k, pe_v, w1_v, w2_v, g):
    b, nc, kk = u.shape
    hid = w1_k.shape[1]
    dh = w2_k.shape[1]
    pe16 = lambda pe: jnp.broadcast_to(pe.reshape(1, -1), (16, pe.size))
    full = lambda shape: pl.BlockSpec(shape, lambda i: (0,) * len(shape))
    return pl.pallas_call(
        functools.partial(_compress_body, nc=nc, g=g),
        grid=(b,),
        in_specs=[pl.BlockSpec((None, nc, kk), lambda i: (i, 0, 0)),
                  full((16, CMP_BLOCK * dh)), full((CMP_BLOCK * dh, hid)), full((hid, dh)),
                  full((16, CMP_BLOCK * dh)), full((CMP_BLOCK * dh, hid)), full((hid, dh))],
        out_specs=pl.BlockSpec((None, 2 * g, nc, dh), lambda i: (i, 0, 0, 0)),
        out_shape=jax.ShapeDtypeStruct((b, 2 * g, nc, dh), BF16),
        compiler_params=_cparams(("parallel",)),
        name="nsa_compress",
    )(u, pe16(pe_k), w1_k, w2_k, pe16(pe_v), w1_v, w2_v)


def _split3(x):
    p1 = x.astype(BF16)
    r = x - p1.astype(F32)
    p2 = r.astype(BF16)
    p3 = (r - p2.astype(F32)).astype(BF16)
    return p1, p2, p3


def _rank_count(score, n_rows):
    groups = []
    for g0 in range(0, n_rows, 8):
        sg = score[g0:min(g0 + 8, n_rows), :]
        n_iota = g0 + lax.broadcasted_iota(jnp.int32, sg.shape, 0)
        cnt = jnp.zeros(sg.shape, F32)
        for m in range(n_rows):
            row = score[m:m + 1, :]
            if m < g0:
                beats = row >= sg
            elif m >= g0 + 8:
                beats = row > sg
            else:
                tie = jnp.where(n_iota > m, 1.0, 0.0)
                beats = jnp.where(row > sg, 1.0, jnp.where(row == sg, tie, 0.0)) > 0.5
            cnt = cnt + jnp.where(beats, 1.0, 0.0)
        groups.append(cnt)
    return jnp.concatenate(groups, axis=0) if len(groups) > 1 else groups[0]


def _nsa_cmp_body(q_ref, kc_ref, vc_ref, bias_ref, ov_ref, oc_ref, mem_ref, *, tq, nc, n_sel):
    t0 = pl.program_id(2) * tq
    kc = kc_ref[...]
    vc = vc_ref[...]
    t_idx = t0 + lax.broadcasted_iota(jnp.int32, (tq, nc), 0)
    c_idx = lax.broadcasted_iota(jnp.int32, (tq, nc), 1)
    dist = t_idx - (c_idx * CMP_STRIDE + (CMP_BLOCK - 1))
    n_k = REL_MAX_DIST // CMP_STRIDE
    kidx = jnp.where(dist < 0, n_k + 1, jnp.minimum(lax.shift_right_logical(dist, 4), n_k))
    assert CMP_STRIDE == 16
    psum = jnp.zeros((tq, nc), F32)
    for j in range(NSA_GROUP):
        hs = slice(j * HEAD_DIM, (j + 1) * HEAD_DIM)
        gt = bias_ref[j] * LOG2E
        bias = jnp.concatenate([jnp.take_along_axis(gt, kidx[:, c0:c0 + LANES], axis=1)
                                for c0 in range(0, nc, LANES)], axis=1)
        s = lax.dot_general(q_ref[:, hs], kc, (((1,), (1,)), ((), ())), preferred_element_type=F32) + bias
        m = jnp.max(s, axis=-1, keepdims=True)
        m = jnp.where(m > -0.5 * MASK_BIG, m, 0.0)
        p = jnp.exp2(s - m)
        d = jnp.sum(p, axis=-1, keepdims=True)
        p = p / jnp.where(d > 0, d, 1.0)
        oc_ref[:, hs] = jnp.dot(p.astype(BF16), vc, preferred_element_type=F32).astype(oc_ref.dtype)
        psum = psum + p
    ov = ov_ref[...]
    nt = (((1,), (1,)), ((), ()))
    p1, p2, p3 = _split3(psum)
    psel = (lax.dot_general(ov, p1, nt, preferred_element_type=F32)
            + lax.dot_general(ov, p2, nt, preferred_element_type=F32)
            + lax.dot_general(ov, p3, nt, preferred_element_type=F32))
    n_idx = lax.broadcasted_iota(jnp.int32, (n_sel, tq), 0)
    tt = t0 + lax.broadcasted_iota(jnp.int32, (n_sel, tq), 1)
    cur = tt // SEL_BLOCK
    forced = jnp.where(n_idx == 0, 1.0, jnp.where(n_idx == cur, 1.0, jnp.where(n_idx == cur - 1, 1.0, 0.0)))
    score = jnp.where(forced > 0.5, FORCED_SCORE, jnp.where(n_idx * SEL_BLOCK <= tt, psel, -1.0))
    cnt = _rank_count(score, n_sel)
    member = jnp.where(cnt < float(min(SEL_TOPN, n_sel)), 1.0, 0.0)
    if n_sel < LANES:
        member = jnp.concatenate([member, jnp.zeros((LANES - n_sel, tq), F32)], axis=0)
    mem_ref[...] = member.T.astype(mem_ref.dtype)


def _nsa_cmp(proj, kvc, tab, overlap, b, s, tq=256):
    g = NSA_KV_HEADS
    nc = kvc.shape[2]
    n_sel = s // SEL_BLOCK
    gw = NSA_GROUP * HEAD_DIM
    assert tq % CMP_STRIDE == 0 and REL_MAX_DIST % CMP_STRIDE == 0
    n_k = REL_MAX_DIST // CMP_STRIDE
    rho = (np.arange(tq)[:, None] - (CMP_BLOCK - 1)) % CMP_STRIDE
    dd = np.concatenate([rho + CMP_STRIDE * np.arange(n_k)[None, :], np.full((tq, 1), REL_MAX_DIST)], axis=1)
    gtab = jnp.transpose(_bias_of_dist(tab, dd), (2, 0, 1)).astype(F32)
    bias_c = jnp.concatenate([gtab, jnp.full(gtab.shape[:2] + (1,), -MASK_BIG, F32),
                              jnp.zeros(gtab.shape[:2] + (LANES - n_k - 2,), F32)], axis=2)
    body = functools.partial(_nsa_cmp_body, tq=tq, nc=nc, n_sel=n_sel)
    return pl.pallas_call(
        body,
        grid=(b, g, s // tq),
        in_specs=[pl.BlockSpec((None, tq, gw), lambda bi, gi, i: (bi, i, gi)),
                  pl.BlockSpec((None, None, nc, HEAD_DIM), lambda bi, gi, i: (bi, gi, 0, 0)),
                  pl.BlockSpec((None, None, nc, HEAD_DIM), lambda bi, gi, i: (bi, g + gi, 0, 0)),
                  pl.BlockSpec((NSA_GROUP, tq, LANES), lambda bi, gi, i: (gi, 0, 0)),
                  pl.BlockSpec((n_sel, nc), lambda bi, gi, i: (0, 0))],
        out_specs=[pl.BlockSpec((None, tq, gw), lambda bi, gi, i: (bi, i, gi)),
                   pl.BlockSpec((None, None, tq, LANES), lambda bi, gi, i: (bi, gi, i, 0))],
        out_shape=[jax.ShapeDtypeStruct((b, s, NSA_HEADS * HEAD_DIM), BF16),
                   jax.ShapeDtypeStruct((b, g, s, LANES), BF16)],
        compiler_params=_cparams(("parallel", "parallel", "parallel")),
        name="nsa_cmp_select",
    )(proj, kvc, kvc, bias_c, overlap)


def _moba_gate_body(q_ref, k_ref, mem_ref, *, s, nblk):
    k = k_ref[...].astype(F32)
    kmean = jnp.mean(k.reshape(nblk, MOBA_BLOCK, HEAD_DIM), axis=1)
    k1 = kmean.astype(BF16)
    k2 = (kmean - k1.astype(F32)).astype(BF16)
    q = q_ref[...]
    nt = (((1,), (1,)), ((), ()))
    gate = (lax.dot_general(k1, q, nt, preferred_element_type=F32)
            + lax.dot_general(k2, q, nt, preferred_element_type=F32))
    n_idx = lax.broadcasted_iota(jnp.int32, (nblk, s), 0)
    own = lax.broadcasted_iota(jnp.int32, (nblk, s), 1) // MOBA_BLOCK
    past = n_idx < own
    score = jnp.where(past, gate, -MASK_BIG)
    cnt = _rank_count(score, nblk)
    n_top = max(1, min(MOBA_TOPK, nblk - 1))
    sel = jnp.where(past, jnp.where(cnt < float(n_top), 1.0, 0.0), 0.0)
    member = jnp.where(n_idx == own, 1.0, sel)
    member = jnp.concatenate([member, jnp.zeros((LANES - nblk, s), F32)], axis=0)
    mem_ref[...] = member.T.astype(mem_ref.dtype)


def _moba_gate(proj, b, s):
    h = MOBA_HEADS
    nblk = s // MOBA_BLOCK
    return pl.pallas_call(
        functools.partial(_moba_gate_body, s=s, nblk=nblk),
        grid=(b, h),
        in_specs=[pl.BlockSpec((None, s, HEAD_DIM), lambda bi, hi: (bi, 0, hi)),
                  pl.BlockSpec((None, s, HEAD_DIM), lambda bi, hi: (bi, 0, h + hi))],
        out_specs=pl.BlockSpec((None, None, s, LANES), lambda bi, hi: (bi, hi, 0, 0)),
        out_shape=jax.ShapeDtypeStruct((b, h, s, LANES), BF16),
        compiler_params=_cparams(("parallel", "parallel")),
        name="moba_gate",
    )(proj, proj)


def _flash_body(qi_ref, ki_ref, bo_ref, fl_ref, *refs, nh, ratio, nm, n_near, has_far):
    if nm:
        (q_ref, k_ref, v_ref, bvec_ref, mem_ref, et_ref, o_ref,
         m_ref, l_ref, acc_ref, sh_ref, al_ref, bias_ref, s_ref, p_ref) = refs
    else:
        q_ref, k_ref, v_ref, bvec_ref, o_ref, m_ref, l_ref, acc_ref, sh_ref, al_ref, bias_ref, s_ref, p_ref = refs
    del qi_ref, ki_ref
    p = pl.program_id(2)
    flag = fl_ref[p]
    bo = bo_ref[p]
    t = q_ref.shape[0]
    rows = 64

    @pl.when(p == 0)
    def _():
        for h in range(nh):
            for o in range(n_near):
                vec = bvec_ref[h, o][0:1, :] * LOG2E
                for rc in range(t // rows):
                    x = pltpu.roll(jnp.broadcast_to(vec, (rows, 2 * t)), rc * rows, 1, stride=1, stride_axis=0)
                    bias_ref[h, o, rc * rows:(rc + 1) * rows, :] = x[:, :t]

    @pl.when((flag & 1) != 0)
    def _():
        m_ref[...] = jnp.full(m_ref.shape, M_INIT, F32)
        l_ref[...] = jnp.zeros(l_ref.shape, F32)
        acc_ref[...] = jnp.zeros(acc_ref.shape, F32)

    nt = (((1,), (1,)), ((), ()))
    reps = t // LANES

    def step(near):
        def pass1(h):
            hs = slice(h * HEAD_DIM, (h + 1) * HEAD_DIM)
            kv = h // ratio
            ks = slice(kv * HEAD_DIM, (kv + 1) * HEAD_DIM)
            q = q_ref[:, hs]
            k = k_ref[:, ks]
            if nm:
                mneg = mem_ref[h // (nh // nm)] - 1.0
                q = jnp.concatenate([q, mneg.astype(BF16)], axis=1)
                k = jnp.concatenate([k, et_ref[...]], axis=1)
            sc = lax.dot_general(q, k, nt, preferred_element_type=F32)
            m_prev = m_ref[h]
            if near:
                sc = sc + bias_ref[h, bo]
                m_new = jnp.maximum(m_prev, jnp.max(sc, axis=-1, keepdims=True))
                sh_ref[h] = m_new
            else:
                cfar = bvec_ref[h, n_near][0:1, 0:LANES] * LOG2E
                m_new = jnp.maximum(m_prev, jnp.max(sc, axis=-1, keepdims=True) + cfar)
                sh_ref[h] = m_new - cfar
            s_ref[h] = sc
            al_ref[h] = jnp.exp2(m_prev - m_new)
            m_ref[h] = m_new

        def pass2(h):
            ks = slice(h // ratio * HEAD_DIM, (h // ratio + 1) * HEAD_DIM)
            for rc in range(t // rows):
                rs = slice(rc * rows, (rc + 1) * rows)
                pm = jnp.exp2(s_ref[h, rs, :] - jnp.tile(sh_ref[h, rs, :], (1, reps)))
                l_ref[h, rs, :] = al_ref[h, rs, :] * l_ref[h, rs, :] + jnp.sum(pm, axis=-1, keepdims=True)
                p_ref[h, rs, :] = pm.astype(BF16)
            acc_ref[h] = al_ref[h] * acc_ref[h] + jnp.dot(p_ref[h], v_ref[:, ks], preferred_element_type=F32)

        for h in range(nh):
            pass1(h)
        for h in range(nh):
            pass2(h)

    if has_far:
        pl.when(bo < n_near)(lambda: step(True))
        pl.when(bo >= n_near)(lambda: step(False))
    else:
        step(True)

    @pl.when((flag & 2) != 0)
    def _():
        for h in range(nh):
            l = l_ref[h]
            o_ref[:, h * HEAD_DIM:(h + 1) * HEAD_DIM] = (acc_ref[h] / jnp.where(l > 0, l, 1.0)).astype(o_ref.dtype)


def _flash(q_arr, q_off, k_arr, k_off, v_arr, v_off, bias, n_heads, ratio, nh, pairs, has_far, member=None,
           et=None, name="flash"):
    b, s, _ = q_arr.shape
    t = ATT_TILE
    nkv = nh // ratio
    ng = n_heads // nh
    qi = jnp.asarray([p[0] for p in pairs], jnp.int32)
    ki = jnp.asarray([p[1] for p in pairs], jnp.int32)
    bo = jnp.asarray([p[2] for p in pairs], jnp.int32)
    fl = jnp.asarray([p[3] for p in pairs], jnp.int32)
    nb = bias.shape[1]
    nm = 0
    in_specs = [
        pl.BlockSpec((None, t, nh * HEAD_DIM), lambda bi, gi, p, qi, ki, bo, fl: (bi, qi[p], q_off + gi)),
        pl.BlockSpec((None, t, nkv * HEAD_DIM), lambda bi, gi, p, qi, ki, bo, fl: (bi, ki[p], k_off + gi)),
        pl.BlockSpec((None, t, nkv * HEAD_DIM), lambda bi, gi, p, qi, ki, bo, fl: (bi, ki[p], v_off + gi)),
        pl.BlockSpec((nh, nb, 8, 2 * t), lambda bi, gi, p, qi, ki, bo, fl: (gi, 0, 0, 0)),
    ]
    args = [q_arr, k_arr, v_arr, bias]
    if member is not None:
        nm = member.shape[1] // ng
        in_specs += [
            pl.BlockSpec((None, nm, t, LANES), lambda bi, gi, p, qi, ki, bo, fl: (bi, gi, qi[p], 0)),
            pl.BlockSpec((t, LANES), lambda bi, gi, p, qi, ki, bo, fl: (ki[p], 0)),
        ]
        args += [member, et]
    n_near = nb - 1 if has_far else nb
    body = functools.partial(_flash_body, nh=nh, ratio=ratio, nm=nm, n_near=n_near, has_far=has_far)
    return pl.pallas_call(
        body,
        grid_spec=pltpu.PrefetchScalarGridSpec(
            num_scalar_prefetch=4,
            grid=(b, ng, len(pairs)),
            in_specs=in_specs,
            out_specs=pl.BlockSpec((None, t, nh * HEAD_DIM), lambda bi, gi, p, qi, ki, bo, fl: (bi, qi[p], gi)),
            scratch_shapes=[pltpu.VMEM((nh, t, LANES), F32)] * 5 + [pltpu.VMEM((nh, n_near, t, t), F32),
                                                                    pltpu.VMEM((nh, t, t), F32),
                                                                    pltpu.VMEM((nh, t, t), BF16)],
        ),
        out_shape=jax.ShapeDtypeStruct((b, s, n_heads * HEAD_DIM), BF16),
        compiler_params=_cparams(("parallel", "parallel", "arbitrary"), vmem=ATT_VMEM_LIMIT),
        name=name,
    )(qi, ki, bo, fl, *args)


def _rel_bucket(dist):
    n = jnp.maximum(jnp.asarray(dist, jnp.int32), 0)
    max_exact = REL_BUCKETS // 2
    nf = jnp.maximum(n, 1).astype(jnp.float32)
    large = max_exact + (jnp.log(nf / max_exact) / math.log(REL_MAX_DIST / max_exact)
                         * (REL_BUCKETS - max_exact)).astype(jnp.int32)
    return jnp.where(n < max_exact, n, jnp.minimum(large, REL_BUCKETS - 1))


def _bias_of_dist(tab, dist):
    hit = _rel_bucket(dist)[..., None, None] == jnp.arange(REL_BUCKETS)[:, None]
    return jnp.sum(jnp.where(hit, tab, 0.0), axis=-2)


def _n_near(t):
    return -(-(REL_MAX_DIST - 1 + t) // t)


def _bias_vecs(tab, t, n_off, window=None):
    k = np.arange(2 * t)[None, :]
    dist = np.arange(n_off)[:, None] * t + np.where(k < t, -k, 2 * t - k)
    ok = dist >= 0
    if window is not None:
        ok &= dist < window
    bias = jnp.where(jnp.asarray(ok)[..., None], _bias_of_dist(tab, dist), -MASK_BIG)
    bias = jnp.transpose(bias, (2, 0, 1)).astype(F32)
    return jnp.broadcast_to(bias[:, :, None, :], (bias.shape[0], n_off, 8, 2 * t))


def _causal_pairs(nq, n_near):
    pairs = []
    for qi in range(nq):
        for ki in range(qi + 1):
            pairs.append((qi, ki, min(qi - ki, n_near), (1 if ki == 0 else 0) | (2 if ki == qi else 0)))
    return pairs


def _window_pairs(nq, n_back):
    pairs = []
    for qi in range(nq):
        lo = max(0, qi - n_back)
        for ki in range(lo, qi + 1):
            pairs.append((qi, ki, qi - ki, (1 if ki == lo else 0) | (2 if ki == qi else 0)))
    return pairs


def _block_onehot(s, blk):
    return jnp.asarray(np.where(np.arange(s)[:, None] // blk == np.arange(LANES)[None, :], MASK_BIG, 0.0), BF16)


def _merge_body(oc_ref, os_ref, ow_ref, gl_ref, ob_ref, gma_ref, gmb_ref, wa_ref, wb_ref, o_ref,
                oa_ref, wa_s, wb_s, *, tn):
    @pl.when(pl.program_id(0) == 0)
    def _():
        def cast_rows(dst, src, chunk=128):
            def body(c, carry):
                rs = pl.ds(pl.multiple_of(c * chunk, chunk), chunk)
                dst[rs, :] = src[rs, :].astype(BF16)
                return carry
            lax.fori_loop(0, dst.shape[0] // chunk, body, 0)
        cast_rows(wa_s, wa_ref)
        cast_rows(wb_s, wb_ref)

    gates = jax.nn.sigmoid(gl_ref[...])
    for h in range(NSA_HEADS):
        hs = slice(h * HEAD_DIM, (h + 1) * HEAD_DIM)
        mix = (gates[:, 3 * h:3 * h + 1] * oc_ref[:, hs].astype(F32)
               + gates[:, 3 * h + 1:3 * h + 2] * os_ref[:, hs].astype(F32)
               + gates[:, 3 * h + 2:3 * h + 3] * ow_ref[:, hs].astype(F32))
        oa_ref[:, hs] = mix.astype(BF16)

    for c0 in range(0, o_ref.shape[1], tn):
        cs = slice(c0, c0 + tn)
        ya = jnp.dot(oa_ref[...], wa_s[:, cs], preferred_element_type=F32)
        yb = jnp.dot(ob_ref[...], wb_s[:, cs], preferred_element_type=F32)
        ga = jax.nn.sigmoid(gma_ref[:, cs].astype(F32))
        gb = jax.nn.sigmoid(gmb_ref[:, cs].astype(F32))
        o_ref[:, cs] = (ga * ya + gb * yb).astype(o_ref.dtype)


def _merge(o_c, o_s, o_w, gate_logits, o_b, gm, w_up_a, w_up_b, tm=512, tn=512):
    t, ka = o_c.shape
    kb = o_b.shape[1]
    d = w_up_a.shape[1]
    tm = min(tm, t)
    row = lambda i: (i, 0)
    once = lambda shape: pl.BlockSpec(shape, lambda i: (0, 0), pipeline_mode=pl.Buffered(1))
    return pl.pallas_call(
        functools.partial(_merge_body, tn=tn),
        grid=(t // tm,),
        in_specs=[pl.BlockSpec((tm, ka), row), pl.BlockSpec((tm, ka), row), pl.BlockSpec((tm, ka), row),
                  pl.BlockSpec((tm, LANES), row), pl.BlockSpec((tm, kb), row),
                  pl.BlockSpec((tm, d), lambda i: (i, 0)),
                  pl.BlockSpec((tm, d), lambda i: (i, 1)),
                  once((ka, d)), once((kb, d))],
        out_specs=pl.BlockSpec((tm, d), row),
        out_shape=jax.ShapeDtypeStruct((t, d), BF16),
        scratch_shapes=[pltpu.VMEM((tm, ka), BF16), pltpu.VMEM((ka, d), BF16), pltpu.VMEM((kb, d), BF16)],
        compiler_params=_cparams(("arbitrary",), vmem=ATT_VMEM_LIMIT),
        name="merge_up",
    )(o_c, o_s, o_w, gate_logits, o_b, gm, gm, w_up_a, w_up_b)


def _route_body(x_ref, g_ref, w_ref, b_ref, h_ref, info_ref, cnt_ref, carry_ref, *, tm):
    @pl.when(pl.program_id(0) == 0)
    def _():
        carry_ref[...] = jnp.zeros(carry_ref.shape, F32)

    x = x_ref[...]
    ms = jnp.mean(x * x, axis=-1, keepdims=True)
    h = x * lax.rsqrt(ms + RMS_EPS) * g_ref[...]
    _store_packed(h_ref, h, tm)
    w = w_ref[...]
    h1 = h.astype(BF16)
    h2 = (h - h1.astype(F32)).astype(BF16)
    w1 = w.astype(BF16)
    w2 = (w - w1.astype(F32)).astype(BF16)
    logits = (jnp.dot(h1, w1, preferred_element_type=F32) + jnp.dot(h1, w2, preferred_element_type=F32)
              + jnp.dot(h2, w1, preferred_element_type=F32)) + b_ref[...]
    lane = lax.broadcasted_iota(jnp.int32, (tm, LANES), 1)
    lanef = lane.astype(F32)

    is_g = lane < N_GROUPS
    gl = jnp.where(is_g, logits, -MASK_BIG)
    ge = jnp.where(is_g, jnp.exp(gl - jnp.max(gl, axis=-1, keepdims=True)), 0.0)
    gp = ge / jnp.sum(ge, axis=-1, keepdims=True)
    g_val = jnp.max(gp, axis=-1, keepdims=True)
    g_idx = jnp.min(jnp.where(gp == g_val, lanef, float(LANES)), axis=-1, keepdims=True)

    lane_grp = ((lane - N_GROUPS) // EXPERTS_PER_GROUP).astype(F32)
    in_e = jnp.where(lane >= N_GROUPS, jnp.where(lane < N_GROUPS + N_EXPERTS, 1.0, 0.0), 0.0)
    is_e = jnp.where(lane_grp == g_idx, in_e, 0.0) > 0.5
    el = jnp.where(is_e, logits, -MASK_BIG)
    ee = jnp.where(is_e, jnp.exp(el - jnp.max(el, axis=-1, keepdims=True)), 0.0)
    ep = jnp.where(is_e, ee / jnp.sum(ee, axis=-1, keepdims=True), -1.0)
    v1 = jnp.max(ep, axis=-1, keepdims=True)
    l1 = jnp.min(jnp.where(ep == v1, lanef, float(LANES)), axis=-1, keepdims=True)
    ep2 = jnp.where(lanef == l1, -1.0, ep)
    v2 = jnp.max(ep2, axis=-1, keepdims=True)
    l2 = jnp.min(jnp.where(ep2 == v2, lanef, float(LANES)), axis=-1, keepdims=True)
    vs = v1 + v2
    wt1 = g_val * v1 / vs
    wt2 = g_val * v2 / vs
    e1 = l1 - float(N_GROUPS)
    e2 = l2 - float(N_GROUPS)

    oh = jnp.where(lanef == e1, 1.0, jnp.where(lanef == e2, 1.0, 0.0))
    r_i = lax.broadcasted_iota(jnp.int32, (tm, tm), 0)
    c_i = lax.broadcasted_iota(jnp.int32, (tm, tm), 1)
    tri = jnp.where(r_i > c_i, 1.0, 0.0).astype(BF16)
    base = jnp.dot(tri, oh.astype(BF16), preferred_element_type=F32) + carry_ref[...]
    r1 = jnp.sum(jnp.where(lanef == e1, base, 0.0), axis=-1, keepdims=True)
    r2 = jnp.sum(jnp.where(lanef == e2, base, 0.0), axis=-1, keepdims=True)
    carry_ref[...] = carry_ref[...] + jnp.sum(oh, axis=0, keepdims=True)
    cnt_ref[...] = jnp.broadcast_to(carry_ref[...], cnt_ref.shape)
    info = jnp.where(lane == 0, e1, jnp.where(lane == 1, e2, jnp.where(lane == 2, wt1, jnp.where(
        lane == 3, wt2, jnp.where(lane == 4, r1, jnp.where(lane == 5, r2, 0.0))))))
    info_ref[...] = info


def _route(x1, g, w_gr, b_gr, tm=512):
    t, d = x1.shape
    return pl.pallas_call(
        functools.partial(_route_body, tm=tm),
        grid=(t // tm,),
        in_specs=[pl.BlockSpec((tm, d), lambda i: (i, 0)),
                  pl.BlockSpec((1, d), lambda i: (0, 0)),
                  pl.BlockSpec((d, LANES), lambda i: (0, 0)),
                  pl.BlockSpec((1, LANES), lambda i: (0, 0))],
        out_specs=[pl.BlockSpec((tm * ROW_SUB, LANES), lambda i: (i, 0)),
                   pl.BlockSpec((tm, LANES), lambda i: (i, 0)),
                   pl.BlockSpec((8, LANES), lambda i: (0, 0))],
        out_shape=[jax.ShapeDtypeStruct((t * ROW_SUB, LANES), jnp.uint32),
                   jax.ShapeDtypeStruct((t, LANES), F32),
                   jax.ShapeDtypeStruct((8, LANES), F32)],
        scratch_shapes=[pltpu.VMEM((1, LANES), F32)],
        compiler_params=_cparams(("arbitrary",)),
        name="moe_route",
    )(x1, g.reshape(1, d), w_gr, b_gr)


ROW_SUB = 8
U32 = jnp.uint32


def _pack_pairs(lo, hi):
    lo_b = lax.bitcast_convert_type(lo.astype(BF16).astype(F32), U32)
    hi_b = lax.bitcast_convert_type(hi.astype(BF16).astype(F32), U32)
    return lax.shift_right_logical(lo_b, U32(16)) | (hi_b & U32(0xFFFF0000))


def _unpack_pairs(w):
    lo = lax.bitcast_convert_type(lax.shift_left(w, U32(16)), F32)
    hi = lax.bitcast_convert_type(w & U32(0xFFFF0000), F32)
    return lo, hi


def _store_packed(ref, y, n):
    half = y.shape[1] // 2
    for s in range(ROW_SUB):
        cs = slice(s * LANES, (s + 1) * LANES)
        ref[pl.ds(s, n, stride=ROW_SUB), :] = _pack_pairs(y[:, cs], y[:, half + s * LANES:half + (s + 1) * LANES])


def _load_packed(ref, n):
    los, his = [], []
    for s in range(ROW_SUB):
        lo, hi = _unpack_pairs(ref[pl.ds(s, n, stride=ROW_SUB), :])
        los.append(lo)
        his.append(hi)
    return jnp.concatenate(los + his, axis=1)


def _row_copy(src_ref, src_row, dst_ref, dst_row, sem):
    return pltpu.make_async_copy(src_ref.at[pl.ds(pl.multiple_of(src_row * ROW_SUB, ROW_SUB), ROW_SUB)],
                                 dst_ref.at[pl.ds(pl.multiple_of(dst_row * ROW_SUB, ROW_SUB), ROW_SUB)], sem)


def _dispatch_body(dest_ref, h_ref, xs_ref, sem, *, tm):
    base = pl.program_id(0) * tm

    def issue(r, c):
        for k in range(EXPERT_TOPK):
            _row_copy(h_ref, r, xs_ref, dest_ref[EXPERT_TOPK * (base + r) + k], sem).start(priority=k % 2)
        return c

    lax.fori_loop(0, tm, issue, 0, unroll=4)

    def drain(r, c):
        for k in range(EXPERT_TOPK):
            _row_copy(h_ref, r, xs_ref, dest_ref[EXPERT_TOPK * (base + r) + k], sem).wait()
        return c

    lax.fori_loop(0, tm, drain, 0, unroll=4)


def _dispatch(dest, hp, tm=512):
    t = hp.shape[0] // ROW_SUB
    n_rows = dest.shape[0]
    return pl.pallas_call(
        functools.partial(_dispatch_body, tm=tm),
        grid_spec=pltpu.PrefetchScalarGridSpec(
            num_scalar_prefetch=1,
            grid=(t // tm,),
            in_specs=[pl.BlockSpec((tm * ROW_SUB, LANES), lambda i, dest: (i, 0))],
            out_specs=pl.BlockSpec(memory_space=pl.ANY),
            scratch_shapes=[pltpu.SemaphoreType.DMA(())],
        ),
        out_shape=jax.ShapeDtypeStruct((n_rows * ROW_SUB, LANES), U32),
        compiler_params=_cparams(("arbitrary",)),
        name="moe_dispatch",
    )(dest, hp)


def _expert_body(nu_ref, sq_ref, es_ref, ns_ref, blk_ref, lo_ref, hi_ref, x_ref, wg_hbm, wu_hbm, wd_hbm, y_ref,
                 wg_b, wu_b, wd_b, wg_s, wu_s, wd_s, y_acc, sem):
    i = pl.program_id(0)
    nu = nu_ref[0]
    ns = ns_ref[0]

    def weight_copies(seq, slot):
        e = es_ref[seq]
        return (pltpu.make_async_copy(wg_hbm.at[e], wg_b.at[slot], sem.at[slot]),
                pltpu.make_async_copy(wu_hbm.at[e], wu_b.at[slot], sem.at[slot]),
                pltpu.make_async_copy(wd_hbm.at[e], wd_b.at[slot], sem.at[slot]))

    def start_weights(seq, slot):
        for c in weight_copies(seq, slot):
            c.start()

    @pl.when(i == 0)
    def _():
        y_acc[...] = jnp.zeros(y_acc.shape, F32)
        start_weights(0, 0)

        @pl.when(ns > 1)
        def _():
            start_weights(1, 1)

    s = sq_ref[i]
    first = (i == 0) | (s != sq_ref[jnp.maximum(i - 1, 0)])

    @pl.when((i < nu) & first)
    def _():
        slot = lax.rem(s, 2)
        for c in weight_copies(s, slot):
            c.wait()
        def cast_rows(dst, src, chunk):
            def body(c, carry):
                rs = pl.ds(pl.multiple_of(c * chunk, chunk), chunk)
                dst[rs, :] = src[slot, rs, :].astype(BF16)
                return carry
            lax.fori_loop(0, dst.shape[0] // chunk, body, 0)

        cast_rows(wg_s, wg_b, 256)
        cast_rows(wu_s, wu_b, 256)
        cast_rows(wd_s, wd_b, 64)

        @pl.when(s + 2 < ns)
        def _():
            start_weights(s + 2, slot)

    @pl.when(i < nu)
    def _():
        x = _load_packed(x_ref, MOE_ROWS).astype(BF16)
        g = jnp.dot(x, wg_s[...], preferred_element_type=F32)
        u = jnp.dot(x, wu_s[...], preferred_element_type=F32)
        mid = (jax.nn.silu(g) * u).astype(BF16)
        y = jnp.dot(mid, wd_s[...], preferred_element_type=F32)
        row = lax.broadcasted_iota(jnp.int32, (MOE_ROWS, 1), 0)
        mine = (row >= lo_ref[i]) & (row < hi_ref[i])
        new_block = (i == 0) | (blk_ref[i] != blk_ref[jnp.maximum(i - 1, 0)])
        y_acc[...] = jnp.where(mine, y, jnp.where(new_block, 0.0, y_acc[...]))
        _store_packed(y_ref, y_acc[...], MOE_ROWS)


def _experts(n_pairs, seq_of_pair, expert_of_seq, n_seq, blk_of_pair, lo, hi, xs, w_gate, w_up, w_down):
    n_steps = seq_of_pair.shape[0]
    _, d, ff = w_gate.shape
    assert d == 2 * ROW_SUB * LANES, "a packed row must be exactly one (8,128) tile"
    blk = lambda i, nu, sq, es, ns, bk, lo, hi: (bk[jnp.minimum(i, nu[0] - 1)], 0)
    hbm = pl.BlockSpec(memory_space=pl.ANY)
    return pl.pallas_call(
        _expert_body,
        grid_spec=pltpu.PrefetchScalarGridSpec(
            num_scalar_prefetch=7,
            grid=(n_steps,),
            in_specs=[pl.BlockSpec((MOE_ROWS * ROW_SUB, LANES), blk), hbm, hbm, hbm],
            out_specs=pl.BlockSpec((MOE_ROWS * ROW_SUB, LANES), blk),
            scratch_shapes=[pltpu.VMEM((2, d, ff), F32), pltpu.VMEM((2, d, ff), F32), pltpu.VMEM((2, ff, d), F32),
                            pltpu.VMEM((d, ff), BF16), pltpu.VMEM((d, ff), BF16), pltpu.VMEM((ff, d), BF16),
                            pltpu.VMEM((MOE_ROWS, d), F32), pltpu.SemaphoreType.DMA((2,))],
        ),
        out_shape=jax.ShapeDtypeStruct(xs.shape, U32),
        compiler_params=_cparams(("arbitrary",)),
        name="moe_experts",
    )(n_pairs, seq_of_pair, expert_of_seq, n_seq, blk_of_pair, lo, hi, xs, w_gate, w_up, w_down)


def _combine_body(dest_ref, x_ref, info_ref, g_ref, ys_ref, o_ref, buf0, buf1, sem, *, tm):
    i = pl.program_id(0)
    slot = lax.rem(i, 2)
    bufs = (buf0, buf1)

    def gather(tile, sl, wait):
        def body(r, c):
            for k in range(EXPERT_TOPK):
                cp = _row_copy(ys_ref, dest_ref[EXPERT_TOPK * (tile * tm + r) + k], bufs[k].at[sl], r, sem.at[sl])
                if wait:
                    cp.wait()
                else:
                    cp.start(priority=k % 2)
            return c
        lax.fori_loop(0, tm, body, 0, unroll=4)

    @pl.when(i == 0)
    def _():
        gather(0, 0, False)

    @pl.when(i + 1 < pl.num_programs(0))
    def _():
        gather(i + 1, 1 - slot, False)

    gather(i, slot, True)
    info = info_ref[...]
    y = x_ref[...] + (info[:, 2:3] * _load_packed(buf0.at[slot], tm) + info[:, 3:4] * _load_packed(buf1.at[slot], tm))
    ms = jnp.mean(y * y, axis=-1, keepdims=True)
    o_ref[...] = y * lax.rsqrt(ms + RMS_EPS) * g_ref[...]


def _combine(dest, x1, info, g, ys, tm=256):
    t, d = x1.shape
    return pl.pallas_call(
        functools.partial(_combine_body, tm=tm),
        grid_spec=pltpu.PrefetchScalarGridSpec(
            num_scalar_prefetch=1,
            grid=(t // tm,),
            in_specs=[pl.BlockSpec((tm, d), lambda i, dest: (i, 0)),
                      pl.BlockSpec((tm, LANES), lambda i, dest: (i, 0)),
                      pl.BlockSpec((1, d), lambda i, dest: (0, 0)),
                      pl.BlockSpec(memory_space=pl.ANY)],
            out_specs=pl.BlockSpec((tm, d), lambda i, dest: (i, 0)),
            scratch_shapes=[pltpu.VMEM((2, tm * ROW_SUB, LANES), U32), pltpu.VMEM((2, tm * ROW_SUB, LANES), U32),
                            pltpu.SemaphoreType.DMA((2,))],
        ),
        out_shape=jax.ShapeDtypeStruct((t, d), F32),
        compiler_params=_cparams(("arbitrary",)),
        name="moe_combine",
    )(dest, x1, info, g.reshape(1, d), ys)


def _nsa(proj, slab, pe_k, w1_k, w2_k, pe_v, w1_v, w2_v, tab, b, s):
    g, dh = NSA_KV_HEADS, HEAD_DIM
    qw = NSA_HEADS * dh
    nc = s // CMP_STRIDE
    kvc = _compress(slab, pe_k, w1_k, w2_k, pe_v, w1_v, w2_v, g)

    c_start = np.arange(nc)[None, :] * CMP_STRIDE
    n_sel = s // SEL_BLOCK
    sb = np.arange(n_sel)[:, None] * SEL_BLOCK
    overlap = jnp.asarray((c_start < sb + SEL_BLOCK) & (c_start + CMP_BLOCK > sb), BF16)
    o_c, member = _nsa_cmp(proj, kvc, tab, overlap, b, s)

    t = ATT_TILE
    nq = s // t
    nn = _n_near(t)
    nh = ATT_HEADS_PER_STEP
    kblk = qw // (nh // NSA_GROUP * dh)
    per = g // (nh // NSA_GROUP)
    bias_d = _bias_vecs(tab, t, nn + 1)
    o_s = _flash(proj, 0, proj, kblk, proj, kblk + per, bias_d, NSA_HEADS, NSA_GROUP, nh,
                 _causal_pairs(nq, nn), True, member=member, et=_block_onehot(s, SEL_BLOCK), name="nsa_selected")
    n_back = -(-WINDOW // t)
    bias_w = _bias_vecs(tab, t, n_back + 1, window=WINDOW)
    o_w = _flash(proj, 0, proj, kblk + 2 * per, proj, kblk + 3 * per, bias_w, NSA_HEADS, NSA_GROUP, nh,
                 _window_pairs(nq, n_back), False, name="nsa_window")
    return o_c, o_s, o_w


def _moba(proj, tab, b, s):
    member = _moba_gate(proj, b, s)
    t = ATT_TILE
    nn = _n_near(t)
    nh = ATT_HEADS_PER_STEP
    ng = MOBA_HEADS // nh
    bias_d = _bias_vecs(tab, t, nn + 1)
    return _flash(proj, 0, proj, ng, proj, 2 * ng, bias_d, MOBA_HEADS, 1, nh, _causal_pairs(s // t, nn),
                  True, member=member, et=_block_onehot(s, MOBA_BLOCK), name="moba_attn")


def _moe(x1, g_ffn, w_group, b_group, w_router, b_router, w_gate, w_up, w_down, g_final):
    t, d = x1.shape
    ng, _, epg = w_router.shape
    w_gr = jnp.concatenate([w_group, jnp.transpose(w_router, (1, 0, 2)).reshape(d, ng * epg),
                            jnp.zeros((d, LANES - ng - ng * epg), F32)], axis=1)
    b_gr = jnp.concatenate([b_group, b_router.reshape(-1), jnp.zeros((LANES - ng - ng * epg,), F32)]).reshape(1, LANES)
    h, info, cnt = _route(x1, g_ffn, w_gr, b_gr)
    n_e = ng * epg
    n_assign = t * EXPERT_TOPK
    assert n_assign % MOE_ROWS == 0
    n_blocks = n_assign // MOE_ROWS
    counts = cnt[0, :n_e].astype(jnp.int32)
    end = jnp.cumsum(counts)
    start = end - counts
    expert = info[:, 0:EXPERT_TOPK].astype(jnp.int32)
    rank = info[:, 4:4 + EXPERT_TOPK].astype(jnp.int32)
    e_ids = jnp.arange(n_e, dtype=jnp.int32)
    dest = (jnp.sum(jnp.where(expert[..., None] == e_ids, start, 0), axis=-1) + rank).reshape(-1)
    owns = counts > 0
    seq_of_expert = jnp.cumsum(owns.astype(jnp.int32)) - 1
    n_seq = jnp.sum(owns.astype(jnp.int32)).reshape(1)
    expert_of_seq = jnp.sum(jnp.where(owns[None, :] & (seq_of_expert[None, :] == e_ids[:, None]), e_ids[None, :], 0),
                            axis=1)
    first = start // MOE_ROWS
    last = jnp.where(owns, (end - 1) // MOE_ROWS, first - 1)
    pair_end = jnp.cumsum(last - first + 1)
    pair_start = pair_end - (last - first + 1)
    n_steps = n_blocks + n_e
    p_ids = jnp.arange(n_steps, dtype=jnp.int32)
    e_of_pair = jnp.minimum(jnp.sum((pair_end[None, :] <= p_ids[:, None]).astype(jnp.int32), axis=1), n_e - 1)
    pick = e_of_pair[:, None] == e_ids[None, :]
    lookup = lambda v: jnp.sum(jnp.where(pick, v[None, :], 0), axis=1)
    blk_of_pair = jnp.clip(lookup(first) + p_ids - lookup(pair_start), 0, n_blocks - 1)
    lo = jnp.clip(lookup(start) - blk_of_pair * MOE_ROWS, 0, MOE_ROWS)
    hi = jnp.clip(lookup(end) - blk_of_pair * MOE_ROWS, 0, MOE_ROWS)
    seq_of_pair = lookup(seq_of_expert)
    n_pairs = pair_end[-1:].astype(jnp.int32)
    xs = _dispatch(dest, h)
    ys = _experts(n_pairs, seq_of_pair, expert_of_seq, n_seq, blk_of_pair, lo, hi, xs, w_gate, w_up, w_down)
    return _combine(dest, x1, info, g_final, ys)


def kernel(x, rel_bias, norm_mix, w_in, cmp_pe_k, cmp_w1_k, cmp_w2_k, cmp_pe_v, cmp_w1_v, cmp_w2_v, w_up_nsa,
           w_up_moba, w_out, norm_ffn, w_group, b_group, w_router, b_router, w_exp_gate, w_exp_up, w_exp_down,
           final_norm):
    b, s, d = x.shape
    t = b * s
    depth = w_in.shape[0]
    tab_a = rel_bias[:, :NSA_HEADS]
    tab_b = rel_bias[:, NSA_HEADS:]
    a_cols = NSA_HEADS * HEAD_DIM + 6 * NSA_KV_HEADS * HEAD_DIM
    gate_cols = 3 * NSA_HEADS
    b_cols = 3 * MOBA_HEADS * HEAD_DIM
    xt = x.reshape(t, d)
    out = None
    for l in range(depth):
        wt = jnp.swapaxes(w_in[l], 0, 1)
        b_col0 = a_cols + gate_cols
        q_cols = NSA_HEADS * HEAD_DIM
        cmp_cols = 2 * NSA_KV_HEADS * HEAD_DIM
        h, slab, gate_a = _proj_head(xt, norm_mix[l], wt, q_cols, cmp_cols, a_cols, CMP_STRIDE)
        slab = slab.reshape(b, s // CMP_STRIDE, CMP_STRIDE * cmp_cols)
        proj_a = _matmul(h, wt, 0, a_cols - cmp_cols, BF16, scaled_cols=q_cols, col_scale=Q_SCALE,
                         w_transposed=True, skip_at=q_cols, skip_cols=cmp_cols,
                         name="in_proj_a").reshape(b, s, a_cols - cmp_cols)
        proj_b = _matmul(h, wt, b_col0, b_cols, BF16, scaled_cols=MOBA_HEADS * HEAD_DIM, col_scale=Q_SCALE,
                         w_transposed=True, name="in_proj_b").reshape(b, s, b_cols)
        gm = _matmul(h, wt, b_col0 + b_cols, 2 * d, BF16, w_transposed=True, name="in_proj_gm")
        o_c, o_s, o_w = _nsa(proj_a, slab, cmp_pe_k[l], cmp_w1_k[l], cmp_w2_k[l],
                             cmp_pe_v[l], cmp_w1_v[l], cmp_w2_v[l], tab_a, b, s)
        o_b = _moba(proj_b, tab_b, b, s)
        merged = _merge(o_c.reshape(t, -1), o_s.reshape(t, -1), o_w.reshape(t, -1), gate_a,
                        o_b.reshape(t, -1), gm, w_up_nsa[l], w_up_moba[l])
        x1 = _matmul(merged, w_out[l], 0, d, F32, res=xt, name="out_proj")
        assert l == depth - 1, "only the last layer's MoE is fused with the final norm"
        out = _moe(x1, norm_ffn[l], w_group[l], b_group[l], w_router[l], b_router[l],
                   w_exp_gate[l], w_exp_up[l], w_exp_down[l], final_norm)
    return out.reshape(b, s, d)
```
